```python
import jax
import jax.numpy as jnp
from jax import lax
import numpy as np

D_MODEL = 1024
BATCH = 8
SEQ = 8192
DEPTH = 1

HEAD_DIM = 64
PLE_DIM = 256
RMS_EPS = 1e-6
DIL_HEADS = 8
DIL_PATTERNS = ((128, 1), (512, 4), (2048, 16))
DIL_BLOCK = 128
NSA_HEADS = 8
NSA_KV_HEADS = 2
CMP_BLOCK = 32
CMP_STRIDE = 16
CMP_HIDDEN = 256
SLC_BLOCK = 64
SLC_TOP_N = 16
SLC_FORCE = 1e9
WIN_SIZE = 512
WIN_BLOCK = 128
NSA_Q_CHUNK = 64
N_EXPERTS = 64
TOP_K = 8
N_GROUPS = 8
TOP_GROUPS = 4
EXPERT_HIDDEN = 256
SHARED_HIDDEN = 256
ROUTED_SCALE = 2.5
MOE_BLOCK = 256
A_WIDTH = DIL_HEADS * HEAD_DIM
B_WIDTH = NSA_HEADS * HEAD_DIM
KV_WIDTH = NSA_KV_HEADS * HEAD_DIM
IN_SIZES = (A_WIDTH, A_WIDTH, A_WIDTH, B_WIDTH, KV_WIDTH, KV_WIDTH, KV_WIDTH, KV_WIDTH, KV_WIDTH, KV_WIDTH, 3 * NSA_HEADS, D_MODEL, D_MODEL)
IN_COLS = sum(IN_SIZES)

kernel_name = "hybrid_dilated_nsa_moe_block"


def rmsnorm(x, g):
    xf = x.astype(jnp.float32)
    r = lax.rsqrt(jnp.mean(xf * xf, axis=-1, keepdims=True) + RMS_EPS)
    return (xf * r).astype(x.dtype) * g


def alibi_slopes(n):
    return 2.0 ** (-8.0 * jnp.arange(1, n + 1, dtype=jnp.float32) / n)


def banded_attention(q, k, v, slopes, window, blk, dist_scale):
    n, L, H, dh = q.shape
    G = k.shape[2]
    rep = H // G
    nblk = -(-L // blk)
    Lp = nblk * blk
    nprev = -(-window // blk)
    span = (nprev + 1) * blk
    scale = dh ** -0.5
    slope = (slopes * dist_scale).reshape(G, rep)
    qp = jnp.pad(q, ((0, 0), (0, Lp - L), (0, 0), (0, 0)))
    kp = jnp.pad(k, ((0, 0), (nprev * blk, Lp - L), (0, 0), (0, 0)))
    vp = jnp.pad(v, ((0, 0), (nprev * blk, Lp - L), (0, 0), (0, 0)))
    qb = qp.reshape(n, nblk, blk, G, rep, dh).transpose(1, 0, 2, 3, 4, 5)

    def one_block(args):
        i, qi = args
        ki = lax.dynamic_slice_in_dim(kp, i * blk, span, axis=1)
        vi = lax.dynamic_slice_in_dim(vp, i * blk, span, axis=1)
        qpos = i * blk + jnp.arange(blk)
        kpos = (i - nprev) * blk + jnp.arange(span)
        dist = (qpos[:, None] - kpos[None, :]).astype(jnp.float32)
        valid = (dist >= 0) & (dist <= window) & (kpos[None, :] >= 0)
        s = jnp.einsum('nqgrd,nkgd->ngrqk', qi, ki).astype(jnp.float32) * scale - slope[:, :, None, None] * dist
        s = jnp.where(valid, s, -jnp.inf)
        lse = jax.nn.logsumexp(s, axis=-1)
        pr = jnp.exp(s - lse[..., None]).astype(vi.dtype)
        o = jnp.einsum('ngrqk,nkgd->nqgrd', pr, vi)
        return o, lse

    o, lse = lax.map(one_block, (jnp.arange(nblk), qb))
    o = o.transpose(1, 0, 2, 3, 4, 5).reshape(n, Lp, H, dh)[:, :L]
    lse = lse.transpose(1, 0, 4, 2, 3).reshape(n, Lp, H)[:, :L]
    return o, lse


def dilated_attention(q, k, v, slopes):
    B, S, H, dh = q.shape
    outs, lses = [], []
    for window, dil in DIL_PATTERNS:
        def split(t):
            return t.reshape(B, S // dil, dil, H, dh).transpose(0, 2, 1, 3, 4).reshape(B * dil, S // dil, H, dh)
        o, lse = banded_attention(split(q), split(k), split(v), slopes, window // dil, DIL_BLOCK, dil)
        outs.append(o.reshape(B, dil, S // dil, H, dh).transpose(0, 2, 1, 3, 4).reshape(B, S, H, dh))
        lses.append(lse.reshape(B, dil, S // dil, H).transpose(0, 2, 1, 3).reshape(B, S, H))
    w = jax.nn.softmax(jnp.stack(lses, axis=0), axis=0).astype(q.dtype)
    out = w[0][..., None] * outs[0]
    for g in range(1, len(DIL_PATTERNS)):
        out = out + w[g][..., None] * outs[g]
    return out


def compress_blocks(t, pos, w1, b1, w2):
    B, S, G, dh = t.shape
    nparts = CMP_BLOCK // CMP_STRIDE
    nchunk = S // CMP_STRIDE
    ncmp = nchunk - nparts + 1
    tc = t.reshape(B, nchunk, CMP_STRIDE, G, dh)
    blocks = jnp.concatenate([tc[:, j:j + ncmp] for j in range(nparts)], axis=2)
    blocks = blocks + pos[:, None, :]
    flat = blocks.transpose(0, 1, 3, 2, 4).reshape(B, ncmp, G, CMP_BLOCK * dh)
    return jax.nn.gelu(flat @ w1 + b1) @ w2


def nsa_attention(q, kc, vc, ks, vs, kw, vw, gates, slopes):
    B, S, H, dh = q.shape
    G = kc.shape[2]
    rep = H // G
    scale = dh ** -0.5
    slope = slopes.reshape(G, rep)
    ncmp = kc.shape[1]
    nslc = S // SLC_BLOCK
    n_sel = min(SLC_TOP_N, nslc)
    C = NSA_Q_CHUNK
    nq = S // C
    cmp_end = jnp.arange(ncmp) * CMP_STRIDE + (CMP_BLOCK - 1)
    ci = jnp.arange(ncmp)[:, None]
    sj = jnp.arange(nslc)[None, :]
    overlap = ((ci * CMP_STRIDE < (sj + 1) * SLC_BLOCK) & (ci * CMP_STRIDE + CMP_BLOCK > sj * SLC_BLOCK)).astype(jnp.float32)
    ks_blk = ks.reshape(B, nslc, SLC_BLOCK, G, dh).transpose(0, 3, 1, 2, 4)
    vs_blk = vs.reshape(B, nslc, SLC_BLOCK, G, dh).transpose(0, 3, 1, 2, 4)
    b_ix = jnp.arange(B)[:, None, None, None]
    g_ix = jnp.arange(G)[None, :, None, None]
    blk_ix = jnp.arange(nslc)
    in_blk = jnp.arange(SLC_BLOCK)
    qc = q.reshape(B, nq, C, G, rep, dh).transpose(1, 0, 2, 3, 4, 5)

    def chunk(args):
        c, qi = args
        tpos = c * C + jnp.arange(C)
        dist = (tpos[:, None] - cmp_end[None, :]).astype(jnp.float32)
        s = jnp.einsum('bqgrd,bkgd->bgrqk', qi, kc).astype(jnp.float32) * scale - slope[:, :, None, None] * dist
        s = jnp.where(dist >= 0, s, -jnp.inf)
        m = jnp.max(s, axis=-1, keepdims=True)
        m = jnp.where(jnp.isfinite(m), m, 0.0)
        e = jnp.exp(s - m)
        den = jnp.sum(e, axis=-1, keepdims=True)
        pc = e / jnp.where(den > 0, den, 1.0)
        o_cmp = jnp.einsum('bgrqk,bkgd->bqgrd', pc.astype(vc.dtype), vc)
        p_slc = jnp.einsum('bgrqk,kj->bgqj', pc, overlap)
        cur = tpos // SLC_BLOCK
        forced = (blk_ix[None, :] == 0) | (blk_ix[None, :] == cur[:, None]) | (blk_ix[None, :] == cur[:, None] - 1)
        future = blk_ix[None, :] * SLC_BLOCK > tpos[:, None]
        score = jnp.where(forced, SLC_FORCE, jnp.where(future, -SLC_FORCE, p_slc))
        _, idx = lax.top_k(score, n_sel)
        kg = ks_blk[b_ix, g_ix, idx]
        vg = vs_blk[b_ix, g_ix, idx]
        kpos = idx[..., None] * SLC_BLOCK + in_blk
        dist2 = (tpos[None, None, :, None, None] - kpos).astype(jnp.float32)[:, :, None]
        s2 = jnp.einsum('bqgrd,bgqnkd->bgrqnk', qi, kg).astype(jnp.float32) * scale - slope[None, :, :, None, None, None] * dist2
        s2 = jnp.where(dist2 >= 0, s2, -jnp.inf)
        p2 = jax.nn.softmax(s2.reshape(B, G, rep, C, n_sel * SLC_BLOCK), axis=-1).reshape(s2.shape)
        o_slc = jnp.einsum('bgrqnk,bgqnkd->bqgrd', p2.astype(vg.dtype), vg)
        return o_cmp, o_slc

    o_cmp, o_slc = lax.map(chunk, (jnp.arange(nq), qc))
    o_cmp = o_cmp.transpose(1, 0, 2, 3, 4, 5).reshape(B, S, H, dh)
    o_slc = o_slc.transpose(1, 0, 2, 3, 4, 5).reshape(B, S, H, dh)
    o_win, _ = banded_attention(q, kw, vw, slopes, WIN_SIZE - 1, WIN_BLOCK, 1)
    return gates[:, :, 0, :, None] * o_cmp + gates[:, :, 1, :, None] * o_slc + gates[:, :, 2, :, None] * o_win


def moe_ffn(x, w_router, router_bias, w_gate, w_up, w_down, ws_gate, ws_up, ws_down):
    B, S, D = x.shape
    N = B * S
    xf = x.reshape(N, D)
    scores = jax.nn.sigmoid((xf @ w_router).astype(jnp.float32))
    sel = scores + router_bias.astype(jnp.float32)
    gscore = jnp.sum(lax.top_k(sel.reshape(N, N_GROUPS, N_EXPERTS // N_GROUPS), 2)[0], axis=-1)
    _, gidx = lax.top_k(gscore, TOP_GROUPS)
    gmask = jnp.any(gidx[:, :, None] == jnp.arange(N_GROUPS), axis=1)
    emask = jnp.repeat(gmask, N_EXPERTS // N_GROUPS, axis=1)
    _, eidx = lax.top_k(jnp.where(emask, sel, -jnp.inf), TOP_K)
    w = jnp.take_along_axis(scores, eidx, axis=1)
    w = w / jnp.sum(w, axis=-1, keepdims=True) * ROUTED_SCALE

    NK = N * TOP_K
    nblk = (NK + N_EXPERTS * (MOE_BLOCK - 1) + MOE_BLOCK - 1) // MOE_BLOCK
    P = nblk * MOE_BLOCK
    flat_e = eidx.reshape(NK)
    flat_tok = jnp.arange(NK, dtype=jnp.int32) // TOP_K
    flat_w = w.reshape(NK)
    order = jnp.argsort(flat_e)
    se = flat_e[order]
    counts = jnp.bincount(flat_e, length=N_EXPERTS)
    padded = (counts + MOE_BLOCK - 1) // MOE_BLOCK * MOE_BLOCK
    pend = jnp.cumsum(padded)
    pstart = pend - padded
    start = jnp.cumsum(counts) - counts
    dest = pstart[se] + jnp.arange(NK) - start[se]
    row_tok = jnp.full((P,), N, dtype=jnp.int32).at[dest].set(flat_tok[order])
    row_w = jnp.zeros((P,), x.dtype).at[dest].set(flat_w[order].astype(x.dtype))
    blk_e = jnp.minimum(jnp.searchsorted(pend, jnp.arange(nblk) * MOE_BLOCK, side='right'), N_EXPERTS - 1)
    x_pad = jnp.concatenate([xf, jnp.zeros((1, D), x.dtype)], axis=0)

    def step(y, args):
        tok, wt, e = args
        xb = x_pad[tok]
        hid = jax.nn.silu(xb @ w_gate[e]) * (xb @ w_up[e])
        return y.at[tok].add((hid @ w_down[e]) * wt[:, None]), None

    y, _ = lax.scan(step, jnp.zeros((N + 1, D), x.dtype),
                    (row_tok.reshape(nblk, MOE_BLOCK), row_w.reshape(nblk, MOE_BLOCK), blk_e))
    shared = (jax.nn.silu(xf @ ws_gate) * (xf @ ws_up)) @ ws_down
    return (y[:N] + shared).reshape(B, S, D)


def setup_inputs(seed: int = 0) -> dict:
    key = jax.random.key(seed)
    ks = jax.random.split(key, 32)
    L, D, dh = DEPTH, D_MODEL, HEAD_DIM

    def nrm(k, shape, fan_in):
        return jax.random.normal(k, shape, jnp.float32) * fan_in ** -0.5

    def gain(k, shape):
        return 1.0 + 0.02 * jax.random.normal(k, shape, jnp.float32)

    def small(k, shape):
        return 0.01 * jax.random.normal(k, shape, jnp.float32)

    return {
        "x": jax.random.normal(ks[0], (BATCH, SEQ, D), jnp.float32),
        "p": jax.random.normal(ks[1], (DEPTH, BATCH, SEQ, PLE_DIM), jnp.float32),
        "g_mix": gain(ks[2], (L, D)),
        "w_in": nrm(ks[3], (L, D, IN_COLS), D),
        "cmp_pos_k": 0.02 * jax.random.normal(ks[4], (L, CMP_BLOCK, dh), jnp.float32),
        "cmp_w1_k": nrm(ks[5], (L, CMP_BLOCK * dh, CMP_HIDDEN), CMP_BLOCK * dh),
        "cmp_b1_k": small(ks[6], (L, CMP_HIDDEN)),
        "cmp_w2_k": nrm(ks[7], (L, CMP_HIDDEN, dh), CMP_HIDDEN),
        "cmp_pos_v": 0.02 * jax.random.normal(ks[8], (L, CMP_BLOCK, dh), jnp.float32),
        "cmp_w1_v": nrm(ks[9], (L, CMP_BLOCK * dh, CMP_HIDDEN), CMP_BLOCK * dh),
        "cmp_b1_v": small(ks[10], (L, CMP_HIDDEN)),
        "cmp_w2_v": nrm(ks[11], (L, CMP_HIDDEN, dh), CMP_HIDDEN),
        "w_branch_a": nrm(ks[12], (L, A_WIDTH, D), A_WIDTH),
        "w_branch_b": nrm(ks[13], (L, B_WIDTH, D), B_WIDTH),
        "w_out": nrm(ks[14], (L, D, D), D),
        "g_ffn": gain(ks[15], (L, D)),
        "w_router": nrm(ks[16], (L, D, N_EXPERTS), D),
        "router_bias": small(ks[17], (L, N_EXPERTS)),
        "w_gate": nrm(ks[18], (L, N_EXPERTS, D, EXPERT_HIDDEN), D),
        "w_up": nrm(ks[19], (L, N_EXPERTS, D, EXPERT_HIDDEN), D),
        "w_down": nrm(ks[20], (L, N_EXPERTS, EXPERT_HIDDEN, D), EXPERT_HIDDEN),
        "ws_gate": nrm(ks[21], (L, D, SHARED_HIDDEN), D),
        "ws_up": nrm(ks[22], (L, D, SHARED_HIDDEN), D),
        "ws_down": nrm(ks[23], (L, SHARED_HIDDEN, D), SHARED_HIDDEN),
        "g_ple": gain(ks[24], (L, D)),
        "w_ple_gate": nrm(ks[25], (L, D, D), D),
        "w_ple": nrm(ks[26], (L, PLE_DIM, D), PLE_DIM),
        "g_final": gain(ks[27], (D,)),
    }


def reference(x, p, g_mix, w_in, cmp_pos_k, cmp_w1_k, cmp_b1_k, cmp_w2_k, cmp_pos_v, cmp_w1_v, cmp_b1_v, cmp_w2_v, w_branch_a, w_branch_b, w_out, g_ffn, w_router, router_bias, w_gate, w_up, w_down, ws_gate, ws_up, ws_down, g_ple, w_ple_gate, w_ple, g_final):
    B, S, D = x.shape
    slopes = alibi_slopes(DIL_HEADS + NSA_HEADS)
    slopes_a = slopes[0::2]
    slopes_b = slopes[1::2]
    offsets = np.cumsum(IN_SIZES)[:-1].tolist()

    def heads(t, n):
        return t.reshape(B, S, n, HEAD_DIM)

    h = x
    for i in range(DEPTH):
        xn = rmsnorm(h, g_mix[i])
        proj = xn @ w_in[i]
        qa, ka, va, qb, kc_raw, vc_raw, k_slc, v_slc, k_win, v_win, g_nsa, g_a, g_b = jnp.split(proj, offsets, axis=-1)
        y_a = dilated_attention(heads(qa, DIL_HEADS), heads(ka, DIL_HEADS), heads(va, DIL_HEADS), slopes_a)
        kc = compress_blocks(heads(kc_raw, NSA_KV_HEADS), cmp_pos_k[i], cmp_w1_k[i], cmp_b1_k[i], cmp_w2_k[i])
        vc = compress_blocks(heads(vc_raw, NSA_KV_HEADS), cmp_pos_v[i], cmp_w1_v[i], cmp_b1_v[i], cmp_w2_v[i])
        gates = jax.nn.sigmoid(g_nsa).reshape(B, S, 3, NSA_HEADS)
        y_b = nsa_attention(heads(qb, NSA_HEADS), kc, vc, heads(k_slc, NSA_KV_HEADS), heads(v_slc, NSA_KV_HEADS),
                            heads(k_win, NSA_KV_HEADS), heads(v_win, NSA_KV_HEADS), gates, slopes_b)
        merged = (jax.nn.sigmoid(g_a) * (y_a.reshape(B, S, A_WIDTH) @ w_branch_a[i])
                  + jax.nn.sigmoid(g_b) * (y_b.reshape(B, S, B_WIDTH) @ w_branch_b[i]))
        h = h + merged @ w_out[i]
        h = h + moe_ffn(rmsnorm(h, g_ffn[i]), w_router[i], router_bias[i], w_gate[i], w_up[i], w_down[i],
                        ws_gate[i], ws_up[i], ws_down[i])
        h = h + (p[i] @ w_ple[i]) * jax.nn.sigmoid(rmsnorm(h, g_ple[i]) @ w_ple_gate[i])
    return rmsnorm(h, g_final)
```

```python
import functools

import numpy as np
import jax
import jax.numpy as jnp
from jax import lax
from jax.experimental import pallas as pl
from jax.experimental.pallas import tpu as pltpu

F32 = jnp.float32
BF16 = jnp.bfloat16

HEAD_DIM = 64
LANE = 128
RMS_EPS = 1e-6
DIL_HEADS = 8
DIL_PATTERNS = ((128, 1), (512, 4), (2048, 16))
DIL_BLOCK = 128
NSA_HEADS = 8
NSA_KV_HEADS = 2
NSA_REP = NSA_HEADS // NSA_KV_HEADS
CMP_BLOCK = 32
CMP_STRIDE = 16
CMP_HIDDEN = 256
SLC_BLOCK = 64
SLC_TOP_N = 16
SLC_FORCE = 1e9
WIN_SIZE = 512
N_EXPERTS = 64
TOP_K = 8
N_GROUPS = 8
TOP_GROUPS = 4
GROUP_SIZE = N_EXPERTS // N_GROUPS
ROUTED_SCALE = 2.5
IN_SIZES = (512, 512, 512, 512, 128, 128, 128, 128, 128, 128, 24, 1024, 1024)

NEG = -1e30
NT_DIMS = (((1,), (1,)), ((), ()))

ROW_TILE = 512
NSA_Q = 128
NSA_TK = 512
MOE_ROWS = 512
VMEM_LIMIT = 48 * 1024 * 1024


def _dot(a, b):
    return jnp.dot(a, b, preferred_element_type=F32)


def _dot_nt(a, b):
    return lax.dot_general(a, b, NT_DIMS, preferred_element_type=F32)


def _split3(x):
    hi = x.astype(BF16)
    r1 = x - hi.astype(F32)
    mid = r1.astype(BF16)
    lo = (r1 - mid.astype(F32)).astype(BF16)
    return hi, mid, lo


def _rms(x, g):
    r = lax.rsqrt(jnp.mean(x * x, axis=-1, keepdims=True) + RMS_EPS)
    return (x * r) * g


def _params(sem):
    return pltpu.CompilerParams(dimension_semantics=sem, vmem_limit_bytes=VMEM_LIMIT)


def _inproj_kernel(x_ref, g_ref, w_ref, o_ref, xn_ref):
    @pl.when(pl.program_id(1) == 0)
    def _():
        xn_ref[...] = _rms(x_ref[...], g_ref[...]).astype(BF16)

    o_ref[...] = _dot(xn_ref[...], w_ref[...]).astype(o_ref.dtype)


def _inproj(x2, g, w, out_dtype, tn):
    n, d = x2.shape
    cols = w.shape[1]
    return pl.pallas_call(
        _inproj_kernel,
        grid=(n // ROW_TILE, cols // tn),
        in_specs=[
            pl.BlockSpec((ROW_TILE, d), lambda i, j: (i, 0)),
            pl.BlockSpec((1, d), lambda i, j: (0, 0)),
            pl.BlockSpec((d, tn), lambda i, j: (0, j)),
        ],
        out_specs=pl.BlockSpec((ROW_TILE, tn), lambda i, j: (i, j)),
        out_shape=jax.ShapeDtypeStruct((n, cols), out_dtype),
        scratch_shapes=[pltpu.VMEM((ROW_TILE, d), BF16)],
        compiler_params=_params(("parallel", "arbitrary")),
    )(x2, g, w)


def _dil_kernel(slope_ref, q_ref, kp_ref, kc_ref, vp_ref, vc_ref, o_ref, l_ref, *, dil):
    i = pl.program_id(2)
    blk = DIL_BLOCK
    qi = lax.broadcasted_iota(jnp.int32, (blk, 2 * blk), 0)
    kj = lax.broadcasted_iota(jnp.int32, (blk, 2 * blk), 1)
    dist = (qi + blk - kj).astype(F32)
    valid = (dist >= 0) & (dist <= blk) & ((kj >= blk) | (i > 0))
    lane = lax.broadcasted_iota(jnp.int32, (blk, LANE), 1)
    lo = lane < HEAD_DIM
    scale = HEAD_DIM ** -0.5
    for hp in range(DIL_HEADS // 2):
        sl = slice(hp * LANE, (hp + 1) * LANE)
        q2 = q_ref[0, :, sl]
        k2 = jnp.concatenate([kp_ref[0, :, sl], kc_ref[0, :, sl]], axis=0)
        v2 = jnp.concatenate([vp_ref[0, :, sl], vc_ref[0, :, sl]], axis=0)
        outs, lses = [], []
        for hh in range(2):
            keep = lo if hh == 0 else jnp.logical_not(lo)
            qm = jnp.where(keep, q2, jnp.zeros_like(q2))
            slope = slope_ref[2 * hp + hh] * float(dil)
            s = _dot_nt(qm, k2) * scale - slope * dist
            s = jnp.where(valid, s, NEG)
            m = jnp.max(s, axis=-1, keepdims=True)
            e = jnp.exp(s - m)
            den = jnp.sum(e, axis=-1, keepdims=True)
            outs.append(_dot(e.astype(BF16), v2) / den)
            lses.append(jnp.broadcast_to(m + jnp.log(den), (blk, LANE)))
        o_ref[0, :, sl] = jnp.where(lo, outs[0], outs[1])
        l_ref[0, :, sl] = jnp.where(lo, lses[0], lses[1])


def _dilated(pa3, slopes_a, dil, b, s, width):
    sd = s // dil
    nblk = sd // DIL_BLOCK
    aw = DIL_HEADS * HEAD_DIM
    per = width // aw
    blk = (1, DIL_BLOCK, aw)

    def spec(col, prev):
        if prev:
            return pl.BlockSpec(blk, lambda bb, r, i: (bb, jnp.maximum(i - 1, 0), r * per + col))
        return pl.BlockSpec(blk, lambda bb, r, i: (bb, i, r * per + col))

    out_spec = pl.BlockSpec(blk, lambda bb, r, i: (bb, i, r))
    return pl.pallas_call(
        functools.partial(_dil_kernel, dil=dil),
        grid=(b, dil, nblk),
        in_specs=[
            pl.BlockSpec(memory_space=pltpu.SMEM),
            spec(0, False), spec(1, True), spec(1, False), spec(2, True), spec(2, False),
        ],
        out_specs=[out_spec, out_spec],
        out_shape=[jax.ShapeDtypeStruct((b, sd, dil * aw), F32)] * 2,
        compiler_params=_params(("parallel", "parallel", "arbitrary")),
    )(slopes_a, pa3, pa3, pa3, pa3, pa3)


def _cmp_kernel(x_ref, pos_ref, w1_ref, b1_ref, w2_ref, o_ref):
    half = x_ref.shape[-1]
    x = x_ref[0, 0]
    w1 = w1_ref[0]
    first = _dot(x, w1[:half])
    second = _dot(x, w1[half:])
    nrow = x.shape[0]
    pre = first + pltpu.roll(second, nrow - 1, 0)
    posb = _dot(jnp.broadcast_to(pos_ref[0], (8, 2 * half)).astype(BF16), w1)[0:1]
    pre = pre + posb + b1_ref[0]
    o_ref[0, 0] = _dot(jax.nn.gelu(pre).astype(BF16), w2_ref[0]).astype(o_ref.dtype)


def _compress(xc, pos, w1, b1, w2):
    b, four, nchunk, width = xc.shape
    return pl.pallas_call(
        _cmp_kernel,
        grid=(b, four),
        in_specs=[
            pl.BlockSpec((1, 1, nchunk, width), lambda bb, j: (bb, j, 0, 0)),
            pl.BlockSpec((1, 1, 2 * width), lambda bb, j: (j // 2, 0, 0)),
            pl.BlockSpec((1, 2 * width, CMP_HIDDEN), lambda bb, j: (j // 2, 0, 0)),
            pl.BlockSpec((1, 1, CMP_HIDDEN), lambda bb, j: (j // 2, 0, 0)),
            pl.BlockSpec((1, CMP_HIDDEN, LANE), lambda bb, j: (j // 2, 0, 0)),
        ],
        out_specs=pl.BlockSpec((1, 1, nchunk, LANE), lambda bb, j: (bb, j, 0, 0)),
        out_shape=jax.ShapeDtypeStruct((b, four, nchunk, LANE), BF16),
        compiler_params=_params(("parallel", "parallel")),
    )(xc, pos, w1, b1, w2)


def _nsa_kernel(slope_ref, q_ref, kcv_ref, ks_ref, vs_ref, kw_ref, vw_ref, gt_ref, ov_ref, ex_ref,
                o_ref, *, seq):
    g = pl.program_id(1)
    c = pl.program_id(2)
    C = NSA_Q
    R = NSA_REP
    rows = R * C
    t0 = c * C
    scale = HEAD_DIM ** -0.5
    nb = seq // SLC_BLOCK
    ncmp = seq // CMP_STRIDE
    n_sel = min(SLC_TOP_N, nb)

    q4 = jnp.concatenate([q_ref[0, :, r * LANE:(r + 1) * LANE] for r in range(R)], axis=0)
    q4 = (q4.astype(F32) * scale).astype(BF16)
    row = lax.broadcasted_iota(jnp.int32, (rows, 1), 0)
    tq = (t0 + row % C).astype(F32)
    slope = jnp.zeros((rows, 1), F32)
    for r in range(R):
        slope = jnp.where(row // C == r, slope_ref[g * R + r], slope)

    kc = kcv_ref[0, 0]
    vc = kcv_ref[0, 1]
    ci = lax.broadcasted_iota(jnp.int32, (1, ncmp), 1)
    cend = (ci * CMP_STRIDE + (CMP_BLOCK - 1)).astype(F32)
    dist = tq - cend
    vis = (dist >= 0) & (ci < ncmp - 1)
    s = jnp.where(vis, _dot_nt(q4, kc) - slope * dist, NEG)
    m = jnp.max(s, axis=-1, keepdims=True)
    e = jnp.where(vis, jnp.exp(s - m), 0.0)
    den = jnp.sum(e, axis=-1, keepdims=True)
    pc = e / jnp.where(den > 0, den, 1.0)
    o_cmp = _dot(pc.astype(BF16), vc)
    pcsum = pc[0:C]
    for r in range(1, R):
        pcsum = pcsum + pc[r * C:(r + 1) * C]
    ov = ov_ref[...]
    p_slc = sum(_dot_nt(ov, part) for part in _split3(pcsum))

    j = lax.broadcasted_iota(jnp.int32, (nb, C), 0)
    t = t0 + lax.broadcasted_iota(jnp.int32, (nb, C), 1)
    cur = t // SLC_BLOCK
    forced = (j == 0) | (j == cur) | (j == cur - 1)
    future = j * SLC_BLOCK > t
    work = jnp.where(forced, SLC_FORCE, jnp.where(future, -SLC_FORCE, p_slc))
    sel = jnp.zeros((nb, C), F32)
    jf = j.astype(F32)
    for _ in range(n_sel):
        mx = jnp.max(work, axis=0, keepdims=True)
        first = jnp.min(jnp.where(work == mx, jf, float(nb)), axis=0, keepdims=True)
        pick = jf == first
        sel = jnp.where(pick, 1.0, sel)
        work = jnp.where(pick, -jnp.inf, work)
    sel_q = sel.T.astype(BF16)

    kiota = lax.broadcasted_iota(jnp.int32, (1, NSA_TK), 1)

    def slc_body(kt, carry):
        m_i, l_i, acc = carry
        start = pl.multiple_of(kt * NSA_TK, NSA_TK)
        k_t = ks_ref[0, pl.ds(start, NSA_TK), :]
        v_t = vs_ref[0, pl.ds(start, NSA_TK), :]
        kpos = (start + kiota).astype(F32)
        d2 = tq - kpos
        picked = _dot(sel_q, ex_ref[kt])
        picked = jnp.concatenate([picked] * R, axis=0)
        ok = (picked > 0.5) & (d2 >= 0)
        s2 = jnp.where(ok, _dot_nt(q4, k_t) - slope * d2, NEG)
        m_n = jnp.maximum(m_i, jnp.max(s2, axis=-1, keepdims=True))
        alpha = jnp.exp(m_i - m_n)
        p2 = jnp.where(ok, jnp.exp(s2 - m_n), 0.0)
        l_n = alpha * l_i + jnp.sum(p2, axis=-1, keepdims=True)
        acc_n = alpha * acc + _dot(p2.astype(BF16), v_t)
        return m_n, l_n, acc_n

    n_kt = (t0 + C + NSA_TK - 1) // NSA_TK
    init = (jnp.full((rows, 1), NEG, F32), jnp.zeros((rows, 1), F32), jnp.zeros((rows, LANE), F32))
    _, l_s, acc_s = lax.fori_loop(0, n_kt, slc_body, init)
    o_slc = acc_s / l_s

    span = WIN_SIZE + C
    ws = pl.multiple_of(jnp.maximum(t0 - WIN_SIZE, 0), C)
    k_w = kw_ref[0, pl.ds(ws, span), :]
    v_w = vw_ref[0, pl.ds(ws, span), :]
    wpos = (ws + lax.broadcasted_iota(jnp.int32, (1, span), 1)).astype(F32)
    d3 = tq - wpos
    okw = (d3 >= 0) & (d3 <= WIN_SIZE - 1)
    s3 = jnp.where(okw, _dot_nt(q4, k_w) - slope * d3, NEG)
    m3 = jnp.max(s3, axis=-1, keepdims=True)
    e3 = jnp.exp(s3 - m3)
    o_win = _dot(e3.astype(BF16), v_w) / jnp.sum(e3, axis=-1, keepdims=True)

    gates = jax.nn.sigmoid(gt_ref[0])
    for r in range(R):
        rs = slice(r * C, (r + 1) * C)
        out = (gates[:, r:r + 1] * o_cmp[rs] + gates[:, R + r:R + r + 1] * o_slc[rs]
               + gates[:, 2 * R + r:2 * R + r + 1] * o_win[rs])
        o_ref[0, :, r * LANE:(r + 1) * LANE] = out.astype(o_ref.dtype)


def _nsa(slopes_b, pa3, gt3, kcv, ov, ex, b, s, col0, gcol0):
    C = NSA_Q
    R = NSA_REP
    G = NSA_KV_HEADS
    nb = s // SLC_BLOCK
    nkt = s // NSA_TK
    qc0 = col0 // R
    kv0 = col0 + NSA_HEADS

    def kv_spec(which):
        return pl.BlockSpec((1, s, LANE), lambda bb, g, c: (bb, 0, kv0 + which * G + g))

    return pl.pallas_call(
        functools.partial(_nsa_kernel, seq=s),
        grid=(b, G, s // C),
        in_specs=[
            pl.BlockSpec(memory_space=pltpu.SMEM),
            pl.BlockSpec((1, C, R * LANE), lambda bb, g, c: (bb, c, qc0 + g)),
            pl.BlockSpec((1, 2, s // CMP_STRIDE, LANE), lambda bb, g, c: (bb * G + g, 0, 0, 0)),
            kv_spec(2), kv_spec(3), kv_spec(4), kv_spec(5),
            pl.BlockSpec((1, C, LANE), lambda bb, g, c: (bb, c, gcol0 + g)),
            pl.BlockSpec((nb, s // CMP_STRIDE), lambda bb, g, c: (0, 0)),
            pl.BlockSpec((nkt, nb, NSA_TK), lambda bb, g, c: (0, 0, 0)),
        ],
        out_specs=pl.BlockSpec((1, C, R * LANE), lambda bb, g, c: (bb, c, g)),
        out_shape=jax.ShapeDtypeStruct((b, s, NSA_HEADS * LANE), BF16),
        compiler_params=_params(("parallel", "parallel", "arbitrary")),
    )(slopes_b, pa3, kcv, pa3, pa3, pa3, pa3, gt3, ov, ex)


def _merge_kernel(o1, o2, o3, l1, l2, l3, yb, ga, gb, x_ref, wa, wb, wo, gf, wrh, wrl,
                  h_ref, xn_ref, lg_ref):
    a1, a2, a3 = l1[...], l2[...], l3[...]
    mx = jnp.maximum(jnp.maximum(a1, a2), a3)
    e1, e2, e3 = jnp.exp(a1 - mx), jnp.exp(a2 - mx), jnp.exp(a3 - mx)
    tot = e1 + e2 + e3
    ya = (e1 / tot) * o1[...] + (e2 / tot) * o2[...] + (e3 / tot) * o3[...]
    merged = (jax.nn.sigmoid(ga[...]) * _dot(ya.astype(BF16), wa[...])
              + jax.nn.sigmoid(gb[...]) * _dot(yb[...], wb[...]))
    h = x_ref[...] + _dot(merged.astype(BF16), wo[...])
    h_ref[...] = h
    xn = _rms(h, gf[...])
    xn_ref[...] = xn.astype(BF16)
    xh = xn.astype(BF16)
    xl = (xn - xh.astype(F32)).astype(BF16)
    lg_ref[...] = _dot_nt(wrh[...], xh) + _dot_nt(wrh[...], xl) + _dot_nt(wrl[...], xh)


def _merge(o_list, l_list, yb, gt, x2, wa, wb, wo, gf, wrh, wrl):
    n, d = x2.shape
    aw = wa.shape[0]
    bw = wb.shape[0]
    tm = ROW_TILE
    row = lambda w: pl.BlockSpec((tm, w), lambda i: (i, 0))
    full = lambda a: pl.BlockSpec(a.shape, lambda i: (0,) * a.ndim)
    return pl.pallas_call(
        _merge_kernel,
        grid=(n // tm,),
        in_specs=[row(aw)] * 6 + [
            row(bw),
            pl.BlockSpec((tm, d), lambda i: (i, 0)),
            pl.BlockSpec((tm, d), lambda i: (i, 1)),
            row(d), full(wa), full(wb), full(wo), full(gf), full(wrh), full(wrl),
        ],
        out_specs=[row(d), row(d), pl.BlockSpec((N_EXPERTS, tm), lambda i: (0, i))],
        out_shape=[
            jax.ShapeDtypeStruct((n, d), F32),
            jax.ShapeDtypeStruct((n, d), BF16),
            jax.ShapeDtypeStruct((N_EXPERTS, n), F32),
        ],
        compiler_params=_params(("parallel",)),
    )(*o_list, *l_list, yb, gt, gt, x2, wa, wb, wo, gf, wrh, wrl)


def _route_kernel(lg_ref, bias_ref, idx_ref, w_ref, mask_ref):
    E = N_EXPERTS
    tn = lg_ref.shape[1]
    scores = jax.nn.sigmoid(lg_ref[...])
    sel = scores + bias_ref[...]
    sub = lax.broadcasted_iota(jnp.int32, (GROUP_SIZE, tn), 0).astype(F32)
    gs = []
    for grp in range(N_GROUPS):
        blk = sel[grp * GROUP_SIZE:(grp + 1) * GROUP_SIZE]
        m1 = jnp.max(blk, axis=0, keepdims=True)
        first = jnp.min(jnp.where(blk == m1, sub, float(GROUP_SIZE)), axis=0, keepdims=True)
        m2 = jnp.max(jnp.where(sub == first, -jnp.inf, blk), axis=0, keepdims=True)
        gs.append(m1 + m2)
    keep = []
    for a in range(N_GROUPS):
        rank = jnp.zeros((1, tn), jnp.int32)
        for o in range(N_GROUPS):
            if o == a:
                continue
            beats = (gs[o] >= gs[a]) if o < a else (gs[o] > gs[a])
            rank = rank + beats.astype(jnp.int32)
        keep.append(jnp.broadcast_to(rank < TOP_GROUPS, (GROUP_SIZE, tn)))
    emask = jnp.concatenate(keep, axis=0)
    cand = jnp.where(emask, sel, -jnp.inf)
    eid = lax.broadcasted_iota(jnp.int32, (E, tn), 0)
    eid_f = eid.astype(F32)
    rank = jnp.zeros((E, tn), jnp.int32)
    for o in range(E):
        other = cand[o:o + 1]
        beats = (other > cand) | ((other == cand) & (eid > o))
        rank = rank + beats.astype(jnp.int32)
    chosen = emask & (rank < TOP_K)
    wsel = jnp.where(chosen, scores, 0.0)
    wnorm = wsel / jnp.sum(wsel, axis=0, keepdims=True) * ROUTED_SCALE
    mask_ref[...] = chosen.astype(jnp.int32)
    for k in range(TOP_K):
        hit = chosen & (rank == k)
        idx_ref[k:k + 1, :] = jnp.sum(jnp.where(hit, eid_f, 0.0), axis=0, keepdims=True).astype(jnp.int32)
        w_ref[k:k + 1, :] = jnp.sum(jnp.where(hit, wnorm, 0.0), axis=0, keepdims=True)


def _route(logits_t, bias_col):
    e, n = logits_t.shape
    tn = ROW_TILE
    return pl.pallas_call(
        _route_kernel,
        grid=(n // tn,),
        in_specs=[pl.BlockSpec((e, tn), lambda i: (0, i)), pl.BlockSpec((e, 1), lambda i: (0, 0))],
        out_specs=[
            pl.BlockSpec((TOP_K, tn), lambda i: (0, i)),
            pl.BlockSpec((TOP_K, tn), lambda i: (0, i)),
            pl.BlockSpec((e, tn), lambda i: (0, i)),
        ],
        out_shape=[
            jax.ShapeDtypeStruct((TOP_K, n), jnp.int32),
            jax.ShapeDtypeStruct((TOP_K, n), F32),
            jax.ShapeDtypeStruct((e, n), jnp.int32),
        ],
        compiler_params=_params(("parallel",)),
    )(logits_t, bias_col)


def _moe_kernel(blk_e_ref, nused_ref, xs_ref, wg_ref, wu_ref, wd_ref, o_ref):
    i = pl.program_id(0)

    @pl.when(i < nused_ref[0])
    def _():
        x = xs_ref[...]
        hg = _dot(x, wg_ref[0])
        hid = (hg * jax.nn.sigmoid(hg)) * _dot(x, wu_ref[0])
        o_ref[...] = _dot(hid.astype(BF16), wd_ref[0]).astype(o_ref.dtype)

    @pl.when(i >= nused_ref[0])
    def _():
        o_ref[...] = jnp.zeros_like(o_ref)


def _moe_experts(blk_e, nused, xs, wg, wu, wd):
    p, d = xs.shape
    hid = wg.shape[2]
    grid_spec = pltpu.PrefetchScalarGridSpec(
        num_scalar_prefetch=2,
        grid=(p // MOE_ROWS,),
        in_specs=[
            pl.BlockSpec((MOE_ROWS, d), lambda i, be, nu: (i, 0)),
            pl.BlockSpec((1, d, hid), lambda i, be, nu: (be[i], 0, 0)),
            pl.BlockSpec((1, d, hid), lambda i, be, nu: (be[i], 0, 0)),
            pl.BlockSpec((1, hid, d), lambda i, be, nu: (be[i], 0, 0)),
        ],
        out_specs=pl.BlockSpec((MOE_ROWS, d), lambda i, be, nu: (i, 0)),
    )
    return pl.pallas_call(
        _moe_kernel,
        grid_spec=grid_spec,
        out_shape=jax.ShapeDtypeStruct((p, d), BF16),
        compiler_params=_params(("arbitrary",)),
    )(blk_e, nused, xs, wg, wu, wd)


def _final_kernel(h_ref, xn_ref, y_ref, p_ref, wsg, wsu, wsd, gp, wpg, wp, gfin, o_ref):
    xn = xn_ref[...]
    hg = _dot(xn, wsg[...])
    shared = _dot(((hg * jax.nn.sigmoid(hg)) * _dot(xn, wsu[...])).astype(BF16), wsd[...])
    h = h_ref[...] + (y_ref[...] + shared)
    gate = jax.nn.sigmoid(_dot(_rms(h, gp[...]).astype(BF16), wpg[...]))
    h = h + _dot(p_ref[...].astype(BF16), wp[...]) * gate
    o_ref[...] = _rms(h, gfin[...])


def _final(h1, xn2, y, p2, wsg, wsu, wsd, gp, wpg, wp, gfin):
    n, d = h1.shape
    tm = ROW_TILE
    row = lambda w: pl.BlockSpec((tm, w), lambda i: (i, 0))
    full = lambda a: pl.BlockSpec(a.shape, lambda i: (0,) * a.ndim)
    return pl.pallas_call(
        _final_kernel,
        grid=(n // tm,),
        in_specs=[row(d), row(d), row(d), row(p2.shape[1]),
                  full(wsg), full(wsu), full(wsd), full(gp), full(wpg), full(wp), full(gfin)],
        out_specs=row(d),
        out_shape=jax.ShapeDtypeStruct((n, d), F32),
        compiler_params=_params(("parallel",)),
    )(h1, xn2, y, p2, wsg, wsu, wsd, gp, wpg, wp, gfin)


def _pad_heads(w):
    d, c = w.shape
    nh = c // HEAD_DIM
    w = w.reshape(d, nh, HEAD_DIM)
    return jnp.pad(w, ((0, 0), (0, 0), (0, LANE - HEAD_DIM))).reshape(d, nh * LANE)


def _alibi_slopes(n):
    return 2.0 ** (-8.0 * jnp.arange(1, n + 1, dtype=jnp.float32) / n)


def kernel(x, p, g_mix, w_in, cmp_pos_k, cmp_w1_k, cmp_b1_k, cmp_w2_k, cmp_pos_v, cmp_w1_v, cmp_b1_v, cmp_w2_v, w_branch_a, w_branch_b, w_out, g_ffn, w_router, router_bias, w_gate, w_up, w_down, ws_gate, ws_up, ws_down, g_ple, w_ple_gate, w_ple, g_final):
    B, S, D = x.shape
    N = B * S
    G, R = NSA_KV_HEADS, NSA_REP
    assert w_in.shape[0] == 1 and p.shape[0] == 1
    assert S % (DIL_BLOCK * DIL_PATTERNS[-1][1]) == 0 and S % NSA_TK == 0 and S >= WIN_SIZE + NSA_Q
    assert N % ROW_TILE == 0
    slopes = _alibi_slopes(DIL_HEADS + NSA_HEADS)
    slopes_a, slopes_b = slopes[0::2], slopes[1::2]

    offs = np.concatenate([[0], np.cumsum(IN_SIZES)])
    parts = [w_in[0][:, offs[i]:offs[i + 1]] for i in range(len(IN_SIZES))]
    w_qa, w_ka, w_va, w_qb, w_kc, w_vc, w_ks, w_vs, w_kw, w_vw, w_gn, w_ga, w_gb = parts
    w_act = jnp.concatenate([w_qa, w_ka, w_va, jnp.zeros((D, 512), F32)]
                            + [_pad_heads(t) for t in (w_qb, w_kc, w_vc, w_ks, w_vs, w_kw, w_vw)],
                            axis=1).astype(BF16)
    act_w = w_act.shape[1]
    nsa_col0 = 2048 // LANE
    w_gn = w_gn.reshape(D, 3, G, R).transpose(0, 2, 1, 3).reshape(D, G, 3 * R)
    w_gn = jnp.pad(w_gn, ((0, 0), (0, 0), (0, LANE - 3 * R))).reshape(D, G * LANE)
    w_gate_in = jnp.concatenate([w_ga, w_gb, w_gn, jnp.zeros((D, 256), F32)], axis=1).astype(BF16)
    gate_w = w_gate_in.shape[1]

    x2 = x.reshape(N, D)
    act = _inproj(x2, g_mix, w_act, BF16, 512)
    gt = _inproj(x2, g_mix, w_gate_in, F32, 512)

    o_list, l_list = [], []
    for _, dil in DIL_PATTERNS:
        o, l = _dilated(act.reshape(B, S // dil, dil * act_w), slopes_a, dil, B, S, act_w)
        o_list.append(o.reshape(N, DIL_HEADS * HEAD_DIM))
        l_list.append(l.reshape(N, DIL_HEADS * HEAD_DIM))

    nchunk = S // CMP_STRIDE
    c0 = nsa_col0 * LANE + NSA_HEADS * LANE
    raw = act[:, c0:c0 + 2 * G * LANE].reshape(B, nchunk, CMP_STRIDE, 2 * G, LANE)[..., :HEAD_DIM]
    xc = raw.transpose(0, 3, 1, 2, 4).reshape(B, 2 * G, nchunk, CMP_STRIDE * HEAD_DIM)
    pos = jnp.stack([cmp_pos_k[0].reshape(1, -1), cmp_pos_v[0].reshape(1, -1)])
    w1 = jnp.stack([cmp_w1_k[0], cmp_w1_v[0]]).astype(BF16)
    b1 = jnp.stack([cmp_b1_k[0][None], cmp_b1_v[0][None]])
    w2 = jnp.pad(jnp.stack([cmp_w2_k[0], cmp_w2_v[0]]), ((0, 0), (0, 0), (0, LANE - HEAD_DIM))).astype(BF16)
    kcv = _compress(xc, pos, w1, b1, w2)
    kcv = kcv.reshape(B, 2, G, nchunk, LANE).transpose(0, 2, 1, 3, 4).reshape(B * G, 2, nchunk, LANE)

    nb = S // SLC_BLOCK
    ci = np.arange(nchunk)[None, :]
    sj = np.arange(nb)[:, None]
    ov = ((ci * CMP_STRIDE < (sj + 1) * SLC_BLOCK) & (ci * CMP_STRIDE + CMP_BLOCK > sj * SLC_BLOCK)
          & (ci < nchunk - 1))
    ov = jnp.asarray(ov, BF16)
    kk = np.arange(S).reshape(S // NSA_TK, 1, NSA_TK)
    ex = jnp.asarray(kk // SLC_BLOCK == np.arange(nb)[None, :, None], BF16)
    yb = _nsa(slopes_b, act.reshape(B, S, act_w), gt.reshape(B, S, gate_w), kcv, ov, ex,
              B, S, nsa_col0, 2 * D // LANE)
    yb = yb.reshape(N, NSA_HEADS * LANE)

    wb_pad = jnp.pad(w_branch_b[0].reshape(NSA_HEADS, HEAD_DIM, D),
                     ((0, 0), (0, LANE - HEAD_DIM), (0, 0))).reshape(NSA_HEADS * LANE, D)
    wr_t = w_router[0].T
    wr_hi = wr_t.astype(BF16)
    wr_lo = (wr_t - wr_hi.astype(F32)).astype(BF16)
    h1, xn2, logits_t = _merge(o_list, l_list, yb, gt, x2, w_branch_a[0].astype(BF16), wb_pad.astype(BF16),
                               w_out[0].astype(BF16), g_ffn, wr_hi, wr_lo)

    idx_t, wsel_t, mask_t = _route(logits_t, router_bias[0][:, None])
    counts = jnp.sum(mask_t, axis=1)
    padded = (counts + MOE_ROWS - 1) // MOE_ROWS * MOE_ROWS
    pend = jnp.cumsum(padded)
    pstart = pend - padded
    rank_t = jnp.cumsum(mask_t, axis=1) - mask_t
    pos_t = jnp.take_along_axis(rank_t + pstart[:, None], idx_t, axis=0)
    NK = N * TOP_K
    nblk = (NK + N_EXPERTS * (MOE_ROWS - 1) + MOE_ROWS - 1) // MOE_ROWS
    P = nblk * MOE_ROWS
    tok = jnp.broadcast_to(jnp.arange(N, dtype=jnp.int32)[None, :], (TOP_K, N))
    row_tok = jnp.zeros((P,), jnp.int32).at[pos_t.reshape(-1)].set(tok.reshape(-1), unique_indices=True)
    blk_e = jnp.minimum(jnp.searchsorted(pend, jnp.arange(nblk, dtype=jnp.int32) * MOE_ROWS, side='right'),
                        N_EXPERTS - 1).astype(jnp.int32)
    nused = (pend[-1] // MOE_ROWS).astype(jnp.int32).reshape(1)
    xs = xn2[row_tok]
    ys = _moe_experts(blk_e, nused, xs, w_gate[0].astype(BF16), w_up[0].astype(BF16), w_down[0].astype(BF16))
    y = jnp.einsum('kn,knd->nd', wsel_t, ys[pos_t].astype(F32))

    out = _final(h1, xn2, y, p[0].reshape(N, -1), ws_gate[0].astype(BF16), ws_up[0].astype(BF16),
                 ws_down[0].astype(BF16), g_ple, w_ple_gate[0].astype(BF16), w_ple[0].astype(BF16),
                 g_final[None, :])
    return out.reshape(B, S, D)
```

```python
import functools

import numpy as np
import jax
import jax.numpy as jnp
from jax import lax
from jax.experimental import pallas as pl
from jax.experimental.pallas import tpu as pltpu

F32 = jnp.float32
BF16 = jnp.bfloat16

HEAD_DIM = 64
LANE = 128
RMS_EPS = 1e-6
DIL_HEADS = 8
DIL_PATTERNS = ((128, 1), (512, 4), (2048, 16))
DIL_BLOCK = 128
NSA_HEADS = 8
NSA_KV_HEADS = 2
NSA_REP = NSA_HEADS // NSA_KV_HEADS
CMP_BLOCK = 32
CMP_STRIDE = 16
CMP_HIDDEN = 256
SLC_BLOCK = 64
SLC_TOP_N = 16
SLC_FORCE = 1e9
WIN_SIZE = 512
N_EXPERTS = 64
TOP_K = 8
N_GROUPS = 8
TOP_GROUPS = 4
GROUP_SIZE = N_EXPERTS // N_GROUPS
ROUTED_SCALE = 2.5
IN_SIZES = (512, 512, 512, 512, 128, 128, 128, 128, 128, 128, 24, 1024, 1024)

NEG = -1e30
NT_DIMS = (((1,), (1,)), ((), ()))

ROW_TILE = 512
NSA_Q = 128
NSA_TK = 512
MOE_ROWS = 512
VMEM_LIMIT = 48 * 1024 * 1024


def _dot(a, b):
    return jnp.dot(a, b, preferred_element_type=F32)


def _dot_nt(a, b):
    return lax.dot_general(a, b, NT_DIMS, preferred_element_type=F32)


def _split3(x):
    hi = x.astype(BF16)
    r1 = x - hi.astype(F32)
    mid = r1.astype(BF16)
    lo = (r1 - mid.astype(F32)).astype(BF16)
    return hi, mid, lo


def _rms(x, g):
    r = lax.rsqrt(jnp.mean(x * x, axis=-1, keepdims=True) + RMS_EPS)
    return (x * r) * g


def _params(sem):
    return pltpu.CompilerParams(dimension_semantics=sem, vmem_limit_bytes=VMEM_LIMIT)


def _inproj_kernel(x_ref, g_ref, w_ref, o_ref, xn_ref):
    @pl.when(pl.program_id(1) == 0)
    def _():
        xn_ref[...] = _rms(x_ref[...], g_ref[...]).astype(BF16)

    o_ref[...] = _dot(xn_ref[...], w_ref[...]).astype(o_ref.dtype)


def _inproj(x2, g, w, out_dtype, tn):
    n, d = x2.shape
    cols = w.shape[1]
    return pl.pallas_call(
        _inproj_kernel,
        grid=(n // ROW_TILE, cols // tn),
        in_specs=[
            pl.BlockSpec((ROW_TILE, d), lambda i, j: (i, 0)),
            pl.BlockSpec((1, d), lambda i, j: (0, 0)),
            pl.BlockSpec((d, tn), lambda i, j: (0, j)),
        ],
        out_specs=pl.BlockSpec((ROW_TILE, tn), lambda i, j: (i, j)),
        out_shape=jax.ShapeDtypeStruct((n, cols), out_dtype),
        scratch_shapes=[pltpu.VMEM((ROW_TILE, d), BF16)],
        compiler_params=_params(("parallel", "arbitrary")),
    )(x2, g, w)


def _dil_kernel(slope_ref, q_ref, kp_ref, kc_ref, vp_ref, vc_ref, o_ref, l_ref, *, dil):
    i = pl.program_id(2)
    blk = DIL_BLOCK
    qi = lax.broadcasted_iota(jnp.int32, (blk, 2 * blk), 0)
    kj = lax.broadcasted_iota(jnp.int32, (blk, 2 * blk), 1)
    dist = (qi + blk - kj).astype(F32)
    valid = (dist >= 0) & (dist <= blk) & ((kj >= blk) | (i > 0))
    lane = lax.broadcasted_iota(jnp.int32, (blk, LANE), 1)
    lo = lane < HEAD_DIM
    scale = HEAD_DIM ** -0.5
    for hp in range(DIL_HEADS // 2):
        sl = slice(hp * LANE, (hp + 1) * LANE)
        q2 = q_ref[0, :, sl]
        k2 = jnp.concatenate([kp_ref[0, :, sl], kc_ref[0, :, sl]], axis=0)
        v2 = jnp.concatenate([vp_ref[0, :, sl], vc_ref[0, :, sl]], axis=0)
        outs, lses = [], []
        for hh in range(2):
            keep = lo if hh == 0 else jnp.logical_not(lo)
            qm = jnp.where(keep, q2, jnp.zeros_like(q2))
            slope = slope_ref[2 * hp + hh] * float(dil)
            s = _dot_nt(qm, k2) * scale - slope * dist
            s = jnp.where(valid, s, NEG)
            m = jnp.max(s, axis=-1, keepdims=True)
            e = jnp.exp(s - m)
            den = jnp.sum(e, axis=-1, keepdims=True)
            outs.append(_dot(e.astype(BF16), v2) / den)
            lses.append(jnp.broadcast_to(m + jnp.log(den), (blk, LANE)))
        o_ref[0, :, sl] = jnp.where(lo, outs[0], outs[1])
        l_ref[0, :, sl] = jnp.where(lo, lses[0], lses[1])


def _dilated(pa3, slopes_a, dil, b, s, width):
    sd = s // dil
    nblk = sd // DIL_BLOCK
    aw = DIL_HEADS * HEAD_DIM
    per = width // aw
    blk = (1, DIL_BLOCK, aw)

    def spec(col, prev):
        if prev:
            return pl.BlockSpec(blk, lambda bb, r, i: (bb, jnp.maximum(i - 1, 0), r * per + col))
        return pl.BlockSpec(blk, lambda bb, r, i: (bb, i, r * per + col))

    out_spec = pl.BlockSpec(blk, lambda bb, r, i: (bb, i, r))
    return pl.pallas_call(
        functools.partial(_dil_kernel, dil=dil),
        grid=(b, dil, nblk),
        in_specs=[
            pl.BlockSpec(memory_space=pltpu.SMEM),
            spec(0, False), spec(1, True), spec(1, False), spec(2, True), spec(2, False),
        ],
        out_specs=[out_spec, out_spec],
        out_shape=[jax.ShapeDtypeStruct((b, sd, dil * aw), F32)] * 2,
        compiler_params=_params(("parallel", "parallel", "arbitrary")),
    )(slopes_a, pa3, pa3, pa3, pa3, pa3)


def _cmp_kernel(x_ref, pos_ref, w1_ref, b1_ref, w2_ref, o_ref):
    half = x_ref.shape[-1]
    x = x_ref[0, 0]
    w1 = w1_ref[0]
    first = _dot(x, w1[:half])
    second = _dot(x, w1[half:])
    nrow = x.shape[0]
    pre = first + pltpu.roll(second, nrow - 1, 0)
    posb = _dot(jnp.broadcast_to(pos_ref[0], (8, 2 * half)).astype(BF16), w1)[0:1]
    pre = pre + posb + b1_ref[0]
    o_ref[0, 0] = _dot(jax.nn.gelu(pre).astype(BF16), w2_ref[0]).astype(o_ref.dtype)


def _compress(xc, pos, w1, b1, w2):
    b, four, nchunk, width = xc.shape
    return pl.pallas_call(
        _cmp_kernel,
        grid=(b, four),
        in_specs=[
            pl.BlockSpec((1, 1, nchunk, width), lambda bb, j: (bb, j, 0, 0)),
            pl.BlockSpec((1, 1, 2 * width), lambda bb, j: (j // 2, 0, 0)),
            pl.BlockSpec((1, 2 * width, CMP_HIDDEN), lambda bb, j: (j // 2, 0, 0)),
            pl.BlockSpec((1, 1, CMP_HIDDEN), lambda bb, j: (j // 2, 0, 0)),
            pl.BlockSpec((1, CMP_HIDDEN, LANE), lambda bb, j: (j // 2, 0, 0)),
        ],
        out_specs=pl.BlockSpec((1, 1, nchunk, LANE), lambda bb, j: (bb, j, 0, 0)),
        out_shape=jax.ShapeDtypeStruct((b, four, nchunk, LANE), BF16),
        compiler_params=_params(("parallel", "parallel")),
    )(xc, pos, w1, b1, w2)


POS_LANE = HEAD_DIM
POS_SPLIT = 64
MASKED = -1e30
ROW_FLOOR = -1e29


def _pos_columns(pos):
    out = np.zeros((pos.shape[0], LANE), np.float32)
    out[:, POS_LANE] = pos // POS_SPLIT
    out[:, POS_LANE + 1] = pos % POS_SPLIT
    out[:, POS_LANE + 2] = 1.0
    out[:, POS_LANE + 3] = 1.0
    return jnp.asarray(out, BF16)


def _nsa_kernel(slope_ref, q_ref, kcv_ref, ks_ref, vs_ref, kw_ref, vw_ref, gt_ref, ov_ref, kpa_ref,
                cpa_ref, oh_ref, o_ref, kaug_ref, sel_ref, m_ref, l_ref, acc_ref, *, seq):
    g = pl.program_id(1)
    c = pl.program_id(2)
    C = NSA_Q
    R = NSA_REP
    TK = NSA_TK
    rows = R * C
    t0 = c * C
    scale = HEAD_DIM ** -0.5
    nb = seq // SLC_BLOCK
    ncmp = seq // CMP_STRIDE
    n_sel = min(SLC_TOP_N, nb)

    @pl.when(c == 0)
    def _():
        kaug_ref[:, 0:LANE] = ks_ref[0] + kpa_ref[...]
        kaug_ref[:, LANE:2 * LANE] = oh_ref[...]

    row = lax.broadcasted_iota(jnp.int32, (rows, 1), 0)
    ti = t0 + row % C
    slope = jnp.zeros((rows, 1), F32)
    for r in range(R):
        slope = jnp.where(row // C == r, slope_ref[g * R + r], slope)
    lane = lax.broadcasted_iota(jnp.int32, (rows, LANE), 1)
    thi = (ti // POS_SPLIT).astype(F32)
    tlo = (ti % POS_SPLIT).astype(F32)
    qcols = jnp.where(lane == POS_LANE, POS_SPLIT * slope,
                      jnp.where(lane == POS_LANE + 1, slope,
                                jnp.where(lane == POS_LANE + 2, -POS_SPLIT * slope * thi,
                                          jnp.where(lane == POS_LANE + 3, -slope * tlo, 0.0))))
    q4 = jnp.concatenate([q_ref[0, :, r * LANE:(r + 1) * LANE] for r in range(R)], axis=0)
    q4 = (q4.astype(F32) * scale + qcols).astype(BF16)
    tc = t0 + lax.broadcasted_iota(jnp.int32, (C, 1), 0)

    def tile_rows(x):
        return jnp.concatenate([x] * R, axis=0)

    kc = kcv_ref[0, 0] + cpa_ref[...]
    vc = kcv_ref[0, 1]
    ci = lax.broadcasted_iota(jnp.int32, (1, ncmp), 1)
    vis = (ci * CMP_STRIDE + (CMP_BLOCK - 1) <= tc) & (ci < ncmp - 1)
    s = _dot_nt(q4, kc) + tile_rows(jnp.where(vis, 0.0, MASKED))
    m = jnp.maximum(jnp.max(s, axis=-1, keepdims=True), ROW_FLOOR)
    e = jnp.exp(s - m)
    den = jnp.sum(e, axis=-1, keepdims=True)
    pc = e * (1.0 / jnp.where(den > 0, den, 1.0))
    o_cmp = _dot(pc.astype(BF16), vc)
    pcsum = pc[0:C]
    for r in range(1, R):
        pcsum = pcsum + pc[r * C:(r + 1) * C]
    ov = ov_ref[...]
    p_slc = sum(_dot_nt(ov, part) for part in _split3(pcsum))

    j = lax.broadcasted_iota(jnp.int32, (LANE, C), 0)
    t = t0 + lax.broadcasted_iota(jnp.int32, (LANE, C), 1)
    cur = t // SLC_BLOCK
    forced = (j == 0) | (j == cur) | (j == cur - 1)
    future = j * SLC_BLOCK > t
    work = jnp.where(forced, SLC_FORCE, jnp.where(future, -SLC_FORCE, p_slc))
    work = jnp.where(j < nb, work, -jnp.inf)
    sel = jnp.zeros((LANE, C), F32)
    jf = j.astype(F32)
    for _ in range(n_sel):
        mx = jnp.max(work, axis=0, keepdims=True)
        first = jnp.min(jnp.where(work == mx, jf, float(LANE)), axis=0, keepdims=True)
        pick = jf == first
        sel = jnp.where(pick, 1.0, sel)
        work = jnp.where(pick, -jnp.inf, work)
    sel = jnp.where(future, 0.0, sel)
    sel_ref[...] = sel
    sel_bias = jnp.where(sel.T > 0.5, 0.0, MASKED).astype(BF16)
    q_aug = jnp.concatenate([q4, tile_rows(sel_bias)], axis=1)

    m_ref[...] = jnp.full((rows, 1), ROW_FLOOR, F32)
    l_ref[...] = jnp.zeros((rows, 1), F32)
    acc_ref[...] = jnp.zeros((rows, LANE), F32)
    kiota = lax.broadcasted_iota(jnp.int32, (1, TK), 1)

    def tile_update(kt, causal):
        start = pl.multiple_of(kt * TK, TK)
        s2 = _dot_nt(q_aug, kaug_ref[pl.ds(start, TK), :])
        if causal:
            s2 = s2 + tile_rows(jnp.where(start + kiota <= tc, 0.0, MASKED))
        m_i = m_ref[...]
        m_n = jnp.maximum(m_i, jnp.max(s2, axis=-1, keepdims=True))
        alpha = jnp.exp(m_i - m_n)
        p2 = jnp.exp(s2 - m_n)
        l_ref[...] = alpha * l_ref[...] + jnp.sum(p2, axis=-1, keepdims=True)
        acc_ref[...] = alpha * acc_ref[...] + _dot(p2.astype(BF16), vs_ref[0, pl.ds(start, TK), :])
        m_ref[...] = m_n

    blocks_per_tile = TK // SLC_BLOCK

    def slc_body(kt, carry):
        picked = sel_ref[pl.ds(pl.multiple_of(kt * blocks_per_tile, blocks_per_tile), blocks_per_tile), :]

        @pl.when(jnp.max(picked) > 0.0)
        def _():
            tile_update(kt, False)

        return carry

    last = (t0 + C - 1) // TK
    lax.fori_loop(0, last, slc_body, 0)
    tile_update(last, True)
    o_slc = acc_ref[...] / l_ref[...]

    span = WIN_SIZE + C
    ws = pl.multiple_of(jnp.maximum(t0 - WIN_SIZE, 0), C)
    k_w = kw_ref[0, pl.ds(ws, span), :] + kpa_ref[pl.ds(ws, span), :]
    v_w = vw_ref[0, pl.ds(ws, span), :]
    wpos = ws + lax.broadcasted_iota(jnp.int32, (1, span), 1)
    okw = (wpos <= tc) & (wpos >= tc - (WIN_SIZE - 1))
    s3 = _dot_nt(q4, k_w) + tile_rows(jnp.where(okw, 0.0, MASKED))
    m3 = jnp.max(s3, axis=-1, keepdims=True)
    e3 = jnp.exp(s3 - m3)
    o_win = _dot(e3.astype(BF16), v_w) / jnp.sum(e3, axis=-1, keepdims=True)

    gates = jax.nn.sigmoid(gt_ref[0])
    for r in range(R):
        rs = slice(r * C, (r + 1) * C)
        out = (gates[:, r:r + 1] * o_cmp[rs] + gates[:, R + r:R + r + 1] * o_slc[rs]
               + gates[:, 2 * R + r:2 * R + r + 1] * o_win[rs])
        o_ref[0, :, r * LANE:(r + 1) * LANE] = out.astype(o_ref.dtype)


def _nsa(slopes_b, pa3, gt3, kcv, b, s, col0, gcol0):
    C = NSA_Q
    R = NSA_REP
    G = NSA_KV_HEADS
    nb = s // SLC_BLOCK
    ncmp = s // CMP_STRIDE
    assert nb <= LANE
    qc0 = col0 // R
    kv0 = col0 + NSA_HEADS
    rows = R * C

    ci = np.arange(ncmp)[None, :]
    sj = np.arange(LANE)[:, None]
    ov = ((ci * CMP_STRIDE < (sj + 1) * SLC_BLOCK) & (ci * CMP_STRIDE + CMP_BLOCK > sj * SLC_BLOCK)
          & (ci < ncmp - 1) & (sj < nb))
    ov = jnp.asarray(ov, BF16)
    kpa = _pos_columns(np.arange(s))
    cpa = _pos_columns(np.arange(ncmp) * CMP_STRIDE + (CMP_BLOCK - 1))
    oh = jnp.asarray(np.arange(s)[:, None] // SLC_BLOCK == np.arange(LANE)[None, :], BF16)

    def kv_spec(which):
        return pl.BlockSpec((1, s, LANE), lambda bb, g, c: (bb, 0, kv0 + which * G + g))

    const = lambda a: pl.BlockSpec(a.shape, lambda bb, g, c: (0,) * a.ndim)
    return pl.pallas_call(
        functools.partial(_nsa_kernel, seq=s),
        grid=(b, G, s // C),
        in_specs=[
            pl.BlockSpec(memory_space=pltpu.SMEM),
            pl.BlockSpec((1, C, R * LANE), lambda bb, g, c: (bb, c, qc0 + g)),
            pl.BlockSpec((1, 2, ncmp, LANE), lambda bb, g, c: (bb * G + g, 0, 0, 0)),
            kv_spec(2), kv_spec(3), kv_spec(4), kv_spec(5),
            pl.BlockSpec((1, C, LANE), lambda bb, g, c: (bb, c, gcol0 + g)),
            const(ov), const(kpa), const(cpa), const(oh),
        ],
        out_specs=pl.BlockSpec((1, C, R * LANE), lambda bb, g, c: (bb, c, g)),
        out_shape=jax.ShapeDtypeStruct((b, s, NSA_HEADS * LANE), BF16),
        scratch_shapes=[
            pltpu.VMEM((s, 2 * LANE), BF16),
            pltpu.VMEM((LANE, C), F32),
            pltpu.VMEM((rows, 1), F32),
            pltpu.VMEM((rows, 1), F32),
            pltpu.VMEM((rows, LANE), F32),
        ],
        compiler_params=_params(("parallel", "parallel", "arbitrary")),
    )(slopes_b, pa3, kcv, pa3, pa3, pa3, pa3, gt3, ov, kpa, cpa, oh)


def _merge_kernel(o1, o2, o3, l1, l2, l3, yb, ga, gb, x_ref, wa, wb, wo, gf, wrh, wrl,
                  h_ref, xn_ref, lg_ref):
    a1, a2, a3 = l1[...], l2[...], l3[...]
    mx = jnp.maximum(jnp.maximum(a1, a2), a3)
    e1, e2, e3 = jnp.exp(a1 - mx), jnp.exp(a2 - mx), jnp.exp(a3 - mx)
    tot = e1 + e2 + e3
    ya = (e1 / tot) * o1[...] + (e2 / tot) * o2[...] + (e3 / tot) * o3[...]
    merged = (jax.nn.sigmoid(ga[...]) * _dot(ya.astype(BF16), wa[...])
              + jax.nn.sigmoid(gb[...]) * _dot(yb[...], wb[...]))
    h = x_ref[...] + _dot(merged.astype(BF16), wo[...])
    h_ref[...] = h
    xn = _rms(h, gf[...])
    xn_ref[...] = xn.astype(BF16)
    xh = xn.astype(BF16)
    xl = (xn - xh.astype(F32)).astype(BF16)
    lg_ref[...] = _dot_nt(wrh[...], xh) + _dot_nt(wrh[...], xl) + _dot_nt(wrl[...], xh)


def _merge(o_list, l_list, yb, gt, x2, wa, wb, wo, gf, wrh, wrl):
    n, d = x2.shape
    aw = wa.shape[0]
    bw = wb.shape[0]
    tm = ROW_TILE
    row = lambda w: pl.BlockSpec((tm, w), lambda i: (i, 0))
    full = lambda a: pl.BlockSpec(a.shape, lambda i: (0,) * a.ndim)
    return pl.pallas_call(
        _merge_kernel,
        grid=(n // tm,),
        in_specs=[row(aw)] * 6 + [
            row(bw),
            pl.BlockSpec((tm, d), lambda i: (i, 0)),
            pl.BlockSpec((tm, d), lambda i: (i, 1)),
            row(d), full(wa), full(wb), full(wo), full(gf), full(wrh), full(wrl),
        ],
        out_specs=[row(d), row(d), pl.BlockSpec((N_EXPERTS, tm), lambda i: (0, i))],
        out_shape=[
            jax.ShapeDtypeStruct((n, d), F32),
            jax.ShapeDtypeStruct((n, d), BF16),
            jax.ShapeDtypeStruct((N_EXPERTS, n), F32),
        ],
        compiler_params=_params(("parallel",)),
    )(*o_list, *l_list, yb, gt, gt, x2, wa, wb, wo, gf, wrh, wrl)


def _route_kernel(lg_ref, bias_ref, idx_ref, w_ref, mask_ref):
    E = N_EXPERTS
    tn = lg_ref.shape[1]
    scores = jax.nn.sigmoid(lg_ref[...])
    sel = scores + bias_ref[...]
    sub = lax.broadcasted_iota(jnp.int32, (GROUP_SIZE, tn), 0).astype(F32)
    gs = []
    for grp in range(N_GROUPS):
        blk = sel[grp * GROUP_SIZE:(grp + 1) * GROUP_SIZE]
        m1 = jnp.max(blk, axis=0, keepdims=True)
        first = jnp.min(jnp.where(blk == m1, sub, float(GROUP_SIZE)), axis=0, keepdims=True)
        m2 = jnp.max(jnp.where(sub == first, -jnp.inf, blk), axis=0, keepdims=True)
        gs.append(m1 + m2)
    keep = []
    for a in range(N_GROUPS):
        rank = jnp.zeros((1, tn), jnp.int32)
        for o in range(N_GROUPS):
            if o == a:
                continue
            beats = (gs[o] >= gs[a]) if o < a else (gs[o] > gs[a])
            rank = rank + beats.astype(jnp.int32)
        keep.append(jnp.broadcast_to(rank < TOP_GROUPS, (GROUP_SIZE, tn)))
    emask = jnp.concatenate(keep, axis=0)
    cand = jnp.where(emask, sel, -jnp.inf)
    eid = lax.broadcasted_iota(jnp.int32, (E, tn), 0)
    eid_f = eid.astype(F32)
    rank = jnp.zeros((E, tn), jnp.int32)
    for o in range(E):
        other = cand[o:o + 1]
        beats = (other > cand) | ((other == cand) & (eid > o))
        rank = rank + beats.astype(jnp.int32)
    chosen = emask & (rank < TOP_K)
    wsel = jnp.where(chosen, scores, 0.0)
    wnorm = wsel / jnp.sum(wsel, axis=0, keepdims=True) * ROUTED_SCALE
    mask_ref[...] = chosen.astype(jnp.int32)
    for k in range(TOP_K):
        hit = chosen & (rank == k)
        idx_ref[k:k + 1, :] = jnp.sum(jnp.where(hit, eid_f, 0.0), axis=0, keepdims=True).astype(jnp.int32)
        w_ref[k:k + 1, :] = jnp.sum(jnp.where(hit, wnorm, 0.0), axis=0, keepdims=True)


def _route(logits_t, bias_col):
    e, n = logits_t.shape
    tn = ROW_TILE
    return pl.pallas_call(
        _route_kernel,
        grid=(n // tn,),
        in_specs=[pl.BlockSpec((e, tn), lambda i: (0, i)), pl.BlockSpec((e, 1), lambda i: (0, 0))],
        out_specs=[
            pl.BlockSpec((TOP_K, tn), lambda i: (0, i)),
            pl.BlockSpec((TOP_K, tn), lambda i: (0, i)),
            pl.BlockSpec((e, tn), lambda i: (0, i)),
        ],
        out_shape=[
            jax.ShapeDtypeStruct((TOP_K, n), jnp.int32),
            jax.ShapeDtypeStruct((TOP_K, n), F32),
            jax.ShapeDtypeStruct((e, n), jnp.int32),
        ],
        compiler_params=_params(("parallel",)),
    )(logits_t, bias_col)


def _moe_kernel(blk_e_ref, nused_ref, xs_ref, wg_ref, wu_ref, wd_ref, o_ref):
    i = pl.program_id(0)

    @pl.when(i < nused_ref[0])
    def _():
        x = xs_ref[...]
        hg = _dot(x, wg_ref[0])
        hid = (hg * jax.nn.sigmoid(hg)) * _dot(x, wu_ref[0])
        o_ref[...] = _dot(hid.astype(BF16), wd_ref[0]).astype(o_ref.dtype)

    @pl.when(i >= nused_ref[0])
    def _():
        o_ref[...] = jnp.zeros_like(o_ref)


def _moe_experts(blk_e, nused, xs, wg, wu, wd):
    p, d = xs.shape
    hid = wg.shape[2]
    grid_spec = pltpu.PrefetchScalarGridSpec(
        num_scalar_prefetch=2,
        grid=(p // MOE_ROWS,),
        in_specs=[
            pl.BlockSpec((MOE_ROWS, d), lambda i, be, nu: (i, 0)),
            pl.BlockSpec((1, d, hid), lambda i, be, nu: (be[i], 0, 0)),
            pl.BlockSpec((1, d, hid), lambda i, be, nu: (be[i], 0, 0)),
            pl.BlockSpec((1, hid, d), lambda i, be, nu: (be[i], 0, 0)),
        ],
        out_specs=pl.BlockSpec((MOE_ROWS, d), lambda i, be, nu: (i, 0)),
    )
    return pl.pallas_call(
        _moe_kernel,
        grid_spec=grid_spec,
        out_shape=jax.ShapeDtypeStruct((p, d), BF16),
        compiler_params=_params(("arbitrary",)),
    )(blk_e, nused, xs, wg, wu, wd)


def _final_kernel(h_ref, xn_ref, ys_ref, w_ref, p_ref, wsg, wsu, wsd, gp, wpg, wp, gfin, o_ref):
    xn = xn_ref[...]
    hg = _dot(xn, wsg[...])
    shared = _dot(((hg * jax.nn.sigmoid(hg)) * _dot(xn, wsu[...])).astype(BF16), wsd[...])
    w = w_ref[...]
    y = w[:, 0:1] * ys_ref[0].astype(F32)
    for k in range(1, TOP_K):
        y = y + w[:, k:k + 1] * ys_ref[k].astype(F32)
    h = h_ref[...] + (y + shared)
    gate = jax.nn.sigmoid(_dot(_rms(h, gp[...]).astype(BF16), wpg[...]))
    h = h + _dot(p_ref[...].astype(BF16), wp[...]) * gate
    o_ref[...] = _rms(h, gfin[...])


def _final(h1, xn2, ysg, w_nk, p2, wsg, wsu, wsd, gp, wpg, wp, gfin):
    n, d = h1.shape
    tm = ROW_TILE
    row = lambda w: pl.BlockSpec((tm, w), lambda i: (i, 0))
    full = lambda a: pl.BlockSpec(a.shape, lambda i: (0,) * a.ndim)
    return pl.pallas_call(
        _final_kernel,
        grid=(n // tm,),
        in_specs=[row(d), row(d), pl.BlockSpec((TOP_K, tm, d), lambda i: (0, i, 0)), row(TOP_K), row(p2.shape[1]),
                  full(wsg), full(wsu), full(wsd), full(gp), full(wpg), full(wp), full(gfin)],
        out_specs=row(d),
        out_shape=jax.ShapeDtypeStruct((n, d), F32),
        compiler_params=_params(("parallel",)),
    )(h1, xn2, ysg, w_nk, p2, wsg, wsu, wsd, gp, wpg, wp, gfin)


def _pad_heads(w):
    d, c = w.shape
    nh = c // HEAD_DIM
    w = w.reshape(d, nh, HEAD_DIM)
    return jnp.pad(w, ((0, 0), (0, 0), (0, LANE - HEAD_DIM))).reshape(d, nh * LANE)


def _alibi_slopes(n):
    return 2.0 ** (-8.0 * jnp.arange(1, n + 1, dtype=jnp.float32) / n)


def kernel(x, p, g_mix, w_in, cmp_pos_k, cmp_w1_k, cmp_b1_k, cmp_w2_k, cmp_pos_v, cmp_w1_v, cmp_b1_v, cmp_w2_v, w_branch_a, w_branch_b, w_out, g_ffn, w_router, router_bias, w_gate, w_up, w_down, ws_gate, ws_up, ws_down, g_ple, w_ple_gate, w_ple, g_final):
    B, S, D = x.shape
    N = B * S
    G, R = NSA_KV_HEADS, NSA_REP
    assert w_in.shape[0] == 1 and p.shape[0] == 1
    assert S % (DIL_BLOCK * DIL_PATTERNS[-1][1]) == 0 and S % NSA_TK == 0 and S >= WIN_SIZE + NSA_Q
    assert N % ROW_TILE == 0
    slopes = _alibi_slopes(DIL_HEADS + NSA_HEADS)
    slopes_a, slopes_b = slopes[0::2], slopes[1::2]

    offs = np.concatenate([[0], np.cumsum(IN_SIZES)])
    parts = [w_in[0][:, offs[i]:offs[i + 1]] for i in range(len(IN_SIZES))]
    w_qa, w_ka, w_va, w_qb, w_kc, w_vc, w_ks, w_vs, w_kw, w_vw, w_gn, w_ga, w_gb = parts
    w_act = jnp.concatenate([w_qa, w_ka, w_va, jnp.zeros((D, 512), F32)]
                            + [_pad_heads(t) for t in (w_qb, w_kc, w_vc, w_ks, w_vs, w_kw, w_vw)],
                            axis=1).astype(BF16)
    act_w = w_act.shape[1]
    nsa_col0 = 2048 // LANE
    w_gn = w_gn.reshape(D, 3, G, R).transpose(0, 2, 1, 3).reshape(D, G, 3 * R)
    w_gn = jnp.pad(w_gn, ((0, 0), (0, 0), (0, LANE - 3 * R))).reshape(D, G * LANE)
    w_gate_in = jnp.concatenate([w_ga, w_gb, w_gn, jnp.zeros((D, 256), F32)], axis=1).astype(BF16)
    gate_w = w_gate_in.shape[1]

    x2 = x.reshape(N, D)
    act = _inproj(x2, g_mix, w_act, BF16, 512)
    gt = _inproj(x2, g_mix, w_gate_in, F32, 512)

    o_list, l_list = [], []
    for _, dil in DIL_PATTERNS:
        o, l = _dilated(act.reshape(B, S // dil, dil * act_w), slopes_a, dil, B, S, act_w)
        o_list.append(o.reshape(N, DIL_HEADS * HEAD_DIM))
        l_list.append(l.reshape(N, DIL_HEADS * HEAD_DIM))

    nchunk = S // CMP_STRIDE
    c0 = nsa_col0 * LANE + NSA_HEADS * LANE
    raw = act[:, c0:c0 + 2 * G * LANE].reshape(B, nchunk, CMP_STRIDE, 2 * G, LANE)[..., :HEAD_DIM]
    xc = raw.transpose(0, 3, 1, 2, 4).reshape(B, 2 * G, nchunk, CMP_STRIDE * HEAD_DIM)
    pos = jnp.stack([cmp_pos_k[0].reshape(1, -1), cmp_pos_v[0].reshape(1, -1)])
    w1 = jnp.stack([cmp_w1_k[0], cmp_w1_v[0]]).astype(BF16)
    b1 = jnp.stack([cmp_b1_k[0][None], cmp_b1_v[0][None]])
    w2 = jnp.pad(jnp.stack([cmp_w2_k[0], cmp_w2_v[0]]), ((0, 0), (0, 0), (0, LANE - HEAD_DIM))).astype(BF16)
    kcv = _compress(xc, pos, w1, b1, w2)
    kcv = kcv.reshape(B, 2, G, nchunk, LANE).transpose(0, 2, 1, 3, 4).reshape(B * G, 2, nchunk, LANE)

    yb = _nsa(slopes_b, act.reshape(B, S, act_w), gt.reshape(B, S, gate_w), kcv, B, S, nsa_col0, 2 * D // LANE)
    yb = yb.reshape(N, NSA_HEADS * LANE)

    wb_pad = jnp.pad(w_branch_b[0].reshape(NSA_HEADS, HEAD_DIM, D),
                     ((0, 0), (0, LANE - HEAD_DIM), (0, 0))).reshape(NSA_HEADS * LANE, D)
    wr_t = w_router[0].T
    wr_hi = wr_t.astype(BF16)
    wr_lo = (wr_t - wr_hi.astype(F32)).astype(BF16)
    h1, xn2, logits_t = _merge(o_list, l_list, yb, gt, x2, w_branch_a[0].astype(BF16), wb_pad.astype(BF16),
                               w_out[0].astype(BF16), g_ffn, wr_hi, wr_lo)

    idx_t, wsel_t, mask_t = _route(logits_t, router_bias[0][:, None])
    counts = jnp.sum(mask_t, axis=1)
    padded = (counts + MOE_ROWS - 1) // MOE_ROWS * MOE_ROWS
    pend = jnp.cumsum(padded)
    pstart = pend - padded
    rank_t = jnp.cumsum(mask_t, axis=1) - mask_t
    pos_t = jnp.take_along_axis(rank_t + pstart[:, None], idx_t, axis=0)
    NK = N * TOP_K
    nblk = (NK + N_EXPERTS * (MOE_ROWS - 1) + MOE_ROWS - 1) // MOE_ROWS
    P = nblk * MOE_ROWS
    tok = jnp.broadcast_to(jnp.arange(N, dtype=jnp.int32)[None, :], (TOP_K, N))
    row_tok = jnp.zeros((P,), jnp.int32).at[pos_t.reshape(-1)].set(tok.reshape(-1), unique_indices=True)
    blk_e = jnp.minimum(jnp.searchsorted(pend, jnp.arange(nblk, dtype=jnp.int32) * MOE_ROWS, side='right'),
                        N_EXPERTS - 1).astype(jnp.int32)
    nused = (pend[-1] // MOE_ROWS).astype(jnp.int32).reshape(1)
    xs = xn2[row_tok]
    ys = _moe_experts(blk_e, nused, xs, w_gate[0].astype(BF16), w_up[0].astype(BF16), w_down[0].astype(BF16))
    out = _final(h1, xn2, ys[pos_t], wsel_t.T, p[0].reshape(N, -1), ws_gate[0].astype(BF16), ws_up[0].astype(BF16),
                 ws_down[0].astype(BF16), g_ple, w_ple_gate[0].astype(BF16), w_ple[0].astype(BF16),
                 g_final[None, :])
    return out.reshape(B, S, D)
```

```python
import functools

import numpy as np
import jax
import jax.numpy as jnp
from jax import lax
from jax.experimental import pallas as pl
from jax.experimental.pallas import tpu as pltpu

F32 = jnp.float32
BF16 = jnp.bfloat16

HEAD_DIM = 64
LANE = 128
RMS_EPS = 1e-6
DIL_HEADS = 8
DIL_PATTERNS = ((128, 1), (512, 4), (2048, 16))
DIL_BLOCK = 128
NSA_HEADS = 8
NSA_KV_HEADS = 2
NSA_REP = NSA_HEADS // NSA_KV_HEADS
CMP_BLOCK = 32
CMP_STRIDE = 16
CMP_HIDDEN = 256
SLC_BLOCK = 64
SLC_TOP_N = 16
SLC_FORCE = 1e9
WIN_SIZE = 512
N_EXPERTS = 64
TOP_K = 8
N_GROUPS = 8
TOP_GROUPS = 4
GROUP_SIZE = N_EXPERTS // N_GROUPS
ROUTED_SCALE = 2.5
IN_SIZES = (512, 512, 512, 512, 128, 128, 128, 128, 128, 128, 24, 1024, 1024)

NEG = -1e30
NT_DIMS = (((1,), (1,)), ((), ()))

ROW_TILE = 512
NSA_Q = 128
NSA_TK = 512
MOE_ROWS = 512
VMEM_LIMIT = 48 * 1024 * 1024


def _dot(a, b):
    return jnp.dot(a, b, preferred_element_type=F32)


def _dot_nt(a, b):
    return lax.dot_general(a, b, NT_DIMS, preferred_element_type=F32)


def _split3(x):
    hi = x.astype(BF16)
    r1 = x - hi.astype(F32)
    mid = r1.astype(BF16)
    lo = (r1 - mid.astype(F32)).astype(BF16)
    return hi, mid, lo


def _rms(x, g):
    r = lax.rsqrt(jnp.mean(x * x, axis=-1, keepdims=True) + RMS_EPS)
    return (x * r) * g


def _params(sem):
    return pltpu.CompilerParams(dimension_semantics=sem, vmem_limit_bytes=VMEM_LIMIT)


def _inproj_kernel(x_ref, g_ref, w_ref, o_ref, *rest, dils, n_dilated):
    dil_refs, (xn_ref, res_ref) = rest[:len(dils)], rest[len(dils):]
    j = pl.program_id(1)

    @pl.when(j == 0)
    def _():
        xn_ref[...] = _rms(x_ref[...], g_ref[...]).astype(BF16)

    res = _dot(xn_ref[...], w_ref[...])
    o_ref[...] = res.astype(o_ref.dtype)

    if dils:
        @pl.when(j < n_dilated)
        def _():
            tn = res.shape[1]
            for c in range(tn // LANE):
                res_ref[c] = res[:, c * LANE:(c + 1) * LANE]
            for d_ref, dil in zip(dil_refs, dils):
                for r in range(dil):
                    for c in range(tn // LANE):
                        piece = res_ref[c, pl.ds(r, ROW_TILE // dil, stride=dil), :]
                        d_ref[:, r * tn + c * LANE:r * tn + (c + 1) * LANE] = piece.astype(d_ref.dtype)


def _inproj(x2, g, w, out_dtype, tn, dils=(), n_dilated=0):
    n, d = x2.shape
    cols = w.shape[1]
    last = max(n_dilated - 1, 0)
    out_specs = [pl.BlockSpec((ROW_TILE, tn), lambda i, j: (i, j))]
    out_shape = [jax.ShapeDtypeStruct((n, cols), out_dtype)]
    for dil in dils:
        out_specs.append(pl.BlockSpec((ROW_TILE // dil, dil * tn), lambda i, j: (i, jnp.minimum(j, last))))
        out_shape.append(jax.ShapeDtypeStruct((n // dil, n_dilated * dil * tn), out_dtype))
    return pl.pallas_call(
        functools.partial(_inproj_kernel, dils=dils, n_dilated=n_dilated),
        grid=(n // ROW_TILE, cols // tn),
        in_specs=[
            pl.BlockSpec((ROW_TILE, d), lambda i, j: (i, 0)),
            pl.BlockSpec((1, d), lambda i, j: (0, 0)),
            pl.BlockSpec((d, tn), lambda i, j: (0, j)),
        ],
        out_specs=out_specs,
        out_shape=out_shape,
        scratch_shapes=[pltpu.VMEM((ROW_TILE, d), BF16), pltpu.VMEM((tn // LANE, ROW_TILE, LANE), F32)],
        compiler_params=_params(("parallel", "arbitrary")),
    )(x2, g, w)


def _dil_kernel(slope_ref, q_ref, kp_ref, kc_ref, vp_ref, vc_ref, o_ref, l_ref, *, dil):
    i = pl.program_id(1)
    res = pl.program_id(2)
    blk = DIL_BLOCK
    out_rows = pl.ds(res, blk, stride=dil) if dil > 1 else slice(None)
    qi = lax.broadcasted_iota(jnp.int32, (blk, 2 * blk), 0)
    kj = lax.broadcasted_iota(jnp.int32, (blk, 2 * blk), 1)
    dist = (qi + blk - kj).astype(F32)
    valid = (dist >= 0) & (dist <= blk) & ((kj >= blk) | (i > 0))
    lane = lax.broadcasted_iota(jnp.int32, (blk, LANE), 1)
    lo = lane < HEAD_DIM
    scale = HEAD_DIM ** -0.5
    for hp in range(DIL_HEADS // 2):
        sl = slice(hp * LANE, (hp + 1) * LANE)
        q2 = q_ref[0, :, sl]
        k2 = jnp.concatenate([kp_ref[0, :, sl], kc_ref[0, :, sl]], axis=0)
        v2 = jnp.concatenate([vp_ref[0, :, sl], vc_ref[0, :, sl]], axis=0)
        outs, lses = [], []
        for hh in range(2):
            keep = lo if hh == 0 else jnp.logical_not(lo)
            qm = jnp.where(keep, q2, jnp.zeros_like(q2))
            slope = slope_ref[2 * hp + hh] * float(dil)
            s = _dot_nt(qm, k2) * scale - slope * dist
            s = jnp.where(valid, s, NEG)
            m = jnp.max(s, axis=-1, keepdims=True)
            e = jnp.exp(s - m)
            den = jnp.sum(e, axis=-1, keepdims=True)
            outs.append(_dot(e.astype(BF16), v2) / den)
            lses.append(jnp.broadcast_to(m + jnp.log(den), (blk, LANE)))
        o_ref[hp, 0, out_rows, :] = jnp.where(lo, outs[0], outs[1])
        l_ref[hp, 0, out_rows, :] = jnp.where(lo, lses[0], lses[1])


def _dilated(src3, slopes_a, dil, b, s):
    nblk = s // dil // DIL_BLOCK
    aw = DIL_HEADS * HEAD_DIM
    npair = DIL_HEADS // 2
    blk = (1, DIL_BLOCK, aw)

    def spec(which, prev):
        if prev:
            return pl.BlockSpec(blk, lambda bb, i, r: (bb, jnp.maximum(i - 1, 0), which * dil + r))
        return pl.BlockSpec(blk, lambda bb, i, r: (bb, i, which * dil + r))

    out_spec = pl.BlockSpec((npair, 1, DIL_BLOCK * dil, LANE), lambda bb, i, r: (0, bb, i, 0))
    return pl.pallas_call(
        functools.partial(_dil_kernel, dil=dil),
        grid=(b, nblk, dil),
        in_specs=[
            pl.BlockSpec(memory_space=pltpu.SMEM),
            spec(0, False), spec(1, True), spec(1, False), spec(2, True), spec(2, False),
        ],
        out_specs=[out_spec, out_spec],
        out_shape=[jax.ShapeDtypeStruct((npair, b, s, LANE), F32)] * 2,
        compiler_params=_params(("parallel", "arbitrary", "arbitrary")),
    )(slopes_a, src3, src3, src3, src3, src3)


def _cmp_kernel(x_ref, pos_ref, w1_ref, b1_ref, w2_ref, o_ref):
    half = x_ref.shape[-1]
    x = x_ref[0, 0]
    w1 = w1_ref[0]
    first = _dot(x, w1[:half])
    second = _dot(x, w1[half:])
    nrow = x.shape[0]
    pre = first + pltpu.roll(second, nrow - 1, 0)
    posb = _dot(jnp.broadcast_to(pos_ref[0], (8, 2 * half)).astype(BF16), w1)[0:1]
    pre = pre + posb + b1_ref[0]
    o_ref[0, 0] = _dot(jax.nn.gelu(pre).astype(BF16), w2_ref[0]).astype(o_ref.dtype)


def _compress(xc, pos, w1, b1, w2):
    b, four, nchunk, width = xc.shape
    return pl.pallas_call(
        _cmp_kernel,
        grid=(b, four),
        in_specs=[
            pl.BlockSpec((1, 1, nchunk, width), lambda bb, j: (bb, j, 0, 0)),
            pl.BlockSpec((1, 1, 2 * width), lambda bb, j: (j // 2, 0, 0)),
            pl.BlockSpec((1, 2 * width, CMP_HIDDEN), lambda bb, j: (j // 2, 0, 0)),
            pl.BlockSpec((1, 1, CMP_HIDDEN), lambda bb, j: (j // 2, 0, 0)),
            pl.BlockSpec((1, CMP_HIDDEN, LANE), lambda bb, j: (j // 2, 0, 0)),
        ],
        out_specs=pl.BlockSpec((1, 1, nchunk, LANE), lambda bb, j: (bb, j, 0, 0)),
        out_shape=jax.ShapeDtypeStruct((b, four, nchunk, LANE), BF16),
        compiler_params=_params(("parallel", "parallel")),
    )(xc, pos, w1, b1, w2)


POS_LANE = HEAD_DIM
POS_SPLIT = 64
MASKED = -1e30
ROW_FLOOR = -1e29


def _pos_columns(pos):
    out = np.zeros((pos.shape[0], LANE), np.float32)
    out[:, POS_LANE] = pos // POS_SPLIT
    out[:, POS_LANE + 1] = pos % POS_SPLIT
    out[:, POS_LANE + 2] = 1.0
    out[:, POS_LANE + 3] = 1.0
    return jnp.asarray(out, BF16)


def _nsa_kernel(slope_ref, q_ref, kcv_ref, ks_ref, vs_ref, kw_ref, vw_ref, gt_ref, ov_ref, kpa_ref,
                cpa_ref, oh_ref, o_ref, kaug_ref, sel_ref, m_ref, l_ref, acc_ref, *, seq):
    g = pl.program_id(1)
    c = pl.program_id(2)
    C = NSA_Q
    R = NSA_REP
    TK = NSA_TK
    rows = R * C
    t0 = c * C
    scale = HEAD_DIM ** -0.5
    nb = seq // SLC_BLOCK
    ncmp = seq // CMP_STRIDE
    n_sel = min(SLC_TOP_N, nb)

    @pl.when(c == 0)
    def _():
        kaug_ref[:, 0:LANE] = ks_ref[0] + kpa_ref[...]
        kaug_ref[:, LANE:2 * LANE] = oh_ref[...]

    row = lax.broadcasted_iota(jnp.int32, (rows, 1), 0)
    ti = t0 + row % C
    slope = jnp.zeros((rows, 1), F32)
    for r in range(R):
        slope = jnp.where(row // C == r, slope_ref[g * R + r], slope)
    lane = lax.broadcasted_iota(jnp.int32, (rows, LANE), 1)
    thi = (ti // POS_SPLIT).astype(F32)
    tlo = (ti % POS_SPLIT).astype(F32)
    qcols = jnp.where(lane == POS_LANE, POS_SPLIT * slope,
                      jnp.where(lane == POS_LANE + 1, slope,
                                jnp.where(lane == POS_LANE + 2, -POS_SPLIT * slope * thi,
                                          jnp.where(lane == POS_LANE + 3, -slope * tlo, 0.0))))
    q4 = jnp.concatenate([q_ref[0, :, r * LANE:(r + 1) * LANE] for r in range(R)], axis=0)
    q4 = (q4.astype(F32) * scale + qcols).astype(BF16)
    tc = t0 + lax.broadcasted_iota(jnp.int32, (C, 1), 0)

    def tile_rows(x):
        return jnp.concatenate([x] * R, axis=0)

    kc = kcv_ref[0, 0] + cpa_ref[...]
    vc = kcv_ref[0, 1]
    ci = lax.broadcasted_iota(jnp.int32, (1, ncmp), 1)
    vis = (ci * CMP_STRIDE + (CMP_BLOCK - 1) <= tc) & (ci < ncmp - 1)
    s = _dot_nt(q4, kc) + tile_rows(jnp.where(vis, 0.0, MASKED))
    m = jnp.maximum(jnp.max(s, axis=-1, keepdims=True), ROW_FLOOR)
    e = jnp.exp(s - m)
    den = jnp.sum(e, axis=-1, keepdims=True)
    pc = e * (1.0 / jnp.where(den > 0, den, 1.0))
    o_cmp = _dot(pc.astype(BF16), vc)
    pcsum = pc[0:C]
    for r in range(1, R):
        pcsum = pcsum + pc[r * C:(r + 1) * C]
    ov = ov_ref[...]
    p_slc = sum(_dot_nt(ov, part) for part in _split3(pcsum))

    j = lax.broadcasted_iota(jnp.int32, (LANE, C), 0)
    t = t0 + lax.broadcasted_iota(jnp.int32, (LANE, C), 1)
    cur = t // SLC_BLOCK
    forced = (j == 0) | (j == cur) | (j == cur - 1)
    future = j * SLC_BLOCK > t
    work = jnp.where(forced, SLC_FORCE, jnp.where(future, -SLC_FORCE, p_slc))
    work = jnp.where(j < nb, work, -jnp.inf)
    sel = jnp.zeros((LANE, C), F32)
    jf = j.astype(F32)
    for _ in range(n_sel):
        mx = jnp.max(work, axis=0, keepdims=True)
        first = jnp.min(jnp.where(work == mx, jf, float(LANE)), axis=0, keepdims=True)
        pick = jf == first
        sel = jnp.where(pick, 1.0, sel)
        work = jnp.where(pick, -jnp.inf, work)
    sel = jnp.where(future, 0.0, sel)
    sel_ref[...] = sel
    sel_bias = jnp.where(sel.T > 0.5, 0.0, MASKED).astype(BF16)
    q_aug = jnp.concatenate([q4, tile_rows(sel_bias)], axis=1)

    m_ref[...] = jnp.full((rows, 1), ROW_FLOOR, F32)
    l_ref[...] = jnp.zeros((rows, 1), F32)
    acc_ref[...] = jnp.zeros((rows, LANE), F32)
    kiota = lax.broadcasted_iota(jnp.int32, (1, TK), 1)

    def tile_update(kt, causal):
        start = pl.multiple_of(kt * TK, TK)
        s2 = _dot_nt(q_aug, kaug_ref[pl.ds(start, TK), :])
        if causal:
            s2 = s2 + tile_rows(jnp.where(start + kiota <= tc, 0.0, MASKED))
        m_i = m_ref[...]
        m_n = jnp.maximum(m_i, jnp.max(s2, axis=-1, keepdims=True))
        alpha = jnp.exp(m_i - m_n)
        p2 = jnp.exp(s2 - m_n)
        l_ref[...] = alpha * l_ref[...] + jnp.sum(p2, axis=-1, keepdims=True)
        acc_ref[...] = alpha * acc_ref[...] + _dot(p2.astype(BF16), vs_ref[0, pl.ds(start, TK), :])
        m_ref[...] = m_n

    blocks_per_tile = TK // SLC_BLOCK

    def slc_body(kt, carry):
        picked = sel_ref[pl.ds(pl.multiple_of(kt * blocks_per_tile, blocks_per_tile), blocks_per_tile), :]

        @pl.when(jnp.max(picked) > 0.0)
        def _():
            tile_update(kt, False)

        return carry

    last = (t0 + C - 1) // TK
    lax.fori_loop(0, last, slc_body, 0)
    tile_update(last, True)
    o_slc = acc_ref[...] / l_ref[...]

    span = WIN_SIZE + C
    ws = pl.multiple_of(jnp.maximum(t0 - WIN_SIZE, 0), C)
    k_w = kw_ref[0, pl.ds(ws, span), :] + kpa_ref[pl.ds(ws, span), :]
    v_w = vw_ref[0, pl.ds(ws, span), :]
    wpos = ws + lax.broadcasted_iota(jnp.int32, (1, span), 1)
    okw = (wpos <= tc) & (wpos >= tc - (WIN_SIZE - 1))
    s3 = _dot_nt(q4, k_w) + tile_rows(jnp.where(okw, 0.0, MASKED))
    m3 = jnp.max(s3, axis=-1, keepdims=True)
    e3 = jnp.exp(s3 - m3)
    o_win = _dot(e3.astype(BF16), v_w) / jnp.sum(e3, axis=-1, keepdims=True)

    gates = jax.nn.sigmoid(gt_ref[0])
    for r in range(R):
        rs = slice(r * C, (r + 1) * C)
        out = (gates[:, r:r + 1] * o_cmp[rs] + gates[:, R + r:R + r + 1] * o_slc[rs]
               + gates[:, 2 * R + r:2 * R + r + 1] * o_win[rs])
        o_ref[0, :, r * LANE:(r + 1) * LANE] = out.astype(o_ref.dtype)


def _nsa(slopes_b, pa3, gt3, kcv, b, s, col0, gcol0):
    C = NSA_Q
    R = NSA_REP
    G = NSA_KV_HEADS
    nb = s // SLC_BLOCK
    ncmp = s // CMP_STRIDE
    assert nb <= LANE
    qc0 = col0 // R
    kv0 = col0 + NSA_HEADS
    rows = R * C

    ci = np.arange(ncmp)[None, :]
    sj = np.arange(LANE)[:, None]
    ov = ((ci * CMP_STRIDE < (sj + 1) * SLC_BLOCK) & (ci * CMP_STRIDE + CMP_BLOCK > sj * SLC_BLOCK)
          & (ci < ncmp - 1) & (sj < nb))
    ov = jnp.asarray(ov, BF16)
    kpa = _pos_columns(np.arange(s))
    cpa = _pos_columns(np.arange(ncmp) * CMP_STRIDE + (CMP_BLOCK - 1))
    oh = jnp.asarray(np.arange(s)[:, None] // SLC_BLOCK == np.arange(LANE)[None, :], BF16)

    def kv_spec(which):
        return pl.BlockSpec((1, s, LANE), lambda bb, g, c: (bb, 0, kv0 + which * G + g))

    const = lambda a: pl.BlockSpec(a.shape, lambda bb, g, c: (0,) * a.ndim)
    return pl.pallas_call(
        functools.partial(_nsa_kernel, seq=s),
        grid=(b, G, s // C),
        in_specs=[
            pl.BlockSpec(memory_space=pltpu.SMEM),
            pl.BlockSpec((1, C, R * LANE), lambda bb, g, c: (bb, c, qc0 + g)),
            pl.BlockSpec((1, 2, ncmp, LANE), lambda bb, g, c: (bb * G + g, 0, 0, 0)),
            kv_spec(2), kv_spec(3), kv_spec(4), kv_spec(5),
            pl.BlockSpec((1, C, LANE), lambda bb, g, c: (bb, c, gcol0 + g)),
            const(ov), const(kpa), const(cpa), const(oh),
        ],
        out_specs=pl.BlockSpec((1, C, R * LANE), lambda bb, g, c: (bb, c, g)),
        out_shape=jax.ShapeDtypeStruct((b, s, NSA_HEADS * LANE), BF16),
        scratch_shapes=[
            pltpu.VMEM((s, 2 * LANE), BF16),
            pltpu.VMEM((LANE, C), F32),
            pltpu.VMEM((rows, 1), F32),
            pltpu.VMEM((rows, 1), F32),
            pltpu.VMEM((rows, LANE), F32),
        ],
        compiler_params=_params(("parallel", "parallel", "arbitrary")),
    )(slopes_b, pa3, kcv, pa3, pa3, pa3, pa3, gt3, ov, kpa, cpa, oh)


def _merge_kernel(o1, o2, o3, l1, l2, l3, yb, ga, gb, x_ref, wa, wb, wo, gf, wrh, wrl,
                  h_ref, xn_ref, lg_ref):
    def pairs(ref):
        return jnp.concatenate([ref[h] for h in range(ref.shape[0])], axis=1)

    a1, a2, a3 = pairs(l1), pairs(l2), pairs(l3)
    mx = jnp.maximum(jnp.maximum(a1, a2), a3)
    e1, e2, e3 = jnp.exp(a1 - mx), jnp.exp(a2 - mx), jnp.exp(a3 - mx)
    tot = e1 + e2 + e3
    ya = (e1 / tot) * pairs(o1) + (e2 / tot) * pairs(o2) + (e3 / tot) * pairs(o3)
    merged = (jax.nn.sigmoid(ga[...]) * _dot(ya.astype(BF16), wa[...])
              + jax.nn.sigmoid(gb[...]) * _dot(yb[...], wb[...]))
    h = x_ref[...] + _dot(merged.astype(BF16), wo[...])
    h_ref[...] = h
    xn = _rms(h, gf[...])
    xn_ref[...] = xn.astype(BF16)
    xh = xn.astype(BF16)
    xl = (xn - xh.astype(F32)).astype(BF16)
    lg_ref[...] = _dot_nt(wrh[...], xh) + _dot_nt(wrh[...], xl) + _dot_nt(wrl[...], xh)


def _merge(o_list, l_list, yb, gt, x2, wa, wb, wo, gf, wrh, wrl):
    n, d = x2.shape
    aw = wa.shape[0]
    bw = wb.shape[0]
    tm = ROW_TILE
    row = lambda w: pl.BlockSpec((tm, w), lambda i: (i, 0))
    full = lambda a: pl.BlockSpec(a.shape, lambda i: (0,) * a.ndim)
    return pl.pallas_call(
        _merge_kernel,
        grid=(n // tm,),
        in_specs=[pl.BlockSpec((aw // LANE, tm, LANE), lambda i: (0, i, 0))] * 6 + [
            row(bw),
            pl.BlockSpec((tm, d), lambda i: (i, 0)),
            pl.BlockSpec((tm, d), lambda i: (i, 1)),
            row(d), full(wa), full(wb), full(wo), full(gf), full(wrh), full(wrl),
        ],
        out_specs=[row(d), row(d), pl.BlockSpec((N_EXPERTS, tm), lambda i: (0, i))],
        out_shape=[
            jax.ShapeDtypeStruct((n, d), F32),
            jax.ShapeDtypeStruct((n, d), BF16),
            jax.ShapeDtypeStruct((N_EXPERTS, n), F32),
        ],
        compiler_params=_params(("parallel",)),
    )(*o_list, *l_list, yb, gt, gt, x2, wa, wb, wo, gf, wrh, wrl)


def _route_kernel(lg_ref, bias_ref, idx_ref, w_ref, mask_ref):
    E = N_EXPERTS
    tn = lg_ref.shape[1]
    scores = jax.nn.sigmoid(lg_ref[...])
    sel = scores + bias_ref[...]
    sub = lax.broadcasted_iota(jnp.int32, (GROUP_SIZE, tn), 0).astype(F32)
    gs = []
    for grp in range(N_GROUPS):
        blk = sel[grp * GROUP_SIZE:(grp + 1) * GROUP_SIZE]
        m1 = jnp.max(blk, axis=0, keepdims=True)
        first = jnp.min(jnp.where(blk == m1, sub, float(GROUP_SIZE)), axis=0, keepdims=True)
        m2 = jnp.max(jnp.where(sub == first, -jnp.inf, blk), axis=0, keepdims=True)
        gs.append(m1 + m2)
    keep = []
    for a in range(N_GROUPS):
        rank = jnp.zeros((1, tn), jnp.int32)
        for o in range(N_GROUPS):
            if o == a:
                continue
            beats = (gs[o] >= gs[a]) if o < a else (gs[o] > gs[a])
            rank = rank + beats.astype(jnp.int32)
        keep.append(jnp.broadcast_to(rank < TOP_GROUPS, (GROUP_SIZE, tn)))
    emask = jnp.concatenate(keep, axis=0)
    cand = jnp.where(emask, sel, -jnp.inf)
    eid = lax.broadcasted_iota(jnp.int32, (E, tn), 0)
    eid_f = eid.astype(F32)
    rank = jnp.zeros((E, tn), jnp.int32)
    for o in range(E):
        other = cand[o:o + 1]
        beats = (other > cand) | ((other == cand) & (eid > o))
        rank = rank + beats.astype(jnp.int32)
    chosen = emask & (rank < TOP_K)
    wsel = jnp.where(chosen, scores, 0.0)
    wnorm = wsel / jnp.sum(wsel, axis=0, keepdims=True) * ROUTED_SCALE
    mask_ref[...] = chosen.astype(jnp.int32)
    for k in range(TOP_K):
        hit = chosen & (rank == k)
        idx_ref[k:k + 1, :] = jnp.sum(jnp.where(hit, eid_f, 0.0), axis=0, keepdims=True).astype(jnp.int32)
        w_ref[k:k + 1, :] = jnp.sum(jnp.where(hit, wnorm, 0.0), axis=0, keepdims=True)


def _route(logits_t, bias_col):
    e, n = logits_t.shape
    tn = ROW_TILE
    return pl.pallas_call(
        _route_kernel,
        grid=(n // tn,),
        in_specs=[pl.BlockSpec((e, tn), lambda i: (0, i)), pl.BlockSpec((e, 1), lambda i: (0, 0))],
        out_specs=[
            pl.BlockSpec((TOP_K, tn), lambda i: (0, i)),
            pl.BlockSpec((TOP_K, tn), lambda i: (0, i)),
            pl.BlockSpec((e, tn), lambda i: (0, i)),
        ],
        out_shape=[
            jax.ShapeDtypeStruct((TOP_K, n), jnp.int32),
            jax.ShapeDtypeStruct((TOP_K, n), F32),
            jax.ShapeDtypeStruct((e, n), jnp.int32),
        ],
        compiler_params=_params(("parallel",)),
    )(logits_t, bias_col)


def _moe_kernel(blk_e_ref, nused_ref, xs_ref, wg_ref, wu_ref, wd_ref, o_ref):
    i = pl.program_id(0)

    @pl.when(i < nused_ref[0])
    def _():
        x = xs_ref[...]
        hg = _dot(x, wg_ref[0])
        hid = (hg * jax.nn.sigmoid(hg)) * _dot(x, wu_ref[0])
        o_ref[...] = _dot(hid.astype(BF16), wd_ref[0]).astype(o_ref.dtype)

    @pl.when(i >= nused_ref[0])
    def _():
        o_ref[...] = jnp.zeros_like(o_ref)


def _moe_experts(blk_e, nused, xs, wg, wu, wd):
    p, d = xs.shape
    hid = wg.shape[2]
    grid_spec = pltpu.PrefetchScalarGridSpec(
        num_scalar_prefetch=2,
        grid=(p // MOE_ROWS,),
        in_specs=[
            pl.BlockSpec((MOE_ROWS, d), lambda i, be, nu: (i, 0)),
            pl.BlockSpec((1, d, hid), lambda i, be, nu: (be[i], 0, 0)),
            pl.BlockSpec((1, d, hid), lambda i, be, nu: (be[i], 0, 0)),
            pl.BlockSpec((1, hid, d), lambda i, be, nu: (be[i], 0, 0)),
        ],
        out_specs=pl.BlockSpec((MOE_ROWS, d), lambda i, be, nu: (i, 0)),
    )
    return pl.pallas_call(
        _moe_kernel,
        grid_spec=grid_spec,
        out_shape=jax.ShapeDtypeStruct((p, d), BF16),
        compiler_params=_params(("arbitrary",)),
    )(blk_e, nused, xs, wg, wu, wd)


def _final_kernel(h_ref, xn_ref, ys_ref, w_ref, p_ref, wsg, wsu, wsd, gp, wpg, wp, gfin, o_ref):
    xn = xn_ref[...]
    hg = _dot(xn, wsg[...])
    shared = _dot(((hg * jax.nn.sigmoid(hg)) * _dot(xn, wsu[...])).astype(BF16), wsd[...])
    w = w_ref[...]
    y = w[:, 0:1] * ys_ref[0].astype(F32)
    for k in range(1, TOP_K):
        y = y + w[:, k:k + 1] * ys_ref[k].astype(F32)
    h = h_ref[...] + (y + shared)
    gate = jax.nn.sigmoid(_dot(_rms(h, gp[...]).astype(BF16), wpg[...]))
    h = h + _dot(p_ref[...].astype(BF16), wp[...]) * gate
    o_ref[...] = _rms(h, gfin[...])


def _final(h1, xn2, ysg, w_nk, p2, wsg, wsu, wsd, gp, wpg, wp, gfin):
    n, d = h1.shape
    tm = ROW_TILE
    row = lambda w: pl.BlockSpec((tm, w), lambda i: (i, 0))
    full = lambda a: pl.BlockSpec(a.shape, lambda i: (0,) * a.ndim)
    return pl.pallas_call(
        _final_kernel,
        grid=(n // tm,),
        in_specs=[row(d), row(d), pl.BlockSpec((TOP_K, tm, d), lambda i: (0, i, 0)), row(TOP_K), row(p2.shape[1]),
                  full(wsg), full(wsu), full(wsd), full(gp), full(wpg), full(wp), full(gfin)],
        out_specs=row(d),
        out_shape=jax.ShapeDtypeStruct((n, d), F32),
        compiler_params=_params(("parallel",)),
    )(h1, xn2, ysg, w_nk, p2, wsg, wsu, wsd, gp, wpg, wp, gfin)


def _pad_heads(w):
    d, c = w.shape
    nh = c // HEAD_DIM
    w = w.reshape(d, nh, HEAD_DIM)
    return jnp.pad(w, ((0, 0), (0, 0), (0, LANE - HEAD_DIM))).reshape(d, nh * LANE)


def _alibi_slopes(n):
    return 2.0 ** (-8.0 * jnp.arange(1, n + 1, dtype=jnp.float32) / n)


def kernel(x, p, g_mix, w_in, cmp_pos_k, cmp_w1_k, cmp_b1_k, cmp_w2_k, cmp_pos_v, cmp_w1_v, cmp_b1_v, cmp_w2_v, w_branch_a, w_branch_b, w_out, g_ffn, w_router, router_bias, w_gate, w_up, w_down, ws_gate, ws_up, ws_down, g_ple, w_ple_gate, w_ple, g_final):
    B, S, D = x.shape
    N = B * S
    G, R = NSA_KV_HEADS, NSA_REP
    assert w_in.shape[0] == 1 and p.shape[0] == 1
    assert S % (DIL_BLOCK * DIL_PATTERNS[-1][1]) == 0 and S % NSA_TK == 0 and S >= WIN_SIZE + NSA_Q
    assert N % ROW_TILE == 0
    slopes = _alibi_slopes(DIL_HEADS + NSA_HEADS)
    slopes_a, slopes_b = slopes[0::2], slopes[1::2]

    offs = np.concatenate([[0], np.cumsum(IN_SIZES)])
    parts = [w_in[0][:, offs[i]:offs[i + 1]] for i in range(len(IN_SIZES))]
    w_qa, w_ka, w_va, w_qb, w_kc, w_vc, w_ks, w_vs, w_kw, w_vw, w_gn, w_ga, w_gb = parts
    w_act = jnp.concatenate([w_qa, w_ka, w_va, jnp.zeros((D, 512), F32)]
                            + [_pad_heads(t) for t in (w_qb, w_kc, w_vc, w_ks, w_vs, w_kw, w_vw)],
                            axis=1).astype(BF16)
    act_w = w_act.shape[1]
    nsa_col0 = 2048 // LANE
    w_gn = w_gn.reshape(D, 3, G, R).transpose(0, 2, 1, 3).reshape(D, G, 3 * R)
    w_gn = jnp.pad(w_gn, ((0, 0), (0, 0), (0, LANE - 3 * R))).reshape(D, G * LANE)
    w_gate_in = jnp.concatenate([w_ga, w_gb, w_gn, jnp.zeros((D, 256), F32)], axis=1).astype(BF16)
    gate_w = w_gate_in.shape[1]

    x2 = x.reshape(N, D)
    dils = tuple(dil for _, dil in DIL_PATTERNS if dil > 1)
    act, *dilated_qkv = _inproj(x2, g_mix, w_act, BF16, 512, dils=dils, n_dilated=3)
    gt, = _inproj(x2, g_mix, w_gate_in, F32, 512)

    o_list, l_list = [], []
    for _, dil in DIL_PATTERNS:
        src = act if dil == 1 else dilated_qkv[dils.index(dil)]
        o, l = _dilated(src.reshape(B, S // dil, -1), slopes_a, dil, B, S)
        o_list.append(o.reshape(-1, N, LANE))
        l_list.append(l.reshape(-1, N, LANE))

    nchunk = S // CMP_STRIDE
    c0 = nsa_col0 * LANE + NSA_HEADS * LANE
    raw = act[:, c0:c0 + 2 * G * LANE].reshape(B, nchunk, CMP_STRIDE, 2 * G, LANE)[..., :HEAD_DIM]
    xc = raw.transpose(0, 3, 1, 2, 4).reshape(B, 2 * G, nchunk, CMP_STRIDE * HEAD_DIM)
    pos = jnp.stack([cmp_pos_k[0].reshape(1, -1), cmp_pos_v[0].reshape(1, -1)])
    w1 = jnp.stack([cmp_w1_k[0], cmp_w1_v[0]]).astype(BF16)
    b1 = jnp.stack([cmp_b1_k[0][None], cmp_b1_v[0][None]])
    w2 = jnp.pad(jnp.stack([cmp_w2_k[0], cmp_w2_v[0]]), ((0, 0), (0, 0), (0, LANE - HEAD_DIM))).astype(BF16)
    kcv = _compress(xc, pos, w1, b1, w2)
    kcv = kcv.reshape(B, 2, G, nchunk, LANE).transpose(0, 2, 1, 3, 4).reshape(B * G, 2, nchunk, LANE)

    yb = _nsa(slopes_b, act.reshape(B, S, act_w), gt.reshape(B, S, gate_w), kcv, B, S, nsa_col0, 2 * D // LANE)
    yb = yb.reshape(N, NSA_HEADS * LANE)

    wb_pad = jnp.pad(w_branch_b[0].reshape(NSA_HEADS, HEAD_DIM, D),
                     ((0, 0), (0, LANE - HEAD_DIM), (0, 0))).reshape(NSA_HEADS * LANE, D)
    wr_t = w_router[0].T
    wr_hi = wr_t.astype(BF16)
    wr_lo = (wr_t - wr_hi.astype(F32)).astype(BF16)
    h1, xn2, logits_t = _merge(o_list, l_list, yb, gt, x2, w_branch_a[0].astype(BF16), wb_pad.astype(BF16),
                               w_out[0].astype(BF16), g_ffn, wr_hi, wr_lo)

    idx_t, wsel_t, mask_t = _route(logits_t, router_bias[0][:, None])
    counts = jnp.sum(mask_t, axis=1)
    padded = (counts + MOE_ROWS - 1) // MOE_ROWS * MOE_ROWS
    pend = jnp.cumsum(padded)
    pstart = pend - padded
    rank_t = jnp.cumsum(mask_t, axis=1) - mask_t
    pos_t = jnp.take_along_axis(rank_t + pstart[:, None], idx_t, axis=0)
    NK = N * TOP_K
    nblk = (NK + N_EXPERTS * (MOE_ROWS - 1) + MOE_ROWS - 1) // MOE_ROWS
    P = nblk * MOE_ROWS
    tok = jnp.broadcast_to(jnp.arange(N, dtype=jnp.int32)[None, :], (TOP_K, N))
    row_tok = jnp.zeros((P,), jnp.int32).at[pos_t.reshape(-1)].set(tok.reshape(-1), unique_indices=True)
    blk_e = jnp.minimum(jnp.searchsorted(pend, jnp.arange(nblk, dtype=jnp.int32) * MOE_ROWS, side='right'),
                        N_EXPERTS - 1).astype(jnp.int32)
    nused = (pend[-1] // MOE_ROWS).astype(jnp.int32).reshape(1)
    xs = xn2[row_tok]
    ys = _moe_experts(blk_e, nused, xs, w_gate[0].astype(BF16), w_up[0].astype(BF16), w_down[0].astype(BF16))
    out = _final(h1, xn2, ys[pos_t], wsel_t.T, p[0].reshape(N, -1), ws_gate[0].astype(BF16), ws_up[0].astype(BF16),
                 ws_down[0].astype(BF16), g_ple, w_ple_gate[0].astype(BF16), w_ple[0].astype(BF16),
                 g_final[None, :])
    return out.reshape(B, S, D)
```

```python
import functools

import numpy as np
import jax
import jax.numpy as jnp
from jax import lax
from jax.experimental import pallas as pl
from jax.experimental.pallas import tpu as pltpu

F32 = jnp.float32
BF16 = jnp.bfloat16

HEAD_DIM = 64
LANE = 128
RMS_EPS = 1e-6
DIL_HEADS = 8
DIL_PATTERNS = ((128, 1), (512, 4), (2048, 16))
DIL_BLOCK = 128
NSA_HEADS = 8
NSA_KV_HEADS = 2
NSA_REP = NSA_HEADS // NSA_KV_HEADS
CMP_BLOCK = 32
CMP_STRIDE = 16
CMP_HIDDEN = 256
SLC_BLOCK = 64
SLC_TOP_N = 16
SLC_FORCE = 1e9
WIN_SIZE = 512
N_EXPERTS = 64
TOP_K = 8
N_GROUPS = 8
TOP_GROUPS = 4
GROUP_SIZE = N_EXPERTS // N_GROUPS
ROUTED_SCALE = 2.5
IN_SIZES = (512, 512, 512, 512, 128, 128, 128, 128, 128, 128, 24, 1024, 1024)

NEG = -1e30
NT_DIMS = (((1,), (1,)), ((), ()))

ROW_TILE = 512
NSA_Q = 128
NSA_TK = 512
MOE_ROWS = 512
VMEM_LIMIT = 48 * 1024 * 1024


def _dot(a, b):
    return jnp.dot(a, b, preferred_element_type=F32)


def _dot_nt(a, b):
    return lax.dot_general(a, b, NT_DIMS, preferred_element_type=F32)


def _split3(x):
    hi = x.astype(BF16)
    r1 = x - hi.astype(F32)
    mid = r1.astype(BF16)
    lo = (r1 - mid.astype(F32)).astype(BF16)
    return hi, mid, lo


def _rms(x, g):
    r = lax.rsqrt(jnp.mean(x * x, axis=-1, keepdims=True) + RMS_EPS)
    return (x * r) * g


def _params(sem):
    return pltpu.CompilerParams(dimension_semantics=sem, vmem_limit_bytes=VMEM_LIMIT)


def _inproj_kernel(x_ref, g_ref, w_ref, o_ref, *rest, dils, n_dilated):
    dil_refs, (xn_ref, res_ref) = rest[:len(dils)], rest[len(dils):]
    j = pl.program_id(1)

    @pl.when(j == 0)
    def _():
        xn_ref[...] = _rms(x_ref[...], g_ref[...]).astype(BF16)

    res = _dot(xn_ref[...], w_ref[...])
    o_ref[...] = res.astype(o_ref.dtype)

    if dils:
        @pl.when(j < n_dilated)
        def _():
            tn = res.shape[1]
            for c in range(tn // LANE):
                res_ref[c] = res[:, c * LANE:(c + 1) * LANE]
            for d_ref, dil in zip(dil_refs, dils):
                for r in range(dil):
                    for c in range(tn // LANE):
                        piece = res_ref[c, pl.ds(r, ROW_TILE // dil, stride=dil), :]
                        d_ref[:, r * tn + c * LANE:r * tn + (c + 1) * LANE] = piece.astype(d_ref.dtype)


def _inproj(x2, g, w, out_dtype, tn, dils=(), n_dilated=0):
    n, d = x2.shape
    cols = w.shape[1]
    last = max(n_dilated - 1, 0)
    out_specs = [pl.BlockSpec((ROW_TILE, tn), lambda i, j: (i, j))]
    out_shape = [jax.ShapeDtypeStruct((n, cols), out_dtype)]
    for dil in dils:
        out_specs.append(pl.BlockSpec((ROW_TILE // dil, dil * tn), lambda i, j: (i, jnp.minimum(j, last))))
        out_shape.append(jax.ShapeDtypeStruct((n // dil, n_dilated * dil * tn), out_dtype))
    return pl.pallas_call(
        functools.partial(_inproj_kernel, dils=dils, n_dilated=n_dilated),
        grid=(n // ROW_TILE, cols // tn),
        in_specs=[
            pl.BlockSpec((ROW_TILE, d), lambda i, j: (i, 0)),
            pl.BlockSpec((1, d), lambda i, j: (0, 0)),
            pl.BlockSpec((d, tn), lambda i, j: (0, j)),
        ],
        out_specs=out_specs,
        out_shape=out_shape,
        scratch_shapes=[pltpu.VMEM((ROW_TILE, d), BF16), pltpu.VMEM((tn // LANE, ROW_TILE, LANE), F32)],
        compiler_params=_params(("parallel", "arbitrary")),
    )(x2, g, w)


def _dil_kernel(slope_ref, q_ref, kp_ref, kc_ref, vp_ref, vc_ref, o_ref, l_ref, *, dil):
    i = pl.program_id(1)
    res = pl.program_id(2)
    blk = DIL_BLOCK
    out_rows = pl.ds(res, blk, stride=dil) if dil > 1 else slice(None)
    qi = lax.broadcasted_iota(jnp.int32, (blk, 2 * blk), 0)
    kj = lax.broadcasted_iota(jnp.int32, (blk, 2 * blk), 1)
    dist = (qi + blk - kj).astype(F32)
    valid = (dist >= 0) & (dist <= blk) & ((kj >= blk) | (i > 0))
    lane = lax.broadcasted_iota(jnp.int32, (blk, LANE), 1)
    lo = lane < HEAD_DIM
    scale = HEAD_DIM ** -0.5
    for hp in range(DIL_HEADS // 2):
        sl = slice(hp * LANE, (hp + 1) * LANE)
        q2 = q_ref[0, :, sl]
        k2 = jnp.concatenate([kp_ref[0, :, sl], kc_ref[0, :, sl]], axis=0)
        v2 = jnp.concatenate([vp_ref[0, :, sl], vc_ref[0, :, sl]], axis=0)
        outs, lses = [], []
        for hh in range(2):
            keep = lo if hh == 0 else jnp.logical_not(lo)
            qm = jnp.where(keep, q2, jnp.zeros_like(q2))
            slope = slope_ref[2 * hp + hh] * float(dil)
            s = _dot_nt(qm, k2) * scale - slope * dist
            s = jnp.where(valid, s, NEG)
            m = jnp.max(s, axis=-1, keepdims=True)
            e = jnp.exp(s - m)
            den = jnp.sum(e, axis=-1, keepdims=True)
            outs.append(_dot(e.astype(BF16), v2) / den)
            lses.append(jnp.broadcast_to(m + jnp.log(den), (blk, LANE)))
        o_ref[hp, 0, out_rows, :] = jnp.where(lo, outs[0], outs[1])
        l_ref[hp, 0, out_rows, :] = jnp.where(lo, lses[0], lses[1])


def _dilated(src3, slopes_a, dil, b, s):
    nblk = s // dil // DIL_BLOCK
    aw = DIL_HEADS * HEAD_DIM
    npair = DIL_HEADS // 2
    blk = (1, DIL_BLOCK, aw)

    def spec(which, prev):
        if prev:
            return pl.BlockSpec(blk, lambda bb, i, r: (bb, jnp.maximum(i - 1, 0), which * dil + r))
        return pl.BlockSpec(blk, lambda bb, i, r: (bb, i, which * dil + r))

    out_spec = pl.BlockSpec((npair, 1, DIL_BLOCK * dil, LANE), lambda bb, i, r: (0, bb, i, 0))
    return pl.pallas_call(
        functools.partial(_dil_kernel, dil=dil),
        grid=(b, nblk, dil),
        in_specs=[
            pl.BlockSpec(memory_space=pltpu.SMEM),
            spec(0, False), spec(1, True), spec(1, False), spec(2, True), spec(2, False),
        ],
        out_specs=[out_spec, out_spec],
        out_shape=[jax.ShapeDtypeStruct((npair, b, s, LANE), F32)] * 2,
        compiler_params=_params(("parallel", "arbitrary", "arbitrary")),
    )(slopes_a, src3, src3, src3, src3, src3)


def _cmp_kernel(x_ref, pos_ref, w1_ref, b1_ref, w2_ref, o_ref):
    half = x_ref.shape[-1]
    x = x_ref[0, 0]
    w1 = w1_ref[0]
    first = _dot(x, w1[:half])
    second = _dot(x, w1[half:])
    nrow = x.shape[0]
    pre = first + pltpu.roll(second, nrow - 1, 0)
    posb = _dot(jnp.broadcast_to(pos_ref[0], (8, 2 * half)).astype(BF16), w1)[0:1]
    pre = pre + posb + b1_ref[0]
    o_ref[0, 0] = _dot(jax.nn.gelu(pre).astype(BF16), w2_ref[0]).astype(o_ref.dtype)


def _compress(xc, pos, w1, b1, w2):
    b, four, nchunk, width = xc.shape
    return pl.pallas_call(
        _cmp_kernel,
        grid=(b, four),
        in_specs=[
            pl.BlockSpec((1, 1, nchunk, width), lambda bb, j: (bb, j, 0, 0)),
            pl.BlockSpec((1, 1, 2 * width), lambda bb, j: (j // 2, 0, 0)),
            pl.BlockSpec((1, 2 * width, CMP_HIDDEN), lambda bb, j: (j // 2, 0, 0)),
            pl.BlockSpec((1, 1, CMP_HIDDEN), lambda bb, j: (j // 2, 0, 0)),
            pl.BlockSpec((1, CMP_HIDDEN, LANE), lambda bb, j: (j // 2, 0, 0)),
        ],
        out_specs=pl.BlockSpec((1, 1, nchunk, LANE), lambda bb, j: (bb, j, 0, 0)),
        out_shape=jax.ShapeDtypeStruct((b, four, nchunk, LANE), BF16),
        compiler_params=_params(("parallel", "parallel")),
    )(xc, pos, w1, b1, w2)


POS_LANE = HEAD_DIM
POS_SPLIT = 64
SUM_LANE = HEAD_DIM
MASKED = -1e30
ROW_FLOOR = -1e29


def _pos_columns(pos):
    out = np.zeros((pos.shape[0], LANE), np.float32)
    out[:, POS_LANE] = pos // POS_SPLIT
    out[:, POS_LANE + 1] = pos % POS_SPLIT
    out[:, POS_LANE + 2] = 1.0
    out[:, POS_LANE + 3] = 1.0
    return jnp.asarray(out, BF16)


def _nsa_kernel(slope_ref, q_ref, kc_ref, vct_ref, ks_ref, kw_ref, vt_ref, gt_ref, ov_ref, kpa_ref, cpa_ref,
                oh_ref, o_ref, kaug_ref, flag_ref, m_ref, acc_ref, *, seq):
    g = pl.program_id(1)
    c = pl.program_id(2)
    C = NSA_Q
    R = NSA_REP
    TK = NSA_TK
    rows = R * C
    t0 = c * C
    scale = HEAD_DIM ** -0.5
    nb = seq // SLC_BLOCK
    ncmp = seq // CMP_STRIDE
    n_sel = min(SLC_TOP_N, nb)
    vblocks = TK // LANE

    @pl.when(c == 0)
    def _():
        kaug_ref[:, 0:LANE] = ks_ref[0] + kpa_ref[...]
        kaug_ref[:, LANE:2 * LANE] = oh_ref[...]

    row = lax.broadcasted_iota(jnp.int32, (rows, 1), 0)
    ti = t0 + row % C
    slope = jnp.zeros((rows, 1), F32)
    for r in range(R):
        slope = jnp.where(row // C == r, slope_ref[g * R + r], slope)
    lane = lax.broadcasted_iota(jnp.int32, (rows, LANE), 1)
    thi = (ti // POS_SPLIT).astype(F32)
    tlo = (ti % POS_SPLIT).astype(F32)
    qcols = jnp.where(lane == POS_LANE, POS_SPLIT * slope,
                      jnp.where(lane == POS_LANE + 1, slope,
                                jnp.where(lane == POS_LANE + 2, -POS_SPLIT * slope * thi,
                                          jnp.where(lane == POS_LANE + 3, -slope * tlo, 0.0))))
    q4 = jnp.concatenate([q_ref[0, :, r * LANE:(r + 1) * LANE] for r in range(R)], axis=0)
    q4 = (q4.astype(F32) * scale + qcols).astype(BF16)
    tq = t0 + lax.broadcasted_iota(jnp.int32, (1, C), 1)

    def heads(x):
        return jnp.concatenate([x] * R, axis=1)

    kc = kc_ref[0] + cpa_ref[...]
    ci = lax.broadcasted_iota(jnp.int32, (ncmp, 1), 0)
    vis = (ci * CMP_STRIDE + (CMP_BLOCK - 1) <= tq) & (ci < ncmp - 1)
    s = _dot_nt(kc, q4) + heads(jnp.where(vis, 0.0, MASKED))
    m = jnp.maximum(jnp.max(s, axis=0, keepdims=True), ROW_FLOOR)
    e = jnp.exp(s - m)
    den = jnp.sum(e, axis=0, keepdims=True)
    pc = e * (1.0 / jnp.where(den > 0, den, 1.0))
    o_cmp = _dot(vct_ref[0], pc.astype(BF16))
    pcsum = pc[:, 0:C]
    for r in range(1, R):
        pcsum = pcsum + pc[:, r * C:(r + 1) * C]
    ov = ov_ref[...]
    p_slc = sum(_dot(ov, part) for part in _split3(pcsum))

    j = lax.broadcasted_iota(jnp.int32, (LANE, C), 0)
    cur = tq // SLC_BLOCK
    forced = (j == 0) | (j == cur) | (j == cur - 1)
    future = j * SLC_BLOCK > tq
    work = jnp.where(forced, SLC_FORCE, jnp.where(future, -SLC_FORCE, p_slc))
    work = jnp.where(j < nb, work, -jnp.inf)
    sel = jnp.zeros((LANE, C), F32)
    jf = j.astype(F32)
    for _ in range(n_sel):
        mx = jnp.max(work, axis=0, keepdims=True)
        first = jnp.min(jnp.where(work == mx, jf, float(LANE)), axis=0, keepdims=True)
        pick = jf == first
        sel = jnp.where(pick, 1.0, sel)
        work = jnp.where(pick, -jnp.inf, work)
    sel = jnp.where(future, 0.0, sel)
    blocks_per_tile = TK // SLC_BLOCK
    for kt in range(seq // TK):
        flag_ref[kt] = jnp.max(sel[kt * blocks_per_tile:(kt + 1) * blocks_per_tile, :])
    sel_bias = jnp.where(sel.T > 0.5, 0.0, MASKED).astype(BF16)
    q_aug = jnp.concatenate([q4, jnp.concatenate([sel_bias] * R, axis=0)], axis=1)

    m_ref[...] = jnp.full((1, rows), ROW_FLOOR, F32)
    acc_ref[...] = jnp.zeros((LANE, rows), F32)
    kio = lax.broadcasted_iota(jnp.int32, (TK, 1), 0)

    def values_t(which, first_block, n):
        return jnp.concatenate([vt_ref[0, which, first_block + i] for i in range(n)], axis=1)

    def tile_update(kt, causal):
        start = pl.multiple_of(kt * TK, TK)
        s2 = _dot_nt(kaug_ref[pl.ds(start, TK), :], q_aug)
        if causal:
            s2 = s2 + heads(jnp.where(start + kio <= tq, 0.0, MASKED))
        m_i = m_ref[...]
        m_n = jnp.maximum(m_i, jnp.max(s2, axis=0, keepdims=True))
        alpha = jnp.exp(m_i - m_n)
        p2 = jnp.exp(s2 - m_n)
        acc_ref[...] = alpha * acc_ref[...] + _dot(values_t(0, kt * vblocks, vblocks), p2.astype(BF16))
        m_ref[...] = m_n

    def slc_body(kt, carry):
        @pl.when(flag_ref[kt] > 0.0)
        def _():
            tile_update(kt, False)

        return carry

    last = (t0 + C - 1) // TK
    lax.fori_loop(0, last, slc_body, 0)
    tile_update(last, True)
    acc = acc_ref[...]
    o_slc = acc / acc[SUM_LANE:SUM_LANE + 1, :]

    span = WIN_SIZE + C
    ws = pl.multiple_of(jnp.maximum(t0 - WIN_SIZE, 0), C)
    k_w = kw_ref[0, pl.ds(ws, span), :] + kpa_ref[pl.ds(ws, span), :]
    wpos = ws + lax.broadcasted_iota(jnp.int32, (span, 1), 0)
    wbias = jnp.where((wpos <= tq) & (wpos >= tq - (WIN_SIZE - 1)), 0.0, MASKED)
    s3 = _dot_nt(k_w, q4) + heads(wbias)
    e3 = jnp.exp(s3 - jnp.max(s3, axis=0, keepdims=True))
    ow = _dot(values_t(1, ws // LANE, span // LANE), e3.astype(BF16))
    o_win = ow / ow[SUM_LANE:SUM_LANE + 1, :]

    gates = jax.nn.sigmoid(gt_ref[0]).T
    for r in range(R):
        rs = slice(r * C, (r + 1) * C)
        out = (gates[r:r + 1] * o_cmp[:, rs] + gates[R + r:R + r + 1] * o_slc[:, rs]
               + gates[2 * R + r:2 * R + r + 1] * o_win[:, rs])
        o_ref[0, :, r * LANE:(r + 1) * LANE] = out.T.astype(o_ref.dtype)


def _nsa(slopes_b, pa3, gt3, kc, vct, vt, b, s, col0, gcol0):
    C = NSA_Q
    R = NSA_REP
    G = NSA_KV_HEADS
    nb = s // SLC_BLOCK
    ncmp = s // CMP_STRIDE
    assert nb <= LANE
    qc0 = col0 // R
    kv0 = col0 + NSA_HEADS
    rows = R * C

    ci = np.arange(ncmp)[None, :]
    sj = np.arange(LANE)[:, None]
    ov = ((ci * CMP_STRIDE < (sj + 1) * SLC_BLOCK) & (ci * CMP_STRIDE + CMP_BLOCK > sj * SLC_BLOCK)
          & (ci < ncmp - 1) & (sj < nb))
    ov = jnp.asarray(ov, BF16)
    kpa = _pos_columns(np.arange(s))
    cpa = _pos_columns(np.arange(ncmp) * CMP_STRIDE + (CMP_BLOCK - 1))
    oh = jnp.asarray(np.arange(s)[:, None] // SLC_BLOCK == np.arange(LANE)[None, :], BF16)

    def k_spec(which):
        return pl.BlockSpec((1, s, LANE), lambda bb, g, c: (bb, 0, kv0 + which * G + g))

    const = lambda a: pl.BlockSpec(a.shape, lambda bb, g, c: (0,) * a.ndim)
    return pl.pallas_call(
        functools.partial(_nsa_kernel, seq=s),
        grid=(b, G, s // C),
        in_specs=[
            pl.BlockSpec(memory_space=pltpu.SMEM),
            pl.BlockSpec((1, C, R * LANE), lambda bb, g, c: (bb, c, qc0 + g)),
            pl.BlockSpec((1, ncmp, LANE), lambda bb, g, c: (bb * G + g, 0, 0)),
            pl.BlockSpec((1, LANE, ncmp), lambda bb, g, c: (bb * G + g, 0, 0)),
            k_spec(2), k_spec(4),
            pl.BlockSpec((1, 2, s // LANE, LANE, LANE), lambda bb, g, c: (bb * G + g, 0, 0, 0, 0)),
            pl.BlockSpec((1, C, LANE), lambda bb, g, c: (bb, c, gcol0 + g)),
            const(ov), const(kpa), const(cpa), const(oh),
        ],
        out_specs=pl.BlockSpec((1, C, R * LANE), lambda bb, g, c: (bb, c, g)),
        out_shape=jax.ShapeDtypeStruct((b, s, NSA_HEADS * LANE), BF16),
        scratch_shapes=[
            pltpu.VMEM((s, 2 * LANE), BF16),
            pltpu.SMEM((s // NSA_TK,), F32),
            pltpu.VMEM((1, rows), F32),
            pltpu.VMEM((LANE, rows), F32),
        ],
        compiler_params=_params(("parallel", "parallel", "arbitrary")),
    )(slopes_b, pa3, kc, vct, pa3, pa3, vt, gt3, ov, kpa, cpa, oh)


def _merge_kernel(o1, o2, o3, l1, l2, l3, yb, ga, gb, x_ref, wa, wb, wo, gf, wrh, wrl,
                  h_ref, xn_ref, lg_ref):
    def pairs(ref):
        return jnp.concatenate([ref[h] for h in range(ref.shape[0])], axis=1)

    a1, a2, a3 = pairs(l1), pairs(l2), pairs(l3)
    mx = jnp.maximum(jnp.maximum(a1, a2), a3)
    e1, e2, e3 = jnp.exp(a1 - mx), jnp.exp(a2 - mx), jnp.exp(a3 - mx)
    tot = e1 + e2 + e3
    ya = (e1 / tot) * pairs(o1) + (e2 / tot) * pairs(o2) + (e3 / tot) * pairs(o3)
    merged = (jax.nn.sigmoid(ga[...]) * _dot(ya.astype(BF16), wa[...])
              + jax.nn.sigmoid(gb[...]) * _dot(yb[...], wb[...]))
    h = x_ref[...] + _dot(merged.astype(BF16), wo[...])
    h_ref[...] = h
    xn = _rms(h, gf[...])
    xn_ref[...] = xn.astype(BF16)
    xh = xn.astype(BF16)
    xl = (xn - xh.astype(F32)).astype(BF16)
    lg_ref[...] = _dot_nt(wrh[...], xh) + _dot_nt(wrh[...], xl) + _dot_nt(wrl[...], xh)


def _merge(o_list, l_list, yb, gt, x2, wa, wb, wo, gf, wrh, wrl):
    n, d = x2.shape
    aw = wa.shape[0]
    bw = wb.shape[0]
    tm = ROW_TILE
    row = lambda w: pl.BlockSpec((tm, w), lambda i: (i, 0))
    full = lambda a: pl.BlockSpec(a.shape, lambda i: (0,) * a.ndim)
    return pl.pallas_call(
        _merge_kernel,
        grid=(n // tm,),
        in_specs=[pl.BlockSpec((aw // LANE, tm, LANE), lambda i: (0, i, 0))] * 6 + [
            row(bw),
            pl.BlockSpec((tm, d), lambda i: (i, 0)),
            pl.BlockSpec((tm, d), lambda i: (i, 1)),
            row(d), full(wa), full(wb), full(wo), full(gf), full(wrh), full(wrl),
        ],
        out_specs=[row(d), row(d), pl.BlockSpec((N_EXPERTS, tm), lambda i: (0, i))],
        out_shape=[
            jax.ShapeDtypeStruct((n, d), F32),
            jax.ShapeDtypeStruct((n, d), BF16),
            jax.ShapeDtypeStruct((N_EXPERTS, n), F32),
        ],
        compiler_params=_params(("parallel",)),
    )(*o_list, *l_list, yb, gt, gt, x2, wa, wb, wo, gf, wrh, wrl)


def _route_kernel(lg_ref, bias_ref, idx_ref, w_ref, mask_ref):
    E = N_EXPERTS
    tn = lg_ref.shape[1]
    scores = jax.nn.sigmoid(lg_ref[...])
    sel = scores + bias_ref[...]
    sub = lax.broadcasted_iota(jnp.int32, (GROUP_SIZE, tn), 0).astype(F32)
    gs = []
    for grp in range(N_GROUPS):
        blk = sel[grp * GROUP_SIZE:(grp + 1) * GROUP_SIZE]
        m1 = jnp.max(blk, axis=0, keepdims=True)
        first = jnp.min(jnp.where(blk == m1, sub, float(GROUP_SIZE)), axis=0, keepdims=True)
        m2 = jnp.max(jnp.where(sub == first, -jnp.inf, blk), axis=0, keepdims=True)
        gs.append(m1 + m2)
    keep = []
    for a in range(N_GROUPS):
        rank = jnp.zeros((1, tn), jnp.int32)
        for o in range(N_GROUPS):
            if o == a:
                continue
            beats = (gs[o] >= gs[a]) if o < a else (gs[o] > gs[a])
            rank = rank + beats.astype(jnp.int32)
        keep.append(jnp.broadcast_to(rank < TOP_GROUPS, (GROUP_SIZE, tn)))
    emask = jnp.concatenate(keep, axis=0)
    cand = jnp.where(emask, sel, -jnp.inf)
    eid = lax.broadcasted_iota(jnp.int32, (E, tn), 0)
    eid_f = eid.astype(F32)
    rank = jnp.zeros((E, tn), jnp.int32)
    for o in range(E):
        other = cand[o:o + 1]
        beats = (other > cand) | ((other == cand) & (eid > o))
        rank = rank + beats.astype(jnp.int32)
    chosen = emask & (rank < TOP_K)
    wsel = jnp.where(chosen, scores, 0.0)
    wnorm = wsel / jnp.sum(wsel, axis=0, keepdims=True) * ROUTED_SCALE
    mask_ref[...] = chosen.astype(jnp.int32)
    for k in range(TOP_K):
        hit = chosen & (rank == k)
        idx_ref[k:k + 1, :] = jnp.sum(jnp.where(hit, eid_f, 0.0), axis=0, keepdims=True).astype(jnp.int32)
        w_ref[k:k + 1, :] = jnp.sum(jnp.where(hit, wnorm, 0.0), axis=0, keepdims=True)


def _route(logits_t, bias_col):
    e, n = logits_t.shape
    tn = ROW_TILE
    return pl.pallas_call(
        _route_kernel,
        grid=(n // tn,),
        in_specs=[pl.BlockSpec((e, tn), lambda i: (0, i)), pl.BlockSpec((e, 1), lambda i: (0, 0))],
        out_specs=[
            pl.BlockSpec((TOP_K, tn), lambda i: (0, i)),
            pl.BlockSpec((TOP_K, tn), lambda i: (0, i)),
            pl.BlockSpec((e, tn), lambda i: (0, i)),
        ],
        out_shape=[
            jax.ShapeDtypeStruct((TOP_K, n), jnp.int32),
            jax.ShapeDtypeStruct((TOP_K, n), F32),
            jax.ShapeDtypeStruct((e, n), jnp.int32),
        ],
        compiler_params=_params(("parallel",)),
    )(logits_t, bias_col)


def _moe_kernel(blk_e_ref, nused_ref, xs_ref, wg_ref, wu_ref, wd_ref, o_ref):
    i = pl.program_id(0)

    @pl.when(i < nused_ref[0])
    def _():
        x = xs_ref[...]
        hg = _dot(x, wg_ref[0])
        hid = (hg * jax.nn.sigmoid(hg)) * _dot(x, wu_ref[0])
        o_ref[...] = _dot(hid.astype(BF16), wd_ref[0]).astype(o_ref.dtype)

    @pl.when(i >= nused_ref[0])
    def _():
        o_ref[...] = jnp.zeros_like(o_ref)


def _moe_experts(blk_e, nused, xs, wg, wu, wd):
    p, d = xs.shape
    hid = wg.shape[2]
    grid_spec = pltpu.PrefetchScalarGridSpec(
        num_scalar_prefetch=2,
        grid=(p // MOE_ROWS,),
        in_specs=[
            pl.BlockSpec((MOE_ROWS, d), lambda i, be, nu: (i, 0)),
            pl.BlockSpec((1, d, hid), lambda i, be, nu: (be[i], 0, 0)),
            pl.BlockSpec((1, d, hid), lambda i, be, nu: (be[i], 0, 0)),
            pl.BlockSpec((1, hid, d), lambda i, be, nu: (be[i], 0, 0)),
        ],
        out_specs=pl.BlockSpec((MOE_ROWS, d), lambda i, be, nu: (i, 0)),
    )
    return pl.pallas_call(
        _moe_kernel,
        grid_spec=grid_spec,
        out_shape=jax.ShapeDtypeStruct((p, d), BF16),
        compiler_params=_params(("arbitrary",)),
    )(blk_e, nused, xs, wg, wu, wd)


def _final_kernel(h_ref, xn_ref, ys_ref, w_ref, p_ref, wsg, wsu, wsd, gp, wpg, wp, gfin, o_ref):
    xn = xn_ref[...]
    hg = _dot(xn, wsg[...])
    shared = _dot(((hg * jax.nn.sigmoid(hg)) * _dot(xn, wsu[...])).astype(BF16), wsd[...])
    w = w_ref[...]
    y = w[:, 0:1] * ys_ref[0].astype(F32)
    for k in range(1, TOP_K):
        y = y + w[:, k:k + 1] * ys_ref[k].astype(F32)
    h = h_ref[...] + (y + shared)
    gate = jax.nn.sigmoid(_dot(_rms(h, gp[...]).astype(BF16), wpg[...]))
    h = h + _dot(p_ref[...].astype(BF16), wp[...]) * gate
    o_ref[...] = _rms(h, gfin[...])


def _final(h1, xn2, ysg, w_nk, p2, wsg, wsu, wsd, gp, wpg, wp, gfin):
    n, d = h1.shape
    tm = ROW_TILE
    row = lambda w: pl.BlockSpec((tm, w), lambda i: (i, 0))
    full = lambda a: pl.BlockSpec(a.shape, lambda i: (0,) * a.ndim)
    return pl.pallas_call(
        _final_kernel,
        grid=(n // tm,),
        in_specs=[row(d), row(d), pl.BlockSpec((TOP_K, tm, d), lambda i: (0, i, 0)), row(TOP_K), row(p2.shape[1]),
                  full(wsg), full(wsu), full(wsd), full(gp), full(wpg), full(wp), full(gfin)],
        out_specs=row(d),
        out_shape=jax.ShapeDtypeStruct((n, d), F32),
        compiler_params=_params(("parallel",)),
    )(h1, xn2, ysg, w_nk, p2, wsg, wsu, wsd, gp, wpg, wp, gfin)


def _pad_heads(w):
    d, c = w.shape
    nh = c // HEAD_DIM
    w = w.reshape(d, nh, HEAD_DIM)
    return jnp.pad(w, ((0, 0), (0, 0), (0, LANE - HEAD_DIM))).reshape(d, nh * LANE)


def _alibi_slopes(n):
    return 2.0 ** (-8.0 * jnp.arange(1, n + 1, dtype=jnp.float32) / n)


def kernel(x, p, g_mix, w_in, cmp_pos_k, cmp_w1_k, cmp_b1_k, cmp_w2_k, cmp_pos_v, cmp_w1_v, cmp_b1_v, cmp_w2_v, w_branch_a, w_branch_b, w_out, g_ffn, w_router, router_bias, w_gate, w_up, w_down, ws_gate, ws_up, ws_down, g_ple, w_ple_gate, w_ple, g_final):
    B, S, D = x.shape
    N = B * S
    G, R = NSA_KV_HEADS, NSA_REP
    assert w_in.shape[0] == 1 and p.shape[0] == 1
    assert S % (DIL_BLOCK * DIL_PATTERNS[-1][1]) == 0 and S % NSA_TK == 0 and S >= WIN_SIZE + NSA_Q
    assert N % ROW_TILE == 0
    slopes = _alibi_slopes(DIL_HEADS + NSA_HEADS)
    slopes_a, slopes_b = slopes[0::2], slopes[1::2]

    offs = np.concatenate([[0], np.cumsum(IN_SIZES)])
    parts = [w_in[0][:, offs[i]:offs[i + 1]] for i in range(len(IN_SIZES))]
    w_qa, w_ka, w_va, w_qb, w_kc, w_vc, w_ks, w_vs, w_kw, w_vw, w_gn, w_ga, w_gb = parts
    w_act = jnp.concatenate([w_qa, w_ka, w_va, jnp.zeros((D, 512), F32)]
                            + [_pad_heads(t) for t in (w_qb, w_kc, w_vc, w_ks, w_vs, w_kw, w_vw)],
                            axis=1).astype(BF16)
    act_w = w_act.shape[1]
    nsa_col0 = 2048 // LANE
    w_gn = w_gn.reshape(D, 3, G, R).transpose(0, 2, 1, 3).reshape(D, G, 3 * R)
    w_gn = jnp.pad(w_gn, ((0, 0), (0, 0), (0, LANE - 3 * R))).reshape(D, G * LANE)
    w_gate_in = jnp.concatenate([w_ga, w_gb, w_gn, jnp.zeros((D, 256), F32)], axis=1).astype(BF16)
    gate_w = w_gate_in.shape[1]

    x2 = x.reshape(N, D)
    dils = tuple(dil for _, dil in DIL_PATTERNS if dil > 1)
    act, *dilated_qkv = _inproj(x2, g_mix, w_act, BF16, 512, dils=dils, n_dilated=3)
    gt, = _inproj(x2, g_mix, w_gate_in, F32, 512)

    o_list, l_list = [], []
    for _, dil in DIL_PATTERNS:
        src = act if dil == 1 else dilated_qkv[dils.index(dil)]
        o, l = _dilated(src.reshape(B, S // dil, -1), slopes_a, dil, B, S)
        o_list.append(o.reshape(-1, N, LANE))
        l_list.append(l.reshape(-1, N, LANE))

    nchunk = S // CMP_STRIDE
    c0 = nsa_col0 * LANE + NSA_HEADS * LANE
    raw = act[:, c0:c0 + 2 * G * LANE].reshape(B, nchunk, CMP_STRIDE, 2 * G, LANE)[..., :HEAD_DIM]
    xc = raw.transpose(0, 3, 1, 2, 4).reshape(B, 2 * G, nchunk, CMP_STRIDE * HEAD_DIM)
    pos = jnp.stack([cmp_pos_k[0].reshape(1, -1), cmp_pos_v[0].reshape(1, -1)])
    w1 = jnp.stack([cmp_w1_k[0], cmp_w1_v[0]]).astype(BF16)
    b1 = jnp.stack([cmp_b1_k[0][None], cmp_b1_v[0][None]])
    w2 = jnp.pad(jnp.stack([cmp_w2_k[0], cmp_w2_v[0]]), ((0, 0), (0, 0), (0, LANE - HEAD_DIM))).astype(BF16)
    kcv = _compress(xc, pos, w1, b1, w2)
    kcv = kcv.reshape(B, 2, G, nchunk, LANE).transpose(1, 0, 2, 3, 4).reshape(2, B * G, nchunk, LANE)
    vct = kcv[1].transpose(0, 2, 1)
    v0 = (nsa_col0 + NSA_HEADS + 3 * G) * LANE
    vsw = jnp.stack([act[:, v0:v0 + G * LANE], act[:, v0 + 2 * G * LANE:v0 + 3 * G * LANE]])
    vsw = vsw.reshape(2, B, S // LANE, LANE, G, LANE).transpose(1, 4, 0, 2, 5, 3)
    ones_row = (jnp.arange(LANE) == SUM_LANE).astype(BF16)[:, None]
    vt = (vsw + ones_row).reshape(B * G, 2, S // LANE, LANE, LANE)
    yb = _nsa(slopes_b, act.reshape(B, S, act_w), gt.reshape(B, S, gate_w), kcv[0], vct, vt,
              B, S, nsa_col0, 2 * D // LANE)
    yb = yb.reshape(N, NSA_HEADS * LANE)

    wb_pad = jnp.pad(w_branch_b[0].reshape(NSA_HEADS, HEAD_DIM, D),
                     ((0, 0), (0, LANE - HEAD_DIM), (0, 0))).reshape(NSA_HEADS * LANE, D)
    wr_t = w_router[0].T
    wr_hi = wr_t.astype(BF16)
    wr_lo = (wr_t - wr_hi.astype(F32)).astype(BF16)
    h1, xn2, logits_t = _merge(o_list, l_list, yb, gt, x2, w_branch_a[0].astype(BF16), wb_pad.astype(BF16),
                               w_out[0].astype(BF16), g_ffn, wr_hi, wr_lo)

    idx_t, wsel_t, mask_t = _route(logits_t, router_bias[0][:, None])
    counts = jnp.sum(mask_t, axis=1)
    padded = (counts + MOE_ROWS - 1) // MOE_ROWS * MOE_ROWS
    pend = jnp.cumsum(padded)
    pstart = pend - padded
    rank_t = jnp.cumsum(mask_t, axis=1) - mask_t
    pos_t = jnp.take_along_axis(rank_t + pstart[:, None], idx_t, axis=0)
    NK = N * TOP_K
    nblk = (NK + N_EXPERTS * (MOE_ROWS - 1) + MOE_ROWS - 1) // MOE_ROWS
    P = nblk * MOE_ROWS
    tok = jnp.broadcast_to(jnp.arange(N, dtype=jnp.int32)[None, :], (TOP_K, N))
    row_tok = jnp.zeros((P,), jnp.int32).at[pos_t.reshape(-1)].set(tok.reshape(-1), unique_indices=True)
    blk_e = jnp.minimum(jnp.searchsorted(pend, jnp.arange(nblk, dtype=jnp.int32) * MOE_ROWS, side='right'),
                        N_EXPERTS - 1).astype(jnp.int32)
    nused = (pend[-1] // MOE_ROWS).astype(jnp.int32).reshape(1)
    xs = xn2[row_tok]
    ys = _moe_experts(blk_e, nused, xs, w_gate[0].astype(BF16), w_up[0].astype(BF16), w_down[0].astype(BF16))
    out = _final(h1, xn2, ys[pos_t], wsel_t.T, p[0].reshape(N, -1), ws_gate[0].astype(BF16), ws_up[0].astype(BF16),
                 ws_down[0].astype(BF16), g_ple, w_ple_gate[0].astype(BF16), w_ple[0].astype(BF16),
                 g_final[None, :])
    return out.reshape(B, S, D)
```

```python
import functools

import numpy as np
import jax
import jax.numpy as jnp
from jax import lax
from jax.experimental import pallas as pl
from jax.experimental.pallas import tpu as pltpu

F32 = jnp.float32
BF16 = jnp.bfloat16

HEAD_DIM = 64
LANE = 128
RMS_EPS = 1e-6
DIL_HEADS = 8
DIL_PATTERNS = ((128, 1), (512, 4), (2048, 16))
DIL_BLOCK = 128
NSA_HEADS = 8
NSA_KV_HEADS = 2
NSA_REP = NSA_HEADS // NSA_KV_HEADS
CMP_BLOCK = 32
CMP_STRIDE = 16
CMP_HIDDEN = 256
SLC_BLOCK = 64
SLC_TOP_N = 16
SLC_FORCE = 1e9
WIN_SIZE = 512
N_EXPERTS = 64
TOP_K = 8
N_GROUPS = 8
TOP_GROUPS = 4
GROUP_SIZE = N_EXPERTS // N_GROUPS
ROUTED_SCALE = 2.5
IN_SIZES = (512, 512, 512, 512, 128, 128, 128, 128, 128, 128, 24, 1024, 1024)

NEG = -1e30
NT_DIMS = (((1,), (1,)), ((), ()))

ROW_TILE = 512
NSA_Q = 128
NSA_TK = 512
MOE_ROWS = 512
VMEM_LIMIT = 48 * 1024 * 1024


def _dot(a, b):
    return jnp.dot(a, b, preferred_element_type=F32)


def _dot_nt(a, b):
    return lax.dot_general(a, b, NT_DIMS, preferred_element_type=F32)


def _split3(x):
    hi = x.astype(BF16)
    r1 = x - hi.astype(F32)
    mid = r1.astype(BF16)
    lo = (r1 - mid.astype(F32)).astype(BF16)
    return hi, mid, lo


def _rms(x, g):
    r = lax.rsqrt(jnp.mean(x * x, axis=-1, keepdims=True) + RMS_EPS)
    return (x * r) * g


def _params(sem):
    return pltpu.CompilerParams(dimension_semantics=sem, vmem_limit_bytes=VMEM_LIMIT)


def _inproj_kernel(x_ref, g_ref, w_ref, o_ref, *rest, dils, n_dilated):
    dil_refs, (xn_ref, res_ref) = rest[:len(dils)], rest[len(dils):]
    j = pl.program_id(1)

    @pl.when(j == 0)
    def _():
        xn_ref[...] = _rms(x_ref[...], g_ref[...]).astype(BF16)

    res = _dot(xn_ref[...], w_ref[...])
    o_ref[...] = res.astype(o_ref.dtype)

    if dils:
        @pl.when(j < n_dilated)
        def _():
            tn = res.shape[1]
            for c in range(tn // LANE):
                res_ref[c] = res[:, c * LANE:(c + 1) * LANE]
            for d_ref, dil in zip(dil_refs, dils):
                for r in range(dil):
                    for c in range(tn // LANE):
                        piece = res_ref[c, pl.ds(r, ROW_TILE // dil, stride=dil), :]
                        d_ref[:, r * tn + c * LANE:r * tn + (c + 1) * LANE] = piece.astype(d_ref.dtype)


def _inproj(x2, g, w, out_dtype, tn, dils=(), n_dilated=0):
    n, d = x2.shape
    cols = w.shape[1]
    last = max(n_dilated - 1, 0)
    out_specs = [pl.BlockSpec((ROW_TILE, tn), lambda i, j: (i, j))]
    out_shape = [jax.ShapeDtypeStruct((n, cols), out_dtype)]
    for dil in dils:
        out_specs.append(pl.BlockSpec((ROW_TILE // dil, dil * tn), lambda i, j: (i, jnp.minimum(j, last))))
        out_shape.append(jax.ShapeDtypeStruct((n // dil, n_dilated * dil * tn), out_dtype))
    return pl.pallas_call(
        functools.partial(_inproj_kernel, dils=dils, n_dilated=n_dilated),
        grid=(n // ROW_TILE, cols // tn),
        in_specs=[
            pl.BlockSpec((ROW_TILE, d), lambda i, j: (i, 0)),
            pl.BlockSpec((1, d), lambda i, j: (0, 0)),
            pl.BlockSpec((d, tn), lambda i, j: (0, j)),
        ],
        out_specs=out_specs,
        out_shape=out_shape,
        scratch_shapes=[pltpu.VMEM((ROW_TILE, d), BF16), pltpu.VMEM((tn // LANE, ROW_TILE, LANE), F32)],
        compiler_params=_params(("parallel", "arbitrary")),
    )(x2, g, w)


def _dil_kernel(slope_ref, q_ref, kp_ref, kc_ref, vp_ref, vc_ref, o_ref, l_ref, *, dil):
    i = pl.program_id(1)
    res = pl.program_id(2)
    blk = DIL_BLOCK
    out_rows = pl.ds(res, blk, stride=dil) if dil > 1 else slice(None)
    qi = lax.broadcasted_iota(jnp.int32, (blk, 2 * blk), 0)
    kj = lax.broadcasted_iota(jnp.int32, (blk, 2 * blk), 1)
    dist = (qi + blk - kj).astype(F32)
    valid = (dist >= 0) & (dist <= blk) & ((kj >= blk) | (i > 0))
    lane = lax.broadcasted_iota(jnp.int32, (blk, LANE), 1)
    lo = lane < HEAD_DIM
    scale = HEAD_DIM ** -0.5
    for hp in range(DIL_HEADS // 2):
        sl = slice(hp * LANE, (hp + 1) * LANE)
        q2 = q_ref[0, :, sl]
        k2 = jnp.concatenate([kp_ref[0, :, sl], kc_ref[0, :, sl]], axis=0)
        v2 = jnp.concatenate([vp_ref[0, :, sl], vc_ref[0, :, sl]], axis=0)
        outs, lses = [], []
        for hh in range(2):
            keep = lo if hh == 0 else jnp.logical_not(lo)
            qm = jnp.where(keep, q2, jnp.zeros_like(q2))
            slope = slope_ref[2 * hp + hh] * float(dil)
            s = _dot_nt(qm, k2) * scale - slope * dist
            s = jnp.where(valid, s, NEG)
            m = jnp.max(s, axis=-1, keepdims=True)
            e = jnp.exp(s - m)
            den = jnp.sum(e, axis=-1, keepdims=True)
            outs.append(_dot(e.astype(BF16), v2) / den)
            lses.append(jnp.broadcast_to(m + jnp.log(den), (blk, LANE)))
        o_ref[hp, 0, out_rows, :] = jnp.where(lo, outs[0], outs[1])
        l_ref[hp, 0, out_rows, :] = jnp.where(lo, lses[0], lses[1])


def _dilated(src3, slopes_a, dil, b, s):
    nblk = s // dil // DIL_BLOCK
    aw = DIL_HEADS * HEAD_DIM
    npair = DIL_HEADS // 2
    blk = (1, DIL_BLOCK, aw)

    def spec(which, prev):
        if prev:
            return pl.BlockSpec(blk, lambda bb, i, r: (bb, jnp.maximum(i - 1, 0), which * dil + r))
        return pl.BlockSpec(blk, lambda bb, i, r: (bb, i, which * dil + r))

    out_spec = pl.BlockSpec((npair, 1, DIL_BLOCK * dil, LANE), lambda bb, i, r: (0, bb, i, 0))
    return pl.pallas_call(
        functools.partial(_dil_kernel, dil=dil),
        grid=(b, nblk, dil),
        in_specs=[
            pl.BlockSpec(memory_space=pltpu.SMEM),
            spec(0, False), spec(1, True), spec(1, False), spec(2, True), spec(2, False),
        ],
        out_specs=[out_spec, out_spec],
        out_shape=[jax.ShapeDtypeStruct((npair, b, s, LANE), F32)] * 2,
        compiler_params=_params(("parallel", "arbitrary", "arbitrary")),
    )(slopes_a, src3, src3, src3, src3, src3)


def _cmp_kernel(x_ref, pos_ref, w1_ref, b1_ref, w2_ref, o_ref):
    half = x_ref.shape[-1]
    x = x_ref[0, 0]
    w1 = w1_ref[0]
    first = _dot(x, w1[:half])
    second = _dot(x, w1[half:])
    nrow = x.shape[0]
    pre = first + pltpu.roll(second, nrow - 1, 0)
    posb = _dot(jnp.broadcast_to(pos_ref[0], (8, 2 * half)).astype(BF16), w1)[0:1]
    pre = pre + posb + b1_ref[0]
    o_ref[0, 0] = _dot(jax.nn.gelu(pre).astype(BF16), w2_ref[0]).astype(o_ref.dtype)


def _compress(xc, pos, w1, b1, w2):
    b, four, nchunk, width = xc.shape
    return pl.pallas_call(
        _cmp_kernel,
        grid=(b, four),
        in_specs=[
            pl.BlockSpec((1, 1, nchunk, width), lambda bb, j: (bb, j, 0, 0)),
            pl.BlockSpec((1, 1, 2 * width), lambda bb, j: (j // 2, 0, 0)),
            pl.BlockSpec((1, 2 * width, CMP_HIDDEN), lambda bb, j: (j // 2, 0, 0)),
            pl.BlockSpec((1, 1, CMP_HIDDEN), lambda bb, j: (j // 2, 0, 0)),
            pl.BlockSpec((1, CMP_HIDDEN, LANE), lambda bb, j: (j // 2, 0, 0)),
        ],
        out_specs=pl.BlockSpec((1, 1, nchunk, LANE), lambda bb, j: (bb, j, 0, 0)),
        out_shape=jax.ShapeDtypeStruct((b, four, nchunk, LANE), BF16),
        compiler_params=_params(("parallel", "parallel")),
    )(xc, pos, w1, b1, w2)


POS_LANE = HEAD_DIM
POS_SPLIT = 64
SUM_LANE = HEAD_DIM
MASKED = -1e30
ROW_FLOOR = -1e29


def _pos_columns(pos):
    out = np.zeros((pos.shape[0], LANE), np.float32)
    out[:, POS_LANE] = pos // POS_SPLIT
    out[:, POS_LANE + 1] = pos % POS_SPLIT
    out[:, POS_LANE + 2] = 1.0
    out[:, POS_LANE + 3] = 1.0
    return jnp.asarray(out, BF16)


def _nsa_kernel(slope_ref, q_ref, kc_ref, vct_ref, ks_ref, kw_ref, vt_ref, gt_ref, ov_ref, kpa_ref, cpa_ref,
                oh_ref, o_ref, kaug_ref, flag_ref, m_ref, acc_ref, *, seq):
    g = pl.program_id(1)
    c = pl.program_id(2)
    C = NSA_Q
    R = NSA_REP
    TK = NSA_TK
    rows = R * C
    t0 = c * C
    scale = HEAD_DIM ** -0.5
    nb = seq // SLC_BLOCK
    ncmp = seq // CMP_STRIDE
    n_sel = min(SLC_TOP_N, nb)
    vblocks = TK // LANE

    @pl.when(c == 0)
    def _():
        kaug_ref[:, 0:LANE] = ks_ref[0] + kpa_ref[...]
        kaug_ref[:, LANE:2 * LANE] = oh_ref[...]

    row = lax.broadcasted_iota(jnp.int32, (rows, 1), 0)
    ti = t0 + row % C
    slope = jnp.zeros((rows, 1), F32)
    for r in range(R):
        slope = jnp.where(row // C == r, slope_ref[g * R + r], slope)
    lane = lax.broadcasted_iota(jnp.int32, (rows, LANE), 1)
    thi = (ti // POS_SPLIT).astype(F32)
    tlo = (ti % POS_SPLIT).astype(F32)
    qcols = jnp.where(lane == POS_LANE, POS_SPLIT * slope,
                      jnp.where(lane == POS_LANE + 1, slope,
                                jnp.where(lane == POS_LANE + 2, -POS_SPLIT * slope * thi,
                                          jnp.where(lane == POS_LANE + 3, -slope * tlo, 0.0))))
    q4 = jnp.concatenate([q_ref[0, :, r * LANE:(r + 1) * LANE] for r in range(R)], axis=0)
    q4 = (q4.astype(F32) * scale + qcols).astype(BF16)
    tq = t0 + lax.broadcasted_iota(jnp.int32, (1, C), 1)

    def heads(x):
        return jnp.concatenate([x] * R, axis=1)

    kc = kc_ref[0] + cpa_ref[...]
    ci = lax.broadcasted_iota(jnp.int32, (ncmp, 1), 0)
    vis = (ci * CMP_STRIDE + (CMP_BLOCK - 1) <= tq) & (ci < ncmp - 1)
    s = _dot_nt(kc, q4) + heads(jnp.where(vis, 0.0, MASKED))
    m = jnp.maximum(jnp.max(s, axis=0, keepdims=True), ROW_FLOOR)
    e = jnp.exp(s - m)
    den = jnp.sum(e, axis=0, keepdims=True)
    pc = e * (1.0 / jnp.where(den > 0, den, 1.0))
    o_cmp = _dot(vct_ref[0], pc.astype(BF16))
    pcsum = pc[:, 0:C]
    for r in range(1, R):
        pcsum = pcsum + pc[:, r * C:(r + 1) * C]
    ov = ov_ref[...]
    p_slc = sum(_dot(ov, part) for part in _split3(pcsum))

    j = lax.broadcasted_iota(jnp.int32, (LANE, C), 0)
    cur = tq // SLC_BLOCK
    forced = (j == 0) | (j == cur) | (j == cur - 1)
    future = j * SLC_BLOCK > tq
    work = jnp.where(forced, SLC_FORCE, jnp.where(future, -SLC_FORCE, p_slc))
    work = jnp.where(j < nb, work, -jnp.inf)
    sel = jnp.zeros((LANE, C), F32)
    jf = j.astype(F32)
    for _ in range(n_sel):
        mx = jnp.max(work, axis=0, keepdims=True)
        first = jnp.min(jnp.where(work == mx, jf, float(LANE)), axis=0, keepdims=True)
        pick = jf == first
        sel = jnp.where(pick, 1.0, sel)
        work = jnp.where(pick, -jnp.inf, work)
    sel = jnp.where(future, 0.0, sel)
    blocks_per_tile = TK // SLC_BLOCK
    for kt in range(seq // TK):
        flag_ref[kt] = jnp.max(sel[kt * blocks_per_tile:(kt + 1) * blocks_per_tile, :])
    sel_bias = jnp.where(sel.T > 0.5, 0.0, MASKED).astype(BF16)
    q_aug = jnp.concatenate([q4, jnp.concatenate([sel_bias] * R, axis=0)], axis=1)

    m_ref[...] = jnp.full((1, rows), ROW_FLOOR, F32)
    acc_ref[...] = jnp.zeros((LANE, rows), F32)
    kio = lax.broadcasted_iota(jnp.int32, (TK, 1), 0)

    def values_t(which, first_block, n):
        return jnp.concatenate([vt_ref[0, which, first_block + i] for i in range(n)], axis=1)

    def tile_update(kt, causal):
        start = pl.multiple_of(kt * TK, TK)
        s2 = _dot_nt(kaug_ref[pl.ds(start, TK), :], q_aug)
        if causal:
            s2 = s2 + heads(jnp.where(start + kio <= tq, 0.0, MASKED))
        m_i = m_ref[...]
        m_n = jnp.maximum(m_i, jnp.max(s2, axis=0, keepdims=True))
        alpha = jnp.exp(m_i - m_n)
        p2 = jnp.exp(s2 - m_n)
        acc_ref[...] = alpha * acc_ref[...] + _dot(values_t(0, kt * vblocks, vblocks), p2.astype(BF16))
        m_ref[...] = m_n

    def slc_body(kt, carry):
        @pl.when(flag_ref[kt] > 0.0)
        def _():
            tile_update(kt, False)

        return carry

    last = (t0 + C - 1) // TK
    lax.fori_loop(0, last, slc_body, 0)
    tile_update(last, True)
    acc = acc_ref[...]
    o_slc = acc / acc[SUM_LANE:SUM_LANE + 1, :]

    span = WIN_SIZE + C
    ws = pl.multiple_of(jnp.maximum(t0 - WIN_SIZE, 0), C)
    k_w = kw_ref[0, pl.ds(ws, span), :] + kpa_ref[pl.ds(ws, span), :]
    wpos = ws + lax.broadcasted_iota(jnp.int32, (span, 1), 0)
    wbias = jnp.where((wpos <= tq) & (wpos >= tq - (WIN_SIZE - 1)), 0.0, MASKED)
    s3 = _dot_nt(k_w, q4) + heads(wbias)
    e3 = jnp.exp(s3 - jnp.max(s3, axis=0, keepdims=True))
    ow = _dot(values_t(1, ws // LANE, span // LANE), e3.astype(BF16))
    o_win = ow / ow[SUM_LANE:SUM_LANE + 1, :]

    gates = jax.nn.sigmoid(gt_ref[0]).T
    for r in range(R):
        rs = slice(r * C, (r + 1) * C)
        out = (gates[r:r + 1] * o_cmp[:, rs] + gates[R + r:R + r + 1] * o_slc[:, rs]
               + gates[2 * R + r:2 * R + r + 1] * o_win[:, rs])
        o_ref[0, :, r * LANE:(r + 1) * LANE] = out.T.astype(o_ref.dtype)


def _nsa(slopes_b, pa3, gt3, kc, vct, vt, b, s, col0, gcol0):
    C = NSA_Q
    R = NSA_REP
    G = NSA_KV_HEADS
    nb = s // SLC_BLOCK
    ncmp = s // CMP_STRIDE
    assert nb <= LANE
    qc0 = col0 // R
    kv0 = col0 + NSA_HEADS
    rows = R * C

    ci = np.arange(ncmp)[None, :]
    sj = np.arange(LANE)[:, None]
    ov = ((ci * CMP_STRIDE < (sj + 1) * SLC_BLOCK) & (ci * CMP_STRIDE + CMP_BLOCK > sj * SLC_BLOCK)
          & (ci < ncmp - 1) & (sj < nb))
    ov = jnp.asarray(ov, BF16)
    kpa = _pos_columns(np.arange(s))
    cpa = _pos_columns(np.arange(ncmp) * CMP_STRIDE + (CMP_BLOCK - 1))
    oh = jnp.asarray(np.arange(s)[:, None] // SLC_BLOCK == np.arange(LANE)[None, :], BF16)

    def k_spec(which):
        return pl.BlockSpec((1, s, LANE), lambda bb, g, c: (bb, 0, kv0 + which * G + g))

    const = lambda a: pl.BlockSpec(a.shape, lambda bb, g, c: (0,) * a.ndim)
    return pl.pallas_call(
        functools.partial(_nsa_kernel, seq=s),
        grid=(b, G, s // C),
        in_specs=[
            pl.BlockSpec(memory_space=pltpu.SMEM),
            pl.BlockSpec((1, C, R * LANE), lambda bb, g, c: (bb, c, qc0 + g)),
            pl.BlockSpec((1, ncmp, LANE), lambda bb, g, c: (bb * G + g, 0, 0)),
            pl.BlockSpec((1, LANE, ncmp), lambda bb, g, c: (bb * G + g, 0, 0)),
            k_spec(2), k_spec(4),
            pl.BlockSpec((1, 2, s // LANE, LANE, LANE), lambda bb, g, c: (bb * G + g, 0, 0, 0, 0)),
            pl.BlockSpec((1, C, LANE), lambda bb, g, c: (bb, c, gcol0 + g)),
            const(ov), const(kpa), const(cpa), const(oh),
        ],
        out_specs=pl.BlockSpec((1, C, R * LANE), lambda bb, g, c: (bb, c, g)),
        out_shape=jax.ShapeDtypeStruct((b, s, NSA_HEADS * LANE), BF16),
        scratch_shapes=[
            pltpu.VMEM((s, 2 * LANE), BF16),
            pltpu.SMEM((s // NSA_TK,), F32),
            pltpu.VMEM((1, rows), F32),
            pltpu.VMEM((LANE, rows), F32),
        ],
        compiler_params=_params(("parallel", "parallel", "arbitrary")),
    )(slopes_b, pa3, kc, vct, pa3, pa3, vt, gt3, ov, kpa, cpa, oh)


def _merge_kernel(o1, o2, o3, l1, l2, l3, yb, ga, gb, x_ref, wa, wb, wo, gf, wrh, wrl,
                  h_ref, xn_ref, lg_ref, xr_ref):
    def pairs(ref):
        return jnp.concatenate([ref[h] for h in range(ref.shape[0])], axis=1)

    a1, a2, a3 = pairs(l1), pairs(l2), pairs(l3)
    mx = jnp.maximum(jnp.maximum(a1, a2), a3)
    e1, e2, e3 = jnp.exp(a1 - mx), jnp.exp(a2 - mx), jnp.exp(a3 - mx)
    tot = e1 + e2 + e3
    ya = (e1 / tot) * pairs(o1) + (e2 / tot) * pairs(o2) + (e3 / tot) * pairs(o3)
    merged = (jax.nn.sigmoid(ga[...]) * _dot(ya.astype(BF16), wa[...])
              + jax.nn.sigmoid(gb[...]) * _dot(yb[...], wb[...]))
    h = x_ref[...] + _dot(merged.astype(BF16), wo[...])
    h_ref[...] = h
    xn = _rms(h, gf[...])
    xn_ref[...] = xn.astype(BF16)
    tm, d = xn.shape
    for c in range(d // LANE):
        xr_ref[pl.ds(c, tm, stride=d // LANE), :] = xn[:, c * LANE:(c + 1) * LANE]
    xh = xn.astype(BF16)
    xl = (xn - xh.astype(F32)).astype(BF16)
    lg_ref[...] = _dot_nt(wrh[...], xh) + _dot_nt(wrh[...], xl) + _dot_nt(wrl[...], xh)


def _merge(o_list, l_list, yb, gt, x2, wa, wb, wo, gf, wrh, wrl):
    n, d = x2.shape
    aw = wa.shape[0]
    bw = wb.shape[0]
    tm = ROW_TILE
    row = lambda w: pl.BlockSpec((tm, w), lambda i: (i, 0))
    full = lambda a: pl.BlockSpec(a.shape, lambda i: (0,) * a.ndim)
    return pl.pallas_call(
        _merge_kernel,
        grid=(n // tm,),
        in_specs=[pl.BlockSpec((aw // LANE, tm, LANE), lambda i: (0, i, 0))] * 6 + [
            row(bw),
            pl.BlockSpec((tm, d), lambda i: (i, 0)),
            pl.BlockSpec((tm, d), lambda i: (i, 1)),
            row(d), full(wa), full(wb), full(wo), full(gf), full(wrh), full(wrl),
        ],
        out_specs=[row(d), row(d), pl.BlockSpec((N_EXPERTS, tm), lambda i: (0, i)),
                   pl.BlockSpec((tm * d // LANE, LANE), lambda i: (i, 0))],
        out_shape=[
            jax.ShapeDtypeStruct((n, d), F32),
            jax.ShapeDtypeStruct((n, d), BF16),
            jax.ShapeDtypeStruct((N_EXPERTS, n), F32),
            jax.ShapeDtypeStruct((n * d // LANE, LANE), F32),
        ],
        compiler_params=_params(("parallel",)),
    )(*o_list, *l_list, yb, gt, gt, x2, wa, wb, wo, gf, wrh, wrl)


def _route_kernel(lg_ref, bias_ref, tri_ref, idx_ref, w_ref, rank_ref, cnt_ref, carry_ref):
    E = N_EXPERTS
    tn = lg_ref.shape[1]
    scores = jax.nn.sigmoid(lg_ref[...])
    sel = scores + bias_ref[...]
    sub = lax.broadcasted_iota(jnp.int32, (GROUP_SIZE, tn), 0).astype(F32)
    gs = []
    for grp in range(N_GROUPS):
        blk = sel[grp * GROUP_SIZE:(grp + 1) * GROUP_SIZE]
        m1 = jnp.max(blk, axis=0, keepdims=True)
        first = jnp.min(jnp.where(blk == m1, sub, float(GROUP_SIZE)), axis=0, keepdims=True)
        m2 = jnp.max(jnp.where(sub == first, -jnp.inf, blk), axis=0, keepdims=True)
        gs.append(m1 + m2)
    keep = []
    for a in range(N_GROUPS):
        rank = jnp.zeros((1, tn), jnp.int32)
        for o in range(N_GROUPS):
            if o == a:
                continue
            beats = (gs[o] >= gs[a]) if o < a else (gs[o] > gs[a])
            rank = rank + beats.astype(jnp.int32)
        keep.append(jnp.broadcast_to(rank < TOP_GROUPS, (GROUP_SIZE, tn)))
    emask = jnp.concatenate(keep, axis=0)
    cand = jnp.where(emask, sel, -jnp.inf)
    eid = lax.broadcasted_iota(jnp.int32, (E, tn), 0)
    eid_f = eid.astype(F32)
    rank = jnp.zeros((E, tn), jnp.int32)
    for o in range(E):
        other = cand[o:o + 1]
        beats = (other > cand) | ((other == cand) & (eid > o))
        rank = rank + beats.astype(jnp.int32)
    chosen = emask & (rank < TOP_K)
    wsel = jnp.where(chosen, scores, 0.0)
    wnorm = wsel / jnp.sum(wsel, axis=0, keepdims=True) * ROUTED_SCALE

    @pl.when(pl.program_id(0) == 0)
    def _():
        carry_ref[...] = jnp.zeros_like(carry_ref)

    chosen_f = jnp.where(chosen, 1.0, 0.0)
    before = carry_ref[...] + _dot(chosen_f.astype(BF16), tri_ref[...]) - chosen_f
    carry_ref[...] = carry_ref[...] + jnp.sum(chosen_f, axis=1, keepdims=True)
    cnt_ref[...] = jnp.broadcast_to(carry_ref[...], cnt_ref.shape).astype(jnp.int32)
    for k in range(TOP_K):
        hit = chosen & (rank == k)
        idx_ref[k:k + 1, :] = jnp.sum(jnp.where(hit, eid_f, 0.0), axis=0, keepdims=True).astype(jnp.int32)
        w_ref[k:k + 1, :] = jnp.sum(jnp.where(hit, wnorm, 0.0), axis=0, keepdims=True)
        rank_ref[k:k + 1, :] = jnp.sum(jnp.where(hit, before, 0.0), axis=0, keepdims=True).astype(jnp.int32)


def _route(logits_t, bias_col):
    e, n = logits_t.shape
    tn = ROW_TILE
    tri = jnp.asarray(np.triu(np.ones((tn, tn), np.float32)), BF16)
    slot = pl.BlockSpec((TOP_K, tn), lambda i: (0, i))
    return pl.pallas_call(
        _route_kernel,
        grid=(n // tn,),
        in_specs=[pl.BlockSpec((e, tn), lambda i: (0, i)), pl.BlockSpec((e, 1), lambda i: (0, 0)),
                  pl.BlockSpec((tn, tn), lambda i: (0, 0))],
        out_specs=[slot, slot, slot, pl.BlockSpec((e, LANE), lambda i: (0, 0))],
        out_shape=[
            jax.ShapeDtypeStruct((TOP_K, n), jnp.int32),
            jax.ShapeDtypeStruct((TOP_K, n), F32),
            jax.ShapeDtypeStruct((TOP_K, n), jnp.int32),
            jax.ShapeDtypeStruct((e, LANE), jnp.int32),
        ],
        scratch_shapes=[pltpu.VMEM((e, 1), F32)],
        compiler_params=_params(("arbitrary",)),
    )(logits_t, bias_col, tri)


def _dispatch_kernel(fill_lo_ref, fill_hi_ref, nused_ref, pos_ref, x_ref, xs_ref, zero_ref, sem, zsem):
    i = pl.program_id(0)
    tm = pos_ref.shape[1]
    sub = x_ref.shape[0] // tm
    blk_rows = zero_ref.shape[0]
    nblk = xs_ref.shape[0] // blk_rows

    def slot(r):
        return xs_ref.at[pl.ds(pl.multiple_of(r * sub, sub), sub), :]

    @pl.when(i == 0)
    def _():
        zero_ref[...] = jnp.zeros_like(zero_ref)

        def per_expert(e, carry):
            def fill(r, c2):
                pltpu.make_async_copy(zero_ref.at[pl.ds(0, sub), :], slot(r), zsem).start()
                return c2

            def drain(r, c2):
                pltpu.make_async_copy(zero_ref.at[pl.ds(0, sub), :], slot(r), zsem).wait()
                return c2

            lax.fori_loop(fill_lo_ref[e], fill_hi_ref[e], fill, 0)
            lax.fori_loop(fill_lo_ref[e], fill_hi_ref[e], drain, 0)
            return carry

        lax.fori_loop(0, fill_lo_ref.shape[0], per_expert, 0)

        def tail(bk, carry):
            cp = pltpu.make_async_copy(
                zero_ref, xs_ref.at[pl.ds(pl.multiple_of(bk * blk_rows, blk_rows), blk_rows), :], zsem)
            cp.start()
            cp.wait()
            return carry

        lax.fori_loop(nused_ref[0], nblk, tail, 0)

    def scatter(t, carry):
        src = x_ref.at[pl.ds(pl.multiple_of(t * sub, sub), sub), :]
        for k in range(TOP_K):
            pltpu.make_async_copy(src, slot(pos_ref[k, t]), sem).start()
        return carry

    lax.fori_loop(0, tm, scatter, 0)
    for k in range(TOP_K):
        pltpu.make_async_copy(x_ref, xs_ref.at[pl.ds(0, tm * sub), :], sem).wait()


def _dispatch(fill_lo, fill_hi, nused, pos_t, xr, p_rows, d):
    k, n = pos_t.shape
    sub = d // LANE
    tm = ROW_TILE
    grid_spec = pltpu.PrefetchScalarGridSpec(
        num_scalar_prefetch=3,
        grid=(n // tm,),
        in_specs=[
            pl.BlockSpec((k, tm), lambda i, *_: (0, i), memory_space=pltpu.SMEM),
            pl.BlockSpec((tm * sub, LANE), lambda i, *_: (i, 0)),
        ],
        out_specs=pl.BlockSpec(memory_space=pl.ANY),
        scratch_shapes=[pltpu.VMEM((MOE_ROWS * sub, LANE), F32), pltpu.SemaphoreType.DMA(()),
                        pltpu.SemaphoreType.DMA(())],
    )
    return pl.pallas_call(
        _dispatch_kernel,
        grid_spec=grid_spec,
        out_shape=jax.ShapeDtypeStruct((p_rows * sub, LANE), F32),
        compiler_params=_params(("arbitrary",)),
    )(fill_lo, fill_hi, nused, pos_t, xr)


def _moe_kernel(blk_e_ref, nused_ref, xs_ref, wg_ref, wu_ref, wd_ref, o_ref):
    i = pl.program_id(0)

    @pl.when(i < nused_ref[0])
    def _():
        sub = xs_ref.shape[0] // MOE_ROWS
        x = jnp.concatenate([xs_ref[pl.ds(c, MOE_ROWS, stride=sub), :].astype(BF16) for c in range(sub)], axis=1)
        hg = _dot(x, wg_ref[0])
        hid = (hg * jax.nn.sigmoid(hg)) * _dot(x, wu_ref[0])
        o_ref[...] = _dot(hid.astype(BF16), wd_ref[0]).astype(o_ref.dtype)

    @pl.when(i >= nused_ref[0])
    def _():
        o_ref[...] = jnp.zeros_like(o_ref)


def _moe_experts(blk_e, nused, xs, wg, wu, wd):
    d = wg.shape[1]
    sub = d // LANE
    p = xs.shape[0] // sub
    hid = wg.shape[2]
    grid_spec = pltpu.PrefetchScalarGridSpec(
        num_scalar_prefetch=2,
        grid=(p // MOE_ROWS,),
        in_specs=[
            pl.BlockSpec((MOE_ROWS * sub, LANE), lambda i, be, nu: (jnp.minimum(i, nu[0] - 1), 0)),
            pl.BlockSpec((1, d, hid), lambda i, be, nu: (be[i], 0, 0)),
            pl.BlockSpec((1, d, hid), lambda i, be, nu: (be[i], 0, 0)),
            pl.BlockSpec((1, hid, d), lambda i, be, nu: (be[i], 0, 0)),
        ],
        out_specs=pl.BlockSpec((MOE_ROWS, d), lambda i, be, nu: (i, 0)),
    )
    return pl.pallas_call(
        _moe_kernel,
        grid_spec=grid_spec,
        out_shape=jax.ShapeDtypeStruct((p, d), BF16),
        compiler_params=_params(("arbitrary",)),
    )(blk_e, nused, xs, wg, wu, wd)


def _final_kernel(h_ref, xn_ref, ys_ref, w_ref, p_ref, wsg, wsu, wsd, gp, wpg, wp, gfin, o_ref):
    xn = xn_ref[...]
    hg = _dot(xn, wsg[...])
    shared = _dot(((hg * jax.nn.sigmoid(hg)) * _dot(xn, wsu[...])).astype(BF16), wsd[...])
    w = w_ref[...]
    y = w[:, 0:1] * ys_ref[0].astype(F32)
    for k in range(1, TOP_K):
        y = y + w[:, k:k + 1] * ys_ref[k].astype(F32)
    h = h_ref[...] + (y + shared)
    gate = jax.nn.sigmoid(_dot(_rms(h, gp[...]).astype(BF16), wpg[...]))
    h = h + _dot(p_ref[...].astype(BF16), wp[...]) * gate
    o_ref[...] = _rms(h, gfin[...])


def _final(h1, xn2, ysg, w_nk, p2, wsg, wsu, wsd, gp, wpg, wp, gfin):
    n, d = h1.shape
    tm = ROW_TILE
    row = lambda w: pl.BlockSpec((tm, w), lambda i: (i, 0))
    full = lambda a: pl.BlockSpec(a.shape, lambda i: (0,) * a.ndim)
    return pl.pallas_call(
        _final_kernel,
        grid=(n // tm,),
        in_specs=[row(d), row(d), pl.BlockSpec((TOP_K, tm, d), lambda i: (0, i, 0)), row(TOP_K), row(p2.shape[1]),
                  full(wsg), full(wsu), full(wsd), full(gp), full(wpg), full(wp), full(gfin)],
        out_specs=row(d),
        out_shape=jax.ShapeDtypeStruct((n, d), F32),
        compiler_params=_params(("parallel",)),
    )(h1, xn2, ysg, w_nk, p2, wsg, wsu, wsd, gp, wpg, wp, gfin)


def _pad_heads(w):
    d, c = w.shape
    nh = c // HEAD_DIM
    w = w.reshape(d, nh, HEAD_DIM)
    return jnp.pad(w, ((0, 0), (0, 0), (0, LANE - HEAD_DIM))).reshape(d, nh * LANE)


def _alibi_slopes(n):
    return 2.0 ** (-8.0 * jnp.arange(1, n + 1, dtype=jnp.float32) / n)


def kernel(x, p, g_mix, w_in, cmp_pos_k, cmp_w1_k, cmp_b1_k, cmp_w2_k, cmp_pos_v, cmp_w1_v, cmp_b1_v, cmp_w2_v, w_branch_a, w_branch_b, w_out, g_ffn, w_router, router_bias, w_gate, w_up, w_down, ws_gate, ws_up, ws_down, g_ple, w_ple_gate, w_ple, g_final):
    B, S, D = x.shape
    N = B * S
    G, R = NSA_KV_HEADS, NSA_REP
    assert w_in.shape[0] == 1 and p.shape[0] == 1
    assert S % (DIL_BLOCK * DIL_PATTERNS[-1][1]) == 0 and S % NSA_TK == 0 and S >= WIN_SIZE + NSA_Q
    assert N % ROW_TILE == 0
    slopes = _alibi_slopes(DIL_HEADS + NSA_HEADS)
    slopes_a, slopes_b = slopes[0::2], slopes[1::2]

    offs = np.concatenate([[0], np.cumsum(IN_SIZES)])
    parts = [w_in[0][:, offs[i]:offs[i + 1]] for i in range(len(IN_SIZES))]
    w_qa, w_ka, w_va, w_qb, w_kc, w_vc, w_ks, w_vs, w_kw, w_vw, w_gn, w_ga, w_gb = parts
    w_act = jnp.concatenate([w_qa, w_ka, w_va]
                            + [_pad_heads(t) for t in (w_qb, w_kc, w_vc, w_ks, w_vs, w_kw, w_vw)],
                            axis=1).astype(BF16)
    act_w = w_act.shape[1]
    nsa_col0 = 3 * DIL_HEADS * HEAD_DIM // LANE
    w_gn = w_gn.reshape(D, 3, G, R).transpose(0, 2, 1, 3).reshape(D, G, 3 * R)
    w_gn = jnp.pad(w_gn, ((0, 0), (0, 0), (0, LANE - 3 * R))).reshape(D, G * LANE)
    w_gate_in = jnp.concatenate([w_ga, w_gb, w_gn], axis=1).astype(BF16)
    gate_w = w_gate_in.shape[1]

    x2 = x.reshape(N, D)
    dils = tuple(dil for _, dil in DIL_PATTERNS if dil > 1)
    act, *dilated_qkv = _inproj(x2, g_mix, w_act, BF16, 512, dils=dils, n_dilated=3)
    gt, = _inproj(x2, g_mix, w_gate_in, F32, 768)

    o_list, l_list = [], []
    for _, dil in DIL_PATTERNS:
        src = act if dil == 1 else dilated_qkv[dils.index(dil)]
        o, l = _dilated(src.reshape(B, S // dil, -1), slopes_a, dil, B, S)
        o_list.append(o.reshape(-1, N, LANE))
        l_list.append(l.reshape(-1, N, LANE))

    nchunk = S // CMP_STRIDE
    c0 = nsa_col0 * LANE + NSA_HEADS * LANE
    raw = act[:, c0:c0 + 2 * G * LANE].reshape(B, nchunk, CMP_STRIDE, 2 * G, LANE)[..., :HEAD_DIM]
    xc = raw.transpose(0, 3, 1, 2, 4).reshape(B, 2 * G, nchunk, CMP_STRIDE * HEAD_DIM)
    pos = jnp.stack([cmp_pos_k[0].reshape(1, -1), cmp_pos_v[0].reshape(1, -1)])
    w1 = jnp.stack([cmp_w1_k[0], cmp_w1_v[0]]).astype(BF16)
    b1 = jnp.stack([cmp_b1_k[0][None], cmp_b1_v[0][None]])
    w2 = jnp.pad(jnp.stack([cmp_w2_k[0], cmp_w2_v[0]]), ((0, 0), (0, 0), (0, LANE - HEAD_DIM))).astype(BF16)
    kcv = _compress(xc, pos, w1, b1, w2)
    kcv = kcv.reshape(B, 2, G, nchunk, LANE).transpose(1, 0, 2, 3, 4).reshape(2, B * G, nchunk, LANE)
    vct = kcv[1].transpose(0, 2, 1)
    v0 = (nsa_col0 + NSA_HEADS + 3 * G) * LANE
    vsw = jnp.stack([act[:, v0:v0 + G * LANE], act[:, v0 + 2 * G * LANE:v0 + 3 * G * LANE]])
    vsw = vsw.reshape(2, B, S // LANE, LANE, G, LANE).transpose(1, 4, 0, 2, 5, 3)
    ones_row = (jnp.arange(LANE) == SUM_LANE).astype(BF16)[:, None]
    vt = (vsw + ones_row).reshape(B * G, 2, S // LANE, LANE, LANE)
    yb = _nsa(slopes_b, act.reshape(B, S, act_w), gt.reshape(B, S, gate_w), kcv[0], vct, vt,
              B, S, nsa_col0, 2 * D // LANE)
    yb = yb.reshape(N, NSA_HEADS * LANE)

    wb_pad = jnp.pad(w_branch_b[0].reshape(NSA_HEADS, HEAD_DIM, D),
                     ((0, 0), (0, LANE - HEAD_DIM), (0, 0))).reshape(NSA_HEADS * LANE, D)
    wr_t = w_router[0].T
    wr_hi = wr_t.astype(BF16)
    wr_lo = (wr_t - wr_hi.astype(F32)).astype(BF16)
    h1, xn2, logits_t, xr = _merge(o_list, l_list, yb, gt, x2, w_branch_a[0].astype(BF16), wb_pad.astype(BF16),
                               w_out[0].astype(BF16), g_ffn, wr_hi, wr_lo)

    idx_t, wsel_t, rank_t, counts = _route(logits_t, router_bias[0][:, None])
    counts = counts[:, 0]
    padded = (counts + MOE_ROWS - 1) // MOE_ROWS * MOE_ROWS
    pend = jnp.cumsum(padded)
    pstart = pend - padded
    pos_t = rank_t + pstart[idx_t]
    NK = N * TOP_K
    nblk = (NK + N_EXPERTS * (MOE_ROWS - 1) + MOE_ROWS - 1) // MOE_ROWS
    P = nblk * MOE_ROWS
    blk_e = jnp.minimum(jnp.sum(pend[None, :] <= (jnp.arange(nblk, dtype=jnp.int32) * MOE_ROWS)[:, None], axis=1),
                        N_EXPERTS - 1).astype(jnp.int32)
    nused = (pend[-1] // MOE_ROWS).astype(jnp.int32).reshape(1)
    xs = _dispatch((pstart + counts).astype(jnp.int32), pend.astype(jnp.int32), nused, pos_t, xr, P, D)
    ys = _moe_experts(blk_e, nused, xs, w_gate[0].astype(BF16), w_up[0].astype(BF16), w_down[0].astype(BF16))
    out = _final(h1, xn2, ys[pos_t], wsel_t.T, p[0].reshape(N, -1), ws_gate[0].astype(BF16), ws_up[0].astype(BF16),
                 ws_down[0].astype(BF16), g_ple, w_ple_gate[0].astype(BF16), w_ple[0].astype(BF16),
                 g_final[None, :])
    return out.reshape(B, S, D)
```

```python
import functools

import numpy as np
import jax
import jax.numpy as jnp
from jax import lax
from jax.experimental import pallas as pl
from jax.experimental.pallas import tpu as pltpu

F32 = jnp.float32
BF16 = jnp.bfloat16

HEAD_DIM = 64
LANE = 128
RMS_EPS = 1e-6
DIL_HEADS = 8
DIL_PATTERNS = ((128, 1), (512, 4), (2048, 16))
DIL_BLOCK = 128
NSA_HEADS = 8
NSA_KV_HEADS = 2
NSA_REP = NSA_HEADS // NSA_KV_HEADS
CMP_BLOCK = 32
CMP_STRIDE = 16
CMP_HIDDEN = 256
SLC_BLOCK = 64
SLC_TOP_N = 16
SLC_FORCE = 1e9
WIN_SIZE = 512
N_EXPERTS = 64
TOP_K = 8
N_GROUPS = 8
TOP_GROUPS = 4
GROUP_SIZE = N_EXPERTS // N_GROUPS
ROUTED_SCALE = 2.5
IN_SIZES = (512, 512, 512, 512, 128, 128, 128, 128, 128, 128, 24, 1024, 1024)

NEG = -1e30
NT_DIMS = (((1,), (1,)), ((), ()))

ROW_TILE = 512
INPROJ_ROWS = 1024
NSA_Q = 128
NSA_TK = 512
MOE_ROWS = 512
VMEM_LIMIT = 48 * 1024 * 1024


def _dot(a, b):
    return jnp.dot(a, b, preferred_element_type=F32)


def _dot_nt(a, b):
    return lax.dot_general(a, b, NT_DIMS, preferred_element_type=F32)


def _split3(x):
    hi = x.astype(BF16)
    r1 = x - hi.astype(F32)
    mid = r1.astype(BF16)
    lo = (r1 - mid.astype(F32)).astype(BF16)
    return hi, mid, lo


def _rms(x, g):
    r = lax.rsqrt(jnp.mean(x * x, axis=-1, keepdims=True) + RMS_EPS)
    return (x * r) * g


def _params(sem):
    return pltpu.CompilerParams(dimension_semantics=sem, vmem_limit_bytes=VMEM_LIMIT)


def _inproj_kernel(x_ref, g_ref, w_ref, o_ref, *rest, dils, n_dilated):
    dil_refs, (xn_ref, res_ref) = rest[:len(dils)], rest[len(dils):]
    j = pl.program_id(1)

    @pl.when(j == 0)
    def _():
        xn_ref[...] = _rms(x_ref[...], g_ref[...]).astype(BF16)

    res = _dot(xn_ref[...], w_ref[...])
    o_ref[...] = res.astype(o_ref.dtype)

    if dils:
        @pl.when(j < n_dilated)
        def _():
            tn = res.shape[1]
            for c in range(tn // LANE):
                res_ref[c] = res[:, c * LANE:(c + 1) * LANE]
            for d_ref, dil in zip(dil_refs, dils):
                for r in range(dil):
                    for c in range(tn // LANE):
                        piece = res_ref[c, pl.ds(r, INPROJ_ROWS // dil, stride=dil), :]
                        d_ref[:, r * tn + c * LANE:r * tn + (c + 1) * LANE] = piece.astype(d_ref.dtype)


def _inproj(x2, g, w, out_dtype, tn, dils=(), n_dilated=0):
    n, d = x2.shape
    cols = w.shape[1]
    last = max(n_dilated - 1, 0)
    out_specs = [pl.BlockSpec((INPROJ_ROWS, tn), lambda i, j: (i, j))]
    out_shape = [jax.ShapeDtypeStruct((n, cols), out_dtype)]
    for dil in dils:
        out_specs.append(pl.BlockSpec((INPROJ_ROWS // dil, dil * tn), lambda i, j: (i, jnp.minimum(j, last))))
        out_shape.append(jax.ShapeDtypeStruct((n // dil, n_dilated * dil * tn), out_dtype))
    return pl.pallas_call(
        functools.partial(_inproj_kernel, dils=dils, n_dilated=n_dilated),
        grid=(n // INPROJ_ROWS, cols // tn),
        in_specs=[
            pl.BlockSpec((INPROJ_ROWS, d), lambda i, j: (i, 0)),
            pl.BlockSpec((1, d), lambda i, j: (0, 0)),
            pl.BlockSpec((d, tn), lambda i, j: (0, j)),
        ],
        out_specs=out_specs,
        out_shape=out_shape,
        scratch_shapes=[pltpu.VMEM((INPROJ_ROWS, d), BF16), pltpu.VMEM((tn // LANE, INPROJ_ROWS, LANE), F32)],
        compiler_params=_params(("parallel", "arbitrary")),
    )(x2, g, w)


def _dil_kernel(slope_ref, q_ref, kp_ref, kc_ref, vp_ref, vc_ref, o_ref, l_ref, *, dil):
    i = pl.program_id(1)
    res = pl.program_id(2)
    blk = DIL_BLOCK
    out_rows = pl.ds(res, blk, stride=dil) if dil > 1 else slice(None)
    qi = lax.broadcasted_iota(jnp.int32, (blk, 2 * blk), 0)
    kj = lax.broadcasted_iota(jnp.int32, (blk, 2 * blk), 1)
    dist = (qi + blk - kj).astype(F32)
    valid = (dist >= 0) & (dist <= blk) & ((kj >= blk) | (i > 0))
    lane = lax.broadcasted_iota(jnp.int32, (blk, LANE), 1)
    lo = lane < HEAD_DIM
    scale = HEAD_DIM ** -0.5
    for hp in range(DIL_HEADS // 2):
        sl = slice(hp * LANE, (hp + 1) * LANE)
        q2 = q_ref[0, :, sl]
        k2 = jnp.concatenate([kp_ref[0, :, sl], kc_ref[0, :, sl]], axis=0)
        v2 = jnp.concatenate([vp_ref[0, :, sl], vc_ref[0, :, sl]], axis=0)
        outs, lses = [], []
        for hh in range(2):
            keep = lo if hh == 0 else jnp.logical_not(lo)
            qm = jnp.where(keep, q2, jnp.zeros_like(q2))
            slope = slope_ref[2 * hp + hh] * float(dil)
            s = _dot_nt(qm, k2) * scale - slope * dist
            s = jnp.where(valid, s, NEG)
            m = jnp.max(s, axis=-1, keepdims=True)
            e = jnp.exp(s - m)
            den = jnp.sum(e, axis=-1, keepdims=True)
            outs.append(_dot(e.astype(BF16), v2) / den)
            lses.append(jnp.broadcast_to(m + jnp.log(den), (blk, LANE)))
        o_ref[hp, 0, out_rows, :] = jnp.where(lo, outs[0], outs[1])
        l_ref[hp, 0, out_rows, :] = jnp.where(lo, lses[0], lses[1])


def _dilated(src3, slopes_a, dil, b, s):
    nblk = s // dil // DIL_BLOCK
    aw = DIL_HEADS * HEAD_DIM
    npair = DIL_HEADS // 2
    blk = (1, DIL_BLOCK, aw)

    def spec(which, prev):
        if prev:
            return pl.BlockSpec(blk, lambda bb, i, r: (bb, jnp.maximum(i - 1, 0), which * dil + r))
        return pl.BlockSpec(blk, lambda bb, i, r: (bb, i, which * dil + r))

    out_spec = pl.BlockSpec((npair, 1, DIL_BLOCK * dil, LANE), lambda bb, i, r: (0, bb, i, 0))
    return pl.pallas_call(
        functools.partial(_dil_kernel, dil=dil),
        grid=(b, nblk, dil),
        in_specs=[
            pl.BlockSpec(memory_space=pltpu.SMEM),
            spec(0, False), spec(1, True), spec(1, False), spec(2, True), spec(2, False),
        ],
        out_specs=[out_spec, out_spec],
        out_shape=[jax.ShapeDtypeStruct((npair, b, s, LANE), F32)] * 2,
        compiler_params=_params(("parallel", "arbitrary", "arbitrary")),
    )(slopes_a, src3, src3, src3, src3, src3)


def _cmp_kernel(x_ref, pos_ref, w1_ref, b1_ref, w2_ref, o_ref):
    half = x_ref.shape[-1]
    x = x_ref[0, 0]
    w1 = w1_ref[0]
    first = _dot(x, w1[:half])
    second = _dot(x, w1[half:])
    nrow = x.shape[0]
    pre = first + pltpu.roll(second, nrow - 1, 0)
    posb = _dot(jnp.broadcast_to(pos_ref[0], (8, 2 * half)).astype(BF16), w1)[0:1]
    pre = pre + posb + b1_ref[0]
    o_ref[0, 0] = _dot(jax.nn.gelu(pre).astype(BF16), w2_ref[0]).astype(o_ref.dtype)


def _compress(xc, pos, w1, b1, w2):
    b, four, nchunk, width = xc.shape
    return pl.pallas_call(
        _cmp_kernel,
        grid=(b, four),
        in_specs=[
            pl.BlockSpec((1, 1, nchunk, width), lambda bb, j: (bb, j, 0, 0)),
            pl.BlockSpec((1, 1, 2 * width), lambda bb, j: (j // 2, 0, 0)),
            pl.BlockSpec((1, 2 * width, CMP_HIDDEN), lambda bb, j: (j // 2, 0, 0)),
            pl.BlockSpec((1, 1, CMP_HIDDEN), lambda bb, j: (j // 2, 0, 0)),
            pl.BlockSpec((1, CMP_HIDDEN, LANE), lambda bb, j: (j // 2, 0, 0)),
        ],
        out_specs=pl.BlockSpec((1, 1, nchunk, LANE), lambda bb, j: (bb, j, 0, 0)),
        out_shape=jax.ShapeDtypeStruct((b, four, nchunk, LANE), BF16),
        compiler_params=_params(("parallel", "parallel")),
    )(xc, pos, w1, b1, w2)


POS_LANE = HEAD_DIM
POS_SPLIT = 64
SUM_LANE = HEAD_DIM
MASKED = -1e30
ROW_FLOOR = -1e29


def _pos_columns(pos):
    out = np.zeros((pos.shape[0], LANE), np.float32)
    out[:, POS_LANE] = pos // POS_SPLIT
    out[:, POS_LANE + 1] = pos % POS_SPLIT
    out[:, POS_LANE + 2] = 1.0
    out[:, POS_LANE + 3] = 1.0
    return jnp.asarray(out, BF16)


def _nsa_kernel(slope_ref, q_ref, kc_ref, vct_ref, ks_ref, kw_ref, vt_ref, gt_ref, ov_ref, kpa_ref, cpa_ref,
                oh_ref, o_ref, kaug_ref, flag_ref, m_ref, acc_ref, *, seq):
    g = pl.program_id(1)
    c = pl.program_id(2)
    C = NSA_Q
    R = NSA_REP
    TK = NSA_TK
    rows = R * C
    t0 = c * C
    scale = HEAD_DIM ** -0.5
    nb = seq // SLC_BLOCK
    ncmp = seq // CMP_STRIDE
    n_sel = min(SLC_TOP_N, nb)
    vblocks = TK // LANE

    @pl.when(c == 0)
    def _():
        kaug_ref[:, 0:LANE] = ks_ref[0] + kpa_ref[...]
        kaug_ref[:, LANE:2 * LANE] = oh_ref[...]

    row = lax.broadcasted_iota(jnp.int32, (rows, 1), 0)
    ti = t0 + row % C
    slope = jnp.zeros((rows, 1), F32)
    for r in range(R):
        slope = jnp.where(row // C == r, slope_ref[g * R + r], slope)
    lane = lax.broadcasted_iota(jnp.int32, (rows, LANE), 1)
    thi = (ti // POS_SPLIT).astype(F32)
    tlo = (ti % POS_SPLIT).astype(F32)
    qcols = jnp.where(lane == POS_LANE, POS_SPLIT * slope,
                      jnp.where(lane == POS_LANE + 1, slope,
                                jnp.where(lane == POS_LANE + 2, -POS_SPLIT * slope * thi,
                                          jnp.where(lane == POS_LANE + 3, -slope * tlo, 0.0))))
    q4 = jnp.concatenate([q_ref[0, :, r * LANE:(r + 1) * LANE] for r in range(R)], axis=0)
    q4 = (q4.astype(F32) * scale + qcols).astype(BF16)
    tq = t0 + lax.broadcasted_iota(jnp.int32, (1, C), 1)

    def heads(x):
        return jnp.concatenate([x] * R, axis=1)

    kc = kc_ref[0] + cpa_ref[...]
    ci = lax.broadcasted_iota(jnp.int32, (ncmp, 1), 0)
    vis = (ci * CMP_STRIDE + (CMP_BLOCK - 1) <= tq) & (ci < ncmp - 1)
    s = _dot_nt(kc, q4) + heads(jnp.where(vis, 0.0, MASKED))
    m = jnp.maximum(jnp.max(s, axis=0, keepdims=True), ROW_FLOOR)
    e = jnp.exp(s - m)
    den = jnp.sum(e, axis=0, keepdims=True)
    pc = e * (1.0 / jnp.where(den > 0, den, 1.0))
    o_cmp = _dot(vct_ref[0], pc.astype(BF16))
    pcsum = pc[:, 0:C]
    for r in range(1, R):
        pcsum = pcsum + pc[:, r * C:(r + 1) * C]
    ov = ov_ref[...]
    p_slc = sum(_dot(ov, part) for part in _split3(pcsum))

    j = lax.broadcasted_iota(jnp.int32, (LANE, C), 0)
    cur = tq // SLC_BLOCK
    forced = (j == 0) | (j == cur) | (j == cur - 1)
    future = j * SLC_BLOCK > tq
    work = jnp.where(forced, SLC_FORCE, jnp.where(future, -SLC_FORCE, p_slc))
    work = jnp.where(j < nb, work, -jnp.inf)
    sel = jnp.zeros((LANE, C), F32)
    jf = j.astype(F32)
    for _ in range(n_sel):
        mx = jnp.max(work, axis=0, keepdims=True)
        first = jnp.min(jnp.where(work == mx, jf, float(LANE)), axis=0, keepdims=True)
        pick = jf == first
        sel = jnp.where(pick, 1.0, sel)
        work = jnp.where(pick, -jnp.inf, work)
    sel = jnp.where(future, 0.0, sel)
    blocks_per_tile = TK // SLC_BLOCK
    for kt in range(seq // TK):
        flag_ref[kt] = jnp.max(sel[kt * blocks_per_tile:(kt + 1) * blocks_per_tile, :])
    sel_bias = jnp.where(sel.T > 0.5, 0.0, MASKED).astype(BF16)
    q_aug = jnp.concatenate([q4, jnp.concatenate([sel_bias] * R, axis=0)], axis=1)

    m_ref[...] = jnp.full((1, rows), ROW_FLOOR, F32)
    acc_ref[...] = jnp.zeros((LANE, rows), F32)
    kio = lax.broadcasted_iota(jnp.int32, (TK, 1), 0)

    def values_t(which, first_block, n):
        return jnp.concatenate([vt_ref[0, which, first_block + i] for i in range(n)], axis=1)

    def tile_update(kt, causal):
        start = pl.multiple_of(kt * TK, TK)
        s2 = _dot_nt(kaug_ref[pl.ds(start, TK), :], q_aug)
        if causal:
            s2 = s2 + heads(jnp.where(start + kio <= tq, 0.0, MASKED))
        m_i = m_ref[...]
        m_n = jnp.maximum(m_i, jnp.max(s2, axis=0, keepdims=True))
        alpha = jnp.exp(m_i - m_n)
        p2 = jnp.exp(s2 - m_n)
        acc_ref[...] = alpha * acc_ref[...] + _dot(values_t(0, kt * vblocks, vblocks), p2.astype(BF16))
        m_ref[...] = m_n

    def slc_body(kt, carry):
        @pl.when(flag_ref[kt] > 0.0)
        def _():
            tile_update(kt, False)

        return carry

    last = (t0 + C - 1) // TK
    lax.fori_loop(0, last, slc_body, 0)
    tile_update(last, True)
    acc = acc_ref[...]
    o_slc = acc / acc[SUM_LANE:SUM_LANE + 1, :]

    span = WIN_SIZE + C
    ws = pl.multiple_of(jnp.maximum(t0 - WIN_SIZE, 0), C)
    k_w = kw_ref[0, pl.ds(ws, span), :] + kpa_ref[pl.ds(ws, span), :]
    wpos = ws + lax.broadcasted_iota(jnp.int32, (span, 1), 0)
    wbias = jnp.where((wpos <= tq) & (wpos >= tq - (WIN_SIZE - 1)), 0.0, MASKED)
    s3 = _dot_nt(k_w, q4) + heads(wbias)
    e3 = jnp.exp(s3 - jnp.max(s3, axis=0, keepdims=True))
    ow = _dot(values_t(1, ws // LANE, span // LANE), e3.astype(BF16))
    o_win = ow / ow[SUM_LANE:SUM_LANE + 1, :]

    gates = jax.nn.sigmoid(gt_ref[0]).T
    for r in range(R):
        rs = slice(r * C, (r + 1) * C)
        out = (gates[r:r + 1] * o_cmp[:, rs] + gates[R + r:R + r + 1] * o_slc[:, rs]
               + gates[2 * R + r:2 * R + r + 1] * o_win[:, rs])
        o_ref[0, :, r * LANE:(r + 1) * LANE] = out.T.astype(o_ref.dtype)


def _nsa(slopes_b, pa3, gt3, kc, vct, vt, b, s, col0, gcol0):
    C = NSA_Q
    R = NSA_REP
    G = NSA_KV_HEADS
    nb = s // SLC_BLOCK
    ncmp = s // CMP_STRIDE
    assert nb <= LANE
    qc0 = col0 // R
    kv0 = col0 + NSA_HEADS
    rows = R * C

    ci = np.arange(ncmp)[None, :]
    sj = np.arange(LANE)[:, None]
    ov = ((ci * CMP_STRIDE < (sj + 1) * SLC_BLOCK) & (ci * CMP_STRIDE + CMP_BLOCK > sj * SLC_BLOCK)
          & (ci < ncmp - 1) & (sj < nb))
    ov = jnp.asarray(ov, BF16)
    kpa = _pos_columns(np.arange(s))
    cpa = _pos_columns(np.arange(ncmp) * CMP_STRIDE + (CMP_BLOCK - 1))
    oh = jnp.asarray(np.arange(s)[:, None] // SLC_BLOCK == np.arange(LANE)[None, :], BF16)

    def k_spec(which):
        return pl.BlockSpec((1, s, LANE), lambda bb, g, c: (bb, 0, kv0 + which * G + g))

    const = lambda a: pl.BlockSpec(a.shape, lambda bb, g, c: (0,) * a.ndim)
    return pl.pallas_call(
        functools.partial(_nsa_kernel, seq=s),
        grid=(b, G, s // C),
        in_specs=[
            pl.BlockSpec(memory_space=pltpu.SMEM),
            pl.BlockSpec((1, C, R * LANE), lambda bb, g, c: (bb, c, qc0 + g)),
            pl.BlockSpec((1, ncmp, LANE), lambda bb, g, c: (bb * G + g, 0, 0)),
            pl.BlockSpec((1, LANE, ncmp), lambda bb, g, c: (bb * G + g, 0, 0)),
            k_spec(2), k_spec(4),
            pl.BlockSpec((1, 2, s // LANE, LANE, LANE), lambda bb, g, c: (bb * G + g, 0, 0, 0, 0)),
            pl.BlockSpec((1, C, LANE), lambda bb, g, c: (bb, c, gcol0 + g)),
            const(ov), const(kpa), const(cpa), const(oh),
        ],
        out_specs=pl.BlockSpec((1, C, R * LANE), lambda bb, g, c: (bb, c, g)),
        out_shape=jax.ShapeDtypeStruct((b, s, NSA_HEADS * LANE), BF16),
        scratch_shapes=[
            pltpu.VMEM((s, 2 * LANE), BF16),
            pltpu.SMEM((s // NSA_TK,), F32),
            pltpu.VMEM((1, rows), F32),
            pltpu.VMEM((LANE, rows), F32),
        ],
        compiler_params=_params(("parallel", "parallel", "arbitrary")),
    )(slopes_b, pa3, kc, vct, pa3, pa3, vt, gt3, ov, kpa, cpa, oh)


def _merge_kernel(o1, o2, o3, l1, l2, l3, yb, ga, gb, x_ref, wa, wb, wo, gf, wrh, wrl,
                  h_ref, xn_ref, lg_ref, xr_ref):
    def pairs(ref):
        return jnp.concatenate([ref[h] for h in range(ref.shape[0])], axis=1)

    a1, a2, a3 = pairs(l1), pairs(l2), pairs(l3)
    mx = jnp.maximum(jnp.maximum(a1, a2), a3)
    e1, e2, e3 = jnp.exp(a1 - mx), jnp.exp(a2 - mx), jnp.exp(a3 - mx)
    tot = e1 + e2 + e3
    ya = (e1 / tot) * pairs(o1) + (e2 / tot) * pairs(o2) + (e3 / tot) * pairs(o3)
    merged = (jax.nn.sigmoid(ga[...]) * _dot(ya.astype(BF16), wa[...])
              + jax.nn.sigmoid(gb[...]) * _dot(yb[...], wb[...]))
    h = x_ref[...] + _dot(merged.astype(BF16), wo[...])
    h_ref[...] = h
    xn = _rms(h, gf[...])
    xn_ref[...] = xn.astype(BF16)
    tm, d = xn.shape
    for c in range(d // LANE):
        xr_ref[pl.ds(c, tm, stride=d // LANE), :] = xn[:, c * LANE:(c + 1) * LANE]
    xh = xn.astype(BF16)
    xl = (xn - xh.astype(F32)).astype(BF16)
    lg_ref[...] = _dot_nt(wrh[...], xh) + _dot_nt(wrh[...], xl) + _dot_nt(wrl[...], xh)


def _merge(o_list, l_list, yb, gt, x2, wa, wb, wo, gf, wrh, wrl):
    n, d = x2.shape
    aw = wa.shape[0]
    bw = wb.shape[0]
    tm = ROW_TILE
    row = lambda w: pl.BlockSpec((tm, w), lambda i: (i, 0))
    full = lambda a: pl.BlockSpec(a.shape, lambda i: (0,) * a.ndim)
    return pl.pallas_call(
        _merge_kernel,
        grid=(n // tm,),
        in_specs=[pl.BlockSpec((aw // LANE, tm, LANE), lambda i: (0, i, 0))] * 6 + [
            row(bw),
            pl.BlockSpec((tm, d), lambda i: (i, 0)),
            pl.BlockSpec((tm, d), lambda i: (i, 1)),
            row(d), full(wa), full(wb), full(wo), full(gf), full(wrh), full(wrl),
        ],
        out_specs=[row(d), row(d), pl.BlockSpec((N_EXPERTS, tm), lambda i: (0, i)),
                   pl.BlockSpec((tm * d // LANE, LANE), lambda i: (i, 0))],
        out_shape=[
            jax.ShapeDtypeStruct((n, d), F32),
            jax.ShapeDtypeStruct((n, d), BF16),
            jax.ShapeDtypeStruct((N_EXPERTS, n), F32),
            jax.ShapeDtypeStruct((n * d // LANE, LANE), F32),
        ],
        compiler_params=_params(("parallel",)),
    )(*o_list, *l_list, yb, gt, gt, x2, wa, wb, wo, gf, wrh, wrl)


def _route_kernel(lg_ref, bias_ref, tri_ref, idx_ref, w_ref, rank_ref, cnt_ref, carry_ref):
    E = N_EXPERTS
    tn = lg_ref.shape[1]
    scores = jax.nn.sigmoid(lg_ref[...])
    sel = scores + bias_ref[...]
    sub = lax.broadcasted_iota(jnp.int32, (GROUP_SIZE, tn), 0).astype(F32)
    gs = []
    for grp in range(N_GROUPS):
        blk = sel[grp * GROUP_SIZE:(grp + 1) * GROUP_SIZE]
        m1 = jnp.max(blk, axis=0, keepdims=True)
        first = jnp.min(jnp.where(blk == m1, sub, float(GROUP_SIZE)), axis=0, keepdims=True)
        m2 = jnp.max(jnp.where(sub == first, -jnp.inf, blk), axis=0, keepdims=True)
        gs.append(m1 + m2)
    keep = []
    for a in range(N_GROUPS):
        rank = jnp.zeros((1, tn), jnp.int32)
        for o in range(N_GROUPS):
            if o == a:
                continue
            beats = (gs[o] >= gs[a]) if o < a else (gs[o] > gs[a])
            rank = rank + beats.astype(jnp.int32)
        keep.append(jnp.broadcast_to(rank < TOP_GROUPS, (GROUP_SIZE, tn)))
    emask = jnp.concatenate(keep, axis=0)
    cand = jnp.where(emask, sel, -jnp.inf)
    eid = lax.broadcasted_iota(jnp.int32, (E, tn), 0)
    eid_f = eid.astype(F32)
    rank = jnp.zeros((E, tn), jnp.int32)
    for o in range(E):
        other = cand[o:o + 1]
        beats = (other > cand) | ((other == cand) & (eid > o))
        rank = rank + beats.astype(jnp.int32)
    chosen = emask & (rank < TOP_K)
    wsel = jnp.where(chosen, scores, 0.0)
    wnorm = wsel / jnp.sum(wsel, axis=0, keepdims=True) * ROUTED_SCALE

    @pl.when(pl.program_id(0) == 0)
    def _():
        carry_ref[...] = jnp.zeros_like(carry_ref)

    chosen_f = jnp.where(chosen, 1.0, 0.0)
    before = carry_ref[...] + _dot(chosen_f.astype(BF16), tri_ref[...]) - chosen_f
    carry_ref[...] = carry_ref[...] + jnp.sum(chosen_f, axis=1, keepdims=True)
    cnt_ref[...] = jnp.broadcast_to(carry_ref[...], cnt_ref.shape).astype(jnp.int32)
    for k in range(TOP_K):
        hit = chosen & (rank == k)
        idx_ref[k:k + 1, :] = jnp.sum(jnp.where(hit, eid_f, 0.0), axis=0, keepdims=True).astype(jnp.int32)
        w_ref[k:k + 1, :] = jnp.sum(jnp.where(hit, wnorm, 0.0), axis=0, keepdims=True)
        rank_ref[k:k + 1, :] = jnp.sum(jnp.where(hit, before, 0.0), axis=0, keepdims=True).astype(jnp.int32)


def _route(logits_t, bias_col):
    e, n = logits_t.shape
    tn = ROW_TILE
    tri = jnp.asarray(np.triu(np.ones((tn, tn), np.float32)), BF16)
    slot = pl.BlockSpec((TOP_K, tn), lambda i: (0, i))
    return pl.pallas_call(
        _route_kernel,
        grid=(n // tn,),
        in_specs=[pl.BlockSpec((e, tn), lambda i: (0, i)), pl.BlockSpec((e, 1), lambda i: (0, 0)),
                  pl.BlockSpec((tn, tn), lambda i: (0, 0))],
        out_specs=[slot, slot, slot, pl.BlockSpec((e, LANE), lambda i: (0, 0))],
        out_shape=[
            jax.ShapeDtypeStruct((TOP_K, n), jnp.int32),
            jax.ShapeDtypeStruct((TOP_K, n), F32),
            jax.ShapeDtypeStruct((TOP_K, n), jnp.int32),
            jax.ShapeDtypeStruct((e, LANE), jnp.int32),
        ],
        scratch_shapes=[pltpu.VMEM((e, 1), F32)],
        compiler_params=_params(("arbitrary",)),
    )(logits_t, bias_col, tri)


def _dispatch_kernel(fill_lo_ref, fill_hi_ref, nused_ref, pos_ref, x_ref, xs_ref, zero_ref, sem, zsem):
    i = pl.program_id(0)
    tm = pos_ref.shape[1]
    sub = x_ref.shape[0] // tm
    blk_rows = zero_ref.shape[0]
    nblk = xs_ref.shape[0] // blk_rows

    def slot(r):
        return xs_ref.at[pl.ds(pl.multiple_of(r * sub, sub), sub), :]

    @pl.when(i == 0)
    def _():
        zero_ref[...] = jnp.zeros_like(zero_ref)

        def per_expert(e, carry):
            def fill(r, c2):
                pltpu.make_async_copy(zero_ref.at[pl.ds(0, sub), :], slot(r), zsem).start()
                return c2

            def drain(r, c2):
                pltpu.make_async_copy(zero_ref.at[pl.ds(0, sub), :], slot(r), zsem).wait()
                return c2

            lax.fori_loop(fill_lo_ref[e], fill_hi_ref[e], fill, 0)
            lax.fori_loop(fill_lo_ref[e], fill_hi_ref[e], drain, 0)
            return carry

        lax.fori_loop(0, fill_lo_ref.shape[0], per_expert, 0)

        def tail(bk, carry):
            cp = pltpu.make_async_copy(
                zero_ref, xs_ref.at[pl.ds(pl.multiple_of(bk * blk_rows, blk_rows), blk_rows), :], zsem)
            cp.start()
            cp.wait()
            return carry

        lax.fori_loop(nused_ref[0], nblk, tail, 0)

    def scatter(t, carry):
        src = x_ref.at[pl.ds(pl.multiple_of(t * sub, sub), sub), :]
        for k in range(TOP_K):
            pltpu.make_async_copy(src, slot(pos_ref[k, t]), sem).start()
        return carry

    lax.fori_loop(0, tm, scatter, 0)
    for k in range(TOP_K):
        pltpu.make_async_copy(x_ref, xs_ref.at[pl.ds(0, tm * sub), :], sem).wait()


def _dispatch(fill_lo, fill_hi, nused, pos_t, xr, p_rows, d):
    k, n = pos_t.shape
    sub = d // LANE
    tm = ROW_TILE
    grid_spec = pltpu.PrefetchScalarGridSpec(
        num_scalar_prefetch=3,
        grid=(n // tm,),
        in_specs=[
            pl.BlockSpec((k, tm), lambda i, *_: (0, i), memory_space=pltpu.SMEM),
            pl.BlockSpec((tm * sub, LANE), lambda i, *_: (i, 0)),
        ],
        out_specs=pl.BlockSpec(memory_space=pl.ANY),
        scratch_shapes=[pltpu.VMEM((MOE_ROWS * sub, LANE), F32), pltpu.SemaphoreType.DMA(()),
                        pltpu.SemaphoreType.DMA(())],
    )
    return pl.pallas_call(
        _dispatch_kernel,
        grid_spec=grid_spec,
        out_shape=jax.ShapeDtypeStruct((p_rows * sub, LANE), F32),
        compiler_params=_params(("arbitrary",)),
    )(fill_lo, fill_hi, nused, pos_t, xr)


def _moe_kernel(blk_e_ref, nused_ref, xs_ref, wg_ref, wu_ref, wd_ref, o_ref):
    i = pl.program_id(0)

    @pl.when(i < nused_ref[0])
    def _():
        sub = xs_ref.shape[0] // MOE_ROWS
        x = jnp.concatenate([xs_ref[pl.ds(c, MOE_ROWS, stride=sub), :].astype(BF16) for c in range(sub)], axis=1)
        hg = _dot(x, wg_ref[0])
        hid = (hg * jax.nn.sigmoid(hg)) * _dot(x, wu_ref[0])
        o_ref[...] = _dot(hid.astype(BF16), wd_ref[0]).astype(o_ref.dtype)

    @pl.when(i >= nused_ref[0])
    def _():
        o_ref[...] = jnp.zeros_like(o_ref)


def _moe_experts(blk_e, nused, xs, wg, wu, wd):
    d = wg.shape[1]
    sub = d // LANE
    p = xs.shape[0] // sub
    hid = wg.shape[2]
    grid_spec = pltpu.PrefetchScalarGridSpec(
        num_scalar_prefetch=2,
        grid=(p // MOE_ROWS,),
        in_specs=[
            pl.BlockSpec((MOE_ROWS * sub, LANE), lambda i, be, nu: (jnp.minimum(i, nu[0] - 1), 0)),
            pl.BlockSpec((1, d, hid), lambda i, be, nu: (be[i], 0, 0)),
            pl.BlockSpec((1, d, hid), lambda i, be, nu: (be[i], 0, 0)),
            pl.BlockSpec((1, hid, d), lambda i, be, nu: (be[i], 0, 0)),
        ],
        out_specs=pl.BlockSpec((MOE_ROWS, d), lambda i, be, nu: (i, 0)),
    )
    return pl.pallas_call(
        _moe_kernel,
        grid_spec=grid_spec,
        out_shape=jax.ShapeDtypeStruct((p, d), BF16),
        compiler_params=_params(("arbitrary",)),
    )(blk_e, nused, xs, wg, wu, wd)


def _final_kernel(h_ref, xn_ref, ys_ref, w_ref, p_ref, wsg, wsu, wsd, gp, wpg, wp, gfin, o_ref):
    xn = xn_ref[...]
    hg = _dot(xn, wsg[...])
    shared = _dot(((hg * jax.nn.sigmoid(hg)) * _dot(xn, wsu[...])).astype(BF16), wsd[...])
    w = w_ref[...]
    y = w[:, 0:1] * ys_ref[0].astype(F32)
    for k in range(1, TOP_K):
        y = y + w[:, k:k + 1] * ys_ref[k].astype(F32)
    h = h_ref[...] + (y + shared)
    gate = jax.nn.sigmoid(_dot(_rms(h, gp[...]).astype(BF16), wpg[...]))
    h = h + _dot(p_ref[...].astype(BF16), wp[...]) * gate
    o_ref[...] = _rms(h, gfin[...])


def _final(h1, xn2, ysg, w_nk, p2, wsg, wsu, wsd, gp, wpg, wp, gfin):
    n, d = h1.shape
    tm = ROW_TILE
    row = lambda w: pl.BlockSpec((tm, w), lambda i: (i, 0))
    full = lambda a: pl.BlockSpec(a.shape, lambda i: (0,) * a.ndim)
    return pl.pallas_call(
        _final_kernel,
        grid=(n // tm,),
        in_specs=[row(d), row(d), pl.BlockSpec((TOP_K, tm, d), lambda i: (0, i, 0)), row(TOP_K), row(p2.shape[1]),
                  full(wsg), full(wsu), full(wsd), full(gp), full(wpg), full(wp), full(gfin)],
        out_specs=row(d),
        out_shape=jax.ShapeDtypeStruct((n, d), F32),
        compiler_params=_params(("parallel",)),
    )(h1, xn2, ysg, w_nk, p2, wsg, wsu, wsd, gp, wpg, wp, gfin)


def _pad_heads(w):
    d, c = w.shape
    nh = c // HEAD_DIM
    w = w.reshape(d, nh, HEAD_DIM)
    return jnp.pad(w, ((0, 0), (0, 0), (0, LANE - HEAD_DIM))).reshape(d, nh * LANE)


def _alibi_slopes(n):
    return 2.0 ** (-8.0 * jnp.arange(1, n + 1, dtype=jnp.float32) / n)


def kernel(x, p, g_mix, w_in, cmp_pos_k, cmp_w1_k, cmp_b1_k, cmp_w2_k, cmp_pos_v, cmp_w1_v, cmp_b1_v, cmp_w2_v, w_branch_a, w_branch_b, w_out, g_ffn, w_router, router_bias, w_gate, w_up, w_down, ws_gate, ws_up, ws_down, g_ple, w_ple_gate, w_ple, g_final):
    B, S, D = x.shape
    N = B * S
    G, R = NSA_KV_HEADS, NSA_REP
    assert w_in.shape[0] == 1 and p.shape[0] == 1
    assert S % (DIL_BLOCK * DIL_PATTERNS[-1][1]) == 0 and S % NSA_TK == 0 and S >= WIN_SIZE + NSA_Q
    assert N % ROW_TILE == 0 and N % INPROJ_ROWS == 0 and S % INPROJ_ROWS == 0
    slopes = _alibi_slopes(DIL_HEADS + NSA_HEADS)
    slopes_a, slopes_b = slopes[0::2], slopes[1::2]

    offs = np.concatenate([[0], np.cumsum(IN_SIZES)])
    parts = [w_in[0][:, offs[i]:offs[i + 1]] for i in range(len(IN_SIZES))]
    w_qa, w_ka, w_va, w_qb, w_kc, w_vc, w_ks, w_vs, w_kw, w_vw, w_gn, w_ga, w_gb = parts
    w_act = jnp.concatenate([w_qa, w_ka, w_va]
                            + [_pad_heads(t) for t in (w_qb, w_kc, w_vc, w_ks, w_vs, w_kw, w_vw)],
                            axis=1).astype(BF16)
    act_w = w_act.shape[1]
    nsa_col0 = 3 * DIL_HEADS * HEAD_DIM // LANE
    w_gn = w_gn.reshape(D, 3, G, R).transpose(0, 2, 1, 3).reshape(D, G, 3 * R)
    w_gn = jnp.pad(w_gn, ((0, 0), (0, 0), (0, LANE - 3 * R))).reshape(D, G * LANE)
    w_gate_in = jnp.concatenate([w_ga, w_gb, w_gn], axis=1).astype(BF16)
    gate_w = w_gate_in.shape[1]

    x2 = x.reshape(N, D)
    dils = tuple(dil for _, dil in DIL_PATTERNS if dil > 1)
    act, *dilated_qkv = _inproj(x2, g_mix, w_act, BF16, 512, dils=dils, n_dilated=3)
    gt, = _inproj(x2, g_mix, w_gate_in, F32, 768)

    o_list, l_list = [], []
    for _, dil in DIL_PATTERNS:
        src = act if dil == 1 else dilated_qkv[dils.index(dil)]
        o, l = _dilated(src.reshape(B, S // dil, -1), slopes_a, dil, B, S)
        o_list.append(o.reshape(-1, N, LANE))
        l_list.append(l.reshape(-1, N, LANE))

    nchunk = S // CMP_STRIDE
    c0 = nsa_col0 * LANE + NSA_HEADS * LANE
    raw = act[:, c0:c0 + 2 * G * LANE].reshape(B, nchunk, CMP_STRIDE, 2 * G, LANE)[..., :HEAD_DIM]
    xc = raw.transpose(0, 3, 1, 2, 4).reshape(B, 2 * G, nchunk, CMP_STRIDE * HEAD_DIM)
    pos = jnp.stack([cmp_pos_k[0].reshape(1, -1), cmp_pos_v[0].reshape(1, -1)])
    w1 = jnp.stack([cmp_w1_k[0], cmp_w1_v[0]]).astype(BF16)
    b1 = jnp.stack([cmp_b1_k[0][None], cmp_b1_v[0][None]])
    w2 = jnp.pad(jnp.stack([cmp_w2_k[0], cmp_w2_v[0]]), ((0, 0), (0, 0), (0, LANE - HEAD_DIM))).astype(BF16)
    kcv = _compress(xc, pos, w1, b1, w2)
    kcv = kcv.reshape(B, 2, G, nchunk, LANE).transpose(1, 0, 2, 3, 4).reshape(2, B * G, nchunk, LANE)
    vct = kcv[1].transpose(0, 2, 1)
    v0 = (nsa_col0 + NSA_HEADS + 3 * G) * LANE
    vsw = jnp.stack([act[:, v0:v0 + G * LANE], act[:, v0 + 2 * G * LANE:v0 + 3 * G * LANE]])
    vsw = vsw.reshape(2, B, S // LANE, LANE, G, LANE).transpose(1, 4, 0, 2, 5, 3)
    ones_row = (jnp.arange(LANE) == SUM_LANE).astype(BF16)[:, None]
    vt = (vsw + ones_row).reshape(B * G, 2, S // LANE, LANE, LANE)
    yb = _nsa(slopes_b, act.reshape(B, S, act_w), gt.reshape(B, S, gate_w), kcv[0], vct, vt,
              B, S, nsa_col0, 2 * D // LANE)
    yb = yb.reshape(N, NSA_HEADS * LANE)

    wb_pad = jnp.pad(w_branch_b[0].reshape(NSA_HEADS, HEAD_DIM, D),
                     ((0, 0), (0, LANE - HEAD_DIM), (0, 0))).reshape(NSA_HEADS * LANE, D)
    wr_t = w_router[0].T
    wr_hi = wr_t.astype(BF16)
    wr_lo = (wr_t - wr_hi.astype(F32)).astype(BF16)
    h1, xn2, logits_t, xr = _merge(o_list, l_list, yb, gt, x2, w_branch_a[0].astype(BF16), wb_pad.astype(BF16),
                               w_out[0].astype(BF16), g_ffn, wr_hi, wr_lo)

    idx_t, wsel_t, rank_t, counts = _route(logits_t, router_bias[0][:, None])
    counts = counts[:, 0]
    padded = (counts + MOE_ROWS - 1) // MOE_ROWS * MOE_ROWS
    pend = jnp.cumsum(padded)
    pstart = pend - padded
    hot = idx_t[None] == jnp.arange(N_EXPERTS, dtype=jnp.int32)[:, None, None]
    pos_t = rank_t + jnp.sum(jnp.where(hot, pstart.astype(jnp.int32)[:, None, None], 0), axis=0)
    NK = N * TOP_K
    nblk = (NK + N_EXPERTS * (MOE_ROWS - 1) + MOE_ROWS - 1) // MOE_ROWS
    P = nblk * MOE_ROWS
    blk_e = jnp.minimum(jnp.sum(pend[None, :] <= (jnp.arange(nblk, dtype=jnp.int32) * MOE_ROWS)[:, None], axis=1),
                        N_EXPERTS - 1).astype(jnp.int32)
    nused = (pend[-1] // MOE_ROWS).astype(jnp.int32).reshape(1)
    xs = _dispatch((pstart + counts).astype(jnp.int32), pend.astype(jnp.int32), nused, pos_t, xr, P, D)
    ys = _moe_experts(blk_e, nused, xs, w_gate[0].astype(BF16), w_up[0].astype(BF16), w_down[0].astype(BF16))
    out = _final(h1, xn2, ys[pos_t], wsel_t.T, p[0].reshape(N, -1), ws_gate[0].astype(BF16), ws_up[0].astype(BF16),
                 ws_down[0].astype(BF16), g_ple, w_ple_gate[0].astype(BF16), w_ple[0].astype(BF16),
                 g_final[None, :])
    return out.reshape(B, S, D)
```

```python
import functools

import numpy as np
import jax
import jax.numpy as jnp
from jax import lax
from jax.experimental import pallas as pl
from jax.experimental.pallas import tpu as pltpu

F32 = jnp.float32
BF16 = jnp.bfloat16

HEAD_DIM = 64
LANE = 128
RMS_EPS = 1e-6
DIL_HEADS = 8
DIL_PATTERNS = ((128, 1), (512, 4), (2048, 16))
DIL_BLOCK = 128
NSA_HEADS = 8
NSA_KV_HEADS = 2
NSA_REP = NSA_HEADS // NSA_KV_HEADS
CMP_BLOCK = 32
CMP_STRIDE = 16
CMP_HIDDEN = 256
SLC_BLOCK = 64
SLC_TOP_N = 16
SLC_FORCE = 1e9
WIN_SIZE = 512
N_EXPERTS = 64
TOP_K = 8
N_GROUPS = 8
TOP_GROUPS = 4
GROUP_SIZE = N_EXPERTS // N_GROUPS
ROUTED_SCALE = 2.5
IN_SIZES = (512, 512, 512, 512, 128, 128, 128, 128, 128, 128, 24, 1024, 1024)

NEG = -1e30
NT_DIMS = (((1,), (1,)), ((), ()))

ROW_TILE = 512
INPROJ_ROWS = 1024
NSA_Q = 128
NSA_TK = 512
MOE_ROWS = 512
VMEM_LIMIT = 48 * 1024 * 1024


def _dot(a, b):
    return jnp.dot(a, b, preferred_element_type=F32)


def _dot_nt(a, b):
    return lax.dot_general(a, b, NT_DIMS, preferred_element_type=F32)


def _split3(x):
    hi = x.astype(BF16)
    r1 = x - hi.astype(F32)
    mid = r1.astype(BF16)
    lo = (r1 - mid.astype(F32)).astype(BF16)
    return hi, mid, lo


def _rms(x, g):
    r = lax.rsqrt(jnp.mean(x * x, axis=-1, keepdims=True) + RMS_EPS)
    return (x * r) * g


def _params(sem):
    return pltpu.CompilerParams(dimension_semantics=sem, vmem_limit_bytes=VMEM_LIMIT)


def _inproj_kernel(x_ref, g_ref, w_ref, o_ref, *rest, dils, n_dilated):
    dil_refs, (xn_ref, res_ref) = rest[:len(dils)], rest[len(dils):]
    j = pl.program_id(1)

    @pl.when(j == 0)
    def _():
        xn_ref[...] = _rms(x_ref[...], g_ref[...]).astype(BF16)

    res = _dot(xn_ref[...], w_ref[...])
    o_ref[...] = res.astype(o_ref.dtype)

    if dils:
        @pl.when(j < n_dilated)
        def _():
            tn = res.shape[1]
            for c in range(tn // LANE):
                res_ref[c] = res[:, c * LANE:(c + 1) * LANE]
            for d_ref, dil in zip(dil_refs, dils):
                for r in range(dil):
                    for c in range(tn // LANE):
                        piece = res_ref[c, pl.ds(r, INPROJ_ROWS // dil, stride=dil), :]
                        d_ref[:, r * tn + c * LANE:r * tn + (c + 1) * LANE] = piece.astype(d_ref.dtype)


def _inproj(x2, g, w, out_dtype, tn, dils=(), n_dilated=0):
    n, d = x2.shape
    cols = w.shape[1]
    last = max(n_dilated - 1, 0)
    out_specs = [pl.BlockSpec((INPROJ_ROWS, tn), lambda i, j: (i, j))]
    out_shape = [jax.ShapeDtypeStruct((n, cols), out_dtype)]
    for dil in dils:
        out_specs.append(pl.BlockSpec((INPROJ_ROWS // dil, dil * tn), lambda i, j: (i, jnp.minimum(j, last))))
        out_shape.append(jax.ShapeDtypeStruct((n // dil, n_dilated * dil * tn), out_dtype))
    return pl.pallas_call(
        functools.partial(_inproj_kernel, dils=dils, n_dilated=n_dilated),
        grid=(n // INPROJ_ROWS, cols // tn),
        in_specs=[
            pl.BlockSpec((INPROJ_ROWS, d), lambda i, j: (i, 0)),
            pl.BlockSpec((1, d), lambda i, j: (0, 0)),
            pl.BlockSpec((d, tn), lambda i, j: (0, j)),
        ],
        out_specs=out_specs,
        out_shape=out_shape,
        scratch_shapes=[pltpu.VMEM((INPROJ_ROWS, d), BF16), pltpu.VMEM((tn // LANE, INPROJ_ROWS, LANE), F32)],
        compiler_params=_params(("parallel", "arbitrary")),
    )(x2, g, w)


def _dil_kernel(q_ref, kp_ref, kc_ref, vp_ref, vc_ref, bias_ref, o_ref, l_ref, *, dil):
    i = pl.program_id(1)
    res = pl.program_id(2)
    blk = DIL_BLOCK
    out_rows = pl.ds(res, blk, stride=dil) if dil > 1 else slice(None)
    variant = jnp.minimum(i, 1) * DIL_HEADS
    lane = lax.broadcasted_iota(jnp.int32, (blk, LANE), 1)
    lo = lane < HEAD_DIM
    for hp in range(DIL_HEADS // 2):
        sl = slice(hp * LANE, (hp + 1) * LANE)
        q2 = q_ref[0, :, sl]
        k2 = jnp.concatenate([kp_ref[0, :, sl], kc_ref[0, :, sl]], axis=0)
        v2 = jnp.concatenate([vp_ref[0, :, sl], vc_ref[0, :, sl]], axis=0)
        outs, lses = [], []
        for hh in range(2):
            keep = lo if hh == 0 else jnp.logical_not(lo)
            qm = jnp.where(keep, q2, jnp.zeros_like(q2))
            s = _dot_nt(qm, k2) + bias_ref[variant + 2 * hp + hh]
            m = jnp.max(s, axis=-1, keepdims=True)
            e = jnp.exp(s - m)
            den = jnp.sum(e, axis=-1, keepdims=True)
            outs.append(_dot(e.astype(BF16), v2) / den)
            lses.append(jnp.broadcast_to(m + jnp.log(den), (blk, LANE)))
        o_ref[hp, 0, out_rows, :] = jnp.where(lo, outs[0], outs[1])
        l_ref[hp, 0, out_rows, :] = jnp.where(lo, lses[0], lses[1])


def _dil_bias(slopes_a, dil):
    blk = DIL_BLOCK
    qi = jnp.arange(blk)[:, None]
    kj = jnp.arange(2 * blk)[None, :]
    dist = (qi + blk - kj).astype(F32)
    band = (dist >= 0) & (dist <= blk)
    slope = (slopes_a * dil)[:, None, None]
    bias = -slope * dist
    first = jnp.where(band & (kj >= blk), bias, NEG)
    rest = jnp.where(band, bias, NEG)
    return jnp.concatenate([first, rest], axis=0)


def _dilated(src3, bias, dil, b, s):
    nblk = s // dil // DIL_BLOCK
    aw = DIL_HEADS * HEAD_DIM
    npair = DIL_HEADS // 2
    blk = (1, DIL_BLOCK, aw)

    def spec(which, prev):
        if prev:
            return pl.BlockSpec(blk, lambda bb, i, r: (bb, jnp.maximum(i - 1, 0), which * dil + r))
        return pl.BlockSpec(blk, lambda bb, i, r: (bb, i, which * dil + r))

    out_spec = pl.BlockSpec((npair, 1, DIL_BLOCK * dil, LANE), lambda bb, i, r: (0, bb, i, 0))
    return pl.pallas_call(
        functools.partial(_dil_kernel, dil=dil),
        grid=(b, nblk, dil),
        in_specs=[
            spec(0, False), spec(1, True), spec(1, False), spec(2, True), spec(2, False),
            pl.BlockSpec(bias.shape, lambda bb, i, r: (0, 0, 0)),
        ],
        out_specs=[out_spec, out_spec],
        out_shape=[jax.ShapeDtypeStruct((npair, b, s, LANE), F32)] * 2,
        compiler_params=_params(("parallel", "arbitrary", "arbitrary")),
    )(src3, src3, src3, src3, src3, bias)


def _cmp_kernel(x_ref, pos_ref, w1_ref, b1_ref, w2_ref, o_ref):
    half = x_ref.shape[-1]
    x = x_ref[0, 0]
    w1 = w1_ref[0]
    first = _dot(x, w1[:half])
    second = _dot(x, w1[half:])
    nrow = x.shape[0]
    pre = first + pltpu.roll(second, nrow - 1, 0)
    posb = _dot(jnp.broadcast_to(pos_ref[0], (8, 2 * half)).astype(BF16), w1)[0:1]
    pre = pre + posb + b1_ref[0]
    o_ref[0, 0] = _dot(jax.nn.gelu(pre).astype(BF16), w2_ref[0]).astype(o_ref.dtype)


def _compress(xc, pos, w1, b1, w2):
    b, four, nchunk, width = xc.shape
    return pl.pallas_call(
        _cmp_kernel,
        grid=(b, four),
        in_specs=[
            pl.BlockSpec((1, 1, nchunk, width), lambda bb, j: (bb, j, 0, 0)),
            pl.BlockSpec((1, 1, 2 * width), lambda bb, j: (j // 2, 0, 0)),
            pl.BlockSpec((1, 2 * width, CMP_HIDDEN), lambda bb, j: (j // 2, 0, 0)),
            pl.BlockSpec((1, 1, CMP_HIDDEN), lambda bb, j: (j // 2, 0, 0)),
            pl.BlockSpec((1, CMP_HIDDEN, LANE), lambda bb, j: (j // 2, 0, 0)),
        ],
        out_specs=pl.BlockSpec((1, 1, nchunk, LANE), lambda bb, j: (bb, j, 0, 0)),
        out_shape=jax.ShapeDtypeStruct((b, four, nchunk, LANE), BF16),
        compiler_params=_params(("parallel", "parallel")),
    )(xc, pos, w1, b1, w2)


POS_LANE = HEAD_DIM
POS_SPLIT = 64
SUM_LANE = HEAD_DIM
MASKED = -1e30
ROW_FLOOR = -1e29


def _pos_columns(pos):
    out = np.zeros((pos.shape[0], LANE), np.float32)
    out[:, POS_LANE] = pos // POS_SPLIT
    out[:, POS_LANE + 1] = pos % POS_SPLIT
    out[:, POS_LANE + 2] = 1.0
    out[:, POS_LANE + 3] = 1.0
    return jnp.asarray(out, BF16)


def _nsa_kernel(slope_ref, q_ref, kc_ref, vct_ref, ks_ref, kw_ref, vt_ref, gt_ref, ov_ref, kpa_ref, cpa_ref,
                oh_ref, o_ref, kaug_ref, flag_ref, m_ref, acc_ref, *, seq):
    g = pl.program_id(1)
    c = pl.program_id(2)
    C = NSA_Q
    R = NSA_REP
    TK = NSA_TK
    rows = R * C
    t0 = c * C
    scale = HEAD_DIM ** -0.5
    nb = seq // SLC_BLOCK
    ncmp = seq // CMP_STRIDE
    n_sel = min(SLC_TOP_N, nb)
    vblocks = TK // LANE

    @pl.when(c == 0)
    def _():
        kaug_ref[:, 0:LANE] = ks_ref[0] + kpa_ref[...]
        kaug_ref[:, LANE:2 * LANE] = oh_ref[...]

    row = lax.broadcasted_iota(jnp.int32, (rows, 1), 0)
    ti = t0 + row % C
    slope = jnp.zeros((rows, 1), F32)
    for r in range(R):
        slope = jnp.where(row // C == r, slope_ref[g * R + r], slope)
    lane = lax.broadcasted_iota(jnp.int32, (rows, LANE), 1)
    thi = (ti // POS_SPLIT).astype(F32)
    tlo = (ti % POS_SPLIT).astype(F32)
    qcols = jnp.where(lane == POS_LANE, POS_SPLIT * slope,
                      jnp.where(lane == POS_LANE + 1, slope,
                                jnp.where(lane == POS_LANE + 2, -POS_SPLIT * slope * thi,
                                          jnp.where(lane == POS_LANE + 3, -slope * tlo, 0.0))))
    q4 = jnp.concatenate([q_ref[0, :, r * LANE:(r + 1) * LANE] for r in range(R)], axis=0)
    q4 = (q4.astype(F32) * scale + qcols).astype(BF16)
    tq = t0 + lax.broadcasted_iota(jnp.int32, (1, C), 1)

    def heads(x):
        return jnp.concatenate([x] * R, axis=1)

    kc = kc_ref[0] + cpa_ref[...]
    ci = lax.broadcasted_iota(jnp.int32, (ncmp, 1), 0)
    vis = (ci * CMP_STRIDE + (CMP_BLOCK - 1) <= tq) & (ci < ncmp - 1)
    s = _dot_nt(kc, q4) + heads(jnp.where(vis, 0.0, MASKED))
    m = jnp.maximum(jnp.max(s, axis=0, keepdims=True), ROW_FLOOR)
    e = jnp.exp(s - m)
    den = jnp.sum(e, axis=0, keepdims=True)
    pc = e * (1.0 / jnp.where(den > 0, den, 1.0))
    o_cmp = _dot(vct_ref[0], pc.astype(BF16))
    pcsum = pc[:, 0:C]
    for r in range(1, R):
        pcsum = pcsum + pc[:, r * C:(r + 1) * C]
    ov = ov_ref[...]
    p_slc = sum(_dot(ov, part) for part in _split3(pcsum))

    def values_t(which, first_block, n):
        return jnp.concatenate([vt_ref[0, which, first_block + i] for i in range(n)], axis=1)

    span = WIN_SIZE + C
    ws = pl.multiple_of(jnp.maximum(t0 - WIN_SIZE, 0), C)
    k_w = kw_ref[0, pl.ds(ws, span), :] + kpa_ref[pl.ds(ws, span), :]
    wpos = ws + lax.broadcasted_iota(jnp.int32, (span, 1), 0)
    wbias = jnp.where((wpos <= tq) & (wpos >= tq - (WIN_SIZE - 1)), 0.0, MASKED)
    s3 = _dot_nt(k_w, q4) + heads(wbias)
    e3 = jnp.exp(s3 - jnp.max(s3, axis=0, keepdims=True))
    ow = _dot(values_t(1, ws // LANE, span // LANE), e3.astype(BF16))
    o_win = ow / ow[SUM_LANE:SUM_LANE + 1, :]

    j = lax.broadcasted_iota(jnp.int32, (LANE, C), 0)
    cur = tq // SLC_BLOCK
    forced = (j == 0) | (j == cur) | (j == cur - 1)
    future = j * SLC_BLOCK > tq
    work = jnp.where(forced, -jnp.inf, jnp.where(future, -SLC_FORCE, p_slc))
    work = jnp.where(j < nb, work, -jnp.inf)
    sel = jnp.where(forced, 1.0, 0.0)
    jf = j.astype(F32)
    for _ in range(n_sel - 3):
        mx = jnp.max(work, axis=0, keepdims=True)
        first = jnp.min(jnp.where(work == mx, jf, float(LANE)), axis=0, keepdims=True)
        pick = jf == first
        sel = jnp.where(pick, 1.0, sel)
        work = jnp.where(pick, -jnp.inf, work)
    sel = jnp.where(future, 0.0, sel)
    blocks_per_tile = TK // SLC_BLOCK
    for kt in range(seq // TK):
        flag_ref[kt] = jnp.max(sel[kt * blocks_per_tile:(kt + 1) * blocks_per_tile, :])
    sel_bias = jnp.where(sel.T > 0.5, 0.0, MASKED).astype(BF16)
    q_aug = jnp.concatenate([q4, jnp.concatenate([sel_bias] * R, axis=0)], axis=1)

    m_ref[...] = jnp.full((1, rows), ROW_FLOOR, F32)
    acc_ref[...] = jnp.zeros((LANE, rows), F32)
    kio = lax.broadcasted_iota(jnp.int32, (TK, 1), 0)

    def tile_update(kt, causal):
        start = pl.multiple_of(kt * TK, TK)
        s2 = _dot_nt(kaug_ref[pl.ds(start, TK), :], q_aug)
        if causal:
            s2 = s2 + heads(jnp.where(start + kio <= tq, 0.0, MASKED))
        m_i = m_ref[...]
        m_n = jnp.maximum(m_i, jnp.max(s2, axis=0, keepdims=True))
        alpha = jnp.exp(m_i - m_n)
        p2 = jnp.exp(s2 - m_n)
        acc_ref[...] = alpha * acc_ref[...] + _dot(values_t(0, kt * vblocks, vblocks), p2.astype(BF16))
        m_ref[...] = m_n

    def slc_body(kt, carry):
        @pl.when(flag_ref[kt] > 0.0)
        def _():
            tile_update(kt, False)

        return carry

    last = (t0 + C - 1) // TK
    lax.fori_loop(0, last, slc_body, 0)
    tile_update(last, True)
    acc = acc_ref[...]
    o_slc = acc / acc[SUM_LANE:SUM_LANE + 1, :]

    gates = jax.nn.sigmoid(gt_ref[0]).T
    for r in range(R):
        rs = slice(r * C, (r + 1) * C)
        out = (gates[r:r + 1] * o_cmp[:, rs] + gates[R + r:R + r + 1] * o_slc[:, rs]
               + gates[2 * R + r:2 * R + r + 1] * o_win[:, rs])
        o_ref[0, :, r * LANE:(r + 1) * LANE] = out.T.astype(o_ref.dtype)


def _nsa(slopes_b, pa3, gt3, kc, vct, vt, b, s, col0, gcol0):
    C = NSA_Q
    R = NSA_REP
    G = NSA_KV_HEADS
    nb = s // SLC_BLOCK
    ncmp = s // CMP_STRIDE
    assert nb <= LANE
    qc0 = col0 // R
    kv0 = col0 + NSA_HEADS
    rows = R * C

    ci = np.arange(ncmp)[None, :]
    sj = np.arange(LANE)[:, None]
    ov = ((ci * CMP_STRIDE < (sj + 1) * SLC_BLOCK) & (ci * CMP_STRIDE + CMP_BLOCK > sj * SLC_BLOCK)
          & (ci < ncmp - 1) & (sj < nb))
    ov = jnp.asarray(ov, BF16)
    kpa = _pos_columns(np.arange(s))
    cpa = _pos_columns(np.arange(ncmp) * CMP_STRIDE + (CMP_BLOCK - 1))
    oh = jnp.asarray(np.arange(s)[:, None] // SLC_BLOCK == np.arange(LANE)[None, :], BF16)

    def k_spec(which):
        return pl.BlockSpec((1, s, LANE), lambda bb, g, c: (bb, 0, kv0 + which * G + g))

    const = lambda a: pl.BlockSpec(a.shape, lambda bb, g, c: (0,) * a.ndim)
    return pl.pallas_call(
        functools.partial(_nsa_kernel, seq=s),
        grid=(b, G, s // C),
        in_specs=[
            pl.BlockSpec(memory_space=pltpu.SMEM),
            pl.BlockSpec((1, C, R * LANE), lambda bb, g, c: (bb, c, qc0 + g)),
            pl.BlockSpec((1, ncmp, LANE), lambda bb, g, c: (bb * G + g, 0, 0)),
            pl.BlockSpec((1, LANE, ncmp), lambda bb, g, c: (bb * G + g, 0, 0)),
            k_spec(2), k_spec(4),
            pl.BlockSpec((1, 2, s // LANE, LANE, LANE), lambda bb, g, c: (bb * G + g, 0, 0, 0, 0)),
            pl.BlockSpec((1, C, LANE), lambda bb, g, c: (bb, c, gcol0 + g)),
            const(ov), const(kpa), const(cpa), const(oh),
        ],
        out_specs=pl.BlockSpec((1, C, R * LANE), lambda bb, g, c: (bb, c, g)),
        out_shape=jax.ShapeDtypeStruct((b, s, NSA_HEADS * LANE), BF16),
        scratch_shapes=[
            pltpu.VMEM((s, 2 * LANE), BF16),
            pltpu.SMEM((s // NSA_TK,), F32),
            pltpu.VMEM((1, rows), F32),
            pltpu.VMEM((LANE, rows), F32),
        ],
        compiler_params=_params(("parallel", "parallel", "arbitrary")),
    )(slopes_b, pa3, kc, vct, pa3, pa3, vt, gt3, ov, kpa, cpa, oh)


def _merge_kernel(o1, o2, o3, l1, l2, l3, yb, ga, gb, x_ref, wa, wb, wo, gf, wrh, wrl,
                  h_ref, xn_ref, lg_ref, xr_ref):
    def pairs(ref):
        return jnp.concatenate([ref[h] for h in range(ref.shape[0])], axis=1)

    a1, a2, a3 = pairs(l1), pairs(l2), pairs(l3)
    mx = jnp.maximum(jnp.maximum(a1, a2), a3)
    e1, e2, e3 = jnp.exp(a1 - mx), jnp.exp(a2 - mx), jnp.exp(a3 - mx)
    tot = e1 + e2 + e3
    ya = (e1 / tot) * pairs(o1) + (e2 / tot) * pairs(o2) + (e3 / tot) * pairs(o3)
    merged = (jax.nn.sigmoid(ga[...]) * _dot(ya.astype(BF16), wa[...])
              + jax.nn.sigmoid(gb[...]) * _dot(yb[...], wb[...]))
    h = x_ref[...] + _dot(merged.astype(BF16), wo[...])
    h_ref[...] = h
    xn = _rms(h, gf[...])
    xn_ref[...] = xn.astype(BF16)
    tm, d = xn.shape
    for c in range(d // LANE):
        xr_ref[pl.ds(c, tm, stride=d // LANE), :] = xn[:, c * LANE:(c + 1) * LANE]
    xh = xn.astype(BF16)
    xl = (xn - xh.astype(F32)).astype(BF16)
    lg_ref[...] = _dot_nt(wrh[...], xh) + _dot_nt(wrh[...], xl) + _dot_nt(wrl[...], xh)


def _merge(o_list, l_list, yb, gt, x2, wa, wb, wo, gf, wrh, wrl):
    n, d = x2.shape
    aw = wa.shape[0]
    bw = wb.shape[0]
    tm = ROW_TILE
    row = lambda w: pl.BlockSpec((tm, w), lambda i: (i, 0))
    full = lambda a: pl.BlockSpec(a.shape, lambda i: (0,) * a.ndim)
    return pl.pallas_call(
        _merge_kernel,
        grid=(n // tm,),
        in_specs=[pl.BlockSpec((aw // LANE, tm, LANE), lambda i: (0, i, 0))] * 6 + [
            row(bw),
            pl.BlockSpec((tm, d), lambda i: (i, 0)),
            pl.BlockSpec((tm, d), lambda i: (i, 1)),
            row(d), full(wa), full(wb), full(wo), full(gf), full(wrh), full(wrl),
        ],
        out_specs=[row(d), row(d), pl.BlockSpec((N_EXPERTS, tm), lambda i: (0, i)),
                   pl.BlockSpec((tm * d // LANE, LANE), lambda i: (i, 0))],
        out_shape=[
            jax.ShapeDtypeStruct((n, d), F32),
            jax.ShapeDtypeStruct((n, d), BF16),
            jax.ShapeDtypeStruct((N_EXPERTS, n), F32),
            jax.ShapeDtypeStruct((n * d // LANE, LANE), F32),
        ],
        compiler_params=_params(("parallel",)),
    )(*o_list, *l_list, yb, gt, gt, x2, wa, wb, wo, gf, wrh, wrl)


def _route_kernel(lg_ref, bias_ref, tri_ref, idx_ref, w_ref, rank_ref, cnt_ref, carry_ref):
    E = N_EXPERTS
    tn = lg_ref.shape[1]
    scores = jax.nn.sigmoid(lg_ref[...])
    sel = scores + bias_ref[...]
    sub = lax.broadcasted_iota(jnp.int32, (GROUP_SIZE, tn), 0).astype(F32)
    gs = []
    for grp in range(N_GROUPS):
        blk = sel[grp * GROUP_SIZE:(grp + 1) * GROUP_SIZE]
        m1 = jnp.max(blk, axis=0, keepdims=True)
        first = jnp.min(jnp.where(blk == m1, sub, float(GROUP_SIZE)), axis=0, keepdims=True)
        m2 = jnp.max(jnp.where(sub == first, -jnp.inf, blk), axis=0, keepdims=True)
        gs.append(m1 + m2)
    keep = []
    for a in range(N_GROUPS):
        rank = jnp.zeros((1, tn), jnp.int32)
        for o in range(N_GROUPS):
            if o == a:
                continue
            beats = (gs[o] >= gs[a]) if o < a else (gs[o] > gs[a])
            rank = rank + beats.astype(jnp.int32)
        keep.append(jnp.broadcast_to(rank < TOP_GROUPS, (GROUP_SIZE, tn)))
    emask = jnp.concatenate(keep, axis=0)
    cand = jnp.where(emask, sel, -jnp.inf)
    eid = lax.broadcasted_iota(jnp.int32, (E, tn), 0)
    eid_f = eid.astype(F32)
    rank = jnp.zeros((E, tn), jnp.int32)
    for o in range(E):
        other = cand[o:o + 1]
        beats = (other > cand) | ((other == cand) & (eid > o))
        rank = rank + beats.astype(jnp.int32)
    chosen = emask & (rank < TOP_K)
    wsel = jnp.where(chosen, scores, 0.0)
    wnorm = wsel / jnp.sum(wsel, axis=0, keepdims=True) * ROUTED_SCALE

    @pl.when(pl.program_id(0) == 0)
    def _():
        carry_ref[...] = jnp.zeros_like(carry_ref)

    chosen_f = jnp.where(chosen, 1.0, 0.0)
    before = carry_ref[...] + _dot(chosen_f.astype(BF16), tri_ref[...]) - chosen_f
    carry_ref[...] = carry_ref[...] + jnp.sum(chosen_f, axis=1, keepdims=True)
    cnt_ref[...] = jnp.broadcast_to(carry_ref[...], cnt_ref.shape).astype(jnp.int32)
    for k in range(TOP_K):
        hit = chosen & (rank == k)
        idx_ref[k:k + 1, :] = jnp.sum(jnp.where(hit, eid_f, 0.0), axis=0, keepdims=True).astype(jnp.int32)
        w_ref[k:k + 1, :] = jnp.sum(jnp.where(hit, wnorm, 0.0), axis=0, keepdims=True)
        rank_ref[k:k + 1, :] = jnp.sum(jnp.where(hit, before, 0.0), axis=0, keepdims=True).astype(jnp.int32)


def _route(logits_t, bias_col):
    e, n = logits_t.shape
    tn = ROW_TILE
    tri = jnp.asarray(np.triu(np.ones((tn, tn), np.float32)), BF16)
    slot = pl.BlockSpec((TOP_K, tn), lambda i: (0, i))
    return pl.pallas_call(
        _route_kernel,
        grid=(n // tn,),
        in_specs=[pl.BlockSpec((e, tn), lambda i: (0, i)), pl.BlockSpec((e, 1), lambda i: (0, 0)),
                  pl.BlockSpec((tn, tn), lambda i: (0, 0))],
        out_specs=[slot, slot, slot, pl.BlockSpec((e, LANE), lambda i: (0, 0))],
        out_shape=[
            jax.ShapeDtypeStruct((TOP_K, n), jnp.int32),
            jax.ShapeDtypeStruct((TOP_K, n), F32),
            jax.ShapeDtypeStruct((TOP_K, n), jnp.int32),
            jax.ShapeDtypeStruct((e, LANE), jnp.int32),
        ],
        scratch_shapes=[pltpu.VMEM((e, 1), F32)],
        compiler_params=_params(("arbitrary",)),
    )(logits_t, bias_col, tri)


def _dispatch_kernel(fill_lo_ref, fill_hi_ref, nused_ref, pos_ref, x_ref, xs_ref, zero_ref, sem, zsem):
    i = pl.program_id(0)
    tm = pos_ref.shape[1]
    sub = x_ref.shape[0] // tm
    blk_rows = zero_ref.shape[0]
    nblk = xs_ref.shape[0] // blk_rows

    def slot(r):
        return xs_ref.at[pl.ds(pl.multiple_of(r * sub, sub), sub), :]

    @pl.when(i == 0)
    def _():
        zero_ref[...] = jnp.zeros_like(zero_ref)

        def per_expert(e, carry):
            def fill(r, c2):
                pltpu.make_async_copy(zero_ref.at[pl.ds(0, sub), :], slot(r), zsem).start()
                return c2

            def drain(r, c2):
                pltpu.make_async_copy(zero_ref.at[pl.ds(0, sub), :], slot(r), zsem).wait()
                return c2

            lax.fori_loop(fill_lo_ref[e], fill_hi_ref[e], fill, 0)
            lax.fori_loop(fill_lo_ref[e], fill_hi_ref[e], drain, 0)
            return carry

        lax.fori_loop(0, fill_lo_ref.shape[0], per_expert, 0)

        def tail(bk, carry):
            cp = pltpu.make_async_copy(
                zero_ref, xs_ref.at[pl.ds(pl.multiple_of(bk * blk_rows, blk_rows), blk_rows), :], zsem)
            cp.start()
            cp.wait()
            return carry

        lax.fori_loop(nused_ref[0], nblk, tail, 0)

    def scatter(t, carry):
        src = x_ref.at[pl.ds(pl.multiple_of(t * sub, sub), sub), :]
        for k in range(TOP_K):
            pltpu.make_async_copy(src, slot(pos_ref[k, t]), sem).start()
        return carry

    lax.fori_loop(0, tm, scatter, 0)
    for k in range(TOP_K):
        pltpu.make_async_copy(x_ref, xs_ref.at[pl.ds(0, tm * sub), :], sem).wait()


def _dispatch(fill_lo, fill_hi, nused, pos_t, xr, p_rows, d):
    k, n = pos_t.shape
    sub = d // LANE
    tm = ROW_TILE
    grid_spec = pltpu.PrefetchScalarGridSpec(
        num_scalar_prefetch=3,
        grid=(n // tm,),
        in_specs=[
            pl.BlockSpec((k, tm), lambda i, *_: (0, i), memory_space=pltpu.SMEM),
            pl.BlockSpec((tm * sub, LANE), lambda i, *_: (i, 0)),
        ],
        out_specs=pl.BlockSpec(memory_space=pl.ANY),
        scratch_shapes=[pltpu.VMEM((MOE_ROWS * sub, LANE), F32), pltpu.SemaphoreType.DMA(()),
                        pltpu.SemaphoreType.DMA(())],
    )
    return pl.pallas_call(
        _dispatch_kernel,
        grid_spec=grid_spec,
        out_shape=jax.ShapeDtypeStruct((p_rows * sub, LANE), F32),
        compiler_params=_params(("arbitrary",)),
    )(fill_lo, fill_hi, nused, pos_t, xr)


def _moe_kernel(blk_e_ref, nused_ref, xs_ref, wg_ref, wu_ref, wd_ref, o_ref):
    i = pl.program_id(0)

    @pl.when(i < nused_ref[0])
    def _():
        sub = xs_ref.shape[0] // MOE_ROWS
        x = jnp.concatenate([xs_ref[pl.ds(c, MOE_ROWS, stride=sub), :].astype(BF16) for c in range(sub)], axis=1)
        hg = _dot(x, wg_ref[0])
        hid = (hg * jax.nn.sigmoid(hg)) * _dot(x, wu_ref[0])
        o_ref[...] = _dot(hid.astype(BF16), wd_ref[0]).astype(o_ref.dtype)

    @pl.when(i >= nused_ref[0])
    def _():
        o_ref[...] = jnp.zeros_like(o_ref)


def _moe_experts(blk_e, nused, xs, wg, wu, wd):
    d = wg.shape[1]
    sub = d // LANE
    p = xs.shape[0] // sub
    hid = wg.shape[2]
    grid_spec = pltpu.PrefetchScalarGridSpec(
        num_scalar_prefetch=2,
        grid=(p // MOE_ROWS,),
        in_specs=[
            pl.BlockSpec((MOE_ROWS * sub, LANE), lambda i, be, nu: (jnp.minimum(i, nu[0] - 1), 0)),
            pl.BlockSpec((1, d, hid), lambda i, be, nu: (be[i], 0, 0)),
            pl.BlockSpec((1, d, hid), lambda i, be, nu: (be[i], 0, 0)),
            pl.BlockSpec((1, hid, d), lambda i, be, nu: (be[i], 0, 0)),
        ],
        out_specs=pl.BlockSpec((MOE_ROWS, d), lambda i, be, nu: (i, 0)),
    )
    return pl.pallas_call(
        _moe_kernel,
        grid_spec=grid_spec,
        out_shape=jax.ShapeDtypeStruct((p, d), BF16),
        compiler_params=_params(("arbitrary",)),
    )(blk_e, nused, xs, wg, wu, wd)


def _final_kernel(h_ref, xn_ref, ys_ref, w_ref, p_ref, wsg, wsu, wsd, gp, wpg, wp, gfin, o_ref):
    xn = xn_ref[...]
    hg = _dot(xn, wsg[...])
    shared = _dot(((hg * jax.nn.sigmoid(hg)) * _dot(xn, wsu[...])).astype(BF16), wsd[...])
    w = w_ref[...]
    y = w[:, 0:1] * ys_ref[0].astype(F32)
    for k in range(1, TOP_K):
        y = y + w[:, k:k + 1] * ys_ref[k].astype(F32)
    h = h_ref[...] + (y + shared)
    gate = jax.nn.sigmoid(_dot(_rms(h, gp[...]).astype(BF16), wpg[...]))
    h = h + _dot(p_ref[...].astype(BF16), wp[...]) * gate
    o_ref[...] = _rms(h, gfin[...])


def _final(h1, xn2, ysg, w_nk, p2, wsg, wsu, wsd, gp, wpg, wp, gfin):
    n, d = h1.shape
    tm = ROW_TILE
    row = lambda w: pl.BlockSpec((tm, w), lambda i: (i, 0))
    full = lambda a: pl.BlockSpec(a.shape, lambda i: (0,) * a.ndim)
    return pl.pallas_call(
        _final_kernel,
        grid=(n // tm,),
        in_specs=[row(d), row(d), pl.BlockSpec((TOP_K, tm, d), lambda i: (0, i, 0)), row(TOP_K), row(p2.shape[1]),
                  full(wsg), full(wsu), full(wsd), full(gp), full(wpg), full(wp), full(gfin)],
        out_specs=row(d),
        out_shape=jax.ShapeDtypeStruct((n, d), F32),
        compiler_params=_params(("parallel",)),
    )(h1, xn2, ysg, w_nk, p2, wsg, wsu, wsd, gp, wpg, wp, gfin)


def _pad_heads(w):
    d, c = w.shape
    nh = c // HEAD_DIM
    w = w.reshape(d, nh, HEAD_DIM)
    return jnp.pad(w, ((0, 0), (0, 0), (0, LANE - HEAD_DIM))).reshape(d, nh * LANE)


def _alibi_slopes(n):
    return 2.0 ** (-8.0 * jnp.arange(1, n + 1, dtype=jnp.float32) / n)


def kernel(x, p, g_mix, w_in, cmp_pos_k, cmp_w1_k, cmp_b1_k, cmp_w2_k, cmp_pos_v, cmp_w1_v, cmp_b1_v, cmp_w2_v, w_branch_a, w_branch_b, w_out, g_ffn, w_router, router_bias, w_gate, w_up, w_down, ws_gate, ws_up, ws_down, g_ple, w_ple_gate, w_ple, g_final):
    B, S, D = x.shape
    N = B * S
    G, R = NSA_KV_HEADS, NSA_REP
    assert w_in.shape[0] == 1 and p.shape[0] == 1
    assert S % (DIL_BLOCK * DIL_PATTERNS[-1][1]) == 0 and S % NSA_TK == 0 and S >= WIN_SIZE + NSA_Q
    assert N % ROW_TILE == 0 and N % INPROJ_ROWS == 0 and S % INPROJ_ROWS == 0
    slopes = _alibi_slopes(DIL_HEADS + NSA_HEADS)
    slopes_a, slopes_b = slopes[0::2], slopes[1::2]

    offs = np.concatenate([[0], np.cumsum(IN_SIZES)])
    parts = [w_in[0][:, offs[i]:offs[i + 1]] for i in range(len(IN_SIZES))]
    w_qa, w_ka, w_va, w_qb, w_kc, w_vc, w_ks, w_vs, w_kw, w_vw, w_gn, w_ga, w_gb = parts
    w_act = jnp.concatenate([w_qa * HEAD_DIM ** -0.5, w_ka, w_va]
                            + [_pad_heads(t) for t in (w_qb, w_kc, w_vc, w_ks, w_vs, w_kw, w_vw)],
                            axis=1).astype(BF16)
    act_w = w_act.shape[1]
    nsa_col0 = 3 * DIL_HEADS * HEAD_DIM // LANE
    w_gn = w_gn.reshape(D, 3, G, R).transpose(0, 2, 1, 3).reshape(D, G, 3 * R)
    w_gn = jnp.pad(w_gn, ((0, 0), (0, 0), (0, LANE - 3 * R))).reshape(D, G * LANE)
    w_gate_in = jnp.concatenate([w_ga, w_gb, w_gn], axis=1).astype(BF16)
    gate_w = w_gate_in.shape[1]

    x2 = x.reshape(N, D)
    dils = tuple(dil for _, dil in DIL_PATTERNS if dil > 1)
    act, *dilated_qkv = _inproj(x2, g_mix, w_act, BF16, 512, dils=dils, n_dilated=3)
    gt, = _inproj(x2, g_mix, w_gate_in, F32, 768)

    o_list, l_list = [], []
    for _, dil in DIL_PATTERNS:
        src = act if dil == 1 else dilated_qkv[dils.index(dil)]
        o, l = _dilated(src.reshape(B, S // dil, -1), _dil_bias(slopes_a, dil), dil, B, S)
        o_list.append(o.reshape(-1, N, LANE))
        l_list.append(l.reshape(-1, N, LANE))

    nchunk = S // CMP_STRIDE
    c0 = nsa_col0 * LANE + NSA_HEADS * LANE
    raw = act[:, c0:c0 + 2 * G * LANE].reshape(B, nchunk, CMP_STRIDE, 2 * G, LANE)[..., :HEAD_DIM]
    xc = raw.transpose(0, 3, 1, 2, 4).reshape(B, 2 * G, nchunk, CMP_STRIDE * HEAD_DIM)
    pos = jnp.stack([cmp_pos_k[0].reshape(1, -1), cmp_pos_v[0].reshape(1, -1)])
    w1 = jnp.stack([cmp_w1_k[0], cmp_w1_v[0]]).astype(BF16)
    b1 = jnp.stack([cmp_b1_k[0][None], cmp_b1_v[0][None]])
    w2 = jnp.pad(jnp.stack([cmp_w2_k[0], cmp_w2_v[0]]), ((0, 0), (0, 0), (0, LANE - HEAD_DIM))).astype(BF16)
    kcv = _compress(xc, pos, w1, b1, w2)
    kcv = kcv.reshape(B, 2, G, nchunk, LANE).transpose(1, 0, 2, 3, 4).reshape(2, B * G, nchunk, LANE)
    vct = kcv[1].transpose(0, 2, 1)
    v0 = (nsa_col0 + NSA_HEADS + 3 * G) * LANE
    vsw = jnp.stack([act[:, v0:v0 + G * LANE], act[:, v0 + 2 * G * LANE:v0 + 3 * G * LANE]])
    vsw = vsw.reshape(2, B, S // LANE, LANE, G, LANE).transpose(1, 4, 0, 2, 5, 3)
    ones_row = (jnp.arange(LANE) == SUM_LANE).astype(BF16)[:, None]
    vt = (vsw + ones_row).reshape(B * G, 2, S // LANE, LANE, LANE)
    yb = _nsa(slopes_b, act.reshape(B, S, act_w), gt.reshape(B, S, gate_w), kcv[0], vct, vt,
              B, S, nsa_col0, 2 * D // LANE)
    yb = yb.reshape(N, NSA_HEADS * LANE)

    wb_pad = jnp.pad(w_branch_b[0].reshape(NSA_HEADS, HEAD_DIM, D),
                     ((0, 0), (0, LANE - HEAD_DIM), (0, 0))).reshape(NSA_HEADS * LANE, D)
    wr_t = w_router[0].T
    wr_hi = wr_t.astype(BF16)
    wr_lo = (wr_t - wr_hi.astype(F32)).astype(BF16)
    h1, xn2, logits_t, xr = _merge(o_list, l_list, yb, gt, x2, w_branch_a[0].astype(BF16), wb_pad.astype(BF16),
                               w_out[0].astype(BF16), g_ffn, wr_hi, wr_lo)

    idx_t, wsel_t, rank_t, counts = _route(logits_t, router_bias[0][:, None])
    counts = counts[:, 0]
    padded = (counts + MOE_ROWS - 1) // MOE_ROWS * MOE_ROWS
    pend = jnp.cumsum(padded)
    pstart = pend - padded
    hot = idx_t[None] == jnp.arange(N_EXPERTS, dtype=jnp.int32)[:, None, None]
    pos_t = rank_t + jnp.sum(jnp.where(hot, pstart.astype(jnp.int32)[:, None, None], 0), axis=0)
    NK = N * TOP_K
    nblk = (NK + N_EXPERTS * (MOE_ROWS - 1) + MOE_ROWS - 1) // MOE_ROWS
    P = nblk * MOE_ROWS
    blk_e = jnp.minimum(jnp.sum(pend[None, :] <= (jnp.arange(nblk, dtype=jnp.int32) * MOE_ROWS)[:, None], axis=1),
                        N_EXPERTS - 1).astype(jnp.int32)
    nused = (pend[-1] // MOE_ROWS).astype(jnp.int32).reshape(1)
    xs = _dispatch((pstart + counts).astype(jnp.int32), pend.astype(jnp.int32), nused, pos_t, xr, P, D)
    ys = _moe_experts(blk_e, nused, xs, w_gate[0].astype(BF16), w_up[0].astype(BF16), w_down[0].astype(BF16))
    out = _final(h1, xn2, ys[pos_t], wsel_t.T, p[0].reshape(N, -1), ws_gate[0].astype(BF16), ws_up[0].astype(BF16),
                 ws_down[0].astype(BF16), g_ple, w_ple_gate[0].astype(BF16), w_ple[0].astype(BF16),
                 g_final[None, :])
    return out.reshape(B, S, D)
```

```python
import functools

import numpy as np
import jax
import jax.numpy as jnp
from jax import lax
from jax.experimental import pallas as pl
from jax.experimental.pallas import tpu as pltpu

F32 = jnp.float32
BF16 = jnp.bfloat16

HEAD_DIM = 64
LANE = 128
RMS_EPS = 1e-6
DIL_HEADS = 8
DIL_PATTERNS = ((128, 1), (512, 4), (2048, 16))
DIL_BLOCK = 128
DIL_STEP_BLOCKS = 2
NSA_HEADS = 8
NSA_KV_HEADS = 2
NSA_REP = NSA_HEADS // NSA_KV_HEADS
CMP_BLOCK = 32
CMP_STRIDE = 16
CMP_HIDDEN = 256
SLC_BLOCK = 64
SLC_TOP_N = 16
SLC_FORCE = 1e9
WIN_SIZE = 512
N_EXPERTS = 64
TOP_K = 8
N_GROUPS = 8
TOP_GROUPS = 4
GROUP_SIZE = N_EXPERTS // N_GROUPS
ROUTED_SCALE = 2.5
IN_SIZES = (512, 512, 512, 512, 128, 128, 128, 128, 128, 128, 24, 1024, 1024)

NEG = -1e30
NT_DIMS = (((1,), (1,)), ((), ()))

ROW_TILE = 512
INPROJ_ROWS = 1024
NSA_Q = 128
NSA_TK = 512
MOE_ROWS = 512
VMEM_LIMIT = 48 * 1024 * 1024


def _dot(a, b):
    return jnp.dot(a, b, preferred_element_type=F32)


def _dot_nt(a, b):
    return lax.dot_general(a, b, NT_DIMS, preferred_element_type=F32)


def _split3(x):
    hi = x.astype(BF16)
    r1 = x - hi.astype(F32)
    mid = r1.astype(BF16)
    lo = (r1 - mid.astype(F32)).astype(BF16)
    return hi, mid, lo


def _rms(x, g):
    r = lax.rsqrt(jnp.mean(x * x, axis=-1, keepdims=True) + RMS_EPS)
    return (x * r) * g


def _params(sem):
    return pltpu.CompilerParams(dimension_semantics=sem, vmem_limit_bytes=VMEM_LIMIT)


def _inproj_kernel(x_ref, g_ref, w_ref, o_ref, *rest, dils):
    dil_refs, xn_ref = rest[:len(dils)], rest[len(dils)]
    res_ref = rest[len(dils) + 1] if dils else None

    @pl.when(pl.program_id(1) == 0)
    def _():
        xn_ref[...] = _rms(x_ref[...], g_ref[...]).astype(BF16)

    res = _dot(xn_ref[...], w_ref[...])
    o_ref[...] = res.astype(o_ref.dtype)

    if dils:
        tn = res.shape[1]
        for c in range(tn // LANE):
            res_ref[c] = res[:, c * LANE:(c + 1) * LANE]
        for d_ref, dil in zip(dil_refs, dils):
            for r in range(dil):
                for c in range(tn // LANE):
                    piece = res_ref[c, pl.ds(r, INPROJ_ROWS // dil, stride=dil), :]
                    d_ref[:, r * tn + c * LANE:r * tn + (c + 1) * LANE] = piece.astype(d_ref.dtype)


def _inproj(x2, g, w, out_dtype, tn, dils=()):
    n, d = x2.shape
    cols = w.shape[1]
    out_specs = [pl.BlockSpec((INPROJ_ROWS, tn), lambda i, j: (i, j))]
    out_shape = [jax.ShapeDtypeStruct((n, cols), out_dtype)]
    for dil in dils:
        out_specs.append(pl.BlockSpec((INPROJ_ROWS // dil, dil * tn), lambda i, j: (i, j)))
        out_shape.append(jax.ShapeDtypeStruct((n // dil, cols * dil), out_dtype))
    return pl.pallas_call(
        functools.partial(_inproj_kernel, dils=dils),
        grid=(n // INPROJ_ROWS, cols // tn),
        in_specs=[
            pl.BlockSpec((INPROJ_ROWS, d), lambda i, j: (i, 0)),
            pl.BlockSpec((1, d), lambda i, j: (0, 0)),
            pl.BlockSpec((d, tn), lambda i, j: (0, j)),
        ],
        out_specs=out_specs,
        out_shape=out_shape,
        scratch_shapes=[pltpu.VMEM((INPROJ_ROWS, d), BF16)]
        + ([pltpu.VMEM((tn // LANE, INPROJ_ROWS, LANE), F32)] if dils else []),
        compiler_params=_params(("parallel", "arbitrary")),
    )(x2, g, w)


def _dil_kernel(q_ref, kp_ref, kc_ref, vp_ref, vc_ref, bias_ref, o_ref, l_ref, *, dil):
    i = pl.program_id(1)
    res = pl.program_id(2)
    blk = DIL_BLOCK
    lane = lax.broadcasted_iota(jnp.int32, (blk, LANE), 1)
    lo = lane < HEAD_DIM
    for sub in range(DIL_STEP_BLOCKS):
        rows = slice(sub * blk, (sub + 1) * blk)
        prev = slice((sub - 1) * blk, sub * blk)
        out_rows = pl.ds(sub * blk * dil + res, blk, stride=dil) if dil > 1 else rows
        variant = jnp.minimum(i, 1) * DIL_HEADS if sub == 0 else DIL_HEADS
        for hp in range(DIL_HEADS // 2):
            sl = slice(hp * LANE, (hp + 1) * LANE)
            q2 = q_ref[0, rows, sl]
            k_prev = kp_ref[0, :, sl] if sub == 0 else kc_ref[0, prev, sl]
            v_prev = vp_ref[0, :, sl] if sub == 0 else vc_ref[0, prev, sl]
            k2 = jnp.concatenate([k_prev, kc_ref[0, rows, sl]], axis=0)
            v2 = jnp.concatenate([v_prev, vc_ref[0, rows, sl]], axis=0)
            outs, lses = [], []
            for hh in range(2):
                keep = lo if hh == 0 else jnp.logical_not(lo)
                qm = jnp.where(keep, q2, jnp.zeros_like(q2))
                s = _dot_nt(qm, k2) + bias_ref[variant + 2 * hp + hh]
                m = jnp.max(s, axis=-1, keepdims=True)
                e = jnp.exp(s - m)
                den = jnp.sum(e, axis=-1, keepdims=True)
                outs.append(_dot(e.astype(BF16), v2) / den)
                lses.append(jnp.broadcast_to(m + jnp.log(den), (blk, LANE)))
            o_ref[hp, 0, out_rows, :] = jnp.where(lo, outs[0], outs[1])
            l_ref[hp, 0, out_rows, :] = jnp.where(lo, lses[0], lses[1])


def _dil_bias(slopes_a, dil):
    blk = DIL_BLOCK
    qi = jnp.arange(blk)[:, None]
    kj = jnp.arange(2 * blk)[None, :]
    dist = (qi + blk - kj).astype(F32)
    band = (dist >= 0) & (dist <= blk)
    slope = (slopes_a * dil)[:, None, None]
    bias = -slope * dist
    first = jnp.where(band & (kj >= blk), bias, NEG)
    rest = jnp.where(band, bias, NEG)
    return jnp.concatenate([first, rest], axis=0)


def _dilated(src3, bias, dil, b, s):
    nq = DIL_STEP_BLOCKS
    nstep = s // dil // (DIL_BLOCK * nq)
    aw = DIL_HEADS * HEAD_DIM
    npair = DIL_HEADS // 2

    def spec(which, prev):
        if prev:
            return pl.BlockSpec((1, DIL_BLOCK, aw), lambda bb, i, r: (bb, jnp.maximum(nq * i - 1, 0), which * dil + r))
        return pl.BlockSpec((1, DIL_BLOCK * nq, aw), lambda bb, i, r: (bb, i, which * dil + r))

    out_spec = pl.BlockSpec((npair, 1, DIL_BLOCK * nq * dil, LANE), lambda bb, i, r: (0, bb, i, 0))
    return pl.pallas_call(
        functools.partial(_dil_kernel, dil=dil),
        grid=(b, nstep, dil),
        in_specs=[
            spec(0, False), spec(1, True), spec(1, False), spec(2, True), spec(2, False),
            pl.BlockSpec(bias.shape, lambda bb, i, r: (0, 0, 0)),
        ],
        out_specs=[out_spec, out_spec],
        out_shape=[jax.ShapeDtypeStruct((npair, b, s, LANE), F32)] * 2,
        compiler_params=_params(("parallel", "arbitrary", "arbitrary")),
    )(src3, src3, src3, src3, src3, bias)


def _cmp_kernel(x_ref, pos_ref, w1_ref, b1_ref, w2_ref, o_ref):
    half = x_ref.shape[-1]
    x = x_ref[0, 0]
    w1 = w1_ref[0]
    first = _dot(x, w1[:half])
    second = _dot(x, w1[half:])
    nrow = x.shape[0]
    pre = first + pltpu.roll(second, nrow - 1, 0)
    posb = _dot(jnp.broadcast_to(pos_ref[0], (8, 2 * half)).astype(BF16), w1)[0:1]
    pre = pre + posb + b1_ref[0]
    o_ref[0, 0] = _dot(jax.nn.gelu(pre).astype(BF16), w2_ref[0]).astype(o_ref.dtype)


def _compress(xc, pos, w1, b1, w2):
    b, four, nchunk, width = xc.shape
    return pl.pallas_call(
        _cmp_kernel,
        grid=(b, four),
        in_specs=[
            pl.BlockSpec((1, 1, nchunk, width), lambda bb, j: (bb, j, 0, 0)),
            pl.BlockSpec((1, 1, 2 * width), lambda bb, j: (j // 2, 0, 0)),
            pl.BlockSpec((1, 2 * width, CMP_HIDDEN), lambda bb, j: (j // 2, 0, 0)),
            pl.BlockSpec((1, 1, CMP_HIDDEN), lambda bb, j: (j // 2, 0, 0)),
            pl.BlockSpec((1, CMP_HIDDEN, LANE), lambda bb, j: (j // 2, 0, 0)),
        ],
        out_specs=pl.BlockSpec((1, 1, nchunk, LANE), lambda bb, j: (bb, j, 0, 0)),
        out_shape=jax.ShapeDtypeStruct((b, four, nchunk, LANE), BF16),
        compiler_params=_params(("parallel", "parallel")),
    )(xc, pos, w1, b1, w2)


POS_LANE = HEAD_DIM
POS_SPLIT = 64
SUM_LANE = HEAD_DIM
MASKED = -1e30
ROW_FLOOR = -1e29


def _pos_columns(pos):
    out = np.zeros((pos.shape[0], LANE), np.float32)
    out[:, POS_LANE] = pos // POS_SPLIT
    out[:, POS_LANE + 1] = pos % POS_SPLIT
    out[:, POS_LANE + 2] = 1.0
    out[:, POS_LANE + 3] = 1.0
    return jnp.asarray(out, BF16)


def _nsa_kernel(slope_ref, q_ref, kc_ref, vct_ref, ks_ref, kw_ref, vt_ref, gt_ref, ov_ref, kpa_ref, cpa_ref,
                oh_ref, o_ref, kaug_ref, flag_ref, m_ref, acc_ref, *, seq):
    g = pl.program_id(1)
    c = pl.program_id(2)
    C = NSA_Q
    R = NSA_REP
    TK = NSA_TK
    rows = R * C
    t0 = c * C
    scale = HEAD_DIM ** -0.5
    nb = seq // SLC_BLOCK
    ncmp = seq // CMP_STRIDE
    n_sel = min(SLC_TOP_N, nb)
    vblocks = TK // LANE

    @pl.when(c == 0)
    def _():
        kaug_ref[:, 0:LANE] = ks_ref[0] + kpa_ref[...]
        kaug_ref[:, LANE:2 * LANE] = oh_ref[...]

    row = lax.broadcasted_iota(jnp.int32, (rows, 1), 0)
    ti = t0 + row % C
    slope = jnp.zeros((rows, 1), F32)
    for r in range(R):
        slope = jnp.where(row // C == r, slope_ref[g * R + r], slope)
    lane = lax.broadcasted_iota(jnp.int32, (rows, LANE), 1)
    thi = (ti // POS_SPLIT).astype(F32)
    tlo = (ti % POS_SPLIT).astype(F32)
    qcols = jnp.where(lane == POS_LANE, POS_SPLIT * slope,
                      jnp.where(lane == POS_LANE + 1, slope,
                                jnp.where(lane == POS_LANE + 2, -POS_SPLIT * slope * thi,
                                          jnp.where(lane == POS_LANE + 3, -slope * tlo, 0.0))))
    q4 = jnp.concatenate([q_ref[0, :, r * LANE:(r + 1) * LANE] for r in range(R)], axis=0)
    q4 = (q4.astype(F32) * scale + qcols).astype(BF16)
    tq = t0 + lax.broadcasted_iota(jnp.int32, (1, C), 1)

    def heads(x):
        return jnp.concatenate([x] * R, axis=1)

    kc = kc_ref[0] + cpa_ref[...]
    ci = lax.broadcasted_iota(jnp.int32, (ncmp, 1), 0)
    vis = (ci * CMP_STRIDE + (CMP_BLOCK - 1) <= tq) & (ci < ncmp - 1)
    s = _dot_nt(kc, q4) + heads(jnp.where(vis, 0.0, MASKED))
    m = jnp.maximum(jnp.max(s, axis=0, keepdims=True), ROW_FLOOR)
    e = jnp.exp(s - m)
    den = jnp.sum(e, axis=0, keepdims=True)
    pc = e * (1.0 / jnp.where(den > 0, den, 1.0))
    o_cmp = _dot(vct_ref[0], pc.astype(BF16))
    pcsum = pc[:, 0:C]
    for r in range(1, R):
        pcsum = pcsum + pc[:, r * C:(r + 1) * C]
    ov = ov_ref[...]
    p_slc = sum(_dot(ov, part) for part in _split3(pcsum))

    def values_t(which, first_block, n):
        return jnp.concatenate([vt_ref[0, which, first_block + i] for i in range(n)], axis=1)

    span = WIN_SIZE + C
    ws = pl.multiple_of(jnp.maximum(t0 - WIN_SIZE, 0), C)
    k_w = kw_ref[0, pl.ds(ws, span), :] + kpa_ref[pl.ds(ws, span), :]
    wpos = ws + lax.broadcasted_iota(jnp.int32, (span, 1), 0)
    wbias = jnp.where((wpos <= tq) & (wpos >= tq - (WIN_SIZE - 1)), 0.0, MASKED)
    s3 = _dot_nt(k_w, q4) + heads(wbias)
    e3 = jnp.exp(s3 - jnp.max(s3, axis=0, keepdims=True))
    ow = _dot(values_t(1, ws // LANE, span // LANE), e3.astype(BF16))
    o_win = ow / ow[SUM_LANE:SUM_LANE + 1, :]

    j = lax.broadcasted_iota(jnp.int32, (LANE, C), 0)
    cur = tq // SLC_BLOCK
    forced = (j == 0) | (j == cur) | (j == cur - 1)
    future = j * SLC_BLOCK > tq
    work = jnp.where(forced, -jnp.inf, jnp.where(future, -SLC_FORCE, p_slc))
    work = jnp.where(j < nb, work, -jnp.inf)
    sel = jnp.where(forced, 1.0, 0.0)
    jf = j.astype(F32)
    for _ in range(n_sel - 3):
        mx = jnp.max(work, axis=0, keepdims=True)
        first = jnp.min(jnp.where(work == mx, jf, float(LANE)), axis=0, keepdims=True)
        pick = jf == first
        sel = jnp.where(pick, 1.0, sel)
        work = jnp.where(pick, -jnp.inf, work)
    sel = jnp.where(future, 0.0, sel)
    blocks_per_tile = TK // SLC_BLOCK
    for kt in range(seq // TK):
        flag_ref[kt] = jnp.max(sel[kt * blocks_per_tile:(kt + 1) * blocks_per_tile, :])
    sel_bias = jnp.where(sel.T > 0.5, 0.0, MASKED).astype(BF16)
    q_aug = jnp.concatenate([q4, jnp.concatenate([sel_bias] * R, axis=0)], axis=1)

    m_ref[...] = jnp.full((1, rows), ROW_FLOOR, F32)
    acc_ref[...] = jnp.zeros((LANE, rows), F32)
    kio = lax.broadcasted_iota(jnp.int32, (TK, 1), 0)

    def tile_update(kt, causal):
        start = pl.multiple_of(kt * TK, TK)
        s2 = _dot_nt(kaug_ref[pl.ds(start, TK), :], q_aug)
        if causal:
            s2 = s2 + heads(jnp.where(start + kio <= tq, 0.0, MASKED))
        m_i = m_ref[...]
        m_n = jnp.maximum(m_i, jnp.max(s2, axis=0, keepdims=True))
        alpha = jnp.exp(m_i - m_n)
        p2 = jnp.exp(s2 - m_n)
        acc_ref[...] = alpha * acc_ref[...] + _dot(values_t(0, kt * vblocks, vblocks), p2.astype(BF16))
        m_ref[...] = m_n

    def slc_body(kt, carry):
        @pl.when(flag_ref[kt] > 0.0)
        def _():
            tile_update(kt, False)

        return carry

    last = (t0 + C - 1) // TK
    lax.fori_loop(0, last, slc_body, 0)
    tile_update(last, True)
    acc = acc_ref[...]
    o_slc = acc / acc[SUM_LANE:SUM_LANE + 1, :]

    gates = jax.nn.sigmoid(gt_ref[0]).T
    for r in range(R):
        rs = slice(r * C, (r + 1) * C)
        out = (gates[r:r + 1] * o_cmp[:, rs] + gates[R + r:R + r + 1] * o_slc[:, rs]
               + gates[2 * R + r:2 * R + r + 1] * o_win[:, rs])
        o_ref[0, :, r * LANE:(r + 1) * LANE] = out.T.astype(o_ref.dtype)


def _nsa(slopes_b, pa3, gt3, kc, vct, vt, b, s, col0, gcol0):
    C = NSA_Q
    R = NSA_REP
    G = NSA_KV_HEADS
    nb = s // SLC_BLOCK
    ncmp = s // CMP_STRIDE
    assert nb <= LANE
    qc0 = col0 // R
    kv0 = col0 + NSA_HEADS
    rows = R * C

    ci = np.arange(ncmp)[None, :]
    sj = np.arange(LANE)[:, None]
    ov = ((ci * CMP_STRIDE < (sj + 1) * SLC_BLOCK) & (ci * CMP_STRIDE + CMP_BLOCK > sj * SLC_BLOCK)
          & (ci < ncmp - 1) & (sj < nb))
    ov = jnp.asarray(ov, BF16)
    kpa = _pos_columns(np.arange(s))
    cpa = _pos_columns(np.arange(ncmp) * CMP_STRIDE + (CMP_BLOCK - 1))
    oh = jnp.asarray(np.arange(s)[:, None] // SLC_BLOCK == np.arange(LANE)[None, :], BF16)

    def k_spec(which):
        return pl.BlockSpec((1, s, LANE), lambda bb, g, c: (bb, 0, kv0 + which * G + g))

    const = lambda a: pl.BlockSpec(a.shape, lambda bb, g, c: (0,) * a.ndim)
    return pl.pallas_call(
        functools.partial(_nsa_kernel, seq=s),
        grid=(b, G, s // C),
        in_specs=[
            pl.BlockSpec(memory_space=pltpu.SMEM),
            pl.BlockSpec((1, C, R * LANE), lambda bb, g, c: (bb, c, qc0 + g)),
            pl.BlockSpec((1, ncmp, LANE), lambda bb, g, c: (bb * G + g, 0, 0)),
            pl.BlockSpec((1, LANE, ncmp), lambda bb, g, c: (bb * G + g, 0, 0)),
            k_spec(2), k_spec(4),
            pl.BlockSpec((1, 2, s // LANE, LANE, LANE), lambda bb, g, c: (bb * G + g, 0, 0, 0, 0)),
            pl.BlockSpec((1, C, LANE), lambda bb, g, c: (bb, c, gcol0 + g)),
            const(ov), const(kpa), const(cpa), const(oh),
        ],
        out_specs=pl.BlockSpec((1, C, R * LANE), lambda bb, g, c: (bb, c, g)),
        out_shape=jax.ShapeDtypeStruct((b, s, NSA_HEADS * LANE), BF16),
        scratch_shapes=[
            pltpu.VMEM((s, 2 * LANE), BF16),
            pltpu.SMEM((s // NSA_TK,), F32),
            pltpu.VMEM((1, rows), F32),
            pltpu.VMEM((LANE, rows), F32),
        ],
        compiler_params=_params(("parallel", "parallel", "arbitrary")),
    )(slopes_b, pa3, kc, vct, pa3, pa3, vt, gt3, ov, kpa, cpa, oh)


def _merge_kernel(o1, o2, o3, l1, l2, l3, yb, ga, gb, x_ref, wa, wb, wo, gf, wrh, wrl,
                  h_ref, xn_ref, lg_ref, xr_ref):
    def pairs(ref):
        return jnp.concatenate([ref[h] for h in range(ref.shape[0])], axis=1)

    a1, a2, a3 = pairs(l1), pairs(l2), pairs(l3)
    mx = jnp.maximum(jnp.maximum(a1, a2), a3)
    e1, e2, e3 = jnp.exp(a1 - mx), jnp.exp(a2 - mx), jnp.exp(a3 - mx)
    tot = e1 + e2 + e3
    ya = (e1 / tot) * pairs(o1) + (e2 / tot) * pairs(o2) + (e3 / tot) * pairs(o3)
    merged = (jax.nn.sigmoid(ga[...]) * _dot(ya.astype(BF16), wa[...])
              + jax.nn.sigmoid(gb[...]) * _dot(yb[...], wb[...]))
    h = x_ref[...] + _dot(merged.astype(BF16), wo[...])
    h_ref[...] = h
    xn = _rms(h, gf[...])
    xn_ref[...] = xn.astype(BF16)
    tm, d = xn.shape
    for c in range(d // LANE):
        xr_ref[pl.ds(c, tm, stride=d // LANE), :] = xn[:, c * LANE:(c + 1) * LANE]
    xh = xn.astype(BF16)
    xl = (xn - xh.astype(F32)).astype(BF16)
    lg_ref[...] = _dot_nt(wrh[...], xh) + _dot_nt(wrh[...], xl) + _dot_nt(wrl[...], xh)


def _merge(o_list, l_list, yb, gt, x2, wa, wb, wo, gf, wrh, wrl):
    n, d = x2.shape
    aw = wa.shape[0]
    bw = wb.shape[0]
    tm = ROW_TILE
    row = lambda w: pl.BlockSpec((tm, w), lambda i: (i, 0))
    full = lambda a: pl.BlockSpec(a.shape, lambda i: (0,) * a.ndim)
    return pl.pallas_call(
        _merge_kernel,
        grid=(n // tm,),
        in_specs=[pl.BlockSpec((aw // LANE, tm, LANE), lambda i: (0, i, 0))] * 6 + [
            row(bw),
            pl.BlockSpec((tm, d), lambda i: (i, 0)),
            pl.BlockSpec((tm, d), lambda i: (i, 1)),
            row(d), full(wa), full(wb), full(wo), full(gf), full(wrh), full(wrl),
        ],
        out_specs=[row(d), row(d), pl.BlockSpec((N_EXPERTS, tm), lambda i: (0, i)),
                   pl.BlockSpec((tm * d // LANE, LANE), lambda i: (i, 0))],
        out_shape=[
            jax.ShapeDtypeStruct((n, d), F32),
            jax.ShapeDtypeStruct((n, d), BF16),
            jax.ShapeDtypeStruct((N_EXPERTS, n), F32),
            jax.ShapeDtypeStruct((n * d // LANE, LANE), F32),
        ],
        compiler_params=_params(("parallel",)),
    )(*o_list, *l_list, yb, gt, gt, x2, wa, wb, wo, gf, wrh, wrl)


def _route_kernel(lg_ref, bias_ref, tri_ref, idx_ref, w_ref, rank_ref, cnt_ref, carry_ref):
    E = N_EXPERTS
    tn = lg_ref.shape[1]
    scores = jax.nn.sigmoid(lg_ref[...])
    sel = scores + bias_ref[...]
    sub = lax.broadcasted_iota(jnp.int32, (GROUP_SIZE, tn), 0).astype(F32)
    gs = []
    for grp in range(N_GROUPS):
        blk = sel[grp * GROUP_SIZE:(grp + 1) * GROUP_SIZE]
        m1 = jnp.max(blk, axis=0, keepdims=True)
        first = jnp.min(jnp.where(blk == m1, sub, float(GROUP_SIZE)), axis=0, keepdims=True)
        m2 = jnp.max(jnp.where(sub == first, -jnp.inf, blk), axis=0, keepdims=True)
        gs.append(m1 + m2)
    keep = []
    for a in range(N_GROUPS):
        rank = jnp.zeros((1, tn), jnp.int32)
        for o in range(N_GROUPS):
            if o == a:
                continue
            beats = (gs[o] >= gs[a]) if o < a else (gs[o] > gs[a])
            rank = rank + beats.astype(jnp.int32)
        keep.append(jnp.broadcast_to(rank < TOP_GROUPS, (GROUP_SIZE, tn)))
    emask = jnp.concatenate(keep, axis=0)
    cand = jnp.where(emask, sel, -jnp.inf)
    eid = lax.broadcasted_iota(jnp.int32, (E, tn), 0)
    eid_f = eid.astype(F32)
    rank = jnp.zeros((E, tn), jnp.int32)
    for o in range(E):
        other = cand[o:o + 1]
        beats = (other > cand) | ((other == cand) & (eid > o))
        rank = rank + beats.astype(jnp.int32)
    chosen = emask & (rank < TOP_K)
    wsel = jnp.where(chosen, scores, 0.0)
    wnorm = wsel / jnp.sum(wsel, axis=0, keepdims=True) * ROUTED_SCALE

    @pl.when(pl.program_id(0) == 0)
    def _():
        carry_ref[...] = jnp.zeros_like(carry_ref)

    chosen_f = jnp.where(chosen, 1.0, 0.0)
    before = carry_ref[...] + _dot(chosen_f.astype(BF16), tri_ref[...]) - chosen_f
    carry_ref[...] = carry_ref[...] + jnp.sum(chosen_f, axis=1, keepdims=True)
    cnt_ref[...] = jnp.broadcast_to(carry_ref[...], cnt_ref.shape).astype(jnp.int32)
    for k in range(TOP_K):
        hit = chosen & (rank == k)
        idx_ref[k:k + 1, :] = jnp.sum(jnp.where(hit, eid_f, 0.0), axis=0, keepdims=True).astype(jnp.int32)
        w_ref[k:k + 1, :] = jnp.sum(jnp.where(hit, wnorm, 0.0), axis=0, keepdims=True)
        rank_ref[k:k + 1, :] = jnp.sum(jnp.where(hit, before, 0.0), axis=0, keepdims=True).astype(jnp.int32)


def _route(logits_t, bias_col):
    e, n = logits_t.shape
    tn = ROW_TILE
    tri = jnp.asarray(np.triu(np.ones((tn, tn), np.float32)), BF16)
    slot = pl.BlockSpec((TOP_K, tn), lambda i: (0, i))
    return pl.pallas_call(
        _route_kernel,
        grid=(n // tn,),
        in_specs=[pl.BlockSpec((e, tn), lambda i: (0, i)), pl.BlockSpec((e, 1), lambda i: (0, 0)),
                  pl.BlockSpec((tn, tn), lambda i: (0, 0))],
        out_specs=[slot, slot, slot, pl.BlockSpec((e, LANE), lambda i: (0, 0))],
        out_shape=[
            jax.ShapeDtypeStruct((TOP_K, n), jnp.int32),
            jax.ShapeDtypeStruct((TOP_K, n), F32),
            jax.ShapeDtypeStruct((TOP_K, n), jnp.int32),
            jax.ShapeDtypeStruct((e, LANE), jnp.int32),
        ],
        scratch_shapes=[pltpu.VMEM((e, 1), F32)],
        compiler_params=_params(("arbitrary",)),
    )(logits_t, bias_col, tri)


def _dispatch_kernel(fill_lo_ref, fill_hi_ref, nused_ref, pos_ref, x_ref, xs_ref, zero_ref, sem, zsem):
    i = pl.program_id(0)
    tm = pos_ref.shape[1]
    sub = x_ref.shape[0] // tm
    blk_rows = zero_ref.shape[0]
    nblk = xs_ref.shape[0] // blk_rows

    def slot(r):
        return xs_ref.at[pl.ds(pl.multiple_of(r * sub, sub), sub), :]

    @pl.when(i == 0)
    def _():
        zero_ref[...] = jnp.zeros_like(zero_ref)

        def per_expert(e, carry):
            def fill(r, c2):
                pltpu.make_async_copy(zero_ref.at[pl.ds(0, sub), :], slot(r), zsem).start()
                return c2

            def drain(r, c2):
                pltpu.make_async_copy(zero_ref.at[pl.ds(0, sub), :], slot(r), zsem).wait()
                return c2

            lax.fori_loop(fill_lo_ref[e], fill_hi_ref[e], fill, 0)
            lax.fori_loop(fill_lo_ref[e], fill_hi_ref[e], drain, 0)
            return carry

        lax.fori_loop(0, fill_lo_ref.shape[0], per_expert, 0)

        def tail(bk, carry):
            cp = pltpu.make_async_copy(
                zero_ref, xs_ref.at[pl.ds(pl.multiple_of(bk * blk_rows, blk_rows), blk_rows), :], zsem)
            cp.start()
            cp.wait()
            return carry

        lax.fori_loop(nused_ref[0], nblk, tail, 0)

    def scatter(t, carry):
        src = x_ref.at[pl.ds(pl.multiple_of(t * sub, sub), sub), :]
        for k in range(TOP_K):
            pltpu.make_async_copy(src, slot(pos_ref[k, t]), sem).start()
        return carry

    lax.fori_loop(0, tm, scatter, 0)
    for k in range(TOP_K):
        pltpu.make_async_copy(x_ref, xs_ref.at[pl.ds(0, tm * sub), :], sem).wait()


def _dispatch(fill_lo, fill_hi, nused, pos_t, xr, p_rows, d):
    k, n = pos_t.shape
    sub = d // LANE
    tm = ROW_TILE
    grid_spec = pltpu.PrefetchScalarGridSpec(
        num_scalar_prefetch=3,
        grid=(n // tm,),
        in_specs=[
            pl.BlockSpec((k, tm), lambda i, *_: (0, i), memory_space=pltpu.SMEM),
            pl.BlockSpec((tm * sub, LANE), lambda i, *_: (i, 0)),
        ],
        out_specs=pl.BlockSpec(memory_space=pl.ANY),
        scratch_shapes=[pltpu.VMEM((MOE_ROWS * sub, LANE), F32), pltpu.SemaphoreType.DMA(()),
                        pltpu.SemaphoreType.DMA(())],
    )
    return pl.pallas_call(
        _dispatch_kernel,
        grid_spec=grid_spec,
        out_shape=jax.ShapeDtypeStruct((p_rows * sub, LANE), F32),
        compiler_params=_params(("arbitrary",)),
    )(fill_lo, fill_hi, nused, pos_t, xr)


def _moe_kernel(blk_e_ref, nused_ref, xs_ref, wg_ref, wu_ref, wd_ref, o_ref):
    i = pl.program_id(0)

    @pl.when(i < nused_ref[0])
    def _():
        sub = xs_ref.shape[0] // MOE_ROWS
        x = jnp.concatenate([xs_ref[pl.ds(c, MOE_ROWS, stride=sub), :].astype(BF16) for c in range(sub)], axis=1)
        hg = _dot(x, wg_ref[0])
        hid = (hg * jax.nn.sigmoid(hg)) * _dot(x, wu_ref[0])
        o_ref[...] = _dot(hid.astype(BF16), wd_ref[0]).astype(o_ref.dtype)

    @pl.when(i >= nused_ref[0])
    def _():
        o_ref[...] = jnp.zeros_like(o_ref)


def _moe_experts(blk_e, nused, xs, wg, wu, wd):
    d = wg.shape[1]
    sub = d // LANE
    p = xs.shape[0] // sub
    hid = wg.shape[2]
    grid_spec = pltpu.PrefetchScalarGridSpec(
        num_scalar_prefetch=2,
        grid=(p // MOE_ROWS,),
        in_specs=[
            pl.BlockSpec((MOE_ROWS * sub, LANE), lambda i, be, nu: (jnp.minimum(i, nu[0] - 1), 0)),
            pl.BlockSpec((1, d, hid), lambda i, be, nu: (be[i], 0, 0)),
            pl.BlockSpec((1, d, hid), lambda i, be, nu: (be[i], 0, 0)),
            pl.BlockSpec((1, hid, d), lambda i, be, nu: (be[i], 0, 0)),
        ],
        out_specs=pl.BlockSpec((MOE_ROWS, d), lambda i, be, nu: (i, 0)),
    )
    return pl.pallas_call(
        _moe_kernel,
        grid_spec=grid_spec,
        out_shape=jax.ShapeDtypeStruct((p, d), BF16),
        compiler_params=_params(("arbitrary",)),
    )(blk_e, nused, xs, wg, wu, wd)


def _final_kernel(h_ref, xn_ref, ys_ref, w_ref, p_ref, wsg, wsu, wsd, gp, wpg, wp, gfin, o_ref):
    xn = xn_ref[...]
    hg = _dot(xn, wsg[...])
    shared = _dot(((hg * jax.nn.sigmoid(hg)) * _dot(xn, wsu[...])).astype(BF16), wsd[...])
    w = w_ref[...]
    y = w[:, 0:1] * ys_ref[0].astype(F32)
    for k in range(1, TOP_K):
        y = y + w[:, k:k + 1] * ys_ref[k].astype(F32)
    h = h_ref[...] + (y + shared)
    gate = jax.nn.sigmoid(_dot(_rms(h, gp[...]).astype(BF16), wpg[...]))
    h = h + _dot(p_ref[...].astype(BF16), wp[...]) * gate
    o_ref[...] = _rms(h, gfin[...])


def _final(h1, xn2, ysg, w_nk, p2, wsg, wsu, wsd, gp, wpg, wp, gfin):
    n, d = h1.shape
    tm = ROW_TILE
    row = lambda w: pl.BlockSpec((tm, w), lambda i: (i, 0))
    full = lambda a: pl.BlockSpec(a.shape, lambda i: (0,) * a.ndim)
    return pl.pallas_call(
        _final_kernel,
        grid=(n // tm,),
        in_specs=[row(d), row(d), pl.BlockSpec((TOP_K, tm, d), lambda i: (0, i, 0)), row(TOP_K), row(p2.shape[1]),
                  full(wsg), full(wsu), full(wsd), full(gp), full(wpg), full(wp), full(gfin)],
        out_specs=row(d),
        out_shape=jax.ShapeDtypeStruct((n, d), F32),
        compiler_params=_params(("parallel",)),
    )(h1, xn2, ysg, w_nk, p2, wsg, wsu, wsd, gp, wpg, wp, gfin)


def _pad_heads(w):
    d, c = w.shape
    nh = c // HEAD_DIM
    w = w.reshape(d, nh, HEAD_DIM)
    return jnp.pad(w, ((0, 0), (0, 0), (0, LANE - HEAD_DIM))).reshape(d, nh * LANE)


def _alibi_slopes(n):
    return 2.0 ** (-8.0 * jnp.arange(1, n + 1, dtype=jnp.float32) / n)


def kernel(x, p, g_mix, w_in, cmp_pos_k, cmp_w1_k, cmp_b1_k, cmp_w2_k, cmp_pos_v, cmp_w1_v, cmp_b1_v, cmp_w2_v, w_branch_a, w_branch_b, w_out, g_ffn, w_router, router_bias, w_gate, w_up, w_down, ws_gate, ws_up, ws_down, g_ple, w_ple_gate, w_ple, g_final):
    B, S, D = x.shape
    N = B * S
    G, R = NSA_KV_HEADS, NSA_REP
    assert w_in.shape[0] == 1 and p.shape[0] == 1
    assert S % (DIL_BLOCK * DIL_STEP_BLOCKS * DIL_PATTERNS[-1][1]) == 0 and S % NSA_TK == 0 and S >= WIN_SIZE + NSA_Q
    assert N % ROW_TILE == 0 and N % INPROJ_ROWS == 0 and S % INPROJ_ROWS == 0
    slopes = _alibi_slopes(DIL_HEADS + NSA_HEADS)
    slopes_a, slopes_b = slopes[0::2], slopes[1::2]

    offs = np.concatenate([[0], np.cumsum(IN_SIZES)])
    parts = [w_in[0][:, offs[i]:offs[i + 1]] for i in range(len(IN_SIZES))]
    w_qa, w_ka, w_va, w_qb, w_kc, w_vc, w_ks, w_vs, w_kw, w_vw, w_gn, w_ga, w_gb = parts
    w_qkv_a = jnp.concatenate([w_qa * HEAD_DIM ** -0.5, w_ka, w_va], axis=1).astype(BF16)
    w_nsa = jnp.concatenate([_pad_heads(t) for t in (w_qb, w_kc, w_vc, w_ks, w_vs, w_kw, w_vw)],
                            axis=1).astype(BF16)
    nsa_w = w_nsa.shape[1]
    w_gn = w_gn.reshape(D, 3, G, R).transpose(0, 2, 1, 3).reshape(D, G, 3 * R)
    w_gn = jnp.pad(w_gn, ((0, 0), (0, 0), (0, LANE - 3 * R))).reshape(D, G * LANE)
    w_gate_in = jnp.concatenate([w_ga, w_gb, w_gn], axis=1).astype(BF16)
    gate_w = w_gate_in.shape[1]

    x2 = x.reshape(N, D)
    dils = tuple(dil for _, dil in DIL_PATTERNS if dil > 1)
    qkv_a, *dilated_qkv = _inproj(x2, g_mix, w_qkv_a, BF16, 512, dils=dils)
    nsa_in, = _inproj(x2, g_mix, w_nsa, BF16, 512)
    gt, = _inproj(x2, g_mix, w_gate_in, F32, 768)

    o_list, l_list = [], []
    for _, dil in DIL_PATTERNS:
        src = qkv_a if dil == 1 else dilated_qkv[dils.index(dil)]
        o, l = _dilated(src.reshape(B, S // dil, -1), _dil_bias(slopes_a, dil), dil, B, S)
        o_list.append(o.reshape(-1, N, LANE))
        l_list.append(l.reshape(-1, N, LANE))

    nchunk = S // CMP_STRIDE
    c0 = NSA_HEADS * LANE
    raw = nsa_in[:, c0:c0 + 2 * G * LANE].reshape(B, nchunk, CMP_STRIDE, 2 * G, LANE)[..., :HEAD_DIM]
    xc = raw.transpose(0, 3, 1, 2, 4).reshape(B, 2 * G, nchunk, CMP_STRIDE * HEAD_DIM)
    pos = jnp.stack([cmp_pos_k[0].reshape(1, -1), cmp_pos_v[0].reshape(1, -1)])
    w1 = jnp.stack([cmp_w1_k[0], cmp_w1_v[0]]).astype(BF16)
    b1 = jnp.stack([cmp_b1_k[0][None], cmp_b1_v[0][None]])
    w2 = jnp.pad(jnp.stack([cmp_w2_k[0], cmp_w2_v[0]]), ((0, 0), (0, 0), (0, LANE - HEAD_DIM))).astype(BF16)
    kcv = _compress(xc, pos, w1, b1, w2)
    kcv = kcv.reshape(B, 2, G, nchunk, LANE).transpose(1, 0, 2, 3, 4).reshape(2, B * G, nchunk, LANE)
    vct = kcv[1].transpose(0, 2, 1)
    v0 = (NSA_HEADS + 3 * G) * LANE
    vsw = jnp.stack([nsa_in[:, v0:v0 + G * LANE], nsa_in[:, v0 + 2 * G * LANE:v0 + 3 * G * LANE]])
    vsw = vsw.reshape(2, B, S // LANE, LANE, G, LANE).transpose(1, 4, 0, 2, 5, 3)
    ones_row = (jnp.arange(LANE) == SUM_LANE).astype(BF16)[:, None]
    vt = (vsw + ones_row).reshape(B * G, 2, S // LANE, LANE, LANE)
    yb = _nsa(slopes_b, nsa_in.reshape(B, S, nsa_w), gt.reshape(B, S, gate_w), kcv[0], vct, vt,
              B, S, 0, 2 * D // LANE)
    yb = yb.reshape(N, NSA_HEADS * LANE)

    wb_pad = jnp.pad(w_branch_b[0].reshape(NSA_HEADS, HEAD_DIM, D),
                     ((0, 0), (0, LANE - HEAD_DIM), (0, 0))).reshape(NSA_HEADS * LANE, D)
    wr_t = w_router[0].T
    wr_hi = wr_t.astype(BF16)
    wr_lo = (wr_t - wr_hi.astype(F32)).astype(BF16)
    h1, xn2, logits_t, xr = _merge(o_list, l_list, yb, gt, x2, w_branch_a[0].astype(BF16), wb_pad.astype(BF16),
                               w_out[0].astype(BF16), g_ffn, wr_hi, wr_lo)

    idx_t, wsel_t, rank_t, counts = _route(logits_t, router_bias[0][:, None])
    counts = counts[:, 0]
    padded = (counts + MOE_ROWS - 1) // MOE_ROWS * MOE_ROWS
    pend = jnp.cumsum(padded)
    pstart = pend - padded
    hot = idx_t[None] == jnp.arange(N_EXPERTS, dtype=jnp.int32)[:, None, None]
    pos_t = rank_t + jnp.sum(jnp.where(hot, pstart.astype(jnp.int32)[:, None, None], 0), axis=0)
    NK = N * TOP_K
    nblk = (NK + N_EXPERTS * (MOE_ROWS - 1) + MOE_ROWS - 1) // MOE_ROWS
    P = nblk * MOE_ROWS
    blk_e = jnp.minimum(jnp.sum(pend[None, :] <= (jnp.arange(nblk, dtype=jnp.int32) * MOE_ROWS)[:, None], axis=1),
                        N_EXPERTS - 1).astype(jnp.int32)
    nused = (pend[-1] // MOE_ROWS).astype(jnp.int32).reshape(1)
    xs = _dispatch((pstart + counts).astype(jnp.int32), pend.astype(jnp.int32), nused, pos_t, xr, P, D)
    ys = _moe_experts(blk_e, nused, xs, w_gate[0].astype(BF16), w_up[0].astype(BF16), w_down[0].astype(BF16))
    out = _final(h1, xn2, ys[pos_t], wsel_t.T, p[0].reshape(N, -1), ws_gate[0].astype(BF16), ws_up[0].astype(BF16),
                 ws_down[0].astype(BF16), g_ple, w_ple_gate[0].astype(BF16), w_ple[0].astype(BF16),
                 g_final[None, :])
    return out.reshape(B, S, D)
```

```python
import functools

import numpy as np
import jax
import jax.numpy as jnp
from jax import lax
from jax.experimental import pallas as pl
from jax.experimental.pallas import tpu as pltpu

F32 = jnp.float32
BF16 = jnp.bfloat16

HEAD_DIM = 64
LANE = 128
RMS_EPS = 1e-6
DIL_HEADS = 8
DIL_PATTERNS = ((128, 1), (512, 4), (2048, 16))
DIL_BLOCK = 128
DIL_STEP_ROWS = 4096
NSA_HEADS = 8
NSA_KV_HEADS = 2
NSA_REP = NSA_HEADS // NSA_KV_HEADS
CMP_BLOCK = 32
CMP_STRIDE = 16
CMP_HIDDEN = 256
SLC_BLOCK = 64
SLC_TOP_N = 16
SLC_FORCE = 1e9
WIN_SIZE = 512
N_EXPERTS = 64
TOP_K = 8
N_GROUPS = 8
TOP_GROUPS = 4
GROUP_SIZE = N_EXPERTS // N_GROUPS
ROUTED_SCALE = 2.5
IN_SIZES = (512, 512, 512, 512, 128, 128, 128, 128, 128, 128, 24, 1024, 1024)

NEG = -1e30
NT_DIMS = (((1,), (1,)), ((), ()))

ROW_TILE = 512
INPROJ_ROWS = 1024
NSA_Q = 256
NSA_TK = 512
MOE_ROWS = 512
VMEM_LIMIT = 48 * 1024 * 1024


def _dot(a, b):
    return jnp.dot(a, b, preferred_element_type=F32)


def _dot_nt(a, b):
    return lax.dot_general(a, b, NT_DIMS, preferred_element_type=F32)


def _split3(x):
    hi = x.astype(BF16)
    r1 = x - hi.astype(F32)
    mid = r1.astype(BF16)
    lo = (r1 - mid.astype(F32)).astype(BF16)
    return hi, mid, lo


def _rms(x, g):
    r = lax.rsqrt(jnp.mean(x * x, axis=-1, keepdims=True) + RMS_EPS)
    return (x * r) * g


def _params(sem):
    return pltpu.CompilerParams(dimension_semantics=sem, vmem_limit_bytes=VMEM_LIMIT)


def _inproj_kernel(x_ref, g_ref, w_ref, o_ref, *rest, dils):
    dil_refs, xn_ref = rest[:len(dils)], rest[len(dils)]
    res_ref = rest[len(dils) + 1] if dils else None

    @pl.when(pl.program_id(1) == 0)
    def _():
        xn_ref[...] = _rms(x_ref[...], g_ref[...]).astype(BF16)

    res = _dot(xn_ref[...], w_ref[...])
    o_ref[...] = res.astype(o_ref.dtype)

    if dils:
        tn = res.shape[1]
        for c in range(tn // LANE):
            res_ref[c] = res[:, c * LANE:(c + 1) * LANE]
        for d_ref, dil in zip(dil_refs, dils):
            for r in range(dil):
                for c in range(tn // LANE):
                    piece = res_ref[c, pl.ds(r, INPROJ_ROWS // dil, stride=dil), :]
                    d_ref[:, r * tn + c * LANE:r * tn + (c + 1) * LANE] = piece.astype(d_ref.dtype)


def _inproj(x2, g, w, out_dtype, tn, dils=()):
    n, d = x2.shape
    cols = w.shape[1]
    out_specs = [pl.BlockSpec((INPROJ_ROWS, tn), lambda i, j: (i, j))]
    out_shape = [jax.ShapeDtypeStruct((n, cols), out_dtype)]
    for dil in dils:
        out_specs.append(pl.BlockSpec((INPROJ_ROWS // dil, dil * tn), lambda i, j: (i, j)))
        out_shape.append(jax.ShapeDtypeStruct((n // dil, cols * dil), out_dtype))
    return pl.pallas_call(
        functools.partial(_inproj_kernel, dils=dils),
        grid=(n // INPROJ_ROWS, cols // tn),
        in_specs=[
            pl.BlockSpec((INPROJ_ROWS, d), lambda i, j: (i, 0)),
            pl.BlockSpec((1, d), lambda i, j: (0, 0)),
            pl.BlockSpec((d, tn), lambda i, j: (0, j)),
        ],
        out_specs=out_specs,
        out_shape=out_shape,
        scratch_shapes=[pltpu.VMEM((INPROJ_ROWS, d), BF16)]
        + ([pltpu.VMEM((tn // LANE, INPROJ_ROWS, LANE), F32)] if dils else []),
        compiler_params=_params(("parallel", "arbitrary")),
    )(x2, g, w)


def _dil_kernel(q_ref, kp_ref, kc_ref, vp_ref, vc_ref, bias_ref, o_ref, l_ref, *, dil, nq):
    i = pl.program_id(1)
    res = pl.program_id(2)
    blk = DIL_BLOCK
    lane = lax.broadcasted_iota(jnp.int32, (blk, LANE), 1)
    lo = lane < HEAD_DIM
    for sub in range(nq):
        rows = slice(sub * blk, (sub + 1) * blk)
        prev = slice((sub - 1) * blk, sub * blk)
        out_rows = pl.ds(sub * blk * dil + res, blk, stride=dil) if dil > 1 else rows
        variant = jnp.minimum(i, 1) * DIL_HEADS if sub == 0 else DIL_HEADS
        for hp in range(DIL_HEADS // 2):
            sl = slice(hp * LANE, (hp + 1) * LANE)
            q2 = q_ref[0, rows, sl]
            k_prev = kp_ref[0, :, sl] if sub == 0 else kc_ref[0, prev, sl]
            v_prev = vp_ref[0, :, sl] if sub == 0 else vc_ref[0, prev, sl]
            k2 = jnp.concatenate([k_prev, kc_ref[0, rows, sl]], axis=0)
            v2 = jnp.concatenate([v_prev, vc_ref[0, rows, sl]], axis=0)
            outs, lses = [], []
            for hh in range(2):
                keep = lo if hh == 0 else jnp.logical_not(lo)
                qm = jnp.where(keep, q2, jnp.zeros_like(q2))
                s = _dot_nt(qm, k2) + bias_ref[variant + 2 * hp + hh]
                m = jnp.max(s, axis=-1, keepdims=True)
                e = jnp.exp(s - m)
                den = jnp.sum(e, axis=-1, keepdims=True)
                outs.append(_dot(e.astype(BF16), v2) / den)
                lses.append(jnp.broadcast_to(m + jnp.log(den), (blk, LANE)))
            o_ref[hp, 0, out_rows, :] = jnp.where(lo, outs[0], outs[1])
            l_ref[hp, 0, out_rows, :] = jnp.where(lo, lses[0], lses[1])


def _dil_bias(slopes_a, dil):
    blk = DIL_BLOCK
    qi = jnp.arange(blk)[:, None]
    kj = jnp.arange(2 * blk)[None, :]
    dist = (qi + blk - kj).astype(F32)
    band = (dist >= 0) & (dist <= blk)
    slope = (slopes_a * dil)[:, None, None]
    bias = -slope * dist
    first = jnp.where(band & (kj >= blk), bias, NEG)
    rest = jnp.where(band, bias, NEG)
    return jnp.concatenate([first, rest], axis=0)


def _dilated(src3, bias, dil, b, s):
    nq = min(8, DIL_STEP_ROWS // (DIL_BLOCK * dil))
    nstep = s // dil // (DIL_BLOCK * nq)
    aw = DIL_HEADS * HEAD_DIM
    npair = DIL_HEADS // 2

    def spec(which, prev):
        if prev:
            return pl.BlockSpec((1, DIL_BLOCK, aw), lambda bb, i, r: (bb, jnp.maximum(nq * i - 1, 0), which * dil + r))
        return pl.BlockSpec((1, DIL_BLOCK * nq, aw), lambda bb, i, r: (bb, i, which * dil + r))

    out_spec = pl.BlockSpec((npair, 1, DIL_BLOCK * nq * dil, LANE), lambda bb, i, r: (0, bb, i, 0))
    return pl.pallas_call(
        functools.partial(_dil_kernel, dil=dil, nq=nq),
        grid=(b, nstep, dil),
        in_specs=[
            spec(0, False), spec(1, True), spec(1, False), spec(2, True), spec(2, False),
            pl.BlockSpec(bias.shape, lambda bb, i, r: (0, 0, 0)),
        ],
        out_specs=[out_spec, out_spec],
        out_shape=[jax.ShapeDtypeStruct((npair, b, s, LANE), F32)] * 2,
        compiler_params=_params(("parallel", "arbitrary", "arbitrary")),
    )(src3, src3, src3, src3, src3, bias)


def _cmp_kernel(x_ref, pos_ref, w1_ref, b1_ref, w2_ref, o_ref):
    half = x_ref.shape[-1]
    x = x_ref[0, 0]
    w1 = w1_ref[0]
    first = _dot(x, w1[:half])
    second = _dot(x, w1[half:])
    nrow = x.shape[0]
    pre = first + pltpu.roll(second, nrow - 1, 0)
    posb = _dot(jnp.broadcast_to(pos_ref[0], (8, 2 * half)).astype(BF16), w1)[0:1]
    pre = pre + posb + b1_ref[0]
    o_ref[0, 0] = _dot(jax.nn.gelu(pre).astype(BF16), w2_ref[0]).astype(o_ref.dtype)


def _compress(xc, pos, w1, b1, w2):
    b, four, nchunk, width = xc.shape
    return pl.pallas_call(
        _cmp_kernel,
        grid=(b, four),
        in_specs=[
            pl.BlockSpec((1, 1, nchunk, width), lambda bb, j: (bb, j, 0, 0)),
            pl.BlockSpec((1, 1, 2 * width), lambda bb, j: (j // 2, 0, 0)),
            pl.BlockSpec((1, 2 * width, CMP_HIDDEN), lambda bb, j: (j // 2, 0, 0)),
            pl.BlockSpec((1, 1, CMP_HIDDEN), lambda bb, j: (j // 2, 0, 0)),
            pl.BlockSpec((1, CMP_HIDDEN, LANE), lambda bb, j: (j // 2, 0, 0)),
        ],
        out_specs=pl.BlockSpec((1, 1, nchunk, LANE), lambda bb, j: (bb, j, 0, 0)),
        out_shape=jax.ShapeDtypeStruct((b, four, nchunk, LANE), BF16),
        compiler_params=_params(("parallel", "parallel")),
    )(xc, pos, w1, b1, w2)


POS_LANE = HEAD_DIM
POS_SPLIT = 64
SUM_LANE = HEAD_DIM
MASKED = -1e30
ROW_FLOOR = -1e29


def _pos_columns(pos):
    out = np.zeros((pos.shape[0], LANE), np.float32)
    out[:, POS_LANE] = pos // POS_SPLIT
    out[:, POS_LANE + 1] = pos % POS_SPLIT
    out[:, POS_LANE + 2] = 1.0
    out[:, POS_LANE + 3] = 1.0
    return jnp.asarray(out, BF16)


def _nsa_kernel(slope_ref, q_ref, kc_ref, vct_ref, ks_ref, kw_ref, vt_ref, gt_ref, ov_ref, kpa_ref, cpa_ref,
                oh_ref, o_ref, kaug_ref, flag_ref, m_ref, acc_ref, *, seq):
    g = pl.program_id(1)
    c = pl.program_id(2)
    C = NSA_Q
    R = NSA_REP
    TK = NSA_TK
    rows = R * C
    t0 = c * C
    scale = HEAD_DIM ** -0.5
    nb = seq // SLC_BLOCK
    ncmp = seq // CMP_STRIDE
    n_sel = min(SLC_TOP_N, nb)
    vblocks = TK // LANE

    @pl.when(c == 0)
    def _():
        kaug_ref[:, 0:LANE] = ks_ref[0] + kpa_ref[...]
        kaug_ref[:, LANE:2 * LANE] = oh_ref[...]

    row = lax.broadcasted_iota(jnp.int32, (rows, 1), 0)
    ti = t0 + row % C
    slope = jnp.zeros((rows, 1), F32)
    for r in range(R):
        slope = jnp.where(row // C == r, slope_ref[g * R + r], slope)
    lane = lax.broadcasted_iota(jnp.int32, (rows, LANE), 1)
    thi = (ti // POS_SPLIT).astype(F32)
    tlo = (ti % POS_SPLIT).astype(F32)
    qcols = jnp.where(lane == POS_LANE, POS_SPLIT * slope,
                      jnp.where(lane == POS_LANE + 1, slope,
                                jnp.where(lane == POS_LANE + 2, -POS_SPLIT * slope * thi,
                                          jnp.where(lane == POS_LANE + 3, -slope * tlo, 0.0))))
    q4 = jnp.concatenate([q_ref[0, :, r * LANE:(r + 1) * LANE] for r in range(R)], axis=0)
    q4 = (q4.astype(F32) * scale + qcols).astype(BF16)
    tq = t0 + lax.broadcasted_iota(jnp.int32, (1, C), 1)

    def heads(x):
        return jnp.concatenate([x] * R, axis=1)

    kc = kc_ref[0] + cpa_ref[...]
    ci = lax.broadcasted_iota(jnp.int32, (ncmp, 1), 0)
    vis = (ci * CMP_STRIDE + (CMP_BLOCK - 1) <= tq) & (ci < ncmp - 1)
    s = _dot_nt(kc, q4) + heads(jnp.where(vis, 0.0, MASKED))
    m = jnp.maximum(jnp.max(s, axis=0, keepdims=True), ROW_FLOOR)
    e = jnp.exp(s - m)
    den = jnp.sum(e, axis=0, keepdims=True)
    pc = e * (1.0 / jnp.where(den > 0, den, 1.0))
    o_cmp = _dot(vct_ref[0], pc.astype(BF16))
    pcsum = pc[:, 0:C]
    for r in range(1, R):
        pcsum = pcsum + pc[:, r * C:(r + 1) * C]
    ov = ov_ref[...]
    p_slc = sum(_dot(ov, part) for part in _split3(pcsum))

    def values_t(which, first_block, n):
        return jnp.concatenate([vt_ref[0, which, first_block + i] for i in range(n)], axis=1)

    span = WIN_SIZE + C
    ws = pl.multiple_of(jnp.maximum(t0 - WIN_SIZE, 0), C)
    k_w = kw_ref[0, pl.ds(ws, span), :] + kpa_ref[pl.ds(ws, span), :]
    wpos = ws + lax.broadcasted_iota(jnp.int32, (span, 1), 0)
    wbias = jnp.where((wpos <= tq) & (wpos >= tq - (WIN_SIZE - 1)), 0.0, MASKED)
    s3 = _dot_nt(k_w, q4) + heads(wbias)
    e3 = jnp.exp(s3 - jnp.max(s3, axis=0, keepdims=True))
    ow = _dot(values_t(1, ws // LANE, span // LANE), e3.astype(BF16))
    o_win = ow / ow[SUM_LANE:SUM_LANE + 1, :]

    j = lax.broadcasted_iota(jnp.int32, (LANE, C), 0)
    cur = tq // SLC_BLOCK
    forced = (j == 0) | (j == cur) | (j == cur - 1)
    future = j * SLC_BLOCK > tq
    work = jnp.where(forced, -jnp.inf, jnp.where(future, -SLC_FORCE, p_slc))
    work = jnp.where(j < nb, work, -jnp.inf)
    sel = jnp.where(forced, 1.0, 0.0)
    jf = j.astype(F32)
    for _ in range(n_sel - 3):
        mx = jnp.max(work, axis=0, keepdims=True)
        first = jnp.min(jnp.where(work == mx, jf, float(LANE)), axis=0, keepdims=True)
        pick = jf == first
        sel = jnp.where(pick, 1.0, sel)
        work = jnp.where(pick, -jnp.inf, work)
    sel = jnp.where(future, 0.0, sel)
    blocks_per_tile = TK // SLC_BLOCK
    for kt in range(seq // TK):
        flag_ref[kt] = jnp.max(sel[kt * blocks_per_tile:(kt + 1) * blocks_per_tile, :])
    sel_bias = jnp.where(sel.T > 0.5, 0.0, MASKED).astype(BF16)
    q_aug = jnp.concatenate([q4, jnp.concatenate([sel_bias] * R, axis=0)], axis=1)

    m_ref[...] = jnp.full((1, rows), ROW_FLOOR, F32)
    acc_ref[...] = jnp.zeros((LANE, rows), F32)
    kio = lax.broadcasted_iota(jnp.int32, (TK, 1), 0)

    def tile_update(kt, causal):
        start = pl.multiple_of(kt * TK, TK)
        s2 = _dot_nt(kaug_ref[pl.ds(start, TK), :], q_aug)
        if causal:
            s2 = s2 + heads(jnp.where(start + kio <= tq, 0.0, MASKED))
        m_i = m_ref[...]
        m_n = jnp.maximum(m_i, jnp.max(s2, axis=0, keepdims=True))
        alpha = jnp.exp(m_i - m_n)
        p2 = jnp.exp(s2 - m_n)
        acc_ref[...] = alpha * acc_ref[...] + _dot(values_t(0, kt * vblocks, vblocks), p2.astype(BF16))
        m_ref[...] = m_n

    def slc_body(kt, carry):
        @pl.when(flag_ref[kt] > 0.0)
        def _():
            tile_update(kt, False)

        return carry

    last = (t0 + C - 1) // TK
    lax.fori_loop(0, last, slc_body, 0)
    tile_update(last, True)
    acc = acc_ref[...]
    o_slc = acc / acc[SUM_LANE:SUM_LANE + 1, :]

    gates = jax.nn.sigmoid(gt_ref[0]).T
    for r in range(R):
        rs = slice(r * C, (r + 1) * C)
        out = (gates[r:r + 1] * o_cmp[:, rs] + gates[R + r:R + r + 1] * o_slc[:, rs]
               + gates[2 * R + r:2 * R + r + 1] * o_win[:, rs])
        o_ref[0, :, r * LANE:(r + 1) * LANE] = out.T.astype(o_ref.dtype)


def _nsa(slopes_b, pa3, gt3, kc, vct, vt, b, s, col0, gcol0):
    C = NSA_Q
    R = NSA_REP
    G = NSA_KV_HEADS
    nb = s // SLC_BLOCK
    ncmp = s // CMP_STRIDE
    assert nb <= LANE
    qc0 = col0 // R
    kv0 = col0 + NSA_HEADS
    rows = R * C

    ci = np.arange(ncmp)[None, :]
    sj = np.arange(LANE)[:, None]
    ov = ((ci * CMP_STRIDE < (sj + 1) * SLC_BLOCK) & (ci * CMP_STRIDE + CMP_BLOCK > sj * SLC_BLOCK)
          & (ci < ncmp - 1) & (sj < nb))
    ov = jnp.asarray(ov, BF16)
    kpa = _pos_columns(np.arange(s))
    cpa = _pos_columns(np.arange(ncmp) * CMP_STRIDE + (CMP_BLOCK - 1))
    oh = jnp.asarray(np.arange(s)[:, None] // SLC_BLOCK == np.arange(LANE)[None, :], BF16)

    def k_spec(which):
        return pl.BlockSpec((1, s, LANE), lambda bb, g, c: (bb, 0, kv0 + which * G + g))

    const = lambda a: pl.BlockSpec(a.shape, lambda bb, g, c: (0,) * a.ndim)
    return pl.pallas_call(
        functools.partial(_nsa_kernel, seq=s),
        grid=(b, G, s // C),
        in_specs=[
            pl.BlockSpec(memory_space=pltpu.SMEM),
            pl.BlockSpec((1, C, R * LANE), lambda bb, g, c: (bb, c, qc0 + g)),
            pl.BlockSpec((1, ncmp, LANE), lambda bb, g, c: (bb * G + g, 0, 0)),
            pl.BlockSpec((1, LANE, ncmp), lambda bb, g, c: (bb * G + g, 0, 0)),
            k_spec(2), k_spec(4),
            pl.BlockSpec((1, 2, s // LANE, LANE, LANE), lambda bb, g, c: (bb * G + g, 0, 0, 0, 0)),
            pl.BlockSpec((1, C, LANE), lambda bb, g, c: (bb, c, gcol0 + g)),
            const(ov), const(kpa), const(cpa), const(oh),
        ],
        out_specs=pl.BlockSpec((1, C, R * LANE), lambda bb, g, c: (bb, c, g)),
        out_shape=jax.ShapeDtypeStruct((b, s, NSA_HEADS * LANE), BF16),
        scratch_shapes=[
            pltpu.VMEM((s, 2 * LANE), BF16),
            pltpu.SMEM((s // NSA_TK,), F32),
            pltpu.VMEM((1, rows), F32),
            pltpu.VMEM((LANE, rows), F32),
        ],
        compiler_params=_params(("parallel", "parallel", "arbitrary")),
    )(slopes_b, pa3, kc, vct, pa3, pa3, vt, gt3, ov, kpa, cpa, oh)


def _merge_kernel(o1, o2, o3, l1, l2, l3, yb, ga, gb, x_ref, wa, wb, wo, gf, wrh, wrl,
                  h_ref, xn_ref, lg_ref, xr_ref):
    def pairs(ref):
        return jnp.concatenate([ref[h] for h in range(ref.shape[0])], axis=1)

    a1, a2, a3 = pairs(l1), pairs(l2), pairs(l3)
    mx = jnp.maximum(jnp.maximum(a1, a2), a3)
    e1, e2, e3 = jnp.exp(a1 - mx), jnp.exp(a2 - mx), jnp.exp(a3 - mx)
    tot = e1 + e2 + e3
    ya = (e1 / tot) * pairs(o1) + (e2 / tot) * pairs(o2) + (e3 / tot) * pairs(o3)
    merged = (jax.nn.sigmoid(ga[...]) * _dot(ya.astype(BF16), wa[...])
              + jax.nn.sigmoid(gb[...]) * _dot(yb[...], wb[...]))
    h = x_ref[...] + _dot(merged.astype(BF16), wo[...])
    h_ref[...] = h
    xn = _rms(h, gf[...])
    xn_ref[...] = xn.astype(BF16)
    tm, d = xn.shape
    for c in range(d // LANE):
        xr_ref[pl.ds(c, tm, stride=d // LANE), :] = xn[:, c * LANE:(c + 1) * LANE]
    xh = xn.astype(BF16)
    xl = (xn - xh.astype(F32)).astype(BF16)
    lg_ref[...] = _dot_nt(wrh[...], xh) + _dot_nt(wrh[...], xl) + _dot_nt(wrl[...], xh)


def _merge(o_list, l_list, yb, gt, x2, wa, wb, wo, gf, wrh, wrl):
    n, d = x2.shape
    aw = wa.shape[0]
    bw = wb.shape[0]
    tm = ROW_TILE
    row = lambda w: pl.BlockSpec((tm, w), lambda i: (i, 0))
    full = lambda a: pl.BlockSpec(a.shape, lambda i: (0,) * a.ndim)
    return pl.pallas_call(
        _merge_kernel,
        grid=(n // tm,),
        in_specs=[pl.BlockSpec((aw // LANE, tm, LANE), lambda i: (0, i, 0))] * 6 + [
            row(bw),
            pl.BlockSpec((tm, d), lambda i: (i, 0)),
            pl.BlockSpec((tm, d), lambda i: (i, 1)),
            row(d), full(wa), full(wb), full(wo), full(gf), full(wrh), full(wrl),
        ],
        out_specs=[row(d), row(d), pl.BlockSpec((N_EXPERTS, tm), lambda i: (0, i)),
                   pl.BlockSpec((tm * d // LANE, LANE), lambda i: (i, 0))],
        out_shape=[
            jax.ShapeDtypeStruct((n, d), F32),
            jax.ShapeDtypeStruct((n, d), BF16),
            jax.ShapeDtypeStruct((N_EXPERTS, n), F32),
            jax.ShapeDtypeStruct((n * d // LANE, LANE), F32),
        ],
        compiler_params=_params(("parallel",)),
    )(*o_list, *l_list, yb, gt, gt, x2, wa, wb, wo, gf, wrh, wrl)


def _route_kernel(lg_ref, bias_ref, tri_ref, idx_ref, w_ref, rank_ref, cnt_ref, carry_ref):
    E = N_EXPERTS
    tn = lg_ref.shape[1]
    scores = jax.nn.sigmoid(lg_ref[...])
    sel = scores + bias_ref[...]
    sub = lax.broadcasted_iota(jnp.int32, (GROUP_SIZE, tn), 0).astype(F32)
    gs = []
    for grp in range(N_GROUPS):
        blk = sel[grp * GROUP_SIZE:(grp + 1) * GROUP_SIZE]
        m1 = jnp.max(blk, axis=0, keepdims=True)
        first = jnp.min(jnp.where(blk == m1, sub, float(GROUP_SIZE)), axis=0, keepdims=True)
        m2 = jnp.max(jnp.where(sub == first, -jnp.inf, blk), axis=0, keepdims=True)
        gs.append(m1 + m2)
    keep = []
    for a in range(N_GROUPS):
        rank = jnp.zeros((1, tn), jnp.int32)
        for o in range(N_GROUPS):
            if o == a:
                continue
            beats = (gs[o] >= gs[a]) if o < a else (gs[o] > gs[a])
            rank = rank + beats.astype(jnp.int32)
        keep.append(jnp.broadcast_to(rank < TOP_GROUPS, (GROUP_SIZE, tn)))
    emask = jnp.concatenate(keep, axis=0)
    cand = jnp.where(emask, sel, -jnp.inf)
    eid = lax.broadcasted_iota(jnp.int32, (E, tn), 0)
    eid_f = eid.astype(F32)
    rank = jnp.zeros((E, tn), jnp.int32)
    for o in range(E):
        other = cand[o:o + 1]
        beats = (other > cand) | ((other == cand) & (eid > o))
        rank = rank + beats.astype(jnp.int32)
    chosen = emask & (rank < TOP_K)
    wsel = jnp.where(chosen, scores, 0.0)
    wnorm = wsel / jnp.sum(wsel, axis=0, keepdims=True) * ROUTED_SCALE

    @pl.when(pl.program_id(0) == 0)
    def _():
        carry_ref[...] = jnp.zeros_like(carry_ref)

    chosen_f = jnp.where(chosen, 1.0, 0.0)
    before = carry_ref[...] + _dot(chosen_f.astype(BF16), tri_ref[...]) - chosen_f
    carry_ref[...] = carry_ref[...] + jnp.sum(chosen_f, axis=1, keepdims=True)
    cnt_ref[...] = jnp.broadcast_to(carry_ref[...], cnt_ref.shape).astype(jnp.int32)
    for k in range(TOP_K):
        hit = chosen & (rank == k)
        idx_ref[k:k + 1, :] = jnp.sum(jnp.where(hit, eid_f, 0.0), axis=0, keepdims=True).astype(jnp.int32)
        w_ref[k:k + 1, :] = jnp.sum(jnp.where(hit, wnorm, 0.0), axis=0, keepdims=True)
        rank_ref[k:k + 1, :] = jnp.sum(jnp.where(hit, before, 0.0), axis=0, keepdims=True).astype(jnp.int32)


def _route(logits_t, bias_col):
    e, n = logits_t.shape
    tn = ROW_TILE
    tri = jnp.asarray(np.triu(np.ones((tn, tn), np.float32)), BF16)
    slot = pl.BlockSpec((TOP_K, tn), lambda i: (0, i))
    return pl.pallas_call(
        _route_kernel,
        grid=(n // tn,),
        in_specs=[pl.BlockSpec((e, tn), lambda i: (0, i)), pl.BlockSpec((e, 1), lambda i: (0, 0)),
                  pl.BlockSpec((tn, tn), lambda i: (0, 0))],
        out_specs=[slot, slot, slot, pl.BlockSpec((e, LANE), lambda i: (0, 0))],
        out_shape=[
            jax.ShapeDtypeStruct((TOP_K, n), jnp.int32),
            jax.ShapeDtypeStruct((TOP_K, n), F32),
            jax.ShapeDtypeStruct((TOP_K, n), jnp.int32),
            jax.ShapeDtypeStruct((e, LANE), jnp.int32),
        ],
        scratch_shapes=[pltpu.VMEM((e, 1), F32)],
        compiler_params=_params(("arbitrary",)),
    )(logits_t, bias_col, tri)


def _dispatch_kernel(fill_lo_ref, fill_hi_ref, nused_ref, pos_ref, x_ref, xs_ref, zero_ref, sem, zsem):
    i = pl.program_id(0)
    tm = pos_ref.shape[1]
    sub = x_ref.shape[0] // tm
    blk_rows = zero_ref.shape[0]
    nblk = xs_ref.shape[0] // blk_rows

    def slot(r):
        return xs_ref.at[pl.ds(pl.multiple_of(r * sub, sub), sub), :]

    @pl.when(i == 0)
    def _():
        zero_ref[...] = jnp.zeros_like(zero_ref)

        def per_expert(e, carry):
            def fill(r, c2):
                pltpu.make_async_copy(zero_ref.at[pl.ds(0, sub), :], slot(r), zsem).start()
                return c2

            def drain(r, c2):
                pltpu.make_async_copy(zero_ref.at[pl.ds(0, sub), :], slot(r), zsem).wait()
                return c2

            lax.fori_loop(fill_lo_ref[e], fill_hi_ref[e], fill, 0)
            lax.fori_loop(fill_lo_ref[e], fill_hi_ref[e], drain, 0)
            return carry

        lax.fori_loop(0, fill_lo_ref.shape[0], per_expert, 0)

        def tail(bk, carry):
            cp = pltpu.make_async_copy(
                zero_ref, xs_ref.at[pl.ds(pl.multiple_of(bk * blk_rows, blk_rows), blk_rows), :], zsem)
            cp.start()
            cp.wait()
            return carry

        lax.fori_loop(nused_ref[0], nblk, tail, 0)

    def scatter(t, carry):
        src = x_ref.at[pl.ds(pl.multiple_of(t * sub, sub), sub), :]
        for k in range(TOP_K):
            pltpu.make_async_copy(src, slot(pos_ref[k, t]), sem).start()
        return carry

    lax.fori_loop(0, tm, scatter, 0)
    for k in range(TOP_K):
        pltpu.make_async_copy(x_ref, xs_ref.at[pl.ds(0, tm * sub), :], sem).wait()


def _dispatch(fill_lo, fill_hi, nused, pos_t, xr, p_rows, d):
    k, n = pos_t.shape
    sub = d // LANE
    tm = ROW_TILE
    grid_spec = pltpu.PrefetchScalarGridSpec(
        num_scalar_prefetch=3,
        grid=(n // tm,),
        in_specs=[
            pl.BlockSpec((k, tm), lambda i, *_: (0, i), memory_space=pltpu.SMEM),
            pl.BlockSpec((tm * sub, LANE), lambda i, *_: (i, 0)),
        ],
        out_specs=pl.BlockSpec(memory_space=pl.ANY),
        scratch_shapes=[pltpu.VMEM((MOE_ROWS * sub, LANE), F32), pltpu.SemaphoreType.DMA(()),
                        pltpu.SemaphoreType.DMA(())],
    )
    return pl.pallas_call(
        _dispatch_kernel,
        grid_spec=grid_spec,
        out_shape=jax.ShapeDtypeStruct((p_rows * sub, LANE), F32),
        compiler_params=_params(("arbitrary",)),
    )(fill_lo, fill_hi, nused, pos_t, xr)


def _moe_kernel(blk_e_ref, nused_ref, xs_ref, wg_ref, wu_ref, wd_ref, o_ref):
    i = pl.program_id(0)

    @pl.when(i < nused_ref[0])
    def _():
        sub = xs_ref.shape[0] // MOE_ROWS
        x = jnp.concatenate([xs_ref[pl.ds(c, MOE_ROWS, stride=sub), :].astype(BF16) for c in range(sub)], axis=1)
        hg = _dot(x, wg_ref[0])
        hid = (hg * jax.nn.sigmoid(hg)) * _dot(x, wu_ref[0])
        o_ref[...] = _dot(hid.astype(BF16), wd_ref[0]).astype(o_ref.dtype)

    @pl.when(i >= nused_ref[0])
    def _():
        o_ref[...] = jnp.zeros_like(o_ref)


def _moe_experts(blk_e, nused, xs, wg, wu, wd):
    d = wg.shape[1]
    sub = d // LANE
    p = xs.shape[0] // sub
    hid = wg.shape[2]
    grid_spec = pltpu.PrefetchScalarGridSpec(
        num_scalar_prefetch=2,
        grid=(p // MOE_ROWS,),
        in_specs=[
            pl.BlockSpec((MOE_ROWS * sub, LANE), lambda i, be, nu: (jnp.minimum(i, nu[0] - 1), 0)),
            pl.BlockSpec((1, d, hid), lambda i, be, nu: (be[i], 0, 0)),
            pl.BlockSpec((1, d, hid), lambda i, be, nu: (be[i], 0, 0)),
            pl.BlockSpec((1, hid, d), lambda i, be, nu: (be[i], 0, 0)),
        ],
        out_specs=pl.BlockSpec((MOE_ROWS, d), lambda i, be, nu: (i, 0)),
    )
    return pl.pallas_call(
        _moe_kernel,
        grid_spec=grid_spec,
        out_shape=jax.ShapeDtypeStruct((p, d), BF16),
        compiler_params=_params(("arbitrary",)),
    )(blk_e, nused, xs, wg, wu, wd)


def _final_kernel(h_ref, xn_ref, ys_ref, w_ref, p_ref, wsg, wsu, wsd, gp, wpg, wp, gfin, o_ref):
    xn = xn_ref[...]
    hg = _dot(xn, wsg[...])
    shared = _dot(((hg * jax.nn.sigmoid(hg)) * _dot(xn, wsu[...])).astype(BF16), wsd[...])
    w = w_ref[...]
    y = w[:, 0:1] * ys_ref[0].astype(F32)
    for k in range(1, TOP_K):
        y = y + w[:, k:k + 1] * ys_ref[k].astype(F32)
    h = h_ref[...] + (y + shared)
    gate = jax.nn.sigmoid(_dot(_rms(h, gp[...]).astype(BF16), wpg[...]))
    h = h + _dot(p_ref[...].astype(BF16), wp[...]) * gate
    o_ref[...] = _rms(h, gfin[...])


def _final(h1, xn2, ysg, w_nk, p2, wsg, wsu, wsd, gp, wpg, wp, gfin):
    n, d = h1.shape
    tm = ROW_TILE
    row = lambda w: pl.BlockSpec((tm, w), lambda i: (i, 0))
    full = lambda a: pl.BlockSpec(a.shape, lambda i: (0,) * a.ndim)
    return pl.pallas_call(
        _final_kernel,
        grid=(n // tm,),
        in_specs=[row(d), row(d), pl.BlockSpec((TOP_K, tm, d), lambda i: (0, i, 0)), row(TOP_K), row(p2.shape[1]),
                  full(wsg), full(wsu), full(wsd), full(gp), full(wpg), full(wp), full(gfin)],
        out_specs=row(d),
        out_shape=jax.ShapeDtypeStruct((n, d), F32),
        compiler_params=_params(("parallel",)),
    )(h1, xn2, ysg, w_nk, p2, wsg, wsu, wsd, gp, wpg, wp, gfin)


def _pad_heads(w):
    d, c = w.shape
    nh = c // HEAD_DIM
    w = w.reshape(d, nh, HEAD_DIM)
    return jnp.pad(w, ((0, 0), (0, 0), (0, LANE - HEAD_DIM))).reshape(d, nh * LANE)


def _alibi_slopes(n):
    return 2.0 ** (-8.0 * jnp.arange(1, n + 1, dtype=jnp.float32) / n)


def kernel(x, p, g_mix, w_in, cmp_pos_k, cmp_w1_k, cmp_b1_k, cmp_w2_k, cmp_pos_v, cmp_w1_v, cmp_b1_v, cmp_w2_v, w_branch_a, w_branch_b, w_out, g_ffn, w_router, router_bias, w_gate, w_up, w_down, ws_gate, ws_up, ws_down, g_ple, w_ple_gate, w_ple, g_final):
    B, S, D = x.shape
    N = B * S
    G, R = NSA_KV_HEADS, NSA_REP
    assert w_in.shape[0] == 1 and p.shape[0] == 1
    assert S % DIL_STEP_ROWS == 0 and S % NSA_TK == 0 and S >= WIN_SIZE + NSA_Q
    assert N % ROW_TILE == 0 and N % INPROJ_ROWS == 0 and S % INPROJ_ROWS == 0
    slopes = _alibi_slopes(DIL_HEADS + NSA_HEADS)
    slopes_a, slopes_b = slopes[0::2], slopes[1::2]

    offs = np.concatenate([[0], np.cumsum(IN_SIZES)])
    parts = [w_in[0][:, offs[i]:offs[i + 1]] for i in range(len(IN_SIZES))]
    w_qa, w_ka, w_va, w_qb, w_kc, w_vc, w_ks, w_vs, w_kw, w_vw, w_gn, w_ga, w_gb = parts
    w_qkv_a = jnp.concatenate([w_qa * HEAD_DIM ** -0.5, w_ka, w_va], axis=1).astype(BF16)
    w_nsa = jnp.concatenate([_pad_heads(t) for t in (w_qb, w_kc, w_vc, w_ks, w_vs, w_kw, w_vw)],
                            axis=1).astype(BF16)
    nsa_w = w_nsa.shape[1]
    w_gn = w_gn.reshape(D, 3, G, R).transpose(0, 2, 1, 3).reshape(D, G, 3 * R)
    w_gn = jnp.pad(w_gn, ((0, 0), (0, 0), (0, LANE - 3 * R))).reshape(D, G * LANE)
    w_gate_in = jnp.concatenate([w_ga, w_gb, w_gn], axis=1).astype(BF16)
    gate_w = w_gate_in.shape[1]

    x2 = x.reshape(N, D)
    dils = tuple(dil for _, dil in DIL_PATTERNS if dil > 1)
    qkv_a, *dilated_qkv = _inproj(x2, g_mix, w_qkv_a, BF16, 512, dils=dils)
    nsa_in, = _inproj(x2, g_mix, w_nsa, BF16, 512)
    gt, = _inproj(x2, g_mix, w_gate_in, F32, 768)

    o_list, l_list = [], []
    for _, dil in DIL_PATTERNS:
        src = qkv_a if dil == 1 else dilated_qkv[dils.index(dil)]
        o, l = _dilated(src.reshape(B, S // dil, -1), _dil_bias(slopes_a, dil), dil, B, S)
        o_list.append(o.reshape(-1, N, LANE))
        l_list.append(l.reshape(-1, N, LANE))

    nchunk = S // CMP_STRIDE
    c0 = NSA_HEADS * LANE
    raw = nsa_in[:, c0:c0 + 2 * G * LANE].reshape(B, nchunk, CMP_STRIDE, 2 * G, LANE)[..., :HEAD_DIM]
    xc = raw.transpose(0, 3, 1, 2, 4).reshape(B, 2 * G, nchunk, CMP_STRIDE * HEAD_DIM)
    pos = jnp.stack([cmp_pos_k[0].reshape(1, -1), cmp_pos_v[0].reshape(1, -1)])
    w1 = jnp.stack([cmp_w1_k[0], cmp_w1_v[0]]).astype(BF16)
    b1 = jnp.stack([cmp_b1_k[0][None], cmp_b1_v[0][None]])
    w2 = jnp.pad(jnp.stack([cmp_w2_k[0], cmp_w2_v[0]]), ((0, 0), (0, 0), (0, LANE - HEAD_DIM))).astype(BF16)
    kcv = _compress(xc, pos, w1, b1, w2)
    kcv = kcv.reshape(B, 2, G, nchunk, LANE).transpose(1, 0, 2, 3, 4).reshape(2, B * G, nchunk, LANE)
    vct = kcv[1].transpose(0, 2, 1)
    v0 = (NSA_HEADS + 3 * G) * LANE
    vsw = jnp.stack([nsa_in[:, v0:v0 + G * LANE], nsa_in[:, v0 + 2 * G * LANE:v0 + 3 * G * LANE]])
    vsw = vsw.reshape(2, B, S // LANE, LANE, G, LANE).transpose(1, 4, 0, 2, 5, 3)
    ones_row = (jnp.arange(LANE) == SUM_LANE).astype(BF16)[:, None]
    vt = (vsw + ones_row).reshape(B * G, 2, S // LANE, LANE, LANE)
    yb = _nsa(slopes_b, nsa_in.reshape(B, S, nsa_w), gt.reshape(B, S, gate_w), kcv[0], vct, vt,
              B, S, 0, 2 * D // LANE)
    yb = yb.reshape(N, NSA_HEADS * LANE)

    wb_pad = jnp.pad(w_branch_b[0].reshape(NSA_HEADS, HEAD_DIM, D),
                     ((0, 0), (0, LANE - HEAD_DIM), (0, 0))).reshape(NSA_HEADS * LANE, D)
    wr_t = w_router[0].T
    wr_hi = wr_t.astype(BF16)
    wr_lo = (wr_t - wr_hi.astype(F32)).astype(BF16)
    h1, xn2, logits_t, xr = _merge(o_list, l_list, yb, gt, x2, w_branch_a[0].astype(BF16), wb_pad.astype(BF16),
                               w_out[0].astype(BF16), g_ffn, wr_hi, wr_lo)

    idx_t, wsel_t, rank_t, counts = _route(logits_t, router_bias[0][:, None])
    counts = counts[:, 0]
    padded = (counts + MOE_ROWS - 1) // MOE_ROWS * MOE_ROWS
    pend = jnp.cumsum(padded)
    pstart = pend - padded
    hot = idx_t[None] == jnp.arange(N_EXPERTS, dtype=jnp.int32)[:, None, None]
    pos_t = rank_t + jnp.sum(jnp.where(hot, pstart.astype(jnp.int32)[:, None, None], 0), axis=0)
    NK = N * TOP_K
    nblk = (NK + N_EXPERTS * (MOE_ROWS - 1) + MOE_ROWS - 1) // MOE_ROWS
    P = nblk * MOE_ROWS
    blk_e = jnp.minimum(jnp.sum(pend[None, :] <= (jnp.arange(nblk, dtype=jnp.int32) * MOE_ROWS)[:, None], axis=1),
                        N_EXPERTS - 1).astype(jnp.int32)
    nused = (pend[-1] // MOE_ROWS).astype(jnp.int32).reshape(1)
    xs = _dispatch((pstart + counts).astype(jnp.int32), pend.astype(jnp.int32), nused, pos_t, xr, P, D)
    ys = _moe_experts(blk_e, nused, xs, w_gate[0].astype(BF16), w_up[0].astype(BF16), w_down[0].astype(BF16))
    out = _final(h1, xn2, ys[pos_t], wsel_t.T, p[0].reshape(N, -1), ws_gate[0].astype(BF16), ws_up[0].astype(BF16),
                 ws_down[0].astype(BF16), g_ple, w_ple_gate[0].astype(BF16), w_ple[0].astype(BF16),
                 g_final[None, :])
    return out.reshape(B, S, D)
```

```python
import functools

import numpy as np
import jax
import jax.numpy as jnp
from jax import lax
from jax.experimental import pallas as pl
from jax.experimental.pallas import tpu as pltpu

F32 = jnp.float32
BF16 = jnp.bfloat16

HEAD_DIM = 64
LANE = 128
RMS_EPS = 1e-6
DIL_HEADS = 8
DIL_PATTERNS = ((128, 1), (512, 4), (2048, 16))
DIL_BLOCK = 128
DIL_STEP_ROWS = 4096
NSA_HEADS = 8
NSA_KV_HEADS = 2
NSA_REP = NSA_HEADS // NSA_KV_HEADS
CMP_BLOCK = 32
CMP_STRIDE = 16
CMP_HIDDEN = 256
SLC_BLOCK = 64
SLC_TOP_N = 16
SLC_FORCE = 1e9
WIN_SIZE = 512
N_EXPERTS = 64
TOP_K = 8
N_GROUPS = 8
TOP_GROUPS = 4
GROUP_SIZE = N_EXPERTS // N_GROUPS
ROUTED_SCALE = 2.5
IN_SIZES = (512, 512, 512, 512, 128, 128, 128, 128, 128, 128, 24, 1024, 1024)

NEG = -1e30
NT_DIMS = (((1,), (1,)), ((), ()))

ROW_TILE = 512
INPROJ_ROWS = 1024
NSA_Q = 256
NSA_TK = 512
MOE_ROWS = 512
VMEM_LIMIT = 48 * 1024 * 1024


def _dot(a, b):
    return jnp.dot(a, b, preferred_element_type=F32)


def _dot_nt(a, b):
    return lax.dot_general(a, b, NT_DIMS, preferred_element_type=F32)


def _split3(x):
    hi = x.astype(BF16)
    r1 = x - hi.astype(F32)
    mid = r1.astype(BF16)
    lo = (r1 - mid.astype(F32)).astype(BF16)
    return hi, mid, lo


def _rms(x, g):
    r = lax.rsqrt(jnp.mean(x * x, axis=-1, keepdims=True) + RMS_EPS)
    return (x * r) * g


def _params(sem):
    return pltpu.CompilerParams(dimension_semantics=sem, vmem_limit_bytes=VMEM_LIMIT)


def _inproj_kernel(x_ref, g_ref, w_ref, o_ref, *rest, dils):
    dil_refs, xn_ref = rest[:len(dils)], rest[len(dils)]
    res_ref = rest[len(dils) + 1] if dils else None

    @pl.when(pl.program_id(1) == 0)
    def _():
        xn_ref[...] = _rms(x_ref[...], g_ref[...]).astype(BF16)

    res = _dot(xn_ref[...], w_ref[...])
    o_ref[...] = res.astype(o_ref.dtype)

    if dils:
        tn = res.shape[1]
        for c in range(tn // LANE):
            res_ref[c] = res[:, c * LANE:(c + 1) * LANE]
        for d_ref, dil in zip(dil_refs, dils):
            for r in range(dil):
                for c in range(tn // LANE):
                    piece = res_ref[c, pl.ds(r, INPROJ_ROWS // dil, stride=dil), :]
                    d_ref[:, r * tn + c * LANE:r * tn + (c + 1) * LANE] = piece.astype(d_ref.dtype)


def _inproj(x2, g, w, out_dtype, tn, dils=()):
    n, d = x2.shape
    cols = w.shape[1]
    out_specs = [pl.BlockSpec((INPROJ_ROWS, tn), lambda i, j: (i, j))]
    out_shape = [jax.ShapeDtypeStruct((n, cols), out_dtype)]
    for dil in dils:
        out_specs.append(pl.BlockSpec((INPROJ_ROWS // dil, dil * tn), lambda i, j: (i, j)))
        out_shape.append(jax.ShapeDtypeStruct((n // dil, cols * dil), out_dtype))
    return pl.pallas_call(
        functools.partial(_inproj_kernel, dils=dils),
        grid=(n // INPROJ_ROWS, cols // tn),
        in_specs=[
            pl.BlockSpec((INPROJ_ROWS, d), lambda i, j: (i, 0)),
            pl.BlockSpec((1, d), lambda i, j: (0, 0)),
            pl.BlockSpec((d, tn), lambda i, j: (0, j)),
        ],
        out_specs=out_specs,
        out_shape=out_shape,
        scratch_shapes=[pltpu.VMEM((INPROJ_ROWS, d), BF16)]
        + ([pltpu.VMEM((tn // LANE, INPROJ_ROWS, LANE), F32)] if dils else []),
        compiler_params=_params(("parallel", "arbitrary")),
    )(x2, g, w)


def _dil_kernel(q_ref, kp_ref, kc_ref, vp_ref, vc_ref, bias_ref, o_ref, l_ref, *, dil, nq):
    i = pl.program_id(1)
    res = pl.program_id(2)
    blk = DIL_BLOCK
    lane = lax.broadcasted_iota(jnp.int32, (blk, LANE), 1)
    lo = lane < HEAD_DIM
    for sub in range(nq):
        rows = slice(sub * blk, (sub + 1) * blk)
        prev = slice((sub - 1) * blk, sub * blk)
        out_rows = pl.ds(sub * blk * dil + res, blk, stride=dil) if dil > 1 else rows
        variant = jnp.minimum(i, 1) * DIL_HEADS if sub == 0 else DIL_HEADS
        for hp in range(DIL_HEADS // 2):
            sl = slice(hp * LANE, (hp + 1) * LANE)
            q2 = q_ref[0, rows, sl]
            k_prev = kp_ref[0, :, sl] if sub == 0 else kc_ref[0, prev, sl]
            v_prev = vp_ref[0, :, sl] if sub == 0 else vc_ref[0, prev, sl]
            k2 = jnp.concatenate([k_prev, kc_ref[0, rows, sl]], axis=0)
            v2 = jnp.concatenate([v_prev, vc_ref[0, rows, sl]], axis=0)
            outs, lses = [], []
            for hh in range(2):
                keep = lo if hh == 0 else jnp.logical_not(lo)
                qm = jnp.where(keep, q2, jnp.zeros_like(q2))
                s = _dot_nt(qm, k2) + bias_ref[variant + 2 * hp + hh]
                m = jnp.max(s, axis=-1, keepdims=True)
                e = jnp.exp(s - m)
                den = jnp.sum(e, axis=-1, keepdims=True)
                outs.append(_dot(e.astype(BF16), v2) / den)
                lses.append(jnp.broadcast_to(m + jnp.log(den), (blk, LANE)))
            o_ref[hp, 0, out_rows, :] = jnp.where(lo, outs[0], outs[1])
            l_ref[hp, 0, out_rows, :] = jnp.where(lo, lses[0], lses[1])


def _dil_bias(slopes_a, dil):
    blk = DIL_BLOCK
    qi = jnp.arange(blk)[:, None]
    kj = jnp.arange(2 * blk)[None, :]
    dist = (qi + blk - kj).astype(F32)
    band = (dist >= 0) & (dist <= blk)
    slope = (slopes_a * dil)[:, None, None]
    bias = -slope * dist
    first = jnp.where(band & (kj >= blk), bias, NEG)
    rest = jnp.where(band, bias, NEG)
    return jnp.concatenate([first, rest], axis=0)


def _dilated(src3, bias, dil, b, s):
    nq = min(8, DIL_STEP_ROWS // (DIL_BLOCK * dil))
    nstep = s // dil // (DIL_BLOCK * nq)
    aw = DIL_HEADS * HEAD_DIM
    npair = DIL_HEADS // 2

    def spec(which, prev):
        if prev:
            return pl.BlockSpec((1, DIL_BLOCK, aw), lambda bb, i, r: (bb, jnp.maximum(nq * i - 1, 0), which * dil + r))
        return pl.BlockSpec((1, DIL_BLOCK * nq, aw), lambda bb, i, r: (bb, i, which * dil + r))

    out_spec = pl.BlockSpec((npair, 1, DIL_BLOCK * nq * dil, LANE), lambda bb, i, r: (0, bb, i, 0))
    return pl.pallas_call(
        functools.partial(_dil_kernel, dil=dil, nq=nq),
        grid=(b, nstep, dil),
        in_specs=[
            spec(0, False), spec(1, True), spec(1, False), spec(2, True), spec(2, False),
            pl.BlockSpec(bias.shape, lambda bb, i, r: (0, 0, 0)),
        ],
        out_specs=[out_spec, out_spec],
        out_shape=[jax.ShapeDtypeStruct((npair, b, s, LANE), F32)] * 2,
        compiler_params=_params(("parallel", "arbitrary", "arbitrary")),
    )(src3, src3, src3, src3, src3, bias)


def _cmp_kernel(x_ref, pos_ref, w1_ref, b1_ref, w2_ref, o_ref):
    half = x_ref.shape[-1]
    x = x_ref[0, 0]
    w1 = w1_ref[0]
    first = _dot(x, w1[:half])
    second = _dot(x, w1[half:])
    nrow = x.shape[0]
    pre = first + pltpu.roll(second, nrow - 1, 0)
    posb = _dot(jnp.broadcast_to(pos_ref[0], (8, 2 * half)).astype(BF16), w1)[0:1]
    pre = pre + posb + b1_ref[0]
    o_ref[0, 0] = _dot(jax.nn.gelu(pre).astype(BF16), w2_ref[0]).astype(o_ref.dtype)


def _compress(xc, pos, w1, b1, w2):
    b, four, nchunk, width = xc.shape
    return pl.pallas_call(
        _cmp_kernel,
        grid=(b, four),
        in_specs=[
            pl.BlockSpec((1, 1, nchunk, width), lambda bb, j: (bb, j, 0, 0)),
            pl.BlockSpec((1, 1, 2 * width), lambda bb, j: (j // 2, 0, 0)),
            pl.BlockSpec((1, 2 * width, CMP_HIDDEN), lambda bb, j: (j // 2, 0, 0)),
            pl.BlockSpec((1, 1, CMP_HIDDEN), lambda bb, j: (j // 2, 0, 0)),
            pl.BlockSpec((1, CMP_HIDDEN, LANE), lambda bb, j: (j // 2, 0, 0)),
        ],
        out_specs=pl.BlockSpec((1, 1, nchunk, LANE), lambda bb, j: (bb, j, 0, 0)),
        out_shape=jax.ShapeDtypeStruct((b, four, nchunk, LANE), BF16),
        compiler_params=_params(("parallel", "parallel")),
    )(xc, pos, w1, b1, w2)


POS_LANE = HEAD_DIM
POS_SPLIT = 64
SUM_LANE = HEAD_DIM
MASKED = -1e30
ROW_FLOOR = -1e29


def _pos_columns(pos):
    out = np.zeros((pos.shape[0], LANE), np.float32)
    out[:, POS_LANE] = pos // POS_SPLIT
    out[:, POS_LANE + 1] = pos % POS_SPLIT
    out[:, POS_LANE + 2] = 1.0
    out[:, POS_LANE + 3] = 1.0
    return jnp.asarray(out, BF16)


def _nsa_kernel(slope_ref, q_ref, kc_ref, vct_ref, ks_ref, kw_ref, vt_ref, gt_ref, ov_ref, kpa_ref, cpa_ref,
                oh_ref, o_ref, kaug_ref, flag_ref, m_ref, acc_ref, *, seq):
    g = pl.program_id(1)
    c = pl.program_id(2)
    C = NSA_Q
    R = NSA_REP
    TK = NSA_TK
    rows = R * C
    t0 = c * C
    scale = HEAD_DIM ** -0.5
    nb = seq // SLC_BLOCK
    ncmp = seq // CMP_STRIDE
    n_sel = min(SLC_TOP_N, nb)
    vblocks = TK // LANE

    @pl.when(c == 0)
    def _():
        kaug_ref[:, 0:LANE] = ks_ref[0] + kpa_ref[...]
        kaug_ref[:, LANE:2 * LANE] = oh_ref[...]

    row = lax.broadcasted_iota(jnp.int32, (rows, 1), 0)
    ti = t0 + row % C
    slope = jnp.zeros((rows, 1), F32)
    for r in range(R):
        slope = jnp.where(row // C == r, slope_ref[g * R + r], slope)
    lane = lax.broadcasted_iota(jnp.int32, (rows, LANE), 1)
    thi = (ti // POS_SPLIT).astype(F32)
    tlo = (ti % POS_SPLIT).astype(F32)
    qcols = jnp.where(lane == POS_LANE, POS_SPLIT * slope,
                      jnp.where(lane == POS_LANE + 1, slope,
                                jnp.where(lane == POS_LANE + 2, -POS_SPLIT * slope * thi,
                                          jnp.where(lane == POS_LANE + 3, -slope * tlo, 0.0))))
    q4 = jnp.concatenate([q_ref[0, :, r * LANE:(r + 1) * LANE] for r in range(R)], axis=0)
    q4 = (q4.astype(F32) * scale + qcols).astype(BF16)
    tq = t0 + lax.broadcasted_iota(jnp.int32, (1, C), 1)

    def heads(x):
        return jnp.concatenate([x] * R, axis=1)

    kc = kc_ref[0] + cpa_ref[...]
    ci = lax.broadcasted_iota(jnp.int32, (ncmp, 1), 0)
    vis = (ci * CMP_STRIDE + (CMP_BLOCK - 1) <= tq) & (ci < ncmp - 1)
    s = _dot_nt(kc, q4) + heads(jnp.where(vis, 0.0, MASKED))
    m = jnp.maximum(jnp.max(s, axis=0, keepdims=True), ROW_FLOOR)
    e = jnp.exp(s - m)
    den = jnp.sum(e, axis=0, keepdims=True)
    pc = e * (1.0 / jnp.where(den > 0, den, 1.0))
    o_cmp = _dot(vct_ref[0], pc.astype(BF16))
    pcsum = pc[:, 0:C]
    for r in range(1, R):
        pcsum = pcsum + pc[:, r * C:(r + 1) * C]
    ov = ov_ref[...]
    p_slc = sum(_dot(ov, part) for part in _split3(pcsum))

    def values_t(which, first_block, n):
        return jnp.concatenate([vt_ref[0, which, first_block + i] for i in range(n)], axis=1)

    span = WIN_SIZE + C
    ws = pl.multiple_of(jnp.maximum(t0 - WIN_SIZE, 0), C)
    k_w = kw_ref[0, pl.ds(ws, span), :] + kpa_ref[pl.ds(ws, span), :]
    wpos = ws + lax.broadcasted_iota(jnp.int32, (span, 1), 0)
    wbias = jnp.where((wpos <= tq) & (wpos >= tq - (WIN_SIZE - 1)), 0.0, MASKED)
    s3 = _dot_nt(k_w, q4) + heads(wbias)
    e3 = jnp.exp(s3 - jnp.max(s3, axis=0, keepdims=True))
    ow = _dot(values_t(1, ws // LANE, span // LANE), e3.astype(BF16))
    o_win = ow / ow[SUM_LANE:SUM_LANE + 1, :]

    j = lax.broadcasted_iota(jnp.int32, (LANE, C), 0)
    cur = tq // SLC_BLOCK
    forced = (j == 0) | (j == cur) | (j == cur - 1)
    future = j * SLC_BLOCK > tq
    work = jnp.where(forced, -jnp.inf, jnp.where(future, -SLC_FORCE, p_slc))
    work = jnp.where(j < nb, work, -jnp.inf)
    sel = jnp.where(forced, 1.0, 0.0)
    jf = j.astype(F32)
    for _ in range(n_sel - 3):
        mx = jnp.max(work, axis=0, keepdims=True)
        first = jnp.min(jnp.where(work == mx, jf, float(LANE)), axis=0, keepdims=True)
        pick = jf == first
        sel = jnp.where(pick, 1.0, sel)
        work = jnp.where(pick, -jnp.inf, work)
    sel = jnp.where(future, 0.0, sel)
    blocks_per_tile = TK // SLC_BLOCK
    for kt in range(seq // TK):
        flag_ref[kt] = jnp.max(sel[kt * blocks_per_tile:(kt + 1) * blocks_per_tile, :])
    sel_bias = jnp.where(sel.T > 0.5, 0.0, MASKED).astype(BF16)
    q_aug = jnp.concatenate([q4, jnp.concatenate([sel_bias] * R, axis=0)], axis=1)

    m_ref[...] = jnp.full((1, rows), ROW_FLOOR, F32)
    acc_ref[...] = jnp.zeros((LANE, rows), F32)
    kio = lax.broadcasted_iota(jnp.int32, (TK, 1), 0)

    def tile_update(kt, causal):
        start = pl.multiple_of(kt * TK, TK)
        s2 = _dot_nt(kaug_ref[pl.ds(start, TK), :], q_aug)
        if causal:
            s2 = s2 + heads(jnp.where(start + kio <= tq, 0.0, MASKED))
        m_i = m_ref[...]
        m_n = jnp.maximum(m_i, jnp.max(s2, axis=0, keepdims=True))
        alpha = jnp.exp(m_i - m_n)
        p2 = jnp.exp(s2 - m_n)
        acc_ref[...] = alpha * acc_ref[...] + _dot(values_t(0, kt * vblocks, vblocks), p2.astype(BF16))
        m_ref[...] = m_n

    def slc_body(kt, carry):
        @pl.when(flag_ref[kt] > 0.0)
        def _():
            tile_update(kt, False)

        return carry

    last = (t0 + C - 1) // TK
    lax.fori_loop(0, last, slc_body, 0)
    tile_update(last, True)
    acc = acc_ref[...]
    o_slc = acc / acc[SUM_LANE:SUM_LANE + 1, :]

    gates = jax.nn.sigmoid(gt_ref[0]).T
    for r in range(R):
        rs = slice(r * C, (r + 1) * C)
        out = (gates[r:r + 1] * o_cmp[:, rs] + gates[R + r:R + r + 1] * o_slc[:, rs]
               + gates[2 * R + r:2 * R + r + 1] * o_win[:, rs])
        o_ref[0, :, r * LANE:(r + 1) * LANE] = out.T.astype(o_ref.dtype)


def _nsa(slopes_b, pa3, gt3, kc, vct, vt, b, s, col0, gcol0):
    C = NSA_Q
    R = NSA_REP
    G = NSA_KV_HEADS
    nb = s // SLC_BLOCK
    ncmp = s // CMP_STRIDE
    assert nb <= LANE
    qc0 = col0 // R
    kv0 = col0 + NSA_HEADS
    rows = R * C

    ci = np.arange(ncmp)[None, :]
    sj = np.arange(LANE)[:, None]
    ov = ((ci * CMP_STRIDE < (sj + 1) * SLC_BLOCK) & (ci * CMP_STRIDE + CMP_BLOCK > sj * SLC_BLOCK)
          & (ci < ncmp - 1) & (sj < nb))
    ov = jnp.asarray(ov, BF16)
    kpa = _pos_columns(np.arange(s))
    cpa = _pos_columns(np.arange(ncmp) * CMP_STRIDE + (CMP_BLOCK - 1))
    oh = jnp.asarray(np.arange(s)[:, None] // SLC_BLOCK == np.arange(LANE)[None, :], BF16)

    def k_spec(which):
        return pl.BlockSpec((1, s, LANE), lambda bb, g, c: (bb, 0, kv0 + which * G + g))

    const = lambda a: pl.BlockSpec(a.shape, lambda bb, g, c: (0,) * a.ndim)
    return pl.pallas_call(
        functools.partial(_nsa_kernel, seq=s),
        grid=(b, G, s // C),
        in_specs=[
            pl.BlockSpec(memory_space=pltpu.SMEM),
            pl.BlockSpec((1, C, R * LANE), lambda bb, g, c: (bb, c, qc0 + g)),
            pl.BlockSpec((1, ncmp, LANE), lambda bb, g, c: (bb * G + g, 0, 0)),
            pl.BlockSpec((1, LANE, ncmp), lambda bb, g, c: (bb * G + g, 0, 0)),
            k_spec(2), k_spec(4),
            pl.BlockSpec((1, 2, s // LANE, LANE, LANE), lambda bb, g, c: (bb * G + g, 0, 0, 0, 0)),
            pl.BlockSpec((1, C, LANE), lambda bb, g, c: (bb, c, gcol0 + g)),
            const(ov), const(kpa), const(cpa), const(oh),
        ],
        out_specs=pl.BlockSpec((1, C, R * LANE), lambda bb, g, c: (bb, c, g)),
        out_shape=jax.ShapeDtypeStruct((b, s, NSA_HEADS * LANE), BF16),
        scratch_shapes=[
            pltpu.VMEM((s, 2 * LANE), BF16),
            pltpu.SMEM((s // NSA_TK,), F32),
            pltpu.VMEM((1, rows), F32),
            pltpu.VMEM((LANE, rows), F32),
        ],
        compiler_params=_params(("parallel", "parallel", "arbitrary")),
    )(slopes_b, pa3, kc, vct, pa3, pa3, vt, gt3, ov, kpa, cpa, oh)


def _merge_kernel(o1, o2, o3, l1, l2, l3, yb, ga, gb, x_ref, wa, wb, wo, gf, wrh, wrl,
                  h_ref, xn_ref, lg_ref, xr_ref):
    def pairs(ref):
        return jnp.concatenate([ref[h] for h in range(ref.shape[0])], axis=1)

    a1, a2, a3 = pairs(l1), pairs(l2), pairs(l3)
    mx = jnp.maximum(jnp.maximum(a1, a2), a3)
    e1, e2, e3 = jnp.exp(a1 - mx), jnp.exp(a2 - mx), jnp.exp(a3 - mx)
    tot = e1 + e2 + e3
    ya = (e1 / tot) * pairs(o1) + (e2 / tot) * pairs(o2) + (e3 / tot) * pairs(o3)
    merged = (jax.nn.sigmoid(ga[...]) * _dot(ya.astype(BF16), wa[...])
              + jax.nn.sigmoid(gb[...]) * _dot(yb[...], wb[...]))
    h = x_ref[...] + _dot(merged.astype(BF16), wo[...])
    h_ref[...] = h
    xn = _rms(h, gf[...])
    xn_ref[...] = xn.astype(BF16)
    tm, d = xn.shape
    half = d // (2 * LANE)

    def bf16_bits(v):
        return pltpu.bitcast(v.astype(BF16).astype(F32), jnp.uint32)

    for c in range(half):
        lo = bf16_bits(xn[:, c * LANE:(c + 1) * LANE])
        hi = bf16_bits(xn[:, (c + half) * LANE:(c + half + 1) * LANE])
        xr_ref[pl.ds(c, tm, stride=half), :] = hi | (lo >> 16)
    xh = xn.astype(BF16)
    xl = (xn - xh.astype(F32)).astype(BF16)
    lg_ref[...] = _dot_nt(wrh[...], xh) + _dot_nt(wrh[...], xl) + _dot_nt(wrl[...], xh)


def _merge(o_list, l_list, yb, gt, x2, wa, wb, wo, gf, wrh, wrl):
    n, d = x2.shape
    aw = wa.shape[0]
    bw = wb.shape[0]
    tm = ROW_TILE
    row = lambda w: pl.BlockSpec((tm, w), lambda i: (i, 0))
    full = lambda a: pl.BlockSpec(a.shape, lambda i: (0,) * a.ndim)
    return pl.pallas_call(
        _merge_kernel,
        grid=(n // tm,),
        in_specs=[pl.BlockSpec((aw // LANE, tm, LANE), lambda i: (0, i, 0))] * 6 + [
            row(bw),
            pl.BlockSpec((tm, d), lambda i: (i, 0)),
            pl.BlockSpec((tm, d), lambda i: (i, 1)),
            row(d), full(wa), full(wb), full(wo), full(gf), full(wrh), full(wrl),
        ],
        out_specs=[row(d), row(d), pl.BlockSpec((N_EXPERTS, tm), lambda i: (0, i)),
                   pl.BlockSpec((tm * d // (2 * LANE), LANE), lambda i: (i, 0))],
        out_shape=[
            jax.ShapeDtypeStruct((n, d), F32),
            jax.ShapeDtypeStruct((n, d), BF16),
            jax.ShapeDtypeStruct((N_EXPERTS, n), F32),
            jax.ShapeDtypeStruct((n * d // (2 * LANE), LANE), jnp.uint32),
        ],
        compiler_params=_params(("parallel",)),
    )(*o_list, *l_list, yb, gt, gt, x2, wa, wb, wo, gf, wrh, wrl)


def _route_kernel(lg_ref, bias_ref, tri_ref, idx_ref, w_ref, rank_ref, cnt_ref, carry_ref):
    E = N_EXPERTS
    tn = lg_ref.shape[1]
    scores = jax.nn.sigmoid(lg_ref[...])
    sel = scores + bias_ref[...]
    sub = lax.broadcasted_iota(jnp.int32, (GROUP_SIZE, tn), 0).astype(F32)
    gs = []
    for grp in range(N_GROUPS):
        blk = sel[grp * GROUP_SIZE:(grp + 1) * GROUP_SIZE]
        m1 = jnp.max(blk, axis=0, keepdims=True)
        first = jnp.min(jnp.where(blk == m1, sub, float(GROUP_SIZE)), axis=0, keepdims=True)
        m2 = jnp.max(jnp.where(sub == first, -jnp.inf, blk), axis=0, keepdims=True)
        gs.append(m1 + m2)
    keep = []
    for a in range(N_GROUPS):
        rank = jnp.zeros((1, tn), jnp.int32)
        for o in range(N_GROUPS):
            if o == a:
                continue
            beats = (gs[o] >= gs[a]) if o < a else (gs[o] > gs[a])
            rank = rank + beats.astype(jnp.int32)
        keep.append(jnp.broadcast_to(rank < TOP_GROUPS, (GROUP_SIZE, tn)))
    emask = jnp.concatenate(keep, axis=0)
    cand = jnp.where(emask, sel, -jnp.inf)
    eid = lax.broadcasted_iota(jnp.int32, (E, tn), 0)
    eid_f = eid.astype(F32)
    rank = jnp.zeros((E, tn), jnp.int32)
    for o in range(E):
        other = cand[o:o + 1]
        beats = (other > cand) | ((other == cand) & (eid > o))
        rank = rank + beats.astype(jnp.int32)
    chosen = emask & (rank < TOP_K)
    wsel = jnp.where(chosen, scores, 0.0)
    wnorm = wsel / jnp.sum(wsel, axis=0, keepdims=True) * ROUTED_SCALE

    @pl.when(pl.program_id(0) == 0)
    def _():
        carry_ref[...] = jnp.zeros_like(carry_ref)

    chosen_f = jnp.where(chosen, 1.0, 0.0)
    before = carry_ref[...] + _dot(chosen_f.astype(BF16), tri_ref[...]) - chosen_f
    carry_ref[...] = carry_ref[...] + jnp.sum(chosen_f, axis=1, keepdims=True)
    cnt_ref[...] = jnp.broadcast_to(carry_ref[...], cnt_ref.shape).astype(jnp.int32)
    for k in range(TOP_K):
        hit = chosen & (rank == k)
        idx_ref[k:k + 1, :] = jnp.sum(jnp.where(hit, eid_f, 0.0), axis=0, keepdims=True).astype(jnp.int32)
        w_ref[k:k + 1, :] = jnp.sum(jnp.where(hit, wnorm, 0.0), axis=0, keepdims=True)
        rank_ref[k:k + 1, :] = jnp.sum(jnp.where(hit, before, 0.0), axis=0, keepdims=True).astype(jnp.int32)


def _route(logits_t, bias_col):
    e, n = logits_t.shape
    tn = ROW_TILE
    tri = jnp.asarray(np.triu(np.ones((tn, tn), np.float32)), BF16)
    slot = pl.BlockSpec((TOP_K, tn), lambda i: (0, i))
    return pl.pallas_call(
        _route_kernel,
        grid=(n // tn,),
        in_specs=[pl.BlockSpec((e, tn), lambda i: (0, i)), pl.BlockSpec((e, 1), lambda i: (0, 0)),
                  pl.BlockSpec((tn, tn), lambda i: (0, 0))],
        out_specs=[slot, slot, slot, pl.BlockSpec((e, LANE), lambda i: (0, 0))],
        out_shape=[
            jax.ShapeDtypeStruct((TOP_K, n), jnp.int32),
            jax.ShapeDtypeStruct((TOP_K, n), F32),
            jax.ShapeDtypeStruct((TOP_K, n), jnp.int32),
            jax.ShapeDtypeStruct((e, LANE), jnp.int32),
        ],
        scratch_shapes=[pltpu.VMEM((e, 1), F32)],
        compiler_params=_params(("arbitrary",)),
    )(logits_t, bias_col, tri)


def _dispatch_kernel(fill_lo_ref, fill_hi_ref, nused_ref, pos_ref, x_ref, xs_ref, zero_ref, sem, zsem):
    i = pl.program_id(0)
    tm = pos_ref.shape[1]
    sub = x_ref.shape[0] // tm
    blk_rows = zero_ref.shape[0]
    nblk = xs_ref.shape[0] // blk_rows

    def slot(r):
        return xs_ref.at[pl.ds(pl.multiple_of(r * sub, sub), sub), :]

    @pl.when(i == 0)
    def _():
        zero_ref[...] = jnp.zeros_like(zero_ref)

        def per_expert(e, carry):
            def fill(r, c2):
                pltpu.make_async_copy(zero_ref.at[pl.ds(0, sub), :], slot(r), zsem).start()
                return c2

            def drain(r, c2):
                pltpu.make_async_copy(zero_ref.at[pl.ds(0, sub), :], slot(r), zsem).wait()
                return c2

            lax.fori_loop(fill_lo_ref[e], fill_hi_ref[e], fill, 0)
            lax.fori_loop(fill_lo_ref[e], fill_hi_ref[e], drain, 0)
            return carry

        lax.fori_loop(0, fill_lo_ref.shape[0], per_expert, 0)

        def tail(bk, carry):
            cp = pltpu.make_async_copy(
                zero_ref, xs_ref.at[pl.ds(pl.multiple_of(bk * blk_rows, blk_rows), blk_rows), :], zsem)
            cp.start()
            cp.wait()
            return carry

        lax.fori_loop(nused_ref[0], nblk, tail, 0)

    def scatter(t, carry):
        src = x_ref.at[pl.ds(pl.multiple_of(t * sub, sub), sub), :]
        for k in range(TOP_K):
            pltpu.make_async_copy(src, slot(pos_ref[k, t]), sem).start()
        return carry

    lax.fori_loop(0, tm, scatter, 0)
    for k in range(TOP_K):
        pltpu.make_async_copy(x_ref, xs_ref.at[pl.ds(0, tm * sub), :], sem).wait()


def _dispatch(fill_lo, fill_hi, nused, pos_t, xr, p_rows, d):
    k, n = pos_t.shape
    sub = d // (2 * LANE)
    tm = ROW_TILE
    grid_spec = pltpu.PrefetchScalarGridSpec(
        num_scalar_prefetch=3,
        grid=(n // tm,),
        in_specs=[
            pl.BlockSpec((k, tm), lambda i, *_: (0, i), memory_space=pltpu.SMEM),
            pl.BlockSpec((tm * sub, LANE), lambda i, *_: (i, 0)),
        ],
        out_specs=pl.BlockSpec(memory_space=pl.ANY),
        scratch_shapes=[pltpu.VMEM((MOE_ROWS * sub, LANE), xr.dtype), pltpu.SemaphoreType.DMA(()),
                        pltpu.SemaphoreType.DMA(())],
    )
    return pl.pallas_call(
        _dispatch_kernel,
        grid_spec=grid_spec,
        out_shape=jax.ShapeDtypeStruct((p_rows * sub, LANE), xr.dtype),
        compiler_params=_params(("arbitrary",)),
    )(fill_lo, fill_hi, nused, pos_t, xr)


def _moe_kernel(blk_e_ref, nused_ref, xs_ref, wg_ref, wu_ref, wd_ref, o_ref):
    i = pl.program_id(0)

    @pl.when(i < nused_ref[0])
    def _():
        sub = xs_ref.shape[0] // MOE_ROWS
        words = [xs_ref[pl.ds(c, MOE_ROWS, stride=sub), :] for c in range(sub)]
        lo = [pltpu.bitcast(w << 16, F32).astype(BF16) for w in words]
        hi = [pltpu.bitcast(w & jnp.uint32(0xFFFF0000), F32).astype(BF16) for w in words]
        x = jnp.concatenate(lo + hi, axis=1)
        hg = _dot(x, wg_ref[0])
        hid = (hg * jax.nn.sigmoid(hg)) * _dot(x, wu_ref[0])
        o_ref[...] = _dot(hid.astype(BF16), wd_ref[0]).astype(o_ref.dtype)

    @pl.when(i >= nused_ref[0])
    def _():
        o_ref[...] = jnp.zeros_like(o_ref)


def _moe_experts(blk_e, nused, xs, wg, wu, wd):
    d = wg.shape[1]
    sub = d // (2 * LANE)
    p = xs.shape[0] // sub
    hid = wg.shape[2]
    grid_spec = pltpu.PrefetchScalarGridSpec(
        num_scalar_prefetch=2,
        grid=(p // MOE_ROWS,),
        in_specs=[
            pl.BlockSpec((MOE_ROWS * sub, LANE), lambda i, be, nu: (jnp.minimum(i, nu[0] - 1), 0)),
            pl.BlockSpec((1, d, hid), lambda i, be, nu: (be[i], 0, 0)),
            pl.BlockSpec((1, d, hid), lambda i, be, nu: (be[i], 0, 0)),
            pl.BlockSpec((1, hid, d), lambda i, be, nu: (be[i], 0, 0)),
        ],
        out_specs=pl.BlockSpec((MOE_ROWS, d), lambda i, be, nu: (i, 0)),
    )
    return pl.pallas_call(
        _moe_kernel,
        grid_spec=grid_spec,
        out_shape=jax.ShapeDtypeStruct((p, d), BF16),
        compiler_params=_params(("arbitrary",)),
    )(blk_e, nused, xs, wg, wu, wd)


def _final_kernel(h_ref, xn_ref, ys_ref, w_ref, p_ref, wsg, wsu, wsd, gp, wpg, wp, gfin, o_ref):
    xn = xn_ref[...]
    hg = _dot(xn, wsg[...])
    shared = _dot(((hg * jax.nn.sigmoid(hg)) * _dot(xn, wsu[...])).astype(BF16), wsd[...])
    w = w_ref[...]
    y = w[:, 0:1] * ys_ref[0].astype(F32)
    for k in range(1, TOP_K):
        y = y + w[:, k:k + 1] * ys_ref[k].astype(F32)
    h = h_ref[...] + (y + shared)
    gate = jax.nn.sigmoid(_dot(_rms(h, gp[...]).astype(BF16), wpg[...]))
    h = h + _dot(p_ref[...].astype(BF16), wp[...]) * gate
    o_ref[...] = _rms(h, gfin[...])


def _final(h1, xn2, ysg, w_nk, p2, wsg, wsu, wsd, gp, wpg, wp, gfin):
    n, d = h1.shape
    tm = ROW_TILE
    row = lambda w: pl.BlockSpec((tm, w), lambda i: (i, 0))
    full = lambda a: pl.BlockSpec(a.shape, lambda i: (0,) * a.ndim)
    return pl.pallas_call(
        _final_kernel,
        grid=(n // tm,),
        in_specs=[row(d), row(d), pl.BlockSpec((TOP_K, tm, d), lambda i: (0, i, 0)), row(TOP_K), row(p2.shape[1]),
                  full(wsg), full(wsu), full(wsd), full(gp), full(wpg), full(wp), full(gfin)],
        out_specs=row(d),
        out_shape=jax.ShapeDtypeStruct((n, d), F32),
        compiler_params=_params(("parallel",)),
    )(h1, xn2, ysg, w_nk, p2, wsg, wsu, wsd, gp, wpg, wp, gfin)


def _pad_heads(w):
    d, c = w.shape
    nh = c // HEAD_DIM
    w = w.reshape(d, nh, HEAD_DIM)
    return jnp.pad(w, ((0, 0), (0, 0), (0, LANE - HEAD_DIM))).reshape(d, nh * LANE)


def _alibi_slopes(n):
    return 2.0 ** (-8.0 * jnp.arange(1, n + 1, dtype=jnp.float32) / n)


def kernel(x, p, g_mix, w_in, cmp_pos_k, cmp_w1_k, cmp_b1_k, cmp_w2_k, cmp_pos_v, cmp_w1_v, cmp_b1_v, cmp_w2_v, w_branch_a, w_branch_b, w_out, g_ffn, w_router, router_bias, w_gate, w_up, w_down, ws_gate, ws_up, ws_down, g_ple, w_ple_gate, w_ple, g_final):
    B, S, D = x.shape
    N = B * S
    G, R = NSA_KV_HEADS, NSA_REP
    assert w_in.shape[0] == 1 and p.shape[0] == 1
    assert S % DIL_STEP_ROWS == 0 and S % NSA_TK == 0 and S >= WIN_SIZE + NSA_Q
    assert N % ROW_TILE == 0 and N % INPROJ_ROWS == 0 and S % INPROJ_ROWS == 0
    slopes = _alibi_slopes(DIL_HEADS + NSA_HEADS)
    slopes_a, slopes_b = slopes[0::2], slopes[1::2]

    offs = np.concatenate([[0], np.cumsum(IN_SIZES)])
    parts = [w_in[0][:, offs[i]:offs[i + 1]] for i in range(len(IN_SIZES))]
    w_qa, w_ka, w_va, w_qb, w_kc, w_vc, w_ks, w_vs, w_kw, w_vw, w_gn, w_ga, w_gb = parts
    w_qkv_a = jnp.concatenate([w_qa * HEAD_DIM ** -0.5, w_ka, w_va], axis=1).astype(BF16)
    w_nsa = jnp.concatenate([_pad_heads(t) for t in (w_qb, w_kc, w_vc, w_ks, w_vs, w_kw, w_vw)],
                            axis=1).astype(BF16)
    nsa_w = w_nsa.shape[1]
    w_gn = w_gn.reshape(D, 3, G, R).transpose(0, 2, 1, 3).reshape(D, G, 3 * R)
    w_gn = jnp.pad(w_gn, ((0, 0), (0, 0), (0, LANE - 3 * R))).reshape(D, G * LANE)
    w_gate_in = jnp.concatenate([w_ga, w_gb, w_gn], axis=1).astype(BF16)
    gate_w = w_gate_in.shape[1]

    x2 = x.reshape(N, D)
    dils = tuple(dil for _, dil in DIL_PATTERNS if dil > 1)
    qkv_a, *dilated_qkv = _inproj(x2, g_mix, w_qkv_a, BF16, 512, dils=dils)
    nsa_in, = _inproj(x2, g_mix, w_nsa, BF16, 512)
    gt, = _inproj(x2, g_mix, w_gate_in, F32, 768)

    o_list, l_list = [], []
    for _, dil in DIL_PATTERNS:
        src = qkv_a if dil == 1 else dilated_qkv[dils.index(dil)]
        o, l = _dilated(src.reshape(B, S // dil, -1), _dil_bias(slopes_a, dil), dil, B, S)
        o_list.append(o.reshape(-1, N, LANE))
        l_list.append(l.reshape(-1, N, LANE))

    nchunk = S // CMP_STRIDE
    c0 = NSA_HEADS * LANE
    raw = nsa_in[:, c0:c0 + 2 * G * LANE].reshape(B, nchunk, CMP_STRIDE, 2 * G, LANE)[..., :HEAD_DIM]
    xc = raw.transpose(0, 3, 1, 2, 4).reshape(B, 2 * G, nchunk, CMP_STRIDE * HEAD_DIM)
    pos = jnp.stack([cmp_pos_k[0].reshape(1, -1), cmp_pos_v[0].reshape(1, -1)])
    w1 = jnp.stack([cmp_w1_k[0], cmp_w1_v[0]]).astype(BF16)
    b1 = jnp.stack([cmp_b1_k[0][None], cmp_b1_v[0][None]])
    w2 = jnp.pad(jnp.stack([cmp_w2_k[0], cmp_w2_v[0]]), ((0, 0), (0, 0), (0, LANE - HEAD_DIM))).astype(BF16)
    kcv = _compress(xc, pos, w1, b1, w2)
    kcv = kcv.reshape(B, 2, G, nchunk, LANE).transpose(1, 0, 2, 3, 4).reshape(2, B * G, nchunk, LANE)
    vct = kcv[1].transpose(0, 2, 1)
    v0 = (NSA_HEADS + 3 * G) * LANE
    vsw = jnp.stack([nsa_in[:, v0:v0 + G * LANE], nsa_in[:, v0 + 2 * G * LANE:v0 + 3 * G * LANE]])
    vsw = vsw.reshape(2, B, S // LANE, LANE, G, LANE).transpose(1, 4, 0, 2, 5, 3)
    ones_row = (jnp.arange(LANE) == SUM_LANE).astype(BF16)[:, None]
    vt = (vsw + ones_row).reshape(B * G, 2, S // LANE, LANE, LANE)
    yb = _nsa(slopes_b, nsa_in.reshape(B, S, nsa_w), gt.reshape(B, S, gate_w), kcv[0], vct, vt,
              B, S, 0, 2 * D // LANE)
    yb = yb.reshape(N, NSA_HEADS * LANE)

    wb_pad = jnp.pad(w_branch_b[0].reshape(NSA_HEADS, HEAD_DIM, D),
                     ((0, 0), (0, LANE - HEAD_DIM), (0, 0))).reshape(NSA_HEADS * LANE, D)
    wr_t = w_router[0].T
    wr_hi = wr_t.astype(BF16)
    wr_lo = (wr_t - wr_hi.astype(F32)).astype(BF16)
    h1, xn2, logits_t, xr = _merge(o_list, l_list, yb, gt, x2, w_branch_a[0].astype(BF16), wb_pad.astype(BF16),
                               w_out[0].astype(BF16), g_ffn, wr_hi, wr_lo)

    idx_t, wsel_t, rank_t, counts = _route(logits_t, router_bias[0][:, None])
    counts = counts[:, 0]
    padded = (counts + MOE_ROWS - 1) // MOE_ROWS * MOE_ROWS
    pend = jnp.cumsum(padded)
    pstart = pend - padded
    hot = idx_t[None] == jnp.arange(N_EXPERTS, dtype=jnp.int32)[:, None, None]
    pos_t = rank_t + jnp.sum(jnp.where(hot, pstart.astype(jnp.int32)[:, None, None], 0), axis=0)
    NK = N * TOP_K
    nblk = (NK + N_EXPERTS * (MOE_ROWS - 1) + MOE_ROWS - 1) // MOE_ROWS
    P = nblk * MOE_ROWS
    blk_e = jnp.minimum(jnp.sum(pend[None, :] <= (jnp.arange(nblk, dtype=jnp.int32) * MOE_ROWS)[:, None], axis=1),
                        N_EXPERTS - 1).astype(jnp.int32)
    nused = (pend[-1] // MOE_ROWS).astype(jnp.int32).reshape(1)
    xs = _dispatch((pstart + counts).astype(jnp.int32), pend.astype(jnp.int32), nused, pos_t, xr, P, D)
    ys = _moe_experts(blk_e, nused, xs, w_gate[0].astype(BF16), w_up[0].astype(BF16), w_down[0].astype(BF16))
    out = _final(h1, xn2, ys[pos_t], wsel_t.T, p[0].reshape(N, -1), ws_gate[0].astype(BF16), ws_up[0].astype(BF16),
                 ws_down[0].astype(BF16), g_ple, w_ple_gate[0].astype(BF16), w_ple[0].astype(BF16),
                 g_final[None, :])
    return out.reshape(B, S, D)
```

```python
import functools

import numpy as np
import jax
import jax.numpy as jnp
from jax import lax
from jax.experimental import pallas as pl
from jax.experimental.pallas import tpu as pltpu
from jax.experimental.pallas import tpu_sc as plsc

F32 = jnp.float32
BF16 = jnp.bfloat16

HEAD_DIM = 64
LANE = 128
RMS_EPS = 1e-6
DIL_HEADS = 8
DIL_PATTERNS = ((128, 1), (512, 4), (2048, 16))
DIL_BLOCK = 128
DIL_STEP_ROWS = 4096
NSA_HEADS = 8
NSA_KV_HEADS = 2
NSA_REP = NSA_HEADS // NSA_KV_HEADS
CMP_BLOCK = 32
CMP_STRIDE = 16
CMP_HIDDEN = 256
SLC_BLOCK = 64
SLC_TOP_N = 16
SLC_FORCE = 1e9
WIN_SIZE = 512
N_EXPERTS = 64
TOP_K = 8
N_GROUPS = 8
TOP_GROUPS = 4
GROUP_SIZE = N_EXPERTS // N_GROUPS
ROUTED_SCALE = 2.5
IN_SIZES = (512, 512, 512, 512, 128, 128, 128, 128, 128, 128, 24, 1024, 1024)

NEG = -1e30
NT_DIMS = (((1,), (1,)), ((), ()))

ROW_TILE = 512
INPROJ_ROWS = 1024
NSA_Q = 256
NSA_TK = 512
MOE_ROWS = 512
VMEM_LIMIT = 48 * 1024 * 1024
SC_CORES = 2
SC_SUBCORES = 16
SC_WINDOW = 128


def _dot(a, b):
    return jnp.dot(a, b, preferred_element_type=F32)


def _dot_nt(a, b):
    return lax.dot_general(a, b, NT_DIMS, preferred_element_type=F32)


def _split3(x):
    hi = x.astype(BF16)
    r1 = x - hi.astype(F32)
    mid = r1.astype(BF16)
    lo = (r1 - mid.astype(F32)).astype(BF16)
    return hi, mid, lo


def _rms(x, g):
    r = lax.rsqrt(jnp.mean(x * x, axis=-1, keepdims=True) + RMS_EPS)
    return (x * r) * g


def _params(sem):
    return pltpu.CompilerParams(dimension_semantics=sem, vmem_limit_bytes=VMEM_LIMIT)


def _inproj_kernel(x_ref, g_ref, w_ref, o_ref, *rest, dils):
    dil_refs, xn_ref = rest[:len(dils)], rest[len(dils)]
    res_ref = rest[len(dils) + 1] if dils else None

    @pl.when(pl.program_id(1) == 0)
    def _():
        xn_ref[...] = _rms(x_ref[...], g_ref[...]).astype(BF16)

    res = _dot(xn_ref[...], w_ref[...])
    o_ref[...] = res.astype(o_ref.dtype)

    if dils:
        tn = res.shape[1]
        for c in range(tn // LANE):
            res_ref[c] = res[:, c * LANE:(c + 1) * LANE]
        for d_ref, dil in zip(dil_refs, dils):
            for r in range(dil):
                for c in range(tn // LANE):
                    piece = res_ref[c, pl.ds(r, INPROJ_ROWS // dil, stride=dil), :]
                    d_ref[:, r * tn + c * LANE:r * tn + (c + 1) * LANE] = piece.astype(d_ref.dtype)


def _inproj(x2, g, w, out_dtype, tn, dils=()):
    n, d = x2.shape
    cols = w.shape[1]
    out_specs = [pl.BlockSpec((INPROJ_ROWS, tn), lambda i, j: (i, j))]
    out_shape = [jax.ShapeDtypeStruct((n, cols), out_dtype)]
    for dil in dils:
        out_specs.append(pl.BlockSpec((INPROJ_ROWS // dil, dil * tn), lambda i, j: (i, j)))
        out_shape.append(jax.ShapeDtypeStruct((n // dil, cols * dil), out_dtype))
    return pl.pallas_call(
        functools.partial(_inproj_kernel, dils=dils),
        grid=(n // INPROJ_ROWS, cols // tn),
        in_specs=[
            pl.BlockSpec((INPROJ_ROWS, d), lambda i, j: (i, 0)),
            pl.BlockSpec((1, d), lambda i, j: (0, 0)),
            pl.BlockSpec((d, tn), lambda i, j: (0, j)),
        ],
        out_specs=out_specs,
        out_shape=out_shape,
        scratch_shapes=[pltpu.VMEM((INPROJ_ROWS, d), BF16)]
        + ([pltpu.VMEM((tn // LANE, INPROJ_ROWS, LANE), F32)] if dils else []),
        compiler_params=_params(("parallel", "arbitrary")),
    )(x2, g, w)


def _dil_kernel(q_ref, kp_ref, kc_ref, vp_ref, vc_ref, bias_ref, o_ref, l_ref, *, dil, nq):
    i = pl.program_id(1)
    res = pl.program_id(2)
    blk = DIL_BLOCK
    lane = lax.broadcasted_iota(jnp.int32, (blk, LANE), 1)
    lo = lane < HEAD_DIM
    for sub in range(nq):
        rows = slice(sub * blk, (sub + 1) * blk)
        prev = slice((sub - 1) * blk, sub * blk)
        out_rows = pl.ds(sub * blk * dil + res, blk, stride=dil) if dil > 1 else rows
        variant = jnp.minimum(i, 1) * DIL_HEADS if sub == 0 else DIL_HEADS
        for hp in range(DIL_HEADS // 2):
            sl = slice(hp * LANE, (hp + 1) * LANE)
            q2 = q_ref[0, rows, sl]
            k_prev = kp_ref[0, :, sl] if sub == 0 else kc_ref[0, prev, sl]
            v_prev = vp_ref[0, :, sl] if sub == 0 else vc_ref[0, prev, sl]
            k2 = jnp.concatenate([k_prev, kc_ref[0, rows, sl]], axis=0)
            v2 = jnp.concatenate([v_prev, vc_ref[0, rows, sl]], axis=0)
            outs, lses = [], []
            for hh in range(2):
                keep = lo if hh == 0 else jnp.logical_not(lo)
                qm = jnp.where(keep, q2, jnp.zeros_like(q2))
                s = _dot_nt(qm, k2) + bias_ref[variant + 2 * hp + hh]
                m = jnp.max(s, axis=-1, keepdims=True)
                e = jnp.exp(s - m)
                den = jnp.sum(e, axis=-1, keepdims=True)
                outs.append(_dot(e.astype(BF16), v2) / den)
                lses.append(jnp.broadcast_to(m + jnp.log(den), (blk, LANE)))
            o_ref[hp, 0, out_rows, :] = jnp.where(lo, outs[0], outs[1])
            l_ref[hp, 0, out_rows, :] = jnp.where(lo, lses[0], lses[1])


def _dil_bias(slopes_a, dil):
    blk = DIL_BLOCK
    qi = jnp.arange(blk)[:, None]
    kj = jnp.arange(2 * blk)[None, :]
    dist = (qi + blk - kj).astype(F32)
    band = (dist >= 0) & (dist <= blk)
    slope = (slopes_a * dil)[:, None, None]
    bias = -slope * dist
    first = jnp.where(band & (kj >= blk), bias, NEG)
    rest = jnp.where(band, bias, NEG)
    return jnp.concatenate([first, rest], axis=0)


def _dilated(src3, bias, dil, b, s):
    nq = min(8, DIL_STEP_ROWS // (DIL_BLOCK * dil))
    nstep = s // dil // (DIL_BLOCK * nq)
    aw = DIL_HEADS * HEAD_DIM
    npair = DIL_HEADS // 2

    def spec(which, prev):
        if prev:
            return pl.BlockSpec((1, DIL_BLOCK, aw), lambda bb, i, r: (bb, jnp.maximum(nq * i - 1, 0), which * dil + r))
        return pl.BlockSpec((1, DIL_BLOCK * nq, aw), lambda bb, i, r: (bb, i, which * dil + r))

    out_spec = pl.BlockSpec((npair, 1, DIL_BLOCK * nq * dil, LANE), lambda bb, i, r: (0, bb, i, 0))
    return pl.pallas_call(
        functools.partial(_dil_kernel, dil=dil, nq=nq),
        grid=(b, nstep, dil),
        in_specs=[
            spec(0, False), spec(1, True), spec(1, False), spec(2, True), spec(2, False),
            pl.BlockSpec(bias.shape, lambda bb, i, r: (0, 0, 0)),
        ],
        out_specs=[out_spec, out_spec],
        out_shape=[jax.ShapeDtypeStruct((npair, b, s, LANE), F32)] * 2,
        compiler_params=_params(("parallel", "arbitrary", "arbitrary")),
    )(src3, src3, src3, src3, src3, bias)


def _cmp_kernel(x_ref, pos_ref, w1_ref, b1_ref, w2_ref, o_ref):
    half = x_ref.shape[-1]
    x = x_ref[0, 0]
    w1 = w1_ref[0]
    first = _dot(x, w1[:half])
    second = _dot(x, w1[half:])
    nrow = x.shape[0]
    pre = first + pltpu.roll(second, nrow - 1, 0)
    posb = _dot(jnp.broadcast_to(pos_ref[0], (8, 2 * half)).astype(BF16), w1)[0:1]
    pre = pre + posb + b1_ref[0]
    o_ref[0, 0] = _dot(jax.nn.gelu(pre).astype(BF16), w2_ref[0]).astype(o_ref.dtype)


def _compress(xc, pos, w1, b1, w2):
    b, four, nchunk, width = xc.shape
    return pl.pallas_call(
        _cmp_kernel,
        grid=(b, four),
        in_specs=[
            pl.BlockSpec((1, 1, nchunk, width), lambda bb, j: (bb, j, 0, 0)),
            pl.BlockSpec((1, 1, 2 * width), lambda bb, j: (j // 2, 0, 0)),
            pl.BlockSpec((1, 2 * width, CMP_HIDDEN), lambda bb, j: (j // 2, 0, 0)),
            pl.BlockSpec((1, 1, CMP_HIDDEN), lambda bb, j: (j // 2, 0, 0)),
            pl.BlockSpec((1, CMP_HIDDEN, LANE), lambda bb, j: (j // 2, 0, 0)),
        ],
        out_specs=pl.BlockSpec((1, 1, nchunk, LANE), lambda bb, j: (bb, j, 0, 0)),
        out_shape=jax.ShapeDtypeStruct((b, four, nchunk, LANE), BF16),
        compiler_params=_params(("parallel", "parallel")),
    )(xc, pos, w1, b1, w2)


POS_LANE = HEAD_DIM
POS_SPLIT = 64
SUM_LANE = HEAD_DIM
MASKED = -1e30
ROW_FLOOR = -1e29


def _pos_columns(pos):
    out = np.zeros((pos.shape[0], LANE), np.float32)
    out[:, POS_LANE] = pos // POS_SPLIT
    out[:, POS_LANE + 1] = pos % POS_SPLIT
    out[:, POS_LANE + 2] = 1.0
    out[:, POS_LANE + 3] = 1.0
    return jnp.asarray(out, BF16)


def _nsa_kernel(slope_ref, q_ref, kc_ref, vct_ref, ks_ref, kw_ref, vt_ref, gt_ref, ov_ref, kpa_ref, cpa_ref,
                oh_ref, o_ref, kaug_ref, flag_ref, m_ref, acc_ref, *, seq):
    g = pl.program_id(1)
    c = pl.program_id(2)
    C = NSA_Q
    R = NSA_REP
    TK = NSA_TK
    rows = R * C
    t0 = c * C
    scale = HEAD_DIM ** -0.5
    nb = seq // SLC_BLOCK
    ncmp = seq // CMP_STRIDE
    n_sel = min(SLC_TOP_N, nb)
    vblocks = TK // LANE

    @pl.when(c == 0)
    def _():
        kaug_ref[:, 0:LANE] = ks_ref[0] + kpa_ref[...]
        kaug_ref[:, LANE:2 * LANE] = oh_ref[...]

    row = lax.broadcasted_iota(jnp.int32, (rows, 1), 0)
    ti = t0 + row % C
    slope = jnp.zeros((rows, 1), F32)
    for r in range(R):
        slope = jnp.where(row // C == r, slope_ref[g * R + r], slope)
    lane = lax.broadcasted_iota(jnp.int32, (rows, LANE), 1)
    thi = (ti // POS_SPLIT).astype(F32)
    tlo = (ti % POS_SPLIT).astype(F32)
    qcols = jnp.where(lane == POS_LANE, POS_SPLIT * slope,
                      jnp.where(lane == POS_LANE + 1, slope,
                                jnp.where(lane == POS_LANE + 2, -POS_SPLIT * slope * thi,
                                          jnp.where(lane == POS_LANE + 3, -slope * tlo, 0.0))))
    q4 = jnp.concatenate([q_ref[0, :, r * LANE:(r + 1) * LANE] for r in range(R)], axis=0)
    q4 = (q4.astype(F32) * scale + qcols).astype(BF16)
    tq = t0 + lax.broadcasted_iota(jnp.int32, (1, C), 1)

    def heads(x):
        return jnp.concatenate([x] * R, axis=1)

    kc = kc_ref[0] + cpa_ref[...]
    ci = lax.broadcasted_iota(jnp.int32, (ncmp, 1), 0)
    vis = (ci * CMP_STRIDE + (CMP_BLOCK - 1) <= tq) & (ci < ncmp - 1)
    s = _dot_nt(kc, q4) + heads(jnp.where(vis, 0.0, MASKED))
    m = jnp.maximum(jnp.max(s, axis=0, keepdims=True), ROW_FLOOR)
    e = jnp.exp(s - m)
    den = jnp.sum(e, axis=0, keepdims=True)
    pc = e * (1.0 / jnp.where(den > 0, den, 1.0))
    o_cmp = _dot(vct_ref[0], pc.astype(BF16))
    pcsum = pc[:, 0:C]
    for r in range(1, R):
        pcsum = pcsum + pc[:, r * C:(r + 1) * C]
    ov = ov_ref[...]
    p_slc = sum(_dot(ov, part) for part in _split3(pcsum))

    def values_t(which, first_block, n):
        return jnp.concatenate([vt_ref[0, which, first_block + i] for i in range(n)], axis=1)

    span = WIN_SIZE + C
    ws = pl.multiple_of(jnp.maximum(t0 - WIN_SIZE, 0), C)
    k_w = kw_ref[0, pl.ds(ws, span), :] + kpa_ref[pl.ds(ws, span), :]
    wpos = ws + lax.broadcasted_iota(jnp.int32, (span, 1), 0)
    wbias = jnp.where((wpos <= tq) & (wpos >= tq - (WIN_SIZE - 1)), 0.0, MASKED)
    s3 = _dot_nt(k_w, q4) + heads(wbias)
    e3 = jnp.exp(s3 - jnp.max(s3, axis=0, keepdims=True))
    ow = _dot(values_t(1, ws // LANE, span // LANE), e3.astype(BF16))
    o_win = ow / ow[SUM_LANE:SUM_LANE + 1, :]

    j = lax.broadcasted_iota(jnp.int32, (LANE, C), 0)
    cur = tq // SLC_BLOCK
    forced = (j == 0) | (j == cur) | (j == cur - 1)
    future = j * SLC_BLOCK > tq
    work = jnp.where(forced, -jnp.inf, jnp.where(future, -SLC_FORCE, p_slc))
    work = jnp.where(j < nb, work, -jnp.inf)
    sel = jnp.where(forced, 1.0, 0.0)
    jf = j.astype(F32)
    for _ in range(n_sel - 3):
        mx = jnp.max(work, axis=0, keepdims=True)
        first = jnp.min(jnp.where(work == mx, jf, float(LANE)), axis=0, keepdims=True)
        pick = jf == first
        sel = jnp.where(pick, 1.0, sel)
        work = jnp.where(pick, -jnp.inf, work)
    sel = jnp.where(future, 0.0, sel)
    blocks_per_tile = TK // SLC_BLOCK
    for kt in range(seq // TK):
        flag_ref[kt] = jnp.max(sel[kt * blocks_per_tile:(kt + 1) * blocks_per_tile, :])
    sel_bias = jnp.where(sel.T > 0.5, 0.0, MASKED).astype(BF16)
    q_aug = jnp.concatenate([q4, jnp.concatenate([sel_bias] * R, axis=0)], axis=1)

    m_ref[...] = jnp.full((1, rows), ROW_FLOOR, F32)
    acc_ref[...] = jnp.zeros((LANE, rows), F32)
    kio = lax.broadcasted_iota(jnp.int32, (TK, 1), 0)

    def tile_update(kt, causal):
        start = pl.multiple_of(kt * TK, TK)
        s2 = _dot_nt(kaug_ref[pl.ds(start, TK), :], q_aug)
        if causal:
            s2 = s2 + heads(jnp.where(start + kio <= tq, 0.0, MASKED))
        m_i = m_ref[...]
        m_n = jnp.maximum(m_i, jnp.max(s2, axis=0, keepdims=True))
        alpha = jnp.exp(m_i - m_n)
        p2 = jnp.exp(s2 - m_n)
        acc_ref[...] = alpha * acc_ref[...] + _dot(values_t(0, kt * vblocks, vblocks), p2.astype(BF16))
        m_ref[...] = m_n

    def slc_body(kt, carry):
        @pl.when(flag_ref[kt] > 0.0)
        def _():
            tile_update(kt, False)

        return carry

    last = (t0 + C - 1) // TK
    lax.fori_loop(0, last, slc_body, 0)
    tile_update(last, True)
    acc = acc_ref[...]
    o_slc = acc / acc[SUM_LANE:SUM_LANE + 1, :]

    gates = jax.nn.sigmoid(gt_ref[0]).T
    for r in range(R):
        rs = slice(r * C, (r + 1) * C)
        out = (gates[r:r + 1] * o_cmp[:, rs] + gates[R + r:R + r + 1] * o_slc[:, rs]
               + gates[2 * R + r:2 * R + r + 1] * o_win[:, rs])
        o_ref[0, :, r * LANE:(r + 1) * LANE] = out.T.astype(o_ref.dtype)


def _nsa(slopes_b, pa3, gt3, kc, vct, vt, b, s, col0, gcol0):
    C = NSA_Q
    R = NSA_REP
    G = NSA_KV_HEADS
    nb = s // SLC_BLOCK
    ncmp = s // CMP_STRIDE
    assert nb <= LANE
    qc0 = col0 // R
    kv0 = col0 + NSA_HEADS
    rows = R * C

    ci = np.arange(ncmp)[None, :]
    sj = np.arange(LANE)[:, None]
    ov = ((ci * CMP_STRIDE < (sj + 1) * SLC_BLOCK) & (ci * CMP_STRIDE + CMP_BLOCK > sj * SLC_BLOCK)
          & (ci < ncmp - 1) & (sj < nb))
    ov = jnp.asarray(ov, BF16)
    kpa = _pos_columns(np.arange(s))
    cpa = _pos_columns(np.arange(ncmp) * CMP_STRIDE + (CMP_BLOCK - 1))
    oh = jnp.asarray(np.arange(s)[:, None] // SLC_BLOCK == np.arange(LANE)[None, :], BF16)

    def k_spec(which):
        return pl.BlockSpec((1, s, LANE), lambda bb, g, c: (bb, 0, kv0 + which * G + g))

    const = lambda a: pl.BlockSpec(a.shape, lambda bb, g, c: (0,) * a.ndim)
    return pl.pallas_call(
        functools.partial(_nsa_kernel, seq=s),
        grid=(b, G, s // C),
        in_specs=[
            pl.BlockSpec(memory_space=pltpu.SMEM),
            pl.BlockSpec((1, C, R * LANE), lambda bb, g, c: (bb, c, qc0 + g)),
            pl.BlockSpec((1, ncmp, LANE), lambda bb, g, c: (bb * G + g, 0, 0)),
            pl.BlockSpec((1, LANE, ncmp), lambda bb, g, c: (bb * G + g, 0, 0)),
            k_spec(2), k_spec(4),
            pl.BlockSpec((1, 2, s // LANE, LANE, LANE), lambda bb, g, c: (bb * G + g, 0, 0, 0, 0)),
            pl.BlockSpec((1, C, LANE), lambda bb, g, c: (bb, c, gcol0 + g)),
            const(ov), const(kpa), const(cpa), const(oh),
        ],
        out_specs=pl.BlockSpec((1, C, R * LANE), lambda bb, g, c: (bb, c, g)),
        out_shape=jax.ShapeDtypeStruct((b, s, NSA_HEADS * LANE), BF16),
        scratch_shapes=[
            pltpu.VMEM((s, 2 * LANE), BF16),
            pltpu.SMEM((s // NSA_TK,), F32),
            pltpu.VMEM((1, rows), F32),
            pltpu.VMEM((LANE, rows), F32),
        ],
        compiler_params=_params(("parallel", "parallel", "arbitrary")),
    )(slopes_b, pa3, kc, vct, pa3, pa3, vt, gt3, ov, kpa, cpa, oh)


def _merge_kernel(o1, o2, o3, l1, l2, l3, yb, ga, gb, x_ref, wa, wb, wo, gf, wrh, wrl,
                  h_ref, xn_ref, lg_ref, xr_ref):
    def pairs(ref):
        return jnp.concatenate([ref[h] for h in range(ref.shape[0])], axis=1)

    a1, a2, a3 = pairs(l1), pairs(l2), pairs(l3)
    mx = jnp.maximum(jnp.maximum(a1, a2), a3)
    e1, e2, e3 = jnp.exp(a1 - mx), jnp.exp(a2 - mx), jnp.exp(a3 - mx)
    tot = e1 + e2 + e3
    ya = (e1 / tot) * pairs(o1) + (e2 / tot) * pairs(o2) + (e3 / tot) * pairs(o3)
    merged = (jax.nn.sigmoid(ga[...]) * _dot(ya.astype(BF16), wa[...])
              + jax.nn.sigmoid(gb[...]) * _dot(yb[...], wb[...]))
    h = x_ref[...] + _dot(merged.astype(BF16), wo[...])
    h_ref[...] = h
    xn = _rms(h, gf[...])
    xn_ref[...] = xn.astype(BF16)
    tm, d = xn.shape
    half = d // (2 * LANE)

    def bf16_bits(v):
        return pltpu.bitcast(v.astype(BF16).astype(F32), jnp.uint32)

    for c in range(half):
        lo = bf16_bits(xn[:, c * LANE:(c + 1) * LANE])
        hi = bf16_bits(xn[:, (c + half) * LANE:(c + half + 1) * LANE])
        xr_ref[pl.ds(c, tm, stride=half), :] = hi | (lo >> 16)
    xh = xn.astype(BF16)
    xl = (xn - xh.astype(F32)).astype(BF16)
    lg_ref[...] = _dot_nt(wrh[...], xh) + _dot_nt(wrh[...], xl) + _dot_nt(wrl[...], xh)


def _merge(o_list, l_list, yb, gt, x2, wa, wb, wo, gf, wrh, wrl):
    n, d = x2.shape
    aw = wa.shape[0]
    bw = wb.shape[0]
    tm = ROW_TILE
    row = lambda w: pl.BlockSpec((tm, w), lambda i: (i, 0))
    full = lambda a: pl.BlockSpec(a.shape, lambda i: (0,) * a.ndim)
    return pl.pallas_call(
        _merge_kernel,
        grid=(n // tm,),
        in_specs=[pl.BlockSpec((aw // LANE, tm, LANE), lambda i: (0, i, 0))] * 6 + [
            row(bw),
            pl.BlockSpec((tm, d), lambda i: (i, 0)),
            pl.BlockSpec((tm, d), lambda i: (i, 1)),
            row(d), full(wa), full(wb), full(wo), full(gf), full(wrh), full(wrl),
        ],
        out_specs=[row(d), row(d), pl.BlockSpec((N_EXPERTS, tm), lambda i: (0, i)),
                   pl.BlockSpec((tm * d // (2 * LANE), LANE), lambda i: (i, 0))],
        out_shape=[
            jax.ShapeDtypeStruct((n, d), F32),
            jax.ShapeDtypeStruct((n, d), BF16),
            jax.ShapeDtypeStruct((N_EXPERTS, n), F32),
            jax.ShapeDtypeStruct((n * d // (2 * LANE), LANE), jnp.uint32),
        ],
        compiler_params=_params(("parallel",)),
    )(*o_list, *l_list, yb, gt, gt, x2, wa, wb, wo, gf, wrh, wrl)


def _route_kernel(lg_ref, bias_ref, tri_ref, idx_ref, w_ref, rank_ref, cnt_ref, carry_ref):
    E = N_EXPERTS
    tn = lg_ref.shape[1]
    scores = jax.nn.sigmoid(lg_ref[...])
    sel = scores + bias_ref[...]
    sub = lax.broadcasted_iota(jnp.int32, (GROUP_SIZE, tn), 0).astype(F32)
    gs = []
    for grp in range(N_GROUPS):
        blk = sel[grp * GROUP_SIZE:(grp + 1) * GROUP_SIZE]
        m1 = jnp.max(blk, axis=0, keepdims=True)
        first = jnp.min(jnp.where(blk == m1, sub, float(GROUP_SIZE)), axis=0, keepdims=True)
        m2 = jnp.max(jnp.where(sub == first, -jnp.inf, blk), axis=0, keepdims=True)
        gs.append(m1 + m2)
    keep = []
    for a in range(N_GROUPS):
        rank = jnp.zeros((1, tn), jnp.int32)
        for o in range(N_GROUPS):
            if o == a:
                continue
            beats = (gs[o] >= gs[a]) if o < a else (gs[o] > gs[a])
            rank = rank + beats.astype(jnp.int32)
        keep.append(jnp.broadcast_to(rank < TOP_GROUPS, (GROUP_SIZE, tn)))
    emask = jnp.concatenate(keep, axis=0)
    cand = jnp.where(emask, sel, -jnp.inf)
    eid = lax.broadcasted_iota(jnp.int32, (E, tn), 0)
    eid_f = eid.astype(F32)
    rank = jnp.zeros((E, tn), jnp.int32)
    for o in range(E):
        other = cand[o:o + 1]
        beats = (other > cand) | ((other == cand) & (eid > o))
        rank = rank + beats.astype(jnp.int32)
    chosen = emask & (rank < TOP_K)
    wsel = jnp.where(chosen, scores, 0.0)
    wnorm = wsel / jnp.sum(wsel, axis=0, keepdims=True) * ROUTED_SCALE

    @pl.when(pl.program_id(0) == 0)
    def _():
        carry_ref[...] = jnp.zeros_like(carry_ref)

    chosen_f = jnp.where(chosen, 1.0, 0.0)
    before = carry_ref[...] + _dot(chosen_f.astype(BF16), tri_ref[...]) - chosen_f
    carry_ref[...] = carry_ref[...] + jnp.sum(chosen_f, axis=1, keepdims=True)
    cnt_ref[...] = jnp.broadcast_to(carry_ref[...], cnt_ref.shape).astype(jnp.int32)
    for k in range(TOP_K):
        hit = chosen & (rank == k)
        idx_ref[k:k + 1, :] = jnp.sum(jnp.where(hit, eid_f, 0.0), axis=0, keepdims=True).astype(jnp.int32)
        w_ref[k:k + 1, :] = jnp.sum(jnp.where(hit, wnorm, 0.0), axis=0, keepdims=True)
        rank_ref[k:k + 1, :] = jnp.sum(jnp.where(hit, before, 0.0), axis=0, keepdims=True).astype(jnp.int32)


def _route(logits_t, bias_col):
    e, n = logits_t.shape
    tn = ROW_TILE
    tri = jnp.asarray(np.triu(np.ones((tn, tn), np.float32)), BF16)
    slot = pl.BlockSpec((TOP_K, tn), lambda i: (0, i))
    return pl.pallas_call(
        _route_kernel,
        grid=(n // tn,),
        in_specs=[pl.BlockSpec((e, tn), lambda i: (0, i)), pl.BlockSpec((e, 1), lambda i: (0, 0)),
                  pl.BlockSpec((tn, tn), lambda i: (0, 0))],
        out_specs=[slot, slot, slot, pl.BlockSpec((e, LANE), lambda i: (0, 0))],
        out_shape=[
            jax.ShapeDtypeStruct((TOP_K, n), jnp.int32),
            jax.ShapeDtypeStruct((TOP_K, n), F32),
            jax.ShapeDtypeStruct((TOP_K, n), jnp.int32),
            jax.ShapeDtypeStruct((e, LANE), jnp.int32),
        ],
        scratch_shapes=[pltpu.VMEM((e, 1), F32)],
        compiler_params=_params(("arbitrary",)),
    )(logits_t, bias_col, tri)


def _dispatch_kernel(fill_lo_ref, fill_hi_ref, nused_ref, pos_ref, x_ref, xs_ref, zero_ref, sem, zsem):
    i = pl.program_id(0)
    tm = pos_ref.shape[1]
    sub = x_ref.shape[0] // tm
    blk_rows = zero_ref.shape[0]
    nblk = xs_ref.shape[0] // blk_rows

    def slot(r):
        return xs_ref.at[pl.ds(pl.multiple_of(r * sub, sub), sub), :]

    @pl.when(i == 0)
    def _():
        zero_ref[...] = jnp.zeros_like(zero_ref)

        def per_expert(e, carry):
            def fill(r, c2):
                pltpu.make_async_copy(zero_ref.at[pl.ds(0, sub), :], slot(r), zsem).start()
                return c2

            def drain(r, c2):
                pltpu.make_async_copy(zero_ref.at[pl.ds(0, sub), :], slot(r), zsem).wait()
                return c2

            lax.fori_loop(fill_lo_ref[e], fill_hi_ref[e], fill, 0)
            lax.fori_loop(fill_lo_ref[e], fill_hi_ref[e], drain, 0)
            return carry

        lax.fori_loop(0, fill_lo_ref.shape[0], per_expert, 0)

        def tail(bk, carry):
            cp = pltpu.make_async_copy(
                zero_ref, xs_ref.at[pl.ds(pl.multiple_of(bk * blk_rows, blk_rows), blk_rows), :], zsem)
            cp.start()
            cp.wait()
            return carry

        lax.fori_loop(nused_ref[0], nblk, tail, 0)

    def scatter(t, carry):
        src = x_ref.at[pl.ds(pl.multiple_of(t * sub, sub), sub), :]
        for k in range(TOP_K):
            pltpu.make_async_copy(src, slot(pos_ref[k, t]), sem).start()
        return carry

    lax.fori_loop(0, tm, scatter, 0)
    for k in range(TOP_K):
        pltpu.make_async_copy(x_ref, xs_ref.at[pl.ds(0, tm * sub), :], sem).wait()


def _dispatch(fill_lo, fill_hi, nused, pos_t, xr, p_rows, d):
    k, n = pos_t.shape
    sub = d // (2 * LANE)
    tm = ROW_TILE
    grid_spec = pltpu.PrefetchScalarGridSpec(
        num_scalar_prefetch=3,
        grid=(n // tm,),
        in_specs=[
            pl.BlockSpec((k, tm), lambda i, *_: (0, i), memory_space=pltpu.SMEM),
            pl.BlockSpec((tm * sub, LANE), lambda i, *_: (i, 0)),
        ],
        out_specs=pl.BlockSpec(memory_space=pl.ANY),
        scratch_shapes=[pltpu.VMEM((MOE_ROWS * sub, LANE), xr.dtype), pltpu.SemaphoreType.DMA(()),
                        pltpu.SemaphoreType.DMA(())],
    )
    return pl.pallas_call(
        _dispatch_kernel,
        grid_spec=grid_spec,
        out_shape=jax.ShapeDtypeStruct((p_rows * sub, LANE), xr.dtype),
        compiler_params=_params(("arbitrary",)),
    )(fill_lo, fill_hi, nused, pos_t, xr)


def _dispatch_sc(pos_t, xr3, p_rows):
    k, n = pos_t.shape
    _, sub, lane = xr3.shape
    nk = k * n
    workers = SC_CORES * SC_SUBCORES
    per_worker = nk // workers
    assert nk % (workers * SC_WINDOW) == 0 and n % SC_WINDOW == 0
    mesh = plsc.VectorSubcoreMesh(core_axis_name="c", subcore_axis_name="s")

    @functools.partial(
        pl.kernel, mesh=mesh,
        out_type=jax.ShapeDtypeStruct((p_rows, sub, lane), xr3.dtype),
        scratch_types=[pltpu.VMEM((SC_WINDOW,), jnp.int32), pltpu.VMEM((SC_WINDOW, sub, lane), xr3.dtype),
                       pltpu.SemaphoreType.DMA],
    )
    def scatter_rows(x_hbm, idx_hbm, o_hbm, idx_v, rows_v, sem):
        wid = lax.axis_index("s") * SC_CORES + lax.axis_index("c")

        @pl.loop(0, per_worker // SC_WINDOW)
        def _(c):
            base = wid * per_worker + c * SC_WINDOW
            tok0 = lax.rem(base, n)
            pltpu.sync_copy(idx_hbm.at[pl.ds(base, SC_WINDOW)], idx_v)
            pltpu.sync_copy(x_hbm.at[pl.ds(tok0, SC_WINDOW)], rows_v)
            pltpu.async_copy(rows_v, o_hbm.at[idx_v], sem).wait()

    return scatter_rows(xr3, pos_t.reshape(nk))


def _moe_kernel(blk_e_ref, nused_ref, nvalid_ref, xs_ref, wg_ref, wu_ref, wd_ref, o_ref):
    i = pl.program_id(0)

    @pl.when(i < nused_ref[0])
    def _():
        sub = xs_ref.shape[0] // MOE_ROWS
        owned = lax.broadcasted_iota(jnp.int32, (MOE_ROWS, LANE), 0) < nvalid_ref[i]
        words = [jnp.where(owned, xs_ref[pl.ds(c, MOE_ROWS, stride=sub), :], jnp.uint32(0)) for c in range(sub)]
        lo = [pltpu.bitcast(w << 16, F32).astype(BF16) for w in words]
        hi = [pltpu.bitcast(w & jnp.uint32(0xFFFF0000), F32).astype(BF16) for w in words]
        x = jnp.concatenate(lo + hi, axis=1)
        hg = _dot(x, wg_ref[0])
        hid = (hg * jax.nn.sigmoid(hg)) * _dot(x, wu_ref[0])
        o_ref[...] = _dot(hid.astype(BF16), wd_ref[0]).astype(o_ref.dtype)

    @pl.when(i >= nused_ref[0])
    def _():
        o_ref[...] = jnp.zeros_like(o_ref)


def _moe_experts(blk_e, nused, nvalid, xs, wg, wu, wd):
    d = wg.shape[1]
    sub = d // (2 * LANE)
    p = xs.shape[0] // sub
    hid = wg.shape[2]
    grid_spec = pltpu.PrefetchScalarGridSpec(
        num_scalar_prefetch=3,
        grid=(p // MOE_ROWS,),
        in_specs=[
            pl.BlockSpec((MOE_ROWS * sub, LANE), lambda i, be, nu, nv: (jnp.minimum(i, nu[0] - 1), 0)),
            pl.BlockSpec((1, d, hid), lambda i, be, nu, nv: (be[i], 0, 0)),
            pl.BlockSpec((1, d, hid), lambda i, be, nu, nv: (be[i], 0, 0)),
            pl.BlockSpec((1, hid, d), lambda i, be, nu, nv: (be[i], 0, 0)),
        ],
        out_specs=pl.BlockSpec((MOE_ROWS, d), lambda i, be, nu, nv: (i, 0)),
    )
    return pl.pallas_call(
        _moe_kernel,
        grid_spec=grid_spec,
        out_shape=jax.ShapeDtypeStruct((p, d), BF16),
        compiler_params=_params(("arbitrary",)),
    )(blk_e, nused, nvalid, xs, wg, wu, wd)


def _final_kernel(h_ref, xn_ref, ys_ref, w_ref, p_ref, wsg, wsu, wsd, gp, wpg, wp, gfin, o_ref):
    xn = xn_ref[...]
    hg = _dot(xn, wsg[...])
    shared = _dot(((hg * jax.nn.sigmoid(hg)) * _dot(xn, wsu[...])).astype(BF16), wsd[...])
    w = w_ref[...]
    y = w[:, 0:1] * ys_ref[0].astype(F32)
    for k in range(1, TOP_K):
        y = y + w[:, k:k + 1] * ys_ref[k].astype(F32)
    h = h_ref[...] + (y + shared)
    gate = jax.nn.sigmoid(_dot(_rms(h, gp[...]).astype(BF16), wpg[...]))
    h = h + _dot(p_ref[...].astype(BF16), wp[...]) * gate
    o_ref[...] = _rms(h, gfin[...])


def _final(h1, xn2, ysg, w_nk, p2, wsg, wsu, wsd, gp, wpg, wp, gfin):
    n, d = h1.shape
    tm = ROW_TILE
    row = lambda w: pl.BlockSpec((tm, w), lambda i: (i, 0))
    full = lambda a: pl.BlockSpec(a.shape, lambda i: (0,) * a.ndim)
    return pl.pallas_call(
        _final_kernel,
        grid=(n // tm,),
        in_specs=[row(d), row(d), pl.BlockSpec((TOP_K, tm, d), lambda i: (0, i, 0)), row(TOP_K), row(p2.shape[1]),
                  full(wsg), full(wsu), full(wsd), full(gp), full(wpg), full(wp), full(gfin)],
        out_specs=row(d),
        out_shape=jax.ShapeDtypeStruct((n, d), F32),
        compiler_params=_params(("parallel",)),
    )(h1, xn2, ysg, w_nk, p2, wsg, wsu, wsd, gp, wpg, wp, gfin)


def _pad_heads(w):
    d, c = w.shape
    nh = c // HEAD_DIM
    w = w.reshape(d, nh, HEAD_DIM)
    return jnp.pad(w, ((0, 0), (0, 0), (0, LANE - HEAD_DIM))).reshape(d, nh * LANE)


def _alibi_slopes(n):
    return 2.0 ** (-8.0 * jnp.arange(1, n + 1, dtype=jnp.float32) / n)


def kernel(x, p, g_mix, w_in, cmp_pos_k, cmp_w1_k, cmp_b1_k, cmp_w2_k, cmp_pos_v, cmp_w1_v, cmp_b1_v, cmp_w2_v, w_branch_a, w_branch_b, w_out, g_ffn, w_router, router_bias, w_gate, w_up, w_down, ws_gate, ws_up, ws_down, g_ple, w_ple_gate, w_ple, g_final):
    B, S, D = x.shape
    N = B * S
    G, R = NSA_KV_HEADS, NSA_REP
    assert w_in.shape[0] == 1 and p.shape[0] == 1
    assert S % DIL_STEP_ROWS == 0 and S % NSA_TK == 0 and S >= WIN_SIZE + NSA_Q
    assert N % ROW_TILE == 0 and N % INPROJ_ROWS == 0 and S % INPROJ_ROWS == 0
    slopes = _alibi_slopes(DIL_HEADS + NSA_HEADS)
    slopes_a, slopes_b = slopes[0::2], slopes[1::2]

    offs = np.concatenate([[0], np.cumsum(IN_SIZES)])
    parts = [w_in[0][:, offs[i]:offs[i + 1]] for i in range(len(IN_SIZES))]
    w_qa, w_ka, w_va, w_qb, w_kc, w_vc, w_ks, w_vs, w_kw, w_vw, w_gn, w_ga, w_gb = parts
    w_qkv_a = jnp.concatenate([w_qa * HEAD_DIM ** -0.5, w_ka, w_va], axis=1).astype(BF16)
    w_nsa = jnp.concatenate([_pad_heads(t) for t in (w_qb, w_kc, w_vc, w_ks, w_vs, w_kw, w_vw)],
                            axis=1).astype(BF16)
    nsa_w = w_nsa.shape[1]
    w_gn = w_gn.reshape(D, 3, G, R).transpose(0, 2, 1, 3).reshape(D, G, 3 * R)
    w_gn = jnp.pad(w_gn, ((0, 0), (0, 0), (0, LANE - 3 * R))).reshape(D, G * LANE)
    w_gate_in = jnp.concatenate([w_ga, w_gb, w_gn], axis=1).astype(BF16)
    gate_w = w_gate_in.shape[1]

    x2 = x.reshape(N, D)
    dils = tuple(dil for _, dil in DIL_PATTERNS if dil > 1)
    qkv_a, *dilated_qkv = _inproj(x2, g_mix, w_qkv_a, BF16, 512, dils=dils)
    nsa_in, = _inproj(x2, g_mix, w_nsa, BF16, 512)
    gt, = _inproj(x2, g_mix, w_gate_in, F32, 768)

    o_list, l_list = [], []
    for _, dil in DIL_PATTERNS:
        src = qkv_a if dil == 1 else dilated_qkv[dils.index(dil)]
        o, l = _dilated(src.reshape(B, S // dil, -1), _dil_bias(slopes_a, dil), dil, B, S)
        o_list.append(o.reshape(-1, N, LANE))
        l_list.append(l.reshape(-1, N, LANE))

    nchunk = S // CMP_STRIDE
    c0 = NSA_HEADS * LANE
    raw = nsa_in[:, c0:c0 + 2 * G * LANE].reshape(B, nchunk, CMP_STRIDE, 2 * G, LANE)[..., :HEAD_DIM]
    xc = raw.transpose(0, 3, 1, 2, 4).reshape(B, 2 * G, nchunk, CMP_STRIDE * HEAD_DIM)
    pos = jnp.stack([cmp_pos_k[0].reshape(1, -1), cmp_pos_v[0].reshape(1, -1)])
    w1 = jnp.stack([cmp_w1_k[0], cmp_w1_v[0]]).astype(BF16)
    b1 = jnp.stack([cmp_b1_k[0][None], cmp_b1_v[0][None]])
    w2 = jnp.pad(jnp.stack([cmp_w2_k[0], cmp_w2_v[0]]), ((0, 0), (0, 0), (0, LANE - HEAD_DIM))).astype(BF16)
    kcv = _compress(xc, pos, w1, b1, w2)
    kcv = kcv.reshape(B, 2, G, nchunk, LANE).transpose(1, 0, 2, 3, 4).reshape(2, B * G, nchunk, LANE)
    vct = kcv[1].transpose(0, 2, 1)
    v0 = (NSA_HEADS + 3 * G) * LANE
    vsw = jnp.stack([nsa_in[:, v0:v0 + G * LANE], nsa_in[:, v0 + 2 * G * LANE:v0 + 3 * G * LANE]])
    vsw = vsw.reshape(2, B, S // LANE, LANE, G, LANE).transpose(1, 4, 0, 2, 5, 3)
    ones_row = (jnp.arange(LANE) == SUM_LANE).astype(BF16)[:, None]
    vt = (vsw + ones_row).reshape(B * G, 2, S // LANE, LANE, LANE)
    yb = _nsa(slopes_b, nsa_in.reshape(B, S, nsa_w), gt.reshape(B, S, gate_w), kcv[0], vct, vt,
              B, S, 0, 2 * D // LANE)
    yb = yb.reshape(N, NSA_HEADS * LANE)

    wb_pad = jnp.pad(w_branch_b[0].reshape(NSA_HEADS, HEAD_DIM, D),
                     ((0, 0), (0, LANE - HEAD_DIM), (0, 0))).reshape(NSA_HEADS * LANE, D)
    wr_t = w_router[0].T
    wr_hi = wr_t.astype(BF16)
    wr_lo = (wr_t - wr_hi.astype(F32)).astype(BF16)
    h1, xn2, logits_t, xr = _merge(o_list, l_list, yb, gt, x2, w_branch_a[0].astype(BF16), wb_pad.astype(BF16),
                               w_out[0].astype(BF16), g_ffn, wr_hi, wr_lo)

    idx_t, wsel_t, rank_t, counts = _route(logits_t, router_bias[0][:, None])
    counts = counts[:, 0]
    padded = (counts + MOE_ROWS - 1) // MOE_ROWS * MOE_ROWS
    pend = jnp.cumsum(padded)
    pstart = pend - padded
    hot = idx_t[None] == jnp.arange(N_EXPERTS, dtype=jnp.int32)[:, None, None]
    pos_t = rank_t + jnp.sum(jnp.where(hot, pstart.astype(jnp.int32)[:, None, None], 0), axis=0)
    NK = N * TOP_K
    nblk = (NK + N_EXPERTS * (MOE_ROWS - 1) + MOE_ROWS - 1) // MOE_ROWS
    P = nblk * MOE_ROWS
    blk_e = jnp.minimum(jnp.sum(pend[None, :] <= (jnp.arange(nblk, dtype=jnp.int32) * MOE_ROWS)[:, None], axis=1),
                        N_EXPERTS - 1).astype(jnp.int32)
    nused = (pend[-1] // MOE_ROWS).astype(jnp.int32).reshape(1)
    row0 = jnp.arange(nblk, dtype=jnp.int32) * MOE_ROWS
    nvalid = jnp.clip((pstart + counts)[blk_e] - row0, 0, MOE_ROWS).astype(jnp.int32)
    sub = D // (2 * LANE)
    xs = _dispatch_sc(pos_t, xr.reshape(N, sub, LANE), P).reshape(P * sub, LANE)
    ys = _moe_experts(blk_e, nused, nvalid, xs, w_gate[0].astype(BF16), w_up[0].astype(BF16), w_down[0].astype(BF16))
    out = _final(h1, xn2, ys[pos_t], wsel_t.T, p[0].reshape(N, -1), ws_gate[0].astype(BF16), ws_up[0].astype(BF16),
                 ws_down[0].astype(BF16), g_ple, w_ple_gate[0].astype(BF16), w_ple[0].astype(BF16),
                 g_final[None, :])
    return out.reshape(B, S, D)
```

```python
import functools

import numpy as np
import jax
import jax.numpy as jnp
from jax import lax
from jax.experimental import pallas as pl
from jax.experimental.pallas import tpu as pltpu
from jax.experimental.pallas import tpu_sc as plsc

F32 = jnp.float32
BF16 = jnp.bfloat16

HEAD_DIM = 64
LANE = 128
RMS_EPS = 1e-6
DIL_HEADS = 8
DIL_PATTERNS = ((128, 1), (512, 4), (2048, 16))
DIL_BLOCK = 128
DIL_STEP_ROWS = 4096
NSA_HEADS = 8
NSA_KV_HEADS = 2
NSA_REP = NSA_HEADS // NSA_KV_HEADS
CMP_BLOCK = 32
CMP_STRIDE = 16
CMP_HIDDEN = 256
SLC_BLOCK = 64
SLC_TOP_N = 16
SLC_FORCE = 1e9
WIN_SIZE = 512
N_EXPERTS = 64
TOP_K = 8
N_GROUPS = 8
TOP_GROUPS = 4
GROUP_SIZE = N_EXPERTS // N_GROUPS
ROUTED_SCALE = 2.5
IN_SIZES = (512, 512, 512, 512, 128, 128, 128, 128, 128, 128, 24, 1024, 1024)

NEG = -1e30
NT_DIMS = (((1,), (1,)), ((), ()))

ROW_TILE = 512
INPROJ_ROWS = 1024
NSA_Q = 256
NSA_TK = 512
MOE_ROWS = 512
VMEM_LIMIT = 48 * 1024 * 1024
SC_CORES = 2
SC_SUBCORES = 16
SC_WINDOW = 128


def _dot(a, b):
    return jnp.dot(a, b, preferred_element_type=F32)


def _dot_nt(a, b):
    return lax.dot_general(a, b, NT_DIMS, preferred_element_type=F32)


def _split3(x):
    hi = x.astype(BF16)
    r1 = x - hi.astype(F32)
    mid = r1.astype(BF16)
    lo = (r1 - mid.astype(F32)).astype(BF16)
    return hi, mid, lo


def _pack_bf16_pairs(lo, hi):
    lo_bits = pltpu.bitcast(lo.astype(BF16).astype(F32), jnp.uint32)
    hi_bits = pltpu.bitcast(hi.astype(BF16).astype(F32), jnp.uint32)
    return hi_bits | (lo_bits >> 16)


def _unpack_bf16_pairs(words):
    return (pltpu.bitcast(words << 16, F32), pltpu.bitcast(words & jnp.uint32(0xFFFF0000), F32))


def _rms(x, g):
    r = lax.rsqrt(jnp.mean(x * x, axis=-1, keepdims=True) + RMS_EPS)
    return (x * r) * g


def _params(sem):
    return pltpu.CompilerParams(dimension_semantics=sem, vmem_limit_bytes=VMEM_LIMIT)


def _inproj_kernel(x_ref, g_ref, w_ref, o_ref, *rest, dils):
    dil_refs, xn_ref = rest[:len(dils)], rest[len(dils)]
    res_ref = rest[len(dils) + 1] if dils else None

    @pl.when(pl.program_id(1) == 0)
    def _():
        xn_ref[...] = _rms(x_ref[...], g_ref[...]).astype(BF16)

    res = _dot(xn_ref[...], w_ref[...])
    o_ref[...] = res.astype(o_ref.dtype)

    if dils:
        tn = res.shape[1]
        for c in range(tn // LANE):
            res_ref[c] = res[:, c * LANE:(c + 1) * LANE]
        for d_ref, dil in zip(dil_refs, dils):
            for r in range(dil):
                for c in range(tn // LANE):
                    piece = res_ref[c, pl.ds(r, INPROJ_ROWS // dil, stride=dil), :]
                    d_ref[:, r * tn + c * LANE:r * tn + (c + 1) * LANE] = piece.astype(d_ref.dtype)


def _inproj(x2, g, w, out_dtype, tn, dils=()):
    n, d = x2.shape
    cols = w.shape[1]
    out_specs = [pl.BlockSpec((INPROJ_ROWS, tn), lambda i, j: (i, j))]
    out_shape = [jax.ShapeDtypeStruct((n, cols), out_dtype)]
    for dil in dils:
        out_specs.append(pl.BlockSpec((INPROJ_ROWS // dil, dil * tn), lambda i, j: (i, j)))
        out_shape.append(jax.ShapeDtypeStruct((n // dil, cols * dil), out_dtype))
    return pl.pallas_call(
        functools.partial(_inproj_kernel, dils=dils),
        grid=(n // INPROJ_ROWS, cols // tn),
        in_specs=[
            pl.BlockSpec((INPROJ_ROWS, d), lambda i, j: (i, 0)),
            pl.BlockSpec((1, d), lambda i, j: (0, 0)),
            pl.BlockSpec((d, tn), lambda i, j: (0, j)),
        ],
        out_specs=out_specs,
        out_shape=out_shape,
        scratch_shapes=[pltpu.VMEM((INPROJ_ROWS, d), BF16)]
        + ([pltpu.VMEM((tn // LANE, INPROJ_ROWS, LANE), F32)] if dils else []),
        compiler_params=_params(("parallel", "arbitrary")),
    )(x2, g, w)


def _dil_kernel(q_ref, kp_ref, kc_ref, vp_ref, vc_ref, bias_ref, o_ref, l_ref, *, dil, nq):
    i = pl.program_id(1)
    res = pl.program_id(2)
    blk = DIL_BLOCK
    lane = lax.broadcasted_iota(jnp.int32, (blk, LANE), 1)
    lo = lane < HEAD_DIM
    for sub in range(nq):
        rows = slice(sub * blk, (sub + 1) * blk)
        prev = slice((sub - 1) * blk, sub * blk)
        out_rows = pl.ds(sub * blk * dil + res, blk, stride=dil) if dil > 1 else rows
        variant = jnp.minimum(i, 1) * DIL_HEADS if sub == 0 else DIL_HEADS
        for hp in range(DIL_HEADS // 2):
            sl = slice(hp * LANE, (hp + 1) * LANE)
            q2 = q_ref[0, rows, sl]
            k_prev = kp_ref[0, :, sl] if sub == 0 else kc_ref[0, prev, sl]
            v_prev = vp_ref[0, :, sl] if sub == 0 else vc_ref[0, prev, sl]
            k2 = jnp.concatenate([k_prev, kc_ref[0, rows, sl]], axis=0)
            v2 = jnp.concatenate([v_prev, vc_ref[0, rows, sl]], axis=0)
            outs, lses = [], []
            for hh in range(2):
                keep = lo if hh == 0 else jnp.logical_not(lo)
                qm = jnp.where(keep, q2, jnp.zeros_like(q2))
                s = _dot_nt(qm, k2) + bias_ref[variant + 2 * hp + hh]
                m = jnp.max(s, axis=-1, keepdims=True)
                e = jnp.exp(s - m)
                den = jnp.sum(e, axis=-1, keepdims=True)
                outs.append(_dot(e.astype(BF16), v2) / den)
                lses.append(jnp.broadcast_to(m + jnp.log(den), (blk, LANE)))
            o_ref[hp, 0, out_rows, :] = jnp.where(lo, outs[0], outs[1])
            l_ref[hp, 0, out_rows, :] = jnp.where(lo, lses[0], lses[1])


def _dil_bias(slopes_a, dil):
    blk = DIL_BLOCK
    qi = jnp.arange(blk)[:, None]
    kj = jnp.arange(2 * blk)[None, :]
    dist = (qi + blk - kj).astype(F32)
    band = (dist >= 0) & (dist <= blk)
    slope = (slopes_a * dil)[:, None, None]
    bias = -slope * dist
    first = jnp.where(band & (kj >= blk), bias, NEG)
    rest = jnp.where(band, bias, NEG)
    return jnp.concatenate([first, rest], axis=0)


def _dilated(src3, bias, dil, b, s):
    nq = min(8, DIL_STEP_ROWS // (DIL_BLOCK * dil))
    nstep = s // dil // (DIL_BLOCK * nq)
    aw = DIL_HEADS * HEAD_DIM
    npair = DIL_HEADS // 2

    def spec(which, prev):
        if prev:
            return pl.BlockSpec((1, DIL_BLOCK, aw), lambda bb, i, r: (bb, jnp.maximum(nq * i - 1, 0), which * dil + r))
        return pl.BlockSpec((1, DIL_BLOCK * nq, aw), lambda bb, i, r: (bb, i, which * dil + r))

    out_spec = pl.BlockSpec((npair, 1, DIL_BLOCK * nq * dil, LANE), lambda bb, i, r: (0, bb, i, 0))
    return pl.pallas_call(
        functools.partial(_dil_kernel, dil=dil, nq=nq),
        grid=(b, nstep, dil),
        in_specs=[
            spec(0, False), spec(1, True), spec(1, False), spec(2, True), spec(2, False),
            pl.BlockSpec(bias.shape, lambda bb, i, r: (0, 0, 0)),
        ],
        out_specs=[out_spec, out_spec],
        out_shape=[jax.ShapeDtypeStruct((npair, b, s, LANE), F32)] * 2,
        compiler_params=_params(("parallel", "arbitrary", "arbitrary")),
    )(src3, src3, src3, src3, src3, bias)


def _cmp_kernel(x_ref, pos_ref, w1_ref, b1_ref, w2_ref, o_ref):
    half = x_ref.shape[-1]
    x = x_ref[0, 0]
    w1 = w1_ref[0]
    first = _dot(x, w1[:half])
    second = _dot(x, w1[half:])
    nrow = x.shape[0]
    pre = first + pltpu.roll(second, nrow - 1, 0)
    posb = _dot(jnp.broadcast_to(pos_ref[0], (8, 2 * half)).astype(BF16), w1)[0:1]
    pre = pre + posb + b1_ref[0]
    o_ref[0, 0] = _dot(jax.nn.gelu(pre).astype(BF16), w2_ref[0]).astype(o_ref.dtype)


def _compress(xc, pos, w1, b1, w2):
    b, four, nchunk, width = xc.shape
    return pl.pallas_call(
        _cmp_kernel,
        grid=(b, four),
        in_specs=[
            pl.BlockSpec((1, 1, nchunk, width), lambda bb, j: (bb, j, 0, 0)),
            pl.BlockSpec((1, 1, 2 * width), lambda bb, j: (j // 2, 0, 0)),
            pl.BlockSpec((1, 2 * width, CMP_HIDDEN), lambda bb, j: (j // 2, 0, 0)),
            pl.BlockSpec((1, 1, CMP_HIDDEN), lambda bb, j: (j // 2, 0, 0)),
            pl.BlockSpec((1, CMP_HIDDEN, LANE), lambda bb, j: (j // 2, 0, 0)),
        ],
        out_specs=pl.BlockSpec((1, 1, nchunk, LANE), lambda bb, j: (bb, j, 0, 0)),
        out_shape=jax.ShapeDtypeStruct((b, four, nchunk, LANE), BF16),
        compiler_params=_params(("parallel", "parallel")),
    )(xc, pos, w1, b1, w2)


POS_LANE = HEAD_DIM
POS_SPLIT = 64
SUM_LANE = HEAD_DIM
MASKED = -1e30
ROW_FLOOR = -1e29


def _pos_columns(pos):
    out = np.zeros((pos.shape[0], LANE), np.float32)
    out[:, POS_LANE] = pos // POS_SPLIT
    out[:, POS_LANE + 1] = pos % POS_SPLIT
    out[:, POS_LANE + 2] = 1.0
    out[:, POS_LANE + 3] = 1.0
    return jnp.asarray(out, BF16)


def _nsa_kernel(slope_ref, q_ref, kc_ref, vct_ref, ks_ref, kw_ref, vt_ref, gt_ref, ov_ref, kpa_ref, cpa_ref,
                oh_ref, o_ref, kaug_ref, flag_ref, m_ref, acc_ref, *, seq):
    g = pl.program_id(1)
    c = pl.program_id(2)
    C = NSA_Q
    R = NSA_REP
    TK = NSA_TK
    rows = R * C
    t0 = c * C
    scale = HEAD_DIM ** -0.5
    nb = seq // SLC_BLOCK
    ncmp = seq // CMP_STRIDE
    n_sel = min(SLC_TOP_N, nb)
    vblocks = TK // LANE

    @pl.when(c == 0)
    def _():
        kaug_ref[:, 0:LANE] = ks_ref[0] + kpa_ref[...]
        kaug_ref[:, LANE:2 * LANE] = oh_ref[...]

    row = lax.broadcasted_iota(jnp.int32, (rows, 1), 0)
    ti = t0 + row % C
    slope = jnp.zeros((rows, 1), F32)
    for r in range(R):
        slope = jnp.where(row // C == r, slope_ref[g * R + r], slope)
    lane = lax.broadcasted_iota(jnp.int32, (rows, LANE), 1)
    thi = (ti // POS_SPLIT).astype(F32)
    tlo = (ti % POS_SPLIT).astype(F32)
    qcols = jnp.where(lane == POS_LANE, POS_SPLIT * slope,
                      jnp.where(lane == POS_LANE + 1, slope,
                                jnp.where(lane == POS_LANE + 2, -POS_SPLIT * slope * thi,
                                          jnp.where(lane == POS_LANE + 3, -slope * tlo, 0.0))))
    q4 = jnp.concatenate([q_ref[0, :, r * LANE:(r + 1) * LANE] for r in range(R)], axis=0)
    q4 = (q4.astype(F32) * scale + qcols).astype(BF16)
    tq = t0 + lax.broadcasted_iota(jnp.int32, (1, C), 1)

    def heads(x):
        return jnp.concatenate([x] * R, axis=1)

    kc = kc_ref[0] + cpa_ref[...]
    ci = lax.broadcasted_iota(jnp.int32, (ncmp, 1), 0)
    vis = (ci * CMP_STRIDE + (CMP_BLOCK - 1) <= tq) & (ci < ncmp - 1)
    s = _dot_nt(kc, q4) + heads(jnp.where(vis, 0.0, MASKED))
    m = jnp.maximum(jnp.max(s, axis=0, keepdims=True), ROW_FLOOR)
    e = jnp.exp(s - m)
    den = jnp.sum(e, axis=0, keepdims=True)
    pc = e * (1.0 / jnp.where(den > 0, den, 1.0))
    o_cmp = _dot(vct_ref[0], pc.astype(BF16))
    pcsum = pc[:, 0:C]
    for r in range(1, R):
        pcsum = pcsum + pc[:, r * C:(r + 1) * C]
    ov = ov_ref[...]
    p_slc = sum(_dot(ov, part) for part in _split3(pcsum))

    def values_t(which, first_block, n):
        return jnp.concatenate([vt_ref[0, which, first_block + i] for i in range(n)], axis=1)

    span = WIN_SIZE + C
    ws = pl.multiple_of(jnp.maximum(t0 - WIN_SIZE, 0), C)
    k_w = kw_ref[0, pl.ds(ws, span), :] + kpa_ref[pl.ds(ws, span), :]
    wpos = ws + lax.broadcasted_iota(jnp.int32, (span, 1), 0)
    wbias = jnp.where((wpos <= tq) & (wpos >= tq - (WIN_SIZE - 1)), 0.0, MASKED)
    s3 = _dot_nt(k_w, q4) + heads(wbias)
    e3 = jnp.exp(s3 - jnp.max(s3, axis=0, keepdims=True))
    ow = _dot(values_t(1, ws // LANE, span // LANE), e3.astype(BF16))
    o_win = ow / ow[SUM_LANE:SUM_LANE + 1, :]

    j = lax.broadcasted_iota(jnp.int32, (LANE, C), 0)
    cur = tq // SLC_BLOCK
    forced = (j == 0) | (j == cur) | (j == cur - 1)
    future = j * SLC_BLOCK > tq
    work = jnp.where(forced, -jnp.inf, jnp.where(future, -SLC_FORCE, p_slc))
    work = jnp.where(j < nb, work, -jnp.inf)
    sel = jnp.where(forced, 1.0, 0.0)
    jf = j.astype(F32)
    for _ in range(n_sel - 3):
        mx = jnp.max(work, axis=0, keepdims=True)
        first = jnp.min(jnp.where(work == mx, jf, float(LANE)), axis=0, keepdims=True)
        pick = jf == first
        sel = jnp.where(pick, 1.0, sel)
        work = jnp.where(pick, -jnp.inf, work)
    sel = jnp.where(future, 0.0, sel)
    blocks_per_tile = TK // SLC_BLOCK
    for kt in range(seq // TK):
        flag_ref[kt] = jnp.max(sel[kt * blocks_per_tile:(kt + 1) * blocks_per_tile, :])
    sel_bias = jnp.where(sel.T > 0.5, 0.0, MASKED).astype(BF16)
    q_aug = jnp.concatenate([q4, jnp.concatenate([sel_bias] * R, axis=0)], axis=1)

    m_ref[...] = jnp.full((1, rows), ROW_FLOOR, F32)
    acc_ref[...] = jnp.zeros((LANE, rows), F32)
    kio = lax.broadcasted_iota(jnp.int32, (TK, 1), 0)

    def tile_update(kt, causal):
        start = pl.multiple_of(kt * TK, TK)
        s2 = _dot_nt(kaug_ref[pl.ds(start, TK), :], q_aug)
        if causal:
            s2 = s2 + heads(jnp.where(start + kio <= tq, 0.0, MASKED))
        m_i = m_ref[...]
        m_n = jnp.maximum(m_i, jnp.max(s2, axis=0, keepdims=True))
        alpha = jnp.exp(m_i - m_n)
        p2 = jnp.exp(s2 - m_n)
        acc_ref[...] = alpha * acc_ref[...] + _dot(values_t(0, kt * vblocks, vblocks), p2.astype(BF16))
        m_ref[...] = m_n

    def slc_body(kt, carry):
        @pl.when(flag_ref[kt] > 0.0)
        def _():
            tile_update(kt, False)

        return carry

    last = (t0 + C - 1) // TK
    lax.fori_loop(0, last, slc_body, 0)
    tile_update(last, True)
    acc = acc_ref[...]
    o_slc = acc / acc[SUM_LANE:SUM_LANE + 1, :]

    gates = jax.nn.sigmoid(gt_ref[0]).T
    for r in range(R):
        rs = slice(r * C, (r + 1) * C)
        out = (gates[r:r + 1] * o_cmp[:, rs] + gates[R + r:R + r + 1] * o_slc[:, rs]
               + gates[2 * R + r:2 * R + r + 1] * o_win[:, rs])
        o_ref[0, :, r * LANE:(r + 1) * LANE] = out.T.astype(o_ref.dtype)


def _nsa(slopes_b, pa3, gt3, kc, vct, vt, b, s, col0, gcol0):
    C = NSA_Q
    R = NSA_REP
    G = NSA_KV_HEADS
    nb = s // SLC_BLOCK
    ncmp = s // CMP_STRIDE
    assert nb <= LANE
    qc0 = col0 // R
    kv0 = col0 + NSA_HEADS
    rows = R * C

    ci = np.arange(ncmp)[None, :]
    sj = np.arange(LANE)[:, None]
    ov = ((ci * CMP_STRIDE < (sj + 1) * SLC_BLOCK) & (ci * CMP_STRIDE + CMP_BLOCK > sj * SLC_BLOCK)
          & (ci < ncmp - 1) & (sj < nb))
    ov = jnp.asarray(ov, BF16)
    kpa = _pos_columns(np.arange(s))
    cpa = _pos_columns(np.arange(ncmp) * CMP_STRIDE + (CMP_BLOCK - 1))
    oh = jnp.asarray(np.arange(s)[:, None] // SLC_BLOCK == np.arange(LANE)[None, :], BF16)

    def k_spec(which):
        return pl.BlockSpec((1, s, LANE), lambda bb, g, c: (bb, 0, kv0 + which * G + g))

    const = lambda a: pl.BlockSpec(a.shape, lambda bb, g, c: (0,) * a.ndim)
    return pl.pallas_call(
        functools.partial(_nsa_kernel, seq=s),
        grid=(b, G, s // C),
        in_specs=[
            pl.BlockSpec(memory_space=pltpu.SMEM),
            pl.BlockSpec((1, C, R * LANE), lambda bb, g, c: (bb, c, qc0 + g)),
            pl.BlockSpec((1, ncmp, LANE), lambda bb, g, c: (bb * G + g, 0, 0)),
            pl.BlockSpec((1, LANE, ncmp), lambda bb, g, c: (bb * G + g, 0, 0)),
            k_spec(2), k_spec(4),
            pl.BlockSpec((1, 2, s // LANE, LANE, LANE), lambda bb, g, c: (bb * G + g, 0, 0, 0, 0)),
            pl.BlockSpec((1, C, LANE), lambda bb, g, c: (bb, c, gcol0 + g)),
            const(ov), const(kpa), const(cpa), const(oh),
        ],
        out_specs=pl.BlockSpec((1, C, R * LANE), lambda bb, g, c: (bb, c, g)),
        out_shape=jax.ShapeDtypeStruct((b, s, NSA_HEADS * LANE), BF16),
        scratch_shapes=[
            pltpu.VMEM((s, 2 * LANE), BF16),
            pltpu.SMEM((s // NSA_TK,), F32),
            pltpu.VMEM((1, rows), F32),
            pltpu.VMEM((LANE, rows), F32),
        ],
        compiler_params=_params(("parallel", "parallel", "arbitrary")),
    )(slopes_b, pa3, kc, vct, pa3, pa3, vt, gt3, ov, kpa, cpa, oh)


def _merge_kernel(o1, o2, o3, l1, l2, l3, yb, ga, gb, x_ref, wa, wb, wo, gf, wrh, wrl,
                  h_ref, xn_ref, lg_ref, xr_ref):
    def pairs(ref):
        return jnp.concatenate([ref[h] for h in range(ref.shape[0])], axis=1)

    a1, a2, a3 = pairs(l1), pairs(l2), pairs(l3)
    mx = jnp.maximum(jnp.maximum(a1, a2), a3)
    e1, e2, e3 = jnp.exp(a1 - mx), jnp.exp(a2 - mx), jnp.exp(a3 - mx)
    tot = e1 + e2 + e3
    ya = (e1 / tot) * pairs(o1) + (e2 / tot) * pairs(o2) + (e3 / tot) * pairs(o3)
    merged = (jax.nn.sigmoid(ga[...]) * _dot(ya.astype(BF16), wa[...])
              + jax.nn.sigmoid(gb[...]) * _dot(yb[...], wb[...]))
    h = x_ref[...] + _dot(merged.astype(BF16), wo[...])
    h_ref[...] = h
    xn = _rms(h, gf[...])
    xn_ref[...] = xn.astype(BF16)
    tm, d = xn.shape
    half = d // (2 * LANE)

    for c in range(half):
        xr_ref[pl.ds(c, tm, stride=half), :] = _pack_bf16_pairs(
            xn[:, c * LANE:(c + 1) * LANE], xn[:, (c + half) * LANE:(c + half + 1) * LANE])
    xh = xn.astype(BF16)
    xl = (xn - xh.astype(F32)).astype(BF16)
    lg_ref[...] = _dot_nt(wrh[...], xh) + _dot_nt(wrh[...], xl) + _dot_nt(wrl[...], xh)


def _merge(o_list, l_list, yb, gt, x2, wa, wb, wo, gf, wrh, wrl):
    n, d = x2.shape
    aw = wa.shape[0]
    bw = wb.shape[0]
    tm = ROW_TILE
    row = lambda w: pl.BlockSpec((tm, w), lambda i: (i, 0))
    full = lambda a: pl.BlockSpec(a.shape, lambda i: (0,) * a.ndim)
    return pl.pallas_call(
        _merge_kernel,
        grid=(n // tm,),
        in_specs=[pl.BlockSpec((aw // LANE, tm, LANE), lambda i: (0, i, 0))] * 6 + [
            row(bw),
            pl.BlockSpec((tm, d), lambda i: (i, 0)),
            pl.BlockSpec((tm, d), lambda i: (i, 1)),
            row(d), full(wa), full(wb), full(wo), full(gf), full(wrh), full(wrl),
        ],
        out_specs=[row(d), row(d), pl.BlockSpec((N_EXPERTS, tm), lambda i: (0, i)),
                   pl.BlockSpec((tm * d // (2 * LANE), LANE), lambda i: (i, 0))],
        out_shape=[
            jax.ShapeDtypeStruct((n, d), F32),
            jax.ShapeDtypeStruct((n, d), BF16),
            jax.ShapeDtypeStruct((N_EXPERTS, n), F32),
            jax.ShapeDtypeStruct((n * d // (2 * LANE), LANE), jnp.uint32),
        ],
        compiler_params=_params(("parallel",)),
    )(*o_list, *l_list, yb, gt, gt, x2, wa, wb, wo, gf, wrh, wrl)


def _route_kernel(lg_ref, bias_ref, tri_ref, idx_ref, w_ref, rank_ref, cnt_ref, carry_ref):
    E = N_EXPERTS
    tn = lg_ref.shape[1]
    scores = jax.nn.sigmoid(lg_ref[...])
    sel = scores + bias_ref[...]
    sub = lax.broadcasted_iota(jnp.int32, (GROUP_SIZE, tn), 0).astype(F32)
    gs = []
    for grp in range(N_GROUPS):
        blk = sel[grp * GROUP_SIZE:(grp + 1) * GROUP_SIZE]
        m1 = jnp.max(blk, axis=0, keepdims=True)
        first = jnp.min(jnp.where(blk == m1, sub, float(GROUP_SIZE)), axis=0, keepdims=True)
        m2 = jnp.max(jnp.where(sub == first, -jnp.inf, blk), axis=0, keepdims=True)
        gs.append(m1 + m2)
    keep = []
    for a in range(N_GROUPS):
        rank = jnp.zeros((1, tn), jnp.int32)
        for o in range(N_GROUPS):
            if o == a:
                continue
            beats = (gs[o] >= gs[a]) if o < a else (gs[o] > gs[a])
            rank = rank + beats.astype(jnp.int32)
        keep.append(jnp.broadcast_to(rank < TOP_GROUPS, (GROUP_SIZE, tn)))
    emask = jnp.concatenate(keep, axis=0)
    cand = jnp.where(emask, sel, -jnp.inf)
    eid = lax.broadcasted_iota(jnp.int32, (E, tn), 0)
    eid_f = eid.astype(F32)
    rank = jnp.zeros((E, tn), jnp.int32)
    for o in range(E):
        other = cand[o:o + 1]
        beats = (other > cand) | ((other == cand) & (eid > o))
        rank = rank + beats.astype(jnp.int32)
    chosen = emask & (rank < TOP_K)
    wsel = jnp.where(chosen, scores, 0.0)
    wnorm = wsel / jnp.sum(wsel, axis=0, keepdims=True) * ROUTED_SCALE

    @pl.when(pl.program_id(0) == 0)
    def _():
        carry_ref[...] = jnp.zeros_like(carry_ref)

    chosen_f = jnp.where(chosen, 1.0, 0.0)
    before = carry_ref[...] + _dot(chosen_f.astype(BF16), tri_ref[...]) - chosen_f
    carry_ref[...] = carry_ref[...] + jnp.sum(chosen_f, axis=1, keepdims=True)
    cnt_ref[...] = jnp.broadcast_to(carry_ref[...], cnt_ref.shape).astype(jnp.int32)
    for k in range(TOP_K):
        hit = chosen & (rank == k)
        idx_ref[k:k + 1, :] = jnp.sum(jnp.where(hit, eid_f, 0.0), axis=0, keepdims=True).astype(jnp.int32)
        w_ref[k:k + 1, :] = jnp.sum(jnp.where(hit, wnorm, 0.0), axis=0, keepdims=True)
        rank_ref[k:k + 1, :] = jnp.sum(jnp.where(hit, before, 0.0), axis=0, keepdims=True).astype(jnp.int32)


def _route(logits_t, bias_col):
    e, n = logits_t.shape
    tn = ROW_TILE
    tri = jnp.asarray(np.triu(np.ones((tn, tn), np.float32)), BF16)
    slot = pl.BlockSpec((TOP_K, tn), lambda i: (0, i))
    return pl.pallas_call(
        _route_kernel,
        grid=(n // tn,),
        in_specs=[pl.BlockSpec((e, tn), lambda i: (0, i)), pl.BlockSpec((e, 1), lambda i: (0, 0)),
                  pl.BlockSpec((tn, tn), lambda i: (0, 0))],
        out_specs=[slot, slot, slot, pl.BlockSpec((e, LANE), lambda i: (0, 0))],
        out_shape=[
            jax.ShapeDtypeStruct((TOP_K, n), jnp.int32),
            jax.ShapeDtypeStruct((TOP_K, n), F32),
            jax.ShapeDtypeStruct((TOP_K, n), jnp.int32),
            jax.ShapeDtypeStruct((e, LANE), jnp.int32),
        ],
        scratch_shapes=[pltpu.VMEM((e, 1), F32)],
        compiler_params=_params(("arbitrary",)),
    )(logits_t, bias_col, tri)


def _dispatch_kernel(fill_lo_ref, fill_hi_ref, nused_ref, pos_ref, x_ref, xs_ref, zero_ref, sem, zsem):
    i = pl.program_id(0)
    tm = pos_ref.shape[1]
    sub = x_ref.shape[0] // tm
    blk_rows = zero_ref.shape[0]
    nblk = xs_ref.shape[0] // blk_rows

    def slot(r):
        return xs_ref.at[pl.ds(pl.multiple_of(r * sub, sub), sub), :]

    @pl.when(i == 0)
    def _():
        zero_ref[...] = jnp.zeros_like(zero_ref)

        def per_expert(e, carry):
            def fill(r, c2):
                pltpu.make_async_copy(zero_ref.at[pl.ds(0, sub), :], slot(r), zsem).start()
                return c2

            def drain(r, c2):
                pltpu.make_async_copy(zero_ref.at[pl.ds(0, sub), :], slot(r), zsem).wait()
                return c2

            lax.fori_loop(fill_lo_ref[e], fill_hi_ref[e], fill, 0)
            lax.fori_loop(fill_lo_ref[e], fill_hi_ref[e], drain, 0)
            return carry

        lax.fori_loop(0, fill_lo_ref.shape[0], per_expert, 0)

        def tail(bk, carry):
            cp = pltpu.make_async_copy(
                zero_ref, xs_ref.at[pl.ds(pl.multiple_of(bk * blk_rows, blk_rows), blk_rows), :], zsem)
            cp.start()
            cp.wait()
            return carry

        lax.fori_loop(nused_ref[0], nblk, tail, 0)

    def scatter(t, carry):
        src = x_ref.at[pl.ds(pl.multiple_of(t * sub, sub), sub), :]
        for k in range(TOP_K):
            pltpu.make_async_copy(src, slot(pos_ref[k, t]), sem).start()
        return carry

    lax.fori_loop(0, tm, scatter, 0)
    for k in range(TOP_K):
        pltpu.make_async_copy(x_ref, xs_ref.at[pl.ds(0, tm * sub), :], sem).wait()


def _dispatch(fill_lo, fill_hi, nused, pos_t, xr, p_rows, d):
    k, n = pos_t.shape
    sub = d // (2 * LANE)
    tm = ROW_TILE
    grid_spec = pltpu.PrefetchScalarGridSpec(
        num_scalar_prefetch=3,
        grid=(n // tm,),
        in_specs=[
            pl.BlockSpec((k, tm), lambda i, *_: (0, i), memory_space=pltpu.SMEM),
            pl.BlockSpec((tm * sub, LANE), lambda i, *_: (i, 0)),
        ],
        out_specs=pl.BlockSpec(memory_space=pl.ANY),
        scratch_shapes=[pltpu.VMEM((MOE_ROWS * sub, LANE), xr.dtype), pltpu.SemaphoreType.DMA(()),
                        pltpu.SemaphoreType.DMA(())],
    )
    return pl.pallas_call(
        _dispatch_kernel,
        grid_spec=grid_spec,
        out_shape=jax.ShapeDtypeStruct((p_rows * sub, LANE), xr.dtype),
        compiler_params=_params(("arbitrary",)),
    )(fill_lo, fill_hi, nused, pos_t, xr)


def _dispatch_sc(pos_t, xr3, p_rows):
    k, n = pos_t.shape
    _, sub, lane = xr3.shape
    nk = k * n
    workers = SC_CORES * SC_SUBCORES
    per_worker = nk // workers
    assert nk % (workers * SC_WINDOW) == 0 and n % SC_WINDOW == 0
    mesh = plsc.VectorSubcoreMesh(core_axis_name="c", subcore_axis_name="s")

    @functools.partial(
        pl.kernel, mesh=mesh,
        out_type=jax.ShapeDtypeStruct((p_rows, sub, lane), xr3.dtype),
        scratch_types=[pltpu.VMEM((SC_WINDOW,), jnp.int32), pltpu.VMEM((SC_WINDOW, sub, lane), xr3.dtype),
                       pltpu.SemaphoreType.DMA],
    )
    def scatter_rows(x_hbm, idx_hbm, o_hbm, idx_v, rows_v, sem):
        wid = lax.axis_index("s") * SC_CORES + lax.axis_index("c")

        @pl.loop(0, per_worker // SC_WINDOW)
        def _(c):
            base = wid * per_worker + c * SC_WINDOW
            tok0 = lax.rem(base, n)
            pltpu.sync_copy(idx_hbm.at[pl.ds(base, SC_WINDOW)], idx_v)
            pltpu.sync_copy(x_hbm.at[pl.ds(tok0, SC_WINDOW)], rows_v)
            pltpu.async_copy(rows_v, o_hbm.at[idx_v], sem).wait()

    return scatter_rows(xr3, pos_t.reshape(nk))


def _combine_sc(pos_t, ys3):
    k, n = pos_t.shape
    _, sub, lane = ys3.shape
    nk = k * n
    workers = SC_CORES * SC_SUBCORES
    per_worker = nk // workers
    assert nk % (workers * SC_WINDOW) == 0
    mesh = plsc.VectorSubcoreMesh(core_axis_name="c", subcore_axis_name="s")

    @functools.partial(
        pl.kernel, mesh=mesh,
        out_type=jax.ShapeDtypeStruct((nk, sub, lane), ys3.dtype),
        scratch_types=[pltpu.VMEM((SC_WINDOW,), jnp.int32), pltpu.VMEM((SC_WINDOW, sub, lane), ys3.dtype),
                       pltpu.SemaphoreType.DMA],
    )
    def gather_rows(y_hbm, idx_hbm, o_hbm, idx_v, rows_v, sem):
        wid = lax.axis_index("s") * SC_CORES + lax.axis_index("c")

        @pl.loop(0, per_worker // SC_WINDOW)
        def _(c):
            base = wid * per_worker + c * SC_WINDOW
            pltpu.sync_copy(idx_hbm.at[pl.ds(base, SC_WINDOW)], idx_v)
            pltpu.async_copy(y_hbm.at[idx_v], rows_v, sem).wait()
            pltpu.sync_copy(rows_v, o_hbm.at[pl.ds(base, SC_WINDOW)])

    return gather_rows(ys3, pos_t.reshape(nk))


def _moe_kernel(blk_e_ref, nused_ref, nvalid_ref, xs_ref, wg_ref, wu_ref, wd_ref, o_ref):
    i = pl.program_id(0)

    @pl.when(i < nused_ref[0])
    def _():
        sub = xs_ref.shape[0] // MOE_ROWS
        owned = lax.broadcasted_iota(jnp.int32, (MOE_ROWS, LANE), 0) < nvalid_ref[i]
        words = [jnp.where(owned, xs_ref[pl.ds(c, MOE_ROWS, stride=sub), :], jnp.uint32(0)) for c in range(sub)]
        halves = [_unpack_bf16_pairs(w) for w in words]
        x = jnp.concatenate([lo for lo, _ in halves] + [hi for _, hi in halves], axis=1).astype(BF16)
        hg = _dot(x, wg_ref[0])
        hid = (hg * jax.nn.sigmoid(hg)) * _dot(x, wu_ref[0])
        y = _dot(hid.astype(BF16), wd_ref[0])
        for c in range(sub):
            o_ref[pl.ds(c, MOE_ROWS, stride=sub), :] = _pack_bf16_pairs(
                y[:, c * LANE:(c + 1) * LANE], y[:, (c + sub) * LANE:(c + sub + 1) * LANE])

    @pl.when(i >= nused_ref[0])
    def _():
        o_ref[...] = jnp.zeros_like(o_ref)


def _moe_experts(blk_e, nused, nvalid, xs, wg, wu, wd):
    d = wg.shape[1]
    sub = d // (2 * LANE)
    p = xs.shape[0] // sub
    hid = wg.shape[2]
    grid_spec = pltpu.PrefetchScalarGridSpec(
        num_scalar_prefetch=3,
        grid=(p // MOE_ROWS,),
        in_specs=[
            pl.BlockSpec((MOE_ROWS * sub, LANE), lambda i, be, nu, nv: (jnp.minimum(i, nu[0] - 1), 0)),
            pl.BlockSpec((1, d, hid), lambda i, be, nu, nv: (be[i], 0, 0)),
            pl.BlockSpec((1, d, hid), lambda i, be, nu, nv: (be[i], 0, 0)),
            pl.BlockSpec((1, hid, d), lambda i, be, nu, nv: (be[i], 0, 0)),
        ],
        out_specs=pl.BlockSpec((MOE_ROWS * sub, LANE), lambda i, be, nu, nv: (i, 0)),
    )
    return pl.pallas_call(
        _moe_kernel,
        grid_spec=grid_spec,
        out_shape=jax.ShapeDtypeStruct((p * sub, LANE), jnp.uint32),
        compiler_params=_params(("arbitrary",)),
    )(blk_e, nused, nvalid, xs, wg, wu, wd)


def _final_kernel(h_ref, xn_ref, ys_ref, w_ref, p_ref, wsg, wsu, wsd, gp, wpg, wp, gfin, o_ref):
    xn = xn_ref[...]
    hg = _dot(xn, wsg[...])
    shared = _dot(((hg * jax.nn.sigmoid(hg)) * _dot(xn, wsu[...])).astype(BF16), wsd[...])
    w = w_ref[...]
    tm = w.shape[0]
    sub = ys_ref.shape[1] // tm
    parts = [None] * (2 * sub)
    for k in range(TOP_K):
        wk = jnp.broadcast_to(w[:, k:k + 1], (tm, LANE))
        for c in range(sub):
            lo, hi = _unpack_bf16_pairs(ys_ref[k, pl.ds(c, tm, stride=sub), :])
            parts[c] = wk * lo if k == 0 else parts[c] + wk * lo
            parts[c + sub] = wk * hi if k == 0 else parts[c + sub] + wk * hi
    y = jnp.concatenate(parts, axis=1)
    h = h_ref[...] + (y + shared)
    gate = jax.nn.sigmoid(_dot(_rms(h, gp[...]).astype(BF16), wpg[...]))
    h = h + _dot(p_ref[...].astype(BF16), wp[...]) * gate
    o_ref[...] = _rms(h, gfin[...])


def _final(h1, xn2, ysg, w_nk, p2, wsg, wsu, wsd, gp, wpg, wp, gfin):
    n, d = h1.shape
    tm = ROW_TILE
    row = lambda w: pl.BlockSpec((tm, w), lambda i: (i, 0))
    full = lambda a: pl.BlockSpec(a.shape, lambda i: (0,) * a.ndim)
    return pl.pallas_call(
        _final_kernel,
        grid=(n // tm,),
        in_specs=[row(d), row(d), pl.BlockSpec((TOP_K, tm * ysg.shape[1] // n, LANE), lambda i: (0, i, 0)),
                  row(TOP_K), row(p2.shape[1]),
                  full(wsg), full(wsu), full(wsd), full(gp), full(wpg), full(wp), full(gfin)],
        out_specs=row(d),
        out_shape=jax.ShapeDtypeStruct((n, d), F32),
        compiler_params=_params(("parallel",)),
    )(h1, xn2, ysg, w_nk, p2, wsg, wsu, wsd, gp, wpg, wp, gfin)


def _pad_heads(w):
    d, c = w.shape
    nh = c // HEAD_DIM
    w = w.reshape(d, nh, HEAD_DIM)
    return jnp.pad(w, ((0, 0), (0, 0), (0, LANE - HEAD_DIM))).reshape(d, nh * LANE)


def _alibi_slopes(n):
    return 2.0 ** (-8.0 * jnp.arange(1, n + 1, dtype=jnp.float32) / n)


def kernel(x, p, g_mix, w_in, cmp_pos_k, cmp_w1_k, cmp_b1_k, cmp_w2_k, cmp_pos_v, cmp_w1_v, cmp_b1_v, cmp_w2_v, w_branch_a, w_branch_b, w_out, g_ffn, w_router, router_bias, w_gate, w_up, w_down, ws_gate, ws_up, ws_down, g_ple, w_ple_gate, w_ple, g_final):
    B, S, D = x.shape
    N = B * S
    G, R = NSA_KV_HEADS, NSA_REP
    assert w_in.shape[0] == 1 and p.shape[0] == 1
    assert S % DIL_STEP_ROWS == 0 and S % NSA_TK == 0 and S >= WIN_SIZE + NSA_Q
    assert N % ROW_TILE == 0 and N % INPROJ_ROWS == 0 and S % INPROJ_ROWS == 0
    slopes = _alibi_slopes(DIL_HEADS + NSA_HEADS)
    slopes_a, slopes_b = slopes[0::2], slopes[1::2]

    offs = np.concatenate([[0], np.cumsum(IN_SIZES)])
    parts = [w_in[0][:, offs[i]:offs[i + 1]] for i in range(len(IN_SIZES))]
    w_qa, w_ka, w_va, w_qb, w_kc, w_vc, w_ks, w_vs, w_kw, w_vw, w_gn, w_ga, w_gb = parts
    w_qkv_a = jnp.concatenate([w_qa * HEAD_DIM ** -0.5, w_ka, w_va], axis=1).astype(BF16)
    w_nsa = jnp.concatenate([_pad_heads(t) for t in (w_qb, w_kc, w_vc, w_ks, w_vs, w_kw, w_vw)],
                            axis=1).astype(BF16)
    nsa_w = w_nsa.shape[1]
    w_gn = w_gn.reshape(D, 3, G, R).transpose(0, 2, 1, 3).reshape(D, G, 3 * R)
    w_gn = jnp.pad(w_gn, ((0, 0), (0, 0), (0, LANE - 3 * R))).reshape(D, G * LANE)
    w_gate_in = jnp.concatenate([w_ga, w_gb, w_gn], axis=1).astype(BF16)
    gate_w = w_gate_in.shape[1]

    x2 = x.reshape(N, D)
    dils = tuple(dil for _, dil in DIL_PATTERNS if dil > 1)
    qkv_a, *dilated_qkv = _inproj(x2, g_mix, w_qkv_a, BF16, 512, dils=dils)
    nsa_in, = _inproj(x2, g_mix, w_nsa, BF16, 512)
    gt, = _inproj(x2, g_mix, w_gate_in, F32, 768)

    o_list, l_list = [], []
    for _, dil in DIL_PATTERNS:
        src = qkv_a if dil == 1 else dilated_qkv[dils.index(dil)]
        o, l = _dilated(src.reshape(B, S // dil, -1), _dil_bias(slopes_a, dil), dil, B, S)
        o_list.append(o.reshape(-1, N, LANE))
        l_list.append(l.reshape(-1, N, LANE))

    nchunk = S // CMP_STRIDE
    c0 = NSA_HEADS * LANE
    raw = nsa_in[:, c0:c0 + 2 * G * LANE].reshape(B, nchunk, CMP_STRIDE, 2 * G, LANE)[..., :HEAD_DIM]
    xc = raw.transpose(0, 3, 1, 2, 4).reshape(B, 2 * G, nchunk, CMP_STRIDE * HEAD_DIM)
    pos = jnp.stack([cmp_pos_k[0].reshape(1, -1), cmp_pos_v[0].reshape(1, -1)])
    w1 = jnp.stack([cmp_w1_k[0], cmp_w1_v[0]]).astype(BF16)
    b1 = jnp.stack([cmp_b1_k[0][None], cmp_b1_v[0][None]])
    w2 = jnp.pad(jnp.stack([cmp_w2_k[0], cmp_w2_v[0]]), ((0, 0), (0, 0), (0, LANE - HEAD_DIM))).astype(BF16)
    kcv = _compress(xc, pos, w1, b1, w2)
    kcv = kcv.reshape(B, 2, G, nchunk, LANE).transpose(1, 0, 2, 3, 4).reshape(2, B * G, nchunk, LANE)
    vct = kcv[1].transpose(0, 2, 1)
    v0 = (NSA_HEADS + 3 * G) * LANE
    vsw = jnp.stack([nsa_in[:, v0:v0 + G * LANE], nsa_in[:, v0 + 2 * G * LANE:v0 + 3 * G * LANE]])
    vsw = vsw.reshape(2, B, S // LANE, LANE, G, LANE).transpose(1, 4, 0, 2, 5, 3)
    ones_row = (jnp.arange(LANE) == SUM_LANE).astype(BF16)[:, None]
    vt = (vsw + ones_row).reshape(B * G, 2, S // LANE, LANE, LANE)
    yb = _nsa(slopes_b, nsa_in.reshape(B, S, nsa_w), gt.reshape(B, S, gate_w), kcv[0], vct, vt,
              B, S, 0, 2 * D // LANE)
    yb = yb.reshape(N, NSA_HEADS * LANE)

    wb_pad = jnp.pad(w_branch_b[0].reshape(NSA_HEADS, HEAD_DIM, D),
                     ((0, 0), (0, LANE - HEAD_DIM), (0, 0))).reshape(NSA_HEADS * LANE, D)
    wr_t = w_router[0].T
    wr_hi = wr_t.astype(BF16)
    wr_lo = (wr_t - wr_hi.astype(F32)).astype(BF16)
    h1, xn2, logits_t, xr = _merge(o_list, l_list, yb, gt, x2, w_branch_a[0].astype(BF16), wb_pad.astype(BF16),
                               w_out[0].astype(BF16), g_ffn, wr_hi, wr_lo)

    idx_t, wsel_t, rank_t, counts = _route(logits_t, router_bias[0][:, None])
    counts = counts[:, 0]
    padded = (counts + MOE_ROWS - 1) // MOE_ROWS * MOE_ROWS
    pend = jnp.cumsum(padded)
    pstart = pend - padded
    hot = idx_t[None] == jnp.arange(N_EXPERTS, dtype=jnp.int32)[:, None, None]
    pos_t = rank_t + jnp.sum(jnp.where(hot, pstart.astype(jnp.int32)[:, None, None], 0), axis=0)
    NK = N * TOP_K
    nblk = (NK + N_EXPERTS * (MOE_ROWS - 1) + MOE_ROWS - 1) // MOE_ROWS
    P = nblk * MOE_ROWS
    blk_e = jnp.minimum(jnp.sum(pend[None, :] <= (jnp.arange(nblk, dtype=jnp.int32) * MOE_ROWS)[:, None], axis=1),
                        N_EXPERTS - 1).astype(jnp.int32)
    nused = (pend[-1] // MOE_ROWS).astype(jnp.int32).reshape(1)
    row0 = jnp.arange(nblk, dtype=jnp.int32) * MOE_ROWS
    nvalid = jnp.clip((pstart + counts)[blk_e] - row0, 0, MOE_ROWS).astype(jnp.int32)
    sub = D // (2 * LANE)
    xs = _dispatch_sc(pos_t, xr.reshape(N, sub, LANE), P).reshape(P * sub, LANE)
    ys = _moe_experts(blk_e, nused, nvalid, xs, w_gate[0].astype(BF16), w_up[0].astype(BF16), w_down[0].astype(BF16))
    ysg = _combine_sc(pos_t, ys.reshape(P, sub, LANE)).reshape(TOP_K, N * sub, LANE)
    out = _final(h1, xn2, ysg, wsel_t.T, p[0].reshape(N, -1), ws_gate[0].astype(BF16), ws_up[0].astype(BF16),
                 ws_down[0].astype(BF16), g_ple, w_ple_gate[0].astype(BF16), w_ple[0].astype(BF16),
                 g_final[None, :])
    return out.reshape(B, S, D)
```

```python
import functools

import numpy as np
import jax
import jax.numpy as jnp
from jax import lax
from jax.experimental import pallas as pl
from jax.experimental.pallas import tpu as pltpu
from jax.experimental.pallas import tpu_sc as plsc

F32 = jnp.float32
BF16 = jnp.bfloat16

HEAD_DIM = 64
LANE = 128
RMS_EPS = 1e-6
DIL_HEADS = 8
DIL_PATTERNS = ((128, 1), (512, 4), (2048, 16))
DIL_BLOCK = 128
DIL_STEP_ROWS = 4096
NSA_HEADS = 8
NSA_KV_HEADS = 2
NSA_REP = NSA_HEADS // NSA_KV_HEADS
CMP_BLOCK = 32
CMP_STRIDE = 16
CMP_HIDDEN = 256
SLC_BLOCK = 64
SLC_TOP_N = 16
SLC_FORCE = 1e9
WIN_SIZE = 512
N_EXPERTS = 64
TOP_K = 8
N_GROUPS = 8
TOP_GROUPS = 4
GROUP_SIZE = N_EXPERTS // N_GROUPS
ROUTED_SCALE = 2.5
IN_SIZES = (512, 512, 512, 512, 128, 128, 128, 128, 128, 128, 24, 1024, 1024)

NEG = -1e30
NT_DIMS = (((1,), (1,)), ((), ()))

ROW_TILE = 512
INPROJ_ROWS = 1024
NSA_Q = 256
NSA_TK = 512
MOE_ROWS = 512
VMEM_LIMIT = 48 * 1024 * 1024
SC_CORES = 2
SC_SUBCORES = 16
SC_WINDOW = 128


def _dot(a, b):
    return jnp.dot(a, b, preferred_element_type=F32)


def _dot_nt(a, b):
    return lax.dot_general(a, b, NT_DIMS, preferred_element_type=F32)


def _split3(x):
    hi = x.astype(BF16)
    r1 = x - hi.astype(F32)
    mid = r1.astype(BF16)
    lo = (r1 - mid.astype(F32)).astype(BF16)
    return hi, mid, lo


def _pack_bf16_pairs(lo, hi):
    lo_bits = pltpu.bitcast(lo.astype(BF16).astype(F32), jnp.uint32)
    hi_bits = pltpu.bitcast(hi.astype(BF16).astype(F32), jnp.uint32)
    return hi_bits | (lo_bits >> 16)


def _unpack_bf16_pairs(words):
    return (pltpu.bitcast(words << 16, F32), pltpu.bitcast(words & jnp.uint32(0xFFFF0000), F32))


def _rms(x, g):
    r = lax.rsqrt(jnp.mean(x * x, axis=-1, keepdims=True) + RMS_EPS)
    return (x * r) * g


def _params(sem):
    return pltpu.CompilerParams(dimension_semantics=sem, vmem_limit_bytes=VMEM_LIMIT)


def _inproj_kernel(x_ref, g_ref, w_ref, o_ref, *rest, dils):
    dil_refs, xn_ref = rest[:len(dils)], rest[len(dils)]
    res_ref = rest[len(dils) + 1] if dils else None

    @pl.when(pl.program_id(1) == 0)
    def _():
        xn_ref[...] = _rms(x_ref[...], g_ref[...]).astype(BF16)

    res = _dot(xn_ref[...], w_ref[...])
    o_ref[...] = res.astype(o_ref.dtype)

    if dils:
        tn = res.shape[1]
        for c in range(tn // LANE):
            res_ref[c] = res[:, c * LANE:(c + 1) * LANE]
        for d_ref, dil in zip(dil_refs, dils):
            for r in range(dil):
                for c in range(tn // LANE):
                    piece = res_ref[c, pl.ds(r, INPROJ_ROWS // dil, stride=dil), :]
                    d_ref[:, r * tn + c * LANE:r * tn + (c + 1) * LANE] = piece.astype(d_ref.dtype)


def _inproj(x2, g, w, out_dtype, tn, dils=()):
    n, d = x2.shape
    cols = w.shape[1]
    out_specs = [pl.BlockSpec((INPROJ_ROWS, tn), lambda i, j: (i, j))]
    out_shape = [jax.ShapeDtypeStruct((n, cols), out_dtype)]
    for dil in dils:
        out_specs.append(pl.BlockSpec((INPROJ_ROWS // dil, dil * tn), lambda i, j: (i, j)))
        out_shape.append(jax.ShapeDtypeStruct((n // dil, cols * dil), out_dtype))
    return pl.pallas_call(
        functools.partial(_inproj_kernel, dils=dils),
        grid=(n // INPROJ_ROWS, cols // tn),
        in_specs=[
            pl.BlockSpec((INPROJ_ROWS, d), lambda i, j: (i, 0)),
            pl.BlockSpec((1, d), lambda i, j: (0, 0)),
            pl.BlockSpec((d, tn), lambda i, j: (0, j)),
        ],
        out_specs=out_specs,
        out_shape=out_shape,
        scratch_shapes=[pltpu.VMEM((INPROJ_ROWS, d), BF16)]
        + ([pltpu.VMEM((tn // LANE, INPROJ_ROWS, LANE), F32)] if dils else []),
        compiler_params=_params(("parallel", "arbitrary")),
    )(x2, g, w)


def _dil_kernel(q_ref, kp_ref, kc_ref, vp_ref, vc_ref, bias_ref, o_ref, l_ref, *, dil, nq):
    i = pl.program_id(1)
    res = pl.program_id(2)
    blk = DIL_BLOCK
    lane = lax.broadcasted_iota(jnp.int32, (blk, LANE), 1)
    lo = lane < HEAD_DIM
    for sub in range(nq):
        rows = slice(sub * blk, (sub + 1) * blk)
        prev = slice((sub - 1) * blk, sub * blk)
        out_rows = pl.ds(sub * blk * dil + res, blk, stride=dil) if dil > 1 else rows
        variant = jnp.minimum(i, 1) * DIL_HEADS if sub == 0 else DIL_HEADS
        for hp in range(DIL_HEADS // 2):
            sl = slice(hp * LANE, (hp + 1) * LANE)
            q2 = q_ref[0, rows, sl]
            k_prev = kp_ref[0, :, sl] if sub == 0 else kc_ref[0, prev, sl]
            v_prev = vp_ref[0, :, sl] if sub == 0 else vc_ref[0, prev, sl]
            k2 = jnp.concatenate([k_prev, kc_ref[0, rows, sl]], axis=0)
            v2 = jnp.concatenate([v_prev, vc_ref[0, rows, sl]], axis=0)
            outs, lses = [], []
            for hh in range(2):
                keep = lo if hh == 0 else jnp.logical_not(lo)
                qm = jnp.where(keep, q2, jnp.zeros_like(q2))
                s = _dot_nt(qm, k2) + bias_ref[variant + 2 * hp + hh]
                m = jnp.max(s, axis=-1, keepdims=True)
                e = jnp.exp(s - m)
                den = jnp.sum(e, axis=-1, keepdims=True)
                outs.append(_dot(e.astype(BF16), v2) / den)
                lses.append(jnp.broadcast_to(m + jnp.log(den), (blk, LANE)))
            o_ref[hp, 0, out_rows, :] = jnp.where(lo, outs[0], outs[1])
            l_ref[hp, 0, out_rows, :] = jnp.where(lo, lses[0], lses[1])


def _dil_bias(slopes_a, dil):
    blk = DIL_BLOCK
    qi = jnp.arange(blk)[:, None]
    kj = jnp.arange(2 * blk)[None, :]
    dist = (qi + blk - kj).astype(F32)
    band = (dist >= 0) & (dist <= blk)
    slope = (slopes_a * dil)[:, None, None]
    bias = -slope * dist
    first = jnp.where(band & (kj >= blk), bias, NEG)
    rest = jnp.where(band, bias, NEG)
    return jnp.concatenate([first, rest], axis=0)


def _dilated(src3, bias, dil, b, s):
    nq = min(8, DIL_STEP_ROWS // (DIL_BLOCK * dil))
    nstep = s // dil // (DIL_BLOCK * nq)
    aw = DIL_HEADS * HEAD_DIM
    npair = DIL_HEADS // 2

    def spec(which, prev):
        if prev:
            return pl.BlockSpec((1, DIL_BLOCK, aw), lambda bb, i, r: (bb, jnp.maximum(nq * i - 1, 0), which * dil + r))
        return pl.BlockSpec((1, DIL_BLOCK * nq, aw), lambda bb, i, r: (bb, i, which * dil + r))

    out_spec = pl.BlockSpec((npair, 1, DIL_BLOCK * nq * dil, LANE), lambda bb, i, r: (0, bb, i, 0))
    return pl.pallas_call(
        functools.partial(_dil_kernel, dil=dil, nq=nq),
        grid=(b, nstep, dil),
        in_specs=[
            spec(0, False), spec(1, True), spec(1, False), spec(2, True), spec(2, False),
            pl.BlockSpec(bias.shape, lambda bb, i, r: (0, 0, 0)),
        ],
        out_specs=[out_spec, out_spec],
        out_shape=[jax.ShapeDtypeStruct((npair, b, s, LANE), F32)] * 2,
        compiler_params=_params(("parallel", "arbitrary", "arbitrary")),
    )(src3, src3, src3, src3, src3, bias)


def _cmp_kernel(x_ref, pos_ref, w1_ref, b1_ref, w2_ref, o_ref):
    half = x_ref.shape[-1]
    x = x_ref[0, 0]
    w1 = w1_ref[0]
    first = _dot(x, w1[:half])
    second = _dot(x, w1[half:])
    nrow = x.shape[0]
    pre = first + pltpu.roll(second, nrow - 1, 0)
    posb = _dot(jnp.broadcast_to(pos_ref[0], (8, 2 * half)).astype(BF16), w1)[0:1]
    pre = pre + posb + b1_ref[0]
    o_ref[0, 0] = _dot(jax.nn.gelu(pre).astype(BF16), w2_ref[0]).astype(o_ref.dtype)


def _compress(xc, pos, w1, b1, w2):
    b, four, nchunk, width = xc.shape
    return pl.pallas_call(
        _cmp_kernel,
        grid=(b, four),
        in_specs=[
            pl.BlockSpec((1, 1, nchunk, width), lambda bb, j: (bb, j, 0, 0)),
            pl.BlockSpec((1, 1, 2 * width), lambda bb, j: (j // 2, 0, 0)),
            pl.BlockSpec((1, 2 * width, CMP_HIDDEN), lambda bb, j: (j // 2, 0, 0)),
            pl.BlockSpec((1, 1, CMP_HIDDEN), lambda bb, j: (j // 2, 0, 0)),
            pl.BlockSpec((1, CMP_HIDDEN, LANE), lambda bb, j: (j // 2, 0, 0)),
        ],
        out_specs=pl.BlockSpec((1, 1, nchunk, LANE), lambda bb, j: (bb, j, 0, 0)),
        out_shape=jax.ShapeDtypeStruct((b, four, nchunk, LANE), BF16),
        compiler_params=_params(("parallel", "parallel")),
    )(xc, pos, w1, b1, w2)


POS_LANE = HEAD_DIM
POS_SPLIT = 64
SUM_LANE = HEAD_DIM
MASKED = -1e30
ROW_FLOOR = -1e29


def _pos_columns(pos):
    out = np.zeros((pos.shape[0], LANE), np.float32)
    out[:, POS_LANE] = pos // POS_SPLIT
    out[:, POS_LANE + 1] = pos % POS_SPLIT
    out[:, POS_LANE + 2] = 1.0
    out[:, POS_LANE + 3] = 1.0
    return jnp.asarray(out, BF16)


def _nsa_kernel(slope_ref, q_ref, kc_ref, vct_ref, ks_ref, kw_ref, vt_ref, gt_ref, ov_ref, kpa_ref, cpa_ref,
                oh_ref, o_ref, kaug_ref, flag_ref, list_ref, sbuf_ref, m_ref, acc_ref, *, seq):
    g = pl.program_id(1)
    c = pl.program_id(2)
    C = NSA_Q
    R = NSA_REP
    TK = NSA_TK
    rows = R * C
    t0 = c * C
    scale = HEAD_DIM ** -0.5
    nb = seq // SLC_BLOCK
    ncmp = seq // CMP_STRIDE
    n_sel = min(SLC_TOP_N, nb)
    vblocks = TK // LANE

    @pl.when(c == 0)
    def _():
        kaug_ref[:, 0:LANE] = ks_ref[0] + kpa_ref[...]
        kaug_ref[:, LANE:2 * LANE] = oh_ref[...]

    row = lax.broadcasted_iota(jnp.int32, (rows, 1), 0)
    ti = t0 + row % C
    slope = jnp.zeros((rows, 1), F32)
    for r in range(R):
        slope = jnp.where(row // C == r, slope_ref[g * R + r], slope)
    lane = lax.broadcasted_iota(jnp.int32, (rows, LANE), 1)
    thi = (ti // POS_SPLIT).astype(F32)
    tlo = (ti % POS_SPLIT).astype(F32)
    qcols = jnp.where(lane == POS_LANE, POS_SPLIT * slope,
                      jnp.where(lane == POS_LANE + 1, slope,
                                jnp.where(lane == POS_LANE + 2, -POS_SPLIT * slope * thi,
                                          jnp.where(lane == POS_LANE + 3, -slope * tlo, 0.0))))
    q4 = jnp.concatenate([q_ref[0, :, r * LANE:(r + 1) * LANE] for r in range(R)], axis=0)
    q4 = (q4.astype(F32) * scale + qcols).astype(BF16)
    tq = t0 + lax.broadcasted_iota(jnp.int32, (1, C), 1)

    def heads(x):
        return jnp.concatenate([x] * R, axis=1)

    kc = kc_ref[0] + cpa_ref[...]
    ci = lax.broadcasted_iota(jnp.int32, (ncmp, 1), 0)
    vis = (ci * CMP_STRIDE + (CMP_BLOCK - 1) <= tq) & (ci < ncmp - 1)
    s = _dot_nt(kc, q4) + heads(jnp.where(vis, 0.0, MASKED))
    m = jnp.maximum(jnp.max(s, axis=0, keepdims=True), ROW_FLOOR)
    e = jnp.exp(s - m)
    den = jnp.sum(e, axis=0, keepdims=True)
    pc = e * (1.0 / jnp.where(den > 0, den, 1.0))
    o_cmp = _dot(vct_ref[0], pc.astype(BF16))
    pcsum = pc[:, 0:C]
    for r in range(1, R):
        pcsum = pcsum + pc[:, r * C:(r + 1) * C]
    ov = ov_ref[...]
    p_slc = sum(_dot(ov, part) for part in _split3(pcsum))

    def values_t(which, first_block, n):
        return jnp.concatenate([vt_ref[0, which, first_block + i] for i in range(n)], axis=1)

    span = WIN_SIZE + C
    ws = pl.multiple_of(jnp.maximum(t0 - WIN_SIZE, 0), C)
    k_w = kw_ref[0, pl.ds(ws, span), :] + kpa_ref[pl.ds(ws, span), :]
    wpos = ws + lax.broadcasted_iota(jnp.int32, (span, 1), 0)
    wbias = jnp.where((wpos <= tq) & (wpos >= tq - (WIN_SIZE - 1)), 0.0, MASKED)
    s3 = _dot_nt(k_w, q4) + heads(wbias)
    e3 = jnp.exp(s3 - jnp.max(s3, axis=0, keepdims=True))
    ow = _dot(values_t(1, ws // LANE, span // LANE), e3.astype(BF16))
    o_win = ow / ow[SUM_LANE:SUM_LANE + 1, :]

    j = lax.broadcasted_iota(jnp.int32, (LANE, C), 0)
    cur = tq // SLC_BLOCK
    forced = (j == 0) | (j == cur) | (j == cur - 1)
    future = j * SLC_BLOCK > tq
    work = jnp.where(forced, -jnp.inf, jnp.where(future, -SLC_FORCE, p_slc))
    work = jnp.where(j < nb, work, -jnp.inf)
    sel = jnp.where(forced, 1.0, 0.0)
    jf = j.astype(F32)
    for _ in range(n_sel - 3):
        mx = jnp.max(work, axis=0, keepdims=True)
        first = jnp.min(jnp.where(work == mx, jf, float(LANE)), axis=0, keepdims=True)
        pick = jf == first
        sel = jnp.where(pick, 1.0, sel)
        work = jnp.where(pick, -jnp.inf, work)
    sel = jnp.where(future, 0.0, sel)
    blocks_per_tile = TK // SLC_BLOCK
    for kt in range(seq // TK):
        flag_ref[kt] = jnp.max(sel[kt * blocks_per_tile:(kt + 1) * blocks_per_tile, :])
    sel_bias = jnp.where(sel.T > 0.5, 0.0, MASKED).astype(BF16)
    q_aug = jnp.concatenate([q4, jnp.concatenate([sel_bias] * R, axis=0)], axis=1)

    m_ref[...] = jnp.full((1, rows), ROW_FLOOR, F32)
    acc_ref[...] = jnp.zeros((LANE, rows), F32)
    kio = lax.broadcasted_iota(jnp.int32, (TK, 1), 0)

    def scores(kt):
        start = pl.multiple_of(kt * TK, TK)
        return _dot_nt(kaug_ref[pl.ds(start, TK), :], q_aug)

    def absorb(s2, kt):
        m_i = m_ref[...]
        m_n = jnp.maximum(m_i, jnp.max(s2, axis=0, keepdims=True))
        alpha = jnp.exp(m_i - m_n)
        p2 = jnp.exp(s2 - m_n)
        acc_ref[...] = alpha * acc_ref[...] + _dot(values_t(0, kt * vblocks, vblocks), p2.astype(BF16))
        m_ref[...] = m_n

    last = (t0 + C - 1) // TK

    def compact(kt, n):
        list_ref[n] = kt
        return n + (flag_ref[kt] > 0.0).astype(jnp.int32)

    n_vis = lax.fori_loop(0, last, compact, 0)
    list_ref[n_vis] = last

    sbuf_ref[0] = scores(list_ref[0])

    def slc_body(j, carry):
        s_next = scores(list_ref[j + 1])
        absorb(sbuf_ref[j % 2], list_ref[j])
        sbuf_ref[(j + 1) % 2] = s_next
        return carry

    lax.fori_loop(0, n_vis, slc_body, 0)
    start = pl.multiple_of(last * TK, TK)
    absorb(sbuf_ref[n_vis % 2] + heads(jnp.where(start + kio <= tq, 0.0, MASKED)), last)
    acc = acc_ref[...]
    o_slc = acc / acc[SUM_LANE:SUM_LANE + 1, :]

    gates = jax.nn.sigmoid(gt_ref[0]).T
    for r in range(R):
        rs = slice(r * C, (r + 1) * C)
        out = (gates[r:r + 1] * o_cmp[:, rs] + gates[R + r:R + r + 1] * o_slc[:, rs]
               + gates[2 * R + r:2 * R + r + 1] * o_win[:, rs])
        o_ref[0, :, r * LANE:(r + 1) * LANE] = out.T.astype(o_ref.dtype)


def _nsa(slopes_b, pa3, gt3, kc, vct, vt, b, s, col0, gcol0):
    C = NSA_Q
    R = NSA_REP
    G = NSA_KV_HEADS
    nb = s // SLC_BLOCK
    ncmp = s // CMP_STRIDE
    assert nb <= LANE
    qc0 = col0 // R
    kv0 = col0 + NSA_HEADS
    rows = R * C

    ci = np.arange(ncmp)[None, :]
    sj = np.arange(LANE)[:, None]
    ov = ((ci * CMP_STRIDE < (sj + 1) * SLC_BLOCK) & (ci * CMP_STRIDE + CMP_BLOCK > sj * SLC_BLOCK)
          & (ci < ncmp - 1) & (sj < nb))
    ov = jnp.asarray(ov, BF16)
    kpa = _pos_columns(np.arange(s))
    cpa = _pos_columns(np.arange(ncmp) * CMP_STRIDE + (CMP_BLOCK - 1))
    oh = jnp.asarray(np.arange(s)[:, None] // SLC_BLOCK == np.arange(LANE)[None, :], BF16)

    def k_spec(which):
        return pl.BlockSpec((1, s, LANE), lambda bb, g, c: (bb, 0, kv0 + which * G + g))

    const = lambda a: pl.BlockSpec(a.shape, lambda bb, g, c: (0,) * a.ndim)
    return pl.pallas_call(
        functools.partial(_nsa_kernel, seq=s),
        grid=(b, G, s // C),
        in_specs=[
            pl.BlockSpec(memory_space=pltpu.SMEM),
            pl.BlockSpec((1, C, R * LANE), lambda bb, g, c: (bb, c, qc0 + g)),
            pl.BlockSpec((1, ncmp, LANE), lambda bb, g, c: (bb * G + g, 0, 0)),
            pl.BlockSpec((1, LANE, ncmp), lambda bb, g, c: (bb * G + g, 0, 0)),
            k_spec(2), k_spec(4),
            pl.BlockSpec((1, 2, s // LANE, LANE, LANE), lambda bb, g, c: (bb * G + g, 0, 0, 0, 0)),
            pl.BlockSpec((1, C, LANE), lambda bb, g, c: (bb, c, gcol0 + g)),
            const(ov), const(kpa), const(cpa), const(oh),
        ],
        out_specs=pl.BlockSpec((1, C, R * LANE), lambda bb, g, c: (bb, c, g)),
        out_shape=jax.ShapeDtypeStruct((b, s, NSA_HEADS * LANE), BF16),
        scratch_shapes=[
            pltpu.VMEM((s, 2 * LANE), BF16),
            pltpu.SMEM((s // NSA_TK,), F32),
            pltpu.SMEM((s // NSA_TK,), jnp.int32),
            pltpu.VMEM((2, NSA_TK, rows), F32),
            pltpu.VMEM((1, rows), F32),
            pltpu.VMEM((LANE, rows), F32),
        ],
        compiler_params=_params(("parallel", "parallel", "arbitrary")),
    )(slopes_b, pa3, kc, vct, pa3, pa3, vt, gt3, ov, kpa, cpa, oh)


def _merge_kernel(o1, o2, o3, l1, l2, l3, yb, ga, gb, x_ref, wa, wb, wo, gf, wrh, wrl,
                  h_ref, xn_ref, lg_ref, xr_ref):
    def pairs(ref):
        return jnp.concatenate([ref[h] for h in range(ref.shape[0])], axis=1)

    a1, a2, a3 = pairs(l1), pairs(l2), pairs(l3)
    mx = jnp.maximum(jnp.maximum(a1, a2), a3)
    e1, e2, e3 = jnp.exp(a1 - mx), jnp.exp(a2 - mx), jnp.exp(a3 - mx)
    tot = e1 + e2 + e3
    ya = (e1 / tot) * pairs(o1) + (e2 / tot) * pairs(o2) + (e3 / tot) * pairs(o3)
    merged = (jax.nn.sigmoid(ga[...]) * _dot(ya.astype(BF16), wa[...])
              + jax.nn.sigmoid(gb[...]) * _dot(yb[...], wb[...]))
    h = x_ref[...] + _dot(merged.astype(BF16), wo[...])
    h_ref[...] = h
    xn = _rms(h, gf[...])
    xn_ref[...] = xn.astype(BF16)
    tm, d = xn.shape
    half = d // (2 * LANE)

    for c in range(half):
        xr_ref[pl.ds(c, tm, stride=half), :] = _pack_bf16_pairs(
            xn[:, c * LANE:(c + 1) * LANE], xn[:, (c + half) * LANE:(c + half + 1) * LANE])
    xh = xn.astype(BF16)
    xl = (xn - xh.astype(F32)).astype(BF16)
    lg_ref[...] = _dot_nt(wrh[...], xh) + _dot_nt(wrh[...], xl) + _dot_nt(wrl[...], xh)


def _merge(o_list, l_list, yb, gt, x2, wa, wb, wo, gf, wrh, wrl):
    n, d = x2.shape
    aw = wa.shape[0]
    bw = wb.shape[0]
    tm = ROW_TILE
    row = lambda w: pl.BlockSpec((tm, w), lambda i: (i, 0))
    full = lambda a: pl.BlockSpec(a.shape, lambda i: (0,) * a.ndim)
    return pl.pallas_call(
        _merge_kernel,
        grid=(n // tm,),
        in_specs=[pl.BlockSpec((aw // LANE, tm, LANE), lambda i: (0, i, 0))] * 6 + [
            row(bw),
            pl.BlockSpec((tm, d), lambda i: (i, 0)),
            pl.BlockSpec((tm, d), lambda i: (i, 1)),
            row(d), full(wa), full(wb), full(wo), full(gf), full(wrh), full(wrl),
        ],
        out_specs=[row(d), row(d), pl.BlockSpec((N_EXPERTS, tm), lambda i: (0, i)),
                   pl.BlockSpec((tm * d // (2 * LANE), LANE), lambda i: (i, 0))],
        out_shape=[
            jax.ShapeDtypeStruct((n, d), F32),
            jax.ShapeDtypeStruct((n, d), BF16),
            jax.ShapeDtypeStruct((N_EXPERTS, n), F32),
            jax.ShapeDtypeStruct((n * d // (2 * LANE), LANE), jnp.uint32),
        ],
        compiler_params=_params(("parallel",)),
    )(*o_list, *l_list, yb, gt, gt, x2, wa, wb, wo, gf, wrh, wrl)


def _route_kernel(lg_ref, bias_ref, tri_ref, idx_ref, w_ref, rank_ref, cnt_ref, carry_ref):
    E = N_EXPERTS
    tn = lg_ref.shape[1]
    scores = jax.nn.sigmoid(lg_ref[...])
    sel = scores + bias_ref[...]
    sub = lax.broadcasted_iota(jnp.int32, (GROUP_SIZE, tn), 0).astype(F32)
    gs = []
    for grp in range(N_GROUPS):
        blk = sel[grp * GROUP_SIZE:(grp + 1) * GROUP_SIZE]
        m1 = jnp.max(blk, axis=0, keepdims=True)
        first = jnp.min(jnp.where(blk == m1, sub, float(GROUP_SIZE)), axis=0, keepdims=True)
        m2 = jnp.max(jnp.where(sub == first, -jnp.inf, blk), axis=0, keepdims=True)
        gs.append(m1 + m2)
    keep = []
    for a in range(N_GROUPS):
        rank = jnp.zeros((1, tn), jnp.int32)
        for o in range(N_GROUPS):
            if o == a:
                continue
            beats = (gs[o] >= gs[a]) if o < a else (gs[o] > gs[a])
            rank = rank + beats.astype(jnp.int32)
        keep.append(jnp.broadcast_to(rank < TOP_GROUPS, (GROUP_SIZE, tn)))
    emask = jnp.concatenate(keep, axis=0)
    cand = jnp.where(emask, sel, -jnp.inf)
    eid = lax.broadcasted_iota(jnp.int32, (E, tn), 0)
    eid_f = eid.astype(F32)
    rank = jnp.zeros((E, tn), jnp.int32)
    for o in range(E):
        other = cand[o:o + 1]
        beats = (other > cand) | ((other == cand) & (eid > o))
        rank = rank + beats.astype(jnp.int32)
    chosen = emask & (rank < TOP_K)
    wsel = jnp.where(chosen, scores, 0.0)
    wnorm = wsel / jnp.sum(wsel, axis=0, keepdims=True) * ROUTED_SCALE

    @pl.when(pl.program_id(0) == 0)
    def _():
        carry_ref[...] = jnp.zeros_like(carry_ref)

    chosen_f = jnp.where(chosen, 1.0, 0.0)
    before = carry_ref[...] + _dot(chosen_f.astype(BF16), tri_ref[...]) - chosen_f
    carry_ref[...] = carry_ref[...] + jnp.sum(chosen_f, axis=1, keepdims=True)
    cnt_ref[...] = jnp.broadcast_to(carry_ref[...], cnt_ref.shape).astype(jnp.int32)
    for k in range(TOP_K):
        hit = chosen & (rank == k)
        idx_ref[k:k + 1, :] = jnp.sum(jnp.where(hit, eid_f, 0.0), axis=0, keepdims=True).astype(jnp.int32)
        w_ref[k:k + 1, :] = jnp.sum(jnp.where(hit, wnorm, 0.0), axis=0, keepdims=True)
        rank_ref[k:k + 1, :] = jnp.sum(jnp.where(hit, before, 0.0), axis=0, keepdims=True).astype(jnp.int32)


def _route(logits_t, bias_col):
    e, n = logits_t.shape
    tn = ROW_TILE
    tri = jnp.asarray(np.triu(np.ones((tn, tn), np.float32)), BF16)
    slot = pl.BlockSpec((TOP_K, tn), lambda i: (0, i))
    return pl.pallas_call(
        _route_kernel,
        grid=(n // tn,),
        in_specs=[pl.BlockSpec((e, tn), lambda i: (0, i)), pl.BlockSpec((e, 1), lambda i: (0, 0)),
                  pl.BlockSpec((tn, tn), lambda i: (0, 0))],
        out_specs=[slot, slot, slot, pl.BlockSpec((e, LANE), lambda i: (0, 0))],
        out_shape=[
            jax.ShapeDtypeStruct((TOP_K, n), jnp.int32),
            jax.ShapeDtypeStruct((TOP_K, n), F32),
            jax.ShapeDtypeStruct((TOP_K, n), jnp.int32),
            jax.ShapeDtypeStruct((e, LANE), jnp.int32),
        ],
        scratch_shapes=[pltpu.VMEM((e, 1), F32)],
        compiler_params=_params(("arbitrary",)),
    )(logits_t, bias_col, tri)


def _dispatch_kernel(fill_lo_ref, fill_hi_ref, nused_ref, pos_ref, x_ref, xs_ref, zero_ref, sem, zsem):
    i = pl.program_id(0)
    tm = pos_ref.shape[1]
    sub = x_ref.shape[0] // tm
    blk_rows = zero_ref.shape[0]
    nblk = xs_ref.shape[0] // blk_rows

    def slot(r):
        return xs_ref.at[pl.ds(pl.multiple_of(r * sub, sub), sub), :]

    @pl.when(i == 0)
    def _():
        zero_ref[...] = jnp.zeros_like(zero_ref)

        def per_expert(e, carry):
            def fill(r, c2):
                pltpu.make_async_copy(zero_ref.at[pl.ds(0, sub), :], slot(r), zsem).start()
                return c2

            def drain(r, c2):
                pltpu.make_async_copy(zero_ref.at[pl.ds(0, sub), :], slot(r), zsem).wait()
                return c2

            lax.fori_loop(fill_lo_ref[e], fill_hi_ref[e], fill, 0)
            lax.fori_loop(fill_lo_ref[e], fill_hi_ref[e], drain, 0)
            return carry

        lax.fori_loop(0, fill_lo_ref.shape[0], per_expert, 0)

        def tail(bk, carry):
            cp = pltpu.make_async_copy(
                zero_ref, xs_ref.at[pl.ds(pl.multiple_of(bk * blk_rows, blk_rows), blk_rows), :], zsem)
            cp.start()
            cp.wait()
            return carry

        lax.fori_loop(nused_ref[0], nblk, tail, 0)

    def scatter(t, carry):
        src = x_ref.at[pl.ds(pl.multiple_of(t * sub, sub), sub), :]
        for k in range(TOP_K):
            pltpu.make_async_copy(src, slot(pos_ref[k, t]), sem).start()
        return carry

    lax.fori_loop(0, tm, scatter, 0)
    for k in range(TOP_K):
        pltpu.make_async_copy(x_ref, xs_ref.at[pl.ds(0, tm * sub), :], sem).wait()


def _dispatch(fill_lo, fill_hi, nused, pos_t, xr, p_rows, d):
    k, n = pos_t.shape
    sub = d // (2 * LANE)
    tm = ROW_TILE
    grid_spec = pltpu.PrefetchScalarGridSpec(
        num_scalar_prefetch=3,
        grid=(n // tm,),
        in_specs=[
            pl.BlockSpec((k, tm), lambda i, *_: (0, i), memory_space=pltpu.SMEM),
            pl.BlockSpec((tm * sub, LANE), lambda i, *_: (i, 0)),
        ],
        out_specs=pl.BlockSpec(memory_space=pl.ANY),
        scratch_shapes=[pltpu.VMEM((MOE_ROWS * sub, LANE), xr.dtype), pltpu.SemaphoreType.DMA(()),
                        pltpu.SemaphoreType.DMA(())],
    )
    return pl.pallas_call(
        _dispatch_kernel,
        grid_spec=grid_spec,
        out_shape=jax.ShapeDtypeStruct((p_rows * sub, LANE), xr.dtype),
        compiler_params=_params(("arbitrary",)),
    )(fill_lo, fill_hi, nused, pos_t, xr)


def _dispatch_sc(pos_t, xr3, p_rows):
    k, n = pos_t.shape
    _, sub, lane = xr3.shape
    nk = k * n
    workers = SC_CORES * SC_SUBCORES
    per_worker = nk // workers
    assert nk % (workers * SC_WINDOW) == 0 and n % SC_WINDOW == 0
    mesh = plsc.VectorSubcoreMesh(core_axis_name="c", subcore_axis_name="s")

    @functools.partial(
        pl.kernel, mesh=mesh,
        out_type=jax.ShapeDtypeStruct((p_rows, sub, lane), xr3.dtype),
        scratch_types=[pltpu.VMEM((SC_WINDOW,), jnp.int32), pltpu.VMEM((SC_WINDOW, sub, lane), xr3.dtype),
                       pltpu.SemaphoreType.DMA],
    )
    def scatter_rows(x_hbm, idx_hbm, o_hbm, idx_v, rows_v, sem):
        wid = lax.axis_index("s") * SC_CORES + lax.axis_index("c")

        @pl.loop(0, per_worker // SC_WINDOW)
        def _(c):
            base = wid * per_worker + c * SC_WINDOW
            tok0 = lax.rem(base, n)
            pltpu.sync_copy(idx_hbm.at[pl.ds(base, SC_WINDOW)], idx_v)
            pltpu.sync_copy(x_hbm.at[pl.ds(tok0, SC_WINDOW)], rows_v)
            pltpu.async_copy(rows_v, o_hbm.at[idx_v], sem).wait()

    return scatter_rows(xr3, pos_t.reshape(nk))


def _combine_sc(pos_t, ys3):
    k, n = pos_t.shape
    _, sub, lane = ys3.shape
    nk = k * n
    workers = SC_CORES * SC_SUBCORES
    per_worker = nk // workers
    assert nk % (workers * SC_WINDOW) == 0
    mesh = plsc.VectorSubcoreMesh(core_axis_name="c", subcore_axis_name="s")

    @functools.partial(
        pl.kernel, mesh=mesh,
        out_type=jax.ShapeDtypeStruct((nk, sub, lane), ys3.dtype),
        scratch_types=[pltpu.VMEM((SC_WINDOW,), jnp.int32), pltpu.VMEM((SC_WINDOW, sub, lane), ys3.dtype),
                       pltpu.SemaphoreType.DMA],
    )
    def gather_rows(y_hbm, idx_hbm, o_hbm, idx_v, rows_v, sem):
        wid = lax.axis_index("s") * SC_CORES + lax.axis_index("c")

        @pl.loop(0, per_worker // SC_WINDOW)
        def _(c):
            base = wid * per_worker + c * SC_WINDOW
            pltpu.sync_copy(idx_hbm.at[pl.ds(base, SC_WINDOW)], idx_v)
            pltpu.async_copy(y_hbm.at[idx_v], rows_v, sem).wait()
            pltpu.sync_copy(rows_v, o_hbm.at[pl.ds(base, SC_WINDOW)])

    return gather_rows(ys3, pos_t.reshape(nk))


def _moe_kernel(blk_e_ref, nused_ref, nvalid_ref, xs_ref, wg_ref, wu_ref, wd_ref, o_ref):
    i = pl.program_id(0)

    @pl.when(i < nused_ref[0])
    def _():
        sub = xs_ref.shape[0] // MOE_ROWS
        owned = lax.broadcasted_iota(jnp.int32, (MOE_ROWS, LANE), 0) < nvalid_ref[i]
        words = [jnp.where(owned, xs_ref[pl.ds(c, MOE_ROWS, stride=sub), :], jnp.uint32(0)) for c in range(sub)]
        halves = [_unpack_bf16_pairs(w) for w in words]
        x = jnp.concatenate([lo for lo, _ in halves] + [hi for _, hi in halves], axis=1).astype(BF16)
        hg = _dot(x, wg_ref[0])
        hid = (hg * jax.nn.sigmoid(hg)) * _dot(x, wu_ref[0])
        y = _dot(hid.astype(BF16), wd_ref[0])
        for c in range(sub):
            o_ref[pl.ds(c, MOE_ROWS, stride=sub), :] = _pack_bf16_pairs(
                y[:, c * LANE:(c + 1) * LANE], y[:, (c + sub) * LANE:(c + sub + 1) * LANE])

    @pl.when(i >= nused_ref[0])
    def _():
        o_ref[...] = jnp.zeros_like(o_ref)


def _moe_experts(blk_e, nused, nvalid, xs, wg, wu, wd):
    d = wg.shape[1]
    sub = d // (2 * LANE)
    p = xs.shape[0] // sub
    hid = wg.shape[2]
    grid_spec = pltpu.PrefetchScalarGridSpec(
        num_scalar_prefetch=3,
        grid=(p // MOE_ROWS,),
        in_specs=[
            pl.BlockSpec((MOE_ROWS * sub, LANE), lambda i, be, nu, nv: (jnp.minimum(i, nu[0] - 1), 0)),
            pl.BlockSpec((1, d, hid), lambda i, be, nu, nv: (be[i], 0, 0)),
            pl.BlockSpec((1, d, hid), lambda i, be, nu, nv: (be[i], 0, 0)),
            pl.BlockSpec((1, hid, d), lambda i, be, nu, nv: (be[i], 0, 0)),
        ],
        out_specs=pl.BlockSpec((MOE_ROWS * sub, LANE), lambda i, be, nu, nv: (i, 0)),
    )
    return pl.pallas_call(
        _moe_kernel,
        grid_spec=grid_spec,
        out_shape=jax.ShapeDtypeStruct((p * sub, LANE), jnp.uint32),
        compiler_params=_params(("arbitrary",)),
    )(blk_e, nused, nvalid, xs, wg, wu, wd)


def _final_kernel(h_ref, xn_ref, ys_ref, w_ref, p_ref, wsg, wsu, wsd, gp, wpg, wp, gfin, o_ref):
    xn = xn_ref[...]
    hg = _dot(xn, wsg[...])
    shared = _dot(((hg * jax.nn.sigmoid(hg)) * _dot(xn, wsu[...])).astype(BF16), wsd[...])
    w = w_ref[...]
    tm = w.shape[0]
    sub = ys_ref.shape[1] // tm
    parts = [None] * (2 * sub)
    for k in range(TOP_K):
        wk = jnp.broadcast_to(w[:, k:k + 1], (tm, LANE))
        for c in range(sub):
            lo, hi = _unpack_bf16_pairs(ys_ref[k, pl.ds(c, tm, stride=sub), :])
            parts[c] = wk * lo if k == 0 else parts[c] + wk * lo
            parts[c + sub] = wk * hi if k == 0 else parts[c + sub] + wk * hi
    y = jnp.concatenate(parts, axis=1)
    h = h_ref[...] + (y + shared)
    gate = jax.nn.sigmoid(_dot(_rms(h, gp[...]).astype(BF16), wpg[...]))
    h = h + _dot(p_ref[...].astype(BF16), wp[...]) * gate
    o_ref[...] = _rms(h, gfin[...])


def _final(h1, xn2, ysg, w_nk, p2, wsg, wsu, wsd, gp, wpg, wp, gfin):
    n, d = h1.shape
    tm = ROW_TILE
    row = lambda w: pl.BlockSpec((tm, w), lambda i: (i, 0))
    full = lambda a: pl.BlockSpec(a.shape, lambda i: (0,) * a.ndim)
    return pl.pallas_call(
        _final_kernel,
        grid=(n // tm,),
        in_specs=[row(d), row(d), pl.BlockSpec((TOP_K, tm * ysg.shape[1] // n, LANE), lambda i: (0, i, 0)),
                  row(TOP_K), row(p2.shape[1]),
                  full(wsg), full(wsu), full(wsd), full(gp), full(wpg), full(wp), full(gfin)],
        out_specs=row(d),
        out_shape=jax.ShapeDtypeStruct((n, d), F32),
        compiler_params=_params(("parallel",)),
    )(h1, xn2, ysg, w_nk, p2, wsg, wsu, wsd, gp, wpg, wp, gfin)


def _pad_heads(w):
    d, c = w.shape
    nh = c // HEAD_DIM
    w = w.reshape(d, nh, HEAD_DIM)
    return jnp.pad(w, ((0, 0), (0, 0), (0, LANE - HEAD_DIM))).reshape(d, nh * LANE)


def _alibi_slopes(n):
    return 2.0 ** (-8.0 * jnp.arange(1, n + 1, dtype=jnp.float32) / n)


def kernel(x, p, g_mix, w_in, cmp_pos_k, cmp_w1_k, cmp_b1_k, cmp_w2_k, cmp_pos_v, cmp_w1_v, cmp_b1_v, cmp_w2_v, w_branch_a, w_branch_b, w_out, g_ffn, w_router, router_bias, w_gate, w_up, w_down, ws_gate, ws_up, ws_down, g_ple, w_ple_gate, w_ple, g_final):
    B, S, D = x.shape
    N = B * S
    G, R = NSA_KV_HEADS, NSA_REP
    assert w_in.shape[0] == 1 and p.shape[0] == 1
    assert S % DIL_STEP_ROWS == 0 and S % NSA_TK == 0 and S >= WIN_SIZE + NSA_Q
    assert N % ROW_TILE == 0 and N % INPROJ_ROWS == 0 and S % INPROJ_ROWS == 0
    slopes = _alibi_slopes(DIL_HEADS + NSA_HEADS)
    slopes_a, slopes_b = slopes[0::2], slopes[1::2]

    offs = np.concatenate([[0], np.cumsum(IN_SIZES)])
    parts = [w_in[0][:, offs[i]:offs[i + 1]] for i in range(len(IN_SIZES))]
    w_qa, w_ka, w_va, w_qb, w_kc, w_vc, w_ks, w_vs, w_kw, w_vw, w_gn, w_ga, w_gb = parts
    w_qkv_a = jnp.concatenate([w_qa * HEAD_DIM ** -0.5, w_ka, w_va], axis=1).astype(BF16)
    w_nsa = jnp.concatenate([_pad_heads(t) for t in (w_qb, w_kc, w_vc, w_ks, w_vs, w_kw, w_vw)],
                            axis=1).astype(BF16)
    nsa_w = w_nsa.shape[1]
    w_gn = w_gn.reshape(D, 3, G, R).transpose(0, 2, 1, 3).reshape(D, G, 3 * R)
    w_gn = jnp.pad(w_gn, ((0, 0), (0, 0), (0, LANE - 3 * R))).reshape(D, G * LANE)
    w_gate_in = jnp.concatenate([w_ga, w_gb, w_gn], axis=1).astype(BF16)
    gate_w = w_gate_in.shape[1]

    x2 = x.reshape(N, D)
    dils = tuple(dil for _, dil in DIL_PATTERNS if dil > 1)
    qkv_a, *dilated_qkv = _inproj(x2, g_mix, w_qkv_a, BF16, 512, dils=dils)
    nsa_in, = _inproj(x2, g_mix, w_nsa, BF16, 512)
    gt, = _inproj(x2, g_mix, w_gate_in, F32, 768)

    o_list, l_list = [], []
    for _, dil in DIL_PATTERNS:
        src = qkv_a if dil == 1 else dilated_qkv[dils.index(dil)]
        o, l = _dilated(src.reshape(B, S // dil, -1), _dil_bias(slopes_a, dil), dil, B, S)
        o_list.append(o.reshape(-1, N, LANE))
        l_list.append(l.reshape(-1, N, LANE))

    nchunk = S // CMP_STRIDE
    c0 = NSA_HEADS * LANE
    raw = nsa_in[:, c0:c0 + 2 * G * LANE].reshape(B, nchunk, CMP_STRIDE, 2 * G, LANE)[..., :HEAD_DIM]
    xc = raw.transpose(0, 3, 1, 2, 4).reshape(B, 2 * G, nchunk, CMP_STRIDE * HEAD_DIM)
    pos = jnp.stack([cmp_pos_k[0].reshape(1, -1), cmp_pos_v[0].reshape(1, -1)])
    w1 = jnp.stack([cmp_w1_k[0], cmp_w1_v[0]]).astype(BF16)
    b1 = jnp.stack([cmp_b1_k[0][None], cmp_b1_v[0][None]])
    w2 = jnp.pad(jnp.stack([cmp_w2_k[0], cmp_w2_v[0]]), ((0, 0), (0, 0), (0, LANE - HEAD_DIM))).astype(BF16)
    kcv = _compress(xc, pos, w1, b1, w2)
    kcv = kcv.reshape(B, 2, G, nchunk, LANE).transpose(1, 0, 2, 3, 4).reshape(2, B * G, nchunk, LANE)
    vct = kcv[1].transpose(0, 2, 1)
    v0 = (NSA_HEADS + 3 * G) * LANE
    vsw = jnp.stack([nsa_in[:, v0:v0 + G * LANE], nsa_in[:, v0 + 2 * G * LANE:v0 + 3 * G * LANE]])
    vsw = vsw.reshape(2, B, S // LANE, LANE, G, LANE).transpose(1, 4, 0, 2, 5, 3)
    ones_row = (jnp.arange(LANE) == SUM_LANE).astype(BF16)[:, None]
    vt = (vsw + ones_row).reshape(B * G, 2, S // LANE, LANE, LANE)
    yb = _nsa(slopes_b, nsa_in.reshape(B, S, nsa_w), gt.reshape(B, S, gate_w), kcv[0], vct, vt,
              B, S, 0, 2 * D // LANE)
    yb = yb.reshape(N, NSA_HEADS * LANE)

    wb_pad = jnp.pad(w_branch_b[0].reshape(NSA_HEADS, HEAD_DIM, D),
                     ((0, 0), (0, LANE - HEAD_DIM), (0, 0))).reshape(NSA_HEADS * LANE, D)
    wr_t = w_router[0].T
    wr_hi = wr_t.astype(BF16)
    wr_lo = (wr_t - wr_hi.astype(F32)).astype(BF16)
    h1, xn2, logits_t, xr = _merge(o_list, l_list, yb, gt, x2, w_branch_a[0].astype(BF16), wb_pad.astype(BF16),
                               w_out[0].astype(BF16), g_ffn, wr_hi, wr_lo)

    idx_t, wsel_t, rank_t, counts = _route(logits_t, router_bias[0][:, None])
    counts = counts[:, 0]
    padded = (counts + MOE_ROWS - 1) // MOE_ROWS * MOE_ROWS
    pend = jnp.cumsum(padded)
    pstart = pend - padded
    hot = idx_t[None] == jnp.arange(N_EXPERTS, dtype=jnp.int32)[:, None, None]
    pos_t = rank_t + jnp.sum(jnp.where(hot, pstart.astype(jnp.int32)[:, None, None], 0), axis=0)
    NK = N * TOP_K
    nblk = (NK + N_EXPERTS * (MOE_ROWS - 1) + MOE_ROWS - 1) // MOE_ROWS
    P = nblk * MOE_ROWS
    blk_e = jnp.minimum(jnp.sum(pend[None, :] <= (jnp.arange(nblk, dtype=jnp.int32) * MOE_ROWS)[:, None], axis=1),
                        N_EXPERTS - 1).astype(jnp.int32)
    nused = (pend[-1] // MOE_ROWS).astype(jnp.int32).reshape(1)
    row0 = jnp.arange(nblk, dtype=jnp.int32) * MOE_ROWS
    nvalid = jnp.clip((pstart + counts)[blk_e] - row0, 0, MOE_ROWS).astype(jnp.int32)
    sub = D // (2 * LANE)
    xs = _dispatch_sc(pos_t, xr.reshape(N, sub, LANE), P).reshape(P * sub, LANE)
    ys = _moe_experts(blk_e, nused, nvalid, xs, w_gate[0].astype(BF16), w_up[0].astype(BF16), w_down[0].astype(BF16))
    ysg = _combine_sc(pos_t, ys.reshape(P, sub, LANE)).reshape(TOP_K, N * sub, LANE)
    out = _final(h1, xn2, ysg, wsel_t.T, p[0].reshape(N, -1), ws_gate[0].astype(BF16), ws_up[0].astype(BF16),
                 ws_down[0].astype(BF16), g_ple, w_ple_gate[0].astype(BF16), w_ple[0].astype(BF16),
                 g_final[None, :])
    return out.reshape(B, S, D)
```

```python
import functools

import numpy as np
import jax
import jax.numpy as jnp
from jax import lax
from jax.experimental import pallas as pl
from jax.experimental.pallas import tpu as pltpu
from jax.experimental.pallas import tpu_sc as plsc

F32 = jnp.float32
BF16 = jnp.bfloat16

HEAD_DIM = 64
LANE = 128
RMS_EPS = 1e-6
DIL_HEADS = 8
DIL_PATTERNS = ((128, 1), (512, 4), (2048, 16))
DIL_BLOCK = 128
DIL_STEP_ROWS = 4096
NSA_HEADS = 8
NSA_KV_HEADS = 2
NSA_REP = NSA_HEADS // NSA_KV_HEADS
CMP_BLOCK = 32
CMP_STRIDE = 16
CMP_HIDDEN = 256
SLC_BLOCK = 64
SLC_TOP_N = 16
SLC_FORCE = 1e9
WIN_SIZE = 512
N_EXPERTS = 64
TOP_K = 8
N_GROUPS = 8
TOP_GROUPS = 4
GROUP_SIZE = N_EXPERTS // N_GROUPS
ROUTED_SCALE = 2.5
IN_SIZES = (512, 512, 512, 512, 128, 128, 128, 128, 128, 128, 24, 1024, 1024)

NEG = -1e30
NT_DIMS = (((1,), (1,)), ((), ()))

ROW_TILE = 512
INPROJ_ROWS = 1024
NSA_Q = 256
NSA_TK = 512
MOE_ROWS = 1024
VMEM_LIMIT = 48 * 1024 * 1024
SC_CORES = 2
SC_SUBCORES = 16
SC_WINDOW = 128


def _dot(a, b):
    return jnp.dot(a, b, preferred_element_type=F32)


def _dot_nt(a, b):
    return lax.dot_general(a, b, NT_DIMS, preferred_element_type=F32)


def _split3(x):
    hi = x.astype(BF16)
    r1 = x - hi.astype(F32)
    mid = r1.astype(BF16)
    lo = (r1 - mid.astype(F32)).astype(BF16)
    return hi, mid, lo


def _pack_bf16_pairs(lo, hi):
    lo_bits = pltpu.bitcast(lo.astype(BF16).astype(F32), jnp.uint32)
    hi_bits = pltpu.bitcast(hi.astype(BF16).astype(F32), jnp.uint32)
    return hi_bits | (lo_bits >> 16)


def _unpack_bf16_pairs(words):
    return (pltpu.bitcast(words << 16, F32), pltpu.bitcast(words & jnp.uint32(0xFFFF0000), F32))


def _rms(x, g):
    r = lax.rsqrt(jnp.mean(x * x, axis=-1, keepdims=True) + RMS_EPS)
    return (x * r) * g


def _params(sem):
    return pltpu.CompilerParams(dimension_semantics=sem, vmem_limit_bytes=VMEM_LIMIT)


def _inproj_kernel(x_ref, g_ref, w_ref, o_ref, *rest, dils):
    dil_refs, xn_ref = rest[:len(dils)], rest[len(dils)]
    res_ref = rest[len(dils) + 1] if dils else None

    @pl.when(pl.program_id(1) == 0)
    def _():
        xn_ref[...] = _rms(x_ref[...], g_ref[...]).astype(BF16)

    res = _dot(xn_ref[...], w_ref[...])
    o_ref[...] = res.astype(o_ref.dtype)

    if dils:
        tn = res.shape[1]
        for c in range(tn // LANE):
            res_ref[c] = res[:, c * LANE:(c + 1) * LANE]
        for d_ref, dil in zip(dil_refs, dils):
            for r in range(dil):
                for c in range(tn // LANE):
                    piece = res_ref[c, pl.ds(r, INPROJ_ROWS // dil, stride=dil), :]
                    d_ref[:, r * tn + c * LANE:r * tn + (c + 1) * LANE] = piece.astype(d_ref.dtype)


def _inproj(x2, g, w, out_dtype, tn, dils=()):
    n, d = x2.shape
    cols = w.shape[1]
    out_specs = [pl.BlockSpec((INPROJ_ROWS, tn), lambda i, j: (i, j))]
    out_shape = [jax.ShapeDtypeStruct((n, cols), out_dtype)]
    for dil in dils:
        out_specs.append(pl.BlockSpec((INPROJ_ROWS // dil, dil * tn), lambda i, j: (i, j)))
        out_shape.append(jax.ShapeDtypeStruct((n // dil, cols * dil), out_dtype))
    return pl.pallas_call(
        functools.partial(_inproj_kernel, dils=dils),
        grid=(n // INPROJ_ROWS, cols // tn),
        in_specs=[
            pl.BlockSpec((INPROJ_ROWS, d), lambda i, j: (i, 0)),
            pl.BlockSpec((1, d), lambda i, j: (0, 0)),
            pl.BlockSpec((d, tn), lambda i, j: (0, j)),
        ],
        out_specs=out_specs,
        out_shape=out_shape,
        scratch_shapes=[pltpu.VMEM((INPROJ_ROWS, d), BF16)]
        + ([pltpu.VMEM((tn // LANE, INPROJ_ROWS, LANE), F32)] if dils else []),
        compiler_params=_params(("parallel", "arbitrary")),
    )(x2, g, w)


def _dil_kernel(q_ref, kp_ref, kc_ref, vp_ref, vc_ref, bias_ref, o_ref, l_ref, *, dil, nq):
    i = pl.program_id(1)
    res = pl.program_id(2)
    blk = DIL_BLOCK
    lane = lax.broadcasted_iota(jnp.int32, (blk, LANE), 1)
    lo = lane < HEAD_DIM
    for sub in range(nq):
        rows = slice(sub * blk, (sub + 1) * blk)
        prev = slice((sub - 1) * blk, sub * blk)
        out_rows = pl.ds(sub * blk * dil + res, blk, stride=dil) if dil > 1 else rows
        variant = jnp.minimum(i, 1) * DIL_HEADS if sub == 0 else DIL_HEADS
        for hp in range(DIL_HEADS // 2):
            sl = slice(hp * LANE, (hp + 1) * LANE)
            q2 = q_ref[0, rows, sl]
            k_prev = kp_ref[0, :, sl] if sub == 0 else kc_ref[0, prev, sl]
            v_prev = vp_ref[0, :, sl] if sub == 0 else vc_ref[0, prev, sl]
            k2 = jnp.concatenate([k_prev, kc_ref[0, rows, sl]], axis=0)
            v2 = jnp.concatenate([v_prev, vc_ref[0, rows, sl]], axis=0)
            outs, lses = [], []
            for hh in range(2):
                keep = lo if hh == 0 else jnp.logical_not(lo)
                qm = jnp.where(keep, q2, jnp.zeros_like(q2))
                s = _dot_nt(qm, k2) + bias_ref[variant + 2 * hp + hh]
                m = jnp.max(s, axis=-1, keepdims=True)
                e = jnp.exp(s - m)
                den = jnp.sum(e, axis=-1, keepdims=True)
                outs.append(_dot(e.astype(BF16), v2) / den)
                lses.append(jnp.broadcast_to(m + jnp.log(den), (blk, LANE)))
            o_ref[hp, 0, out_rows, :] = jnp.where(lo, outs[0], outs[1])
            l_ref[hp, 0, out_rows, :] = jnp.where(lo, lses[0], lses[1])


def _dil_bias(slopes_a, dil):
    blk = DIL_BLOCK
    qi = jnp.arange(blk)[:, None]
    kj = jnp.arange(2 * blk)[None, :]
    dist = (qi + blk - kj).astype(F32)
    band = (dist >= 0) & (dist <= blk)
    slope = (slopes_a * dil)[:, None, None]
    bias = -slope * dist
    first = jnp.where(band & (kj >= blk), bias, NEG)
    rest = jnp.where(band, bias, NEG)
    return jnp.concatenate([first, rest], axis=0)


def _dilated(src3, bias, dil, b, s):
    nq = min(8, DIL_STEP_ROWS // (DIL_BLOCK * dil))
    nstep = s // dil // (DIL_BLOCK * nq)
    aw = DIL_HEADS * HEAD_DIM
    npair = DIL_HEADS // 2

    def spec(which, prev):
        if prev:
            return pl.BlockSpec((1, DIL_BLOCK, aw), lambda bb, i, r: (bb, jnp.maximum(nq * i - 1, 0), which * dil + r))
        return pl.BlockSpec((1, DIL_BLOCK * nq, aw), lambda bb, i, r: (bb, i, which * dil + r))

    out_spec = pl.BlockSpec((npair, 1, DIL_BLOCK * nq * dil, LANE), lambda bb, i, r: (0, bb, i, 0))
    return pl.pallas_call(
        functools.partial(_dil_kernel, dil=dil, nq=nq),
        grid=(b, nstep, dil),
        in_specs=[
            spec(0, False), spec(1, True), spec(1, False), spec(2, True), spec(2, False),
            pl.BlockSpec(bias.shape, lambda bb, i, r: (0, 0, 0)),
        ],
        out_specs=[out_spec, out_spec],
        out_shape=[jax.ShapeDtypeStruct((npair, b, s, LANE), F32)] * 2,
        compiler_params=_params(("parallel", "arbitrary", "arbitrary")),
    )(src3, src3, src3, src3, src3, bias)


def _cmp_kernel(x_ref, pos_ref, w1_ref, b1_ref, w2_ref, o_ref):
    half = x_ref.shape[-1]
    x = x_ref[0, 0]
    w1 = w1_ref[0]
    first = _dot(x, w1[:half])
    second = _dot(x, w1[half:])
    nrow = x.shape[0]
    pre = first + pltpu.roll(second, nrow - 1, 0)
    posb = _dot(jnp.broadcast_to(pos_ref[0], (8, 2 * half)).astype(BF16), w1)[0:1]
    pre = pre + posb + b1_ref[0]
    o_ref[0, 0] = _dot(jax.nn.gelu(pre).astype(BF16), w2_ref[0]).astype(o_ref.dtype)


def _compress(xc, pos, w1, b1, w2):
    b, four, nchunk, width = xc.shape
    return pl.pallas_call(
        _cmp_kernel,
        grid=(b, four),
        in_specs=[
            pl.BlockSpec((1, 1, nchunk, width), lambda bb, j: (bb, j, 0, 0)),
            pl.BlockSpec((1, 1, 2 * width), lambda bb, j: (j // 2, 0, 0)),
            pl.BlockSpec((1, 2 * width, CMP_HIDDEN), lambda bb, j: (j // 2, 0, 0)),
            pl.BlockSpec((1, 1, CMP_HIDDEN), lambda bb, j: (j // 2, 0, 0)),
            pl.BlockSpec((1, CMP_HIDDEN, LANE), lambda bb, j: (j // 2, 0, 0)),
        ],
        out_specs=pl.BlockSpec((1, 1, nchunk, LANE), lambda bb, j: (bb, j, 0, 0)),
        out_shape=jax.ShapeDtypeStruct((b, four, nchunk, LANE), BF16),
        compiler_params=_params(("parallel", "parallel")),
    )(xc, pos, w1, b1, w2)


POS_LANE = HEAD_DIM
POS_SPLIT = 64
SUM_LANE = HEAD_DIM
MASKED = -1e30
ROW_FLOOR = -1e29


def _pos_columns(pos):
    out = np.zeros((pos.shape[0], LANE), np.float32)
    out[:, POS_LANE] = pos // POS_SPLIT
    out[:, POS_LANE + 1] = pos % POS_SPLIT
    out[:, POS_LANE + 2] = 1.0
    out[:, POS_LANE + 3] = 1.0
    return jnp.asarray(out, BF16)


def _nsa_kernel(slope_ref, q_ref, kc_ref, vct_ref, ks_ref, kw_ref, vt_ref, gt_ref, ov_ref, kpa_ref, cpa_ref,
                oh_ref, o_ref, kaug_ref, flag_ref, list_ref, sbuf_ref, m_ref, acc_ref, *, seq):
    g = pl.program_id(1)
    c = pl.program_id(2)
    C = NSA_Q
    R = NSA_REP
    TK = NSA_TK
    rows = R * C
    t0 = c * C
    scale = HEAD_DIM ** -0.5
    nb = seq // SLC_BLOCK
    ncmp = seq // CMP_STRIDE
    n_sel = min(SLC_TOP_N, nb)
    vblocks = TK // LANE

    @pl.when(c == 0)
    def _():
        kaug_ref[:, 0:LANE] = ks_ref[0] + kpa_ref[...]
        kaug_ref[:, LANE:2 * LANE] = oh_ref[...]

    row = lax.broadcasted_iota(jnp.int32, (rows, 1), 0)
    ti = t0 + row % C
    slope = jnp.zeros((rows, 1), F32)
    for r in range(R):
        slope = jnp.where(row // C == r, slope_ref[g * R + r], slope)
    lane = lax.broadcasted_iota(jnp.int32, (rows, LANE), 1)
    thi = (ti // POS_SPLIT).astype(F32)
    tlo = (ti % POS_SPLIT).astype(F32)
    qcols = jnp.where(lane == POS_LANE, POS_SPLIT * slope,
                      jnp.where(lane == POS_LANE + 1, slope,
                                jnp.where(lane == POS_LANE + 2, -POS_SPLIT * slope * thi,
                                          jnp.where(lane == POS_LANE + 3, -slope * tlo, 0.0))))
    q4 = jnp.concatenate([q_ref[0, :, r * LANE:(r + 1) * LANE] for r in range(R)], axis=0)
    q4 = (q4.astype(F32) * scale + qcols).astype(BF16)
    tq = t0 + lax.broadcasted_iota(jnp.int32, (1, C), 1)

    def heads(x):
        return jnp.concatenate([x] * R, axis=1)

    kc = kc_ref[0] + cpa_ref[...]
    ci = lax.broadcasted_iota(jnp.int32, (ncmp, 1), 0)
    vis = (ci * CMP_STRIDE + (CMP_BLOCK - 1) <= tq) & (ci < ncmp - 1)
    s = _dot_nt(kc, q4) + heads(jnp.where(vis, 0.0, MASKED))
    m = jnp.maximum(jnp.max(s, axis=0, keepdims=True), ROW_FLOOR)
    e = jnp.exp(s - m)
    den = jnp.sum(e, axis=0, keepdims=True)
    pc = e * (1.0 / jnp.where(den > 0, den, 1.0))
    o_cmp = _dot(vct_ref[0], pc.astype(BF16))
    pcsum = pc[:, 0:C]
    for r in range(1, R):
        pcsum = pcsum + pc[:, r * C:(r + 1) * C]
    ov = ov_ref[...]
    p_slc = sum(_dot(ov, part) for part in _split3(pcsum))

    def values_t(which, first_block, n):
        return jnp.concatenate([vt_ref[0, which, first_block + i] for i in range(n)], axis=1)

    span = WIN_SIZE + C
    ws = pl.multiple_of(jnp.maximum(t0 - WIN_SIZE, 0), C)
    k_w = kw_ref[0, pl.ds(ws, span), :] + kpa_ref[pl.ds(ws, span), :]
    wpos = ws + lax.broadcasted_iota(jnp.int32, (span, 1), 0)
    wbias = jnp.where((wpos <= tq) & (wpos >= tq - (WIN_SIZE - 1)), 0.0, MASKED)
    s3 = _dot_nt(k_w, q4) + heads(wbias)
    e3 = jnp.exp(s3 - jnp.max(s3, axis=0, keepdims=True))
    ow = _dot(values_t(1, ws // LANE, span // LANE), e3.astype(BF16))
    o_win = ow / ow[SUM_LANE:SUM_LANE + 1, :]

    j = lax.broadcasted_iota(jnp.int32, (LANE, C), 0)
    cur = tq // SLC_BLOCK
    forced = (j == 0) | (j == cur) | (j == cur - 1)
    future = j * SLC_BLOCK > tq
    work = jnp.where(forced, -jnp.inf, jnp.where(future, -SLC_FORCE, p_slc))
    work = jnp.where(j < nb, work, -jnp.inf)
    sel = jnp.where(forced, 1.0, 0.0)
    jf = j.astype(F32)
    for _ in range(n_sel - 3):
        mx = jnp.max(work, axis=0, keepdims=True)
        first = jnp.min(jnp.where(work == mx, jf, float(LANE)), axis=0, keepdims=True)
        pick = jf == first
        sel = jnp.where(pick, 1.0, sel)
        work = jnp.where(pick, -jnp.inf, work)
    sel = jnp.where(future, 0.0, sel)
    blocks_per_tile = TK // SLC_BLOCK
    for kt in range(seq // TK):
        flag_ref[kt] = jnp.max(sel[kt * blocks_per_tile:(kt + 1) * blocks_per_tile, :])
    sel_bias = jnp.where(sel.T > 0.5, 0.0, MASKED).astype(BF16)
    q_aug = jnp.concatenate([q4, jnp.concatenate([sel_bias] * R, axis=0)], axis=1)

    m_ref[...] = jnp.full((1, rows), ROW_FLOOR, F32)
    acc_ref[...] = jnp.zeros((LANE, rows), F32)
    kio = lax.broadcasted_iota(jnp.int32, (TK, 1), 0)

    def scores(kt):
        start = pl.multiple_of(kt * TK, TK)
        return _dot_nt(kaug_ref[pl.ds(start, TK), :], q_aug)

    def absorb(s2, kt):
        m_i = m_ref[...]
        m_n = jnp.maximum(m_i, jnp.max(s2, axis=0, keepdims=True))
        alpha = jnp.exp(m_i - m_n)
        p2 = jnp.exp(s2 - m_n)
        acc_ref[...] = alpha * acc_ref[...] + _dot(values_t(0, kt * vblocks, vblocks), p2.astype(BF16))
        m_ref[...] = m_n

    last = (t0 + C - 1) // TK

    def compact(kt, n):
        list_ref[n] = kt
        return n + (flag_ref[kt] > 0.0).astype(jnp.int32)

    n_vis = lax.fori_loop(0, last, compact, 0)
    list_ref[n_vis] = last

    sbuf_ref[0] = scores(list_ref[0])

    def slc_body(j, carry):
        s_next = scores(list_ref[j + 1])
        absorb(sbuf_ref[j % 2], list_ref[j])
        sbuf_ref[(j + 1) % 2] = s_next
        return carry

    lax.fori_loop(0, n_vis, slc_body, 0)
    start = pl.multiple_of(last * TK, TK)
    absorb(sbuf_ref[n_vis % 2] + heads(jnp.where(start + kio <= tq, 0.0, MASKED)), last)
    acc = acc_ref[...]
    o_slc = acc / acc[SUM_LANE:SUM_LANE + 1, :]

    gates = jax.nn.sigmoid(gt_ref[0]).T
    for r in range(R):
        rs = slice(r * C, (r + 1) * C)
        out = (gates[r:r + 1] * o_cmp[:, rs] + gates[R + r:R + r + 1] * o_slc[:, rs]
               + gates[2 * R + r:2 * R + r + 1] * o_win[:, rs])
        o_ref[0, :, r * LANE:(r + 1) * LANE] = out.T.astype(o_ref.dtype)


def _nsa(slopes_b, pa3, gt3, kc, vct, vt, b, s, col0, gcol0):
    C = NSA_Q
    R = NSA_REP
    G = NSA_KV_HEADS
    nb = s // SLC_BLOCK
    ncmp = s // CMP_STRIDE
    assert nb <= LANE
    qc0 = col0 // R
    kv0 = col0 + NSA_HEADS
    rows = R * C

    ci = np.arange(ncmp)[None, :]
    sj = np.arange(LANE)[:, None]
    ov = ((ci * CMP_STRIDE < (sj + 1) * SLC_BLOCK) & (ci * CMP_STRIDE + CMP_BLOCK > sj * SLC_BLOCK)
          & (ci < ncmp - 1) & (sj < nb))
    ov = jnp.asarray(ov, BF16)
    kpa = _pos_columns(np.arange(s))
    cpa = _pos_columns(np.arange(ncmp) * CMP_STRIDE + (CMP_BLOCK - 1))
    oh = jnp.asarray(np.arange(s)[:, None] // SLC_BLOCK == np.arange(LANE)[None, :], BF16)

    def k_spec(which):
        return pl.BlockSpec((1, s, LANE), lambda bb, g, c: (bb, 0, kv0 + which * G + g))

    const = lambda a: pl.BlockSpec(a.shape, lambda bb, g, c: (0,) * a.ndim)
    return pl.pallas_call(
        functools.partial(_nsa_kernel, seq=s),
        grid=(b, G, s // C),
        in_specs=[
            pl.BlockSpec(memory_space=pltpu.SMEM),
            pl.BlockSpec((1, C, R * LANE), lambda bb, g, c: (bb, c, qc0 + g)),
            pl.BlockSpec((1, ncmp, LANE), lambda bb, g, c: (bb * G + g, 0, 0)),
            pl.BlockSpec((1, LANE, ncmp), lambda bb, g, c: (bb * G + g, 0, 0)),
            k_spec(2), k_spec(4),
            pl.BlockSpec((1, 2, s // LANE, LANE, LANE), lambda bb, g, c: (bb * G + g, 0, 0, 0, 0)),
            pl.BlockSpec((1, C, LANE), lambda bb, g, c: (bb, c, gcol0 + g)),
            const(ov), const(kpa), const(cpa), const(oh),
        ],
        out_specs=pl.BlockSpec((1, C, R * LANE), lambda bb, g, c: (bb, c, g)),
        out_shape=jax.ShapeDtypeStruct((b, s, NSA_HEADS * LANE), BF16),
        scratch_shapes=[
            pltpu.VMEM((s, 2 * LANE), BF16),
            pltpu.SMEM((s // NSA_TK,), F32),
            pltpu.SMEM((s // NSA_TK,), jnp.int32),
            pltpu.VMEM((2, NSA_TK, rows), F32),
            pltpu.VMEM((1, rows), F32),
            pltpu.VMEM((LANE, rows), F32),
        ],
        compiler_params=_params(("parallel", "parallel", "arbitrary")),
    )(slopes_b, pa3, kc, vct, pa3, pa3, vt, gt3, ov, kpa, cpa, oh)


def _merge_kernel(o1, o2, o3, l1, l2, l3, yb, ga, gb, x_ref, wa, wb, wo, gf, wrh, wrl,
                  h_ref, xn_ref, lg_ref, xr_ref):
    def pairs(ref):
        return jnp.concatenate([ref[h] for h in range(ref.shape[0])], axis=1)

    a1, a2, a3 = pairs(l1), pairs(l2), pairs(l3)
    mx = jnp.maximum(jnp.maximum(a1, a2), a3)
    e1, e2, e3 = jnp.exp(a1 - mx), jnp.exp(a2 - mx), jnp.exp(a3 - mx)
    tot = e1 + e2 + e3
    ya = (e1 / tot) * pairs(o1) + (e2 / tot) * pairs(o2) + (e3 / tot) * pairs(o3)
    merged = (jax.nn.sigmoid(ga[...]) * _dot(ya.astype(BF16), wa[...])
              + jax.nn.sigmoid(gb[...]) * _dot(yb[...], wb[...]))
    h = x_ref[...] + _dot(merged.astype(BF16), wo[...])
    h_ref[...] = h
    xn = _rms(h, gf[...])
    xn_ref[...] = xn.astype(BF16)
    tm, d = xn.shape
    half = d // (2 * LANE)

    for c in range(half):
        xr_ref[pl.ds(c, tm, stride=half), :] = _pack_bf16_pairs(
            xn[:, c * LANE:(c + 1) * LANE], xn[:, (c + half) * LANE:(c + half + 1) * LANE])
    xh = xn.astype(BF16)
    xl = (xn - xh.astype(F32)).astype(BF16)
    lg_ref[...] = _dot_nt(wrh[...], xh) + _dot_nt(wrh[...], xl) + _dot_nt(wrl[...], xh)


def _merge(o_list, l_list, yb, gt, x2, wa, wb, wo, gf, wrh, wrl):
    n, d = x2.shape
    aw = wa.shape[0]
    bw = wb.shape[0]
    tm = ROW_TILE
    row = lambda w: pl.BlockSpec((tm, w), lambda i: (i, 0))
    full = lambda a: pl.BlockSpec(a.shape, lambda i: (0,) * a.ndim)
    return pl.pallas_call(
        _merge_kernel,
        grid=(n // tm,),
        in_specs=[pl.BlockSpec((aw // LANE, tm, LANE), lambda i: (0, i, 0))] * 6 + [
            row(bw),
            pl.BlockSpec((tm, d), lambda i: (i, 0)),
            pl.BlockSpec((tm, d), lambda i: (i, 1)),
            row(d), full(wa), full(wb), full(wo), full(gf), full(wrh), full(wrl),
        ],
        out_specs=[row(d), row(d), pl.BlockSpec((N_EXPERTS, tm), lambda i: (0, i)),
                   pl.BlockSpec((tm * d // (2 * LANE), LANE), lambda i: (i, 0))],
        out_shape=[
            jax.ShapeDtypeStruct((n, d), F32),
            jax.ShapeDtypeStruct((n, d), BF16),
            jax.ShapeDtypeStruct((N_EXPERTS, n), F32),
            jax.ShapeDtypeStruct((n * d // (2 * LANE), LANE), jnp.uint32),
        ],
        compiler_params=_params(("parallel",)),
    )(*o_list, *l_list, yb, gt, gt, x2, wa, wb, wo, gf, wrh, wrl)


def _route_kernel(lg_ref, bias_ref, tri_ref, idx_ref, w_ref, rank_ref, cnt_ref, carry_ref):
    E = N_EXPERTS
    tn = lg_ref.shape[1]
    scores = jax.nn.sigmoid(lg_ref[...])
    sel = scores + bias_ref[...]
    sub = lax.broadcasted_iota(jnp.int32, (GROUP_SIZE, tn), 0).astype(F32)
    gs = []
    for grp in range(N_GROUPS):
        blk = sel[grp * GROUP_SIZE:(grp + 1) * GROUP_SIZE]
        m1 = jnp.max(blk, axis=0, keepdims=True)
        first = jnp.min(jnp.where(blk == m1, sub, float(GROUP_SIZE)), axis=0, keepdims=True)
        m2 = jnp.max(jnp.where(sub == first, -jnp.inf, blk), axis=0, keepdims=True)
        gs.append(m1 + m2)
    keep = []
    for a in range(N_GROUPS):
        rank = jnp.zeros((1, tn), jnp.int32)
        for o in range(N_GROUPS):
            if o == a:
                continue
            beats = (gs[o] >= gs[a]) if o < a else (gs[o] > gs[a])
            rank = rank + beats.astype(jnp.int32)
        keep.append(jnp.broadcast_to(rank < TOP_GROUPS, (GROUP_SIZE, tn)))
    emask = jnp.concatenate(keep, axis=0)
    cand = jnp.where(emask, sel, -jnp.inf)
    eid = lax.broadcasted_iota(jnp.int32, (E, tn), 0)
    eid_f = eid.astype(F32)
    rank = jnp.zeros((E, tn), jnp.int32)
    for o in range(E):
        other = cand[o:o + 1]
        beats = (other > cand) | ((other == cand) & (eid > o))
        rank = rank + beats.astype(jnp.int32)
    chosen = emask & (rank < TOP_K)
    wsel = jnp.where(chosen, scores, 0.0)
    wnorm = wsel / jnp.sum(wsel, axis=0, keepdims=True) * ROUTED_SCALE

    @pl.when(pl.program_id(0) == 0)
    def _():
        carry_ref[...] = jnp.zeros_like(carry_ref)

    chosen_f = jnp.where(chosen, 1.0, 0.0)
    before = carry_ref[...] + _dot(chosen_f.astype(BF16), tri_ref[...]) - chosen_f
    carry_ref[...] = carry_ref[...] + jnp.sum(chosen_f, axis=1, keepdims=True)
    cnt_ref[...] = jnp.broadcast_to(carry_ref[...], cnt_ref.shape).astype(jnp.int32)
    for k in range(TOP_K):
        hit = chosen & (rank == k)
        idx_ref[k:k + 1, :] = jnp.sum(jnp.where(hit, eid_f, 0.0), axis=0, keepdims=True).astype(jnp.int32)
        w_ref[k:k + 1, :] = jnp.sum(jnp.where(hit, wnorm, 0.0), axis=0, keepdims=True)
        rank_ref[k:k + 1, :] = jnp.sum(jnp.where(hit, before, 0.0), axis=0, keepdims=True).astype(jnp.int32)


def _route(logits_t, bias_col):
    e, n = logits_t.shape
    tn = ROW_TILE
    tri = jnp.asarray(np.triu(np.ones((tn, tn), np.float32)), BF16)
    slot = pl.BlockSpec((TOP_K, tn), lambda i: (0, i))
    return pl.pallas_call(
        _route_kernel,
        grid=(n // tn,),
        in_specs=[pl.BlockSpec((e, tn), lambda i: (0, i)), pl.BlockSpec((e, 1), lambda i: (0, 0)),
                  pl.BlockSpec((tn, tn), lambda i: (0, 0))],
        out_specs=[slot, slot, slot, pl.BlockSpec((e, LANE), lambda i: (0, 0))],
        out_shape=[
            jax.ShapeDtypeStruct((TOP_K, n), jnp.int32),
            jax.ShapeDtypeStruct((TOP_K, n), F32),
            jax.ShapeDtypeStruct((TOP_K, n), jnp.int32),
            jax.ShapeDtypeStruct((e, LANE), jnp.int32),
        ],
        scratch_shapes=[pltpu.VMEM((e, 1), F32)],
        compiler_params=_params(("arbitrary",)),
    )(logits_t, bias_col, tri)


def _dispatch_sc(pos_t, xr3, p_rows):
    k, n = pos_t.shape
    _, sub, lane = xr3.shape
    nk = k * n
    workers = SC_CORES * SC_SUBCORES
    per_worker = nk // workers
    assert nk % (workers * SC_WINDOW) == 0 and n % SC_WINDOW == 0
    mesh = plsc.VectorSubcoreMesh(core_axis_name="c", subcore_axis_name="s")

    @functools.partial(
        pl.kernel, mesh=mesh,
        out_type=jax.ShapeDtypeStruct((p_rows, sub, lane), xr3.dtype),
        scratch_types=[pltpu.VMEM((SC_WINDOW,), jnp.int32), pltpu.VMEM((SC_WINDOW, sub, lane), xr3.dtype),
                       pltpu.SemaphoreType.DMA],
    )
    def scatter_rows(x_hbm, idx_hbm, o_hbm, idx_v, rows_v, sem):
        wid = lax.axis_index("s") * SC_CORES + lax.axis_index("c")

        @pl.loop(0, per_worker // SC_WINDOW)
        def _(c):
            base = wid * per_worker + c * SC_WINDOW
            tok0 = lax.rem(base, n)
            pltpu.sync_copy(idx_hbm.at[pl.ds(base, SC_WINDOW)], idx_v)
            pltpu.sync_copy(x_hbm.at[pl.ds(tok0, SC_WINDOW)], rows_v)
            pltpu.async_copy(rows_v, o_hbm.at[idx_v], sem).wait()

    return scatter_rows(xr3, pos_t.reshape(nk))


def _combine_sc(pos_t, ys3):
    k, n = pos_t.shape
    _, sub, lane = ys3.shape
    nk = k * n
    workers = SC_CORES * SC_SUBCORES
    per_worker = nk // workers
    assert nk % (workers * SC_WINDOW) == 0
    mesh = plsc.VectorSubcoreMesh(core_axis_name="c", subcore_axis_name="s")

    @functools.partial(
        pl.kernel, mesh=mesh,
        out_type=jax.ShapeDtypeStruct((nk, sub, lane), ys3.dtype),
        scratch_types=[pltpu.VMEM((SC_WINDOW,), jnp.int32), pltpu.VMEM((SC_WINDOW, sub, lane), ys3.dtype),
                       pltpu.SemaphoreType.DMA],
    )
    def gather_rows(y_hbm, idx_hbm, o_hbm, idx_v, rows_v, sem):
        wid = lax.axis_index("s") * SC_CORES + lax.axis_index("c")

        @pl.loop(0, per_worker // SC_WINDOW)
        def _(c):
            base = wid * per_worker + c * SC_WINDOW
            pltpu.sync_copy(idx_hbm.at[pl.ds(base, SC_WINDOW)], idx_v)
            pltpu.async_copy(y_hbm.at[idx_v], rows_v, sem).wait()
            pltpu.sync_copy(rows_v, o_hbm.at[pl.ds(base, SC_WINDOW)])

    return gather_rows(ys3, pos_t.reshape(nk))


def _moe_kernel(blk_e_ref, nused_ref, nvalid_ref, xs_ref, wg_ref, wu_ref, wd_ref, o_ref):
    i = pl.program_id(0)

    @pl.when(i < nused_ref[0])
    def _():
        sub = xs_ref.shape[0] // MOE_ROWS
        owned = lax.broadcasted_iota(jnp.int32, (MOE_ROWS, LANE), 0) < nvalid_ref[i]
        words = [jnp.where(owned, xs_ref[pl.ds(c, MOE_ROWS, stride=sub), :], jnp.uint32(0)) for c in range(sub)]
        halves = [_unpack_bf16_pairs(w) for w in words]
        x = jnp.concatenate([lo for lo, _ in halves] + [hi for _, hi in halves], axis=1).astype(BF16)
        hg = _dot(x, wg_ref[0])
        hid = (hg * jax.nn.sigmoid(hg)) * _dot(x, wu_ref[0])
        y = _dot(hid.astype(BF16), wd_ref[0])
        for c in range(sub):
            o_ref[pl.ds(c, MOE_ROWS, stride=sub), :] = _pack_bf16_pairs(
                y[:, c * LANE:(c + 1) * LANE], y[:, (c + sub) * LANE:(c + sub + 1) * LANE])

    @pl.when(i >= nused_ref[0])
    def _():
        o_ref[...] = jnp.zeros_like(o_ref)


def _moe_experts(blk_e, nused, nvalid, xs, wg, wu, wd):
    d = wg.shape[1]
    sub = d // (2 * LANE)
    p = xs.shape[0] // sub
    hid = wg.shape[2]
    grid_spec = pltpu.PrefetchScalarGridSpec(
        num_scalar_prefetch=3,
        grid=(p // MOE_ROWS,),
        in_specs=[
            pl.BlockSpec((MOE_ROWS * sub, LANE), lambda i, be, nu, nv: (jnp.minimum(i, nu[0] - 1), 0)),
            pl.BlockSpec((1, d, hid), lambda i, be, nu, nv: (be[i], 0, 0)),
            pl.BlockSpec((1, d, hid), lambda i, be, nu, nv: (be[i], 0, 0)),
            pl.BlockSpec((1, hid, d), lambda i, be, nu, nv: (be[i], 0, 0)),
        ],
        out_specs=pl.BlockSpec((MOE_ROWS * sub, LANE), lambda i, be, nu, nv: (i, 0)),
    )
    return pl.pallas_call(
        _moe_kernel,
        grid_spec=grid_spec,
        out_shape=jax.ShapeDtypeStruct((p * sub, LANE), jnp.uint32),
        compiler_params=_params(("arbitrary",)),
    )(blk_e, nused, nvalid, xs, wg, wu, wd)


def _side_kernel(xn_ref, p_ref, wsg, wsu, wsd, wp, sh_ref, pe_ref):
    xn = xn_ref[...]
    hg = _dot(xn, wsg[...])
    sh_ref[...] = _dot(((hg * jax.nn.sigmoid(hg)) * _dot(xn, wsu[...])).astype(BF16), wsd[...]).astype(sh_ref.dtype)
    pe_ref[...] = _dot(p_ref[...].astype(BF16), wp[...]).astype(pe_ref.dtype)


def _side(xn2, p2, wsg, wsu, wsd, wp):
    n, d = xn2.shape
    tm = ROW_TILE
    row = lambda w: pl.BlockSpec((tm, w), lambda i: (i, 0))
    full = lambda a: pl.BlockSpec(a.shape, lambda i: (0,) * a.ndim)
    return pl.pallas_call(
        _side_kernel,
        grid=(n // tm,),
        in_specs=[row(d), row(p2.shape[1]), full(wsg), full(wsu), full(wsd), full(wp)],
        out_specs=[row(d), row(d)],
        out_shape=[jax.ShapeDtypeStruct((n, d), BF16)] * 2,
        compiler_params=_params(("parallel",)),
    )(xn2, p2, wsg, wsu, wsd, wp)


def _final_kernel(h_ref, sh_ref, pe_ref, ys_ref, w_ref, gp, wpg, gfin, o_ref):
    w = w_ref[...]
    tm = w.shape[0]
    sub = ys_ref.shape[1] // tm
    parts = [None] * (2 * sub)
    for k in range(TOP_K):
        wk = jnp.broadcast_to(w[:, k:k + 1], (tm, LANE))
        for c in range(sub):
            lo, hi = _unpack_bf16_pairs(ys_ref[k, pl.ds(c, tm, stride=sub), :])
            parts[c] = wk * lo if k == 0 else parts[c] + wk * lo
            parts[c + sub] = wk * hi if k == 0 else parts[c + sub] + wk * hi
    y = jnp.concatenate(parts, axis=1)
    h = h_ref[...] + (y + sh_ref[...].astype(F32))
    gate = jax.nn.sigmoid(_dot(_rms(h, gp[...]).astype(BF16), wpg[...]))
    h = h + pe_ref[...].astype(F32) * gate
    o_ref[...] = _rms(h, gfin[...])


def _final(h1, shared, ple, ysg, w_nk, gp, wpg, gfin):
    n, d = h1.shape
    tm = ROW_TILE
    row = lambda w: pl.BlockSpec((tm, w), lambda i: (i, 0))
    full = lambda a: pl.BlockSpec(a.shape, lambda i: (0,) * a.ndim)
    return pl.pallas_call(
        _final_kernel,
        grid=(n // tm,),
        in_specs=[row(d), row(d), row(d), pl.BlockSpec((TOP_K, tm * ysg.shape[1] // n, LANE), lambda i: (0, i, 0)),
                  row(TOP_K), full(gp), full(wpg), full(gfin)],
        out_specs=row(d),
        out_shape=jax.ShapeDtypeStruct((n, d), F32),
        compiler_params=_params(("parallel",)),
    )(h1, shared, ple, ysg, w_nk, gp, wpg, gfin)


def _pad_heads(w):
    d, c = w.shape
    nh = c // HEAD_DIM
    w = w.reshape(d, nh, HEAD_DIM)
    return jnp.pad(w, ((0, 0), (0, 0), (0, LANE - HEAD_DIM))).reshape(d, nh * LANE)


def _alibi_slopes(n):
    return 2.0 ** (-8.0 * jnp.arange(1, n + 1, dtype=jnp.float32) / n)


def kernel(x, p, g_mix, w_in, cmp_pos_k, cmp_w1_k, cmp_b1_k, cmp_w2_k, cmp_pos_v, cmp_w1_v, cmp_b1_v, cmp_w2_v, w_branch_a, w_branch_b, w_out, g_ffn, w_router, router_bias, w_gate, w_up, w_down, ws_gate, ws_up, ws_down, g_ple, w_ple_gate, w_ple, g_final):
    B, S, D = x.shape
    N = B * S
    G, R = NSA_KV_HEADS, NSA_REP
    assert w_in.shape[0] == 1 and p.shape[0] == 1
    assert S % DIL_STEP_ROWS == 0 and S % NSA_TK == 0 and S >= WIN_SIZE + NSA_Q
    assert N % ROW_TILE == 0 and N % INPROJ_ROWS == 0 and S % INPROJ_ROWS == 0
    slopes = _alibi_slopes(DIL_HEADS + NSA_HEADS)
    slopes_a, slopes_b = slopes[0::2], slopes[1::2]

    offs = np.concatenate([[0], np.cumsum(IN_SIZES)])
    parts = [w_in[0][:, offs[i]:offs[i + 1]] for i in range(len(IN_SIZES))]
    w_qa, w_ka, w_va, w_qb, w_kc, w_vc, w_ks, w_vs, w_kw, w_vw, w_gn, w_ga, w_gb = parts
    w_qkv_a = jnp.concatenate([w_qa * HEAD_DIM ** -0.5, w_ka, w_va], axis=1).astype(BF16)
    w_nsa = jnp.concatenate([_pad_heads(t) for t in (w_qb, w_kc, w_vc, w_ks, w_vs, w_kw, w_vw)],
                            axis=1).astype(BF16)
    nsa_w = w_nsa.shape[1]
    w_gn = w_gn.reshape(D, 3, G, R).transpose(0, 2, 1, 3).reshape(D, G, 3 * R)
    w_gn = jnp.pad(w_gn, ((0, 0), (0, 0), (0, LANE - 3 * R))).reshape(D, G * LANE)
    w_gate_in = jnp.concatenate([w_ga, w_gb, w_gn], axis=1).astype(BF16)
    gate_w = w_gate_in.shape[1]

    x2 = x.reshape(N, D)
    dils = tuple(dil for _, dil in DIL_PATTERNS if dil > 1)
    qkv_a, *dilated_qkv = _inproj(x2, g_mix, w_qkv_a, BF16, 512, dils=dils)
    nsa_in, = _inproj(x2, g_mix, w_nsa, BF16, 512)
    gt, = _inproj(x2, g_mix, w_gate_in, F32, 768)

    o_list, l_list = [], []
    for _, dil in DIL_PATTERNS:
        src = qkv_a if dil == 1 else dilated_qkv[dils.index(dil)]
        o, l = _dilated(src.reshape(B, S // dil, -1), _dil_bias(slopes_a, dil), dil, B, S)
        o_list.append(o.reshape(-1, N, LANE))
        l_list.append(l.reshape(-1, N, LANE))

    nchunk = S // CMP_STRIDE
    c0 = NSA_HEADS * LANE
    raw = nsa_in[:, c0:c0 + 2 * G * LANE].reshape(B, nchunk, CMP_STRIDE, 2 * G, LANE)[..., :HEAD_DIM]
    xc = raw.transpose(0, 3, 1, 2, 4).reshape(B, 2 * G, nchunk, CMP_STRIDE * HEAD_DIM)
    pos = jnp.stack([cmp_pos_k[0].reshape(1, -1), cmp_pos_v[0].reshape(1, -1)])
    w1 = jnp.stack([cmp_w1_k[0], cmp_w1_v[0]]).astype(BF16)
    b1 = jnp.stack([cmp_b1_k[0][None], cmp_b1_v[0][None]])
    w2 = jnp.pad(jnp.stack([cmp_w2_k[0], cmp_w2_v[0]]), ((0, 0), (0, 0), (0, LANE - HEAD_DIM))).astype(BF16)
    kcv = _compress(xc, pos, w1, b1, w2)
    kcv = kcv.reshape(B, 2, G, nchunk, LANE).transpose(1, 0, 2, 3, 4).reshape(2, B * G, nchunk, LANE)
    vct = kcv[1].transpose(0, 2, 1)
    v0 = (NSA_HEADS + 3 * G) * LANE
    vsw = jnp.stack([nsa_in[:, v0:v0 + G * LANE], nsa_in[:, v0 + 2 * G * LANE:v0 + 3 * G * LANE]])
    vsw = vsw.reshape(2, B, S // LANE, LANE, G, LANE).transpose(1, 4, 0, 2, 5, 3)
    ones_row = (jnp.arange(LANE) == SUM_LANE).astype(BF16)[:, None]
    vt = (vsw + ones_row).reshape(B * G, 2, S // LANE, LANE, LANE)
    yb = _nsa(slopes_b, nsa_in.reshape(B, S, nsa_w), gt.reshape(B, S, gate_w), kcv[0], vct, vt,
              B, S, 0, 2 * D // LANE)
    yb = yb.reshape(N, NSA_HEADS * LANE)

    wb_pad = jnp.pad(w_branch_b[0].reshape(NSA_HEADS, HEAD_DIM, D),
                     ((0, 0), (0, LANE - HEAD_DIM), (0, 0))).reshape(NSA_HEADS * LANE, D)
    wr_t = w_router[0].T
    wr_hi = wr_t.astype(BF16)
    wr_lo = (wr_t - wr_hi.astype(F32)).astype(BF16)
    h1, xn2, logits_t, xr = _merge(o_list, l_list, yb, gt, x2, w_branch_a[0].astype(BF16), wb_pad.astype(BF16),
                               w_out[0].astype(BF16), g_ffn, wr_hi, wr_lo)

    shared, ple = _side(xn2, p[0].reshape(N, -1), ws_gate[0].astype(BF16), ws_up[0].astype(BF16),
                        ws_down[0].astype(BF16), w_ple[0].astype(BF16))

    idx_t, wsel_t, rank_t, counts = _route(logits_t, router_bias[0][:, None])
    counts = counts[:, 0]
    padded = (counts + MOE_ROWS - 1) // MOE_ROWS * MOE_ROWS
    pend = jnp.cumsum(padded)
    pstart = pend - padded
    hot = idx_t[None] == jnp.arange(N_EXPERTS, dtype=jnp.int32)[:, None, None]
    pos_t = rank_t + jnp.sum(jnp.where(hot, pstart.astype(jnp.int32)[:, None, None], 0), axis=0)
    NK = N * TOP_K
    nblk = (NK + N_EXPERTS * (MOE_ROWS - 1) + MOE_ROWS - 1) // MOE_ROWS
    P = nblk * MOE_ROWS
    blk_e = jnp.minimum(jnp.sum(pend[None, :] <= (jnp.arange(nblk, dtype=jnp.int32) * MOE_ROWS)[:, None], axis=1),
                        N_EXPERTS - 1).astype(jnp.int32)
    nused = (pend[-1] // MOE_ROWS).astype(jnp.int32).reshape(1)
    row0 = jnp.arange(nblk, dtype=jnp.int32) * MOE_ROWS
    nvalid = jnp.clip((pstart + counts)[blk_e] - row0, 0, MOE_ROWS).astype(jnp.int32)
    sub = D // (2 * LANE)
    xs = _dispatch_sc(pos_t, xr.reshape(N, sub, LANE), P).reshape(P * sub, LANE)
    ys = _moe_experts(blk_e, nused, nvalid, xs, w_gate[0].astype(BF16), w_up[0].astype(BF16), w_down[0].astype(BF16))
    ysg = _combine_sc(pos_t, ys.reshape(P, sub, LANE)).reshape(TOP_K, N * sub, LANE)
    out = _final(h1, shared, ple, ysg, wsel_t.T, g_ple, w_ple_gate[0].astype(BF16), g_final[None, :])
    return out.reshape(B, S, D)
```

```python
import functools

import numpy as np
import jax
import jax.numpy as jnp
from jax import lax
from jax.experimental import pallas as pl
from jax.experimental.pallas import tpu as pltpu
from jax.experimental.pallas import tpu_sc as plsc

F32 = jnp.float32
BF16 = jnp.bfloat16

HEAD_DIM = 64
LANE = 128
RMS_EPS = 1e-6
DIL_HEADS = 8
DIL_PATTERNS = ((128, 1), (512, 4), (2048, 16))
DIL_BLOCK = 128
DIL_STEP_ROWS = 4096
NSA_HEADS = 8
NSA_KV_HEADS = 2
NSA_REP = NSA_HEADS // NSA_KV_HEADS
CMP_BLOCK = 32
CMP_STRIDE = 16
CMP_HIDDEN = 256
SLC_BLOCK = 64
SLC_TOP_N = 16
SLC_FORCE = 1e9
WIN_SIZE = 512
N_EXPERTS = 64
TOP_K = 8
N_GROUPS = 8
TOP_GROUPS = 4
GROUP_SIZE = N_EXPERTS // N_GROUPS
ROUTED_SCALE = 2.5
IN_SIZES = (512, 512, 512, 512, 128, 128, 128, 128, 128, 128, 24, 1024, 1024)

NEG = -1e30
NT_DIMS = (((1,), (1,)), ((), ()))

ROW_TILE = 512
INPROJ_ROWS = 1024
NSA_Q = 256
NSA_TK = 512
MOE_ROWS = 2048
VMEM_LIMIT = 48 * 1024 * 1024
SC_CORES = 2
SC_SUBCORES = 16
SC_WINDOW = 128


def _dot(a, b):
    return jnp.dot(a, b, preferred_element_type=F32)


def _dot_nt(a, b):
    return lax.dot_general(a, b, NT_DIMS, preferred_element_type=F32)


def _split3(x):
    hi = x.astype(BF16)
    r1 = x - hi.astype(F32)
    mid = r1.astype(BF16)
    lo = (r1 - mid.astype(F32)).astype(BF16)
    return hi, mid, lo


def _pack_bf16_pairs(lo, hi):
    lo_bits = pltpu.bitcast(lo.astype(BF16).astype(F32), jnp.uint32)
    hi_bits = pltpu.bitcast(hi.astype(BF16).astype(F32), jnp.uint32)
    return hi_bits | (lo_bits >> 16)


def _unpack_bf16_pairs(words):
    return (pltpu.bitcast(words << 16, F32), pltpu.bitcast(words & jnp.uint32(0xFFFF0000), F32))


def _rms(x, g):
    r = lax.rsqrt(jnp.mean(x * x, axis=-1, keepdims=True) + RMS_EPS)
    return (x * r) * g


def _params(sem):
    return pltpu.CompilerParams(dimension_semantics=sem, vmem_limit_bytes=VMEM_LIMIT)


def _inproj_kernel(x_ref, g_ref, w_ref, o_ref, *rest, dils):
    dil_refs, xn_ref = rest[:len(dils)], rest[len(dils)]
    res_ref = rest[len(dils) + 1] if dils else None

    @pl.when(pl.program_id(1) == 0)
    def _():
        xn_ref[...] = _rms(x_ref[...], g_ref[...]).astype(BF16)

    res = _dot(xn_ref[...], w_ref[...])
    o_ref[...] = res.astype(o_ref.dtype)

    if dils:
        tn = res.shape[1]
        for c in range(tn // LANE):
            res_ref[c] = res[:, c * LANE:(c + 1) * LANE]
        for d_ref, dil in zip(dil_refs, dils):
            for r in range(dil):
                for c in range(tn // LANE):
                    piece = res_ref[c, pl.ds(r, INPROJ_ROWS // dil, stride=dil), :]
                    d_ref[:, r * tn + c * LANE:r * tn + (c + 1) * LANE] = piece.astype(d_ref.dtype)


def _inproj(x2, g, w, out_dtype, tn, dils=()):
    n, d = x2.shape
    cols = w.shape[1]
    out_specs = [pl.BlockSpec((INPROJ_ROWS, tn), lambda i, j: (i, j))]
    out_shape = [jax.ShapeDtypeStruct((n, cols), out_dtype)]
    for dil in dils:
        out_specs.append(pl.BlockSpec((INPROJ_ROWS // dil, dil * tn), lambda i, j: (i, j)))
        out_shape.append(jax.ShapeDtypeStruct((n // dil, cols * dil), out_dtype))
    return pl.pallas_call(
        functools.partial(_inproj_kernel, dils=dils),
        grid=(n // INPROJ_ROWS, cols // tn),
        in_specs=[
            pl.BlockSpec((INPROJ_ROWS, d), lambda i, j: (i, 0)),
            pl.BlockSpec((1, d), lambda i, j: (0, 0)),
            pl.BlockSpec((d, tn), lambda i, j: (0, j)),
        ],
        out_specs=out_specs,
        out_shape=out_shape,
        scratch_shapes=[pltpu.VMEM((INPROJ_ROWS, d), BF16)]
        + ([pltpu.VMEM((tn // LANE, INPROJ_ROWS, LANE), F32)] if dils else []),
        compiler_params=_params(("parallel", "arbitrary")),
    )(x2, g, w)


def _dil_kernel(q_ref, kp_ref, kc_ref, vp_ref, vc_ref, bias_ref, o_ref, l_ref, *, dil, nq):
    i = pl.program_id(1)
    res = pl.program_id(2)
    blk = DIL_BLOCK
    lane = lax.broadcasted_iota(jnp.int32, (blk, LANE), 1)
    lo = lane < HEAD_DIM
    for sub in range(nq):
        rows = slice(sub * blk, (sub + 1) * blk)
        prev = slice((sub - 1) * blk, sub * blk)
        out_rows = pl.ds(sub * blk * dil + res, blk, stride=dil) if dil > 1 else rows
        variant = jnp.minimum(i, 1) * DIL_HEADS if sub == 0 else DIL_HEADS
        for hp in range(DIL_HEADS // 2):
            sl = slice(hp * LANE, (hp + 1) * LANE)
            q2 = q_ref[0, rows, sl]
            k_prev = kp_ref[0, :, sl] if sub == 0 else kc_ref[0, prev, sl]
            v_prev = vp_ref[0, :, sl] if sub == 0 else vc_ref[0, prev, sl]
            k2 = jnp.concatenate([k_prev, kc_ref[0, rows, sl]], axis=0)
            v2 = jnp.concatenate([v_prev, vc_ref[0, rows, sl]], axis=0)
            outs, lses = [], []
            for hh in range(2):
                keep = lo if hh == 0 else jnp.logical_not(lo)
                qm = jnp.where(keep, q2, jnp.zeros_like(q2))
                s = _dot_nt(qm, k2) + bias_ref[variant + 2 * hp + hh]
                m = jnp.max(s, axis=-1, keepdims=True)
                e = jnp.exp(s - m)
                den = jnp.sum(e, axis=-1, keepdims=True)
                outs.append(_dot(e.astype(BF16), v2) / den)
                lses.append(jnp.broadcast_to(m + jnp.log(den), (blk, LANE)))
            o_ref[hp, 0, out_rows, :] = jnp.where(lo, outs[0], outs[1])
            l_ref[hp, 0, out_rows, :] = jnp.where(lo, lses[0], lses[1])


def _dil_bias(slopes_a, dil):
    blk = DIL_BLOCK
    qi = jnp.arange(blk)[:, None]
    kj = jnp.arange(2 * blk)[None, :]
    dist = (qi + blk - kj).astype(F32)
    band = (dist >= 0) & (dist <= blk)
    slope = (slopes_a * dil)[:, None, None]
    bias = -slope * dist
    first = jnp.where(band & (kj >= blk), bias, NEG)
    rest = jnp.where(band, bias, NEG)
    return jnp.concatenate([first, rest], axis=0)


def _dilated(src3, bias, dil, b, s):
    nq = min(8, DIL_STEP_ROWS // (DIL_BLOCK * dil))
    nstep = s // dil // (DIL_BLOCK * nq)
    aw = DIL_HEADS * HEAD_DIM
    npair = DIL_HEADS // 2

    def spec(which, prev):
        if prev:
            return pl.BlockSpec((1, DIL_BLOCK, aw), lambda bb, i, r: (bb, jnp.maximum(nq * i - 1, 0), which * dil + r))
        return pl.BlockSpec((1, DIL_BLOCK * nq, aw), lambda bb, i, r: (bb, i, which * dil + r))

    out_spec = pl.BlockSpec((npair, 1, DIL_BLOCK * nq * dil, LANE), lambda bb, i, r: (0, bb, i, 0))
    return pl.pallas_call(
        functools.partial(_dil_kernel, dil=dil, nq=nq),
        grid=(b, nstep, dil),
        in_specs=[
            spec(0, False), spec(1, True), spec(1, False), spec(2, True), spec(2, False),
            pl.BlockSpec(bias.shape, lambda bb, i, r: (0, 0, 0)),
        ],
        out_specs=[out_spec, out_spec],
        out_shape=[jax.ShapeDtypeStruct((npair, b, s, LANE), F32)] * 2,
        compiler_params=_params(("parallel", "arbitrary", "arbitrary")),
    )(src3, src3, src3, src3, src3, bias)


def _cmp_kernel(x_ref, pos_ref, w1_ref, b1_ref, w2_ref, o_ref):
    half = x_ref.shape[-1]
    x = x_ref[0, 0]
    w1 = w1_ref[0]
    first = _dot(x, w1[:half])
    second = _dot(x, w1[half:])
    nrow = x.shape[0]
    pre = first + pltpu.roll(second, nrow - 1, 0)
    posb = _dot(jnp.broadcast_to(pos_ref[0], (8, 2 * half)).astype(BF16), w1)[0:1]
    pre = pre + posb + b1_ref[0]
    o_ref[0, 0] = _dot(jax.nn.gelu(pre).astype(BF16), w2_ref[0]).astype(o_ref.dtype)


def _compress(xc, pos, w1, b1, w2):
    b, four, nchunk, width = xc.shape
    return pl.pallas_call(
        _cmp_kernel,
        grid=(b, four),
        in_specs=[
            pl.BlockSpec((1, 1, nchunk, width), lambda bb, j: (bb, j, 0, 0)),
            pl.BlockSpec((1, 1, 2 * width), lambda bb, j: (j // 2, 0, 0)),
            pl.BlockSpec((1, 2 * width, CMP_HIDDEN), lambda bb, j: (j // 2, 0, 0)),
            pl.BlockSpec((1, 1, CMP_HIDDEN), lambda bb, j: (j // 2, 0, 0)),
            pl.BlockSpec((1, CMP_HIDDEN, LANE), lambda bb, j: (j // 2, 0, 0)),
        ],
        out_specs=pl.BlockSpec((1, 1, nchunk, LANE), lambda bb, j: (bb, j, 0, 0)),
        out_shape=jax.ShapeDtypeStruct((b, four, nchunk, LANE), BF16),
        compiler_params=_params(("parallel", "parallel")),
    )(xc, pos, w1, b1, w2)


POS_LANE = HEAD_DIM
POS_SPLIT = 64
SUM_LANE = HEAD_DIM
MASKED = -1e30
ROW_FLOOR = -1e29


def _pos_columns(pos):
    out = np.zeros((pos.shape[0], LANE), np.float32)
    out[:, POS_LANE] = pos // POS_SPLIT
    out[:, POS_LANE + 1] = pos % POS_SPLIT
    out[:, POS_LANE + 2] = 1.0
    out[:, POS_LANE + 3] = 1.0
    return jnp.asarray(out, BF16)


def _nsa_kernel(slope_ref, q_ref, kc_ref, vct_ref, ks_ref, kw_ref, vt_ref, gt_ref, ov_ref, kpa_ref, cpa_ref,
                oh_ref, o_ref, kaug_ref, flag_ref, list_ref, sbuf_ref, m_ref, acc_ref, *, seq):
    g = pl.program_id(1)
    c = pl.program_id(2)
    C = NSA_Q
    R = NSA_REP
    TK = NSA_TK
    rows = R * C
    t0 = c * C
    scale = HEAD_DIM ** -0.5
    nb = seq // SLC_BLOCK
    ncmp = seq // CMP_STRIDE
    n_sel = min(SLC_TOP_N, nb)
    vblocks = TK // LANE

    @pl.when(c == 0)
    def _():
        kaug_ref[:, 0:LANE] = ks_ref[0] + kpa_ref[...]
        kaug_ref[:, LANE:2 * LANE] = oh_ref[...]

    row = lax.broadcasted_iota(jnp.int32, (rows, 1), 0)
    ti = t0 + row % C
    slope = jnp.zeros((rows, 1), F32)
    for r in range(R):
        slope = jnp.where(row // C == r, slope_ref[g * R + r], slope)
    lane = lax.broadcasted_iota(jnp.int32, (rows, LANE), 1)
    thi = (ti // POS_SPLIT).astype(F32)
    tlo = (ti % POS_SPLIT).astype(F32)
    qcols = jnp.where(lane == POS_LANE, POS_SPLIT * slope,
                      jnp.where(lane == POS_LANE + 1, slope,
                                jnp.where(lane == POS_LANE + 2, -POS_SPLIT * slope * thi,
                                          jnp.where(lane == POS_LANE + 3, -slope * tlo, 0.0))))
    q4 = jnp.concatenate([q_ref[0, :, r * LANE:(r + 1) * LANE] for r in range(R)], axis=0)
    q4 = (q4.astype(F32) * scale + qcols).astype(BF16)
    tq = t0 + lax.broadcasted_iota(jnp.int32, (1, C), 1)

    def heads(x):
        return jnp.concatenate([x] * R, axis=1)

    kc = kc_ref[0] + cpa_ref[...]
    ci = lax.broadcasted_iota(jnp.int32, (ncmp, 1), 0)
    vis = (ci * CMP_STRIDE + (CMP_BLOCK - 1) <= tq) & (ci < ncmp - 1)
    s = _dot_nt(kc, q4) + heads(jnp.where(vis, 0.0, MASKED))
    m = jnp.maximum(jnp.max(s, axis=0, keepdims=True), ROW_FLOOR)
    e = jnp.exp(s - m)
    den = jnp.sum(e, axis=0, keepdims=True)
    pc = e * (1.0 / jnp.where(den > 0, den, 1.0))
    o_cmp = _dot(vct_ref[0], pc.astype(BF16))
    pcsum = pc[:, 0:C]
    for r in range(1, R):
        pcsum = pcsum + pc[:, r * C:(r + 1) * C]
    ov = ov_ref[...]
    p_slc = sum(_dot(ov, part) for part in _split3(pcsum))

    def values_t(which, first_block, n):
        return jnp.concatenate([vt_ref[0, which, first_block + i] for i in range(n)], axis=1)

    span = WIN_SIZE + C
    ws = pl.multiple_of(jnp.maximum(t0 - WIN_SIZE, 0), C)
    k_w = kw_ref[0, pl.ds(ws, span), :] + kpa_ref[pl.ds(ws, span), :]
    wpos = ws + lax.broadcasted_iota(jnp.int32, (span, 1), 0)
    wbias = jnp.where((wpos <= tq) & (wpos >= tq - (WIN_SIZE - 1)), 0.0, MASKED)
    s3 = _dot_nt(k_w, q4) + heads(wbias)
    e3 = jnp.exp(s3 - jnp.max(s3, axis=0, keepdims=True))
    ow = _dot(values_t(1, ws // LANE, span // LANE), e3.astype(BF16))
    o_win = ow / ow[SUM_LANE:SUM_LANE + 1, :]

    j = lax.broadcasted_iota(jnp.int32, (LANE, C), 0)
    cur = tq // SLC_BLOCK
    forced = (j == 0) | (j == cur) | (j == cur - 1)
    future = j * SLC_BLOCK > tq
    work = jnp.where(forced, -jnp.inf, jnp.where(future, -SLC_FORCE, p_slc))
    work = jnp.where(j < nb, work, -jnp.inf)
    sel = jnp.where(forced, 1.0, 0.0)
    jf = j.astype(F32)
    for _ in range(n_sel - 3):
        mx = jnp.max(work, axis=0, keepdims=True)
        first = jnp.min(jnp.where(work == mx, jf, float(LANE)), axis=0, keepdims=True)
        pick = jf == first
        sel = jnp.where(pick, 1.0, sel)
        work = jnp.where(pick, -jnp.inf, work)
    sel = jnp.where(future, 0.0, sel)
    blocks_per_tile = TK // SLC_BLOCK
    for kt in range(seq // TK):
        flag_ref[kt] = jnp.max(sel[kt * blocks_per_tile:(kt + 1) * blocks_per_tile, :])
    sel_bias = jnp.where(sel.T > 0.5, 0.0, MASKED).astype(BF16)
    q_aug = jnp.concatenate([q4, jnp.concatenate([sel_bias] * R, axis=0)], axis=1)

    m_ref[...] = jnp.full((1, rows), ROW_FLOOR, F32)
    acc_ref[...] = jnp.zeros((LANE, rows), F32)
    kio = lax.broadcasted_iota(jnp.int32, (TK, 1), 0)

    def scores(kt):
        start = pl.multiple_of(kt * TK, TK)
        return _dot_nt(kaug_ref[pl.ds(start, TK), :], q_aug)

    def absorb(s2, kt):
        m_i = m_ref[...]
        m_n = jnp.maximum(m_i, jnp.max(s2, axis=0, keepdims=True))
        alpha = jnp.exp(m_i - m_n)
        p2 = jnp.exp(s2 - m_n)
        acc_ref[...] = alpha * acc_ref[...] + _dot(values_t(0, kt * vblocks, vblocks), p2.astype(BF16))
        m_ref[...] = m_n

    last = (t0 + C - 1) // TK

    def compact(kt, n):
        list_ref[n] = kt
        return n + (flag_ref[kt] > 0.0).astype(jnp.int32)

    n_vis = lax.fori_loop(0, last, compact, 0)
    list_ref[n_vis] = last

    sbuf_ref[0] = scores(list_ref[0])

    def slc_body(j, carry):
        s_next = scores(list_ref[j + 1])
        absorb(sbuf_ref[j % 2], list_ref[j])
        sbuf_ref[(j + 1) % 2] = s_next
        return carry

    lax.fori_loop(0, n_vis, slc_body, 0)
    start = pl.multiple_of(last * TK, TK)
    absorb(sbuf_ref[n_vis % 2] + heads(jnp.where(start + kio <= tq, 0.0, MASKED)), last)
    acc = acc_ref[...]
    o_slc = acc / acc[SUM_LANE:SUM_LANE + 1, :]

    gates = jax.nn.sigmoid(gt_ref[0]).T
    for r in range(R):
        rs = slice(r * C, (r + 1) * C)
        out = (gates[r:r + 1] * o_cmp[:, rs] + gates[R + r:R + r + 1] * o_slc[:, rs]
               + gates[2 * R + r:2 * R + r + 1] * o_win[:, rs])
        o_ref[0, :, r * LANE:(r + 1) * LANE] = out.T.astype(o_ref.dtype)


def _nsa(slopes_b, pa3, gt3, kc, vct, vt, b, s, col0, gcol0):
    C = NSA_Q
    R = NSA_REP
    G = NSA_KV_HEADS
    nb = s // SLC_BLOCK
    ncmp = s // CMP_STRIDE
    assert nb <= LANE
    qc0 = col0 // R
    kv0 = col0 + NSA_HEADS
    rows = R * C

    ci = np.arange(ncmp)[None, :]
    sj = np.arange(LANE)[:, None]
    ov = ((ci * CMP_STRIDE < (sj + 1) * SLC_BLOCK) & (ci * CMP_STRIDE + CMP_BLOCK > sj * SLC_BLOCK)
          & (ci < ncmp - 1) & (sj < nb))
    ov = jnp.asarray(ov, BF16)
    kpa = _pos_columns(np.arange(s))
    cpa = _pos_columns(np.arange(ncmp) * CMP_STRIDE + (CMP_BLOCK - 1))
    oh = jnp.asarray(np.arange(s)[:, None] // SLC_BLOCK == np.arange(LANE)[None, :], BF16)

    def k_spec(which):
        return pl.BlockSpec((1, s, LANE), lambda bb, g, c: (bb, 0, kv0 + which * G + g))

    const = lambda a: pl.BlockSpec(a.shape, lambda bb, g, c: (0,) * a.ndim)
    return pl.pallas_call(
        functools.partial(_nsa_kernel, seq=s),
        grid=(b, G, s // C),
        in_specs=[
            pl.BlockSpec(memory_space=pltpu.SMEM),
            pl.BlockSpec((1, C, R * LANE), lambda bb, g, c: (bb, c, qc0 + g)),
            pl.BlockSpec((1, ncmp, LANE), lambda bb, g, c: (bb * G + g, 0, 0)),
            pl.BlockSpec((1, LANE, ncmp), lambda bb, g, c: (bb * G + g, 0, 0)),
            k_spec(2), k_spec(4),
            pl.BlockSpec((1, 2, s // LANE, LANE, LANE), lambda bb, g, c: (bb * G + g, 0, 0, 0, 0)),
            pl.BlockSpec((1, C, LANE), lambda bb, g, c: (bb, c, gcol0 + g)),
            const(ov), const(kpa), const(cpa), const(oh),
        ],
        out_specs=pl.BlockSpec((1, C, R * LANE), lambda bb, g, c: (bb, c, g)),
        out_shape=jax.ShapeDtypeStruct((b, s, NSA_HEADS * LANE), BF16),
        scratch_shapes=[
            pltpu.VMEM((s, 2 * LANE), BF16),
            pltpu.SMEM((s // NSA_TK,), F32),
            pltpu.SMEM((s // NSA_TK,), jnp.int32),
            pltpu.VMEM((2, NSA_TK, rows), F32),
            pltpu.VMEM((1, rows), F32),
            pltpu.VMEM((LANE, rows), F32),
        ],
        compiler_params=_params(("parallel", "parallel", "arbitrary")),
    )(slopes_b, pa3, kc, vct, pa3, pa3, vt, gt3, ov, kpa, cpa, oh)


def _merge_kernel(o1, o2, o3, l1, l2, l3, yb, ga, gb, x_ref, wa, wb, wo, gf, wrh, wrl,
                  h_ref, xn_ref, lg_ref, xr_ref):
    def pairs(ref):
        return jnp.concatenate([ref[h] for h in range(ref.shape[0])], axis=1)

    a1, a2, a3 = pairs(l1), pairs(l2), pairs(l3)
    mx = jnp.maximum(jnp.maximum(a1, a2), a3)
    e1, e2, e3 = jnp.exp(a1 - mx), jnp.exp(a2 - mx), jnp.exp(a3 - mx)
    tot = e1 + e2 + e3
    ya = (e1 / tot) * pairs(o1) + (e2 / tot) * pairs(o2) + (e3 / tot) * pairs(o3)
    merged = (jax.nn.sigmoid(ga[...]) * _dot(ya.astype(BF16), wa[...])
              + jax.nn.sigmoid(gb[...]) * _dot(yb[...], wb[...]))
    h = x_ref[...] + _dot(merged.astype(BF16), wo[...])
    h_ref[...] = h
    xn = _rms(h, gf[...])
    xn_ref[...] = xn.astype(BF16)
    tm, d = xn.shape
    half = d // (2 * LANE)

    for c in range(half):
        xr_ref[pl.ds(c, tm, stride=half), :] = _pack_bf16_pairs(
            xn[:, c * LANE:(c + 1) * LANE], xn[:, (c + half) * LANE:(c + half + 1) * LANE])
    xh = xn.astype(BF16)
    xl = (xn - xh.astype(F32)).astype(BF16)
    lg_ref[...] = _dot_nt(wrh[...], xh) + _dot_nt(wrh[...], xl) + _dot_nt(wrl[...], xh)


def _merge(o_list, l_list, yb, gt, x2, wa, wb, wo, gf, wrh, wrl):
    n, d = x2.shape
    aw = wa.shape[0]
    bw = wb.shape[0]
    tm = ROW_TILE
    row = lambda w: pl.BlockSpec((tm, w), lambda i: (i, 0))
    full = lambda a: pl.BlockSpec(a.shape, lambda i: (0,) * a.ndim)
    return pl.pallas_call(
        _merge_kernel,
        grid=(n // tm,),
        in_specs=[pl.BlockSpec((aw // LANE, tm, LANE), lambda i: (0, i, 0))] * 6 + [
            row(bw),
            pl.BlockSpec((tm, d), lambda i: (i, 0)),
            pl.BlockSpec((tm, d), lambda i: (i, 1)),
            row(d), full(wa), full(wb), full(wo), full(gf), full(wrh), full(wrl),
        ],
        out_specs=[row(d), row(d), pl.BlockSpec((N_EXPERTS, tm), lambda i: (0, i)),
                   pl.BlockSpec((tm * d // (2 * LANE), LANE), lambda i: (i, 0))],
        out_shape=[
            jax.ShapeDtypeStruct((n, d), F32),
            jax.ShapeDtypeStruct((n, d), BF16),
            jax.ShapeDtypeStruct((N_EXPERTS, n), F32),
            jax.ShapeDtypeStruct((n * d // (2 * LANE), LANE), jnp.uint32),
        ],
        compiler_params=_params(("parallel",)),
    )(*o_list, *l_list, yb, gt, gt, x2, wa, wb, wo, gf, wrh, wrl)


def _route_kernel(lg_ref, bias_ref, tri_ref, idx_ref, w_ref, rank_ref, cnt_ref, carry_ref):
    E = N_EXPERTS
    tn = lg_ref.shape[1]
    scores = jax.nn.sigmoid(lg_ref[...])
    sel = scores + bias_ref[...]
    sub = lax.broadcasted_iota(jnp.int32, (GROUP_SIZE, tn), 0).astype(F32)
    gs = []
    for grp in range(N_GROUPS):
        blk = sel[grp * GROUP_SIZE:(grp + 1) * GROUP_SIZE]
        m1 = jnp.max(blk, axis=0, keepdims=True)
        first = jnp.min(jnp.where(blk == m1, sub, float(GROUP_SIZE)), axis=0, keepdims=True)
        m2 = jnp.max(jnp.where(sub == first, -jnp.inf, blk), axis=0, keepdims=True)
        gs.append(m1 + m2)
    keep = []
    for a in range(N_GROUPS):
        rank = jnp.zeros((1, tn), jnp.int32)
        for o in range(N_GROUPS):
            if o == a:
                continue
            beats = (gs[o] >= gs[a]) if o < a else (gs[o] > gs[a])
            rank = rank + beats.astype(jnp.int32)
        keep.append(jnp.broadcast_to(rank < TOP_GROUPS, (GROUP_SIZE, tn)))
    emask = jnp.concatenate(keep, axis=0)
    cand = jnp.where(emask, sel, -jnp.inf)
    eid = lax.broadcasted_iota(jnp.int32, (E, tn), 0)
    eid_f = eid.astype(F32)
    rank = jnp.zeros((E, tn), jnp.int32)
    for o in range(E):
        other = cand[o:o + 1]
        beats = (other > cand) | ((other == cand) & (eid > o))
        rank = rank + beats.astype(jnp.int32)
    chosen = emask & (rank < TOP_K)
    wsel = jnp.where(chosen, scores, 0.0)
    wnorm = wsel / jnp.sum(wsel, axis=0, keepdims=True) * ROUTED_SCALE

    @pl.when(pl.program_id(0) == 0)
    def _():
        carry_ref[...] = jnp.zeros_like(carry_ref)

    chosen_f = jnp.where(chosen, 1.0, 0.0)
    before = carry_ref[...] + _dot(chosen_f.astype(BF16), tri_ref[...]) - chosen_f
    carry_ref[...] = carry_ref[...] + jnp.sum(chosen_f, axis=1, keepdims=True)
    cnt_ref[...] = jnp.broadcast_to(carry_ref[...], cnt_ref.shape).astype(jnp.int32)
    for k in range(TOP_K):
        hit = chosen & (rank == k)
        idx_ref[k:k + 1, :] = jnp.sum(jnp.where(hit, eid_f, 0.0), axis=0, keepdims=True).astype(jnp.int32)
        w_ref[k:k + 1, :] = jnp.sum(jnp.where(hit, wnorm, 0.0), axis=0, keepdims=True)
        rank_ref[k:k + 1, :] = jnp.sum(jnp.where(hit, before, 0.0), axis=0, keepdims=True).astype(jnp.int32)


def _route(logits_t, bias_col):
    e, n = logits_t.shape
    tn = ROW_TILE
    tri = jnp.asarray(np.triu(np.ones((tn, tn), np.float32)), BF16)
    slot = pl.BlockSpec((TOP_K, tn), lambda i: (0, i))
    return pl.pallas_call(
        _route_kernel,
        grid=(n // tn,),
        in_specs=[pl.BlockSpec((e, tn), lambda i: (0, i)), pl.BlockSpec((e, 1), lambda i: (0, 0)),
                  pl.BlockSpec((tn, tn), lambda i: (0, 0))],
        out_specs=[slot, slot, slot, pl.BlockSpec((e, LANE), lambda i: (0, 0))],
        out_shape=[
            jax.ShapeDtypeStruct((TOP_K, n), jnp.int32),
            jax.ShapeDtypeStruct((TOP_K, n), F32),
            jax.ShapeDtypeStruct((TOP_K, n), jnp.int32),
            jax.ShapeDtypeStruct((e, LANE), jnp.int32),
        ],
        scratch_shapes=[pltpu.VMEM((e, 1), F32)],
        compiler_params=_params(("arbitrary",)),
    )(logits_t, bias_col, tri)


def _dispatch_sc(pos_t, xr3, p_rows):
    k, n = pos_t.shape
    _, sub, lane = xr3.shape
    nk = k * n
    workers = SC_CORES * SC_SUBCORES
    per_worker = nk // workers
    assert nk % (workers * SC_WINDOW) == 0 and n % SC_WINDOW == 0
    mesh = plsc.VectorSubcoreMesh(core_axis_name="c", subcore_axis_name="s")

    @functools.partial(
        pl.kernel, mesh=mesh,
        out_type=jax.ShapeDtypeStruct((p_rows, sub, lane), xr3.dtype),
        scratch_types=[pltpu.VMEM((SC_WINDOW,), jnp.int32), pltpu.VMEM((SC_WINDOW, sub, lane), xr3.dtype),
                       pltpu.SemaphoreType.DMA],
    )
    def scatter_rows(x_hbm, idx_hbm, o_hbm, idx_v, rows_v, sem):
        wid = lax.axis_index("s") * SC_CORES + lax.axis_index("c")

        @pl.loop(0, per_worker // SC_WINDOW)
        def _(c):
            base = wid * per_worker + c * SC_WINDOW
            tok0 = lax.rem(base, n)
            pltpu.sync_copy(idx_hbm.at[pl.ds(base, SC_WINDOW)], idx_v)
            pltpu.sync_copy(x_hbm.at[pl.ds(tok0, SC_WINDOW)], rows_v)
            pltpu.async_copy(rows_v, o_hbm.at[idx_v], sem).wait()

    return scatter_rows(xr3, pos_t.reshape(nk))


def _combine_sc(pos_t, ys3):
    k, n = pos_t.shape
    _, sub, lane = ys3.shape
    nk = k * n
    workers = SC_CORES * SC_SUBCORES
    per_worker = nk // workers
    assert nk % (workers * SC_WINDOW) == 0
    mesh = plsc.VectorSubcoreMesh(core_axis_name="c", subcore_axis_name="s")

    @functools.partial(
        pl.kernel, mesh=mesh,
        out_type=jax.ShapeDtypeStruct((nk, sub, lane), ys3.dtype),
        scratch_types=[pltpu.VMEM((SC_WINDOW,), jnp.int32), pltpu.VMEM((SC_WINDOW, sub, lane), ys3.dtype),
                       pltpu.SemaphoreType.DMA],
    )
    def gather_rows(y_hbm, idx_hbm, o_hbm, idx_v, rows_v, sem):
        wid = lax.axis_index("s") * SC_CORES + lax.axis_index("c")

        @pl.loop(0, per_worker // SC_WINDOW)
        def _(c):
            base = wid * per_worker + c * SC_WINDOW
            pltpu.sync_copy(idx_hbm.at[pl.ds(base, SC_WINDOW)], idx_v)
            pltpu.async_copy(y_hbm.at[idx_v], rows_v, sem).wait()
            pltpu.sync_copy(rows_v, o_hbm.at[pl.ds(base, SC_WINDOW)])

    return gather_rows(ys3, pos_t.reshape(nk))


def _moe_kernel(blk_e_ref, nused_ref, nvalid_ref, xs_ref, wg_ref, wu_ref, wd_ref, o_ref):
    i = pl.program_id(0)

    @pl.when(i < nused_ref[0])
    def _():
        sub = xs_ref.shape[0] // MOE_ROWS
        owned = lax.broadcasted_iota(jnp.int32, (MOE_ROWS, LANE), 0) < nvalid_ref[i]
        words = [jnp.where(owned, xs_ref[pl.ds(c, MOE_ROWS, stride=sub), :], jnp.uint32(0)) for c in range(sub)]
        halves = [_unpack_bf16_pairs(w) for w in words]
        x = jnp.concatenate([lo for lo, _ in halves] + [hi for _, hi in halves], axis=1).astype(BF16)
        hg = _dot(x, wg_ref[0])
        hid = (hg * jax.nn.sigmoid(hg)) * _dot(x, wu_ref[0])
        y = _dot(hid.astype(BF16), wd_ref[0])
        for c in range(sub):
            o_ref[pl.ds(c, MOE_ROWS, stride=sub), :] = _pack_bf16_pairs(
                y[:, c * LANE:(c + 1) * LANE], y[:, (c + sub) * LANE:(c + sub + 1) * LANE])

    @pl.when(i >= nused_ref[0])
    def _():
        o_ref[...] = jnp.zeros_like(o_ref)


def _moe_experts(blk_e, nused, nvalid, xs, wg, wu, wd):
    d = wg.shape[1]
    sub = d // (2 * LANE)
    p = xs.shape[0] // sub
    hid = wg.shape[2]
    grid_spec = pltpu.PrefetchScalarGridSpec(
        num_scalar_prefetch=3,
        grid=(p // MOE_ROWS,),
        in_specs=[
            pl.BlockSpec((MOE_ROWS * sub, LANE), lambda i, be, nu, nv: (jnp.minimum(i, nu[0] - 1), 0)),
            pl.BlockSpec((1, d, hid), lambda i, be, nu, nv: (be[i], 0, 0)),
            pl.BlockSpec((1, d, hid), lambda i, be, nu, nv: (be[i], 0, 0)),
            pl.BlockSpec((1, hid, d), lambda i, be, nu, nv: (be[i], 0, 0)),
        ],
        out_specs=pl.BlockSpec((MOE_ROWS * sub, LANE), lambda i, be, nu, nv: (i, 0)),
    )
    return pl.pallas_call(
        _moe_kernel,
        grid_spec=grid_spec,
        out_shape=jax.ShapeDtypeStruct((p * sub, LANE), jnp.uint32),
        compiler_params=_params(("arbitrary",)),
    )(blk_e, nused, nvalid, xs, wg, wu, wd)


def _side_kernel(xn_ref, p_ref, wsg, wsu, wsd, wp, sh_ref, pe_ref):
    xn = xn_ref[...]
    hg = _dot(xn, wsg[...])
    sh_ref[...] = _dot(((hg * jax.nn.sigmoid(hg)) * _dot(xn, wsu[...])).astype(BF16), wsd[...]).astype(sh_ref.dtype)
    pe_ref[...] = _dot(p_ref[...].astype(BF16), wp[...]).astype(pe_ref.dtype)


def _side(xn2, p2, wsg, wsu, wsd, wp):
    n, d = xn2.shape
    tm = ROW_TILE
    row = lambda w: pl.BlockSpec((tm, w), lambda i: (i, 0))
    full = lambda a: pl.BlockSpec(a.shape, lambda i: (0,) * a.ndim)
    return pl.pallas_call(
        _side_kernel,
        grid=(n // tm,),
        in_specs=[row(d), row(p2.shape[1]), full(wsg), full(wsu), full(wsd), full(wp)],
        out_specs=[row(d), row(d)],
        out_shape=[jax.ShapeDtypeStruct((n, d), BF16)] * 2,
        compiler_params=_params(("parallel",)),
    )(xn2, p2, wsg, wsu, wsd, wp)


def _final_kernel(h_ref, sh_ref, pe_ref, ys_ref, w_ref, gp, wpg, gfin, o_ref):
    w = w_ref[...]
    tm = w.shape[0]
    sub = ys_ref.shape[1] // tm
    parts = [None] * (2 * sub)
    for k in range(TOP_K):
        wk = jnp.broadcast_to(w[:, k:k + 1], (tm, LANE))
        for c in range(sub):
            lo, hi = _unpack_bf16_pairs(ys_ref[k, pl.ds(c, tm, stride=sub), :])
            parts[c] = wk * lo if k == 0 else parts[c] + wk * lo
            parts[c + sub] = wk * hi if k == 0 else parts[c + sub] + wk * hi
    y = jnp.concatenate(parts, axis=1)
    h = h_ref[...] + (y + sh_ref[...].astype(F32))
    gate = jax.nn.sigmoid(_dot(_rms(h, gp[...]).astype(BF16), wpg[...]))
    h = h + pe_ref[...].astype(F32) * gate
    o_ref[...] = _rms(h, gfin[...])


def _final(h1, shared, ple, ysg, w_nk, gp, wpg, gfin):
    n, d = h1.shape
    tm = ROW_TILE
    row = lambda w: pl.BlockSpec((tm, w), lambda i: (i, 0))
    full = lambda a: pl.BlockSpec(a.shape, lambda i: (0,) * a.ndim)
    return pl.pallas_call(
        _final_kernel,
        grid=(n // tm,),
        in_specs=[row(d), row(d), row(d), pl.BlockSpec((TOP_K, tm * ysg.shape[1] // n, LANE), lambda i: (0, i, 0)),
                  row(TOP_K), full(gp), full(wpg), full(gfin)],
        out_specs=row(d),
        out_shape=jax.ShapeDtypeStruct((n, d), F32),
        compiler_params=_params(("parallel",)),
    )(h1, shared, ple, ysg, w_nk, gp, wpg, gfin)


def _pad_heads(w):
    d, c = w.shape
    nh = c // HEAD_DIM
    w = w.reshape(d, nh, HEAD_DIM)
    return jnp.pad(w, ((0, 0), (0, 0), (0, LANE - HEAD_DIM))).reshape(d, nh * LANE)


def _alibi_slopes(n):
    return 2.0 ** (-8.0 * jnp.arange(1, n + 1, dtype=jnp.float32) / n)


def kernel(x, p, g_mix, w_in, cmp_pos_k, cmp_w1_k, cmp_b1_k, cmp_w2_k, cmp_pos_v, cmp_w1_v, cmp_b1_v, cmp_w2_v, w_branch_a, w_branch_b, w_out, g_ffn, w_router, router_bias, w_gate, w_up, w_down, ws_gate, ws_up, ws_down, g_ple, w_ple_gate, w_ple, g_final):
    B, S, D = x.shape
    N = B * S
    G, R = NSA_KV_HEADS, NSA_REP
    assert w_in.shape[0] == 1 and p.shape[0] == 1
    assert S % DIL_STEP_ROWS == 0 and S % NSA_TK == 0 and S >= WIN_SIZE + NSA_Q
    assert N % ROW_TILE == 0 and N % INPROJ_ROWS == 0 and S % INPROJ_ROWS == 0
    slopes = _alibi_slopes(DIL_HEADS + NSA_HEADS)
    slopes_a, slopes_b = slopes[0::2], slopes[1::2]

    offs = np.concatenate([[0], np.cumsum(IN_SIZES)])
    parts = [w_in[0][:, offs[i]:offs[i + 1]] for i in range(len(IN_SIZES))]
    w_qa, w_ka, w_va, w_qb, w_kc, w_vc, w_ks, w_vs, w_kw, w_vw, w_gn, w_ga, w_gb = parts
    w_qkv_a = jnp.concatenate([w_qa * HEAD_DIM ** -0.5, w_ka, w_va], axis=1).astype(BF16)
    w_nsa = jnp.concatenate([_pad_heads(t) for t in (w_qb, w_kc, w_vc, w_ks, w_vs, w_kw, w_vw)],
                            axis=1).astype(BF16)
    nsa_w = w_nsa.shape[1]
    w_gn = w_gn.reshape(D, 3, G, R).transpose(0, 2, 1, 3).reshape(D, G, 3 * R)
    w_gn = jnp.pad(w_gn, ((0, 0), (0, 0), (0, LANE - 3 * R))).reshape(D, G * LANE)
    w_gate_in = jnp.concatenate([w_ga, w_gb, w_gn], axis=1).astype(BF16)
    gate_w = w_gate_in.shape[1]

    x2 = x.reshape(N, D)
    dils = tuple(dil for _, dil in DIL_PATTERNS if dil > 1)
    qkv_a, *dilated_qkv = _inproj(x2, g_mix, w_qkv_a, BF16, 512, dils=dils)
    nsa_in, = _inproj(x2, g_mix, w_nsa, BF16, 1280)
    gt, = _inproj(x2, g_mix, w_gate_in, F32, 1152)

    o_list, l_list = [], []
    for _, dil in DIL_PATTERNS:
        src = qkv_a if dil == 1 else dilated_qkv[dils.index(dil)]
        o, l = _dilated(src.reshape(B, S // dil, -1), _dil_bias(slopes_a, dil), dil, B, S)
        o_list.append(o.reshape(-1, N, LANE))
        l_list.append(l.reshape(-1, N, LANE))

    nchunk = S // CMP_STRIDE
    c0 = NSA_HEADS * LANE
    raw = nsa_in[:, c0:c0 + 2 * G * LANE].reshape(B, nchunk, CMP_STRIDE, 2 * G, LANE)[..., :HEAD_DIM]
    xc = raw.transpose(0, 3, 1, 2, 4).reshape(B, 2 * G, nchunk, CMP_STRIDE * HEAD_DIM)
    pos = jnp.stack([cmp_pos_k[0].reshape(1, -1), cmp_pos_v[0].reshape(1, -1)])
    w1 = jnp.stack([cmp_w1_k[0], cmp_w1_v[0]]).astype(BF16)
    b1 = jnp.stack([cmp_b1_k[0][None], cmp_b1_v[0][None]])
    w2 = jnp.pad(jnp.stack([cmp_w2_k[0], cmp_w2_v[0]]), ((0, 0), (0, 0), (0, LANE - HEAD_DIM))).astype(BF16)
    kcv = _compress(xc, pos, w1, b1, w2)
    kcv = kcv.reshape(B, 2, G, nchunk, LANE).transpose(1, 0, 2, 3, 4).reshape(2, B * G, nchunk, LANE)
    vct = kcv[1].transpose(0, 2, 1)
    v0 = (NSA_HEADS + 3 * G) * LANE
    vsw = jnp.stack([nsa_in[:, v0:v0 + G * LANE], nsa_in[:, v0 + 2 * G * LANE:v0 + 3 * G * LANE]])
    vsw = vsw.reshape(2, B, S // LANE, LANE, G, LANE).transpose(1, 4, 0, 2, 5, 3)
    ones_row = (jnp.arange(LANE) == SUM_LANE).astype(BF16)[:, None]
    vt = (vsw + ones_row).reshape(B * G, 2, S // LANE, LANE, LANE)
    yb = _nsa(slopes_b, nsa_in.reshape(B, S, nsa_w), gt.reshape(B, S, gate_w), kcv[0], vct, vt,
              B, S, 0, 2 * D // LANE)
    yb = yb.reshape(N, NSA_HEADS * LANE)

    wb_pad = jnp.pad(w_branch_b[0].reshape(NSA_HEADS, HEAD_DIM, D),
                     ((0, 0), (0, LANE - HEAD_DIM), (0, 0))).reshape(NSA_HEADS * LANE, D)
    wr_t = w_router[0].T
    wr_hi = wr_t.astype(BF16)
    wr_lo = (wr_t - wr_hi.astype(F32)).astype(BF16)
    h1, xn2, logits_t, xr = _merge(o_list, l_list, yb, gt, x2, w_branch_a[0].astype(BF16), wb_pad.astype(BF16),
                               w_out[0].astype(BF16), g_ffn, wr_hi, wr_lo)

    shared, ple = _side(xn2, p[0].reshape(N, -1), ws_gate[0].astype(BF16), ws_up[0].astype(BF16),
                        ws_down[0].astype(BF16), w_ple[0].astype(BF16))

    idx_t, wsel_t, rank_t, counts = _route(logits_t, router_bias[0][:, None])
    counts = counts[:, 0]
    padded = (counts + MOE_ROWS - 1) // MOE_ROWS * MOE_ROWS
    pend = jnp.cumsum(padded)
    pstart = pend - padded
    hot = idx_t[None] == jnp.arange(N_EXPERTS, dtype=jnp.int32)[:, None, None]
    pos_t = rank_t + jnp.sum(jnp.where(hot, pstart.astype(jnp.int32)[:, None, None], 0), axis=0)
    NK = N * TOP_K
    nblk = (NK + N_EXPERTS * (MOE_ROWS - 1) + MOE_ROWS - 1) // MOE_ROWS
    P = nblk * MOE_ROWS
    blk_e = jnp.minimum(jnp.sum(pend[None, :] <= (jnp.arange(nblk, dtype=jnp.int32) * MOE_ROWS)[:, None], axis=1),
                        N_EXPERTS - 1).astype(jnp.int32)
    nused = (pend[-1] // MOE_ROWS).astype(jnp.int32).reshape(1)
    row0 = jnp.arange(nblk, dtype=jnp.int32) * MOE_ROWS
    nvalid = jnp.clip((pstart + counts)[blk_e] - row0, 0, MOE_ROWS).astype(jnp.int32)
    sub = D // (2 * LANE)
    xs = _dispatch_sc(pos_t, xr.reshape(N, sub, LANE), P).reshape(P * sub, LANE)
    ys = _moe_experts(blk_e, nused, nvalid, xs, w_gate[0].astype(BF16), w_up[0].astype(BF16), w_down[0].astype(BF16))
    ysg = _combine_sc(pos_t, ys.reshape(P, sub, LANE)).reshape(TOP_K, N * sub, LANE)
    out = _final(h1, shared, ple, ysg, wsel_t.T, g_ple, w_ple_gate[0].astype(BF16), g_final[None, :])
    return out.reshape(B, S, D)
```

```python
import functools

import numpy as np
import jax
import jax.numpy as jnp
from jax import lax
from jax.experimental import pallas as pl
from jax.experimental.pallas import tpu as pltpu
from jax.experimental.pallas import tpu_sc as plsc

F32 = jnp.float32
BF16 = jnp.bfloat16

HEAD_DIM = 64
LANE = 128
RMS_EPS = 1e-6
DIL_HEADS = 8
DIL_PATTERNS = ((128, 1), (512, 4), (2048, 16))
DIL_BLOCK = 128
DIL_STEP_ROWS = 4096
NSA_HEADS = 8
NSA_KV_HEADS = 2
NSA_REP = NSA_HEADS // NSA_KV_HEADS
CMP_BLOCK = 32
CMP_STRIDE = 16
CMP_HIDDEN = 256
SLC_BLOCK = 64
SLC_TOP_N = 16
SLC_FORCE = 1e9
WIN_SIZE = 512
N_EXPERTS = 64
TOP_K = 8
N_GROUPS = 8
TOP_GROUPS = 4
GROUP_SIZE = N_EXPERTS // N_GROUPS
ROUTED_SCALE = 2.5
IN_SIZES = (512, 512, 512, 512, 128, 128, 128, 128, 128, 128, 24, 1024, 1024)

NEG = -1e30
NT_DIMS = (((1,), (1,)), ((), ()))

ROW_TILE = 512
INPROJ_ROWS = 1024
NSA_Q = 256
NSA_TK = 512
MOE_ROWS = 2048
VMEM_LIMIT = 48 * 1024 * 1024
SC_CORES = 2
SC_SUBCORES = 16
SC_WINDOW = 128


def _dot(a, b):
    return jnp.dot(a, b, preferred_element_type=F32)


def _dot_nt(a, b):
    return lax.dot_general(a, b, NT_DIMS, preferred_element_type=F32)


def _split3(x):
    hi = x.astype(BF16)
    r1 = x - hi.astype(F32)
    mid = r1.astype(BF16)
    lo = (r1 - mid.astype(F32)).astype(BF16)
    return hi, mid, lo


def _pack_bf16_pairs(lo, hi):
    lo_bits = pltpu.bitcast(lo.astype(BF16).astype(F32), jnp.uint32)
    hi_bits = pltpu.bitcast(hi.astype(BF16).astype(F32), jnp.uint32)
    return hi_bits | (lo_bits >> 16)


def _unpack_bf16_pairs(words):
    return (pltpu.bitcast(words << 16, F32), pltpu.bitcast(words & jnp.uint32(0xFFFF0000), F32))


def _rms(x, g):
    r = lax.rsqrt(jnp.mean(x * x, axis=-1, keepdims=True) + RMS_EPS)
    return (x * r) * g


def _params(sem):
    return pltpu.CompilerParams(dimension_semantics=sem, vmem_limit_bytes=VMEM_LIMIT)


def _inproj_kernel(x_ref, g_ref, w_ref, o_ref, *rest, dils):
    dil_refs, xn_ref = rest[:len(dils)], rest[len(dils)]
    res_ref = rest[len(dils) + 1] if dils else None

    @pl.when(pl.program_id(1) == 0)
    def _():
        xn_ref[...] = _rms(x_ref[...], g_ref[...]).astype(BF16)

    res = _dot(xn_ref[...], w_ref[...])
    o_ref[...] = res.astype(o_ref.dtype)

    if dils:
        tn = res.shape[1]
        for c in range(tn // LANE):
            res_ref[c] = res[:, c * LANE:(c + 1) * LANE]
        for d_ref, dil in zip(dil_refs, dils):
            for r in range(dil):
                for c in range(tn // LANE):
                    piece = res_ref[c, pl.ds(r, INPROJ_ROWS // dil, stride=dil), :]
                    d_ref[:, r * tn + c * LANE:r * tn + (c + 1) * LANE] = piece.astype(d_ref.dtype)


def _inproj(x2, g, w, out_dtype, tn, dils=()):
    n, d = x2.shape
    cols = w.shape[1]
    out_specs = [pl.BlockSpec((INPROJ_ROWS, tn), lambda i, j: (i, j))]
    out_shape = [jax.ShapeDtypeStruct((n, cols), out_dtype)]
    for dil in dils:
        out_specs.append(pl.BlockSpec((INPROJ_ROWS // dil, dil * tn), lambda i, j: (i, j)))
        out_shape.append(jax.ShapeDtypeStruct((n // dil, cols * dil), out_dtype))
    return pl.pallas_call(
        functools.partial(_inproj_kernel, dils=dils),
        grid=(n // INPROJ_ROWS, cols // tn),
        in_specs=[
            pl.BlockSpec((INPROJ_ROWS, d), lambda i, j: (i, 0)),
            pl.BlockSpec((1, d), lambda i, j: (0, 0)),
            pl.BlockSpec((d, tn), lambda i, j: (0, j)),
        ],
        out_specs=out_specs,
        out_shape=out_shape,
        scratch_shapes=[pltpu.VMEM((INPROJ_ROWS, d), BF16)]
        + ([pltpu.VMEM((tn // LANE, INPROJ_ROWS, LANE), F32)] if dils else []),
        compiler_params=_params(("parallel", "arbitrary")),
    )(x2, g, w)


def _dil_kernel(q_ref, kp_ref, kc_ref, vp_ref, vc_ref, bias_ref, o_ref, l_ref, *, dil, nq):
    i = pl.program_id(1)
    res = pl.program_id(2)
    blk = DIL_BLOCK
    lane = lax.broadcasted_iota(jnp.int32, (blk, LANE), 1)
    lo = lane < HEAD_DIM
    for sub in range(nq):
        rows = slice(sub * blk, (sub + 1) * blk)
        prev = slice((sub - 1) * blk, sub * blk)
        out_rows = pl.ds(sub * blk * dil + res, blk, stride=dil) if dil > 1 else rows
        variant = jnp.minimum(i, 1) * DIL_HEADS if sub == 0 else DIL_HEADS
        for hp in range(DIL_HEADS // 2):
            sl = slice(hp * LANE, (hp + 1) * LANE)
            q2 = q_ref[0, rows, sl]
            k_prev = kp_ref[0, :, sl] if sub == 0 else kc_ref[0, prev, sl]
            v_prev = vp_ref[0, :, sl] if sub == 0 else vc_ref[0, prev, sl]
            k2 = jnp.concatenate([k_prev, kc_ref[0, rows, sl]], axis=0)
            v2 = jnp.concatenate([v_prev, vc_ref[0, rows, sl]], axis=0)
            outs, lses = [], []
            for hh in range(2):
                keep = lo if hh == 0 else jnp.logical_not(lo)
                qm = jnp.where(keep, q2, jnp.zeros_like(q2))
                s = _dot_nt(qm, k2) + bias_ref[variant + 2 * hp + hh]
                m = jnp.max(s, axis=-1, keepdims=True)
                e = jnp.exp(s - m)
                den = jnp.sum(e, axis=-1, keepdims=True)
                outs.append(_dot(e.astype(BF16), v2) / den)
                lses.append(jnp.broadcast_to(m + jnp.log(den), (blk, LANE)))
            o_ref[hp, 0, out_rows, :] = jnp.where(lo, outs[0], outs[1])
            l_ref[hp, 0, out_rows, :] = jnp.where(lo, lses[0], lses[1])


def _dil_bias(slopes_a, dil):
    blk = DIL_BLOCK
    qi = jnp.arange(blk)[:, None]
    kj = jnp.arange(2 * blk)[None, :]
    dist = (qi + blk - kj).astype(F32)
    band = (dist >= 0) & (dist <= blk)
    slope = (slopes_a * dil)[:, None, None]
    bias = -slope * dist
    first = jnp.where(band & (kj >= blk), bias, NEG)
    rest = jnp.where(band, bias, NEG)
    return jnp.concatenate([first, rest], axis=0)


def _dilated(src3, bias, dil, b, s):
    nq = min(8, DIL_STEP_ROWS // (DIL_BLOCK * dil))
    nstep = s // dil // (DIL_BLOCK * nq)
    aw = DIL_HEADS * HEAD_DIM
    npair = DIL_HEADS // 2

    def spec(which, prev):
        if prev:
            return pl.BlockSpec((1, DIL_BLOCK, aw), lambda bb, i, r: (bb, jnp.maximum(nq * i - 1, 0), 3 * r + which))
        return pl.BlockSpec((1, DIL_BLOCK * nq, aw), lambda bb, i, r: (bb, i, 3 * r + which))

    out_spec = pl.BlockSpec((npair, 1, DIL_BLOCK * nq * dil, LANE), lambda bb, i, r: (0, bb, i, 0))
    return pl.pallas_call(
        functools.partial(_dil_kernel, dil=dil, nq=nq),
        grid=(b, nstep, dil),
        in_specs=[
            spec(0, False), spec(1, True), spec(1, False), spec(2, True), spec(2, False),
            pl.BlockSpec(bias.shape, lambda bb, i, r: (0, 0, 0)),
        ],
        out_specs=[out_spec, out_spec],
        out_shape=[jax.ShapeDtypeStruct((npair, b, s, LANE), F32)] * 2,
        compiler_params=_params(("parallel", "arbitrary", "arbitrary")),
    )(src3, src3, src3, src3, src3, bias)


def _cmp_kernel(x_ref, pos_ref, w1_ref, b1_ref, w2_ref, o_ref):
    half = x_ref.shape[-1]
    x = x_ref[0, 0]
    w1 = w1_ref[0]
    first = _dot(x, w1[:half])
    second = _dot(x, w1[half:])
    nrow = x.shape[0]
    pre = first + pltpu.roll(second, nrow - 1, 0)
    posb = _dot(jnp.broadcast_to(pos_ref[0], (8, 2 * half)).astype(BF16), w1)[0:1]
    pre = pre + posb + b1_ref[0]
    o_ref[0, 0] = _dot(jax.nn.gelu(pre).astype(BF16), w2_ref[0]).astype(o_ref.dtype)


def _compress(xc, pos, w1, b1, w2):
    b, four, nchunk, width = xc.shape
    return pl.pallas_call(
        _cmp_kernel,
        grid=(b, four),
        in_specs=[
            pl.BlockSpec((1, 1, nchunk, width), lambda bb, j: (bb, j, 0, 0)),
            pl.BlockSpec((1, 1, 2 * width), lambda bb, j: (j // 2, 0, 0)),
            pl.BlockSpec((1, 2 * width, CMP_HIDDEN), lambda bb, j: (j // 2, 0, 0)),
            pl.BlockSpec((1, 1, CMP_HIDDEN), lambda bb, j: (j // 2, 0, 0)),
            pl.BlockSpec((1, CMP_HIDDEN, LANE), lambda bb, j: (j // 2, 0, 0)),
        ],
        out_specs=pl.BlockSpec((1, 1, nchunk, LANE), lambda bb, j: (bb, j, 0, 0)),
        out_shape=jax.ShapeDtypeStruct((b, four, nchunk, LANE), BF16),
        compiler_params=_params(("parallel", "parallel")),
    )(xc, pos, w1, b1, w2)


POS_LANE = HEAD_DIM
POS_SPLIT = 64
SUM_LANE = HEAD_DIM
MASKED = -1e30
ROW_FLOOR = -1e29


def _pos_columns(pos):
    out = np.zeros((pos.shape[0], LANE), np.float32)
    out[:, POS_LANE] = pos // POS_SPLIT
    out[:, POS_LANE + 1] = pos % POS_SPLIT
    out[:, POS_LANE + 2] = 1.0
    out[:, POS_LANE + 3] = 1.0
    return jnp.asarray(out, BF16)


def _nsa_kernel(slope_ref, q_ref, kc_ref, vct_ref, ks_ref, kw_ref, vt_ref, gt_ref, ov_ref, kpa_ref, cpa_ref,
                oh_ref, o_ref, kaug_ref, flag_ref, list_ref, sbuf_ref, m_ref, acc_ref, *, seq):
    g = pl.program_id(1)
    c = pl.program_id(2)
    C = NSA_Q
    R = NSA_REP
    TK = NSA_TK
    rows = R * C
    t0 = c * C
    scale = HEAD_DIM ** -0.5
    nb = seq // SLC_BLOCK
    ncmp = seq // CMP_STRIDE
    n_sel = min(SLC_TOP_N, nb)
    vblocks = TK // LANE

    @pl.when(c == 0)
    def _():
        kaug_ref[:, 0:LANE] = ks_ref[0] + kpa_ref[...]
        kaug_ref[:, LANE:2 * LANE] = oh_ref[...]

    row = lax.broadcasted_iota(jnp.int32, (rows, 1), 0)
    ti = t0 + row % C
    slope = jnp.zeros((rows, 1), F32)
    for r in range(R):
        slope = jnp.where(row // C == r, slope_ref[g * R + r], slope)
    lane = lax.broadcasted_iota(jnp.int32, (rows, LANE), 1)
    thi = (ti // POS_SPLIT).astype(F32)
    tlo = (ti % POS_SPLIT).astype(F32)
    qcols = jnp.where(lane == POS_LANE, POS_SPLIT * slope,
                      jnp.where(lane == POS_LANE + 1, slope,
                                jnp.where(lane == POS_LANE + 2, -POS_SPLIT * slope * thi,
                                          jnp.where(lane == POS_LANE + 3, -slope * tlo, 0.0))))
    q4 = jnp.concatenate([q_ref[0, :, r * LANE:(r + 1) * LANE] for r in range(R)], axis=0)
    q4 = (q4.astype(F32) * scale + qcols).astype(BF16)
    tq = t0 + lax.broadcasted_iota(jnp.int32, (1, C), 1)

    def heads(x):
        return jnp.concatenate([x] * R, axis=1)

    kc = kc_ref[0] + cpa_ref[...]
    ci = lax.broadcasted_iota(jnp.int32, (ncmp, 1), 0)
    vis = (ci * CMP_STRIDE + (CMP_BLOCK - 1) <= tq) & (ci < ncmp - 1)
    s = _dot_nt(kc, q4) + heads(jnp.where(vis, 0.0, MASKED))
    m = jnp.maximum(jnp.max(s, axis=0, keepdims=True), ROW_FLOOR)
    e = jnp.exp(s - m)
    den = jnp.sum(e, axis=0, keepdims=True)
    pc = e * (1.0 / jnp.where(den > 0, den, 1.0))
    o_cmp = _dot(vct_ref[0], pc.astype(BF16))
    pcsum = pc[:, 0:C]
    for r in range(1, R):
        pcsum = pcsum + pc[:, r * C:(r + 1) * C]
    ov = ov_ref[...]
    p_slc = sum(_dot(ov, part) for part in _split3(pcsum))

    def values_t(which, first_block, n):
        return jnp.concatenate([vt_ref[0, which, first_block + i] for i in range(n)], axis=1)

    span = WIN_SIZE + C
    ws = pl.multiple_of(jnp.maximum(t0 - WIN_SIZE, 0), C)
    k_w = kw_ref[0, pl.ds(ws, span), :] + kpa_ref[pl.ds(ws, span), :]
    wpos = ws + lax.broadcasted_iota(jnp.int32, (span, 1), 0)
    wbias = jnp.where((wpos <= tq) & (wpos >= tq - (WIN_SIZE - 1)), 0.0, MASKED)
    s3 = _dot_nt(k_w, q4) + heads(wbias)
    e3 = jnp.exp(s3 - jnp.max(s3, axis=0, keepdims=True))
    ow = _dot(values_t(1, ws // LANE, span // LANE), e3.astype(BF16))
    o_win = ow / ow[SUM_LANE:SUM_LANE + 1, :]

    j = lax.broadcasted_iota(jnp.int32, (LANE, C), 0)
    cur = tq // SLC_BLOCK
    forced = (j == 0) | (j == cur) | (j == cur - 1)
    future = j * SLC_BLOCK > tq
    work = jnp.where(forced, -jnp.inf, jnp.where(future, -SLC_FORCE, p_slc))
    work = jnp.where(j < nb, work, -jnp.inf)
    sel = jnp.where(forced, 1.0, 0.0)
    jf = j.astype(F32)
    for _ in range(n_sel - 3):
        mx = jnp.max(work, axis=0, keepdims=True)
        first = jnp.min(jnp.where(work == mx, jf, float(LANE)), axis=0, keepdims=True)
        pick = jf == first
        sel = jnp.where(pick, 1.0, sel)
        work = jnp.where(pick, -jnp.inf, work)
    sel = jnp.where(future, 0.0, sel)
    blocks_per_tile = TK // SLC_BLOCK
    for kt in range(seq // TK):
        flag_ref[kt] = jnp.max(sel[kt * blocks_per_tile:(kt + 1) * blocks_per_tile, :])
    sel_bias = jnp.where(sel.T > 0.5, 0.0, MASKED).astype(BF16)
    q_aug = jnp.concatenate([q4, jnp.concatenate([sel_bias] * R, axis=0)], axis=1)

    m_ref[...] = jnp.full((1, rows), ROW_FLOOR, F32)
    acc_ref[...] = jnp.zeros((LANE, rows), F32)
    kio = lax.broadcasted_iota(jnp.int32, (TK, 1), 0)

    def scores(kt):
        start = pl.multiple_of(kt * TK, TK)
        return _dot_nt(kaug_ref[pl.ds(start, TK), :], q_aug)

    def absorb(s2, kt):
        m_i = m_ref[...]
        m_n = jnp.maximum(m_i, jnp.max(s2, axis=0, keepdims=True))
        alpha = jnp.exp(m_i - m_n)
        p2 = jnp.exp(s2 - m_n)
        acc_ref[...] = alpha * acc_ref[...] + _dot(values_t(0, kt * vblocks, vblocks), p2.astype(BF16))
        m_ref[...] = m_n

    last = (t0 + C - 1) // TK

    def compact(kt, n):
        list_ref[n] = kt
        return n + (flag_ref[kt] > 0.0).astype(jnp.int32)

    n_vis = lax.fori_loop(0, last, compact, 0)
    list_ref[n_vis] = last

    sbuf_ref[0] = scores(list_ref[0])

    def slc_body(j, carry):
        s_next = scores(list_ref[j + 1])
        absorb(sbuf_ref[j % 2], list_ref[j])
        sbuf_ref[(j + 1) % 2] = s_next
        return carry

    lax.fori_loop(0, n_vis, slc_body, 0)
    start = pl.multiple_of(last * TK, TK)
    absorb(sbuf_ref[n_vis % 2] + heads(jnp.where(start + kio <= tq, 0.0, MASKED)), last)
    acc = acc_ref[...]
    o_slc = acc / acc[SUM_LANE:SUM_LANE + 1, :]

    gates = jax.nn.sigmoid(gt_ref[0]).T
    for r in range(R):
        rs = slice(r * C, (r + 1) * C)
        out = (gates[r:r + 1] * o_cmp[:, rs] + gates[R + r:R + r + 1] * o_slc[:, rs]
               + gates[2 * R + r:2 * R + r + 1] * o_win[:, rs])
        o_ref[0, :, r * LANE:(r + 1) * LANE] = out.T.astype(o_ref.dtype)


def _nsa(slopes_b, pa3, gt3, kc, vct, vt, b, s, col0, gcol0):
    C = NSA_Q
    R = NSA_REP
    G = NSA_KV_HEADS
    nb = s // SLC_BLOCK
    ncmp = s // CMP_STRIDE
    assert nb <= LANE
    qc0 = col0 // R
    kv0 = col0 + NSA_HEADS
    rows = R * C

    ci = np.arange(ncmp)[None, :]
    sj = np.arange(LANE)[:, None]
    ov = ((ci * CMP_STRIDE < (sj + 1) * SLC_BLOCK) & (ci * CMP_STRIDE + CMP_BLOCK > sj * SLC_BLOCK)
          & (ci < ncmp - 1) & (sj < nb))
    ov = jnp.asarray(ov, BF16)
    kpa = _pos_columns(np.arange(s))
    cpa = _pos_columns(np.arange(ncmp) * CMP_STRIDE + (CMP_BLOCK - 1))
    oh = jnp.asarray(np.arange(s)[:, None] // SLC_BLOCK == np.arange(LANE)[None, :], BF16)

    def k_spec(which):
        return pl.BlockSpec((1, s, LANE), lambda bb, g, c: (bb, 0, kv0 + which * G + g))

    const = lambda a: pl.BlockSpec(a.shape, lambda bb, g, c: (0,) * a.ndim)
    return pl.pallas_call(
        functools.partial(_nsa_kernel, seq=s),
        grid=(b, G, s // C),
        in_specs=[
            pl.BlockSpec(memory_space=pltpu.SMEM),
            pl.BlockSpec((1, C, R * LANE), lambda bb, g, c: (bb, c, qc0 + g)),
            pl.BlockSpec((1, ncmp, LANE), lambda bb, g, c: (bb * G + g, 0, 0)),
            pl.BlockSpec((1, LANE, ncmp), lambda bb, g, c: (bb * G + g, 0, 0)),
            k_spec(2), k_spec(4),
            pl.BlockSpec((1, 2, s // LANE, LANE, LANE), lambda bb, g, c: (bb * G + g, 0, 0, 0, 0)),
            pl.BlockSpec((1, C, LANE), lambda bb, g, c: (bb, c, gcol0 + g)),
            const(ov), const(kpa), const(cpa), const(oh),
        ],
        out_specs=pl.BlockSpec((1, C, R * LANE), lambda bb, g, c: (bb, c, g)),
        out_shape=jax.ShapeDtypeStruct((b, s, NSA_HEADS * LANE), BF16),
        scratch_shapes=[
            pltpu.VMEM((s, 2 * LANE), BF16),
            pltpu.SMEM((s // NSA_TK,), F32),
            pltpu.SMEM((s // NSA_TK,), jnp.int32),
            pltpu.VMEM((2, NSA_TK, rows), F32),
            pltpu.VMEM((1, rows), F32),
            pltpu.VMEM((LANE, rows), F32),
        ],
        compiler_params=_params(("parallel", "parallel", "arbitrary")),
    )(slopes_b, pa3, kc, vct, pa3, pa3, vt, gt3, ov, kpa, cpa, oh)


def _merge_kernel(o1, o2, o3, l1, l2, l3, yb, ga, gb, x_ref, wa, wb, wo, gf, wrh, wrl,
                  h_ref, xn_ref, lg_ref, xr_ref):
    def pairs(ref):
        return jnp.concatenate([ref[h] for h in range(ref.shape[0])], axis=1)

    a1, a2, a3 = pairs(l1), pairs(l2), pairs(l3)
    mx = jnp.maximum(jnp.maximum(a1, a2), a3)
    e1, e2, e3 = jnp.exp(a1 - mx), jnp.exp(a2 - mx), jnp.exp(a3 - mx)
    tot = e1 + e2 + e3
    ya = (e1 / tot) * pairs(o1) + (e2 / tot) * pairs(o2) + (e3 / tot) * pairs(o3)
    merged = (jax.nn.sigmoid(ga[...]) * _dot(ya.astype(BF16), wa[...])
              + jax.nn.sigmoid(gb[...]) * _dot(yb[...], wb[...]))
    h = x_ref[...] + _dot(merged.astype(BF16), wo[...])
    h_ref[...] = h
    xn = _rms(h, gf[...])
    xn_ref[...] = xn.astype(BF16)
    tm, d = xn.shape
    half = d // (2 * LANE)

    for c in range(half):
        xr_ref[pl.ds(c, tm, stride=half), :] = _pack_bf16_pairs(
            xn[:, c * LANE:(c + 1) * LANE], xn[:, (c + half) * LANE:(c + half + 1) * LANE])
    xh = xn.astype(BF16)
    xl = (xn - xh.astype(F32)).astype(BF16)
    lg_ref[...] = _dot_nt(wrh[...], xh) + _dot_nt(wrh[...], xl) + _dot_nt(wrl[...], xh)


def _merge(o_list, l_list, yb, gt, x2, wa, wb, wo, gf, wrh, wrl):
    n, d = x2.shape
    aw = wa.shape[0]
    bw = wb.shape[0]
    tm = ROW_TILE
    row = lambda w: pl.BlockSpec((tm, w), lambda i: (i, 0))
    full = lambda a: pl.BlockSpec(a.shape, lambda i: (0,) * a.ndim)
    return pl.pallas_call(
        _merge_kernel,
        grid=(n // tm,),
        in_specs=[pl.BlockSpec((aw // LANE, tm, LANE), lambda i: (0, i, 0))] * 6 + [
            row(bw),
            pl.BlockSpec((tm, d), lambda i: (i, 0)),
            pl.BlockSpec((tm, d), lambda i: (i, 1)),
            row(d), full(wa), full(wb), full(wo), full(gf), full(wrh), full(wrl),
        ],
        out_specs=[row(d), row(d), pl.BlockSpec((N_EXPERTS, tm), lambda i: (0, i)),
                   pl.BlockSpec((tm * d // (2 * LANE), LANE), lambda i: (i, 0))],
        out_shape=[
            jax.ShapeDtypeStruct((n, d), F32),
            jax.ShapeDtypeStruct((n, d), BF16),
            jax.ShapeDtypeStruct((N_EXPERTS, n), F32),
            jax.ShapeDtypeStruct((n * d // (2 * LANE), LANE), jnp.uint32),
        ],
        compiler_params=_params(("parallel",)),
    )(*o_list, *l_list, yb, gt, gt, x2, wa, wb, wo, gf, wrh, wrl)


def _route_kernel(lg_ref, bias_ref, tri_ref, idx_ref, w_ref, rank_ref, cnt_ref, carry_ref):
    E = N_EXPERTS
    tn = lg_ref.shape[1]
    scores = jax.nn.sigmoid(lg_ref[...])
    sel = scores + bias_ref[...]
    sub = lax.broadcasted_iota(jnp.int32, (GROUP_SIZE, tn), 0).astype(F32)
    gs = []
    for grp in range(N_GROUPS):
        blk = sel[grp * GROUP_SIZE:(grp + 1) * GROUP_SIZE]
        m1 = jnp.max(blk, axis=0, keepdims=True)
        first = jnp.min(jnp.where(blk == m1, sub, float(GROUP_SIZE)), axis=0, keepdims=True)
        m2 = jnp.max(jnp.where(sub == first, -jnp.inf, blk), axis=0, keepdims=True)
        gs.append(m1 + m2)
    keep = []
    for a in range(N_GROUPS):
        rank = jnp.zeros((1, tn), jnp.int32)
        for o in range(N_GROUPS):
            if o == a:
                continue
            beats = (gs[o] >= gs[a]) if o < a else (gs[o] > gs[a])
            rank = rank + beats.astype(jnp.int32)
        keep.append(jnp.broadcast_to(rank < TOP_GROUPS, (GROUP_SIZE, tn)))
    emask = jnp.concatenate(keep, axis=0)
    cand = jnp.where(emask, sel, -jnp.inf)
    eid = lax.broadcasted_iota(jnp.int32, (E, tn), 0)
    eid_f = eid.astype(F32)
    rank = jnp.zeros((E, tn), jnp.int32)
    for o in range(E):
        other = cand[o:o + 1]
        beats = (other > cand) | ((other == cand) & (eid > o))
        rank = rank + beats.astype(jnp.int32)
    chosen = emask & (rank < TOP_K)
    wsel = jnp.where(chosen, scores, 0.0)
    wnorm = wsel / jnp.sum(wsel, axis=0, keepdims=True) * ROUTED_SCALE

    @pl.when(pl.program_id(0) == 0)
    def _():
        carry_ref[...] = jnp.zeros_like(carry_ref)

    chosen_f = jnp.where(chosen, 1.0, 0.0)
    before = carry_ref[...] + _dot(chosen_f.astype(BF16), tri_ref[...]) - chosen_f
    carry_ref[...] = carry_ref[...] + jnp.sum(chosen_f, axis=1, keepdims=True)
    cnt_ref[...] = jnp.broadcast_to(carry_ref[...], cnt_ref.shape).astype(jnp.int32)
    for k in range(TOP_K):
        hit = chosen & (rank == k)
        idx_ref[k:k + 1, :] = jnp.sum(jnp.where(hit, eid_f, 0.0), axis=0, keepdims=True).astype(jnp.int32)
        w_ref[k:k + 1, :] = jnp.sum(jnp.where(hit, wnorm, 0.0), axis=0, keepdims=True)
        rank_ref[k:k + 1, :] = jnp.sum(jnp.where(hit, before, 0.0), axis=0, keepdims=True).astype(jnp.int32)


def _route(logits_t, bias_col):
    e, n = logits_t.shape
    tn = ROW_TILE
    tri = jnp.asarray(np.triu(np.ones((tn, tn), np.float32)), BF16)
    slot = pl.BlockSpec((TOP_K, tn), lambda i: (0, i))
    return pl.pallas_call(
        _route_kernel,
        grid=(n // tn,),
        in_specs=[pl.BlockSpec((e, tn), lambda i: (0, i)), pl.BlockSpec((e, 1), lambda i: (0, 0)),
                  pl.BlockSpec((tn, tn), lambda i: (0, 0))],
        out_specs=[slot, slot, slot, pl.BlockSpec((e, LANE), lambda i: (0, 0))],
        out_shape=[
            jax.ShapeDtypeStruct((TOP_K, n), jnp.int32),
            jax.ShapeDtypeStruct((TOP_K, n), F32),
            jax.ShapeDtypeStruct((TOP_K, n), jnp.int32),
            jax.ShapeDtypeStruct((e, LANE), jnp.int32),
        ],
        scratch_shapes=[pltpu.VMEM((e, 1), F32)],
        compiler_params=_params(("arbitrary",)),
    )(logits_t, bias_col, tri)


def _dispatch_sc(pos_t, xr3, p_rows):
    k, n = pos_t.shape
    _, sub, lane = xr3.shape
    nk = k * n
    workers = SC_CORES * SC_SUBCORES
    per_worker = nk // workers
    assert nk % (workers * SC_WINDOW) == 0 and n % SC_WINDOW == 0
    mesh = plsc.VectorSubcoreMesh(core_axis_name="c", subcore_axis_name="s")

    @functools.partial(
        pl.kernel, mesh=mesh,
        out_type=jax.ShapeDtypeStruct((p_rows, sub, lane), xr3.dtype),
        scratch_types=[pltpu.VMEM((SC_WINDOW,), jnp.int32), pltpu.VMEM((SC_WINDOW, sub, lane), xr3.dtype),
                       pltpu.SemaphoreType.DMA],
    )
    def scatter_rows(x_hbm, idx_hbm, o_hbm, idx_v, rows_v, sem):
        wid = lax.axis_index("s") * SC_CORES + lax.axis_index("c")

        @pl.loop(0, per_worker // SC_WINDOW)
        def _(c):
            base = wid * per_worker + c * SC_WINDOW
            tok0 = lax.rem(base, n)
            pltpu.sync_copy(idx_hbm.at[pl.ds(base, SC_WINDOW)], idx_v)
            pltpu.sync_copy(x_hbm.at[pl.ds(tok0, SC_WINDOW)], rows_v)
            pltpu.async_copy(rows_v, o_hbm.at[idx_v], sem).wait()

    return scatter_rows(xr3, pos_t.reshape(nk))


def _combine_sc(pos_t, ys3):
    k, n = pos_t.shape
    _, sub, lane = ys3.shape
    nk = k * n
    workers = SC_CORES * SC_SUBCORES
    per_worker = nk // workers
    assert nk % (workers * SC_WINDOW) == 0
    mesh = plsc.VectorSubcoreMesh(core_axis_name="c", subcore_axis_name="s")

    @functools.partial(
        pl.kernel, mesh=mesh,
        out_type=jax.ShapeDtypeStruct((nk, sub, lane), ys3.dtype),
        scratch_types=[pltpu.VMEM((SC_WINDOW,), jnp.int32), pltpu.VMEM((SC_WINDOW, sub, lane), ys3.dtype),
                       pltpu.SemaphoreType.DMA],
    )
    def gather_rows(y_hbm, idx_hbm, o_hbm, idx_v, rows_v, sem):
        wid = lax.axis_index("s") * SC_CORES + lax.axis_index("c")

        @pl.loop(0, per_worker // SC_WINDOW)
        def _(c):
            base = wid * per_worker + c * SC_WINDOW
            pltpu.sync_copy(idx_hbm.at[pl.ds(base, SC_WINDOW)], idx_v)
            pltpu.async_copy(y_hbm.at[idx_v], rows_v, sem).wait()
            pltpu.sync_copy(rows_v, o_hbm.at[pl.ds(base, SC_WINDOW)])

    return gather_rows(ys3, pos_t.reshape(nk))


def _moe_kernel(blk_e_ref, nused_ref, nvalid_ref, xs_ref, wg_ref, wu_ref, wd_ref, o_ref):
    i = pl.program_id(0)

    @pl.when(i < nused_ref[0])
    def _():
        sub = xs_ref.shape[0] // MOE_ROWS
        owned = lax.broadcasted_iota(jnp.int32, (MOE_ROWS, LANE), 0) < nvalid_ref[i]
        words = [jnp.where(owned, xs_ref[pl.ds(c, MOE_ROWS, stride=sub), :], jnp.uint32(0)) for c in range(sub)]
        halves = [_unpack_bf16_pairs(w) for w in words]
        x = jnp.concatenate([lo for lo, _ in halves] + [hi for _, hi in halves], axis=1).astype(BF16)
        hg = _dot(x, wg_ref[0])
        hid = (hg * jax.nn.sigmoid(hg)) * _dot(x, wu_ref[0])
        y = _dot(hid.astype(BF16), wd_ref[0])
        for c in range(sub):
            o_ref[pl.ds(c, MOE_ROWS, stride=sub), :] = _pack_bf16_pairs(
                y[:, c * LANE:(c + 1) * LANE], y[:, (c + sub) * LANE:(c + sub + 1) * LANE])

    @pl.when(i >= nused_ref[0])
    def _():
        o_ref[...] = jnp.zeros_like(o_ref)


def _moe_experts(blk_e, nused, nvalid, xs, wg, wu, wd):
    d = wg.shape[1]
    sub = d // (2 * LANE)
    p = xs.shape[0] // sub
    hid = wg.shape[2]
    grid_spec = pltpu.PrefetchScalarGridSpec(
        num_scalar_prefetch=3,
        grid=(p // MOE_ROWS,),
        in_specs=[
            pl.BlockSpec((MOE_ROWS * sub, LANE), lambda i, be, nu, nv: (jnp.minimum(i, nu[0] - 1), 0)),
            pl.BlockSpec((1, d, hid), lambda i, be, nu, nv: (be[i], 0, 0)),
            pl.BlockSpec((1, d, hid), lambda i, be, nu, nv: (be[i], 0, 0)),
            pl.BlockSpec((1, hid, d), lambda i, be, nu, nv: (be[i], 0, 0)),
        ],
        out_specs=pl.BlockSpec((MOE_ROWS * sub, LANE), lambda i, be, nu, nv: (i, 0)),
    )
    return pl.pallas_call(
        _moe_kernel,
        grid_spec=grid_spec,
        out_shape=jax.ShapeDtypeStruct((p * sub, LANE), jnp.uint32),
        compiler_params=_params(("arbitrary",)),
    )(blk_e, nused, nvalid, xs, wg, wu, wd)


def _side_kernel(xn_ref, p_ref, wsg, wsu, wsd, wp, sh_ref, pe_ref):
    xn = xn_ref[...]
    hg = _dot(xn, wsg[...])
    sh_ref[...] = _dot(((hg * jax.nn.sigmoid(hg)) * _dot(xn, wsu[...])).astype(BF16), wsd[...]).astype(sh_ref.dtype)
    pe_ref[...] = _dot(p_ref[...].astype(BF16), wp[...]).astype(pe_ref.dtype)


def _side(xn2, p2, wsg, wsu, wsd, wp):
    n, d = xn2.shape
    tm = ROW_TILE
    row = lambda w: pl.BlockSpec((tm, w), lambda i: (i, 0))
    full = lambda a: pl.BlockSpec(a.shape, lambda i: (0,) * a.ndim)
    return pl.pallas_call(
        _side_kernel,
        grid=(n // tm,),
        in_specs=[row(d), row(p2.shape[1]), full(wsg), full(wsu), full(wsd), full(wp)],
        out_specs=[row(d), row(d)],
        out_shape=[jax.ShapeDtypeStruct((n, d), BF16)] * 2,
        compiler_params=_params(("parallel",)),
    )(xn2, p2, wsg, wsu, wsd, wp)


def _final_kernel(h_ref, sh_ref, pe_ref, ys_ref, w_ref, gp, wpg, gfin, o_ref):
    w = w_ref[...]
    tm = w.shape[0]
    sub = ys_ref.shape[1] // tm
    parts = [None] * (2 * sub)
    for k in range(TOP_K):
        wk = jnp.broadcast_to(w[:, k:k + 1], (tm, LANE))
        for c in range(sub):
            lo, hi = _unpack_bf16_pairs(ys_ref[k, pl.ds(c, tm, stride=sub), :])
            parts[c] = wk * lo if k == 0 else parts[c] + wk * lo
            parts[c + sub] = wk * hi if k == 0 else parts[c + sub] + wk * hi
    y = jnp.concatenate(parts, axis=1)
    h = h_ref[...] + (y + sh_ref[...].astype(F32))
    gate = jax.nn.sigmoid(_dot(_rms(h, gp[...]).astype(BF16), wpg[...]))
    h = h + pe_ref[...].astype(F32) * gate
    o_ref[...] = _rms(h, gfin[...])


def _final(h1, shared, ple, ysg, w_nk, gp, wpg, gfin):
    n, d = h1.shape
    tm = ROW_TILE
    row = lambda w: pl.BlockSpec((tm, w), lambda i: (i, 0))
    full = lambda a: pl.BlockSpec(a.shape, lambda i: (0,) * a.ndim)
    return pl.pallas_call(
        _final_kernel,
        grid=(n // tm,),
        in_specs=[row(d), row(d), row(d), pl.BlockSpec((TOP_K, tm * ysg.shape[1] // n, LANE), lambda i: (0, i, 0)),
                  row(TOP_K), full(gp), full(wpg), full(gfin)],
        out_specs=row(d),
        out_shape=jax.ShapeDtypeStruct((n, d), F32),
        compiler_params=_params(("parallel",)),
    )(h1, shared, ple, ysg, w_nk, gp, wpg, gfin)


def _pad_heads(w):
    d, c = w.shape
    nh = c // HEAD_DIM
    w = w.reshape(d, nh, HEAD_DIM)
    return jnp.pad(w, ((0, 0), (0, 0), (0, LANE - HEAD_DIM))).reshape(d, nh * LANE)


def _alibi_slopes(n):
    return 2.0 ** (-8.0 * jnp.arange(1, n + 1, dtype=jnp.float32) / n)


def kernel(x, p, g_mix, w_in, cmp_pos_k, cmp_w1_k, cmp_b1_k, cmp_w2_k, cmp_pos_v, cmp_w1_v, cmp_b1_v, cmp_w2_v, w_branch_a, w_branch_b, w_out, g_ffn, w_router, router_bias, w_gate, w_up, w_down, ws_gate, ws_up, ws_down, g_ple, w_ple_gate, w_ple, g_final):
    B, S, D = x.shape
    N = B * S
    G, R = NSA_KV_HEADS, NSA_REP
    assert w_in.shape[0] == 1 and p.shape[0] == 1
    assert S % DIL_STEP_ROWS == 0 and S % NSA_TK == 0 and S >= WIN_SIZE + NSA_Q
    assert N % ROW_TILE == 0 and N % INPROJ_ROWS == 0 and S % INPROJ_ROWS == 0
    slopes = _alibi_slopes(DIL_HEADS + NSA_HEADS)
    slopes_a, slopes_b = slopes[0::2], slopes[1::2]

    offs = np.concatenate([[0], np.cumsum(IN_SIZES)])
    parts = [w_in[0][:, offs[i]:offs[i + 1]] for i in range(len(IN_SIZES))]
    w_qa, w_ka, w_va, w_qb, w_kc, w_vc, w_ks, w_vs, w_kw, w_vw, w_gn, w_ga, w_gb = parts
    w_qkv_a = jnp.concatenate([w_qa * HEAD_DIM ** -0.5, w_ka, w_va], axis=1).astype(BF16)
    w_nsa = jnp.concatenate([_pad_heads(t) for t in (w_qb, w_kc, w_vc, w_ks, w_vs, w_kw, w_vw)],
                            axis=1).astype(BF16)
    nsa_w = w_nsa.shape[1]
    w_gn = w_gn.reshape(D, 3, G, R).transpose(0, 2, 1, 3).reshape(D, G, 3 * R)
    w_gn = jnp.pad(w_gn, ((0, 0), (0, 0), (0, LANE - 3 * R))).reshape(D, G * LANE)
    w_gate_in = jnp.concatenate([w_ga, w_gb, w_gn], axis=1).astype(BF16)
    gate_w = w_gate_in.shape[1]

    x2 = x.reshape(N, D)
    dils = tuple(dil for _, dil in DIL_PATTERNS if dil > 1)
    qkv_a, *dilated_qkv = _inproj(x2, g_mix, w_qkv_a, BF16, w_qkv_a.shape[1], dils=dils)
    nsa_in, = _inproj(x2, g_mix, w_nsa, BF16, nsa_w)
    gt, = _inproj(x2, g_mix, w_gate_in, F32, gate_w)

    o_list, l_list = [], []
    for _, dil in DIL_PATTERNS:
        src = qkv_a if dil == 1 else dilated_qkv[dils.index(dil)]
        o, l = _dilated(src.reshape(B, S // dil, -1), _dil_bias(slopes_a, dil), dil, B, S)
        o_list.append(o.reshape(-1, N, LANE))
        l_list.append(l.reshape(-1, N, LANE))

    nchunk = S // CMP_STRIDE
    c0 = NSA_HEADS * LANE
    raw = nsa_in[:, c0:c0 + 2 * G * LANE].reshape(B, nchunk, CMP_STRIDE, 2 * G, LANE)[..., :HEAD_DIM]
    xc = raw.transpose(0, 3, 1, 2, 4).reshape(B, 2 * G, nchunk, CMP_STRIDE * HEAD_DIM)
    pos = jnp.stack([cmp_pos_k[0].reshape(1, -1), cmp_pos_v[0].reshape(1, -1)])
    w1 = jnp.stack([cmp_w1_k[0], cmp_w1_v[0]]).astype(BF16)
    b1 = jnp.stack([cmp_b1_k[0][None], cmp_b1_v[0][None]])
    w2 = jnp.pad(jnp.stack([cmp_w2_k[0], cmp_w2_v[0]]), ((0, 0), (0, 0), (0, LANE - HEAD_DIM))).astype(BF16)
    kcv = _compress(xc, pos, w1, b1, w2)
    kcv = kcv.reshape(B, 2, G, nchunk, LANE).transpose(1, 0, 2, 3, 4).reshape(2, B * G, nchunk, LANE)
    vct = kcv[1].transpose(0, 2, 1)
    v0 = (NSA_HEADS + 3 * G) * LANE
    vsw = jnp.stack([nsa_in[:, v0:v0 + G * LANE], nsa_in[:, v0 + 2 * G * LANE:v0 + 3 * G * LANE]])
    vsw = vsw.reshape(2, B, S // LANE, LANE, G, LANE).transpose(1, 4, 0, 2, 5, 3)
    ones_row = (jnp.arange(LANE) == SUM_LANE).astype(BF16)[:, None]
    vt = (vsw + ones_row).reshape(B * G, 2, S // LANE, LANE, LANE)
    yb = _nsa(slopes_b, nsa_in.reshape(B, S, nsa_w), gt.reshape(B, S, gate_w), kcv[0], vct, vt,
              B, S, 0, 2 * D // LANE)
    yb = yb.reshape(N, NSA_HEADS * LANE)

    wb_pad = jnp.pad(w_branch_b[0].reshape(NSA_HEADS, HEAD_DIM, D),
                     ((0, 0), (0, LANE - HEAD_DIM), (0, 0))).reshape(NSA_HEADS * LANE, D)
    wr_t = w_router[0].T
    wr_hi = wr_t.astype(BF16)
    wr_lo = (wr_t - wr_hi.astype(F32)).astype(BF16)
    h1, xn2, logits_t, xr = _merge(o_list, l_list, yb, gt, x2, w_branch_a[0].astype(BF16), wb_pad.astype(BF16),
                               w_out[0].astype(BF16), g_ffn, wr_hi, wr_lo)

    shared, ple = _side(xn2, p[0].reshape(N, -1), ws_gate[0].astype(BF16), ws_up[0].astype(BF16),
                        ws_down[0].astype(BF16), w_ple[0].astype(BF16))

    idx_t, wsel_t, rank_t, counts = _route(logits_t, router_bias[0][:, None])
    counts = counts[:, 0]
    padded = (counts + MOE_ROWS - 1) // MOE_ROWS * MOE_ROWS
    pend = jnp.cumsum(padded)
    pstart = pend - padded
    hot = idx_t[None] == jnp.arange(N_EXPERTS, dtype=jnp.int32)[:, None, None]
    pos_t = rank_t + jnp.sum(jnp.where(hot, pstart.astype(jnp.int32)[:, None, None], 0), axis=0)
    NK = N * TOP_K
    nblk = (NK + N_EXPERTS * (MOE_ROWS - 1) + MOE_ROWS - 1) // MOE_ROWS
    P = nblk * MOE_ROWS
    blk_e = jnp.minimum(jnp.sum(pend[None, :] <= (jnp.arange(nblk, dtype=jnp.int32) * MOE_ROWS)[:, None], axis=1),
                        N_EXPERTS - 1).astype(jnp.int32)
    nused = (pend[-1] // MOE_ROWS).astype(jnp.int32).reshape(1)
    row0 = jnp.arange(nblk, dtype=jnp.int32) * MOE_ROWS
    nvalid = jnp.clip((pstart + counts)[blk_e] - row0, 0, MOE_ROWS).astype(jnp.int32)
    sub = D // (2 * LANE)
    xs = _dispatch_sc(pos_t, xr.reshape(N, sub, LANE), P).reshape(P * sub, LANE)
    ys = _moe_experts(blk_e, nused, nvalid, xs, w_gate[0].astype(BF16), w_up[0].astype(BF16), w_down[0].astype(BF16))
    ysg = _combine_sc(pos_t, ys.reshape(P, sub, LANE)).reshape(TOP_K, N * sub, LANE)
    out = _final(h1, shared, ple, ysg, wsel_t.T, g_ple, w_ple_gate[0].astype(BF16), g_final[None, :])
    return out.reshape(B, S, D)
```

```python
import functools

import numpy as np
import jax
import jax.numpy as jnp
from jax import lax
from jax.experimental import pallas as pl
from jax.experimental.pallas import tpu as pltpu
from jax.experimental.pallas import tpu_sc as plsc

F32 = jnp.float32
BF16 = jnp.bfloat16

HEAD_DIM = 64
LANE = 128
RMS_EPS = 1e-6
DIL_HEADS = 8
DIL_PATTERNS = ((128, 1), (512, 4), (2048, 16))
DIL_BLOCK = 128
DIL_STEP_ROWS = 4096
NSA_HEADS = 8
NSA_KV_HEADS = 2
NSA_REP = NSA_HEADS // NSA_KV_HEADS
CMP_BLOCK = 32
CMP_STRIDE = 16
CMP_HIDDEN = 256
SLC_BLOCK = 64
SLC_TOP_N = 16
SLC_FORCE = 1e9
WIN_SIZE = 512
N_EXPERTS = 64
TOP_K = 8
N_GROUPS = 8
TOP_GROUPS = 4
GROUP_SIZE = N_EXPERTS // N_GROUPS
ROUTED_SCALE = 2.5
IN_SIZES = (512, 512, 512, 512, 128, 128, 128, 128, 128, 128, 24, 1024, 1024)

NEG = -1e30
NT_DIMS = (((1,), (1,)), ((), ()))

ROW_TILE = 512
INPROJ_ROWS = 1024
NSA_Q = 256
NSA_TK = 512
MOE_ROWS = 2048
VMEM_LIMIT = 48 * 1024 * 1024
SC_CORES = 2
SC_SUBCORES = 16
SC_WINDOW = 128


def _dot(a, b):
    return jnp.dot(a, b, preferred_element_type=F32)


def _dot_nt(a, b):
    return lax.dot_general(a, b, NT_DIMS, preferred_element_type=F32)


def _split3(x):
    hi = x.astype(BF16)
    r1 = x - hi.astype(F32)
    mid = r1.astype(BF16)
    lo = (r1 - mid.astype(F32)).astype(BF16)
    return hi, mid, lo


def _pack_bf16_pairs(lo, hi):
    lo_bits = pltpu.bitcast(lo.astype(BF16).astype(F32), jnp.uint32)
    hi_bits = pltpu.bitcast(hi.astype(BF16).astype(F32), jnp.uint32)
    return hi_bits | (lo_bits >> 16)


def _unpack_bf16_pairs(words):
    return (pltpu.bitcast(words << 16, F32), pltpu.bitcast(words & jnp.uint32(0xFFFF0000), F32))


def _rms(x, g):
    r = lax.rsqrt(jnp.mean(x * x, axis=-1, keepdims=True) + RMS_EPS)
    return (x * r) * g


def _params(sem):
    return pltpu.CompilerParams(dimension_semantics=sem, vmem_limit_bytes=VMEM_LIMIT)


def _inproj_kernel(x_ref, g_ref, w_ref, o_ref, *rest, dils):
    dil_refs, xn_ref = rest[:len(dils)], rest[len(dils)]
    res_ref = rest[len(dils) + 1] if dils else None

    @pl.when(pl.program_id(1) == 0)
    def _():
        xn_ref[...] = _rms(x_ref[...], g_ref[...]).astype(BF16)

    res = _dot(xn_ref[...], w_ref[...])
    o_ref[...] = res.astype(o_ref.dtype)

    if dils:
        tn = res.shape[1]
        for c in range(tn // LANE):
            res_ref[c] = res[:, c * LANE:(c + 1) * LANE]
        for d_ref, dil in zip(dil_refs, dils):
            for r in range(dil):
                for c in range(tn // LANE):
                    piece = res_ref[c, pl.ds(r, INPROJ_ROWS // dil, stride=dil), :]
                    d_ref[:, r * tn + c * LANE:r * tn + (c + 1) * LANE] = piece.astype(d_ref.dtype)


def _inproj(x2, g, w, out_dtype, tn, dils=()):
    n, d = x2.shape
    cols = w.shape[1]
    out_specs = [pl.BlockSpec((INPROJ_ROWS, tn), lambda i, j: (i, j))]
    out_shape = [jax.ShapeDtypeStruct((n, cols), out_dtype)]
    for dil in dils:
        out_specs.append(pl.BlockSpec((INPROJ_ROWS // dil, dil * tn), lambda i, j: (i, j)))
        out_shape.append(jax.ShapeDtypeStruct((n // dil, cols * dil), out_dtype))
    return pl.pallas_call(
        functools.partial(_inproj_kernel, dils=dils),
        grid=(n // INPROJ_ROWS, cols // tn),
        in_specs=[
            pl.BlockSpec((INPROJ_ROWS, d), lambda i, j: (i, 0)),
            pl.BlockSpec((1, d), lambda i, j: (0, 0)),
            pl.BlockSpec((d, tn), lambda i, j: (0, j)),
        ],
        out_specs=out_specs,
        out_shape=out_shape,
        scratch_shapes=[pltpu.VMEM((INPROJ_ROWS, d), BF16)]
        + ([pltpu.VMEM((tn // LANE, INPROJ_ROWS, LANE), F32)] if dils else []),
        compiler_params=_params(("parallel", "arbitrary")),
    )(x2, g, w)


def _dil_kernel(q_ref, kp_ref, kc_ref, vp_ref, vc_ref, bias_ref, o_ref, l_ref, *, dil, nq):
    i = pl.program_id(1)
    res = pl.program_id(2)
    blk = DIL_BLOCK
    lane = lax.broadcasted_iota(jnp.int32, (blk, LANE), 1)
    lo = lane < HEAD_DIM
    for sub in range(nq):
        rows = slice(sub * blk, (sub + 1) * blk)
        prev = slice((sub - 1) * blk, sub * blk)
        out_rows = pl.ds(sub * blk * dil + res, blk, stride=dil) if dil > 1 else rows
        variant = jnp.minimum(i, 1) * DIL_HEADS if sub == 0 else DIL_HEADS
        for hp in range(DIL_HEADS // 2):
            sl = slice(hp * LANE, (hp + 1) * LANE)
            q2 = q_ref[0, rows, sl]
            k_prev = kp_ref[0, :, sl] if sub == 0 else kc_ref[0, prev, sl]
            v_prev = vp_ref[0, :, sl] if sub == 0 else vc_ref[0, prev, sl]
            k2 = jnp.concatenate([k_prev, kc_ref[0, rows, sl]], axis=0)
            v2 = jnp.concatenate([v_prev, vc_ref[0, rows, sl]], axis=0)
            outs, lses = [], []
            for hh in range(2):
                keep = lo if hh == 0 else jnp.logical_not(lo)
                qm = jnp.where(keep, q2, jnp.zeros_like(q2))
                s = _dot_nt(qm, k2) + bias_ref[variant + 2 * hp + hh]
                m = jnp.max(s, axis=-1, keepdims=True)
                e = jnp.exp(s - m)
                den = jnp.sum(e, axis=-1, keepdims=True)
                outs.append(_dot(e.astype(BF16), v2) / den)
                lses.append(jnp.broadcast_to(m + jnp.log(den), (blk, LANE)))
            o_ref[hp, 0, out_rows, :] = jnp.where(lo, outs[0], outs[1])
            l_ref[hp, 0, out_rows, :] = jnp.where(lo, lses[0], lses[1])


def _dil_bias(slopes_a, dil):
    blk = DIL_BLOCK
    qi = jnp.arange(blk)[:, None]
    kj = jnp.arange(2 * blk)[None, :]
    dist = (qi + blk - kj).astype(F32)
    band = (dist >= 0) & (dist <= blk)
    slope = (slopes_a * dil)[:, None, None]
    bias = -slope * dist
    first = jnp.where(band & (kj >= blk), bias, NEG)
    rest = jnp.where(band, bias, NEG)
    return jnp.concatenate([first, rest], axis=0)


def _dilated(src3, bias, dil, b, s):
    nq = min(8, DIL_STEP_ROWS // (DIL_BLOCK * dil))
    nstep = s // dil // (DIL_BLOCK * nq)
    aw = DIL_HEADS * HEAD_DIM
    npair = DIL_HEADS // 2

    def spec(which, prev):
        if prev:
            return pl.BlockSpec((1, DIL_BLOCK, aw), lambda bb, i, r: (bb, jnp.maximum(nq * i - 1, 0), 3 * r + which))
        return pl.BlockSpec((1, DIL_BLOCK * nq, aw), lambda bb, i, r: (bb, i, 3 * r + which))

    out_spec = pl.BlockSpec((npair, 1, DIL_BLOCK * nq * dil, LANE), lambda bb, i, r: (0, bb, i, 0))
    return pl.pallas_call(
        functools.partial(_dil_kernel, dil=dil, nq=nq),
        grid=(b, nstep, dil),
        in_specs=[
            spec(0, False), spec(1, True), spec(1, False), spec(2, True), spec(2, False),
            pl.BlockSpec(bias.shape, lambda bb, i, r: (0, 0, 0)),
        ],
        out_specs=[out_spec, out_spec],
        out_shape=[jax.ShapeDtypeStruct((npair, b, s, LANE), F32)] * 2,
        compiler_params=_params(("parallel", "arbitrary", "arbitrary")),
    )(src3, src3, src3, src3, src3, bias)


def _cmp_kernel(x_ref, pos_ref, w1_ref, b1_ref, w2_ref, o_ref):
    half = x_ref.shape[-1]
    x = x_ref[0, 0]
    w1 = w1_ref[0]
    first = _dot(x, w1[:half])
    second = _dot(x, w1[half:])
    nrow = x.shape[0]
    pre = first + pltpu.roll(second, nrow - 1, 0)
    posb = _dot(jnp.broadcast_to(pos_ref[0], (8, 2 * half)).astype(BF16), w1)[0:1]
    pre = pre + posb + b1_ref[0]
    o_ref[0, 0] = _dot(jax.nn.gelu(pre).astype(BF16), w2_ref[0]).astype(o_ref.dtype)


def _compress(xc, pos, w1, b1, w2):
    b, four, nchunk, width = xc.shape
    return pl.pallas_call(
        _cmp_kernel,
        grid=(b, four),
        in_specs=[
            pl.BlockSpec((1, 1, nchunk, width), lambda bb, j: (bb, j, 0, 0)),
            pl.BlockSpec((1, 1, 2 * width), lambda bb, j: (j // 2, 0, 0)),
            pl.BlockSpec((1, 2 * width, CMP_HIDDEN), lambda bb, j: (j // 2, 0, 0)),
            pl.BlockSpec((1, 1, CMP_HIDDEN), lambda bb, j: (j // 2, 0, 0)),
            pl.BlockSpec((1, CMP_HIDDEN, LANE), lambda bb, j: (j // 2, 0, 0)),
        ],
        out_specs=pl.BlockSpec((1, 1, nchunk, LANE), lambda bb, j: (bb, j, 0, 0)),
        out_shape=jax.ShapeDtypeStruct((b, four, nchunk, LANE), BF16),
        compiler_params=_params(("parallel", "parallel")),
    )(xc, pos, w1, b1, w2)


POS_LANE = HEAD_DIM
POS_SPLIT = 64
SUM_LANE = HEAD_DIM
MASKED = -1e30
ROW_FLOOR = -1e29


def _pos_columns(pos):
    out = np.zeros((pos.shape[0], LANE), np.float32)
    out[:, POS_LANE] = pos // POS_SPLIT
    out[:, POS_LANE + 1] = pos % POS_SPLIT
    out[:, POS_LANE + 2] = 1.0
    out[:, POS_LANE + 3] = 1.0
    return jnp.asarray(out, BF16)


def _nsa_kernel(slope_ref, q_ref, kc_ref, vct_ref, ks_ref, kw_ref, vt_ref, gt_ref, ov_ref, kpa_ref, cpa_ref,
                oh_ref, o_ref, kaug_ref, flag_ref, list_ref, sbuf_ref, m_ref, acc_ref, *, seq):
    g = pl.program_id(1)
    c = pl.program_id(2)
    C = NSA_Q
    R = NSA_REP
    TK = NSA_TK
    rows = R * C
    t0 = c * C
    scale = HEAD_DIM ** -0.5
    nb = seq // SLC_BLOCK
    ncmp = seq // CMP_STRIDE
    n_sel = min(SLC_TOP_N, nb)
    vblocks = TK // LANE

    @pl.when(c == 0)
    def _():
        kaug_ref[:, 0:LANE] = ks_ref[0] + kpa_ref[...]
        kaug_ref[:, LANE:2 * LANE] = oh_ref[...]

    row = lax.broadcasted_iota(jnp.int32, (rows, 1), 0)
    ti = t0 + row % C
    slope = jnp.zeros((rows, 1), F32)
    for r in range(R):
        slope = jnp.where(row // C == r, slope_ref[g * R + r], slope)
    lane = lax.broadcasted_iota(jnp.int32, (rows, LANE), 1)
    thi = (ti // POS_SPLIT).astype(F32)
    tlo = (ti % POS_SPLIT).astype(F32)
    qcols = jnp.where(lane == POS_LANE, POS_SPLIT * slope,
                      jnp.where(lane == POS_LANE + 1, slope,
                                jnp.where(lane == POS_LANE + 2, -POS_SPLIT * slope * thi,
                                          jnp.where(lane == POS_LANE + 3, -slope * tlo, 0.0))))
    q4 = jnp.concatenate([q_ref[0, :, r * LANE:(r + 1) * LANE] for r in range(R)], axis=0)
    q4 = (q4.astype(F32) * scale + qcols).astype(BF16)
    tq = t0 + lax.broadcasted_iota(jnp.int32, (1, C), 1)

    def heads(x):
        return jnp.concatenate([x] * R, axis=1)

    kc = kc_ref[0] + cpa_ref[...]
    ci = lax.broadcasted_iota(jnp.int32, (ncmp, 1), 0)
    vis = (ci * CMP_STRIDE + (CMP_BLOCK - 1) <= tq) & (ci < ncmp - 1)
    s = _dot_nt(kc, q4) + heads(jnp.where(vis, 0.0, MASKED))
    m = jnp.maximum(jnp.max(s, axis=0, keepdims=True), ROW_FLOOR)
    e = jnp.exp(s - m)
    den = jnp.sum(e, axis=0, keepdims=True)
    pc = e * (1.0 / jnp.where(den > 0, den, 1.0))
    o_cmp = _dot(vct_ref[0], pc.astype(BF16))
    pcsum = pc[:, 0:C]
    for r in range(1, R):
        pcsum = pcsum + pc[:, r * C:(r + 1) * C]
    ov = ov_ref[...]
    p_slc = sum(_dot(ov, part) for part in _split3(pcsum))

    def values_t(which, first_block, n):
        return jnp.concatenate([vt_ref[0, which, first_block + i] for i in range(n)], axis=1)

    span = WIN_SIZE + C
    ws = pl.multiple_of(jnp.maximum(t0 - WIN_SIZE, 0), C)
    k_w = kw_ref[0, pl.ds(ws, span), :] + kpa_ref[pl.ds(ws, span), :]
    wpos = ws + lax.broadcasted_iota(jnp.int32, (span, 1), 0)
    wbias = jnp.where((wpos <= tq) & (wpos >= tq - (WIN_SIZE - 1)), 0.0, MASKED)
    s3 = _dot_nt(k_w, q4) + heads(wbias)
    e3 = jnp.exp(s3 - jnp.max(s3, axis=0, keepdims=True))
    ow = _dot(values_t(1, ws // LANE, span // LANE), e3.astype(BF16))
    o_win = ow / ow[SUM_LANE:SUM_LANE + 1, :]

    j = lax.broadcasted_iota(jnp.int32, (LANE, C), 0)
    cur = tq // SLC_BLOCK
    forced = (j == 0) | (j == cur) | (j == cur - 1)
    future = j * SLC_BLOCK > tq
    work = jnp.where(forced, -jnp.inf, jnp.where(future, -SLC_FORCE, p_slc))
    work = jnp.where(j < nb, work, -jnp.inf)
    sel = jnp.where(forced, 1.0, 0.0)
    jf = j.astype(F32)
    for _ in range(n_sel - 3):
        mx = jnp.max(work, axis=0, keepdims=True)
        first = jnp.min(jnp.where(work == mx, jf, float(LANE)), axis=0, keepdims=True)
        pick = jf == first
        sel = jnp.where(pick, 1.0, sel)
        work = jnp.where(pick, -jnp.inf, work)
    sel = jnp.where(future, 0.0, sel)
    blocks_per_tile = TK // SLC_BLOCK
    for kt in range(seq // TK):
        flag_ref[kt] = jnp.max(sel[kt * blocks_per_tile:(kt + 1) * blocks_per_tile, :])
    sel_bias = jnp.where(sel.T > 0.5, 0.0, MASKED).astype(BF16)
    q_aug = jnp.concatenate([q4, jnp.concatenate([sel_bias] * R, axis=0)], axis=1)

    m_ref[...] = jnp.full((1, rows), ROW_FLOOR, F32)
    acc_ref[...] = jnp.zeros((LANE, rows), F32)
    kio = lax.broadcasted_iota(jnp.int32, (TK, 1), 0)

    def scores(kt):
        start = pl.multiple_of(kt * TK, TK)
        return _dot_nt(kaug_ref[pl.ds(start, TK), :], q_aug)

    def absorb(s2, kt):
        m_i = m_ref[...]
        m_n = jnp.maximum(m_i, jnp.max(s2, axis=0, keepdims=True))
        alpha = jnp.exp(m_i - m_n)
        p2 = jnp.exp(s2 - m_n)
        acc_ref[...] = alpha * acc_ref[...] + _dot(values_t(0, kt * vblocks, vblocks), p2.astype(BF16))
        m_ref[...] = m_n

    last = (t0 + C - 1) // TK

    def compact(kt, n):
        list_ref[n] = kt
        return n + (flag_ref[kt] > 0.0).astype(jnp.int32)

    n_vis = lax.fori_loop(0, last, compact, 0)
    list_ref[n_vis] = last

    sbuf_ref[0] = scores(list_ref[0])

    def slc_body(j, carry):
        s_next = scores(list_ref[j + 1])
        absorb(sbuf_ref[j % 2], list_ref[j])
        sbuf_ref[(j + 1) % 2] = s_next
        return carry

    lax.fori_loop(0, n_vis, slc_body, 0)
    start = pl.multiple_of(last * TK, TK)
    absorb(sbuf_ref[n_vis % 2] + heads(jnp.where(start + kio <= tq, 0.0, MASKED)), last)
    acc = acc_ref[...]
    o_slc = acc / acc[SUM_LANE:SUM_LANE + 1, :]

    gates = jax.nn.sigmoid(gt_ref[0]).T
    for r in range(R):
        rs = slice(r * C, (r + 1) * C)
        out = (gates[r:r + 1] * o_cmp[:, rs] + gates[R + r:R + r + 1] * o_slc[:, rs]
               + gates[2 * R + r:2 * R + r + 1] * o_win[:, rs])
        o_ref[0, :, r * LANE:(r + 1) * LANE] = out.T.astype(o_ref.dtype)


def _nsa(slopes_b, pa3, gt3, kc, vct, vt, b, s, col0, gcol0):
    C = NSA_Q
    R = NSA_REP
    G = NSA_KV_HEADS
    nb = s // SLC_BLOCK
    ncmp = s // CMP_STRIDE
    assert nb <= LANE
    qc0 = col0 // R
    kv0 = col0 + NSA_HEADS
    rows = R * C

    ci = np.arange(ncmp)[None, :]
    sj = np.arange(LANE)[:, None]
    ov = ((ci * CMP_STRIDE < (sj + 1) * SLC_BLOCK) & (ci * CMP_STRIDE + CMP_BLOCK > sj * SLC_BLOCK)
          & (ci < ncmp - 1) & (sj < nb))
    ov = jnp.asarray(ov, BF16)
    kpa = _pos_columns(np.arange(s))
    cpa = _pos_columns(np.arange(ncmp) * CMP_STRIDE + (CMP_BLOCK - 1))
    oh = jnp.asarray(np.arange(s)[:, None] // SLC_BLOCK == np.arange(LANE)[None, :], BF16)

    def k_spec(which):
        return pl.BlockSpec((1, s, LANE), lambda bb, g, c: (bb, 0, kv0 + which * G + g))

    const = lambda a: pl.BlockSpec(a.shape, lambda bb, g, c: (0,) * a.ndim)
    return pl.pallas_call(
        functools.partial(_nsa_kernel, seq=s),
        grid=(b, G, s // C),
        in_specs=[
            pl.BlockSpec(memory_space=pltpu.SMEM),
            pl.BlockSpec((1, C, R * LANE), lambda bb, g, c: (bb, c, qc0 + g)),
            pl.BlockSpec((1, ncmp, LANE), lambda bb, g, c: (bb * G + g, 0, 0)),
            pl.BlockSpec((1, LANE, ncmp), lambda bb, g, c: (bb * G + g, 0, 0)),
            k_spec(2), k_spec(4),
            pl.BlockSpec((1, 2, s // LANE, LANE, LANE), lambda bb, g, c: (bb * G + g, 0, 0, 0, 0)),
            pl.BlockSpec((1, C, LANE), lambda bb, g, c: (bb, c, gcol0 + g)),
            const(ov), const(kpa), const(cpa), const(oh),
        ],
        out_specs=pl.BlockSpec((1, C, R * LANE), lambda bb, g, c: (bb, c, g)),
        out_shape=jax.ShapeDtypeStruct((b, s, NSA_HEADS * LANE), BF16),
        scratch_shapes=[
            pltpu.VMEM((s, 2 * LANE), BF16),
            pltpu.SMEM((s // NSA_TK,), F32),
            pltpu.SMEM((s // NSA_TK,), jnp.int32),
            pltpu.VMEM((2, NSA_TK, rows), F32),
            pltpu.VMEM((1, rows), F32),
            pltpu.VMEM((LANE, rows), F32),
        ],
        compiler_params=_params(("parallel", "parallel", "arbitrary")),
    )(slopes_b, pa3, kc, vct, pa3, pa3, vt, gt3, ov, kpa, cpa, oh)


def _merge_kernel(o1, o2, o3, l1, l2, l3, yb, ga, gb, x_ref, wa, wb, wo, gf, wrh, wrl,
                  h_ref, xn_ref, lg_ref, xr_ref):
    def pairs(ref):
        return jnp.concatenate([ref[h] for h in range(ref.shape[0])], axis=1)

    a1, a2, a3 = pairs(l1), pairs(l2), pairs(l3)
    mx = jnp.maximum(jnp.maximum(a1, a2), a3)
    e1, e2, e3 = jnp.exp(a1 - mx), jnp.exp(a2 - mx), jnp.exp(a3 - mx)
    tot = e1 + e2 + e3
    ya = (e1 / tot) * pairs(o1) + (e2 / tot) * pairs(o2) + (e3 / tot) * pairs(o3)
    merged = (jax.nn.sigmoid(ga[...]) * _dot(ya.astype(BF16), wa[...])
              + jax.nn.sigmoid(gb[...]) * _dot(yb[...], wb[...]))
    h = x_ref[...] + _dot(merged.astype(BF16), wo[...])
    h_ref[...] = h
    xn = _rms(h, gf[...])
    xn_ref[...] = xn.astype(BF16)
    tm, d = xn.shape
    half = d // (2 * LANE)

    for c in range(half):
        xr_ref[pl.ds(c, tm, stride=half), :] = _pack_bf16_pairs(
            xn[:, c * LANE:(c + 1) * LANE], xn[:, (c + half) * LANE:(c + half + 1) * LANE])
    xh = xn.astype(BF16)
    xl = (xn - xh.astype(F32)).astype(BF16)
    lg_ref[...] = _dot_nt(wrh[...], xh) + _dot_nt(wrh[...], xl) + _dot_nt(wrl[...], xh)


def _merge(o_list, l_list, yb, gt, x2, wa, wb, wo, gf, wrh, wrl):
    n, d = x2.shape
    aw = wa.shape[0]
    bw = wb.shape[0]
    tm = ROW_TILE
    row = lambda w: pl.BlockSpec((tm, w), lambda i: (i, 0))
    full = lambda a: pl.BlockSpec(a.shape, lambda i: (0,) * a.ndim)
    return pl.pallas_call(
        _merge_kernel,
        grid=(n // tm,),
        in_specs=[pl.BlockSpec((aw // LANE, tm, LANE), lambda i: (0, i, 0))] * 6 + [
            row(bw),
            pl.BlockSpec((tm, d), lambda i: (i, 0)),
            pl.BlockSpec((tm, d), lambda i: (i, 1)),
            row(d), full(wa), full(wb), full(wo), full(gf), full(wrh), full(wrl),
        ],
        out_specs=[row(d), row(d), pl.BlockSpec((N_EXPERTS, tm), lambda i: (0, i)),
                   pl.BlockSpec((tm * d // (2 * LANE), LANE), lambda i: (i, 0))],
        out_shape=[
            jax.ShapeDtypeStruct((n, d), F32),
            jax.ShapeDtypeStruct((n, d), BF16),
            jax.ShapeDtypeStruct((N_EXPERTS, n), F32),
            jax.ShapeDtypeStruct((n * d // (2 * LANE), LANE), jnp.uint32),
        ],
        compiler_params=_params(("parallel",)),
    )(*o_list, *l_list, yb, gt, gt, x2, wa, wb, wo, gf, wrh, wrl)


def _route_kernel(lg_ref, bias_ref, tri_ref, idx_ref, w_ref, rank_ref, cnt_ref, carry_ref):
    E = N_EXPERTS
    tn = lg_ref.shape[1]
    scores = jax.nn.sigmoid(lg_ref[...])
    sel = scores + bias_ref[...]
    sub = lax.broadcasted_iota(jnp.int32, (GROUP_SIZE, tn), 0).astype(F32)
    gs = []
    for grp in range(N_GROUPS):
        blk = sel[grp * GROUP_SIZE:(grp + 1) * GROUP_SIZE]
        m1 = jnp.max(blk, axis=0, keepdims=True)
        first = jnp.min(jnp.where(blk == m1, sub, float(GROUP_SIZE)), axis=0, keepdims=True)
        m2 = jnp.max(jnp.where(sub == first, -jnp.inf, blk), axis=0, keepdims=True)
        gs.append(m1 + m2)
    keep = []
    for a in range(N_GROUPS):
        rank = jnp.zeros((1, tn), jnp.int32)
        for o in range(N_GROUPS):
            if o == a:
                continue
            beats = (gs[o] >= gs[a]) if o < a else (gs[o] > gs[a])
            rank = rank + beats.astype(jnp.int32)
        keep.append(jnp.broadcast_to(rank < TOP_GROUPS, (GROUP_SIZE, tn)))
    emask = jnp.concatenate(keep, axis=0)
    cand = jnp.where(emask, sel, -jnp.inf)
    eid = lax.broadcasted_iota(jnp.int32, (E, tn), 0)
    eid_f = eid.astype(F32)
    rank = jnp.zeros((E, tn), jnp.int32)
    for o in range(E):
        other = cand[o:o + 1]
        beats = (other > cand) | ((other == cand) & (eid > o))
        rank = rank + beats.astype(jnp.int32)
    chosen = emask & (rank < TOP_K)
    wsel = jnp.where(chosen, scores, 0.0)
    wnorm = wsel / jnp.sum(wsel, axis=0, keepdims=True) * ROUTED_SCALE

    @pl.when(pl.program_id(0) == 0)
    def _():
        carry_ref[...] = jnp.zeros_like(carry_ref)

    chosen_f = jnp.where(chosen, 1.0, 0.0)
    before = carry_ref[...] + _dot(chosen_f.astype(BF16), tri_ref[...]) - chosen_f
    carry_ref[...] = carry_ref[...] + jnp.sum(chosen_f, axis=1, keepdims=True)
    cnt_ref[...] = jnp.broadcast_to(carry_ref[...], cnt_ref.shape).astype(jnp.int32)
    for k in range(TOP_K):
        hit = chosen & (rank == k)
        idx_ref[k:k + 1, :] = jnp.sum(jnp.where(hit, eid_f, 0.0), axis=0, keepdims=True).astype(jnp.int32)
        w_ref[k:k + 1, :] = jnp.sum(jnp.where(hit, wnorm, 0.0), axis=0, keepdims=True)
        rank_ref[k:k + 1, :] = jnp.sum(jnp.where(hit, before, 0.0), axis=0, keepdims=True).astype(jnp.int32)


def _route(logits_t, bias_col):
    e, n = logits_t.shape
    tn = ROW_TILE
    tri = jnp.asarray(np.triu(np.ones((tn, tn), np.float32)), BF16)
    slot = pl.BlockSpec((TOP_K, tn), lambda i: (0, i))
    return pl.pallas_call(
        _route_kernel,
        grid=(n // tn,),
        in_specs=[pl.BlockSpec((e, tn), lambda i: (0, i)), pl.BlockSpec((e, 1), lambda i: (0, 0)),
                  pl.BlockSpec((tn, tn), lambda i: (0, 0))],
        out_specs=[slot, slot, slot, pl.BlockSpec((e, LANE), lambda i: (0, 0))],
        out_shape=[
            jax.ShapeDtypeStruct((TOP_K, n), jnp.int32),
            jax.ShapeDtypeStruct((TOP_K, n), F32),
            jax.ShapeDtypeStruct((TOP_K, n), jnp.int32),
            jax.ShapeDtypeStruct((e, LANE), jnp.int32),
        ],
        scratch_shapes=[pltpu.VMEM((e, 1), F32)],
        compiler_params=_params(("arbitrary",)),
    )(logits_t, bias_col, tri)


def _dispatch_sc(pos_t, xr3, p_rows):
    k, n = pos_t.shape
    _, sub, lane = xr3.shape
    workers = SC_CORES * SC_SUBCORES
    windows = n // SC_WINDOW
    per_worker = windows // workers
    assert n % (workers * SC_WINDOW) == 0
    mesh = plsc.VectorSubcoreMesh(core_axis_name="c", subcore_axis_name="s")

    @functools.partial(
        pl.kernel, mesh=mesh,
        out_type=jax.ShapeDtypeStruct((p_rows, sub, lane), xr3.dtype),
        scratch_types=[pltpu.VMEM((k, SC_WINDOW), jnp.int32), pltpu.VMEM((SC_WINDOW, sub, lane), xr3.dtype),
                       pltpu.SemaphoreType.DMA],
    )
    def scatter_rows(x_hbm, idx_hbm, o_hbm, idx_v, rows_v, sem):
        wid = lax.axis_index("s") * SC_CORES + lax.axis_index("c")

        @pl.loop(0, per_worker)
        def _(c):
            win = wid * per_worker + c
            pltpu.sync_copy(idx_hbm.at[win], idx_v)
            pltpu.sync_copy(x_hbm.at[pl.ds(win * SC_WINDOW, SC_WINDOW)], rows_v)
            for slot in range(k):
                pltpu.async_copy(rows_v, o_hbm.at[idx_v.at[slot]], sem).wait()

    return scatter_rows(xr3, pos_t.reshape(k, windows, SC_WINDOW).transpose(1, 0, 2))


def _combine_sc(pos_t, ys3):
    k, n = pos_t.shape
    _, sub, lane = ys3.shape
    nk = k * n
    workers = SC_CORES * SC_SUBCORES
    per_worker = nk // workers
    assert nk % (workers * SC_WINDOW) == 0
    mesh = plsc.VectorSubcoreMesh(core_axis_name="c", subcore_axis_name="s")

    @functools.partial(
        pl.kernel, mesh=mesh,
        out_type=jax.ShapeDtypeStruct((nk, sub, lane), ys3.dtype),
        scratch_types=[pltpu.VMEM((SC_WINDOW,), jnp.int32), pltpu.VMEM((SC_WINDOW, sub, lane), ys3.dtype),
                       pltpu.SemaphoreType.DMA],
    )
    def gather_rows(y_hbm, idx_hbm, o_hbm, idx_v, rows_v, sem):
        wid = lax.axis_index("s") * SC_CORES + lax.axis_index("c")

        @pl.loop(0, per_worker // SC_WINDOW)
        def _(c):
            base = wid * per_worker + c * SC_WINDOW
            pltpu.sync_copy(idx_hbm.at[pl.ds(base, SC_WINDOW)], idx_v)
            pltpu.async_copy(y_hbm.at[idx_v], rows_v, sem).wait()
            pltpu.sync_copy(rows_v, o_hbm.at[pl.ds(base, SC_WINDOW)])

    return gather_rows(ys3, pos_t.reshape(nk))


def _moe_kernel(blk_e_ref, nused_ref, nvalid_ref, xs_ref, wg_ref, wu_ref, wd_ref, o_ref):
    i = pl.program_id(0)

    @pl.when(i < nused_ref[0])
    def _():
        sub = xs_ref.shape[0] // MOE_ROWS
        owned = lax.broadcasted_iota(jnp.int32, (MOE_ROWS, LANE), 0) < nvalid_ref[i]
        words = [jnp.where(owned, xs_ref[pl.ds(c, MOE_ROWS, stride=sub), :], jnp.uint32(0)) for c in range(sub)]
        halves = [_unpack_bf16_pairs(w) for w in words]
        x = jnp.concatenate([lo for lo, _ in halves] + [hi for _, hi in halves], axis=1).astype(BF16)
        hg = _dot(x, wg_ref[0])
        hid = (hg * jax.nn.sigmoid(hg)) * _dot(x, wu_ref[0])
        y = _dot(hid.astype(BF16), wd_ref[0])
        for c in range(sub):
            o_ref[pl.ds(c, MOE_ROWS, stride=sub), :] = _pack_bf16_pairs(
                y[:, c * LANE:(c + 1) * LANE], y[:, (c + sub) * LANE:(c + sub + 1) * LANE])

    @pl.when(i >= nused_ref[0])
    def _():
        o_ref[...] = jnp.zeros_like(o_ref)


def _moe_experts(blk_e, nused, nvalid, xs, wg, wu, wd):
    d = wg.shape[1]
    sub = d // (2 * LANE)
    p = xs.shape[0] // sub
    hid = wg.shape[2]
    grid_spec = pltpu.PrefetchScalarGridSpec(
        num_scalar_prefetch=3,
        grid=(p // MOE_ROWS,),
        in_specs=[
            pl.BlockSpec((MOE_ROWS * sub, LANE), lambda i, be, nu, nv: (jnp.minimum(i, nu[0] - 1), 0)),
            pl.BlockSpec((1, d, hid), lambda i, be, nu, nv: (be[i], 0, 0)),
            pl.BlockSpec((1, d, hid), lambda i, be, nu, nv: (be[i], 0, 0)),
            pl.BlockSpec((1, hid, d), lambda i, be, nu, nv: (be[i], 0, 0)),
        ],
        out_specs=pl.BlockSpec((MOE_ROWS * sub, LANE), lambda i, be, nu, nv: (i, 0)),
    )
    return pl.pallas_call(
        _moe_kernel,
        grid_spec=grid_spec,
        out_shape=jax.ShapeDtypeStruct((p * sub, LANE), jnp.uint32),
        compiler_params=_params(("arbitrary",)),
    )(blk_e, nused, nvalid, xs, wg, wu, wd)


def _side_kernel(xn_ref, p_ref, wsg, wsu, wsd, wp, sh_ref, pe_ref):
    xn = xn_ref[...]
    hg = _dot(xn, wsg[...])
    sh_ref[...] = _dot(((hg * jax.nn.sigmoid(hg)) * _dot(xn, wsu[...])).astype(BF16), wsd[...]).astype(sh_ref.dtype)
    pe_ref[...] = _dot(p_ref[...].astype(BF16), wp[...]).astype(pe_ref.dtype)


def _side(xn2, p2, wsg, wsu, wsd, wp):
    n, d = xn2.shape
    tm = ROW_TILE
    row = lambda w: pl.BlockSpec((tm, w), lambda i: (i, 0))
    full = lambda a: pl.BlockSpec(a.shape, lambda i: (0,) * a.ndim)
    return pl.pallas_call(
        _side_kernel,
        grid=(n // tm,),
        in_specs=[row(d), row(p2.shape[1]), full(wsg), full(wsu), full(wsd), full(wp)],
        out_specs=[row(d), row(d)],
        out_shape=[jax.ShapeDtypeStruct((n, d), BF16)] * 2,
        compiler_params=_params(("parallel",)),
    )(xn2, p2, wsg, wsu, wsd, wp)


def _final_kernel(h_ref, sh_ref, pe_ref, ys_ref, w_ref, gp, wpg, gfin, o_ref):
    w = w_ref[...]
    tm = w.shape[0]
    sub = ys_ref.shape[1] // tm
    parts = [None] * (2 * sub)
    for k in range(TOP_K):
        wk = jnp.broadcast_to(w[:, k:k + 1], (tm, LANE))
        for c in range(sub):
            lo, hi = _unpack_bf16_pairs(ys_ref[k, pl.ds(c, tm, stride=sub), :])
            parts[c] = wk * lo if k == 0 else parts[c] + wk * lo
            parts[c + sub] = wk * hi if k == 0 else parts[c + sub] + wk * hi
    y = jnp.concatenate(parts, axis=1)
    h = h_ref[...] + (y + sh_ref[...].astype(F32))
    gate = jax.nn.sigmoid(_dot(_rms(h, gp[...]).astype(BF16), wpg[...]))
    h = h + pe_ref[...].astype(F32) * gate
    o_ref[...] = _rms(h, gfin[...])


def _final(h1, shared, ple, ysg, w_nk, gp, wpg, gfin):
    n, d = h1.shape
    tm = ROW_TILE
    row = lambda w: pl.BlockSpec((tm, w), lambda i: (i, 0))
    full = lambda a: pl.BlockSpec(a.shape, lambda i: (0,) * a.ndim)
    return pl.pallas_call(
        _final_kernel,
        grid=(n // tm,),
        in_specs=[row(d), row(d), row(d), pl.BlockSpec((TOP_K, tm * ysg.shape[1] // n, LANE), lambda i: (0, i, 0)),
                  row(TOP_K), full(gp), full(wpg), full(gfin)],
        out_specs=row(d),
        out_shape=jax.ShapeDtypeStruct((n, d), F32),
        compiler_params=_params(("parallel",)),
    )(h1, shared, ple, ysg, w_nk, gp, wpg, gfin)


def _pad_heads(w):
    d, c = w.shape
    nh = c // HEAD_DIM
    w = w.reshape(d, nh, HEAD_DIM)
    return jnp.pad(w, ((0, 0), (0, 0), (0, LANE - HEAD_DIM))).reshape(d, nh * LANE)


def _alibi_slopes(n):
    return 2.0 ** (-8.0 * jnp.arange(1, n + 1, dtype=jnp.float32) / n)


def kernel(x, p, g_mix, w_in, cmp_pos_k, cmp_w1_k, cmp_b1_k, cmp_w2_k, cmp_pos_v, cmp_w1_v, cmp_b1_v, cmp_w2_v, w_branch_a, w_branch_b, w_out, g_ffn, w_router, router_bias, w_gate, w_up, w_down, ws_gate, ws_up, ws_down, g_ple, w_ple_gate, w_ple, g_final):
    B, S, D = x.shape
    N = B * S
    G, R = NSA_KV_HEADS, NSA_REP
    assert w_in.shape[0] == 1 and p.shape[0] == 1
    assert S % DIL_STEP_ROWS == 0 and S % NSA_TK == 0 and S >= WIN_SIZE + NSA_Q
    assert N % ROW_TILE == 0 and N % INPROJ_ROWS == 0 and S % INPROJ_ROWS == 0
    slopes = _alibi_slopes(DIL_HEADS + NSA_HEADS)
    slopes_a, slopes_b = slopes[0::2], slopes[1::2]

    offs = np.concatenate([[0], np.cumsum(IN_SIZES)])
    parts = [w_in[0][:, offs[i]:offs[i + 1]] for i in range(len(IN_SIZES))]
    w_qa, w_ka, w_va, w_qb, w_kc, w_vc, w_ks, w_vs, w_kw, w_vw, w_gn, w_ga, w_gb = parts
    w_qkv_a = jnp.concatenate([w_qa * HEAD_DIM ** -0.5, w_ka, w_va], axis=1).astype(BF16)
    w_nsa = jnp.concatenate([_pad_heads(t) for t in (w_qb, w_kc, w_vc, w_ks, w_vs, w_kw, w_vw)],
                            axis=1).astype(BF16)
    nsa_w = w_nsa.shape[1]
    w_gn = w_gn.reshape(D, 3, G, R).transpose(0, 2, 1, 3).reshape(D, G, 3 * R)
    w_gn = jnp.pad(w_gn, ((0, 0), (0, 0), (0, LANE - 3 * R))).reshape(D, G * LANE)
    w_gate_in = jnp.concatenate([w_ga, w_gb, w_gn], axis=1).astype(BF16)
    gate_w = w_gate_in.shape[1]

    x2 = x.reshape(N, D)
    dils = tuple(dil for _, dil in DIL_PATTERNS if dil > 1)
    qkv_a, *dilated_qkv = _inproj(x2, g_mix, w_qkv_a, BF16, w_qkv_a.shape[1], dils=dils)
    nsa_in, = _inproj(x2, g_mix, w_nsa, BF16, nsa_w)
    gt, = _inproj(x2, g_mix, w_gate_in, F32, gate_w)

    o_list, l_list = [], []
    for _, dil in DIL_PATTERNS:
        src = qkv_a if dil == 1 else dilated_qkv[dils.index(dil)]
        o, l = _dilated(src.reshape(B, S // dil, -1), _dil_bias(slopes_a, dil), dil, B, S)
        o_list.append(o.reshape(-1, N, LANE))
        l_list.append(l.reshape(-1, N, LANE))

    nchunk = S // CMP_STRIDE
    c0 = NSA_HEADS * LANE
    raw = nsa_in[:, c0:c0 + 2 * G * LANE].reshape(B, nchunk, CMP_STRIDE, 2 * G, LANE)[..., :HEAD_DIM]
    xc = raw.transpose(0, 3, 1, 2, 4).reshape(B, 2 * G, nchunk, CMP_STRIDE * HEAD_DIM)
    pos = jnp.stack([cmp_pos_k[0].reshape(1, -1), cmp_pos_v[0].reshape(1, -1)])
    w1 = jnp.stack([cmp_w1_k[0], cmp_w1_v[0]]).astype(BF16)
    b1 = jnp.stack([cmp_b1_k[0][None], cmp_b1_v[0][None]])
    w2 = jnp.pad(jnp.stack([cmp_w2_k[0], cmp_w2_v[0]]), ((0, 0), (0, 0), (0, LANE - HEAD_DIM))).astype(BF16)
    kcv = _compress(xc, pos, w1, b1, w2)
    kcv = kcv.reshape(B, 2, G, nchunk, LANE).transpose(1, 0, 2, 3, 4).reshape(2, B * G, nchunk, LANE)
    vct = kcv[1].transpose(0, 2, 1)
    v0 = (NSA_HEADS + 3 * G) * LANE
    vsw = jnp.stack([nsa_in[:, v0:v0 + G * LANE], nsa_in[:, v0 + 2 * G * LANE:v0 + 3 * G * LANE]])
    vsw = vsw.reshape(2, B, S // LANE, LANE, G, LANE).transpose(1, 4, 0, 2, 5, 3)
    ones_row = (jnp.arange(LANE) == SUM_LANE).astype(BF16)[:, None]
    vt = (vsw + ones_row).reshape(B * G, 2, S // LANE, LANE, LANE)
    yb = _nsa(slopes_b, nsa_in.reshape(B, S, nsa_w), gt.reshape(B, S, gate_w), kcv[0], vct, vt,
              B, S, 0, 2 * D // LANE)
    yb = yb.reshape(N, NSA_HEADS * LANE)

    wb_pad = jnp.pad(w_branch_b[0].reshape(NSA_HEADS, HEAD_DIM, D),
                     ((0, 0), (0, LANE - HEAD_DIM), (0, 0))).reshape(NSA_HEADS * LANE, D)
    wr_t = w_router[0].T
    wr_hi = wr_t.astype(BF16)
    wr_lo = (wr_t - wr_hi.astype(F32)).astype(BF16)
    h1, xn2, logits_t, xr = _merge(o_list, l_list, yb, gt, x2, w_branch_a[0].astype(BF16), wb_pad.astype(BF16),
                               w_out[0].astype(BF16), g_ffn, wr_hi, wr_lo)

    shared, ple = _side(xn2, p[0].reshape(N, -1), ws_gate[0].astype(BF16), ws_up[0].astype(BF16),
                        ws_down[0].astype(BF16), w_ple[0].astype(BF16))

    idx_t, wsel_t, rank_t, counts = _route(logits_t, router_bias[0][:, None])
    counts = counts[:, 0]
    padded = (counts + MOE_ROWS - 1) // MOE_ROWS * MOE_ROWS
    pend = jnp.cumsum(padded)
    pstart = pend - padded
    hot = idx_t[None] == jnp.arange(N_EXPERTS, dtype=jnp.int32)[:, None, None]
    pos_t = rank_t + jnp.sum(jnp.where(hot, pstart.astype(jnp.int32)[:, None, None], 0), axis=0)
    NK = N * TOP_K
    nblk = (NK + N_EXPERTS * (MOE_ROWS - 1) + MOE_ROWS - 1) // MOE_ROWS
    P = nblk * MOE_ROWS
    blk_e = jnp.minimum(jnp.sum(pend[None, :] <= (jnp.arange(nblk, dtype=jnp.int32) * MOE_ROWS)[:, None], axis=1),
                        N_EXPERTS - 1).astype(jnp.int32)
    nused = (pend[-1] // MOE_ROWS).astype(jnp.int32).reshape(1)
    row0 = jnp.arange(nblk, dtype=jnp.int32) * MOE_ROWS
    nvalid = jnp.clip((pstart + counts)[blk_e] - row0, 0, MOE_ROWS).astype(jnp.int32)
    sub = D // (2 * LANE)
    xs = _dispatch_sc(pos_t, xr.reshape(N, sub, LANE), P).reshape(P * sub, LANE)
    ys = _moe_experts(blk_e, nused, nvalid, xs, w_gate[0].astype(BF16), w_up[0].astype(BF16), w_down[0].astype(BF16))
    ysg = _combine_sc(pos_t, ys.reshape(P, sub, LANE)).reshape(TOP_K, N * sub, LANE)
    out = _final(h1, shared, ple, ysg, wsel_t.T, g_ple, w_ple_gate[0].astype(BF16), g_final[None, :])
    return out.reshape(B, S, D)
```

```python
import functools

import numpy as np
import jax
import jax.numpy as jnp
from jax import lax
from jax.experimental import pallas as pl
from jax.experimental.pallas import tpu as pltpu
from jax.experimental.pallas import tpu_sc as plsc

F32 = jnp.float32
BF16 = jnp.bfloat16

HEAD_DIM = 64
LANE = 128
RMS_EPS = 1e-6
DIL_HEADS = 8
DIL_PATTERNS = ((128, 1), (512, 4), (2048, 16))
DIL_BLOCK = 128
DIL_STEP_ROWS = 4096
NSA_HEADS = 8
NSA_KV_HEADS = 2
NSA_REP = NSA_HEADS // NSA_KV_HEADS
CMP_BLOCK = 32
CMP_STRIDE = 16
CMP_HIDDEN = 256
SLC_BLOCK = 64
SLC_TOP_N = 16
SLC_FORCE = 1e9
WIN_SIZE = 512
N_EXPERTS = 64
TOP_K = 8
N_GROUPS = 8
TOP_GROUPS = 4
GROUP_SIZE = N_EXPERTS // N_GROUPS
ROUTED_SCALE = 2.5
IN_SIZES = (512, 512, 512, 512, 128, 128, 128, 128, 128, 128, 24, 1024, 1024)

NEG = -1e30
NT_DIMS = (((1,), (1,)), ((), ()))

ROW_TILE = 512
INPROJ_ROWS = 1024
NSA_Q = 256
NSA_TK = 512
MOE_ROWS = 2048
VMEM_LIMIT = 48 * 1024 * 1024
SC_CORES = 2
SC_SUBCORES = 16
SC_WINDOW = 128
COMBINE_PARTS = 4


def _dot(a, b):
    return jnp.dot(a, b, preferred_element_type=F32)


def _dot_nt(a, b):
    return lax.dot_general(a, b, NT_DIMS, preferred_element_type=F32)


def _split3(x):
    hi = x.astype(BF16)
    r1 = x - hi.astype(F32)
    mid = r1.astype(BF16)
    lo = (r1 - mid.astype(F32)).astype(BF16)
    return hi, mid, lo


def _pack_bf16_pairs(lo, hi):
    lo_bits = pltpu.bitcast(lo.astype(BF16).astype(F32), jnp.uint32)
    hi_bits = pltpu.bitcast(hi.astype(BF16).astype(F32), jnp.uint32)
    return hi_bits | (lo_bits >> 16)


def _unpack_bf16_pairs(words):
    return (pltpu.bitcast(words << 16, F32), pltpu.bitcast(words & jnp.uint32(0xFFFF0000), F32))


def _rms(x, g):
    r = lax.rsqrt(jnp.mean(x * x, axis=-1, keepdims=True) + RMS_EPS)
    return (x * r) * g


def _params(sem):
    return pltpu.CompilerParams(dimension_semantics=sem, vmem_limit_bytes=VMEM_LIMIT)


def _inproj_kernel(x_ref, g_ref, w_ref, o_ref, *rest, dils):
    dil_refs, xn_ref = rest[:len(dils)], rest[len(dils)]
    res_ref = rest[len(dils) + 1] if dils else None

    @pl.when(pl.program_id(1) == 0)
    def _():
        xn_ref[...] = _rms(x_ref[...], g_ref[...]).astype(BF16)

    res = _dot(xn_ref[...], w_ref[...])
    o_ref[...] = res.astype(o_ref.dtype)

    if dils:
        tn = res.shape[1]
        for c in range(tn // LANE):
            res_ref[c] = res[:, c * LANE:(c + 1) * LANE]
        for d_ref, dil in zip(dil_refs, dils):
            for r in range(dil):
                for c in range(tn // LANE):
                    piece = res_ref[c, pl.ds(r, INPROJ_ROWS // dil, stride=dil), :]
                    d_ref[:, r * tn + c * LANE:r * tn + (c + 1) * LANE] = piece.astype(d_ref.dtype)


def _inproj(x2, g, w, out_dtype, tn, dils=()):
    n, d = x2.shape
    cols = w.shape[1]
    out_specs = [pl.BlockSpec((INPROJ_ROWS, tn), lambda i, j: (i, j))]
    out_shape = [jax.ShapeDtypeStruct((n, cols), out_dtype)]
    for dil in dils:
        out_specs.append(pl.BlockSpec((INPROJ_ROWS // dil, dil * tn), lambda i, j: (i, j)))
        out_shape.append(jax.ShapeDtypeStruct((n // dil, cols * dil), out_dtype))
    return pl.pallas_call(
        functools.partial(_inproj_kernel, dils=dils),
        grid=(n // INPROJ_ROWS, cols // tn),
        in_specs=[
            pl.BlockSpec((INPROJ_ROWS, d), lambda i, j: (i, 0)),
            pl.BlockSpec((1, d), lambda i, j: (0, 0)),
            pl.BlockSpec((d, tn), lambda i, j: (0, j)),
        ],
        out_specs=out_specs,
        out_shape=out_shape,
        scratch_shapes=[pltpu.VMEM((INPROJ_ROWS, d), BF16)]
        + ([pltpu.VMEM((tn // LANE, INPROJ_ROWS, LANE), F32)] if dils else []),
        compiler_params=_params(("parallel", "arbitrary")),
    )(x2, g, w)


def _dil_kernel(q_ref, kp_ref, kc_ref, vp_ref, vc_ref, bias_ref, o_ref, l_ref, *, dil, nq):
    i = pl.program_id(1)
    res = pl.program_id(2)
    blk = DIL_BLOCK
    lane = lax.broadcasted_iota(jnp.int32, (blk, LANE), 1)
    lo = lane < HEAD_DIM
    for sub in range(nq):
        rows = slice(sub * blk, (sub + 1) * blk)
        prev = slice((sub - 1) * blk, sub * blk)
        out_rows = pl.ds(sub * blk * dil + res, blk, stride=dil) if dil > 1 else rows
        variant = jnp.minimum(i, 1) * DIL_HEADS if sub == 0 else DIL_HEADS
        for hp in range(DIL_HEADS // 2):
            sl = slice(hp * LANE, (hp + 1) * LANE)
            q2 = q_ref[0, rows, sl]
            k_prev = kp_ref[0, :, sl] if sub == 0 else kc_ref[0, prev, sl]
            v_prev = vp_ref[0, :, sl] if sub == 0 else vc_ref[0, prev, sl]
            k2 = jnp.concatenate([k_prev, kc_ref[0, rows, sl]], axis=0)
            v2 = jnp.concatenate([v_prev, vc_ref[0, rows, sl]], axis=0)
            outs, lses = [], []
            for hh in range(2):
                keep = lo if hh == 0 else jnp.logical_not(lo)
                qm = jnp.where(keep, q2, jnp.zeros_like(q2))
                s = _dot_nt(qm, k2) + bias_ref[variant + 2 * hp + hh]
                m = jnp.max(s, axis=-1, keepdims=True)
                e = jnp.exp(s - m)
                den = jnp.sum(e, axis=-1, keepdims=True)
                outs.append(_dot(e.astype(BF16), v2) / den)
                lses.append(jnp.broadcast_to(m + jnp.log(den), (blk, LANE)))
            o_ref[hp, 0, out_rows, :] = jnp.where(lo, outs[0], outs[1])
            l_ref[hp, 0, out_rows, :] = jnp.where(lo, lses[0], lses[1])


def _dil_bias(slopes_a, dil):
    blk = DIL_BLOCK
    qi = jnp.arange(blk)[:, None]
    kj = jnp.arange(2 * blk)[None, :]
    dist = (qi + blk - kj).astype(F32)
    band = (dist >= 0) & (dist <= blk)
    slope = (slopes_a * dil)[:, None, None]
    bias = -slope * dist
    first = jnp.where(band & (kj >= blk), bias, NEG)
    rest = jnp.where(band, bias, NEG)
    return jnp.concatenate([first, rest], axis=0)


def _dilated(src3, bias, dil, b, s):
    nq = min(8, DIL_STEP_ROWS // (DIL_BLOCK * dil))
    nstep = s // dil // (DIL_BLOCK * nq)
    aw = DIL_HEADS * HEAD_DIM
    npair = DIL_HEADS // 2

    def spec(which, prev):
        if prev:
            return pl.BlockSpec((1, DIL_BLOCK, aw), lambda bb, i, r: (bb, jnp.maximum(nq * i - 1, 0), 3 * r + which))
        return pl.BlockSpec((1, DIL_BLOCK * nq, aw), lambda bb, i, r: (bb, i, 3 * r + which))

    out_spec = pl.BlockSpec((npair, 1, DIL_BLOCK * nq * dil, LANE), lambda bb, i, r: (0, bb, i, 0))
    return pl.pallas_call(
        functools.partial(_dil_kernel, dil=dil, nq=nq),
        grid=(b, nstep, dil),
        in_specs=[
            spec(0, False), spec(1, True), spec(1, False), spec(2, True), spec(2, False),
            pl.BlockSpec(bias.shape, lambda bb, i, r: (0, 0, 0)),
        ],
        out_specs=[out_spec, out_spec],
        out_shape=[jax.ShapeDtypeStruct((npair, b, s, LANE), F32)] * 2,
        compiler_params=_params(("parallel", "arbitrary", "arbitrary")),
    )(src3, src3, src3, src3, src3, bias)


def _cmp_kernel(x_ref, pos_ref, w1_ref, b1_ref, w2_ref, o_ref):
    half = x_ref.shape[-1]
    x = x_ref[0, 0]
    w1 = w1_ref[0]
    first = _dot(x, w1[:half])
    second = _dot(x, w1[half:])
    nrow = x.shape[0]
    pre = first + pltpu.roll(second, nrow - 1, 0)
    posb = _dot(jnp.broadcast_to(pos_ref[0], (8, 2 * half)).astype(BF16), w1)[0:1]
    pre = pre + posb + b1_ref[0]
    o_ref[0, 0] = _dot(jax.nn.gelu(pre).astype(BF16), w2_ref[0]).astype(o_ref.dtype)


def _compress(xc, pos, w1, b1, w2):
    b, four, nchunk, width = xc.shape
    return pl.pallas_call(
        _cmp_kernel,
        grid=(b, four),
        in_specs=[
            pl.BlockSpec((1, 1, nchunk, width), lambda bb, j: (bb, j, 0, 0)),
            pl.BlockSpec((1, 1, 2 * width), lambda bb, j: (j // 2, 0, 0)),
            pl.BlockSpec((1, 2 * width, CMP_HIDDEN), lambda bb, j: (j // 2, 0, 0)),
            pl.BlockSpec((1, 1, CMP_HIDDEN), lambda bb, j: (j // 2, 0, 0)),
            pl.BlockSpec((1, CMP_HIDDEN, LANE), lambda bb, j: (j // 2, 0, 0)),
        ],
        out_specs=pl.BlockSpec((1, 1, nchunk, LANE), lambda bb, j: (bb, j, 0, 0)),
        out_shape=jax.ShapeDtypeStruct((b, four, nchunk, LANE), BF16),
        compiler_params=_params(("parallel", "parallel")),
    )(xc, pos, w1, b1, w2)


POS_LANE = HEAD_DIM
POS_SPLIT = 64
SUM_LANE = HEAD_DIM
MASKED = -1e30
ROW_FLOOR = -1e29


def _pos_columns(pos):
    out = np.zeros((pos.shape[0], LANE), np.float32)
    out[:, POS_LANE] = pos // POS_SPLIT
    out[:, POS_LANE + 1] = pos % POS_SPLIT
    out[:, POS_LANE + 2] = 1.0
    out[:, POS_LANE + 3] = 1.0
    return jnp.asarray(out, BF16)


def _nsa_kernel(slope_ref, q_ref, kc_ref, vct_ref, ks_ref, kw_ref, vt_ref, gt_ref, ov_ref, kpa_ref, cpa_ref,
                oh_ref, o_ref, kaug_ref, flag_ref, list_ref, sbuf_ref, m_ref, acc_ref, *, seq):
    g = pl.program_id(1)
    c = pl.program_id(2)
    C = NSA_Q
    R = NSA_REP
    TK = NSA_TK
    rows = R * C
    t0 = c * C
    scale = HEAD_DIM ** -0.5
    nb = seq // SLC_BLOCK
    ncmp = seq // CMP_STRIDE
    n_sel = min(SLC_TOP_N, nb)
    vblocks = TK // LANE

    @pl.when(c == 0)
    def _():
        kaug_ref[:, 0:LANE] = ks_ref[0] + kpa_ref[...]
        kaug_ref[:, LANE:2 * LANE] = oh_ref[...]

    row = lax.broadcasted_iota(jnp.int32, (rows, 1), 0)
    ti = t0 + row % C
    slope = jnp.zeros((rows, 1), F32)
    for r in range(R):
        slope = jnp.where(row // C == r, slope_ref[g * R + r], slope)
    lane = lax.broadcasted_iota(jnp.int32, (rows, LANE), 1)
    thi = (ti // POS_SPLIT).astype(F32)
    tlo = (ti % POS_SPLIT).astype(F32)
    qcols = jnp.where(lane == POS_LANE, POS_SPLIT * slope,
                      jnp.where(lane == POS_LANE + 1, slope,
                                jnp.where(lane == POS_LANE + 2, -POS_SPLIT * slope * thi,
                                          jnp.where(lane == POS_LANE + 3, -slope * tlo, 0.0))))
    q4 = jnp.concatenate([q_ref[0, :, r * LANE:(r + 1) * LANE] for r in range(R)], axis=0)
    q4 = (q4.astype(F32) * scale + qcols).astype(BF16)
    tq = t0 + lax.broadcasted_iota(jnp.int32, (1, C), 1)

    def heads(x):
        return jnp.concatenate([x] * R, axis=1)

    kc = kc_ref[0] + cpa_ref[...]
    ci = lax.broadcasted_iota(jnp.int32, (ncmp, 1), 0)
    vis = (ci * CMP_STRIDE + (CMP_BLOCK - 1) <= tq) & (ci < ncmp - 1)
    s = _dot_nt(kc, q4) + heads(jnp.where(vis, 0.0, MASKED))
    m = jnp.maximum(jnp.max(s, axis=0, keepdims=True), ROW_FLOOR)
    e = jnp.exp(s - m)
    den = jnp.sum(e, axis=0, keepdims=True)
    pc = e * (1.0 / jnp.where(den > 0, den, 1.0))
    o_cmp = _dot(vct_ref[0], pc.astype(BF16))
    pcsum = pc[:, 0:C]
    for r in range(1, R):
        pcsum = pcsum + pc[:, r * C:(r + 1) * C]
    ov = ov_ref[...]
    p_slc = sum(_dot(ov, part) for part in _split3(pcsum))

    def values_t(which, first_block, n):
        return jnp.concatenate([vt_ref[0, which, first_block + i] for i in range(n)], axis=1)

    span = WIN_SIZE + C
    ws = pl.multiple_of(jnp.maximum(t0 - WIN_SIZE, 0), C)
    k_w = kw_ref[0, pl.ds(ws, span), :] + kpa_ref[pl.ds(ws, span), :]
    wpos = ws + lax.broadcasted_iota(jnp.int32, (span, 1), 0)
    wbias = jnp.where((wpos <= tq) & (wpos >= tq - (WIN_SIZE - 1)), 0.0, MASKED)
    s3 = _dot_nt(k_w, q4) + heads(wbias)
    e3 = jnp.exp(s3 - jnp.max(s3, axis=0, keepdims=True))
    ow = _dot(values_t(1, ws // LANE, span // LANE), e3.astype(BF16))
    o_win = ow / ow[SUM_LANE:SUM_LANE + 1, :]

    j = lax.broadcasted_iota(jnp.int32, (LANE, C), 0)
    cur = tq // SLC_BLOCK
    forced = (j == 0) | (j == cur) | (j == cur - 1)
    future = j * SLC_BLOCK > tq
    work = jnp.where(forced, -jnp.inf, jnp.where(future, -SLC_FORCE, p_slc))
    work = jnp.where(j < nb, work, -jnp.inf)
    sel = jnp.where(forced, 1.0, 0.0)
    jf = j.astype(F32)
    for _ in range(n_sel - 3):
        mx = jnp.max(work, axis=0, keepdims=True)
        first = jnp.min(jnp.where(work == mx, jf, float(LANE)), axis=0, keepdims=True)
        pick = jf == first
        sel = jnp.where(pick, 1.0, sel)
        work = jnp.where(pick, -jnp.inf, work)
    sel = jnp.where(future, 0.0, sel)
    blocks_per_tile = TK // SLC_BLOCK
    for kt in range(seq // TK):
        flag_ref[kt] = jnp.max(sel[kt * blocks_per_tile:(kt + 1) * blocks_per_tile, :])
    sel_bias = jnp.where(sel.T > 0.5, 0.0, MASKED).astype(BF16)
    q_aug = jnp.concatenate([q4, jnp.concatenate([sel_bias] * R, axis=0)], axis=1)

    m_ref[...] = jnp.full((1, rows), ROW_FLOOR, F32)
    acc_ref[...] = jnp.zeros((LANE, rows), F32)
    kio = lax.broadcasted_iota(jnp.int32, (TK, 1), 0)

    def scores(kt):
        start = pl.multiple_of(kt * TK, TK)
        return _dot_nt(kaug_ref[pl.ds(start, TK), :], q_aug)

    def absorb(s2, kt):
        m_i = m_ref[...]
        m_n = jnp.maximum(m_i, jnp.max(s2, axis=0, keepdims=True))
        alpha = jnp.exp(m_i - m_n)
        p2 = jnp.exp(s2 - m_n)
        acc_ref[...] = alpha * acc_ref[...] + _dot(values_t(0, kt * vblocks, vblocks), p2.astype(BF16))
        m_ref[...] = m_n

    last = (t0 + C - 1) // TK

    def compact(kt, n):
        list_ref[n] = kt
        return n + (flag_ref[kt] > 0.0).astype(jnp.int32)

    n_vis = lax.fori_loop(0, last, compact, 0)
    list_ref[n_vis] = last

    sbuf_ref[0] = scores(list_ref[0])

    def slc_body(j, carry):
        s_next = scores(list_ref[j + 1])
        absorb(sbuf_ref[j % 2], list_ref[j])
        sbuf_ref[(j + 1) % 2] = s_next
        return carry

    lax.fori_loop(0, n_vis, slc_body, 0)
    start = pl.multiple_of(last * TK, TK)
    absorb(sbuf_ref[n_vis % 2] + heads(jnp.where(start + kio <= tq, 0.0, MASKED)), last)
    acc = acc_ref[...]
    o_slc = acc / acc[SUM_LANE:SUM_LANE + 1, :]

    gates = jax.nn.sigmoid(gt_ref[0]).T
    for r in range(R):
        rs = slice(r * C, (r + 1) * C)
        out = (gates[r:r + 1] * o_cmp[:, rs] + gates[R + r:R + r + 1] * o_slc[:, rs]
               + gates[2 * R + r:2 * R + r + 1] * o_win[:, rs])
        o_ref[0, :, r * LANE:(r + 1) * LANE] = out.T.astype(o_ref.dtype)


def _nsa(slopes_b, pa3, gt3, kc, vct, vt, b, s, col0, gcol0):
    C = NSA_Q
    R = NSA_REP
    G = NSA_KV_HEADS
    nb = s // SLC_BLOCK
    ncmp = s // CMP_STRIDE
    assert nb <= LANE
    qc0 = col0 // R
    kv0 = col0 + NSA_HEADS
    rows = R * C

    ci = np.arange(ncmp)[None, :]
    sj = np.arange(LANE)[:, None]
    ov = ((ci * CMP_STRIDE < (sj + 1) * SLC_BLOCK) & (ci * CMP_STRIDE + CMP_BLOCK > sj * SLC_BLOCK)
          & (ci < ncmp - 1) & (sj < nb))
    ov = jnp.asarray(ov, BF16)
    kpa = _pos_columns(np.arange(s))
    cpa = _pos_columns(np.arange(ncmp) * CMP_STRIDE + (CMP_BLOCK - 1))
    oh = jnp.asarray(np.arange(s)[:, None] // SLC_BLOCK == np.arange(LANE)[None, :], BF16)

    def k_spec(which):
        return pl.BlockSpec((1, s, LANE), lambda bb, g, c: (bb, 0, kv0 + which * G + g))

    const = lambda a: pl.BlockSpec(a.shape, lambda bb, g, c: (0,) * a.ndim)
    return pl.pallas_call(
        functools.partial(_nsa_kernel, seq=s),
        grid=(b, G, s // C),
        in_specs=[
            pl.BlockSpec(memory_space=pltpu.SMEM),
            pl.BlockSpec((1, C, R * LANE), lambda bb, g, c: (bb, c, qc0 + g)),
            pl.BlockSpec((1, ncmp, LANE), lambda bb, g, c: (bb * G + g, 0, 0)),
            pl.BlockSpec((1, LANE, ncmp), lambda bb, g, c: (bb * G + g, 0, 0)),
            k_spec(2), k_spec(4),
            pl.BlockSpec((1, 2, s // LANE, LANE, LANE), lambda bb, g, c: (bb * G + g, 0, 0, 0, 0)),
            pl.BlockSpec((1, C, LANE), lambda bb, g, c: (bb, c, gcol0 + g)),
            const(ov), const(kpa), const(cpa), const(oh),
        ],
        out_specs=pl.BlockSpec((1, C, R * LANE), lambda bb, g, c: (bb, c, g)),
        out_shape=jax.ShapeDtypeStruct((b, s, NSA_HEADS * LANE), BF16),
        scratch_shapes=[
            pltpu.VMEM((s, 2 * LANE), BF16),
            pltpu.SMEM((s // NSA_TK,), F32),
            pltpu.SMEM((s // NSA_TK,), jnp.int32),
            pltpu.VMEM((2, NSA_TK, rows), F32),
            pltpu.VMEM((1, rows), F32),
            pltpu.VMEM((LANE, rows), F32),
        ],
        compiler_params=_params(("parallel", "parallel", "arbitrary")),
    )(slopes_b, pa3, kc, vct, pa3, pa3, vt, gt3, ov, kpa, cpa, oh)


def _merge_kernel(o1, o2, o3, l1, l2, l3, yb, ga, gb, x_ref, wa, wb, wo, gf, wrh, wrl,
                  h_ref, xn_ref, lg_ref, xr_ref):
    def pairs(ref):
        return jnp.concatenate([ref[h] for h in range(ref.shape[0])], axis=1)

    a1, a2, a3 = pairs(l1), pairs(l2), pairs(l3)
    mx = jnp.maximum(jnp.maximum(a1, a2), a3)
    e1, e2, e3 = jnp.exp(a1 - mx), jnp.exp(a2 - mx), jnp.exp(a3 - mx)
    tot = e1 + e2 + e3
    ya = (e1 / tot) * pairs(o1) + (e2 / tot) * pairs(o2) + (e3 / tot) * pairs(o3)
    merged = (jax.nn.sigmoid(ga[...]) * _dot(ya.astype(BF16), wa[...])
              + jax.nn.sigmoid(gb[...]) * _dot(yb[...], wb[...]))
    h = x_ref[...] + _dot(merged.astype(BF16), wo[...])
    h_ref[...] = h
    xn = _rms(h, gf[...])
    xn_ref[...] = xn.astype(BF16)
    tm, d = xn.shape
    half = d // (2 * LANE)

    for c in range(half):
        xr_ref[pl.ds(c, tm, stride=half), :] = _pack_bf16_pairs(
            xn[:, c * LANE:(c + 1) * LANE], xn[:, (c + half) * LANE:(c + half + 1) * LANE])
    xh = xn.astype(BF16)
    xl = (xn - xh.astype(F32)).astype(BF16)
    lg_ref[...] = _dot_nt(wrh[...], xh) + _dot_nt(wrh[...], xl) + _dot_nt(wrl[...], xh)


def _merge(o_list, l_list, yb, gt, x2, wa, wb, wo, gf, wrh, wrl):
    n, d = x2.shape
    aw = wa.shape[0]
    bw = wb.shape[0]
    tm = ROW_TILE
    row = lambda w: pl.BlockSpec((tm, w), lambda i: (i, 0))
    full = lambda a: pl.BlockSpec(a.shape, lambda i: (0,) * a.ndim)
    return pl.pallas_call(
        _merge_kernel,
        grid=(n // tm,),
        in_specs=[pl.BlockSpec((aw // LANE, tm, LANE), lambda i: (0, i, 0))] * 6 + [
            row(bw),
            pl.BlockSpec((tm, d), lambda i: (i, 0)),
            pl.BlockSpec((tm, d), lambda i: (i, 1)),
            row(d), full(wa), full(wb), full(wo), full(gf), full(wrh), full(wrl),
        ],
        out_specs=[row(d), row(d), pl.BlockSpec((N_EXPERTS, tm), lambda i: (0, i)),
                   pl.BlockSpec((tm * d // (2 * LANE), LANE), lambda i: (i, 0))],
        out_shape=[
            jax.ShapeDtypeStruct((n, d), F32),
            jax.ShapeDtypeStruct((n, d), BF16),
            jax.ShapeDtypeStruct((N_EXPERTS, n), F32),
            jax.ShapeDtypeStruct((n * d // (2 * LANE), LANE), jnp.uint32),
        ],
        compiler_params=_params(("parallel",)),
    )(*o_list, *l_list, yb, gt, gt, x2, wa, wb, wo, gf, wrh, wrl)


def _route_kernel(lg_ref, bias_ref, tri_ref, idx_ref, w_ref, rank_ref, cnt_ref, carry_ref):
    E = N_EXPERTS
    tn = lg_ref.shape[1]
    scores = jax.nn.sigmoid(lg_ref[...])
    sel = scores + bias_ref[...]
    sub = lax.broadcasted_iota(jnp.int32, (GROUP_SIZE, tn), 0).astype(F32)
    gs = []
    for grp in range(N_GROUPS):
        blk = sel[grp * GROUP_SIZE:(grp + 1) * GROUP_SIZE]
        m1 = jnp.max(blk, axis=0, keepdims=True)
        first = jnp.min(jnp.where(blk == m1, sub, float(GROUP_SIZE)), axis=0, keepdims=True)
        m2 = jnp.max(jnp.where(sub == first, -jnp.inf, blk), axis=0, keepdims=True)
        gs.append(m1 + m2)
    keep = []
    for a in range(N_GROUPS):
        rank = jnp.zeros((1, tn), jnp.int32)
        for o in range(N_GROUPS):
            if o == a:
                continue
            beats = (gs[o] >= gs[a]) if o < a else (gs[o] > gs[a])
            rank = rank + beats.astype(jnp.int32)
        keep.append(jnp.broadcast_to(rank < TOP_GROUPS, (GROUP_SIZE, tn)))
    emask = jnp.concatenate(keep, axis=0)
    cand = jnp.where(emask, sel, -jnp.inf)
    eid = lax.broadcasted_iota(jnp.int32, (E, tn), 0)
    eid_f = eid.astype(F32)
    rank = jnp.zeros((E, tn), jnp.int32)
    for o in range(E):
        other = cand[o:o + 1]
        beats = (other > cand) | ((other == cand) & (eid > o))
        rank = rank + beats.astype(jnp.int32)
    chosen = emask & (rank < TOP_K)
    wsel = jnp.where(chosen, scores, 0.0)
    wnorm = wsel / jnp.sum(wsel, axis=0, keepdims=True) * ROUTED_SCALE

    @pl.when(pl.program_id(0) == 0)
    def _():
        carry_ref[...] = jnp.zeros_like(carry_ref)

    chosen_f = jnp.where(chosen, 1.0, 0.0)
    before = carry_ref[...] + _dot(chosen_f.astype(BF16), tri_ref[...]) - chosen_f
    carry_ref[...] = carry_ref[...] + jnp.sum(chosen_f, axis=1, keepdims=True)
    cnt_ref[...] = jnp.broadcast_to(carry_ref[...], cnt_ref.shape).astype(jnp.int32)
    for k in range(TOP_K):
        hit = chosen & (rank == k)
        idx_ref[k:k + 1, :] = jnp.sum(jnp.where(hit, eid_f, 0.0), axis=0, keepdims=True).astype(jnp.int32)
        w_ref[k:k + 1, :] = jnp.sum(jnp.where(hit, wnorm, 0.0), axis=0, keepdims=True)
        rank_ref[k:k + 1, :] = jnp.sum(jnp.where(hit, before, 0.0), axis=0, keepdims=True).astype(jnp.int32)


def _route(logits_t, bias_col):
    e, n = logits_t.shape
    tn = ROW_TILE
    tri = jnp.asarray(np.triu(np.ones((tn, tn), np.float32)), BF16)
    slot = pl.BlockSpec((TOP_K, tn), lambda i: (0, i))
    return pl.pallas_call(
        _route_kernel,
        grid=(n // tn,),
        in_specs=[pl.BlockSpec((e, tn), lambda i: (0, i)), pl.BlockSpec((e, 1), lambda i: (0, 0)),
                  pl.BlockSpec((tn, tn), lambda i: (0, 0))],
        out_specs=[slot, slot, slot, pl.BlockSpec((e, LANE), lambda i: (0, 0))],
        out_shape=[
            jax.ShapeDtypeStruct((TOP_K, n), jnp.int32),
            jax.ShapeDtypeStruct((TOP_K, n), F32),
            jax.ShapeDtypeStruct((TOP_K, n), jnp.int32),
            jax.ShapeDtypeStruct((e, LANE), jnp.int32),
        ],
        scratch_shapes=[pltpu.VMEM((e, 1), F32)],
        compiler_params=_params(("arbitrary",)),
    )(logits_t, bias_col, tri)


def _dispatch_sc(pos_t, xr3, p_rows):
    k, n = pos_t.shape
    _, sub, lane = xr3.shape
    workers = SC_CORES * SC_SUBCORES
    windows = n // SC_WINDOW
    per_worker = windows // workers
    assert n % (workers * SC_WINDOW) == 0
    mesh = plsc.VectorSubcoreMesh(core_axis_name="c", subcore_axis_name="s")

    @functools.partial(
        pl.kernel, mesh=mesh,
        out_type=jax.ShapeDtypeStruct((p_rows, sub, lane), xr3.dtype),
        scratch_types=[pltpu.VMEM((k, SC_WINDOW), jnp.int32), pltpu.VMEM((SC_WINDOW, sub, lane), xr3.dtype),
                       pltpu.SemaphoreType.DMA],
    )
    def scatter_rows(x_hbm, idx_hbm, o_hbm, idx_v, rows_v, sem):
        wid = lax.axis_index("s") * SC_CORES + lax.axis_index("c")

        @pl.loop(0, per_worker)
        def _(c):
            win = wid * per_worker + c
            pltpu.sync_copy(idx_hbm.at[win], idx_v)
            pltpu.sync_copy(x_hbm.at[pl.ds(win * SC_WINDOW, SC_WINDOW)], rows_v)
            for slot in range(k):
                pltpu.async_copy(rows_v, o_hbm.at[idx_v.at[slot]], sem).wait()

    return scatter_rows(xr3, pos_t.reshape(k, windows, SC_WINDOW).transpose(1, 0, 2))


def _combine_sc(pos_t, ys3):
    k, n = pos_t.shape
    _, sub, lane = ys3.shape
    nk = k * n
    workers = SC_CORES * SC_SUBCORES
    per_worker = nk // workers
    assert nk % (workers * SC_WINDOW) == 0
    mesh = plsc.VectorSubcoreMesh(core_axis_name="c", subcore_axis_name="s")

    @functools.partial(
        pl.kernel, mesh=mesh,
        out_type=jax.ShapeDtypeStruct((nk, sub, lane), ys3.dtype),
        scratch_types=[pltpu.VMEM((SC_WINDOW,), jnp.int32), pltpu.VMEM((SC_WINDOW, sub, lane), ys3.dtype),
                       pltpu.SemaphoreType.DMA],
    )
    def gather_rows(y_hbm, idx_hbm, o_hbm, idx_v, rows_v, sem):
        wid = lax.axis_index("s") * SC_CORES + lax.axis_index("c")

        @pl.loop(0, per_worker // SC_WINDOW)
        def _(c):
            base = wid * per_worker + c * SC_WINDOW
            pltpu.sync_copy(idx_hbm.at[pl.ds(base, SC_WINDOW)], idx_v)
            pltpu.async_copy(y_hbm.at[idx_v], rows_v, sem).wait()
            pltpu.sync_copy(rows_v, o_hbm.at[pl.ds(base, SC_WINDOW)])

    return gather_rows(ys3, pos_t.reshape(nk))


def _moe_kernel(blk_e_ref, nused_ref, nvalid_ref, xs_ref, wg_ref, wu_ref, wd_ref, o_ref):
    i = pl.program_id(0)

    @pl.when(i < nused_ref[0])
    def _():
        sub = xs_ref.shape[0] // MOE_ROWS
        owned = lax.broadcasted_iota(jnp.int32, (MOE_ROWS, LANE), 0) < nvalid_ref[i]
        words = [jnp.where(owned, xs_ref[pl.ds(c, MOE_ROWS, stride=sub), :], jnp.uint32(0)) for c in range(sub)]
        halves = [_unpack_bf16_pairs(w) for w in words]
        x = jnp.concatenate([lo for lo, _ in halves] + [hi for _, hi in halves], axis=1).astype(BF16)
        hg = _dot(x, wg_ref[0])
        hid = (hg * jax.nn.sigmoid(hg)) * _dot(x, wu_ref[0])
        y = _dot(hid.astype(BF16), wd_ref[0])
        for c in range(sub):
            o_ref[pl.ds(c, MOE_ROWS, stride=sub), :] = _pack_bf16_pairs(
                y[:, c * LANE:(c + 1) * LANE], y[:, (c + sub) * LANE:(c + sub + 1) * LANE])

    @pl.when(i >= nused_ref[0])
    def _():
        o_ref[...] = jnp.zeros_like(o_ref)


def _moe_experts(blk_e, nused, nvalid, xs, wg, wu, wd):
    d = wg.shape[1]
    sub = d // (2 * LANE)
    p = xs.shape[0] // sub
    hid = wg.shape[2]
    grid_spec = pltpu.PrefetchScalarGridSpec(
        num_scalar_prefetch=3,
        grid=(p // MOE_ROWS,),
        in_specs=[
            pl.BlockSpec((MOE_ROWS * sub, LANE), lambda i, be, nu, nv: (jnp.minimum(i, nu[0] - 1), 0)),
            pl.BlockSpec((1, d, hid), lambda i, be, nu, nv: (be[i], 0, 0)),
            pl.BlockSpec((1, d, hid), lambda i, be, nu, nv: (be[i], 0, 0)),
            pl.BlockSpec((1, hid, d), lambda i, be, nu, nv: (be[i], 0, 0)),
        ],
        out_specs=pl.BlockSpec((MOE_ROWS * sub, LANE), lambda i, be, nu, nv: (i, 0)),
    )
    return pl.pallas_call(
        _moe_kernel,
        grid_spec=grid_spec,
        out_shape=jax.ShapeDtypeStruct((p * sub, LANE), jnp.uint32),
        compiler_params=_params(("arbitrary",)),
    )(blk_e, nused, nvalid, xs, wg, wu, wd)


def _side_kernel(xn_ref, p_ref, wsg, wsu, wsd, wp, sh_ref, pe_ref):
    xn = xn_ref[...]
    hg = _dot(xn, wsg[...])
    sh_ref[...] = _dot(((hg * jax.nn.sigmoid(hg)) * _dot(xn, wsu[...])).astype(BF16), wsd[...]).astype(sh_ref.dtype)
    pe_ref[...] = _dot(p_ref[...].astype(BF16), wp[...]).astype(pe_ref.dtype)


def _side(xn2, p2, wsg, wsu, wsd, wp):
    n, d = xn2.shape
    tm = ROW_TILE
    row = lambda w: pl.BlockSpec((tm, w), lambda i: (i, 0))
    full = lambda a: pl.BlockSpec(a.shape, lambda i: (0,) * a.ndim)
    return pl.pallas_call(
        _side_kernel,
        grid=(n // tm,),
        in_specs=[row(d), row(p2.shape[1]), full(wsg), full(wsu), full(wsd), full(wp)],
        out_specs=[row(d), row(d)],
        out_shape=[jax.ShapeDtypeStruct((n, d), BF16)] * 2,
        compiler_params=_params(("parallel",)),
    )(xn2, p2, wsg, wsu, wsd, wp)


def _final_kernel(h_ref, sh_ref, pe_ref, ys_ref, w_ref, gp, wpg, gfin, o_ref):
    w = w_ref[...]
    tm = w.shape[0]
    sub = ys_ref.shape[1] // tm
    parts = [None] * (2 * sub)
    for k in range(TOP_K):
        wk = jnp.broadcast_to(w[:, k:k + 1], (tm, LANE))
        for c in range(sub):
            lo, hi = _unpack_bf16_pairs(ys_ref[k, pl.ds(c, tm, stride=sub), :])
            parts[c] = wk * lo if k == 0 else parts[c] + wk * lo
            parts[c + sub] = wk * hi if k == 0 else parts[c + sub] + wk * hi
    y = jnp.concatenate(parts, axis=1)
    h = h_ref[...] + (y + sh_ref[...].astype(F32))
    gate = jax.nn.sigmoid(_dot(_rms(h, gp[...]).astype(BF16), wpg[...]))
    h = h + pe_ref[...].astype(F32) * gate
    o_ref[...] = _rms(h, gfin[...])


def _final(h1, shared, ple, ysg, w_nk, gp, wpg, gfin, part, out_prev):
    n, d = h1.shape
    tm = ROW_TILE
    steps = ysg.shape[1] * LANE * 2 // d // tm
    first = part * steps
    row = lambda w: pl.BlockSpec((tm, w), lambda i: (first + i, 0))
    full = lambda a: pl.BlockSpec(a.shape, lambda i: (0,) * a.ndim)
    in_specs = [row(d), row(d), row(d), pl.BlockSpec((TOP_K, ysg.shape[1] // steps, LANE), lambda i: (0, i, 0)),
                row(TOP_K), full(gp), full(wpg), full(gfin)]
    args = [h1, shared, ple, ysg, w_nk, gp, wpg, gfin]
    kernel_fn, aliases = _final_kernel, {}
    if out_prev is not None:
        in_specs.append(pl.BlockSpec(memory_space=pl.ANY))
        args.append(out_prev)
        aliases = {len(args) - 1: 0}
        kernel_fn = lambda *refs: _final_kernel(*refs[:8], refs[9])
    return pl.pallas_call(
        kernel_fn,
        grid=(steps,),
        in_specs=in_specs,
        out_specs=row(d),
        out_shape=jax.ShapeDtypeStruct((n, d), F32),
        input_output_aliases=aliases,
        compiler_params=_params(("parallel",)),
    )(*args)


def _pad_heads(w):
    d, c = w.shape
    nh = c // HEAD_DIM
    w = w.reshape(d, nh, HEAD_DIM)
    return jnp.pad(w, ((0, 0), (0, 0), (0, LANE - HEAD_DIM))).reshape(d, nh * LANE)


def _alibi_slopes(n):
    return 2.0 ** (-8.0 * jnp.arange(1, n + 1, dtype=jnp.float32) / n)


def kernel(x, p, g_mix, w_in, cmp_pos_k, cmp_w1_k, cmp_b1_k, cmp_w2_k, cmp_pos_v, cmp_w1_v, cmp_b1_v, cmp_w2_v, w_branch_a, w_branch_b, w_out, g_ffn, w_router, router_bias, w_gate, w_up, w_down, ws_gate, ws_up, ws_down, g_ple, w_ple_gate, w_ple, g_final):
    B, S, D = x.shape
    N = B * S
    G, R = NSA_KV_HEADS, NSA_REP
    assert w_in.shape[0] == 1 and p.shape[0] == 1
    assert S % DIL_STEP_ROWS == 0 and S % NSA_TK == 0 and S >= WIN_SIZE + NSA_Q
    assert N % (ROW_TILE * COMBINE_PARTS) == 0 and N % INPROJ_ROWS == 0 and S % INPROJ_ROWS == 0
    slopes = _alibi_slopes(DIL_HEADS + NSA_HEADS)
    slopes_a, slopes_b = slopes[0::2], slopes[1::2]

    offs = np.concatenate([[0], np.cumsum(IN_SIZES)])
    parts = [w_in[0][:, offs[i]:offs[i + 1]] for i in range(len(IN_SIZES))]
    w_qa, w_ka, w_va, w_qb, w_kc, w_vc, w_ks, w_vs, w_kw, w_vw, w_gn, w_ga, w_gb = parts
    w_qkv_a = jnp.concatenate([w_qa * HEAD_DIM ** -0.5, w_ka, w_va], axis=1).astype(BF16)
    w_nsa = jnp.concatenate([_pad_heads(t) for t in (w_qb, w_kc, w_vc, w_ks, w_vs, w_kw, w_vw)],
                            axis=1).astype(BF16)
    nsa_w = w_nsa.shape[1]
    w_gn = w_gn.reshape(D, 3, G, R).transpose(0, 2, 1, 3).reshape(D, G, 3 * R)
    w_gn = jnp.pad(w_gn, ((0, 0), (0, 0), (0, LANE - 3 * R))).reshape(D, G * LANE)
    w_gate_in = jnp.concatenate([w_ga, w_gb, w_gn], axis=1).astype(BF16)
    gate_w = w_gate_in.shape[1]

    x2 = x.reshape(N, D)
    dils = tuple(dil for _, dil in DIL_PATTERNS if dil > 1)
    qkv_a, *dilated_qkv = _inproj(x2, g_mix, w_qkv_a, BF16, w_qkv_a.shape[1], dils=dils)
    nsa_in, = _inproj(x2, g_mix, w_nsa, BF16, nsa_w)
    gt, = _inproj(x2, g_mix, w_gate_in, F32, gate_w)

    o_list, l_list = [], []
    for _, dil in DIL_PATTERNS:
        src = qkv_a if dil == 1 else dilated_qkv[dils.index(dil)]
        o, l = _dilated(src.reshape(B, S // dil, -1), _dil_bias(slopes_a, dil), dil, B, S)
        o_list.append(o.reshape(-1, N, LANE))
        l_list.append(l.reshape(-1, N, LANE))

    nchunk = S // CMP_STRIDE
    c0 = NSA_HEADS * LANE
    raw = nsa_in[:, c0:c0 + 2 * G * LANE].reshape(B, nchunk, CMP_STRIDE, 2 * G, LANE)[..., :HEAD_DIM]
    xc = raw.transpose(0, 3, 1, 2, 4).reshape(B, 2 * G, nchunk, CMP_STRIDE * HEAD_DIM)
    pos = jnp.stack([cmp_pos_k[0].reshape(1, -1), cmp_pos_v[0].reshape(1, -1)])
    w1 = jnp.stack([cmp_w1_k[0], cmp_w1_v[0]]).astype(BF16)
    b1 = jnp.stack([cmp_b1_k[0][None], cmp_b1_v[0][None]])
    w2 = jnp.pad(jnp.stack([cmp_w2_k[0], cmp_w2_v[0]]), ((0, 0), (0, 0), (0, LANE - HEAD_DIM))).astype(BF16)
    kcv = _compress(xc, pos, w1, b1, w2)
    kcv = kcv.reshape(B, 2, G, nchunk, LANE).transpose(1, 0, 2, 3, 4).reshape(2, B * G, nchunk, LANE)
    vct = kcv[1].transpose(0, 2, 1)
    v0 = (NSA_HEADS + 3 * G) * LANE
    vsw = jnp.stack([nsa_in[:, v0:v0 + G * LANE], nsa_in[:, v0 + 2 * G * LANE:v0 + 3 * G * LANE]])
    vsw = vsw.reshape(2, B, S // LANE, LANE, G, LANE).transpose(1, 4, 0, 2, 5, 3)
    ones_row = (jnp.arange(LANE) == SUM_LANE).astype(BF16)[:, None]
    vt = (vsw + ones_row).reshape(B * G, 2, S // LANE, LANE, LANE)
    yb = _nsa(slopes_b, nsa_in.reshape(B, S, nsa_w), gt.reshape(B, S, gate_w), kcv[0], vct, vt,
              B, S, 0, 2 * D // LANE)
    yb = yb.reshape(N, NSA_HEADS * LANE)

    wb_pad = jnp.pad(w_branch_b[0].reshape(NSA_HEADS, HEAD_DIM, D),
                     ((0, 0), (0, LANE - HEAD_DIM), (0, 0))).reshape(NSA_HEADS * LANE, D)
    wr_t = w_router[0].T
    wr_hi = wr_t.astype(BF16)
    wr_lo = (wr_t - wr_hi.astype(F32)).astype(BF16)
    h1, xn2, logits_t, xr = _merge(o_list, l_list, yb, gt, x2, w_branch_a[0].astype(BF16), wb_pad.astype(BF16),
                               w_out[0].astype(BF16), g_ffn, wr_hi, wr_lo)

    shared, ple = _side(xn2, p[0].reshape(N, -1), ws_gate[0].astype(BF16), ws_up[0].astype(BF16),
                        ws_down[0].astype(BF16), w_ple[0].astype(BF16))

    idx_t, wsel_t, rank_t, counts = _route(logits_t, router_bias[0][:, None])
    counts = counts[:, 0]
    padded = (counts + MOE_ROWS - 1) // MOE_ROWS * MOE_ROWS
    pend = jnp.cumsum(padded)
    pstart = pend - padded
    hot = idx_t[None] == jnp.arange(N_EXPERTS, dtype=jnp.int32)[:, None, None]
    pos_t = rank_t + jnp.sum(jnp.where(hot, pstart.astype(jnp.int32)[:, None, None], 0), axis=0)
    NK = N * TOP_K
    nblk = (NK + N_EXPERTS * (MOE_ROWS - 1) + MOE_ROWS - 1) // MOE_ROWS
    P = nblk * MOE_ROWS
    blk_e = jnp.minimum(jnp.sum(pend[None, :] <= (jnp.arange(nblk, dtype=jnp.int32) * MOE_ROWS)[:, None], axis=1),
                        N_EXPERTS - 1).astype(jnp.int32)
    nused = (pend[-1] // MOE_ROWS).astype(jnp.int32).reshape(1)
    row0 = jnp.arange(nblk, dtype=jnp.int32) * MOE_ROWS
    nvalid = jnp.clip((pstart + counts)[blk_e] - row0, 0, MOE_ROWS).astype(jnp.int32)
    sub = D // (2 * LANE)
    xs = _dispatch_sc(pos_t, xr.reshape(N, sub, LANE), P).reshape(P * sub, LANE)
    ys = _moe_experts(blk_e, nused, nvalid, xs, w_gate[0].astype(BF16), w_up[0].astype(BF16), w_down[0].astype(BF16))
    n_part = N // COMBINE_PARTS
    ys3, w_nk, wpg, out = ys.reshape(P, sub, LANE), wsel_t.T, w_ple_gate[0].astype(BF16), None
    for part in range(COMBINE_PARTS):
        ysg = _combine_sc(pos_t[:, part * n_part:(part + 1) * n_part], ys3).reshape(TOP_K, n_part * sub, LANE)
        out = _final(h1, shared, ple, ysg, w_nk, g_ple, wpg, g_final[None, :], part, out)
    return out.reshape(B, S, D)
```

```python
import functools

import numpy as np
import jax
import jax.numpy as jnp
from jax import lax
from jax.experimental import pallas as pl
from jax.experimental.pallas import tpu as pltpu
from jax.experimental.pallas import tpu_sc as plsc

F32 = jnp.float32
BF16 = jnp.bfloat16

HEAD_DIM = 64
LANE = 128
RMS_EPS = 1e-6
DIL_HEADS = 8
DIL_PATTERNS = ((128, 1), (512, 4), (2048, 16))
DIL_BLOCK = 128
DIL_STEP_ROWS = 4096
NSA_HEADS = 8
NSA_KV_HEADS = 2
NSA_REP = NSA_HEADS // NSA_KV_HEADS
CMP_BLOCK = 32
CMP_STRIDE = 16
CMP_HIDDEN = 256
SLC_BLOCK = 64
SLC_TOP_N = 16
SLC_FORCE = 1e9
WIN_SIZE = 512
N_EXPERTS = 64
TOP_K = 8
N_GROUPS = 8
TOP_GROUPS = 4
GROUP_SIZE = N_EXPERTS // N_GROUPS
ROUTED_SCALE = 2.5
IN_SIZES = (512, 512, 512, 512, 128, 128, 128, 128, 128, 128, 24, 1024, 1024)

NEG = -1e30
NT_DIMS = (((1,), (1,)), ((), ()))

ROW_TILE = 512
INPROJ_ROWS = 1024
NSA_Q = 256
NSA_TK = 512
MOE_ROWS = 2048
VMEM_LIMIT = 48 * 1024 * 1024
SC_CORES = 2
SC_SUBCORES = 16
SC_WINDOW = 128
COMBINE_PARTS = 8


def _dot(a, b):
    return jnp.dot(a, b, preferred_element_type=F32)


def _dot_nt(a, b):
    return lax.dot_general(a, b, NT_DIMS, preferred_element_type=F32)


def _split3(x):
    hi = x.astype(BF16)
    r1 = x - hi.astype(F32)
    mid = r1.astype(BF16)
    lo = (r1 - mid.astype(F32)).astype(BF16)
    return hi, mid, lo


def _pack_bf16_pairs(lo, hi):
    lo_bits = pltpu.bitcast(lo.astype(BF16).astype(F32), jnp.uint32)
    hi_bits = pltpu.bitcast(hi.astype(BF16).astype(F32), jnp.uint32)
    return hi_bits | (lo_bits >> 16)


def _unpack_bf16_pairs(words):
    return (pltpu.bitcast(words << 16, F32), pltpu.bitcast(words & jnp.uint32(0xFFFF0000), F32))


def _rms(x, g):
    r = lax.rsqrt(jnp.mean(x * x, axis=-1, keepdims=True) + RMS_EPS)
    return (x * r) * g


def _params(sem):
    return pltpu.CompilerParams(dimension_semantics=sem, vmem_limit_bytes=VMEM_LIMIT)


def _inproj_kernel(x_ref, g_ref, w_ref, o_ref, *rest, dils):
    dil_refs, xn_ref = rest[:len(dils)], rest[len(dils)]
    res_ref = rest[len(dils) + 1] if dils else None

    @pl.when(pl.program_id(1) == 0)
    def _():
        xn_ref[...] = _rms(x_ref[...], g_ref[...]).astype(BF16)

    res = _dot(xn_ref[...], w_ref[...])
    o_ref[...] = res.astype(o_ref.dtype)

    if dils:
        tn = res.shape[1]
        for c in range(tn // LANE):
            res_ref[c] = res[:, c * LANE:(c + 1) * LANE]
        for d_ref, dil in zip(dil_refs, dils):
            for r in range(dil):
                for c in range(tn // LANE):
                    piece = res_ref[c, pl.ds(r, INPROJ_ROWS // dil, stride=dil), :]
                    d_ref[:, r * tn + c * LANE:r * tn + (c + 1) * LANE] = piece.astype(d_ref.dtype)


def _inproj(x2, g, w, out_dtype, tn, dils=()):
    n, d = x2.shape
    cols = w.shape[1]
    out_specs = [pl.BlockSpec((INPROJ_ROWS, tn), lambda i, j: (i, j))]
    out_shape = [jax.ShapeDtypeStruct((n, cols), out_dtype)]
    for dil in dils:
        out_specs.append(pl.BlockSpec((INPROJ_ROWS // dil, dil * tn), lambda i, j: (i, j)))
        out_shape.append(jax.ShapeDtypeStruct((n // dil, cols * dil), out_dtype))
    return pl.pallas_call(
        functools.partial(_inproj_kernel, dils=dils),
        grid=(n // INPROJ_ROWS, cols // tn),
        in_specs=[
            pl.BlockSpec((INPROJ_ROWS, d), lambda i, j: (i, 0)),
            pl.BlockSpec((1, d), lambda i, j: (0, 0)),
            pl.BlockSpec((d, tn), lambda i, j: (0, j)),
        ],
        out_specs=out_specs,
        out_shape=out_shape,
        scratch_shapes=[pltpu.VMEM((INPROJ_ROWS, d), BF16)]
        + ([pltpu.VMEM((tn // LANE, INPROJ_ROWS, LANE), F32)] if dils else []),
        compiler_params=_params(("parallel", "arbitrary")),
    )(x2, g, w)


def _dil_kernel(q_ref, kp_ref, kc_ref, vp_ref, vc_ref, bias_ref, o_ref, l_ref, *, dil, nq):
    i = pl.program_id(1)
    res = pl.program_id(2)
    blk = DIL_BLOCK
    lane = lax.broadcasted_iota(jnp.int32, (blk, LANE), 1)
    lo = lane < HEAD_DIM
    for sub in range(nq):
        rows = slice(sub * blk, (sub + 1) * blk)
        prev = slice((sub - 1) * blk, sub * blk)
        out_rows = pl.ds(sub * blk * dil + res, blk, stride=dil) if dil > 1 else rows
        variant = jnp.minimum(i, 1) * DIL_HEADS if sub == 0 else DIL_HEADS
        for hp in range(DIL_HEADS // 2):
            sl = slice(hp * LANE, (hp + 1) * LANE)
            q2 = q_ref[0, rows, sl]
            k_prev = kp_ref[0, :, sl] if sub == 0 else kc_ref[0, prev, sl]
            v_prev = vp_ref[0, :, sl] if sub == 0 else vc_ref[0, prev, sl]
            k2 = jnp.concatenate([k_prev, kc_ref[0, rows, sl]], axis=0)
            v2 = jnp.concatenate([v_prev, vc_ref[0, rows, sl]], axis=0)
            outs, lses = [], []
            for hh in range(2):
                keep = lo if hh == 0 else jnp.logical_not(lo)
                qm = jnp.where(keep, q2, jnp.zeros_like(q2))
                s = _dot_nt(qm, k2) + bias_ref[variant + 2 * hp + hh]
                m = jnp.max(s, axis=-1, keepdims=True)
                e = jnp.exp(s - m)
                den = jnp.sum(e, axis=-1, keepdims=True)
                outs.append(_dot(e.astype(BF16), v2) / den)
                lses.append(jnp.broadcast_to(m + jnp.log(den), (blk, LANE)))
            o_ref[hp, 0, out_rows, :] = jnp.where(lo, outs[0], outs[1])
            l_ref[hp, 0, out_rows, :] = jnp.where(lo, lses[0], lses[1])


def _dil_bias(slopes_a, dil):
    blk = DIL_BLOCK
    qi = jnp.arange(blk)[:, None]
    kj = jnp.arange(2 * blk)[None, :]
    dist = (qi + blk - kj).astype(F32)
    band = (dist >= 0) & (dist <= blk)
    slope = (slopes_a * dil)[:, None, None]
    bias = -slope * dist
    first = jnp.where(band & (kj >= blk), bias, NEG)
    rest = jnp.where(band, bias, NEG)
    return jnp.concatenate([first, rest], axis=0)


def _dilated(src3, bias, dil, b, s):
    nq = min(8, DIL_STEP_ROWS // (DIL_BLOCK * dil))
    nstep = s // dil // (DIL_BLOCK * nq)
    aw = DIL_HEADS * HEAD_DIM
    npair = DIL_HEADS // 2

    def spec(which, prev):
        if prev:
            return pl.BlockSpec((1, DIL_BLOCK, aw), lambda bb, i, r: (bb, jnp.maximum(nq * i - 1, 0), 3 * r + which))
        return pl.BlockSpec((1, DIL_BLOCK * nq, aw), lambda bb, i, r: (bb, i, 3 * r + which))

    out_spec = pl.BlockSpec((npair, 1, DIL_BLOCK * nq * dil, LANE), lambda bb, i, r: (0, bb, i, 0))
    return pl.pallas_call(
        functools.partial(_dil_kernel, dil=dil, nq=nq),
        grid=(b, nstep, dil),
        in_specs=[
            spec(0, False), spec(1, True), spec(1, False), spec(2, True), spec(2, False),
            pl.BlockSpec(bias.shape, lambda bb, i, r: (0, 0, 0)),
        ],
        out_specs=[out_spec, out_spec],
        out_shape=[jax.ShapeDtypeStruct((npair, b, s, LANE), F32)] * 2,
        compiler_params=_params(("parallel", "arbitrary", "arbitrary")),
    )(src3, src3, src3, src3, src3, bias)


def _cmp_kernel(x_ref, pos_ref, w1_ref, b1_ref, w2_ref, o_ref):
    half = x_ref.shape[-1]
    x = x_ref[0, 0]
    w1 = w1_ref[0]
    first = _dot(x, w1[:half])
    second = _dot(x, w1[half:])
    nrow = x.shape[0]
    pre = first + pltpu.roll(second, nrow - 1, 0)
    posb = _dot(jnp.broadcast_to(pos_ref[0], (8, 2 * half)).astype(BF16), w1)[0:1]
    pre = pre + posb + b1_ref[0]
    o_ref[0, 0] = _dot(jax.nn.gelu(pre).astype(BF16), w2_ref[0]).astype(o_ref.dtype)


def _compress(xc, pos, w1, b1, w2):
    b, four, nchunk, width = xc.shape
    return pl.pallas_call(
        _cmp_kernel,
        grid=(b, four),
        in_specs=[
            pl.BlockSpec((1, 1, nchunk, width), lambda bb, j: (bb, j, 0, 0)),
            pl.BlockSpec((1, 1, 2 * width), lambda bb, j: (j // 2, 0, 0)),
            pl.BlockSpec((1, 2 * width, CMP_HIDDEN), lambda bb, j: (j // 2, 0, 0)),
            pl.BlockSpec((1, 1, CMP_HIDDEN), lambda bb, j: (j // 2, 0, 0)),
            pl.BlockSpec((1, CMP_HIDDEN, LANE), lambda bb, j: (j // 2, 0, 0)),
        ],
        out_specs=pl.BlockSpec((1, 1, nchunk, LANE), lambda bb, j: (bb, j, 0, 0)),
        out_shape=jax.ShapeDtypeStruct((b, four, nchunk, LANE), BF16),
        compiler_params=_params(("parallel", "parallel")),
    )(xc, pos, w1, b1, w2)


POS_LANE = HEAD_DIM
POS_SPLIT = 64
SUM_LANE = HEAD_DIM
MASKED = -1e30
ROW_FLOOR = -1e29


def _pos_columns(pos):
    out = np.zeros((pos.shape[0], LANE), np.float32)
    out[:, POS_LANE] = pos // POS_SPLIT
    out[:, POS_LANE + 1] = pos % POS_SPLIT
    out[:, POS_LANE + 2] = 1.0
    out[:, POS_LANE + 3] = 1.0
    return jnp.asarray(out, BF16)


def _nsa_kernel(slope_ref, q_ref, kc_ref, vct_ref, ks_ref, kw_ref, vt_ref, gt_ref, ov_ref, kpa_ref, cpa_ref,
                oh_ref, o_ref, kaug_ref, flag_ref, list_ref, sbuf_ref, m_ref, acc_ref, *, seq):
    g = pl.program_id(1)
    c = pl.program_id(2)
    C = NSA_Q
    R = NSA_REP
    TK = NSA_TK
    rows = R * C
    t0 = c * C
    scale = HEAD_DIM ** -0.5
    nb = seq // SLC_BLOCK
    ncmp = seq // CMP_STRIDE
    n_sel = min(SLC_TOP_N, nb)
    vblocks = TK // LANE

    @pl.when(c == 0)
    def _():
        kaug_ref[:, 0:LANE] = ks_ref[0] + kpa_ref[...]
        kaug_ref[:, LANE:2 * LANE] = oh_ref[...]

    row = lax.broadcasted_iota(jnp.int32, (rows, 1), 0)
    ti = t0 + row % C
    slope = jnp.zeros((rows, 1), F32)
    for r in range(R):
        slope = jnp.where(row // C == r, slope_ref[g * R + r], slope)
    lane = lax.broadcasted_iota(jnp.int32, (rows, LANE), 1)
    thi = (ti // POS_SPLIT).astype(F32)
    tlo = (ti % POS_SPLIT).astype(F32)
    qcols = jnp.where(lane == POS_LANE, POS_SPLIT * slope,
                      jnp.where(lane == POS_LANE + 1, slope,
                                jnp.where(lane == POS_LANE + 2, -POS_SPLIT * slope * thi,
                                          jnp.where(lane == POS_LANE + 3, -slope * tlo, 0.0))))
    q4 = jnp.concatenate([q_ref[0, :, r * LANE:(r + 1) * LANE] for r in range(R)], axis=0)
    q4 = (q4.astype(F32) * scale + qcols).astype(BF16)
    tq = t0 + lax.broadcasted_iota(jnp.int32, (1, C), 1)

    def heads(x):
        return jnp.concatenate([x] * R, axis=1)

    kc = kc_ref[0] + cpa_ref[...]
    ci = lax.broadcasted_iota(jnp.int32, (ncmp, 1), 0)
    vis = (ci * CMP_STRIDE + (CMP_BLOCK - 1) <= tq) & (ci < ncmp - 1)
    s = _dot_nt(kc, q4) + heads(jnp.where(vis, 0.0, MASKED))
    m = jnp.maximum(jnp.max(s, axis=0, keepdims=True), ROW_FLOOR)
    e = jnp.exp(s - m)
    den = jnp.sum(e, axis=0, keepdims=True)
    pc = e * (1.0 / jnp.where(den > 0, den, 1.0))
    o_cmp = _dot(vct_ref[0], pc.astype(BF16))
    pcsum = pc[:, 0:C]
    for r in range(1, R):
        pcsum = pcsum + pc[:, r * C:(r + 1) * C]
    ov = ov_ref[...]
    p_slc = sum(_dot(ov, part) for part in _split3(pcsum))

    def values_t(which, first_block, n):
        return jnp.concatenate([vt_ref[0, which, first_block + i] for i in range(n)], axis=1)

    span = WIN_SIZE + C
    ws = pl.multiple_of(jnp.maximum(t0 - WIN_SIZE, 0), C)
    k_w = kw_ref[0, pl.ds(ws, span), :] + kpa_ref[pl.ds(ws, span), :]
    wpos = ws + lax.broadcasted_iota(jnp.int32, (span, 1), 0)
    wbias = jnp.where((wpos <= tq) & (wpos >= tq - (WIN_SIZE - 1)), 0.0, MASKED)
    s3 = _dot_nt(k_w, q4) + heads(wbias)
    e3 = jnp.exp(s3 - jnp.max(s3, axis=0, keepdims=True))
    ow = _dot(values_t(1, ws // LANE, span // LANE), e3.astype(BF16))
    o_win = ow / ow[SUM_LANE:SUM_LANE + 1, :]

    j = lax.broadcasted_iota(jnp.int32, (LANE, C), 0)
    cur = tq // SLC_BLOCK
    forced = (j == 0) | (j == cur) | (j == cur - 1)
    future = j * SLC_BLOCK > tq
    work = jnp.where(forced, -jnp.inf, jnp.where(future, -SLC_FORCE, p_slc))
    work = jnp.where(j < nb, work, -jnp.inf)
    sel = jnp.where(forced, 1.0, 0.0)
    jf = j.astype(F32)
    for _ in range(n_sel - 3):
        mx = jnp.max(work, axis=0, keepdims=True)
        first = jnp.min(jnp.where(work == mx, jf, float(LANE)), axis=0, keepdims=True)
        pick = jf == first
        sel = jnp.where(pick, 1.0, sel)
        work = jnp.where(pick, -jnp.inf, work)
    sel = jnp.where(future, 0.0, sel)
    blocks_per_tile = TK // SLC_BLOCK
    for kt in range(seq // TK):
        flag_ref[kt] = jnp.max(sel[kt * blocks_per_tile:(kt + 1) * blocks_per_tile, :])
    sel_bias = jnp.where(sel.T > 0.5, 0.0, MASKED).astype(BF16)
    q_aug = jnp.concatenate([q4, jnp.concatenate([sel_bias] * R, axis=0)], axis=1)

    m_ref[...] = jnp.full((1, rows), ROW_FLOOR, F32)
    acc_ref[...] = jnp.zeros((LANE, rows), F32)
    kio = lax.broadcasted_iota(jnp.int32, (TK, 1), 0)

    def scores(kt):
        start = pl.multiple_of(kt * TK, TK)
        return _dot_nt(kaug_ref[pl.ds(start, TK), :], q_aug)

    def absorb(s2, kt):
        m_i = m_ref[...]
        m_n = jnp.maximum(m_i, jnp.max(s2, axis=0, keepdims=True))
        alpha = jnp.exp(m_i - m_n)
        p2 = jnp.exp(s2 - m_n)
        acc_ref[...] = alpha * acc_ref[...] + _dot(values_t(0, kt * vblocks, vblocks), p2.astype(BF16))
        m_ref[...] = m_n

    last = (t0 + C - 1) // TK

    def compact(kt, n):
        list_ref[n] = kt
        return n + (flag_ref[kt] > 0.0).astype(jnp.int32)

    n_vis = lax.fori_loop(0, last, compact, 0)
    list_ref[n_vis] = last

    sbuf_ref[0] = scores(list_ref[0])

    def slc_body(j, carry):
        s_next = scores(list_ref[j + 1])
        absorb(sbuf_ref[j % 2], list_ref[j])
        sbuf_ref[(j + 1) % 2] = s_next
        return carry

    lax.fori_loop(0, n_vis, slc_body, 0)
    start = pl.multiple_of(last * TK, TK)
    absorb(sbuf_ref[n_vis % 2] + heads(jnp.where(start + kio <= tq, 0.0, MASKED)), last)
    acc = acc_ref[...]
    o_slc = acc / acc[SUM_LANE:SUM_LANE + 1, :]

    gates = jax.nn.sigmoid(gt_ref[0]).T
    for r in range(R):
        rs = slice(r * C, (r + 1) * C)
        out = (gates[r:r + 1] * o_cmp[:, rs] + gates[R + r:R + r + 1] * o_slc[:, rs]
               + gates[2 * R + r:2 * R + r + 1] * o_win[:, rs])
        o_ref[0, :, r * LANE:(r + 1) * LANE] = out.T.astype(o_ref.dtype)


def _nsa(slopes_b, pa3, gt3, kc, vct, vt, b, s, col0, gcol0):
    C = NSA_Q
    R = NSA_REP
    G = NSA_KV_HEADS
    nb = s // SLC_BLOCK
    ncmp = s // CMP_STRIDE
    assert nb <= LANE
    qc0 = col0 // R
    kv0 = col0 + NSA_HEADS
    rows = R * C

    ci = np.arange(ncmp)[None, :]
    sj = np.arange(LANE)[:, None]
    ov = ((ci * CMP_STRIDE < (sj + 1) * SLC_BLOCK) & (ci * CMP_STRIDE + CMP_BLOCK > sj * SLC_BLOCK)
          & (ci < ncmp - 1) & (sj < nb))
    ov = jnp.asarray(ov, BF16)
    kpa = _pos_columns(np.arange(s))
    cpa = _pos_columns(np.arange(ncmp) * CMP_STRIDE + (CMP_BLOCK - 1))
    oh = jnp.asarray(np.arange(s)[:, None] // SLC_BLOCK == np.arange(LANE)[None, :], BF16)

    def k_spec(which):
        return pl.BlockSpec((1, s, LANE), lambda bb, g, c: (bb, 0, kv0 + which * G + g))

    const = lambda a: pl.BlockSpec(a.shape, lambda bb, g, c: (0,) * a.ndim)
    return pl.pallas_call(
        functools.partial(_nsa_kernel, seq=s),
        grid=(b, G, s // C),
        in_specs=[
            pl.BlockSpec(memory_space=pltpu.SMEM),
            pl.BlockSpec((1, C, R * LANE), lambda bb, g, c: (bb, c, qc0 + g)),
            pl.BlockSpec((1, ncmp, LANE), lambda bb, g, c: (bb * G + g, 0, 0)),
            pl.BlockSpec((1, LANE, ncmp), lambda bb, g, c: (bb * G + g, 0, 0)),
            k_spec(2), k_spec(4),
            pl.BlockSpec((1, 2, s // LANE, LANE, LANE), lambda bb, g, c: (bb * G + g, 0, 0, 0, 0)),
            pl.BlockSpec((1, C, LANE), lambda bb, g, c: (bb, c, gcol0 + g)),
            const(ov), const(kpa), const(cpa), const(oh),
        ],
        out_specs=pl.BlockSpec((1, C, R * LANE), lambda bb, g, c: (bb, c, g)),
        out_shape=jax.ShapeDtypeStruct((b, s, NSA_HEADS * LANE), BF16),
        scratch_shapes=[
            pltpu.VMEM((s, 2 * LANE), BF16),
            pltpu.SMEM((s // NSA_TK,), F32),
            pltpu.SMEM((s // NSA_TK,), jnp.int32),
            pltpu.VMEM((2, NSA_TK, rows), F32),
            pltpu.VMEM((1, rows), F32),
            pltpu.VMEM((LANE, rows), F32),
        ],
        compiler_params=_params(("parallel", "parallel", "arbitrary")),
    )(slopes_b, pa3, kc, vct, pa3, pa3, vt, gt3, ov, kpa, cpa, oh)


def _merge_kernel(o1, o2, o3, l1, l2, l3, yb, ga, gb, x_ref, wa, wb, wo, gf, wrh, wrl,
                  h_ref, xn_ref, lg_ref, xr_ref):
    def pairs(ref):
        return jnp.concatenate([ref[h] for h in range(ref.shape[0])], axis=1)

    a1, a2, a3 = pairs(l1), pairs(l2), pairs(l3)
    mx = jnp.maximum(jnp.maximum(a1, a2), a3)
    e1, e2, e3 = jnp.exp(a1 - mx), jnp.exp(a2 - mx), jnp.exp(a3 - mx)
    tot = e1 + e2 + e3
    ya = (e1 / tot) * pairs(o1) + (e2 / tot) * pairs(o2) + (e3 / tot) * pairs(o3)
    merged = (jax.nn.sigmoid(ga[...]) * _dot(ya.astype(BF16), wa[...])
              + jax.nn.sigmoid(gb[...]) * _dot(yb[...], wb[...]))
    h = x_ref[...] + _dot(merged.astype(BF16), wo[...])
    h_ref[...] = h
    xn = _rms(h, gf[...])
    xn_ref[...] = xn.astype(BF16)
    tm, d = xn.shape
    half = d // (2 * LANE)

    for c in range(half):
        xr_ref[pl.ds(c, tm, stride=half), :] = _pack_bf16_pairs(
            xn[:, c * LANE:(c + 1) * LANE], xn[:, (c + half) * LANE:(c + half + 1) * LANE])
    xh = xn.astype(BF16)
    xl = (xn - xh.astype(F32)).astype(BF16)
    lg_ref[...] = _dot_nt(wrh[...], xh) + _dot_nt(wrh[...], xl) + _dot_nt(wrl[...], xh)


def _merge(o_list, l_list, yb, gt, x2, wa, wb, wo, gf, wrh, wrl):
    n, d = x2.shape
    aw = wa.shape[0]
    bw = wb.shape[0]
    tm = ROW_TILE
    row = lambda w: pl.BlockSpec((tm, w), lambda i: (i, 0))
    full = lambda a: pl.BlockSpec(a.shape, lambda i: (0,) * a.ndim)
    return pl.pallas_call(
        _merge_kernel,
        grid=(n // tm,),
        in_specs=[pl.BlockSpec((aw // LANE, tm, LANE), lambda i: (0, i, 0))] * 6 + [
            row(bw),
            pl.BlockSpec((tm, d), lambda i: (i, 0)),
            pl.BlockSpec((tm, d), lambda i: (i, 1)),
            row(d), full(wa), full(wb), full(wo), full(gf), full(wrh), full(wrl),
        ],
        out_specs=[row(d), row(d), pl.BlockSpec((N_EXPERTS, tm), lambda i: (0, i)),
                   pl.BlockSpec((tm * d // (2 * LANE), LANE), lambda i: (i, 0))],
        out_shape=[
            jax.ShapeDtypeStruct((n, d), F32),
            jax.ShapeDtypeStruct((n, d), BF16),
            jax.ShapeDtypeStruct((N_EXPERTS, n), F32),
            jax.ShapeDtypeStruct((n * d // (2 * LANE), LANE), jnp.uint32),
        ],
        compiler_params=_params(("parallel",)),
    )(*o_list, *l_list, yb, gt, gt, x2, wa, wb, wo, gf, wrh, wrl)


def _route_kernel(lg_ref, bias_ref, tri_ref, idx_ref, w_ref, rank_ref, cnt_ref, carry_ref):
    E = N_EXPERTS
    tn = lg_ref.shape[1]
    scores = jax.nn.sigmoid(lg_ref[...])
    sel = scores + bias_ref[...]
    sub = lax.broadcasted_iota(jnp.int32, (GROUP_SIZE, tn), 0).astype(F32)
    gs = []
    for grp in range(N_GROUPS):
        blk = sel[grp * GROUP_SIZE:(grp + 1) * GROUP_SIZE]
        m1 = jnp.max(blk, axis=0, keepdims=True)
        first = jnp.min(jnp.where(blk == m1, sub, float(GROUP_SIZE)), axis=0, keepdims=True)
        m2 = jnp.max(jnp.where(sub == first, -jnp.inf, blk), axis=0, keepdims=True)
        gs.append(m1 + m2)
    keep = []
    for a in range(N_GROUPS):
        rank = jnp.zeros((1, tn), jnp.int32)
        for o in range(N_GROUPS):
            if o == a:
                continue
            beats = (gs[o] >= gs[a]) if o < a else (gs[o] > gs[a])
            rank = rank + beats.astype(jnp.int32)
        keep.append(jnp.broadcast_to(rank < TOP_GROUPS, (GROUP_SIZE, tn)))
    emask = jnp.concatenate(keep, axis=0)
    cand = jnp.where(emask, sel, -jnp.inf)
    eid = lax.broadcasted_iota(jnp.int32, (E, tn), 0)
    eid_f = eid.astype(F32)
    rank = jnp.zeros((E, tn), jnp.int32)
    for o in range(E):
        other = cand[o:o + 1]
        beats = (other > cand) | ((other == cand) & (eid > o))
        rank = rank + beats.astype(jnp.int32)
    chosen = emask & (rank < TOP_K)
    wsel = jnp.where(chosen, scores, 0.0)
    wnorm = wsel / jnp.sum(wsel, axis=0, keepdims=True) * ROUTED_SCALE

    @pl.when(pl.program_id(0) == 0)
    def _():
        carry_ref[...] = jnp.zeros_like(carry_ref)

    chosen_f = jnp.where(chosen, 1.0, 0.0)
    before = carry_ref[...] + _dot(chosen_f.astype(BF16), tri_ref[...]) - chosen_f
    carry_ref[...] = carry_ref[...] + jnp.sum(chosen_f, axis=1, keepdims=True)
    cnt_ref[...] = jnp.broadcast_to(carry_ref[...], cnt_ref.shape).astype(jnp.int32)
    for k in range(TOP_K):
        hit = chosen & (rank == k)
        idx_ref[k:k + 1, :] = jnp.sum(jnp.where(hit, eid_f, 0.0), axis=0, keepdims=True).astype(jnp.int32)
        w_ref[k:k + 1, :] = jnp.sum(jnp.where(hit, wnorm, 0.0), axis=0, keepdims=True)
        rank_ref[k:k + 1, :] = jnp.sum(jnp.where(hit, before, 0.0), axis=0, keepdims=True).astype(jnp.int32)


def _route(logits_t, bias_col):
    e, n = logits_t.shape
    tn = ROW_TILE
    tri = jnp.asarray(np.triu(np.ones((tn, tn), np.float32)), BF16)
    slot = pl.BlockSpec((TOP_K, tn), lambda i: (0, i))
    return pl.pallas_call(
        _route_kernel,
        grid=(n // tn,),
        in_specs=[pl.BlockSpec((e, tn), lambda i: (0, i)), pl.BlockSpec((e, 1), lambda i: (0, 0)),
                  pl.BlockSpec((tn, tn), lambda i: (0, 0))],
        out_specs=[slot, slot, slot, pl.BlockSpec((e, LANE), lambda i: (0, 0))],
        out_shape=[
            jax.ShapeDtypeStruct((TOP_K, n), jnp.int32),
            jax.ShapeDtypeStruct((TOP_K, n), F32),
            jax.ShapeDtypeStruct((TOP_K, n), jnp.int32),
            jax.ShapeDtypeStruct((e, LANE), jnp.int32),
        ],
        scratch_shapes=[pltpu.VMEM((e, 1), F32)],
        compiler_params=_params(("arbitrary",)),
    )(logits_t, bias_col, tri)


def _dispatch_sc(pos_t, xr3, p_rows):
    k, n = pos_t.shape
    _, sub, lane = xr3.shape
    workers = SC_CORES * SC_SUBCORES
    windows = n // SC_WINDOW
    per_worker = windows // workers
    assert n % (workers * SC_WINDOW) == 0
    mesh = plsc.VectorSubcoreMesh(core_axis_name="c", subcore_axis_name="s")

    @functools.partial(
        pl.kernel, mesh=mesh,
        out_type=jax.ShapeDtypeStruct((p_rows, sub, lane), xr3.dtype),
        scratch_types=[pltpu.VMEM((k, SC_WINDOW), jnp.int32), pltpu.VMEM((SC_WINDOW, sub, lane), xr3.dtype),
                       pltpu.SemaphoreType.DMA],
    )
    def scatter_rows(x_hbm, idx_hbm, o_hbm, idx_v, rows_v, sem):
        wid = lax.axis_index("s") * SC_CORES + lax.axis_index("c")

        @pl.loop(0, per_worker)
        def _(c):
            win = wid * per_worker + c
            pltpu.sync_copy(idx_hbm.at[win], idx_v)
            pltpu.sync_copy(x_hbm.at[pl.ds(win * SC_WINDOW, SC_WINDOW)], rows_v)
            for slot in range(k):
                pltpu.async_copy(rows_v, o_hbm.at[idx_v.at[slot]], sem).wait()

    return scatter_rows(xr3, pos_t.reshape(k, windows, SC_WINDOW).transpose(1, 0, 2))


def _combine_sc(pos_t, ys3):
    k, n = pos_t.shape
    _, sub, lane = ys3.shape
    nk = k * n
    workers = SC_CORES * SC_SUBCORES
    per_worker = nk // workers
    assert nk % (workers * SC_WINDOW) == 0
    mesh = plsc.VectorSubcoreMesh(core_axis_name="c", subcore_axis_name="s")

    @functools.partial(
        pl.kernel, mesh=mesh,
        out_type=jax.ShapeDtypeStruct((nk, sub, lane), ys3.dtype),
        scratch_types=[pltpu.VMEM((SC_WINDOW,), jnp.int32), pltpu.VMEM((SC_WINDOW, sub, lane), ys3.dtype),
                       pltpu.SemaphoreType.DMA],
    )
    def gather_rows(y_hbm, idx_hbm, o_hbm, idx_v, rows_v, sem):
        wid = lax.axis_index("s") * SC_CORES + lax.axis_index("c")

        @pl.loop(0, per_worker // SC_WINDOW)
        def _(c):
            base = wid * per_worker + c * SC_WINDOW
            pltpu.sync_copy(idx_hbm.at[pl.ds(base, SC_WINDOW)], idx_v)
            pltpu.async_copy(y_hbm.at[idx_v], rows_v, sem).wait()
            pltpu.sync_copy(rows_v, o_hbm.at[pl.ds(base, SC_WINDOW)])

    return gather_rows(ys3, pos_t.reshape(nk))


def _moe_kernel(blk_e_ref, nused_ref, nvalid_ref, xs_ref, wg_ref, wu_ref, wd_ref, o_ref):
    i = pl.program_id(0)

    @pl.when(i < nused_ref[0])
    def _():
        sub = xs_ref.shape[0] // MOE_ROWS
        owned = lax.broadcasted_iota(jnp.int32, (MOE_ROWS, LANE), 0) < nvalid_ref[i]
        words = [jnp.where(owned, xs_ref[pl.ds(c, MOE_ROWS, stride=sub), :], jnp.uint32(0)) for c in range(sub)]
        halves = [_unpack_bf16_pairs(w) for w in words]
        x = jnp.concatenate([lo for lo, _ in halves] + [hi for _, hi in halves], axis=1).astype(BF16)
        hg = _dot(x, wg_ref[0])
        hid = (hg * jax.nn.sigmoid(hg)) * _dot(x, wu_ref[0])
        y = _dot(hid.astype(BF16), wd_ref[0])
        for c in range(sub):
            o_ref[pl.ds(c, MOE_ROWS, stride=sub), :] = _pack_bf16_pairs(
                y[:, c * LANE:(c + 1) * LANE], y[:, (c + sub) * LANE:(c + sub + 1) * LANE])

    @pl.when(i >= nused_ref[0])
    def _():
        o_ref[...] = jnp.zeros_like(o_ref)


def _moe_experts(blk_e, nused, nvalid, xs, wg, wu, wd):
    d = wg.shape[1]
    sub = d // (2 * LANE)
    p = xs.shape[0] // sub
    hid = wg.shape[2]
    grid_spec = pltpu.PrefetchScalarGridSpec(
        num_scalar_prefetch=3,
        grid=(p // MOE_ROWS,),
        in_specs=[
            pl.BlockSpec((MOE_ROWS * sub, LANE), lambda i, be, nu, nv: (jnp.minimum(i, nu[0] - 1), 0)),
            pl.BlockSpec((1, d, hid), lambda i, be, nu, nv: (be[i], 0, 0)),
            pl.BlockSpec((1, d, hid), lambda i, be, nu, nv: (be[i], 0, 0)),
            pl.BlockSpec((1, hid, d), lambda i, be, nu, nv: (be[i], 0, 0)),
        ],
        out_specs=pl.BlockSpec((MOE_ROWS * sub, LANE), lambda i, be, nu, nv: (i, 0)),
    )
    return pl.pallas_call(
        _moe_kernel,
        grid_spec=grid_spec,
        out_shape=jax.ShapeDtypeStruct((p * sub, LANE), jnp.uint32),
        compiler_params=_params(("arbitrary",)),
    )(blk_e, nused, nvalid, xs, wg, wu, wd)


def _side_kernel(xn_ref, p_ref, wsg, wsu, wsd, wp, sh_ref, pe_ref):
    xn = xn_ref[...]
    hg = _dot(xn, wsg[...])
    sh_ref[...] = _dot(((hg * jax.nn.sigmoid(hg)) * _dot(xn, wsu[...])).astype(BF16), wsd[...]).astype(sh_ref.dtype)
    pe_ref[...] = _dot(p_ref[...].astype(BF16), wp[...]).astype(pe_ref.dtype)


def _side(xn2, p2, wsg, wsu, wsd, wp):
    n, d = xn2.shape
    tm = ROW_TILE
    row = lambda w: pl.BlockSpec((tm, w), lambda i: (i, 0))
    full = lambda a: pl.BlockSpec(a.shape, lambda i: (0,) * a.ndim)
    return pl.pallas_call(
        _side_kernel,
        grid=(n // tm,),
        in_specs=[row(d), row(p2.shape[1]), full(wsg), full(wsu), full(wsd), full(wp)],
        out_specs=[row(d), row(d)],
        out_shape=[jax.ShapeDtypeStruct((n, d), BF16)] * 2,
        compiler_params=_params(("parallel",)),
    )(xn2, p2, wsg, wsu, wsd, wp)


def _final_kernel(h_ref, sh_ref, pe_ref, ys_ref, w_ref, gp, wpg, gfin, o_ref):
    w = w_ref[...]
    tm = w.shape[0]
    sub = ys_ref.shape[1] // tm
    parts = [None] * (2 * sub)
    for k in range(TOP_K):
        wk = jnp.broadcast_to(w[:, k:k + 1], (tm, LANE))
        for c in range(sub):
            lo, hi = _unpack_bf16_pairs(ys_ref[k, pl.ds(c, tm, stride=sub), :])
            parts[c] = wk * lo if k == 0 else parts[c] + wk * lo
            parts[c + sub] = wk * hi if k == 0 else parts[c + sub] + wk * hi
    y = jnp.concatenate(parts, axis=1)
    h = h_ref[...] + (y + sh_ref[...].astype(F32))
    gate = jax.nn.sigmoid(_dot(_rms(h, gp[...]).astype(BF16), wpg[...]))
    h = h + pe_ref[...].astype(F32) * gate
    o_ref[...] = _rms(h, gfin[...])


def _final(h1, shared, ple, ysg, w_nk, gp, wpg, gfin, part, out_prev):
    n, d = h1.shape
    tm = ROW_TILE
    steps = ysg.shape[1] * LANE * 2 // d // tm
    first = part * steps
    row = lambda w: pl.BlockSpec((tm, w), lambda i: (first + i, 0))
    full = lambda a: pl.BlockSpec(a.shape, lambda i: (0,) * a.ndim)
    in_specs = [row(d), row(d), row(d), pl.BlockSpec((TOP_K, ysg.shape[1] // steps, LANE), lambda i: (0, i, 0)),
                row(TOP_K), full(gp), full(wpg), full(gfin)]
    args = [h1, shared, ple, ysg, w_nk, gp, wpg, gfin]
    kernel_fn, aliases = _final_kernel, {}
    if out_prev is not None:
        in_specs.append(pl.BlockSpec(memory_space=pl.ANY))
        args.append(out_prev)
        aliases = {len(args) - 1: 0}
        kernel_fn = lambda *refs: _final_kernel(*refs[:8], refs[9])
    return pl.pallas_call(
        kernel_fn,
        grid=(steps,),
        in_specs=in_specs,
        out_specs=row(d),
        out_shape=jax.ShapeDtypeStruct((n, d), F32),
        input_output_aliases=aliases,
        compiler_params=_params(("parallel",)),
    )(*args)


def _pad_heads(w):
    d, c = w.shape
    nh = c // HEAD_DIM
    w = w.reshape(d, nh, HEAD_DIM)
    return jnp.pad(w, ((0, 0), (0, 0), (0, LANE - HEAD_DIM))).reshape(d, nh * LANE)


def _alibi_slopes(n):
    return 2.0 ** (-8.0 * jnp.arange(1, n + 1, dtype=jnp.float32) / n)


def kernel(x, p, g_mix, w_in, cmp_pos_k, cmp_w1_k, cmp_b1_k, cmp_w2_k, cmp_pos_v, cmp_w1_v, cmp_b1_v, cmp_w2_v, w_branch_a, w_branch_b, w_out, g_ffn, w_router, router_bias, w_gate, w_up, w_down, ws_gate, ws_up, ws_down, g_ple, w_ple_gate, w_ple, g_final):
    B, S, D = x.shape
    N = B * S
    G, R = NSA_KV_HEADS, NSA_REP
    assert w_in.shape[0] == 1 and p.shape[0] == 1
    assert S % DIL_STEP_ROWS == 0 and S % NSA_TK == 0 and S >= WIN_SIZE + NSA_Q
    assert N % (ROW_TILE * COMBINE_PARTS) == 0 and N % INPROJ_ROWS == 0 and S % INPROJ_ROWS == 0
    slopes = _alibi_slopes(DIL_HEADS + NSA_HEADS)
    slopes_a, slopes_b = slopes[0::2], slopes[1::2]

    offs = np.concatenate([[0], np.cumsum(IN_SIZES)])
    parts = [w_in[0][:, offs[i]:offs[i + 1]] for i in range(len(IN_SIZES))]
    w_qa, w_ka, w_va, w_qb, w_kc, w_vc, w_ks, w_vs, w_kw, w_vw, w_gn, w_ga, w_gb = parts
    w_qkv_a = jnp.concatenate([w_qa * HEAD_DIM ** -0.5, w_ka, w_va], axis=1).astype(BF16)
    w_nsa = jnp.concatenate([_pad_heads(t) for t in (w_qb, w_kc, w_vc, w_ks, w_vs, w_kw, w_vw)],
                            axis=1).astype(BF16)
    nsa_w = w_nsa.shape[1]
    w_gn = w_gn.reshape(D, 3, G, R).transpose(0, 2, 1, 3).reshape(D, G, 3 * R)
    w_gn = jnp.pad(w_gn, ((0, 0), (0, 0), (0, LANE - 3 * R))).reshape(D, G * LANE)
    w_gate_in = jnp.concatenate([w_ga, w_gb, w_gn], axis=1).astype(BF16)
    gate_w = w_gate_in.shape[1]

    x2 = x.reshape(N, D)
    dils = tuple(dil for _, dil in DIL_PATTERNS if dil > 1)
    qkv_a, *dilated_qkv = _inproj(x2, g_mix, w_qkv_a, BF16, w_qkv_a.shape[1], dils=dils)
    nsa_in, = _inproj(x2, g_mix, w_nsa, BF16, nsa_w)
    gt, = _inproj(x2, g_mix, w_gate_in, F32, gate_w)

    o_list, l_list = [], []
    for _, dil in DIL_PATTERNS:
        src = qkv_a if dil == 1 else dilated_qkv[dils.index(dil)]
        o, l = _dilated(src.reshape(B, S // dil, -1), _dil_bias(slopes_a, dil), dil, B, S)
        o_list.append(o.reshape(-1, N, LANE))
        l_list.append(l.reshape(-1, N, LANE))

    nchunk = S // CMP_STRIDE
    c0 = NSA_HEADS * LANE
    raw = nsa_in[:, c0:c0 + 2 * G * LANE].reshape(B, nchunk, CMP_STRIDE, 2 * G, LANE)[..., :HEAD_DIM]
    xc = raw.transpose(0, 3, 1, 2, 4).reshape(B, 2 * G, nchunk, CMP_STRIDE * HEAD_DIM)
    pos = jnp.stack([cmp_pos_k[0].reshape(1, -1), cmp_pos_v[0].reshape(1, -1)])
    w1 = jnp.stack([cmp_w1_k[0], cmp_w1_v[0]]).astype(BF16)
    b1 = jnp.stack([cmp_b1_k[0][None], cmp_b1_v[0][None]])
    w2 = jnp.pad(jnp.stack([cmp_w2_k[0], cmp_w2_v[0]]), ((0, 0), (0, 0), (0, LANE - HEAD_DIM))).astype(BF16)
    kcv = _compress(xc, pos, w1, b1, w2)
    kcv = kcv.reshape(B, 2, G, nchunk, LANE).transpose(1, 0, 2, 3, 4).reshape(2, B * G, nchunk, LANE)
    vct = kcv[1].transpose(0, 2, 1)
    v0 = (NSA_HEADS + 3 * G) * LANE
    vsw = jnp.stack([nsa_in[:, v0:v0 + G * LANE], nsa_in[:, v0 + 2 * G * LANE:v0 + 3 * G * LANE]])
    vsw = vsw.reshape(2, B, S // LANE, LANE, G, LANE).transpose(1, 4, 0, 2, 5, 3)
    ones_row = (jnp.arange(LANE) == SUM_LANE).astype(BF16)[:, None]
    vt = (vsw + ones_row).reshape(B * G, 2, S // LANE, LANE, LANE)
    yb = _nsa(slopes_b, nsa_in.reshape(B, S, nsa_w), gt.reshape(B, S, gate_w), kcv[0], vct, vt,
              B, S, 0, 2 * D // LANE)
    yb = yb.reshape(N, NSA_HEADS * LANE)

    wb_pad = jnp.pad(w_branch_b[0].reshape(NSA_HEADS, HEAD_DIM, D),
                     ((0, 0), (0, LANE - HEAD_DIM), (0, 0))).reshape(NSA_HEADS * LANE, D)
    wr_t = w_router[0].T
    wr_hi = wr_t.astype(BF16)
    wr_lo = (wr_t - wr_hi.astype(F32)).astype(BF16)
    h1, xn2, logits_t, xr = _merge(o_list, l_list, yb, gt, x2, w_branch_a[0].astype(BF16), wb_pad.astype(BF16),
                               w_out[0].astype(BF16), g_ffn, wr_hi, wr_lo)

    shared, ple = _side(xn2, p[0].reshape(N, -1), ws_gate[0].astype(BF16), ws_up[0].astype(BF16),
                        ws_down[0].astype(BF16), w_ple[0].astype(BF16))

    idx_t, wsel_t, rank_t, counts = _route(logits_t, router_bias[0][:, None])
    counts = counts[:, 0]
    padded = (counts + MOE_ROWS - 1) // MOE_ROWS * MOE_ROWS
    pend = jnp.cumsum(padded)
    pstart = pend - padded
    hot = idx_t[None] == jnp.arange(N_EXPERTS, dtype=jnp.int32)[:, None, None]
    pos_t = rank_t + jnp.sum(jnp.where(hot, pstart.astype(jnp.int32)[:, None, None], 0), axis=0)
    NK = N * TOP_K
    nblk = (NK + N_EXPERTS * (MOE_ROWS - 1) + MOE_ROWS - 1) // MOE_ROWS
    P = nblk * MOE_ROWS
    blk_e = jnp.minimum(jnp.sum(pend[None, :] <= (jnp.arange(nblk, dtype=jnp.int32) * MOE_ROWS)[:, None], axis=1),
                        N_EXPERTS - 1).astype(jnp.int32)
    nused = (pend[-1] // MOE_ROWS).astype(jnp.int32).reshape(1)
    row0 = jnp.arange(nblk, dtype=jnp.int32) * MOE_ROWS
    nvalid = jnp.clip((pstart + counts)[blk_e] - row0, 0, MOE_ROWS).astype(jnp.int32)
    sub = D // (2 * LANE)
    xs = _dispatch_sc(pos_t, xr.reshape(N, sub, LANE), P).reshape(P * sub, LANE)
    ys = _moe_experts(blk_e, nused, nvalid, xs, w_gate[0].astype(BF16), w_up[0].astype(BF16), w_down[0].astype(BF16))
    n_part = N // COMBINE_PARTS
    ys3, w_nk, wpg, out = ys.reshape(P, sub, LANE), wsel_t.T, w_ple_gate[0].astype(BF16), None
    for part in range(COMBINE_PARTS):
        ysg = _combine_sc(pos_t[:, part * n_part:(part + 1) * n_part], ys3).reshape(TOP_K, n_part * sub, LANE)
        out = _final(h1, shared, ple, ysg, w_nk, g_ple, wpg, g_final[None, :], part, out)
    return out.reshape(B, S, D)
```

```python
import functools

import numpy as np
import jax
import jax.numpy as jnp
from jax import lax
from jax.experimental import pallas as pl
from jax.experimental.pallas import tpu as pltpu
from jax.experimental.pallas import tpu_sc as plsc

F32 = jnp.float32
BF16 = jnp.bfloat16

HEAD_DIM = 64
LANE = 128
RMS_EPS = 1e-6
DIL_HEADS = 8
DIL_PATTERNS = ((128, 1), (512, 4), (2048, 16))
DIL_BLOCK = 128
DIL_STEP_ROWS = 4096
NSA_HEADS = 8
NSA_KV_HEADS = 2
NSA_REP = NSA_HEADS // NSA_KV_HEADS
CMP_BLOCK = 32
CMP_STRIDE = 16
CMP_HIDDEN = 256
SLC_BLOCK = 64
SLC_TOP_N = 16
SLC_FORCE = 1e9
WIN_SIZE = 512
N_EXPERTS = 64
TOP_K = 8
N_GROUPS = 8
TOP_GROUPS = 4
GROUP_SIZE = N_EXPERTS // N_GROUPS
ROUTED_SCALE = 2.5
IN_SIZES = (512, 512, 512, 512, 128, 128, 128, 128, 128, 128, 24, 1024, 1024)

NEG = -1e30
NT_DIMS = (((1,), (1,)), ((), ()))

ROW_TILE = 512
INPROJ_ROWS = 1024
NSA_Q = 256
NSA_TK = 512
MOE_ROWS = 2048
VMEM_LIMIT = 48 * 1024 * 1024
SC_CORES = 2
SC_SUBCORES = 16
SC_WINDOW = 128
COMBINE_PARTS = 4


def _dot(a, b):
    return jnp.dot(a, b, preferred_element_type=F32)


def _dot_nt(a, b):
    return lax.dot_general(a, b, NT_DIMS, preferred_element_type=F32)


def _split3(x):
    hi = x.astype(BF16)
    r1 = x - hi.astype(F32)
    mid = r1.astype(BF16)
    lo = (r1 - mid.astype(F32)).astype(BF16)
    return hi, mid, lo


def _pack_bf16_pairs(lo, hi):
    lo_bits = pltpu.bitcast(lo.astype(BF16).astype(F32), jnp.uint32)
    hi_bits = pltpu.bitcast(hi.astype(BF16).astype(F32), jnp.uint32)
    return hi_bits | (lo_bits >> 16)


def _unpack_bf16_pairs(words):
    return (pltpu.bitcast(words << 16, F32), pltpu.bitcast(words & jnp.uint32(0xFFFF0000), F32))


def _rms(x, g):
    r = lax.rsqrt(jnp.mean(x * x, axis=-1, keepdims=True) + RMS_EPS)
    return (x * r) * g


def _params(sem):
    return pltpu.CompilerParams(dimension_semantics=sem, vmem_limit_bytes=VMEM_LIMIT)


def _inproj_kernel(x_ref, g_ref, w_ref, o_ref, *rest, dils):
    dil_refs, xn_ref = rest[:len(dils)], rest[len(dils)]
    res_ref = rest[len(dils) + 1] if dils else None

    @pl.when(pl.program_id(1) == 0)
    def _():
        xn_ref[...] = _rms(x_ref[...], g_ref[...]).astype(BF16)

    res = _dot(xn_ref[...], w_ref[...])
    o_ref[...] = res.astype(o_ref.dtype)

    if dils:
        tn = res.shape[1]
        for c in range(tn // LANE):
            res_ref[c] = res[:, c * LANE:(c + 1) * LANE]
        for d_ref, dil in zip(dil_refs, dils):
            for r in range(dil):
                for c in range(tn // LANE):
                    piece = res_ref[c, pl.ds(r, INPROJ_ROWS // dil, stride=dil), :]
                    d_ref[:, r * tn + c * LANE:r * tn + (c + 1) * LANE] = piece.astype(d_ref.dtype)


def _inproj(x2, g, w, out_dtype, tn, dils=()):
    n, d = x2.shape
    cols = w.shape[1]
    out_specs = [pl.BlockSpec((INPROJ_ROWS, tn), lambda i, j: (i, j))]
    out_shape = [jax.ShapeDtypeStruct((n, cols), out_dtype)]
    for dil in dils:
        out_specs.append(pl.BlockSpec((INPROJ_ROWS // dil, dil * tn), lambda i, j: (i, j)))
        out_shape.append(jax.ShapeDtypeStruct((n // dil, cols * dil), out_dtype))
    return pl.pallas_call(
        functools.partial(_inproj_kernel, dils=dils),
        grid=(n // INPROJ_ROWS, cols // tn),
        in_specs=[
            pl.BlockSpec((INPROJ_ROWS, d), lambda i, j: (i, 0)),
            pl.BlockSpec((1, d), lambda i, j: (0, 0)),
            pl.BlockSpec((d, tn), lambda i, j: (0, j)),
        ],
        out_specs=out_specs,
        out_shape=out_shape,
        scratch_shapes=[pltpu.VMEM((INPROJ_ROWS, d), BF16)]
        + ([pltpu.VMEM((tn // LANE, INPROJ_ROWS, LANE), F32)] if dils else []),
        compiler_params=_params(("parallel", "arbitrary")),
    )(x2, g, w)


def _dil_kernel(q_ref, kp_ref, kc_ref, vp_ref, vc_ref, bias_ref, o_ref, l_ref, *, dil, nq):
    i = pl.program_id(1)
    res = pl.program_id(2)
    blk = DIL_BLOCK
    lane = lax.broadcasted_iota(jnp.int32, (blk, LANE), 1)
    lo = lane < HEAD_DIM
    for sub in range(nq):
        rows = slice(sub * blk, (sub + 1) * blk)
        prev = slice((sub - 1) * blk, sub * blk)
        out_rows = pl.ds(sub * blk * dil + res, blk, stride=dil) if dil > 1 else rows
        variant = jnp.minimum(i, 1) * DIL_HEADS if sub == 0 else DIL_HEADS
        for hp in range(DIL_HEADS // 2):
            sl = slice(hp * LANE, (hp + 1) * LANE)
            q2 = q_ref[0, rows, sl]
            k_prev = kp_ref[0, :, sl] if sub == 0 else kc_ref[0, prev, sl]
            v_prev = vp_ref[0, :, sl] if sub == 0 else vc_ref[0, prev, sl]
            k2 = jnp.concatenate([k_prev, kc_ref[0, rows, sl]], axis=0)
            v2 = jnp.concatenate([v_prev, vc_ref[0, rows, sl]], axis=0)
            outs, lses = [], []
            for hh in range(2):
                keep = lo if hh == 0 else jnp.logical_not(lo)
                qm = jnp.where(keep, q2, jnp.zeros_like(q2))
                s = _dot_nt(qm, k2) + bias_ref[variant + 2 * hp + hh]
                m = jnp.max(s, axis=-1, keepdims=True)
                e = jnp.exp(s - m)
                den = jnp.sum(e, axis=-1, keepdims=True)
                outs.append(_dot(e.astype(BF16), v2) / den)
                lses.append(jnp.broadcast_to(m + jnp.log(den), (blk, LANE)))
            o_ref[hp, 0, out_rows, :] = jnp.where(lo, outs[0], outs[1])
            l_ref[hp, 0, out_rows, :] = jnp.where(lo, lses[0], lses[1])


def _dil_bias(slopes_a, dil):
    blk = DIL_BLOCK
    qi = jnp.arange(blk)[:, None]
    kj = jnp.arange(2 * blk)[None, :]
    dist = (qi + blk - kj).astype(F32)
    band = (dist >= 0) & (dist <= blk)
    slope = (slopes_a * dil)[:, None, None]
    bias = -slope * dist
    first = jnp.where(band & (kj >= blk), bias, NEG)
    rest = jnp.where(band, bias, NEG)
    return jnp.concatenate([first, rest], axis=0)


def _dilated(src3, bias, dil, b, s):
    nq = min(8, DIL_STEP_ROWS // (DIL_BLOCK * dil))
    nstep = s // dil // (DIL_BLOCK * nq)
    aw = DIL_HEADS * HEAD_DIM
    npair = DIL_HEADS // 2

    def spec(which, prev):
        if prev:
            return pl.BlockSpec((1, DIL_BLOCK, aw), lambda bb, i, r: (bb, jnp.maximum(nq * i - 1, 0), 3 * r + which))
        return pl.BlockSpec((1, DIL_BLOCK * nq, aw), lambda bb, i, r: (bb, i, 3 * r + which))

    out_spec = pl.BlockSpec((npair, 1, DIL_BLOCK * nq * dil, LANE), lambda bb, i, r: (0, bb, i, 0))
    return pl.pallas_call(
        functools.partial(_dil_kernel, dil=dil, nq=nq),
        grid=(b, nstep, dil),
        in_specs=[
            spec(0, False), spec(1, True), spec(1, False), spec(2, True), spec(2, False),
            pl.BlockSpec(bias.shape, lambda bb, i, r: (0, 0, 0)),
        ],
        out_specs=[out_spec, out_spec],
        out_shape=[jax.ShapeDtypeStruct((npair, b, s, LANE), F32)] * 2,
        compiler_params=_params(("parallel", "arbitrary", "arbitrary")),
    )(src3, src3, src3, src3, src3, bias)


def _cmp_kernel(x_ref, pos_ref, w1_ref, b1_ref, w2_ref, o_ref):
    half = x_ref.shape[-1]
    x = x_ref[0, 0]
    w1 = w1_ref[0]
    first = _dot(x, w1[:half])
    second = _dot(x, w1[half:])
    nrow = x.shape[0]
    pre = first + pltpu.roll(second, nrow - 1, 0)
    posb = _dot(jnp.broadcast_to(pos_ref[0], (8, 2 * half)).astype(BF16), w1)[0:1]
    pre = pre + posb + b1_ref[0]
    o_ref[0, 0] = _dot(jax.nn.gelu(pre).astype(BF16), w2_ref[0]).astype(o_ref.dtype)


def _compress(xc, pos, w1, b1, w2):
    b, four, nchunk, width = xc.shape
    return pl.pallas_call(
        _cmp_kernel,
        grid=(b, four),
        in_specs=[
            pl.BlockSpec((1, 1, nchunk, width), lambda bb, j: (bb, j, 0, 0)),
            pl.BlockSpec((1, 1, 2 * width), lambda bb, j: (j // 2, 0, 0)),
            pl.BlockSpec((1, 2 * width, CMP_HIDDEN), lambda bb, j: (j // 2, 0, 0)),
            pl.BlockSpec((1, 1, CMP_HIDDEN), lambda bb, j: (j // 2, 0, 0)),
            pl.BlockSpec((1, CMP_HIDDEN, LANE), lambda bb, j: (j // 2, 0, 0)),
        ],
        out_specs=pl.BlockSpec((1, 1, nchunk, LANE), lambda bb, j: (bb, j, 0, 0)),
        out_shape=jax.ShapeDtypeStruct((b, four, nchunk, LANE), BF16),
        compiler_params=_params(("parallel", "parallel")),
    )(xc, pos, w1, b1, w2)


POS_LANE = HEAD_DIM
POS_SPLIT = 64
SUM_LANE = HEAD_DIM
MASKED = -1e30
ROW_FLOOR = -1e29


def _pos_columns(pos):
    out = np.zeros((pos.shape[0], LANE), np.float32)
    out[:, POS_LANE] = pos // POS_SPLIT
    out[:, POS_LANE + 1] = pos % POS_SPLIT
    out[:, POS_LANE + 2] = 1.0
    out[:, POS_LANE + 3] = 1.0
    return jnp.asarray(out, BF16)


def _nsa_kernel(slope_ref, q_ref, kc_ref, vct_ref, ks_ref, kw_ref, vt_ref, gt_ref, ov_ref, kpa_ref, cpa_ref,
                oh_ref, o_ref, kaug_ref, flag_ref, list_ref, sbuf_ref, m_ref, acc_ref, *, seq):
    g = pl.program_id(1)
    c = pl.program_id(2)
    C = NSA_Q
    R = NSA_REP
    TK = NSA_TK
    rows = R * C
    t0 = c * C
    scale = HEAD_DIM ** -0.5
    nb = seq // SLC_BLOCK
    ncmp = seq // CMP_STRIDE
    n_sel = min(SLC_TOP_N, nb)
    vblocks = TK // LANE

    @pl.when(c == 0)
    def _():
        kaug_ref[:, 0:LANE] = ks_ref[0] + kpa_ref[...]
        kaug_ref[:, LANE:2 * LANE] = oh_ref[...]

    row = lax.broadcasted_iota(jnp.int32, (rows, 1), 0)
    ti = t0 + row % C
    slope = jnp.zeros((rows, 1), F32)
    for r in range(R):
        slope = jnp.where(row // C == r, slope_ref[g * R + r], slope)
    lane = lax.broadcasted_iota(jnp.int32, (rows, LANE), 1)
    thi = (ti // POS_SPLIT).astype(F32)
    tlo = (ti % POS_SPLIT).astype(F32)
    qcols = jnp.where(lane == POS_LANE, POS_SPLIT * slope,
                      jnp.where(lane == POS_LANE + 1, slope,
                                jnp.where(lane == POS_LANE + 2, -POS_SPLIT * slope * thi,
                                          jnp.where(lane == POS_LANE + 3, -slope * tlo, 0.0))))
    q4 = jnp.concatenate([q_ref[0, :, r * LANE:(r + 1) * LANE] for r in range(R)], axis=0)
    q4 = (q4.astype(F32) * scale + qcols).astype(BF16)
    tq = t0 + lax.broadcasted_iota(jnp.int32, (1, C), 1)

    def heads(x):
        return jnp.concatenate([x] * R, axis=1)

    kc = kc_ref[0] + cpa_ref[...]
    ci = lax.broadcasted_iota(jnp.int32, (ncmp, 1), 0)
    vis = (ci * CMP_STRIDE + (CMP_BLOCK - 1) <= tq) & (ci < ncmp - 1)
    s = _dot_nt(kc, q4) + heads(jnp.where(vis, 0.0, MASKED))
    m = jnp.maximum(jnp.max(s, axis=0, keepdims=True), ROW_FLOOR)
    e = jnp.exp(s - m)
    den = jnp.sum(e, axis=0, keepdims=True)
    pc = e * (1.0 / jnp.where(den > 0, den, 1.0))
    o_cmp = _dot(vct_ref[0], pc.astype(BF16))
    pcsum = pc[:, 0:C]
    for r in range(1, R):
        pcsum = pcsum + pc[:, r * C:(r + 1) * C]
    ov = ov_ref[...]
    p_slc = sum(_dot(ov, part) for part in _split3(pcsum))

    def values_t(which, first_block, n):
        return jnp.concatenate([vt_ref[0, which, first_block + i] for i in range(n)], axis=1)

    span = WIN_SIZE + C
    ws = pl.multiple_of(jnp.maximum(t0 - WIN_SIZE, 0), C)
    k_w = kw_ref[0, pl.ds(ws, span), :] + kpa_ref[pl.ds(ws, span), :]
    wpos = ws + lax.broadcasted_iota(jnp.int32, (span, 1), 0)
    wbias = jnp.where((wpos <= tq) & (wpos >= tq - (WIN_SIZE - 1)), 0.0, MASKED)
    s3 = _dot_nt(k_w, q4) + heads(wbias)
    e3 = jnp.exp(s3 - jnp.max(s3, axis=0, keepdims=True))
    ow = _dot(values_t(1, ws // LANE, span // LANE), e3.astype(BF16))
    o_win = ow / ow[SUM_LANE:SUM_LANE + 1, :]

    j = lax.broadcasted_iota(jnp.int32, (LANE, C), 0)
    cur = tq // SLC_BLOCK
    forced = (j == 0) | (j == cur) | (j == cur - 1)
    future = j * SLC_BLOCK > tq
    work = jnp.where(forced, -jnp.inf, jnp.where(future, -SLC_FORCE, p_slc))
    work = jnp.where(j < nb, work, -jnp.inf)
    sel = jnp.where(forced, 1.0, 0.0)
    jf = j.astype(F32)
    for _ in range(n_sel - 3):
        mx = jnp.max(work, axis=0, keepdims=True)
        first = jnp.min(jnp.where(work == mx, jf, float(LANE)), axis=0, keepdims=True)
        pick = jf == first
        sel = jnp.where(pick, 1.0, sel)
        work = jnp.where(pick, -jnp.inf, work)
    sel = jnp.where(future, 0.0, sel)
    blocks_per_tile = TK // SLC_BLOCK
    for kt in range(seq // TK):
        flag_ref[kt] = jnp.max(sel[kt * blocks_per_tile:(kt + 1) * blocks_per_tile, :])
    sel_bias = jnp.where(sel.T > 0.5, 0.0, MASKED).astype(BF16)
    q_aug = jnp.concatenate([q4, jnp.concatenate([sel_bias] * R, axis=0)], axis=1)

    m_ref[...] = jnp.full((1, rows), ROW_FLOOR, F32)
    acc_ref[...] = jnp.zeros((LANE, rows), F32)
    kio = lax.broadcasted_iota(jnp.int32, (TK, 1), 0)

    def scores(kt):
        start = pl.multiple_of(kt * TK, TK)
        return _dot_nt(kaug_ref[pl.ds(start, TK), :], q_aug)

    def absorb(s2, kt):
        m_i = m_ref[...]
        m_n = jnp.maximum(m_i, jnp.max(s2, axis=0, keepdims=True))
        alpha = jnp.exp(m_i - m_n)
        p2 = jnp.exp(s2 - m_n)
        acc_ref[...] = alpha * acc_ref[...] + _dot(values_t(0, kt * vblocks, vblocks), p2.astype(BF16))
        m_ref[...] = m_n

    last = (t0 + C - 1) // TK

    def compact(kt, n):
        list_ref[n] = kt
        return n + (flag_ref[kt] > 0.0).astype(jnp.int32)

    n_vis = lax.fori_loop(0, last, compact, 0)
    list_ref[n_vis] = last

    sbuf_ref[0] = scores(list_ref[0])

    def slc_body(j, carry):
        s_next = scores(list_ref[j + 1])
        absorb(sbuf_ref[j % 2], list_ref[j])
        sbuf_ref[(j + 1) % 2] = s_next
        return carry

    lax.fori_loop(0, n_vis, slc_body, 0)
    start = pl.multiple_of(last * TK, TK)
    absorb(sbuf_ref[n_vis % 2] + heads(jnp.where(start + kio <= tq, 0.0, MASKED)), last)
    acc = acc_ref[...]
    o_slc = acc / acc[SUM_LANE:SUM_LANE + 1, :]

    gates = jax.nn.sigmoid(gt_ref[0]).T
    for r in range(R):
        rs = slice(r * C, (r + 1) * C)
        out = (gates[r:r + 1] * o_cmp[:, rs] + gates[R + r:R + r + 1] * o_slc[:, rs]
               + gates[2 * R + r:2 * R + r + 1] * o_win[:, rs])
        o_ref[0, :, r * LANE:(r + 1) * LANE] = out.T.astype(o_ref.dtype)


def _nsa(slopes_b, pa3, gt3, kc, vct, vt, b, s, col0, gcol0):
    C = NSA_Q
    R = NSA_REP
    G = NSA_KV_HEADS
    nb = s // SLC_BLOCK
    ncmp = s // CMP_STRIDE
    assert nb <= LANE
    qc0 = col0 // R
    kv0 = col0 + NSA_HEADS
    rows = R * C

    ci = np.arange(ncmp)[None, :]
    sj = np.arange(LANE)[:, None]
    ov = ((ci * CMP_STRIDE < (sj + 1) * SLC_BLOCK) & (ci * CMP_STRIDE + CMP_BLOCK > sj * SLC_BLOCK)
          & (ci < ncmp - 1) & (sj < nb))
    ov = jnp.asarray(ov, BF16)
    kpa = _pos_columns(np.arange(s))
    cpa = _pos_columns(np.arange(ncmp) * CMP_STRIDE + (CMP_BLOCK - 1))
    oh = jnp.asarray(np.arange(s)[:, None] // SLC_BLOCK == np.arange(LANE)[None, :], BF16)

    def k_spec(which):
        return pl.BlockSpec((1, s, LANE), lambda bb, g, c: (bb, 0, kv0 + which * G + g))

    const = lambda a: pl.BlockSpec(a.shape, lambda bb, g, c: (0,) * a.ndim)
    return pl.pallas_call(
        functools.partial(_nsa_kernel, seq=s),
        grid=(b, G, s // C),
        in_specs=[
            pl.BlockSpec(memory_space=pltpu.SMEM),
            pl.BlockSpec((1, C, R * LANE), lambda bb, g, c: (bb, c, qc0 + g)),
            pl.BlockSpec((1, ncmp, LANE), lambda bb, g, c: (bb * G + g, 0, 0)),
            pl.BlockSpec((1, LANE, ncmp), lambda bb, g, c: (bb * G + g, 0, 0)),
            k_spec(2), k_spec(4),
            pl.BlockSpec((1, 2, s // LANE, LANE, LANE), lambda bb, g, c: (bb * G + g, 0, 0, 0, 0)),
            pl.BlockSpec((1, C, LANE), lambda bb, g, c: (bb, c, gcol0 + g)),
            const(ov), const(kpa), const(cpa), const(oh),
        ],
        out_specs=pl.BlockSpec((1, C, R * LANE), lambda bb, g, c: (bb, c, g)),
        out_shape=jax.ShapeDtypeStruct((b, s, NSA_HEADS * LANE), BF16),
        scratch_shapes=[
            pltpu.VMEM((s, 2 * LANE), BF16),
            pltpu.SMEM((s // NSA_TK,), F32),
            pltpu.SMEM((s // NSA_TK,), jnp.int32),
            pltpu.VMEM((2, NSA_TK, rows), F32),
            pltpu.VMEM((1, rows), F32),
            pltpu.VMEM((LANE, rows), F32),
        ],
        compiler_params=_params(("parallel", "parallel", "arbitrary")),
    )(slopes_b, pa3, kc, vct, pa3, pa3, vt, gt3, ov, kpa, cpa, oh)


def _merge_kernel(o1, o2, o3, l1, l2, l3, yb, ga, gb, x_ref, wa, wb, wo, gf, wrh, wrl,
                  bias_ref, tri_ref, h_ref, xn_ref, xr_ref, idx_ref, w_ref, rank_ref, cnt_ref, carry_ref):
    def pairs(ref):
        return jnp.concatenate([ref[h] for h in range(ref.shape[0])], axis=1)

    a1, a2, a3 = pairs(l1), pairs(l2), pairs(l3)
    mx = jnp.maximum(jnp.maximum(a1, a2), a3)
    e1, e2, e3 = jnp.exp(a1 - mx), jnp.exp(a2 - mx), jnp.exp(a3 - mx)
    tot = e1 + e2 + e3
    ya = (e1 / tot) * pairs(o1) + (e2 / tot) * pairs(o2) + (e3 / tot) * pairs(o3)
    merged = (jax.nn.sigmoid(ga[...]) * _dot(ya.astype(BF16), wa[...])
              + jax.nn.sigmoid(gb[...]) * _dot(yb[...], wb[...]))
    h = x_ref[...] + _dot(merged.astype(BF16), wo[...])
    h_ref[...] = h
    xn = _rms(h, gf[...])
    xn_ref[...] = xn.astype(BF16)
    tm, d = xn.shape
    half = d // (2 * LANE)

    for c in range(half):
        xr_ref[pl.ds(c, tm, stride=half), :] = _pack_bf16_pairs(
            xn[:, c * LANE:(c + 1) * LANE], xn[:, (c + half) * LANE:(c + half + 1) * LANE])
    xh = xn.astype(BF16)
    xl = (xn - xh.astype(F32)).astype(BF16)
    logits = _dot_nt(wrh[...], xh) + _dot_nt(wrh[...], xl) + _dot_nt(wrl[...], xh)
    _route_tile(logits, bias_ref, tri_ref, idx_ref, w_ref, rank_ref, cnt_ref, carry_ref)


def _merge(o_list, l_list, yb, gt, x2, wa, wb, wo, gf, wrh, wrl, bias_col):
    n, d = x2.shape
    aw = wa.shape[0]
    bw = wb.shape[0]
    tm = ROW_TILE
    row = lambda w: pl.BlockSpec((tm, w), lambda i: (i, 0))
    full = lambda a: pl.BlockSpec(a.shape, lambda i: (0,) * a.ndim)
    tri = jnp.asarray(np.triu(np.ones((tm, tm), np.float32)), BF16)
    slot = pl.BlockSpec((TOP_K, tm), lambda i: (0, i))
    return pl.pallas_call(
        _merge_kernel,
        grid=(n // tm,),
        in_specs=[pl.BlockSpec((aw // LANE, tm, LANE), lambda i: (0, i, 0))] * 6 + [
            row(bw),
            pl.BlockSpec((tm, d), lambda i: (i, 0)),
            pl.BlockSpec((tm, d), lambda i: (i, 1)),
            row(d), full(wa), full(wb), full(wo), full(gf), full(wrh), full(wrl), full(bias_col), full(tri),
        ],
        out_specs=[row(d), row(d), pl.BlockSpec((tm * d // (2 * LANE), LANE), lambda i: (i, 0)),
                   slot, slot, slot, pl.BlockSpec((N_EXPERTS, LANE), lambda i: (0, 0))],
        out_shape=[
            jax.ShapeDtypeStruct((n, d), F32),
            jax.ShapeDtypeStruct((n, d), BF16),
            jax.ShapeDtypeStruct((n * d // (2 * LANE), LANE), jnp.uint32),
            jax.ShapeDtypeStruct((TOP_K, n), jnp.int32),
            jax.ShapeDtypeStruct((TOP_K, n), F32),
            jax.ShapeDtypeStruct((TOP_K, n), jnp.int32),
            jax.ShapeDtypeStruct((N_EXPERTS, LANE), jnp.int32),
        ],
        scratch_shapes=[pltpu.VMEM((N_EXPERTS, 1), F32)],
        compiler_params=_params(("arbitrary",)),
    )(*o_list, *l_list, yb, gt, gt, x2, wa, wb, wo, gf, wrh, wrl, bias_col, tri)


def _route_tile(logits, bias_ref, tri_ref, idx_ref, w_ref, rank_ref, cnt_ref, carry_ref):
    E = N_EXPERTS
    tn = logits.shape[1]
    scores = jax.nn.sigmoid(logits)
    sel = scores + bias_ref[...]
    sub = lax.broadcasted_iota(jnp.int32, (GROUP_SIZE, tn), 0).astype(F32)
    gs = []
    for grp in range(N_GROUPS):
        blk = sel[grp * GROUP_SIZE:(grp + 1) * GROUP_SIZE]
        m1 = jnp.max(blk, axis=0, keepdims=True)
        first = jnp.min(jnp.where(blk == m1, sub, float(GROUP_SIZE)), axis=0, keepdims=True)
        m2 = jnp.max(jnp.where(sub == first, -jnp.inf, blk), axis=0, keepdims=True)
        gs.append(m1 + m2)
    keep = []
    for a in range(N_GROUPS):
        rank = jnp.zeros((1, tn), jnp.int32)
        for o in range(N_GROUPS):
            if o == a:
                continue
            beats = (gs[o] >= gs[a]) if o < a else (gs[o] > gs[a])
            rank = rank + beats.astype(jnp.int32)
        keep.append(jnp.broadcast_to(rank < TOP_GROUPS, (GROUP_SIZE, tn)))
    emask = jnp.concatenate(keep, axis=0)
    cand = jnp.where(emask, sel, -jnp.inf)
    eid = lax.broadcasted_iota(jnp.int32, (E, tn), 0)
    eid_f = eid.astype(F32)
    rank = jnp.zeros((E, tn), jnp.int32)
    for o in range(E):
        other = cand[o:o + 1]
        beats = (other > cand) | ((other == cand) & (eid > o))
        rank = rank + beats.astype(jnp.int32)
    chosen = emask & (rank < TOP_K)
    wsel = jnp.where(chosen, scores, 0.0)
    wnorm = wsel / jnp.sum(wsel, axis=0, keepdims=True) * ROUTED_SCALE

    @pl.when(pl.program_id(0) == 0)
    def _():
        carry_ref[...] = jnp.zeros_like(carry_ref)

    chosen_f = jnp.where(chosen, 1.0, 0.0)
    before = carry_ref[...] + _dot(chosen_f.astype(BF16), tri_ref[...]) - chosen_f
    carry_ref[...] = carry_ref[...] + jnp.sum(chosen_f, axis=1, keepdims=True)
    cnt_ref[...] = jnp.broadcast_to(carry_ref[...], cnt_ref.shape).astype(jnp.int32)
    for k in range(TOP_K):
        hit = chosen & (rank == k)
        idx_ref[k:k + 1, :] = jnp.sum(jnp.where(hit, eid_f, 0.0), axis=0, keepdims=True).astype(jnp.int32)
        w_ref[k:k + 1, :] = jnp.sum(jnp.where(hit, wnorm, 0.0), axis=0, keepdims=True)
        rank_ref[k:k + 1, :] = jnp.sum(jnp.where(hit, before, 0.0), axis=0, keepdims=True).astype(jnp.int32)


def _dispatch_sc(pos_t, xr3, p_rows):
    k, n = pos_t.shape
    _, sub, lane = xr3.shape
    workers = SC_CORES * SC_SUBCORES
    windows = n // SC_WINDOW
    per_worker = windows // workers
    assert n % (workers * SC_WINDOW) == 0
    mesh = plsc.VectorSubcoreMesh(core_axis_name="c", subcore_axis_name="s")

    @functools.partial(
        pl.kernel, mesh=mesh,
        out_type=jax.ShapeDtypeStruct((p_rows, sub, lane), xr3.dtype),
        scratch_types=[pltpu.VMEM((k, SC_WINDOW), jnp.int32), pltpu.VMEM((SC_WINDOW, sub, lane), xr3.dtype),
                       pltpu.SemaphoreType.DMA],
    )
    def scatter_rows(x_hbm, idx_hbm, o_hbm, idx_v, rows_v, sem):
        wid = lax.axis_index("s") * SC_CORES + lax.axis_index("c")

        @pl.loop(0, per_worker)
        def _(c):
            win = wid * per_worker + c
            pltpu.sync_copy(idx_hbm.at[win], idx_v)
            pltpu.sync_copy(x_hbm.at[pl.ds(win * SC_WINDOW, SC_WINDOW)], rows_v)
            for slot in range(k):
                pltpu.async_copy(rows_v, o_hbm.at[idx_v.at[slot]], sem).wait()

    return scatter_rows(xr3, pos_t.reshape(k, windows, SC_WINDOW).transpose(1, 0, 2))


def _combine_sc(pos_t, ys3):
    k, n = pos_t.shape
    _, sub, lane = ys3.shape
    nk = k * n
    workers = SC_CORES * SC_SUBCORES
    per_worker = nk // workers
    assert nk % (workers * SC_WINDOW) == 0
    mesh = plsc.VectorSubcoreMesh(core_axis_name="c", subcore_axis_name="s")

    @functools.partial(
        pl.kernel, mesh=mesh,
        out_type=jax.ShapeDtypeStruct((nk, sub, lane), ys3.dtype),
        scratch_types=[pltpu.VMEM((SC_WINDOW,), jnp.int32), pltpu.VMEM((SC_WINDOW, sub, lane), ys3.dtype),
                       pltpu.SemaphoreType.DMA],
    )
    def gather_rows(y_hbm, idx_hbm, o_hbm, idx_v, rows_v, sem):
        wid = lax.axis_index("s") * SC_CORES + lax.axis_index("c")

        @pl.loop(0, per_worker // SC_WINDOW)
        def _(c):
            base = wid * per_worker + c * SC_WINDOW
            pltpu.sync_copy(idx_hbm.at[pl.ds(base, SC_WINDOW)], idx_v)
            pltpu.async_copy(y_hbm.at[idx_v], rows_v, sem).wait()
            pltpu.sync_copy(rows_v, o_hbm.at[pl.ds(base, SC_WINDOW)])

    return gather_rows(ys3, pos_t.reshape(nk))


def _moe_kernel(blk_e_ref, nused_ref, nvalid_ref, xs_ref, wg_ref, wu_ref, wd_ref, o_ref):
    i = pl.program_id(0)

    @pl.when(i < nused_ref[0])
    def _():
        sub = xs_ref.shape[0] // MOE_ROWS
        owned = lax.broadcasted_iota(jnp.int32, (MOE_ROWS, LANE), 0) < nvalid_ref[i]
        words = [jnp.where(owned, xs_ref[pl.ds(c, MOE_ROWS, stride=sub), :], jnp.uint32(0)) for c in range(sub)]
        halves = [_unpack_bf16_pairs(w) for w in words]
        x = jnp.concatenate([lo for lo, _ in halves] + [hi for _, hi in halves], axis=1).astype(BF16)
        hg = _dot(x, wg_ref[0])
        hid = (hg * jax.nn.sigmoid(hg)) * _dot(x, wu_ref[0])
        y = _dot(hid.astype(BF16), wd_ref[0])
        for c in range(sub):
            o_ref[pl.ds(c, MOE_ROWS, stride=sub), :] = _pack_bf16_pairs(
                y[:, c * LANE:(c + 1) * LANE], y[:, (c + sub) * LANE:(c + sub + 1) * LANE])

    @pl.when(i >= nused_ref[0])
    def _():
        o_ref[...] = jnp.zeros_like(o_ref)


def _moe_experts(blk_e, nused, nvalid, xs, wg, wu, wd):
    d = wg.shape[1]
    sub = d // (2 * LANE)
    p = xs.shape[0] // sub
    hid = wg.shape[2]
    grid_spec = pltpu.PrefetchScalarGridSpec(
        num_scalar_prefetch=3,
        grid=(p // MOE_ROWS,),
        in_specs=[
            pl.BlockSpec((MOE_ROWS * sub, LANE), lambda i, be, nu, nv: (jnp.minimum(i, nu[0] - 1), 0)),
            pl.BlockSpec((1, d, hid), lambda i, be, nu, nv: (be[i], 0, 0)),
            pl.BlockSpec((1, d, hid), lambda i, be, nu, nv: (be[i], 0, 0)),
            pl.BlockSpec((1, hid, d), lambda i, be, nu, nv: (be[i], 0, 0)),
        ],
        out_specs=pl.BlockSpec((MOE_ROWS * sub, LANE), lambda i, be, nu, nv: (i, 0)),
    )
    return pl.pallas_call(
        _moe_kernel,
        grid_spec=grid_spec,
        out_shape=jax.ShapeDtypeStruct((p * sub, LANE), jnp.uint32),
        compiler_params=_params(("arbitrary",)),
    )(blk_e, nused, nvalid, xs, wg, wu, wd)


def _side_kernel(xn_ref, p_ref, wsg, wsu, wsd, wp, sh_ref, pe_ref):
    xn = xn_ref[...]
    hg = _dot(xn, wsg[...])
    sh_ref[...] = _dot(((hg * jax.nn.sigmoid(hg)) * _dot(xn, wsu[...])).astype(BF16), wsd[...]).astype(sh_ref.dtype)
    pe_ref[...] = _dot(p_ref[...].astype(BF16), wp[...]).astype(pe_ref.dtype)


def _side(xn2, p2, wsg, wsu, wsd, wp):
    n, d = xn2.shape
    tm = ROW_TILE
    row = lambda w: pl.BlockSpec((tm, w), lambda i: (i, 0))
    full = lambda a: pl.BlockSpec(a.shape, lambda i: (0,) * a.ndim)
    return pl.pallas_call(
        _side_kernel,
        grid=(n // tm,),
        in_specs=[row(d), row(p2.shape[1]), full(wsg), full(wsu), full(wsd), full(wp)],
        out_specs=[row(d), row(d)],
        out_shape=[jax.ShapeDtypeStruct((n, d), BF16)] * 2,
        compiler_params=_params(("parallel",)),
    )(xn2, p2, wsg, wsu, wsd, wp)


def _final_kernel(h_ref, sh_ref, pe_ref, ys_ref, w_ref, gp, wpg, gfin, o_ref):
    w = w_ref[...]
    tm = w.shape[0]
    sub = ys_ref.shape[1] // tm
    parts = [None] * (2 * sub)
    for k in range(TOP_K):
        wk = jnp.broadcast_to(w[:, k:k + 1], (tm, LANE))
        for c in range(sub):
            lo, hi = _unpack_bf16_pairs(ys_ref[k, pl.ds(c, tm, stride=sub), :])
            parts[c] = wk * lo if k == 0 else parts[c] + wk * lo
            parts[c + sub] = wk * hi if k == 0 else parts[c + sub] + wk * hi
    y = jnp.concatenate(parts, axis=1)
    h = h_ref[...] + (y + sh_ref[...].astype(F32))
    gate = jax.nn.sigmoid(_dot(_rms(h, gp[...]).astype(BF16), wpg[...]))
    h = h + pe_ref[...].astype(F32) * gate
    o_ref[...] = _rms(h, gfin[...])


def _final(h1, shared, ple, ysg, w_nk, gp, wpg, gfin, part, out_prev):
    n, d = h1.shape
    tm = ROW_TILE
    steps = ysg.shape[1] * LANE * 2 // d // tm
    first = part * steps
    row = lambda w: pl.BlockSpec((tm, w), lambda i: (first + i, 0))
    full = lambda a: pl.BlockSpec(a.shape, lambda i: (0,) * a.ndim)
    in_specs = [row(d), row(d), row(d), pl.BlockSpec((TOP_K, ysg.shape[1] // steps, LANE), lambda i: (0, i, 0)),
                row(TOP_K), full(gp), full(wpg), full(gfin)]
    args = [h1, shared, ple, ysg, w_nk, gp, wpg, gfin]
    kernel_fn, aliases = _final_kernel, {}
    if out_prev is not None:
        in_specs.append(pl.BlockSpec(memory_space=pl.ANY))
        args.append(out_prev)
        aliases = {len(args) - 1: 0}
        kernel_fn = lambda *refs: _final_kernel(*refs[:8], refs[9])
    return pl.pallas_call(
        kernel_fn,
        grid=(steps,),
        in_specs=in_specs,
        out_specs=row(d),
        out_shape=jax.ShapeDtypeStruct((n, d), F32),
        input_output_aliases=aliases,
        compiler_params=_params(("parallel",)),
    )(*args)


def _pad_heads(w):
    d, c = w.shape
    nh = c // HEAD_DIM
    w = w.reshape(d, nh, HEAD_DIM)
    return jnp.pad(w, ((0, 0), (0, 0), (0, LANE - HEAD_DIM))).reshape(d, nh * LANE)


def _alibi_slopes(n):
    return 2.0 ** (-8.0 * jnp.arange(1, n + 1, dtype=jnp.float32) / n)


def kernel(x, p, g_mix, w_in, cmp_pos_k, cmp_w1_k, cmp_b1_k, cmp_w2_k, cmp_pos_v, cmp_w1_v, cmp_b1_v, cmp_w2_v, w_branch_a, w_branch_b, w_out, g_ffn, w_router, router_bias, w_gate, w_up, w_down, ws_gate, ws_up, ws_down, g_ple, w_ple_gate, w_ple, g_final):
    B, S, D = x.shape
    N = B * S
    G, R = NSA_KV_HEADS, NSA_REP
    assert w_in.shape[0] == 1 and p.shape[0] == 1
    assert S % DIL_STEP_ROWS == 0 and S % NSA_TK == 0 and S >= WIN_SIZE + NSA_Q
    assert N % (ROW_TILE * COMBINE_PARTS) == 0 and N % INPROJ_ROWS == 0 and S % INPROJ_ROWS == 0
    slopes = _alibi_slopes(DIL_HEADS + NSA_HEADS)
    slopes_a, slopes_b = slopes[0::2], slopes[1::2]

    offs = np.concatenate([[0], np.cumsum(IN_SIZES)])
    parts = [w_in[0][:, offs[i]:offs[i + 1]] for i in range(len(IN_SIZES))]
    w_qa, w_ka, w_va, w_qb, w_kc, w_vc, w_ks, w_vs, w_kw, w_vw, w_gn, w_ga, w_gb = parts
    w_qkv_a = jnp.concatenate([w_qa * HEAD_DIM ** -0.5, w_ka, w_va], axis=1).astype(BF16)
    w_nsa = jnp.concatenate([_pad_heads(t) for t in (w_qb, w_kc, w_vc, w_ks, w_vs, w_kw, w_vw)],
                            axis=1).astype(BF16)
    nsa_w = w_nsa.shape[1]
    w_gn = w_gn.reshape(D, 3, G, R).transpose(0, 2, 1, 3).reshape(D, G, 3 * R)
    w_gn = jnp.pad(w_gn, ((0, 0), (0, 0), (0, LANE - 3 * R))).reshape(D, G * LANE)
    w_gate_in = jnp.concatenate([w_ga, w_gb, w_gn], axis=1).astype(BF16)
    gate_w = w_gate_in.shape[1]

    x2 = x.reshape(N, D)
    dils = tuple(dil for _, dil in DIL_PATTERNS if dil > 1)
    qkv_a, *dilated_qkv = _inproj(x2, g_mix, w_qkv_a, BF16, w_qkv_a.shape[1], dils=dils)
    nsa_in, = _inproj(x2, g_mix, w_nsa, BF16, nsa_w)
    gt, = _inproj(x2, g_mix, w_gate_in, F32, gate_w)

    o_list, l_list = [], []
    for _, dil in DIL_PATTERNS:
        src = qkv_a if dil == 1 else dilated_qkv[dils.index(dil)]
        o, l = _dilated(src.reshape(B, S // dil, -1), _dil_bias(slopes_a, dil), dil, B, S)
        o_list.append(o.reshape(-1, N, LANE))
        l_list.append(l.reshape(-1, N, LANE))

    nchunk = S // CMP_STRIDE
    c0 = NSA_HEADS * LANE
    raw = nsa_in[:, c0:c0 + 2 * G * LANE].reshape(B, nchunk, CMP_STRIDE, 2 * G, LANE)[..., :HEAD_DIM]
    xc = raw.transpose(0, 3, 1, 2, 4).reshape(B, 2 * G, nchunk, CMP_STRIDE * HEAD_DIM)
    pos = jnp.stack([cmp_pos_k[0].reshape(1, -1), cmp_pos_v[0].reshape(1, -1)])
    w1 = jnp.stack([cmp_w1_k[0], cmp_w1_v[0]]).astype(BF16)
    b1 = jnp.stack([cmp_b1_k[0][None], cmp_b1_v[0][None]])
    w2 = jnp.pad(jnp.stack([cmp_w2_k[0], cmp_w2_v[0]]), ((0, 0), (0, 0), (0, LANE - HEAD_DIM))).astype(BF16)
    kcv = _compress(xc, pos, w1, b1, w2)
    kcv = kcv.reshape(B, 2, G, nchunk, LANE).transpose(1, 0, 2, 3, 4).reshape(2, B * G, nchunk, LANE)
    vct = kcv[1].transpose(0, 2, 1)
    v0 = (NSA_HEADS + 3 * G) * LANE
    vsw = jnp.stack([nsa_in[:, v0:v0 + G * LANE], nsa_in[:, v0 + 2 * G * LANE:v0 + 3 * G * LANE]])
    vsw = vsw.reshape(2, B, S // LANE, LANE, G, LANE).transpose(1, 4, 0, 2, 5, 3)
    ones_row = (jnp.arange(LANE) == SUM_LANE).astype(BF16)[:, None]
    vt = (vsw + ones_row).reshape(B * G, 2, S // LANE, LANE, LANE)
    yb = _nsa(slopes_b, nsa_in.reshape(B, S, nsa_w), gt.reshape(B, S, gate_w), kcv[0], vct, vt,
              B, S, 0, 2 * D // LANE)
    yb = yb.reshape(N, NSA_HEADS * LANE)

    wb_pad = jnp.pad(w_branch_b[0].reshape(NSA_HEADS, HEAD_DIM, D),
                     ((0, 0), (0, LANE - HEAD_DIM), (0, 0))).reshape(NSA_HEADS * LANE, D)
    wr_t = w_router[0].T
    wr_hi = wr_t.astype(BF16)
    wr_lo = (wr_t - wr_hi.astype(F32)).astype(BF16)
    h1, xn2, xr, idx_t, wsel_t, rank_t, counts = _merge(
        o_list, l_list, yb, gt, x2, w_branch_a[0].astype(BF16), wb_pad.astype(BF16), w_out[0].astype(BF16),
        g_ffn, wr_hi, wr_lo, router_bias[0][:, None])

    shared, ple = _side(xn2, p[0].reshape(N, -1), ws_gate[0].astype(BF16), ws_up[0].astype(BF16),
                        ws_down[0].astype(BF16), w_ple[0].astype(BF16))

    counts = counts[:, 0]
    padded = (counts + MOE_ROWS - 1) // MOE_ROWS * MOE_ROWS
    pend = jnp.cumsum(padded)
    pstart = pend - padded
    hot = idx_t[None] == jnp.arange(N_EXPERTS, dtype=jnp.int32)[:, None, None]
    pos_t = rank_t + jnp.sum(jnp.where(hot, pstart.astype(jnp.int32)[:, None, None], 0), axis=0)
    NK = N * TOP_K
    nblk = (NK + N_EXPERTS * (MOE_ROWS - 1) + MOE_ROWS - 1) // MOE_ROWS
    P = nblk * MOE_ROWS
    blk_e = jnp.minimum(jnp.sum(pend[None, :] <= (jnp.arange(nblk, dtype=jnp.int32) * MOE_ROWS)[:, None], axis=1),
                        N_EXPERTS - 1).astype(jnp.int32)
    nused = (pend[-1] // MOE_ROWS).astype(jnp.int32).reshape(1)
    row0 = jnp.arange(nblk, dtype=jnp.int32) * MOE_ROWS
    nvalid = jnp.clip((pstart + counts)[blk_e] - row0, 0, MOE_ROWS).astype(jnp.int32)
    sub = D // (2 * LANE)
    xs = _dispatch_sc(pos_t, xr.reshape(N, sub, LANE), P).reshape(P * sub, LANE)
    ys = _moe_experts(blk_e, nused, nvalid, xs, w_gate[0].astype(BF16), w_up[0].astype(BF16), w_down[0].astype(BF16))
    n_part = N // COMBINE_PARTS
    ys3, w_nk, wpg, out = ys.reshape(P, sub, LANE), wsel_t.T, w_ple_gate[0].astype(BF16), None
    for part in range(COMBINE_PARTS):
        ysg = _combine_sc(pos_t[:, part * n_part:(part + 1) * n_part], ys3).reshape(TOP_K, n_part * sub, LANE)
        out = _final(h1, shared, ple, ysg, w_nk, g_ple, wpg, g_final[None, :], part, out)
    return out.reshape(B, S, D)
```
